```python
import jax, jax.numpy as jnp
from jax import lax
import numpy as np

D_MODEL = 1024
BATCH = 2
SEQ = 16384
DEPTH = 2

N_MIXERS = 2
N_HEADS = 16
HEAD_DIM = 64
ATTN_WIDTH = N_HEADS * HEAD_DIM
NSA_KV_GROUPS = 4
NSA_Q_PER_GROUP = N_HEADS // NSA_KV_GROUPS
CMP_BLOCK = 32
CMP_STRIDE = 16
CMP_HIDDEN = 2 * HEAD_DIM
SLC_BLOCK = 64
SLC_TOPK = 16
WINDOW = 512
FORCE_SCORE = 1.0e4
NSA_IN_COLS = ATTN_WIDTH + 6 * NSA_KV_GROUPS * HEAD_DIM + 3 * N_HEADS
FOX_IN_COLS = 3 * ATTN_WIDTH + N_HEADS
FOX_F_BIAS_INIT = 3.0
Q_BLOCK = 128
ROPE_THETA = 10000.0
PEER_HEADS = 8
PEER_N_KEYS = 128
PEER_N_EXPERTS = PEER_N_KEYS * PEER_N_KEYS
PEER_QUERY_DIM = 256
PEER_TOPK = 16
PEER_TOKEN_CHUNK = 128
RMS_EPS = 1e-6
NEG_INF = -1e30

kernel_name = 'hybrid_nsa_fox_peer'


def rms_norm(x, g):
    xf = x.astype(jnp.float32)
    y = xf * lax.rsqrt(jnp.mean(xf * xf, axis=-1, keepdims=True) + RMS_EPS)
    return (y * g.astype(jnp.float32)).astype(x.dtype)


def rope(x):
    seq = x.shape[1]
    half = HEAD_DIM // 2
    inv_freq = ROPE_THETA ** (-jnp.arange(half, dtype=jnp.float32) / half)
    ang = jnp.arange(seq, dtype=jnp.float32)[:, None] * inv_freq[None, :]
    cos = jnp.cos(ang)[None, :, None, :]
    sin = jnp.sin(ang)[None, :, None, :]
    x1 = x[..., :half].astype(jnp.float32)
    x2 = x[..., half:].astype(jnp.float32)
    return jnp.concatenate([x1 * cos - x2 * sin, x2 * cos + x1 * sin], axis=-1).astype(x.dtype)


def masked_softmax(s, mask):
    s = jnp.where(mask, s.astype(jnp.float32), NEG_INF)
    m = jnp.max(s, axis=-1, keepdims=True)
    p = jnp.exp(s - m) * mask
    return p / jnp.maximum(jnp.sum(p, axis=-1, keepdims=True), 1e-30)


def compress_blocks(tok, pe, w1, w2):
    seq = tok.shape[1]
    n_cmp = (seq - CMP_BLOCK) // CMP_STRIDE + 1
    idx = np.arange(n_cmp)[:, None] * CMP_STRIDE + np.arange(CMP_BLOCK)[None, :]
    blk = tok[:, idx] + jnp.transpose(pe, (1, 0, 2))[None, None]
    hid = jax.nn.gelu(jnp.einsum('bclgd,gldh->bgch', blk, w1))
    return jnp.einsum('bgch,ghd->bgcd', hid, w2)


def cmp_to_slc_matrix(n_cmp, n_slc):
    i = np.arange(n_cmp)[:, None]
    j = np.arange(n_slc)[None, :]
    lo = np.maximum(i * CMP_STRIDE, j * SLC_BLOCK)
    hi = np.minimum(i * CMP_STRIDE + CMP_BLOCK, (j + 1) * SLC_BLOCK)
    return jnp.asarray(np.maximum(hi - lo, 0).astype(np.float32) / CMP_BLOCK)


def nsa_mixer(x, w_in, pe_k, w1_k, w2_k, pe_v, w1_v, w2_v, w_out):
    B, S, _ = x.shape
    G, R, hd = NSA_KV_GROUPS, NSA_Q_PER_GROUP, HEAD_DIM
    kvd = G * hd
    proj = x @ w_in
    q = rope(proj[..., :ATTN_WIDTH].reshape(B, S, N_HEADS, hd))
    kv = proj[..., ATTN_WIDTH:ATTN_WIDTH + 6 * kvd].reshape(B, S, 6, G, hd)
    gates = jax.nn.sigmoid(proj[..., ATTN_WIDTH + 6 * kvd:].astype(jnp.float32))
    gates = gates.reshape(B, S, 3, G, R).transpose(0, 2, 3, 4, 1).astype(x.dtype)
    k_cmp = compress_blocks(rope(kv[:, :, 0]), pe_k, w1_k, w2_k)
    v_cmp = compress_blocks(kv[:, :, 1], pe_v, w1_v, w2_v)
    k_slc, v_slc = rope(kv[:, :, 2]), kv[:, :, 3]
    k_win, v_win = rope(kv[:, :, 4]), kv[:, :, 5]
    n_cmp = k_cmp.shape[2]
    n_slc = S // SLC_BLOCK
    n_sel = min(SLC_TOPK, n_slc)
    cmp_end = jnp.arange(n_cmp) * CMP_STRIDE + CMP_BLOCK - 1
    slc_map = cmp_to_slc_matrix(n_cmp, n_slc)
    q = q.reshape(B, S, G, R, hd).transpose(0, 2, 3, 1, 4) * (hd ** -0.5)
    k_blk = k_slc.reshape(B, n_slc, SLC_BLOCK, G, hd).transpose(0, 3, 1, 2, 4)
    v_blk = v_slc.reshape(B, n_slc, SLC_BLOCK, G, hd).transpose(0, 3, 1, 2, 4)
    pad = ((0, 0), (0, 0), (WINDOW, 0), (0, 0))
    k_win_p = jnp.pad(k_win.transpose(0, 2, 1, 3), pad)
    v_win_p = jnp.pad(v_win.transpose(0, 2, 1, 3), pad)
    bi = jnp.arange(B)[:, None, None, None]
    gi = jnp.arange(G)[None, :, None, None]
    blk_ids = jnp.arange(n_slc)
    win_off = jnp.arange(WINDOW + Q_BLOCK)

    def one_block(b_idx):
        t0 = b_idx * Q_BLOCK
        qb = lax.dynamic_slice_in_dim(q, t0, Q_BLOCK, axis=3)
        gb = lax.dynamic_slice_in_dim(gates, t0, Q_BLOCK, axis=4)
        t = t0 + jnp.arange(Q_BLOCK)
        s_c = jnp.einsum('bgrqd,bgcd->bgrqc', qb, k_cmp)
        p_c = masked_softmax(s_c, cmp_end[None, :] <= t[:, None])
        o_c = jnp.einsum('bgrqc,bgcd->bgrqd', p_c.astype(v_cmp.dtype), v_cmp)
        imp = jnp.einsum('bgrqc,cj->bgqj', p_c, slc_map)
        cur = t // SLC_BLOCK
        forced = (blk_ids[None, :] == 0) | (blk_ids[None, :] == cur[:, None]) | (blk_ids[None, :] == cur[:, None] - 1)
        causal_blk = blk_ids[None, :] <= cur[:, None]
        imp = jnp.where(forced, FORCE_SCORE, jnp.where(causal_blk, imp, -1.0))
        _, sel = lax.top_k(imp, n_sel)
        ks = k_blk[bi, gi, sel]
        vs = v_blk[bi, gi, sel]
        s_s = jnp.einsum('bgrqd,bgqknd->bgrqkn', qb, ks)
        pos_s = sel[..., None] * SLC_BLOCK + jnp.arange(SLC_BLOCK)
        m_s = (pos_s <= t[None, None, :, None, None]).reshape(B, G, 1, Q_BLOCK, n_sel * SLC_BLOCK)
        p_s = masked_softmax(s_s.reshape(B, G, R, Q_BLOCK, n_sel * SLC_BLOCK), m_s)
        p_s = p_s.reshape(B, G, R, Q_BLOCK, n_sel, SLC_BLOCK)
        o_s = jnp.einsum('bgrqkn,bgqknd->bgrqd', p_s.astype(vs.dtype), vs)
        kw = lax.dynamic_slice_in_dim(k_win_p, t0, WINDOW + Q_BLOCK, axis=2)
        vw = lax.dynamic_slice_in_dim(v_win_p, t0, WINDOW + Q_BLOCK, axis=2)
        pos_w = t0 - WINDOW + win_off
        m_w = (pos_w[None, :] >= 0) & (pos_w[None, :] <= t[:, None]) & (pos_w[None, :] > t[:, None] - WINDOW)
        p_w = masked_softmax(jnp.einsum('bgrqd,bgkd->bgrqk', qb, kw), m_w)
        o_w = jnp.einsum('bgrqk,bgkd->bgrqd', p_w.astype(vw.dtype), vw)
        return gb[:, 0][..., None] * o_c + gb[:, 1][..., None] * o_s + gb[:, 2][..., None] * o_w

    o = lax.map(one_block, jnp.arange(S // Q_BLOCK))
    o = o.transpose(1, 0, 4, 2, 3, 5).reshape(B, S, ATTN_WIDTH)
    return o @ w_out


def fox_mixer(x, w_in, f_bias, w_out):
    B, S, _ = x.shape
    hd = HEAD_DIM
    proj = x @ w_in
    q, k, v = [proj[..., i * ATTN_WIDTH:(i + 1) * ATTN_WIDTH].reshape(B, S, N_HEADS, hd).transpose(0, 2, 1, 3) for i in range(3)]
    q = q * (hd ** -0.5)
    log_f = jax.nn.log_sigmoid(proj[..., 3 * ATTN_WIDTH:].astype(jnp.float32) + f_bias.astype(jnp.float32))
    c = jnp.cumsum(log_f, axis=1).transpose(0, 2, 1)
    pos = jnp.arange(S)

    def one_block(b_idx):
        t0 = b_idx * Q_BLOCK
        qb = lax.dynamic_slice_in_dim(q, t0, Q_BLOCK, axis=2)
        cb = lax.dynamic_slice_in_dim(c, t0, Q_BLOCK, axis=2)
        t = t0 + jnp.arange(Q_BLOCK)
        s = jnp.einsum('bhqd,bhkd->bhqk', qb, k).astype(jnp.float32) + cb[..., None] - c[:, :, None, :]
        p = masked_softmax(s, pos[None, :] <= t[:, None])
        return jnp.einsum('bhqk,bhkd->bhqd', p.astype(v.dtype), v)

    o = lax.map(one_block, jnp.arange(S // Q_BLOCK))
    o = o.transpose(1, 0, 3, 2, 4).reshape(B, S, ATTN_WIDTH)
    return o @ w_out


def peer_ffn(x, w_q, sub_keys, u, v):
    B, S, D = x.shape
    hq = PEER_QUERY_DIM // 2
    q = (x @ w_q).reshape(B, S, PEER_HEADS, 2, hq)
    s_half = jnp.einsum('bshpc,hpkc->bshpk', q, sub_keys).astype(jnp.float32)
    top_s, top_i = lax.top_k(s_half, PEER_TOPK)
    cand_s = top_s[..., 0, :, None] + top_s[..., 1, None, :]
    cand_i = top_i[..., 0, :, None] * PEER_N_KEYS + top_i[..., 1, None, :]
    n_cand = PEER_TOPK * PEER_TOPK
    best_s, best_pos = lax.top_k(cand_s.reshape(B, S, PEER_HEADS, n_cand), PEER_TOPK)
    experts = jnp.take_along_axis(cand_i.reshape(B, S, PEER_HEADS, n_cand), best_pos, axis=-1)
    gate = jax.nn.softmax(best_s, axis=-1)
    n_chunks = (B * S) // PEER_TOKEN_CHUNK
    n_sel = PEER_HEADS * PEER_TOPK
    xs = (x.reshape(n_chunks, PEER_TOKEN_CHUNK, D),
          experts.reshape(n_chunks, PEER_TOKEN_CHUNK, n_sel),
          gate.reshape(n_chunks, PEER_TOKEN_CHUNK, n_sel))

    def one_chunk(args):
        xc, ec, gc = args
        h = jax.nn.gelu(jnp.einsum('ted,td->te', u[ec], xc))
        return jnp.einsum('te,ted->td', gc.astype(h.dtype) * h, v[ec])

    return lax.map(one_chunk, xs).reshape(B, S, D)


def setup_inputs(seed: int = 0) -> dict:
    key = jax.random.key(seed)
    ks = jax.random.split(key, 25)
    f32 = jnp.float32
    G = NSA_KV_GROUPS

    def nrm(k, shape, scale):
        return jax.random.normal(k, shape, f32) * scale

    def gain(k):
        return 1.0 + 0.02 * jax.random.normal(k, (D_MODEL,), f32)

    cmp_w1_shape = (G, CMP_BLOCK, HEAD_DIM, CMP_HIDDEN)
    cmp_w1_scale = (CMP_BLOCK * HEAD_DIM) ** -0.5
    keys_shape = (PEER_HEADS, 2, PEER_N_KEYS, PEER_QUERY_DIM // 2)
    return {
        'x': nrm(ks[0], (BATCH, SEQ, D_MODEL), 1.0),
        'l0_attn_norm': gain(ks[1]),
        'l0_w_in': nrm(ks[2], (D_MODEL, NSA_IN_COLS), D_MODEL ** -0.5),
        'l0_cmp_pe_k': nrm(ks[3], (G, CMP_BLOCK, HEAD_DIM), 0.02),
        'l0_cmp_w1_k': nrm(ks[4], cmp_w1_shape, cmp_w1_scale),
        'l0_cmp_w2_k': nrm(ks[5], (G, CMP_HIDDEN, HEAD_DIM), CMP_HIDDEN ** -0.5),
        'l0_cmp_pe_v': nrm(ks[6], (G, CMP_BLOCK, HEAD_DIM), 0.02),
        'l0_cmp_w1_v': nrm(ks[7], cmp_w1_shape, cmp_w1_scale),
        'l0_cmp_w2_v': nrm(ks[8], (G, CMP_HIDDEN, HEAD_DIM), CMP_HIDDEN ** -0.5),
        'l0_w_out': nrm(ks[9], (ATTN_WIDTH, D_MODEL), ATTN_WIDTH ** -0.5),
        'l0_ffn_norm': gain(ks[10]),
        'l0_peer_wq': nrm(ks[11], (D_MODEL, PEER_HEADS * PEER_QUERY_DIM), D_MODEL ** -0.5),
        'l0_peer_keys': nrm(ks[12], keys_shape, (PEER_QUERY_DIM // 2) ** -0.5),
        'l0_peer_u': nrm(ks[13], (PEER_N_EXPERTS, D_MODEL), D_MODEL ** -0.5),
        'l0_peer_v': nrm(ks[14], (PEER_N_EXPERTS, D_MODEL), D_MODEL ** -0.5),
        'l1_attn_norm': gain(ks[15]),
        'l1_w_in': nrm(ks[16], (D_MODEL, FOX_IN_COLS), D_MODEL ** -0.5),
        'l1_f_bias': FOX_F_BIAS_INIT + nrm(ks[17], (N_HEADS,), 0.1),
        'l1_w_out': nrm(ks[18], (ATTN_WIDTH, D_MODEL), ATTN_WIDTH ** -0.5),
        'l1_ffn_norm': gain(ks[19]),
        'l1_peer_wq': nrm(ks[20], (D_MODEL, PEER_HEADS * PEER_QUERY_DIM), D_MODEL ** -0.5),
        'l1_peer_keys': nrm(ks[21], keys_shape, (PEER_QUERY_DIM // 2) ** -0.5),
        'l1_peer_u': nrm(ks[22], (PEER_N_EXPERTS, D_MODEL), D_MODEL ** -0.5),
        'l1_peer_v': nrm(ks[23], (PEER_N_EXPERTS, D_MODEL), D_MODEL ** -0.5),
        'final_norm': gain(ks[24]),
    }


def reference(x, l0_attn_norm, l0_w_in, l0_cmp_pe_k, l0_cmp_w1_k, l0_cmp_w2_k, l0_cmp_pe_v, l0_cmp_w1_v, l0_cmp_w2_v, l0_w_out,
              l0_ffn_norm, l0_peer_wq, l0_peer_keys, l0_peer_u, l0_peer_v,
              l1_attn_norm, l1_w_in, l1_f_bias, l1_w_out,
              l1_ffn_norm, l1_peer_wq, l1_peer_keys, l1_peer_u, l1_peer_v,
              final_norm):
    mixers = (nsa_mixer, fox_mixer)
    attn_norms = (l0_attn_norm, l1_attn_norm)
    mixer_args = ((l0_w_in, l0_cmp_pe_k, l0_cmp_w1_k, l0_cmp_w2_k, l0_cmp_pe_v, l0_cmp_w1_v, l0_cmp_w2_v, l0_w_out),
                  (l1_w_in, l1_f_bias, l1_w_out))
    ffn_norms = (l0_ffn_norm, l1_ffn_norm)
    peer_args = ((l0_peer_wq, l0_peer_keys, l0_peer_u, l0_peer_v),
                 (l1_peer_wq, l1_peer_keys, l1_peer_u, l1_peer_v))
    h = x
    for i in range(DEPTH):
        h = h + mixers[i % N_MIXERS](rms_norm(h, attn_norms[i]), *mixer_args[i])
        h = h + peer_ffn(rms_norm(h, ffn_norms[i]), *peer_args[i])
    return rms_norm(h, final_norm)
```

```python
import functools

import numpy as np
import jax
import jax.numpy as jnp
from jax import lax
from jax.experimental import pallas as pl
from jax.experimental.pallas import tpu as pltpu

F32 = jnp.float32
BF16 = jnp.bfloat16

D_MODEL = 1024
N_HEADS = 16
HEAD_DIM = 64
ATTN_WIDTH = N_HEADS * HEAD_DIM
NSA_GROUPS = 4
NSA_Q_PER_GROUP = N_HEADS // NSA_GROUPS
CMP_BLOCK = 32
CMP_STRIDE = 16
CMP_HIDDEN = 2 * HEAD_DIM
SLC_BLOCK = 64
SLC_TOPK = 16
WINDOW = 512
FORCE_SCORE = 1.0e4
ROPE_THETA = 10000.0
PEER_HEADS = 8
PEER_N_KEYS = 128
PEER_TOPK = 16
PEER_HALF_DIM = 128
RMS_EPS = 1e-6
NEG_INF = -1e30

LANES = 128
VMEM_LIMIT_BYTES = 56 * 1024 * 1024

_NT = (((1,), (1,)), ((), ()))


def _cparams(sem, vmem=VMEM_LIMIT_BYTES):
    return pltpu.CompilerParams(dimension_semantics=sem, vmem_limit_bytes=vmem)


def _gelu_tanh(x):
    return 0.5 * x * (1.0 + jnp.tanh(0.7978845608028654 * (x + 0.044715 * (x * x * x))))


def _rms_rows(x, g):
    ms = jnp.mean(x * x, axis=-1, keepdims=True)
    return x * lax.rsqrt(ms + RMS_EPS) * g


def _norm_mm_kernel(*refs, act, has_bias, rope, emit_xn):
    it = iter(refs)
    x_ref, g_ref = next(it), next(it)
    wa_ref = next(it)
    wb_ref = next(it) if rope else None
    cos_ref = next(it) if rope else None
    sin_ref = next(it) if rope else None
    b_ref = next(it) if has_bias else None
    o_ref = next(it)
    xo_ref = next(it) if emit_xn else None
    xn_ref = next(it)

    @pl.when(pl.program_id(1) == 0)
    def _():
        xn = _rms_rows(x_ref[...], g_ref[...]).astype(BF16)
        xn_ref[...] = xn
        if emit_xn:
            xo_ref[...] = xn

    xn = xn_ref[...]
    y = jnp.dot(xn, wa_ref[...], preferred_element_type=F32)
    if rope:
        yb = jnp.dot(xn, wb_ref[...], preferred_element_type=F32)
        cos, sin = cos_ref[...], sin_ref[...]
        for s in range(y.shape[1] // LANES):
            sl = slice(s * LANES, (s + 1) * LANES)
            o_ref[:, sl] = (y[:, sl] * cos + yb[:, sl] * sin).astype(o_ref.dtype)
        return
    if has_bias:
        y = y + b_ref[...]
    if act == "sigmoid":
        y = jax.nn.sigmoid(y)
    elif act == "log_sigmoid":
        y = jax.nn.log_sigmoid(y)
    o_ref[...] = y.astype(o_ref.dtype)


def norm_matmul(x, gamma, wa, *, wb=None, cos=None, sin=None, table_of_tile=None, bias=None,
                act=None, out_dtype=BF16, emit_xn=False, tm=512, tn=512):
    T, D = x.shape
    N = wa.shape[1]
    tn = min(tn, N)
    assert T % tm == 0 and N % tn == 0 and tn % LANES == 0
    rope = wb is not None
    in_specs = [pl.BlockSpec((tm, D), lambda i, j: (i, 0)),
                pl.BlockSpec((1, D), lambda i, j: (0, 0)),
                pl.BlockSpec((D, tn), lambda i, j: (0, j))]
    args = [x, gamma.reshape(1, D).astype(F32), wa]
    if rope:
        S = cos.shape[1]
        assert S % tm == 0
        n_pos = S // tm
        tmap = table_of_tile if table_of_tile is not None else (lambda j: 0)
        in_specs += [pl.BlockSpec((D, tn), lambda i, j: (0, j)),
                     pl.BlockSpec((None, tm, LANES), lambda i, j: (tmap(j), i % n_pos, 0)),
                     pl.BlockSpec((None, tm, LANES), lambda i, j: (tmap(j), i % n_pos, 0))]
        args += [wb, cos, sin]
    if bias is not None:
        in_specs.append(pl.BlockSpec((1, tn), lambda i, j: (0, j)))
        args.append(bias.reshape(1, N).astype(F32))
    out_shape = [jax.ShapeDtypeStruct((T, N), out_dtype)]
    out_specs = [pl.BlockSpec((tm, tn), lambda i, j: (i, j))]
    if emit_xn:
        out_shape.append(jax.ShapeDtypeStruct((T, D), BF16))
        out_specs.append(pl.BlockSpec((tm, D), lambda i, j: (i, 0)))
    res = pl.pallas_call(
        functools.partial(_norm_mm_kernel, act=act, has_bias=bias is not None, rope=rope, emit_xn=emit_xn),
        grid=(T // tm, N // tn),
        in_specs=in_specs,
        out_specs=out_specs,
        out_shape=out_shape,
        scratch_shapes=[pltpu.VMEM((tm, D), BF16)],
        compiler_params=_cparams(("parallel", "arbitrary")),
    )(*args)
    return res if emit_xn else res[0]


def _mm_res_kernel(a_ref, w_ref, r_ref, o_ref):
    o_ref[...] = r_ref[...] + jnp.dot(a_ref[...], w_ref[...], preferred_element_type=F32)


def matmul_residual(a, w, res, *, tm=512, tn=512):
    T, K = a.shape
    N = w.shape[1]
    assert T % tm == 0 and N % tn == 0
    return pl.pallas_call(
        _mm_res_kernel,
        grid=(T // tm, N // tn),
        in_specs=[pl.BlockSpec((tm, K), lambda i, j: (i, 0)),
                  pl.BlockSpec((K, tn), lambda i, j: (0, j)),
                  pl.BlockSpec((tm, tn), lambda i, j: (i, j))],
        out_specs=pl.BlockSpec((tm, tn), lambda i, j: (i, j)),
        out_shape=jax.ShapeDtypeStruct((T, N), F32),
        compiler_params=_cparams(("parallel", "arbitrary")),
    )(a, w, res)


def _rmsnorm_kernel(x_ref, g_ref, o_ref):
    o_ref[...] = _rms_rows(x_ref[...], g_ref[...])


def rmsnorm(x, gamma, *, tm=512):
    T, D = x.shape
    return pl.pallas_call(
        _rmsnorm_kernel,
        grid=(T // tm,),
        in_specs=[pl.BlockSpec((tm, D), lambda i: (i, 0)), pl.BlockSpec((1, D), lambda i: (0, 0))],
        out_specs=pl.BlockSpec((tm, D), lambda i: (i, 0)),
        out_shape=jax.ShapeDtypeStruct((T, D), F32),
        compiler_params=_cparams(("parallel",)),
    )(x, gamma.reshape(1, D).astype(F32))


def _peer_cand_tables(tn):
    fidx, vmask = [], []
    for k2 in range(16):
        fidx.append(k2); vmask.append(0.0)
    for k1 in range(1, 8):
        lim = PEER_TOPK // (k1 + 1)
        for k2 in range(8):
            fidx.append(k1 * 16 + k2); vmask.append(0.0 if k2 < lim else -np.inf)
    for k1 in range(8, 16):
        fidx.append(k1 * 16); vmask.append(0.0)
    fidx = np.broadcast_to(np.asarray(fidx, np.int32)[:, None], (80, tn))
    vmask = np.broadcast_to(np.asarray(vmask, np.float32)[:, None], (80, tn))
    return jnp.asarray(fidx), jnp.asarray(vmask)


def _top16_rows(s):
    n, tn = s.shape
    rows = lax.broadcasted_iota(jnp.int32, (n, tn), 0)
    rows16 = lax.broadcasted_iota(jnp.int32, (PEER_TOPK, tn), 0)
    rank = jnp.full((n, tn), float(PEER_TOPK), F32)
    tops = jnp.zeros((PEER_TOPK, tn), F32)
    v = s
    for k in range(PEER_TOPK):
        m = jnp.max(v, axis=0, keepdims=True)
        idx = jnp.min(jnp.where(v == m, rows, n), axis=0, keepdims=True)
        hit = rows == idx
        rank = jnp.where(hit, float(k), rank)
        v = jnp.where(hit, -jnp.inf, v)
        tops = jnp.where(rows16 == k, m, tops)
    return tops, rank


def _peer_select_kernel(q_ref, keys_ref, fidx_ref, vmask_ref, cnt_ref, rank2_ref, e1_ref, e2_ref):
    fidx = fidx_ref[...]
    vmask = vmask_ref[...]
    for h in range(PEER_HEADS):
        tops, ranks, es = [], [], []
        for p in range(2):
            hp = 2 * h + p
            q = q_ref[:, hp * PEER_HALF_DIM:(hp + 1) * PEER_HALF_DIM]
            s = lax.dot_general(keys_ref[hp], q, _NT, preferred_element_type=F32)
            t, r = _top16_rows(s)
            tops.append(t); ranks.append(r)
            es.append(jnp.exp(s - t[0:1, :]))
        ts1, ts2 = tops
        pieces = [ts1[0:1, :] + ts2]
        for k1 in range(1, 8):
            pieces.append(ts1[k1:k1 + 1, :] + ts2[0:8, :])
        pieces.append(ts1[8:16, :] + ts2[0:1, :])
        cand0 = jnp.concatenate(pieces, axis=0) + vmask
        cand = cand0
        for _ in range(PEER_TOPK):
            m = jnp.max(cand, axis=0, keepdims=True)
            idx = jnp.min(jnp.where(cand == m, fidx, 4096), axis=0, keepdims=True)
            cand = jnp.where(fidx == idx, -jnp.inf, cand)
        taken = jnp.logical_and(cand == -jnp.inf, vmask == 0.0)
        takenf = taken.astype(F32)
        best = ts1[0:1, :] + ts2[0:1, :]
        z = jnp.sum(jnp.where(taken, jnp.exp(cand0 - best), 0.0), axis=0, keepdims=True)
        counts = [jnp.sum(takenf[0:16, :], axis=0, keepdims=True)]
        for k1 in range(1, 8):
            counts.append(jnp.sum(takenf[16 + 8 * (k1 - 1):16 + 8 * k1, :], axis=0, keepdims=True))
        tail = takenf[72:80, :]
        cnt = jnp.zeros_like(ranks[0])
        for k1 in range(PEER_TOPK):
            nk = counts[k1] if k1 < 8 else tail[k1 - 8:k1 - 7, :]
            cnt = jnp.where(ranks[0] == float(k1), nk, cnt)
        sl = slice(h * PEER_N_KEYS, (h + 1) * PEER_N_KEYS)
        cnt_ref[sl, :] = cnt
        rank2_ref[sl, :] = ranks[1]
        e1_ref[sl, :] = es[0]
        e2_ref[sl, :] = es[1] / z


def peer_select(q, keys, *, tn=512):
    T = q.shape[0]
    fidx, vmask = _peer_cand_tables(tn)
    rows = PEER_HEADS * PEER_N_KEYS
    out = jax.ShapeDtypeStruct((rows, T), F32)
    ospec = pl.BlockSpec((rows, tn), lambda i: (0, i))
    return pl.pallas_call(
        _peer_select_kernel,
        grid=(T // tn,),
        in_specs=[pl.BlockSpec((tn, q.shape[1]), lambda i: (i, 0)),
                  pl.BlockSpec(keys.shape, lambda i: (0, 0, 0)),
                  pl.BlockSpec((80, tn), lambda i: (0, 0)),
                  pl.BlockSpec((80, tn), lambda i: (0, 0))],
        out_specs=[ospec] * 4,
        out_shape=[out] * 4,
        compiler_params=_cparams(("parallel",)),
    )(q, keys, fidx, vmask)


def _peer_dense_kernel(xn_ref, u_ref, vt_ref, cnt_ref, rank2_ref, e1_ref, e2_ref, res_ref, o_ref, acc_ref,
                       *, c_per_step):
    j = pl.program_id(1)

    @pl.when(j == 0)
    def _():
        acc_ref[...] = jnp.zeros_like(acc_ref)

    hT = lax.dot_general(u_ref[...], xn_ref[...], _NT, preferred_element_type=F32)
    gT = _gelu_tanh(hT)
    blocks = []
    for cc in range(c_per_step):
        c = j * c_per_step + cc
        w = None
        for h in range(PEER_HEADS):
            row = h * PEER_N_KEYS + c
            n_row = cnt_ref[pl.ds(row, 1), :]
            e1_row = e1_ref[pl.ds(row, 1), :]
            sl = slice(h * PEER_N_KEYS, (h + 1) * PEER_N_KEYS)
            term = jnp.where(rank2_ref[sl, :] < n_row, e2_ref[sl, :], 0.0) * e1_row
            w = term if w is None else w + term
        blocks.append((w * gT[cc * PEER_N_KEYS:(cc + 1) * PEER_N_KEYS, :]).astype(BF16))
    aT = jnp.concatenate(blocks, axis=0)
    acc_ref[...] += jnp.dot(vt_ref[...], aT, preferred_element_type=F32)

    @pl.when(j == pl.num_programs(1) - 1)
    def _():
        o_ref[...] = res_ref[...] + acc_ref[...].T


def peer_dense(xn, u, vt, cnt, rank2, e1, e2, res, *, tn=512, te=1024):
    T, D = xn.shape
    E = u.shape[0]
    rows = PEER_HEADS * PEER_N_KEYS
    sel_spec = pl.BlockSpec((rows, tn), lambda i, j: (0, i))
    return pl.pallas_call(
        functools.partial(_peer_dense_kernel, c_per_step=te // PEER_N_KEYS),
        grid=(T // tn, E // te),
        in_specs=[pl.BlockSpec((tn, D), lambda i, j: (i, 0)),
                  pl.BlockSpec((te, D), lambda i, j: (j, 0)),
                  pl.BlockSpec((D, te), lambda i, j: (0, j)),
                  sel_spec, sel_spec, sel_spec, sel_spec,
                  pl.BlockSpec((tn, D), lambda i, j: (i, 0))],
        out_specs=pl.BlockSpec((tn, D), lambda i, j: (i, 0)),
        out_shape=jax.ShapeDtypeStruct((T, D), F32),
        scratch_shapes=[pltpu.VMEM((D, tn), F32)],
        compiler_params=_cparams(("parallel", "arbitrary")),
    )(xn, u, vt, cnt, rank2, e1, e2, res)


def peer_layer(h, gamma, w_q, sub_keys, u, v):
    q, xn = norm_matmul(h, gamma, w_q.astype(BF16), emit_xn=True)
    keys = sub_keys.reshape(2 * PEER_HEADS, PEER_N_KEYS, PEER_HALF_DIM).astype(BF16)
    cnt, rank2, e1, e2 = peer_select(q, keys)
    return peer_dense(xn, u.astype(BF16), v.T.astype(BF16), cnt, rank2, e1, e2, h)


def _cumsum_aug_kernel(lf_ref, tri_ref, place_q_ref, place_k_ref, ones_q_ref, ones_k_ref,
                       qa_ref, ka_ref, carry_ref):
    @pl.when(pl.program_id(1) == 0)
    def _():
        carry_ref[...] = jnp.zeros_like(carry_ref)

    lf = lf_ref[0]
    c = jnp.dot(tri_ref[...], lf, preferred_element_type=F32, precision=lax.Precision.HIGHEST) + carry_ref[...]
    carry_ref[...] = c[-1:, :]
    hi = c.astype(BF16)
    r1 = c - hi.astype(F32)
    mid = r1.astype(BF16)
    lo = (r1 - mid.astype(F32)).astype(BF16)
    nh = N_HEADS
    lane = lax.broadcasted_iota(jnp.int32, c.shape, 1)
    parts = jnp.where(lane < nh, hi.astype(F32),
                      jnp.where(lane < 2 * nh, pltpu.roll(mid.astype(F32), nh, axis=1),
                                pltpu.roll(lo.astype(F32), 2 * nh, axis=1)))
    parts = jnp.where(lane < 3 * nh, parts, 0.0).astype(BF16)
    qa_ref[0] = (jnp.dot(parts, place_q_ref[...], preferred_element_type=F32) + ones_q_ref[...]).astype(BF16)
    ka_ref[0] = (jnp.dot(parts, place_k_ref[...], preferred_element_type=F32) + ones_k_ref[...]).astype(BF16)


def fox_bias_operands(logf, *, tc=512):
    B, S, _ = logf.shape
    nh = N_HEADS
    tri = jnp.asarray(np.tril(np.ones((tc, tc), np.float32)))
    pq = np.zeros((LANES, nh * LANES), np.float32)
    pk = np.zeros((LANES, nh * LANES), np.float32)
    oq = np.zeros((1, nh * LANES), np.float32)
    ok = np.zeros((1, nh * LANES), np.float32)
    for h in range(nh):
        for part in range(3):
            pq[part * nh + h, h * LANES + part] = 1.0
            pk[part * nh + h, h * LANES + 3 + part] = -1.0
            oq[0, h * LANES + 3 + part] = 1.0
            ok[0, h * LANES + part] = 1.0
    const = lambda a: pl.BlockSpec(a.shape, lambda b, i: (0,) * a.ndim)
    pq, pk, oq, ok = jnp.asarray(pq, BF16), jnp.asarray(pk, BF16), jnp.asarray(oq), jnp.asarray(ok)
    out = jax.ShapeDtypeStruct((B, S, nh * LANES), BF16)
    return pl.pallas_call(
        _cumsum_aug_kernel,
        grid=(B, S // tc),
        in_specs=[pl.BlockSpec((1, tc, LANES), lambda b, i: (b, i, 0)),
                  const(tri), const(pq), const(pk), const(oq), const(ok)],
        out_specs=[pl.BlockSpec((1, tc, nh * LANES), lambda b, i: (b, i, 0))] * 2,
        out_shape=[out, out],
        scratch_shapes=[pltpu.VMEM((1, LANES), F32)],
        compiler_params=_cparams(("parallel", "arbitrary")),
    )(logf, tri, pq, pk, oq, ok)


def _fox_attn_kernel(q_ref, qa_ref, kv_ref, ka_ref, o_ref, *, tq, tk, heads_per_step):
    qi = pl.program_id(2)
    t0 = qi * tq
    n_full = t0 // tk
    n_diag = tq // tk
    outs = []
    for hh in range(heads_per_step):
        lsl = slice(hh * LANES, (hh + 1) * LANES)
        q = jnp.concatenate([q_ref[0, :, lsl], qa_ref[0, :, lsl]], axis=1)

        def step(j, carry, masked):
            m, l, acc = carry
            rows = pl.ds(pl.multiple_of(j * tk, tk), tk)
            kv = kv_ref[0, rows, lsl]
            kk = jnp.concatenate([kv, ka_ref[0, rows, lsl]], axis=1)
            s = lax.dot_general(q, kk, _NT, preferred_element_type=F32)
            if masked:
                qpos = t0 + lax.broadcasted_iota(jnp.int32, (tq, tk), 0)
                kpos = j * tk + lax.broadcasted_iota(jnp.int32, (tq, tk), 1)
                s = jnp.where(kpos <= qpos, s, NEG_INF)
            m_new = jnp.maximum(m, jnp.max(s, axis=1, keepdims=True))
            p = jnp.exp(s - m_new)
            alpha = jnp.exp(m - m_new)
            l = alpha * l + jnp.sum(p, axis=1, keepdims=True)
            acc = alpha * acc + jnp.dot(p.astype(BF16), kv, preferred_element_type=F32)
            return m_new, l, acc

        carry = (jnp.full((tq, 1), NEG_INF, F32), jnp.zeros((tq, 1), F32), jnp.zeros((tq, LANES), F32))
        carry = lax.fori_loop(0, n_full, functools.partial(step, masked=False), carry)
        for d in range(n_diag):
            carry = step(n_full + d, carry, True)
        m, l, acc = carry
        outs.append(acc / l)
    lane = lax.broadcasted_iota(jnp.int32, (tq, LANES), 1)
    blocks = []
    for pair in range(heads_per_step // 2):
        a, b = outs[2 * pair], outs[2 * pair + 1]
        blocks.append(jnp.where(lane < HEAD_DIM, pltpu.roll(a, HEAD_DIM, axis=1), b))
    o_ref[0] = jnp.concatenate(blocks, axis=1).astype(o_ref.dtype) if len(blocks) > 1 else blocks[0].astype(o_ref.dtype)


def _head_slots(w, n_heads, second=None):
    D = w.shape[0]
    a = w.reshape(D, n_heads, HEAD_DIM)
    b = jnp.zeros_like(a) if second is None else second.reshape(D, n_heads, HEAD_DIM)
    return jnp.concatenate([a, b], axis=-1).reshape(D, n_heads * LANES)


def fox_layer(h, gamma, w_in, f_bias, w_out, B, S):
    aw = ATTN_WIDTH
    wq = _head_slots(w_in[:, :aw] * (HEAD_DIM ** -0.5), N_HEADS)
    wkv = _head_slots(w_in[:, aw:2 * aw], N_HEADS, w_in[:, 2 * aw:3 * aw])
    w_main = jnp.concatenate([wq, wkv], axis=1).astype(BF16)
    wf = jnp.pad(w_in[:, 3 * aw:], ((0, 0), (0, LANES - N_HEADS))).astype(BF16)
    bf = jnp.pad(f_bias.astype(F32), (0, LANES - N_HEADS))
    qkv = norm_matmul(h, gamma, w_main)
    logf = norm_matmul(h, gamma, wf, bias=bf, act="log_sigmoid", out_dtype=F32)
    qa, ka = fox_bias_operands(logf.reshape(B, S, LANES))
    qkv = qkv.reshape(B, S, 2 * N_HEADS * LANES)
    o = fox_attention(qkv, qa, ka)
    return matmul_residual(o.reshape(B * S, aw), w_out.astype(BF16), h)


def fox_attention(qkv, qa, ka, *, tq=512, tk=512, heads_per_step=2):
    B, S, _ = qkv.shape
    hs = heads_per_step
    wq = hs * LANES
    n_qblk = N_HEADS // hs
    return pl.pallas_call(
        functools.partial(_fox_attn_kernel, tq=tq, tk=tk, heads_per_step=hs),
        grid=(B, n_qblk, S // tq),
        in_specs=[pl.BlockSpec((1, tq, wq), lambda b, h, i: (b, i, h)),
                  pl.BlockSpec((1, tq, wq), lambda b, h, i: (b, i, h)),
                  pl.BlockSpec((1, S, wq), lambda b, h, i: (b, 0, n_qblk + h)),
                  pl.BlockSpec((1, S, wq), lambda b, h, i: (b, 0, h))],
        out_specs=pl.BlockSpec((1, tq, hs * HEAD_DIM), lambda b, h, i: (b, i, h)),
        out_shape=jax.ShapeDtypeStruct((B, S, ATTN_WIDTH), BF16),
        compiler_params=_cparams(("parallel", "parallel", "arbitrary")),
    )(qkv, qa, qkv, ka)


def _rot_half_cols(w):
    D = w.shape[0]
    a = w.reshape(D, -1, HEAD_DIM)
    half = HEAD_DIM // 2
    return jnp.concatenate([-a[..., half:], a[..., :half]], axis=-1).reshape(w.shape)


def _rope_tables(S):
    half = HEAD_DIM // 2
    inv_freq = ROPE_THETA ** (-jnp.arange(half, dtype=F32) / half)
    ang = jnp.arange(S, dtype=F32)[:, None] * inv_freq[None, :]
    c, s = jnp.cos(ang), jnp.sin(ang)
    c2, s2 = jnp.concatenate([c, c], axis=1), jnp.concatenate([s, s], axis=1)
    cos = jnp.stack([jnp.concatenate([c2, jnp.ones_like(c2)], axis=1), jnp.concatenate([c2, c2], axis=1)])
    sin = jnp.stack([jnp.concatenate([s2, jnp.zeros_like(s2)], axis=1), jnp.concatenate([s2, s2], axis=1)])
    return cos, sin


def _compress_kernel(x_ref, pea_ref, peb_ref, wa_ref, wb_ref, w2_ref, o_ref, pa_ref, pb0_ref, *, n_rows):
    u = pl.program_id(1)
    x = x_ref[0].astype(F32)
    pa = jnp.dot((x + pea_ref[...]).astype(BF16), wa_ref[...], preferred_element_type=F32)
    pb = jnp.dot((x + peb_ref[...]).astype(BF16), wb_ref[...], preferred_element_type=F32)

    def emit(slab, hid):
        y = jnp.dot(_gelu_tanh(hid).astype(BF16), w2_ref[...], preferred_element_type=F32)
        o_ref[0, pl.ds(pl.multiple_of(slab * n_rows, n_rows), n_rows), :] = y

    @pl.when(u == 0)
    def _():
        pb0_ref[...] = pb

    @pl.when(u > 0)
    def _():
        emit(u - 1, pa_ref[...] + pb)

    @pl.when(u == 3)
    def _():
        emit(3, pa + pltpu.roll(pb0_ref[...], n_rows - 1, axis=0))

    pa_ref[...] = pa


def nsa_compress(src, pe, w1, w2):
    B, S, W = src.shape
    G = NSA_GROUPS
    n_rows = S // 64
    half = CMP_BLOCK // 2
    cw = half * W
    xv = src.reshape(B, n_rows, 4 * cw)
    pe_flat = jnp.transpose(pe, (1, 0, 2)).reshape(CMP_BLOCK, W).astype(F32)
    pea, peb = pe_flat[:half].reshape(1, cw), pe_flat[half:].reshape(1, cw)
    eye = jnp.eye(G, dtype=F32)
    wfull = jnp.einsum('gldh,gk->lkdgh', w1.astype(F32), eye).reshape(CMP_BLOCK, W, G * CMP_HIDDEN)
    wa = wfull[:half].reshape(cw, G * CMP_HIDDEN).astype(BF16)
    wb = wfull[half:].reshape(cw, G * CMP_HIDDEN).astype(BF16)
    w2bd = jnp.einsum('ghd,gk->ghkd', w2.astype(F32), eye).reshape(G * CMP_HIDDEN, W).astype(BF16)
    const = lambda a: pl.BlockSpec(a.shape, lambda b, u: (0,) * a.ndim)
    return pl.pallas_call(
        functools.partial(_compress_kernel, n_rows=n_rows),
        grid=(B, 4),
        in_specs=[pl.BlockSpec((1, n_rows, cw), lambda b, u: (b, 0, u)),
                  const(pea), const(peb), const(wa), const(wb), const(w2bd)],
        out_specs=pl.BlockSpec((1, 4 * n_rows, W), lambda b, u: (b, 0, 0)),
        out_shape=jax.ShapeDtypeStruct((B, 4 * n_rows, W), F32),
        scratch_shapes=[pltpu.VMEM((n_rows, G * CMP_HIDDEN), F32), pltpu.VMEM((n_rows, G * CMP_HIDDEN), F32)],
        compiler_params=_cparams(("parallel", "arbitrary")),
    )(xv, pea, peb, wa, wb, w2bd)


def _masked_softmax_rows(s, mask):
    s = jnp.where(mask, s, NEG_INF)
    m = jnp.max(s, axis=1, keepdims=True)
    p = jnp.where(mask, jnp.exp(s - m), 0.0)
    return p / jnp.maximum(jnp.sum(p, axis=1, keepdims=True), 1e-30)


def _nsa_attn_kernel(q_ref, kvs_ref, kvw_ref, kvc_ref, kvct_ref, gate_ref, bmat_ref, o_ref,
                     *, tq, tk, seq):
    R = NSA_Q_PER_GROUP
    M = R * tq
    n_slc = seq // SLC_BLOCK
    n_cmp = 4 * n_slc
    n_sel = min(SLC_TOPK, n_slc)
    blocks_per_chunk = tk // SLC_BLOCK
    assert tq & (tq - 1) == 0 and n_slc & (n_slc - 1) == 0 and tk % tq == 0
    log_slc = n_slc.bit_length() - 1
    qi = pl.program_id(2)
    t0 = qi * tq

    qs = jnp.concatenate([q_ref[0, :, r * LANES:(r + 1) * LANES] for r in range(R)], axis=0)

    kvc = kvc_ref[0, 0]
    sT = lax.dot_general(kvc, qs, _NT, preferred_element_type=F32)
    rowi = lax.broadcasted_iota(jnp.int32, (n_cmp, M), 0)
    coli = lax.broadcasted_iota(jnp.int32, (n_cmp, M), 1)
    cmp_end = (rowi & (n_slc - 1)) * SLC_BLOCK + (rowi >> log_slc) * CMP_STRIDE + (CMP_BLOCK - 1)
    maskc = cmp_end <= t0 + (coli & (tq - 1))
    sm = jnp.where(maskc, sT, NEG_INF)
    mx = jnp.max(sm, axis=0, keepdims=True)
    e = jnp.where(maskc, jnp.exp(sm - mx), 0.0)
    pT = e / jnp.maximum(jnp.sum(e, axis=0, keepdims=True), 1e-30)
    o_c = jnp.dot(kvct_ref[0, 0], pT.astype(BF16), preferred_element_type=F32).T

    psum = pT[:, 0:tq]
    for r in range(1, R):
        psum = psum + pT[:, r * tq:(r + 1) * tq]
    p0, p1, p2, p3 = (psum[u * n_slc:(u + 1) * n_slc, :] for u in range(4))
    jrow = lax.broadcasted_iota(jnp.int32, (n_slc, tq), 0)
    p3_prev = jnp.where(jrow == 0, 0.0, pltpu.roll(p3, 1, axis=0))
    imp = p0 + p1 + p2 + 0.5 * p3 + 0.5 * p3_prev
    cur = (t0 + lax.broadcasted_iota(jnp.int32, (n_slc, tq), 1)) >> (SLC_BLOCK.bit_length() - 1)
    forced = (jrow == 0) | (jrow == cur) | (jrow == cur - 1)
    vals = jnp.where(forced, FORCE_SCORE, jnp.where(jrow <= cur, imp, -1.0))
    sel = jnp.zeros((n_slc, tq), F32)
    for _ in range(n_sel):
        m = jnp.max(vals, axis=0, keepdims=True)
        idx = jnp.min(jnp.where(vals == m, jrow, n_slc), axis=0, keepdims=True)
        hit = jrow == idx
        sel = jnp.where(hit, 1.0, sel)
        vals = jnp.where(hit, -jnp.inf, vals)
    sel_bias = ((sel.T - 1.0) * (-NEG_INF)).astype(BF16)

    qrow = t0 + (lax.broadcasted_iota(jnp.int32, (M, tk), 0) & (tq - 1))
    kcol = lax.broadcasted_iota(jnp.int32, (M, tk), 1)
    n_chunks = seq // tk
    j_last = t0 // tk

    def slc_step(j, carry, masked):
        m, l, acc = carry
        kv = kvs_ref[0, pl.ds(pl.multiple_of(j * tk, tk), tk), :]
        s = lax.dot_general(qs, kv, _NT, preferred_element_type=F32)
        off = pl.multiple_of(blocks_per_chunk * (n_chunks - 1 - j), blocks_per_chunk)
        expand = bmat_ref[pl.ds(off, n_slc), :].astype(BF16)
        bias = jnp.dot(sel_bias, expand, preferred_element_type=F32)
        s = s + jnp.concatenate([bias] * R, axis=0)
        if masked:
            s = jnp.where(j * tk + kcol <= qrow, s, NEG_INF)
        m_new = jnp.maximum(m, jnp.max(s, axis=1, keepdims=True))
        p = jnp.exp(s - m_new)
        alpha = jnp.exp(m - m_new)
        l = alpha * l + jnp.sum(p, axis=1, keepdims=True)
        acc = alpha * acc + jnp.dot(p.astype(BF16), kv, preferred_element_type=F32)
        return m_new, l, acc

    carry = (jnp.full((M, 1), NEG_INF, F32), jnp.zeros((M, 1), F32), jnp.zeros((M, LANES), F32))
    carry = lax.fori_loop(0, j_last, functools.partial(slc_step, masked=False), carry)
    _, l_s, acc_s = slc_step(j_last, carry, True)
    o_s = acc_s / l_s

    wlen = WINDOW + tq
    start = jnp.maximum(t0 - WINDOW, 0)
    kvw = kvw_ref[0, pl.ds(pl.multiple_of(start, tq), wlen), :]
    s_w = lax.dot_general(qs, kvw, _NT, preferred_element_type=F32)
    qpos = t0 + (lax.broadcasted_iota(jnp.int32, (M, wlen), 0) & (tq - 1))
    kpos = start + lax.broadcasted_iota(jnp.int32, (M, wlen), 1)
    p_w = _masked_softmax_rows(s_w, (kpos <= qpos) & (kpos > qpos - WINDOW))
    o_w = jnp.dot(p_w.astype(BF16), kvw, preferred_element_type=F32)

    gates = gate_ref[0]
    lane = lax.broadcasted_iota(jnp.int32, (tq, LANES), 1)
    comb = []
    for r in range(R):
        rs = slice(r * tq, (r + 1) * tq)
        comb.append(gates[:, r:r + 1] * o_c[rs] + gates[:, R + r:R + r + 1] * o_s[rs]
                    + gates[:, 2 * R + r:2 * R + r + 1] * o_w[rs])
    out = [jnp.where(lane < HEAD_DIM, pltpu.roll(comb[2 * i], HEAD_DIM, axis=1), comb[2 * i + 1])
           for i in range(R // 2)]
    o_ref[0] = jnp.concatenate(out, axis=1).astype(o_ref.dtype)


def nsa_attention(qkv, kvc, kvct, gates, *, tq=256, tk=512):
    B, S, _ = qkv.shape
    G, R = NSA_GROUPS, NSA_Q_PER_GROUP
    tk = min(tk, S)
    n_slc = S // SLC_BLOCK
    n_cmp = kvc.shape[2]
    bpc = tk // SLC_BLOCK
    off = bpc * (S // tk - 1)
    i = np.arange(n_slc + off)[:, None]
    n = np.arange(tk)[None, :]
    bmat = jnp.asarray((i - off == n // SLC_BLOCK).astype(np.float32))
    q_blk0, slc_blk0, win_blk0 = 0, N_HEADS, N_HEADS + G
    return pl.pallas_call(
        functools.partial(_nsa_attn_kernel, tq=tq, tk=tk, seq=S),
        grid=(B, G, S // tq),
        in_specs=[pl.BlockSpec((1, tq, R * LANES), lambda b, g, i: (b, i, g)),
                  pl.BlockSpec((1, S, LANES), lambda b, g, i: (b, 0, slc_blk0 + g)),
                  pl.BlockSpec((1, S, LANES), lambda b, g, i: (b, 0, win_blk0 + g)),
                  pl.BlockSpec((1, 1, n_cmp, LANES), lambda b, g, i: (b, g, 0, 0)),
                  pl.BlockSpec((1, 1, LANES, n_cmp), lambda b, g, i: (b, g, 0, 0)),
                  pl.BlockSpec((1, tq, LANES), lambda b, g, i: (b, i, g)),
                  pl.BlockSpec(bmat.shape, lambda b, g, i: (0, 0))],
        out_specs=pl.BlockSpec((1, tq, R * HEAD_DIM), lambda b, g, i: (b, i, g)),
        out_shape=jax.ShapeDtypeStruct((B, S, ATTN_WIDTH), BF16),
        compiler_params=_cparams(("parallel", "parallel", "arbitrary")),
    )(qkv, qkv, qkv, kvc, kvct, gates, bmat)


def nsa_layer(h, gamma, w_in, pe_k, w1_k, w2_k, pe_v, w1_v, w2_v, w_out, B, S):
    G, R, hd, aw = NSA_GROUPS, NSA_Q_PER_GROUP, HEAD_DIM, ATTN_WIDTH
    kvd = G * hd
    sec = lambda i: w_in[:, aw + i * kvd: aw + (i + 1) * kvd]
    wq = w_in[:, :aw] * (hd ** -0.5)
    wa = jnp.concatenate([_head_slots(wq, N_HEADS), _head_slots(sec(2), G, sec(3)),
                          _head_slots(sec(4), G, sec(5))], axis=1).astype(BF16)
    wb = jnp.concatenate([_head_slots(_rot_half_cols(wq), N_HEADS), _head_slots(_rot_half_cols(sec(2)), G),
                          _head_slots(_rot_half_cols(sec(4)), G)], axis=1).astype(BF16)
    cos, sin = _rope_tables(S)
    qkv = norm_matmul(h, gamma, wa, wb=wb, cos=cos, sin=sin)
    kc_src = norm_matmul(h, gamma, sec(0).astype(BF16), wb=_rot_half_cols(sec(0)).astype(BF16),
                         cos=cos, sin=sin, table_of_tile=lambda j: 1)
    vc_src = norm_matmul(h, gamma, sec(1).astype(BF16))
    wg = w_in[:, aw + 6 * kvd:].reshape(-1, 3, G, R)
    wg = jnp.transpose(wg, (0, 2, 1, 3)).reshape(-1, G, 3 * R)
    wg = jnp.pad(wg, ((0, 0), (0, 0), (0, LANES - 3 * R))).reshape(-1, G * LANES).astype(BF16)
    gates = norm_matmul(h, gamma, wg, act="sigmoid", out_dtype=F32)
    kc = nsa_compress(kc_src.reshape(B, S, kvd), pe_k, w1_k, w2_k)
    vc = nsa_compress(vc_src.reshape(B, S, kvd), pe_v, w1_v, w2_v)
    n_cmp = kc.shape[1]
    kvc = jnp.concatenate([kc.reshape(B, n_cmp, G, hd), vc.reshape(B, n_cmp, G, hd)], axis=-1)
    kvc = jnp.transpose(kvc, (0, 2, 1, 3)).astype(BF16)
    kvct = jnp.swapaxes(kvc, 2, 3)
    o = nsa_attention(qkv.reshape(B, S, -1), kvc, kvct, gates.reshape(B, S, G * LANES))
    return matmul_residual(o.reshape(B * S, aw), w_out.astype(BF16), h)


def kernel(x, l0_attn_norm, l0_w_in, l0_cmp_pe_k, l0_cmp_w1_k, l0_cmp_w2_k, l0_cmp_pe_v, l0_cmp_w1_v,
           l0_cmp_w2_v, l0_w_out, l0_ffn_norm, l0_peer_wq, l0_peer_keys, l0_peer_u, l0_peer_v,
           l1_attn_norm, l1_w_in, l1_f_bias, l1_w_out, l1_ffn_norm, l1_peer_wq, l1_peer_keys, l1_peer_u,
           l1_peer_v, final_norm):
    B, S, D = x.shape
    h = x.reshape(B * S, D)
    h = nsa_layer(h, l0_attn_norm, l0_w_in, l0_cmp_pe_k, l0_cmp_w1_k, l0_cmp_w2_k, l0_cmp_pe_v, l0_cmp_w1_v,
                  l0_cmp_w2_v, l0_w_out, B, S)
    h = peer_layer(h, l0_ffn_norm, l0_peer_wq, l0_peer_keys, l0_peer_u, l0_peer_v)
    h = fox_layer(h, l1_attn_norm, l1_w_in, l1_f_bias, l1_w_out, B, S)
    h = peer_layer(h, l1_ffn_norm, l1_peer_wq, l1_peer_keys, l1_peer_u, l1_peer_v)
    return rmsnorm(h, final_norm).reshape(B, S, D)
```

```python
import functools

import numpy as np
import jax
import jax.numpy as jnp
from jax import lax
from jax.experimental import pallas as pl
from jax.experimental.pallas import tpu as pltpu

F32 = jnp.float32
BF16 = jnp.bfloat16

D_MODEL = 1024
N_HEADS = 16
HEAD_DIM = 64
ATTN_WIDTH = N_HEADS * HEAD_DIM
NSA_GROUPS = 4
NSA_Q_PER_GROUP = N_HEADS // NSA_GROUPS
CMP_BLOCK = 32
CMP_STRIDE = 16
CMP_HIDDEN = 2 * HEAD_DIM
SLC_BLOCK = 64
SLC_TOPK = 16
WINDOW = 512
FORCE_SCORE = 1.0e4
ROPE_THETA = 10000.0
PEER_HEADS = 8
PEER_N_KEYS = 128
PEER_TOPK = 16
PEER_HALF_DIM = 128
RMS_EPS = 1e-6
NEG_INF = -1e30
LOG2E = 1.4426950408889634

LANES = 128
VMEM_LIMIT_BYTES = 56 * 1024 * 1024

_NT = (((1,), (1,)), ((), ()))


def _cparams(sem, vmem=VMEM_LIMIT_BYTES):
    return pltpu.CompilerParams(dimension_semantics=sem, vmem_limit_bytes=vmem)


def _gelu_tanh(x):
    return 0.5 * x * (1.0 + jnp.tanh(0.7978845608028654 * (x + 0.044715 * (x * x * x))))


def _rms_rows(x, g):
    ms = jnp.mean(x * x, axis=-1, keepdims=True)
    return x * lax.rsqrt(ms + RMS_EPS) * g


def _norm_mm_kernel(*refs, act, has_bias, rope, emit_xn):
    it = iter(refs)
    x_ref, g_ref = next(it), next(it)
    wa_ref = next(it)
    wb_ref = next(it) if rope else None
    cos_ref = next(it) if rope else None
    sin_ref = next(it) if rope else None
    b_ref = next(it) if has_bias else None
    o_ref = next(it)
    xo_ref = next(it) if emit_xn else None
    xn_ref = next(it)

    @pl.when(pl.program_id(1) == 0)
    def _():
        xn = _rms_rows(x_ref[...], g_ref[...]).astype(BF16)
        xn_ref[...] = xn
        if emit_xn:
            xo_ref[...] = xn

    xn = xn_ref[...]
    y = jnp.dot(xn, wa_ref[...], preferred_element_type=F32)
    if rope:
        yb = jnp.dot(xn, wb_ref[...], preferred_element_type=F32)
        cos, sin = cos_ref[...], sin_ref[...]
        for s in range(y.shape[1] // LANES):
            sl = slice(s * LANES, (s + 1) * LANES)
            o_ref[:, sl] = (y[:, sl] * cos + yb[:, sl] * sin).astype(o_ref.dtype)
        return
    if has_bias:
        y = y + b_ref[...]
    if act == "sigmoid":
        y = jax.nn.sigmoid(y)
    elif act == "log_sigmoid":
        y = jax.nn.log_sigmoid(y)
    o_ref[...] = y.astype(o_ref.dtype)


def norm_matmul(x, gamma, wa, *, wb=None, cos=None, sin=None, table_of_tile=None, bias=None,
                act=None, out_dtype=BF16, emit_xn=False, tm=512, tn=512):
    T, D = x.shape
    N = wa.shape[1]
    tn = min(tn, N)
    assert T % tm == 0 and N % tn == 0 and tn % LANES == 0
    rope = wb is not None
    in_specs = [pl.BlockSpec((tm, D), lambda i, j: (i, 0)),
                pl.BlockSpec((1, D), lambda i, j: (0, 0)),
                pl.BlockSpec((D, tn), lambda i, j: (0, j))]
    args = [x, gamma.reshape(1, D).astype(F32), wa]
    if rope:
        S = cos.shape[1]
        assert S % tm == 0
        n_pos = S // tm
        tmap = table_of_tile if table_of_tile is not None else (lambda j: 0)
        in_specs += [pl.BlockSpec((D, tn), lambda i, j: (0, j)),
                     pl.BlockSpec((None, tm, LANES), lambda i, j: (tmap(j), i % n_pos, 0)),
                     pl.BlockSpec((None, tm, LANES), lambda i, j: (tmap(j), i % n_pos, 0))]
        args += [wb, cos, sin]
    if bias is not None:
        in_specs.append(pl.BlockSpec((1, tn), lambda i, j: (0, j)))
        args.append(bias.reshape(1, N).astype(F32))
    out_shape = [jax.ShapeDtypeStruct((T, N), out_dtype)]
    out_specs = [pl.BlockSpec((tm, tn), lambda i, j: (i, j))]
    if emit_xn:
        out_shape.append(jax.ShapeDtypeStruct((T, D), BF16))
        out_specs.append(pl.BlockSpec((tm, D), lambda i, j: (i, 0)))
    res = pl.pallas_call(
        functools.partial(_norm_mm_kernel, act=act, has_bias=bias is not None, rope=rope, emit_xn=emit_xn),
        grid=(T // tm, N // tn),
        in_specs=in_specs,
        out_specs=out_specs,
        out_shape=out_shape,
        scratch_shapes=[pltpu.VMEM((tm, D), BF16)],
        compiler_params=_cparams(("parallel", "arbitrary")),
    )(*args)
    return res if emit_xn else res[0]


def _mm_res_kernel(a_ref, w_ref, r_ref, o_ref):
    o_ref[...] = r_ref[...] + jnp.dot(a_ref[...], w_ref[...], preferred_element_type=F32)


def matmul_residual(a, w, res, *, tm=512, tn=512):
    T, K = a.shape
    N = w.shape[1]
    assert T % tm == 0 and N % tn == 0
    return pl.pallas_call(
        _mm_res_kernel,
        grid=(T // tm, N // tn),
        in_specs=[pl.BlockSpec((tm, K), lambda i, j: (i, 0)),
                  pl.BlockSpec((K, tn), lambda i, j: (0, j)),
                  pl.BlockSpec((tm, tn), lambda i, j: (i, j))],
        out_specs=pl.BlockSpec((tm, tn), lambda i, j: (i, j)),
        out_shape=jax.ShapeDtypeStruct((T, N), F32),
        compiler_params=_cparams(("parallel", "arbitrary")),
    )(a, w, res)


def _rmsnorm_kernel(x_ref, g_ref, o_ref):
    o_ref[...] = _rms_rows(x_ref[...], g_ref[...])


def rmsnorm(x, gamma, *, tm=512):
    T, D = x.shape
    return pl.pallas_call(
        _rmsnorm_kernel,
        grid=(T // tm,),
        in_specs=[pl.BlockSpec((tm, D), lambda i: (i, 0)), pl.BlockSpec((1, D), lambda i: (0, 0))],
        out_specs=pl.BlockSpec((tm, D), lambda i: (i, 0)),
        out_shape=jax.ShapeDtypeStruct((T, D), F32),
        compiler_params=_cparams(("parallel",)),
    )(x, gamma.reshape(1, D).astype(F32))


def _peer_cand_tables(tn):
    fidx, vmask = [], []
    for k2 in range(16):
        fidx.append(k2); vmask.append(0.0)
    for k1 in range(1, 8):
        lim = PEER_TOPK // (k1 + 1)
        for k2 in range(8):
            fidx.append(k1 * 16 + k2); vmask.append(0.0 if k2 < lim else -np.inf)
    for k1 in range(8, 16):
        fidx.append(k1 * 16); vmask.append(0.0)
    fidx = np.broadcast_to(np.asarray(fidx, np.int32)[:, None], (80, tn))
    vmask = np.broadcast_to(np.asarray(vmask, np.float32)[:, None], (80, tn))
    return jnp.asarray(fidx), jnp.asarray(vmask)


def _top16_rows(s):
    n, tn = s.shape
    rows = lax.broadcasted_iota(jnp.int32, (n, tn), 0)
    rows16 = lax.broadcasted_iota(jnp.int32, (PEER_TOPK, tn), 0)
    rank = jnp.full((n, tn), float(PEER_TOPK), F32)
    tops = jnp.zeros((PEER_TOPK, tn), F32)
    v = s
    for k in range(PEER_TOPK):
        m = jnp.max(v, axis=0, keepdims=True)
        idx = jnp.min(jnp.where(v == m, rows, n), axis=0, keepdims=True)
        hit = rows == idx
        rank = jnp.where(hit, float(k), rank)
        v = jnp.where(hit, -jnp.inf, v)
        tops = jnp.where(rows16 == k, m, tops)
    return tops, rank


def _peer_select_kernel(q_ref, keys_ref, fidx_ref, vmask_ref, cnt_ref, rank2_ref, e1_ref, e2_ref):
    fidx = fidx_ref[...]
    vmask = vmask_ref[...]
    for h in range(PEER_HEADS):
        tops, ranks, es = [], [], []
        for p in range(2):
            hp = 2 * h + p
            q = q_ref[:, hp * PEER_HALF_DIM:(hp + 1) * PEER_HALF_DIM]
            s = lax.dot_general(keys_ref[hp], q, _NT, preferred_element_type=F32)
            t, r = _top16_rows(s)
            tops.append(t); ranks.append(r)
            es.append(jnp.exp(s - t[0:1, :]))
        ts1, ts2 = tops
        pieces = [ts1[0:1, :] + ts2]
        for k1 in range(1, 8):
            pieces.append(ts1[k1:k1 + 1, :] + ts2[0:8, :])
        pieces.append(ts1[8:16, :] + ts2[0:1, :])
        cand0 = jnp.concatenate(pieces, axis=0) + vmask
        cand = cand0
        for _ in range(PEER_TOPK):
            m = jnp.max(cand, axis=0, keepdims=True)
            idx = jnp.min(jnp.where(cand == m, fidx, 4096), axis=0, keepdims=True)
            cand = jnp.where(fidx == idx, -jnp.inf, cand)
        taken = jnp.logical_and(cand == -jnp.inf, vmask == 0.0)
        takenf = taken.astype(F32)
        best = ts1[0:1, :] + ts2[0:1, :]
        z = jnp.sum(jnp.where(taken, jnp.exp(cand0 - best), 0.0), axis=0, keepdims=True)
        counts = [jnp.sum(takenf[0:16, :], axis=0, keepdims=True)]
        for k1 in range(1, 8):
            counts.append(jnp.sum(takenf[16 + 8 * (k1 - 1):16 + 8 * k1, :], axis=0, keepdims=True))
        tail = takenf[72:80, :]
        cnt = jnp.zeros_like(ranks[0])
        for k1 in range(PEER_TOPK):
            nk = counts[k1] if k1 < 8 else tail[k1 - 8:k1 - 7, :]
            cnt = jnp.where(ranks[0] == float(k1), nk, cnt)
        sl = slice(h * PEER_N_KEYS, (h + 1) * PEER_N_KEYS)
        cnt_ref[sl, :] = cnt
        rank2_ref[sl, :] = ranks[1].astype(rank2_ref.dtype)
        e1_ref[sl, :] = es[0]
        e2_ref[sl, :] = (es[1] / z).astype(e2_ref.dtype)


def peer_select(q, keys, *, tn=512):
    T = q.shape[0]
    fidx, vmask = _peer_cand_tables(tn)
    rows = PEER_HEADS * PEER_N_KEYS
    ospec = pl.BlockSpec((rows, tn), lambda i: (0, i))
    return pl.pallas_call(
        _peer_select_kernel,
        grid=(T // tn,),
        in_specs=[pl.BlockSpec((tn, q.shape[1]), lambda i: (i, 0)),
                  pl.BlockSpec(keys.shape, lambda i: (0, 0, 0)),
                  pl.BlockSpec((80, tn), lambda i: (0, 0)),
                  pl.BlockSpec((80, tn), lambda i: (0, 0))],
        out_specs=[ospec] * 4,
        out_shape=[jax.ShapeDtypeStruct((rows, T), dt) for dt in (F32, BF16, F32, BF16)],
        compiler_params=_cparams(("parallel",)),
    )(q, keys, fidx, vmask)


def _peer_dense_kernel(xn_ref, u_ref, vt_ref, cnt_ref, rank2_ref, e1_ref, e2_ref, res_ref, o_ref, acc_ref,
                       *, c_per_step):
    j = pl.program_id(1)

    @pl.when(j == 0)
    def _():
        acc_ref[...] = jnp.zeros_like(acc_ref)

    hT = lax.dot_general(u_ref[...], xn_ref[...], _NT, preferred_element_type=F32)
    gT = _gelu_tanh(hT).astype(BF16)
    blocks = []
    for cc in range(c_per_step):
        c = j * c_per_step + cc
        w = None
        for h in range(PEER_HEADS):
            row = h * PEER_N_KEYS + c
            n_row = cnt_ref[pl.ds(row, 1), :].astype(BF16)
            e1_row = e1_ref[pl.ds(row, 1), :].astype(BF16)
            sl = slice(h * PEER_N_KEYS, (h + 1) * PEER_N_KEYS)
            term = jnp.where(rank2_ref[sl, :] < n_row, e2_ref[sl, :], 0.0) * e1_row
            w = term if w is None else w + term
        blocks.append(w * gT[cc * PEER_N_KEYS:(cc + 1) * PEER_N_KEYS, :])
    aT = jnp.concatenate(blocks, axis=0)
    acc_ref[...] += jnp.dot(vt_ref[...], aT, preferred_element_type=F32)

    @pl.when(j == pl.num_programs(1) - 1)
    def _():
        o_ref[...] = res_ref[...] + acc_ref[...].T


def peer_dense(xn, u, vt, cnt, rank2, e1, e2, res, *, tn=512, te=1024):
    T, D = xn.shape
    E = u.shape[0]
    rows = PEER_HEADS * PEER_N_KEYS
    sel_spec = pl.BlockSpec((rows, tn), lambda i, j: (0, i))
    return pl.pallas_call(
        functools.partial(_peer_dense_kernel, c_per_step=te // PEER_N_KEYS),
        grid=(T // tn, E // te),
        in_specs=[pl.BlockSpec((tn, D), lambda i, j: (i, 0)),
                  pl.BlockSpec((te, D), lambda i, j: (j, 0)),
                  pl.BlockSpec((D, te), lambda i, j: (0, j)),
                  sel_spec, sel_spec, sel_spec, sel_spec,
                  pl.BlockSpec((tn, D), lambda i, j: (i, 0))],
        out_specs=pl.BlockSpec((tn, D), lambda i, j: (i, 0)),
        out_shape=jax.ShapeDtypeStruct((T, D), F32),
        scratch_shapes=[pltpu.VMEM((D, tn), F32)],
        compiler_params=_cparams(("parallel", "arbitrary")),
    )(xn, u, vt, cnt, rank2, e1, e2, res)


def peer_layer(h, gamma, w_q, sub_keys, u, v):
    q, xn = norm_matmul(h, gamma, w_q.astype(BF16), emit_xn=True)
    keys = sub_keys.reshape(2 * PEER_HEADS, PEER_N_KEYS, PEER_HALF_DIM).astype(BF16)
    cnt, rank2, e1, e2 = peer_select(q, keys)
    return peer_dense(xn, u.astype(BF16), v.T.astype(BF16), cnt, rank2, e1, e2, h)


def _cumsum_aug_kernel(lf_ref, tri_ref, place_q_ref, place_k_ref, ones_q_ref, ones_k_ref,
                       qa_ref, ka_ref, carry_ref):
    @pl.when(pl.program_id(1) == 0)
    def _():
        carry_ref[...] = jnp.zeros_like(carry_ref)

    lf = lf_ref[0]
    c = jnp.dot(tri_ref[...], lf, preferred_element_type=F32, precision=lax.Precision.HIGHEST) + carry_ref[...]
    carry_ref[...] = c[-1:, :]
    c = c * LOG2E
    hi = c.astype(BF16)
    r1 = c - hi.astype(F32)
    mid = r1.astype(BF16)
    lo = (r1 - mid.astype(F32)).astype(BF16)
    nh = N_HEADS
    lane = lax.broadcasted_iota(jnp.int32, c.shape, 1)
    parts = jnp.where(lane < nh, hi.astype(F32),
                      jnp.where(lane < 2 * nh, pltpu.roll(mid.astype(F32), nh, axis=1),
                                pltpu.roll(lo.astype(F32), 2 * nh, axis=1)))
    parts = jnp.where(lane < 3 * nh, parts, 0.0).astype(BF16)
    qa_ref[0] = (jnp.dot(parts, place_q_ref[...], preferred_element_type=F32) + ones_q_ref[...]).astype(BF16)
    ka_ref[0] = (jnp.dot(parts, place_k_ref[...], preferred_element_type=F32) + ones_k_ref[...]).astype(BF16)


def fox_bias_operands(logf, *, tc=512):
    B, S, _ = logf.shape
    nh = N_HEADS
    tri = jnp.asarray(np.tril(np.ones((tc, tc), np.float32)))
    pq = np.zeros((LANES, nh * LANES), np.float32)
    pk = np.zeros((LANES, nh * LANES), np.float32)
    oq = np.zeros((1, nh * LANES), np.float32)
    ok = np.zeros((1, nh * LANES), np.float32)
    for h in range(nh):
        for part in range(3):
            pq[part * nh + h, h * LANES + part] = 1.0
            pk[part * nh + h, h * LANES + 3 + part] = -1.0
            oq[0, h * LANES + 3 + part] = 1.0
            ok[0, h * LANES + part] = 1.0
    const = lambda a: pl.BlockSpec(a.shape, lambda b, i: (0,) * a.ndim)
    pq, pk, oq, ok = jnp.asarray(pq, BF16), jnp.asarray(pk, BF16), jnp.asarray(oq), jnp.asarray(ok)
    out = jax.ShapeDtypeStruct((B, S, nh * LANES), BF16)
    return pl.pallas_call(
        _cumsum_aug_kernel,
        grid=(B, S // tc),
        in_specs=[pl.BlockSpec((1, tc, LANES), lambda b, i: (b, i, 0)),
                  const(tri), const(pq), const(pk), const(oq), const(ok)],
        out_specs=[pl.BlockSpec((1, tc, nh * LANES), lambda b, i: (b, i, 0))] * 2,
        out_shape=[out, out],
        scratch_shapes=[pltpu.VMEM((1, LANES), F32)],
        compiler_params=_cparams(("parallel", "arbitrary")),
    )(logf, tri, pq, pk, oq, ok)


def _fox_attn_kernel(q_ref, qa_ref, kv_ref, ka_ref, o_ref, *, tq, tk, heads_per_step):
    qi = pl.program_id(2)
    t0 = qi * tq
    n_full = t0 // tk
    n_diag = tq // tk
    lanes = [slice(hh * LANES, (hh + 1) * LANES) for hh in range(heads_per_step)]
    qs = [jnp.concatenate([q_ref[0, :, lsl], qa_ref[0, :, lsl]], axis=1) for lsl in lanes]

    def step(j, carry, masked):
        rows = pl.ds(pl.multiple_of(j * tk, tk), tk)
        klane = lax.broadcasted_iota(jnp.int32, (tk, LANES), 1)
        new = []
        for lsl, q, (m, acc) in zip(lanes, qs, carry):
            kv = kv_ref[0, rows, lsl]
            kk = jnp.concatenate([kv, ka_ref[0, rows, lsl]], axis=1)
            ones_v = jnp.where(klane < HEAD_DIM, 1.0, kv).astype(BF16)
            s = lax.dot_general(q, kk, _NT, preferred_element_type=F32)
            if masked:
                qpos = t0 + lax.broadcasted_iota(jnp.int32, (tq, tk), 0)
                kpos = j * tk + lax.broadcasted_iota(jnp.int32, (tq, tk), 1)
                s = jnp.where(kpos <= qpos, s, NEG_INF)
            m_new = jnp.maximum(m, jnp.max(s, axis=1, keepdims=True))
            p = jnp.exp2(s - m_new).astype(BF16)
            acc = jnp.exp2(m - m_new) * acc + jnp.dot(p, ones_v, preferred_element_type=F32)
            new.append((m_new, acc))
        return tuple(new)

    init = (jnp.full((tq, 1), NEG_INF, F32), jnp.zeros((tq, LANES), F32))
    carry = lax.fori_loop(0, n_full, functools.partial(step, masked=False), (init,) * heads_per_step)
    for d in range(n_diag):
        carry = step(n_full + d, carry, True)
    outs = [acc / acc[:, 0:1] for (_, acc) in carry]
    lane = lax.broadcasted_iota(jnp.int32, (tq, LANES), 1)
    blocks = []
    for pair in range(heads_per_step // 2):
        a, b = outs[2 * pair], outs[2 * pair + 1]
        blocks.append(jnp.where(lane < HEAD_DIM, pltpu.roll(a, HEAD_DIM, axis=1), b))
    o_ref[0] = jnp.concatenate(blocks, axis=1).astype(o_ref.dtype) if len(blocks) > 1 else blocks[0].astype(o_ref.dtype)


def _head_slots(w, n_heads, second=None):
    D = w.shape[0]
    a = w.reshape(D, n_heads, HEAD_DIM)
    b = jnp.zeros_like(a) if second is None else second.reshape(D, n_heads, HEAD_DIM)
    return jnp.concatenate([a, b], axis=-1).reshape(D, n_heads * LANES)


def fox_layer(h, gamma, w_in, f_bias, w_out, B, S):
    aw = ATTN_WIDTH
    wq = _head_slots(w_in[:, :aw] * (HEAD_DIM ** -0.5 * LOG2E), N_HEADS)
    wkv = _head_slots(w_in[:, aw:2 * aw], N_HEADS, w_in[:, 2 * aw:3 * aw])
    w_main = jnp.concatenate([wq, wkv], axis=1).astype(BF16)
    wf = jnp.pad(w_in[:, 3 * aw:], ((0, 0), (0, LANES - N_HEADS))).astype(BF16)
    bf = jnp.pad(f_bias.astype(F32), (0, LANES - N_HEADS))
    qkv = norm_matmul(h, gamma, w_main)
    logf = norm_matmul(h, gamma, wf, bias=bf, act="log_sigmoid", out_dtype=F32)
    qa, ka = fox_bias_operands(logf.reshape(B, S, LANES))
    qkv = qkv.reshape(B, S, 2 * N_HEADS * LANES)
    o = fox_attention(qkv, qa, ka)
    return matmul_residual(o.reshape(B * S, aw), w_out.astype(BF16), h)


def fox_attention(qkv, qa, ka, *, tq=1024, tk=1024, heads_per_step=2):
    B, S, _ = qkv.shape
    hs = heads_per_step
    wq = hs * LANES
    n_qblk = N_HEADS // hs
    return pl.pallas_call(
        functools.partial(_fox_attn_kernel, tq=tq, tk=tk, heads_per_step=hs),
        grid=(B, n_qblk, S // tq),
        in_specs=[pl.BlockSpec((1, tq, wq), lambda b, h, i: (b, i, h)),
                  pl.BlockSpec((1, tq, wq), lambda b, h, i: (b, i, h)),
                  pl.BlockSpec((1, S, wq), lambda b, h, i: (b, 0, n_qblk + h)),
                  pl.BlockSpec((1, S, wq), lambda b, h, i: (b, 0, h))],
        out_specs=pl.BlockSpec((1, tq, hs * HEAD_DIM), lambda b, h, i: (b, i, h)),
        out_shape=jax.ShapeDtypeStruct((B, S, ATTN_WIDTH), BF16),
        compiler_params=_cparams(("parallel", "parallel", "arbitrary")),
    )(qkv, qa, qkv, ka)


def _rot_half_cols(w):
    D = w.shape[0]
    a = w.reshape(D, -1, HEAD_DIM)
    half = HEAD_DIM // 2
    return jnp.concatenate([-a[..., half:], a[..., :half]], axis=-1).reshape(w.shape)


def _rope_tables(S):
    half = HEAD_DIM // 2
    inv_freq = ROPE_THETA ** (-jnp.arange(half, dtype=F32) / half)
    ang = jnp.arange(S, dtype=F32)[:, None] * inv_freq[None, :]
    c, s = jnp.cos(ang), jnp.sin(ang)
    c2, s2 = jnp.concatenate([c, c], axis=1), jnp.concatenate([s, s], axis=1)
    cos = jnp.stack([jnp.concatenate([c2, jnp.ones_like(c2)], axis=1), jnp.concatenate([c2, c2], axis=1)])
    sin = jnp.stack([jnp.concatenate([s2, jnp.zeros_like(s2)], axis=1), jnp.concatenate([s2, s2], axis=1)])
    return cos, sin


def _compress_kernel(x_ref, pea_ref, peb_ref, wa_ref, wb_ref, w2_ref, o_ref, pa_ref, pb0_ref, *, n_rows):
    u = pl.program_id(1)
    x = x_ref[0].astype(F32)
    pa = jnp.dot((x + pea_ref[...]).astype(BF16), wa_ref[...], preferred_element_type=F32)
    pb = jnp.dot((x + peb_ref[...]).astype(BF16), wb_ref[...], preferred_element_type=F32)

    def emit(slab, hid):
        y = jnp.dot(_gelu_tanh(hid).astype(BF16), w2_ref[...], preferred_element_type=F32)
        o_ref[0, pl.ds(pl.multiple_of(slab * n_rows, n_rows), n_rows), :] = y

    @pl.when(u == 0)
    def _():
        pb0_ref[...] = pb

    @pl.when(u > 0)
    def _():
        emit(u - 1, pa_ref[...] + pb)

    @pl.when(u == 3)
    def _():
        emit(3, pa + pltpu.roll(pb0_ref[...], n_rows - 1, axis=0))

    pa_ref[...] = pa


def nsa_compress(src, pe, w1, w2):
    B, S, W = src.shape
    G = NSA_GROUPS
    n_rows = S // 64
    half = CMP_BLOCK // 2
    cw = half * W
    xv = src.reshape(B, n_rows, 4 * cw)
    pe_flat = jnp.transpose(pe, (1, 0, 2)).reshape(CMP_BLOCK, W).astype(F32)
    pea, peb = pe_flat[:half].reshape(1, cw), pe_flat[half:].reshape(1, cw)
    eye = jnp.eye(G, dtype=F32)
    wfull = jnp.einsum('gldh,gk->lkdgh', w1.astype(F32), eye).reshape(CMP_BLOCK, W, G * CMP_HIDDEN)
    wa = wfull[:half].reshape(cw, G * CMP_HIDDEN).astype(BF16)
    wb = wfull[half:].reshape(cw, G * CMP_HIDDEN).astype(BF16)
    w2bd = jnp.einsum('ghd,gk->ghkd', w2.astype(F32), eye).reshape(G * CMP_HIDDEN, W).astype(BF16)
    const = lambda a: pl.BlockSpec(a.shape, lambda b, u: (0,) * a.ndim)
    return pl.pallas_call(
        functools.partial(_compress_kernel, n_rows=n_rows),
        grid=(B, 4),
        in_specs=[pl.BlockSpec((1, n_rows, cw), lambda b, u: (b, 0, u)),
                  const(pea), const(peb), const(wa), const(wb), const(w2bd)],
        out_specs=pl.BlockSpec((1, 4 * n_rows, W), lambda b, u: (b, 0, 0)),
        out_shape=jax.ShapeDtypeStruct((B, 4 * n_rows, W), F32),
        scratch_shapes=[pltpu.VMEM((n_rows, G * CMP_HIDDEN), F32), pltpu.VMEM((n_rows, G * CMP_HIDDEN), F32)],
        compiler_params=_cparams(("parallel", "arbitrary")),
    )(xv, pea, peb, wa, wb, w2bd)


def _masked_softmax_rows(s, mask):
    s = jnp.where(mask, s, NEG_INF)
    m = jnp.max(s, axis=1, keepdims=True)
    p = jnp.where(mask, jnp.exp2(s - m), 0.0)
    return p / jnp.maximum(jnp.sum(p, axis=1, keepdims=True), 1e-30)


def _nsa_attn_kernel(q_ref, kvs_ref, kvw_ref, kvc_ref, kvct_ref, gate_ref, bmat_ref, pmat_ref, o_ref,
                     *, tq, tk, seq):
    R = NSA_Q_PER_GROUP
    M = R * tq
    n_slc = seq // SLC_BLOCK
    n_cmp = 4 * n_slc
    n_sel = min(SLC_TOPK, n_slc)
    blocks_per_chunk = tk // SLC_BLOCK
    assert tq & (tq - 1) == 0 and n_slc & (n_slc - 1) == 0 and tk % tq == 0
    log_slc = n_slc.bit_length() - 1
    qi = pl.program_id(2)
    t0 = qi * tq

    qs = jnp.concatenate([q_ref[0, :, r * LANES:(r + 1) * LANES] for r in range(R)], axis=0)

    kvc = kvc_ref[0, 0]
    sT = lax.dot_general(kvc, qs, _NT, preferred_element_type=F32)
    rowi = lax.broadcasted_iota(jnp.int32, (n_cmp, M), 0)
    coli = lax.broadcasted_iota(jnp.int32, (n_cmp, M), 1)
    cmp_end = (rowi & (n_slc - 1)) * SLC_BLOCK + (rowi >> log_slc) * CMP_STRIDE + (CMP_BLOCK - 1)
    maskc = cmp_end <= t0 + (coli & (tq - 1))
    sm = jnp.where(maskc, sT, NEG_INF)
    mx = jnp.max(sm, axis=0, keepdims=True)
    e = jnp.where(maskc, jnp.exp2(sm - mx), 0.0)
    pT = e / jnp.maximum(jnp.sum(e, axis=0, keepdims=True), 1e-30)
    o_c = jnp.dot(kvct_ref[0, 0], pT.astype(BF16), preferred_element_type=F32).T

    psum = pT[:, 0:tq]
    for r in range(1, R):
        psum = psum + pT[:, r * tq:(r + 1) * tq]
    p0, p1, p2, p3 = (psum[u * n_slc:(u + 1) * n_slc, :] for u in range(4))
    jrow = lax.broadcasted_iota(jnp.int32, (n_slc, tq), 0)
    p3_prev = jnp.where(jrow == 0, 0.0, pltpu.roll(p3, 1, axis=0))
    imp = p0 + p1 + p2 + 0.5 * p3 + 0.5 * p3_prev
    cur = (t0 + lax.broadcasted_iota(jnp.int32, (n_slc, tq), 1)) >> (SLC_BLOCK.bit_length() - 1)
    forced = (jrow == 0) | (jrow == cur) | (jrow == cur - 1)
    vals = jnp.where(forced, FORCE_SCORE, jnp.where(jrow <= cur, imp, -1.0))
    sel = jnp.zeros((n_slc, tq), F32)
    for _ in range(n_sel):
        m = jnp.max(vals, axis=0, keepdims=True)
        idx = jnp.min(jnp.where(vals == m, jrow, n_slc), axis=0, keepdims=True)
        hit = jrow == idx
        sel = jnp.where(hit, 1.0, sel)
        vals = jnp.where(hit, -jnp.inf, vals)
    sel_bias = ((sel.T - 1.0) * (-NEG_INF)).astype(BF16)

    n_parts = 2
    hp = R // n_parts
    mp = hp * tq
    qparts = [qs[i * mp:(i + 1) * mp] for i in range(n_parts)]
    qrow = t0 + (lax.broadcasted_iota(jnp.int32, (mp, tk), 0) & (tq - 1))
    kcol = lax.broadcasted_iota(jnp.int32, (mp, tk), 1)
    n_chunks = seq // tk
    j_last = t0 // tk

    klane = lax.broadcasted_iota(jnp.int32, (tk, LANES), 1)

    def slc_step(j, carry, masked):
        kv = kvs_ref[0, pl.ds(pl.multiple_of(j * tk, tk), tk), :]
        k_sel = jnp.where(klane < HEAD_DIM, kv, bmat_ref[...]).astype(BF16)
        ones_v = jnp.where(klane < HEAD_DIM, 1.0, kv).astype(BF16)
        off = pl.multiple_of(blocks_per_chunk * (n_chunks - 1 - j), blocks_per_chunk)
        place = pmat_ref[pl.ds(off, n_slc), :].astype(BF16)
        q_bias = jnp.dot(sel_bias, place, preferred_element_type=F32).astype(BF16)
        q_bias = jnp.concatenate([q_bias] * hp, axis=0)
        new = []
        for qp, (m, acc) in zip(qparts, carry):
            s = lax.dot_general(qp + q_bias, k_sel, _NT, preferred_element_type=F32)
            if masked:
                s = jnp.where(j * tk + kcol <= qrow, s, NEG_INF)
            m_new = jnp.maximum(m, jnp.max(s, axis=1, keepdims=True))
            p = jnp.exp2(s - m_new).astype(BF16)
            acc = jnp.exp2(m - m_new) * acc + jnp.dot(p, ones_v, preferred_element_type=F32)
            new.append((m_new, acc))
        return tuple(new)

    init = (jnp.full((mp, 1), NEG_INF, F32), jnp.zeros((mp, LANES), F32))
    carry = lax.fori_loop(0, j_last, functools.partial(slc_step, masked=False), (init,) * n_parts)
    carry = slc_step(j_last, carry, True)
    o_s = jnp.concatenate([acc / acc[:, 0:1] for (_, acc) in carry], axis=0)

    wlen = WINDOW + tq
    start = jnp.maximum(t0 - WINDOW, 0)
    kvw = kvw_ref[0, pl.ds(pl.multiple_of(start, tq), wlen), :]
    s_w = lax.dot_general(qs, kvw, _NT, preferred_element_type=F32)
    qpos = t0 + (lax.broadcasted_iota(jnp.int32, (M, wlen), 0) & (tq - 1))
    kpos = start + lax.broadcasted_iota(jnp.int32, (M, wlen), 1)
    p_w = _masked_softmax_rows(s_w, (kpos <= qpos) & (kpos > qpos - WINDOW))
    o_w = jnp.dot(p_w.astype(BF16), kvw, preferred_element_type=F32)

    gates = gate_ref[0]
    lane = lax.broadcasted_iota(jnp.int32, (tq, LANES), 1)
    comb = []
    for r in range(R):
        rs = slice(r * tq, (r + 1) * tq)
        comb.append(gates[:, r:r + 1] * o_c[rs] + gates[:, R + r:R + r + 1] * o_s[rs]
                    + gates[:, 2 * R + r:2 * R + r + 1] * o_w[rs])
    out = [jnp.where(lane < HEAD_DIM, pltpu.roll(comb[2 * i], HEAD_DIM, axis=1), comb[2 * i + 1])
           for i in range(R // 2)]
    o_ref[0] = jnp.concatenate(out, axis=1).astype(o_ref.dtype)


def nsa_attention(qkv, kvc, kvct, gates, *, tq=256, tk=1024):
    B, S, _ = qkv.shape
    G, R = NSA_GROUPS, NSA_Q_PER_GROUP
    tk = min(tk, S)
    n_slc = S // SLC_BLOCK
    n_cmp = kvc.shape[2]
    bpc = tk // SLC_BLOCK
    assert bpc <= LANES - HEAD_DIM
    off = bpc * (S // tk - 1)
    lane = np.arange(LANES)[None, :]
    bmat = jnp.asarray(lane - HEAD_DIM == np.arange(tk)[:, None] // SLC_BLOCK, BF16)
    pmat = jnp.asarray((np.arange(n_slc + off)[:, None] - off == lane - HEAD_DIM) & (lane >= HEAD_DIM)
                       & (lane < HEAD_DIM + bpc), F32)
    slc_blk0, win_blk0 = N_HEADS, N_HEADS + G
    return pl.pallas_call(
        functools.partial(_nsa_attn_kernel, tq=tq, tk=tk, seq=S),
        grid=(B, G, S // tq),
        in_specs=[pl.BlockSpec((1, tq, R * LANES), lambda b, g, i: (b, i, g)),
                  pl.BlockSpec((1, S, LANES), lambda b, g, i: (b, 0, slc_blk0 + g)),
                  pl.BlockSpec((1, S, LANES), lambda b, g, i: (b, 0, win_blk0 + g)),
                  pl.BlockSpec((1, 1, n_cmp, LANES), lambda b, g, i: (b, g, 0, 0)),
                  pl.BlockSpec((1, 1, LANES, n_cmp), lambda b, g, i: (b, g, 0, 0)),
                  pl.BlockSpec((1, tq, LANES), lambda b, g, i: (b, i, g)),
                  pl.BlockSpec(bmat.shape, lambda b, g, i: (0, 0)),
                  pl.BlockSpec(pmat.shape, lambda b, g, i: (0, 0))],
        out_specs=pl.BlockSpec((1, tq, R * HEAD_DIM), lambda b, g, i: (b, i, g)),
        out_shape=jax.ShapeDtypeStruct((B, S, ATTN_WIDTH), BF16),
        compiler_params=_cparams(("parallel", "parallel", "arbitrary")),
    )(qkv, qkv, qkv, kvc, kvct, gates, bmat, pmat)


def nsa_layer(h, gamma, w_in, pe_k, w1_k, w2_k, pe_v, w1_v, w2_v, w_out, B, S):
    G, R, hd, aw = NSA_GROUPS, NSA_Q_PER_GROUP, HEAD_DIM, ATTN_WIDTH
    kvd = G * hd
    sec = lambda i: w_in[:, aw + i * kvd: aw + (i + 1) * kvd]
    wq = w_in[:, :aw] * (hd ** -0.5 * LOG2E)
    wa = jnp.concatenate([_head_slots(wq, N_HEADS), _head_slots(sec(2), G, sec(3)),
                          _head_slots(sec(4), G, sec(5))], axis=1).astype(BF16)
    wb = jnp.concatenate([_head_slots(_rot_half_cols(wq), N_HEADS), _head_slots(_rot_half_cols(sec(2)), G),
                          _head_slots(_rot_half_cols(sec(4)), G)], axis=1).astype(BF16)
    cos, sin = _rope_tables(S)
    qkv = norm_matmul(h, gamma, wa, wb=wb, cos=cos, sin=sin)
    kc_src = norm_matmul(h, gamma, sec(0).astype(BF16), wb=_rot_half_cols(sec(0)).astype(BF16),
                         cos=cos, sin=sin, table_of_tile=lambda j: 1)
    vc_src = norm_matmul(h, gamma, sec(1).astype(BF16))
    wg = w_in[:, aw + 6 * kvd:].reshape(-1, 3, G, R)
    wg = jnp.transpose(wg, (0, 2, 1, 3)).reshape(-1, G, 3 * R)
    wg = jnp.pad(wg, ((0, 0), (0, 0), (0, LANES - 3 * R))).reshape(-1, G * LANES).astype(BF16)
    gates = norm_matmul(h, gamma, wg, act="sigmoid", out_dtype=F32)
    kc = nsa_compress(kc_src.reshape(B, S, kvd), pe_k, w1_k, w2_k)
    vc = nsa_compress(vc_src.reshape(B, S, kvd), pe_v, w1_v, w2_v)
    n_cmp = kc.shape[1]
    kvc = jnp.concatenate([kc.reshape(B, n_cmp, G, hd), vc.reshape(B, n_cmp, G, hd)], axis=-1)
    kvc = jnp.transpose(kvc, (0, 2, 1, 3)).astype(BF16)
    kvct = jnp.swapaxes(kvc, 2, 3)
    o = nsa_attention(qkv.reshape(B, S, -1), kvc, kvct, gates.reshape(B, S, G * LANES))
    return matmul_residual(o.reshape(B * S, aw), w_out.astype(BF16), h)


def kernel(x, l0_attn_norm, l0_w_in, l0_cmp_pe_k, l0_cmp_w1_k, l0_cmp_w2_k, l0_cmp_pe_v, l0_cmp_w1_v,
           l0_cmp_w2_v, l0_w_out, l0_ffn_norm, l0_peer_wq, l0_peer_keys, l0_peer_u, l0_peer_v,
           l1_attn_norm, l1_w_in, l1_f_bias, l1_w_out, l1_ffn_norm, l1_peer_wq, l1_peer_keys, l1_peer_u,
           l1_peer_v, final_norm):
    B, S, D = x.shape
    h = x.reshape(B * S, D)
    h = nsa_layer(h, l0_attn_norm, l0_w_in, l0_cmp_pe_k, l0_cmp_w1_k, l0_cmp_w2_k, l0_cmp_pe_v, l0_cmp_w1_v,
                  l0_cmp_w2_v, l0_w_out, B, S)
    h = peer_layer(h, l0_ffn_norm, l0_peer_wq, l0_peer_keys, l0_peer_u, l0_peer_v)
    h = fox_layer(h, l1_attn_norm, l1_w_in, l1_f_bias, l1_w_out, B, S)
    h = peer_layer(h, l1_ffn_norm, l1_peer_wq, l1_peer_keys, l1_peer_u, l1_peer_v)
    return rmsnorm(h, final_norm).reshape(B, S, D)
```

```python
import functools

import numpy as np
import jax
import jax.numpy as jnp
from jax import lax
from jax.experimental import pallas as pl
from jax.experimental.pallas import tpu as pltpu

F32 = jnp.float32
BF16 = jnp.bfloat16

D_MODEL = 1024
N_HEADS = 16
HEAD_DIM = 64
ATTN_WIDTH = N_HEADS * HEAD_DIM
NSA_GROUPS = 4
NSA_Q_PER_GROUP = N_HEADS // NSA_GROUPS
CMP_BLOCK = 32
CMP_STRIDE = 16
CMP_HIDDEN = 2 * HEAD_DIM
SLC_BLOCK = 64
SLC_TOPK = 16
WINDOW = 512
FORCE_SCORE = 1.0e4
ROPE_THETA = 10000.0
PEER_HEADS = 8
PEER_N_KEYS = 128
PEER_TOPK = 16
PEER_HALF_DIM = 128
RMS_EPS = 1e-6
NEG_INF = -1e30
LOG2E = 1.4426950408889634

LANES = 128
VMEM_LIMIT_BYTES = 56 * 1024 * 1024

_NT = (((1,), (1,)), ((), ()))


def _cparams(sem, vmem=VMEM_LIMIT_BYTES):
    return pltpu.CompilerParams(dimension_semantics=sem, vmem_limit_bytes=vmem)


def _gelu_tanh(x):
    return 0.5 * x * (1.0 + jnp.tanh(0.7978845608028654 * (x + 0.044715 * (x * x * x))))


def _gelu_sigmoid(x):
    c = -2.0 * 0.7978845608028654 * LOG2E
    t = x * (c + (c * 0.044715) * (x * x))
    return x / (1.0 + jnp.exp2(t))


def _rms_rows(x, g):
    ms = jnp.mean(x * x, axis=-1, keepdims=True)
    return x * lax.rsqrt(ms + RMS_EPS) * g


def _norm_mm_kernel(*refs, act, has_bias, rope, emit_xn):
    it = iter(refs)
    x_ref, g_ref = next(it), next(it)
    wa_ref = next(it)
    wb_ref = next(it) if rope else None
    cos_ref = next(it) if rope else None
    sin_ref = next(it) if rope else None
    b_ref = next(it) if has_bias else None
    o_ref = next(it)
    xo_ref = next(it) if emit_xn else None
    xn_ref = next(it)

    @pl.when(pl.program_id(1) == 0)
    def _():
        xn = _rms_rows(x_ref[...], g_ref[...])
        xn_ref[...] = xn.astype(BF16)
        if emit_xn:
            xo_ref[...] = xn.T.astype(BF16)

    xn = xn_ref[...]
    y = jnp.dot(xn, wa_ref[...], preferred_element_type=F32)
    if rope:
        yb = jnp.dot(xn, wb_ref[...], preferred_element_type=F32)
        cos, sin = cos_ref[...], sin_ref[...]
        for s in range(y.shape[1] // LANES):
            sl = slice(s * LANES, (s + 1) * LANES)
            o_ref[:, sl] = (y[:, sl] * cos + yb[:, sl] * sin).astype(o_ref.dtype)
        return
    if has_bias:
        y = y + b_ref[...]
    if act == "sigmoid":
        y = jax.nn.sigmoid(y)
    elif act == "log_sigmoid":
        y = jax.nn.log_sigmoid(y)
    o_ref[...] = y.astype(o_ref.dtype)


def norm_matmul(x, gamma, wa, *, wb=None, cos=None, sin=None, table_of_tile=None, bias=None,
                act=None, out_dtype=BF16, emit_xn=False, tm=512, tn=512):
    T, D = x.shape
    N = wa.shape[1]
    tn = min(tn, N)
    assert T % tm == 0 and N % tn == 0 and tn % LANES == 0
    rope = wb is not None
    in_specs = [pl.BlockSpec((tm, D), lambda i, j: (i, 0)),
                pl.BlockSpec((1, D), lambda i, j: (0, 0)),
                pl.BlockSpec((D, tn), lambda i, j: (0, j))]
    args = [x, gamma.reshape(1, D).astype(F32), wa]
    if rope:
        S = cos.shape[1]
        assert S % tm == 0
        n_pos = S // tm
        tmap = table_of_tile if table_of_tile is not None else (lambda j: 0)
        in_specs += [pl.BlockSpec((D, tn), lambda i, j: (0, j)),
                     pl.BlockSpec((None, tm, LANES), lambda i, j: (tmap(j), i % n_pos, 0)),
                     pl.BlockSpec((None, tm, LANES), lambda i, j: (tmap(j), i % n_pos, 0))]
        args += [wb, cos, sin]
    if bias is not None:
        in_specs.append(pl.BlockSpec((1, tn), lambda i, j: (0, j)))
        args.append(bias.reshape(1, N).astype(F32))
    out_shape = [jax.ShapeDtypeStruct((T, N), out_dtype)]
    out_specs = [pl.BlockSpec((tm, tn), lambda i, j: (i, j))]
    if emit_xn:
        out_shape.append(jax.ShapeDtypeStruct((D, T), BF16))
        out_specs.append(pl.BlockSpec((D, tm), lambda i, j: (0, i)))
    res = pl.pallas_call(
        functools.partial(_norm_mm_kernel, act=act, has_bias=bias is not None, rope=rope, emit_xn=emit_xn),
        grid=(T // tm, N // tn),
        in_specs=in_specs,
        out_specs=out_specs,
        out_shape=out_shape,
        scratch_shapes=[pltpu.VMEM((tm, D), BF16)],
        compiler_params=_cparams(("parallel", "arbitrary")),
    )(*args)
    return res if emit_xn else res[0]


def _mm_res_kernel(a_ref, w_ref, r_ref, o_ref):
    o_ref[...] = r_ref[...] + jnp.dot(a_ref[...], w_ref[...], preferred_element_type=F32)


def matmul_residual(a, w, res, *, tm=512, tn=512):
    T, K = a.shape
    N = w.shape[1]
    assert T % tm == 0 and N % tn == 0
    return pl.pallas_call(
        _mm_res_kernel,
        grid=(T // tm, N // tn),
        in_specs=[pl.BlockSpec((tm, K), lambda i, j: (i, 0)),
                  pl.BlockSpec((K, tn), lambda i, j: (0, j)),
                  pl.BlockSpec((tm, tn), lambda i, j: (i, j))],
        out_specs=pl.BlockSpec((tm, tn), lambda i, j: (i, j)),
        out_shape=jax.ShapeDtypeStruct((T, N), F32),
        compiler_params=_cparams(("parallel", "arbitrary")),
    )(a, w, res)


def _rmsnorm_kernel(x_ref, g_ref, o_ref):
    o_ref[...] = _rms_rows(x_ref[...], g_ref[...])


def rmsnorm(x, gamma, *, tm=512):
    T, D = x.shape
    return pl.pallas_call(
        _rmsnorm_kernel,
        grid=(T // tm,),
        in_specs=[pl.BlockSpec((tm, D), lambda i: (i, 0)), pl.BlockSpec((1, D), lambda i: (0, 0))],
        out_specs=pl.BlockSpec((tm, D), lambda i: (i, 0)),
        out_shape=jax.ShapeDtypeStruct((T, D), F32),
        compiler_params=_cparams(("parallel",)),
    )(x, gamma.reshape(1, D).astype(F32))


def _peer_cand_tables(tn):
    fidx, vmask = [], []
    for k2 in range(16):
        fidx.append(k2); vmask.append(0.0)
    for k1 in range(1, 8):
        lim = PEER_TOPK // (k1 + 1)
        for k2 in range(8):
            fidx.append(k1 * 16 + k2); vmask.append(0.0 if k2 < lim else -np.inf)
    for k1 in range(8, 16):
        fidx.append(k1 * 16); vmask.append(0.0)
    fidx = np.broadcast_to(np.asarray(fidx, np.int32)[:, None], (80, tn))
    vmask = np.broadcast_to(np.asarray(vmask, np.float32)[:, None], (80, tn))
    return jnp.asarray(fidx), jnp.asarray(vmask)


def _top16_rows(s):
    n, tn = s.shape
    rows = lax.broadcasted_iota(jnp.int32, (n, tn), 0)
    rows16 = lax.broadcasted_iota(jnp.int32, (PEER_TOPK, tn), 0)
    rank = jnp.full((n, tn), float(PEER_TOPK), F32)
    tops = jnp.zeros((PEER_TOPK, tn), F32)
    v = s
    for k in range(PEER_TOPK):
        m = jnp.max(v, axis=0, keepdims=True)
        idx = jnp.min(jnp.where(v == m, rows, n), axis=0, keepdims=True)
        hit = rows == idx
        rank = jnp.where(hit, float(k), rank)
        v = jnp.where(hit, -jnp.inf, v)
        tops = jnp.where(rows16 == k, m, tops)
    return tops, rank


def _peer_select_kernel(q_ref, keys_ref, fidx_ref, vmask_ref, cnt_ref, rank2_ref, e1_ref, e2_ref):
    fidx = fidx_ref[...]
    vmask = vmask_ref[...]
    for h in range(1):
        tops, ranks, es = [], [], []
        for p in range(2):
            q = q_ref[:, p * PEER_HALF_DIM:(p + 1) * PEER_HALF_DIM]
            s = lax.dot_general(keys_ref[p], q, _NT, preferred_element_type=F32)
            t, r = _top16_rows(s)
            tops.append(t); ranks.append(r)
            es.append(jnp.exp(s - t[0:1, :]))
        ts1, ts2 = tops
        pieces = [ts1[0:1, :] + ts2]
        for k1 in range(1, 8):
            pieces.append(ts1[k1:k1 + 1, :] + ts2[0:8, :])
        pieces.append(ts1[8:16, :] + ts2[0:1, :])
        cand0 = jnp.concatenate(pieces, axis=0) + vmask
        cand = cand0
        for _ in range(PEER_TOPK):
            m = jnp.max(cand, axis=0, keepdims=True)
            idx = jnp.min(jnp.where(cand == m, fidx, 4096), axis=0, keepdims=True)
            cand = jnp.where(fidx == idx, -jnp.inf, cand)
        taken = jnp.logical_and(cand == -jnp.inf, vmask == 0.0)
        takenf = taken.astype(F32)
        best = ts1[0:1, :] + ts2[0:1, :]
        z = jnp.sum(jnp.where(taken, jnp.exp(cand0 - best), 0.0), axis=0, keepdims=True)
        counts = [jnp.sum(takenf[0:16, :], axis=0, keepdims=True)]
        for k1 in range(1, 8):
            counts.append(jnp.sum(takenf[16 + 8 * (k1 - 1):16 + 8 * k1, :], axis=0, keepdims=True))
        tail = takenf[72:80, :]
        cnt = jnp.zeros_like(ranks[0])
        for k1 in range(PEER_TOPK):
            nk = counts[k1] if k1 < 8 else tail[k1 - 8:k1 - 7, :]
            cnt = jnp.where(ranks[0] == float(k1), nk, cnt)
        sl = slice(h * PEER_N_KEYS, (h + 1) * PEER_N_KEYS)
        cnt_ref[sl, :] = cnt
        rank2_ref[sl, :] = ranks[1].astype(rank2_ref.dtype)
        e1_ref[sl, :] = es[0]
        e2_ref[sl, :] = (es[1] / z).astype(e2_ref.dtype)


def peer_select(q, keys, *, tn=512):
    T = q.shape[0]
    fidx, vmask = _peer_cand_tables(tn)
    rows = PEER_HEADS * PEER_N_KEYS
    ospec = pl.BlockSpec((PEER_N_KEYS, tn), lambda i, h: (h, i))
    return pl.pallas_call(
        _peer_select_kernel,
        grid=(T // tn, PEER_HEADS),
        in_specs=[pl.BlockSpec((tn, 2 * PEER_HALF_DIM), lambda i, h: (i, h)),
                  pl.BlockSpec((2, PEER_N_KEYS, PEER_HALF_DIM), lambda i, h: (h, 0, 0)),
                  pl.BlockSpec((80, tn), lambda i, h: (0, 0)),
                  pl.BlockSpec((80, tn), lambda i, h: (0, 0))],
        out_specs=[ospec] * 4,
        out_shape=[jax.ShapeDtypeStruct((rows, T), dt) for dt in (F32, BF16, F32, BF16)],
        compiler_params=_cparams(("parallel", "parallel")),
    )(q, keys, fidx, vmask)


def _peer_dense_kernel(xn_ref, u_ref, vt_ref, cnt_ref, rank2_ref, e1_ref, e2_ref, res_ref, o_ref,
                       acc_ref, g0_ref, g1_ref, *, c_per_step):
    j = pl.program_id(1)
    n_tiles = pl.num_programs(1) - 1

    @pl.when(j == 0)
    def _():
        acc_ref[...] = jnp.zeros_like(acc_ref)
        g1_ref[...] = jnp.zeros_like(g1_ref)

    @pl.when(j % 2 == 0)
    def _():
        _peer_dense_step(xn_ref, u_ref, vt_ref, cnt_ref, rank2_ref, e1_ref, e2_ref, acc_ref,
                         g1_ref, g0_ref, j, c_per_step)

    @pl.when(j % 2 == 1)
    def _():
        _peer_dense_step(xn_ref, u_ref, vt_ref, cnt_ref, rank2_ref, e1_ref, e2_ref, acc_ref,
                         g0_ref, g1_ref, j, c_per_step)

    @pl.when(j == n_tiles)
    def _():
        o_ref[...] = res_ref[...] + acc_ref[...].T


def _peer_dense_step(xn_ref, u_ref, vt_ref, cnt_ref, rank2_ref, e1_ref, e2_ref, acc_ref, g_ref, g_next_ref,
                     j, c_per_step):

    tn = xn_ref.shape[1]
    bf16_rows = 16
    reps = PEER_N_KEYS // bf16_rows

    def row_tile(ref, row):
        r16 = jnp.broadcast_to(ref[pl.ds(row, 1), :], (bf16_rows, tn)).astype(BF16)
        return jnp.concatenate([r16] * reps, axis=0)

    c0 = jnp.maximum(j - 1, 0) * c_per_step
    up_rows = 2 * PEER_N_KEYS
    blocks = []
    for cc in range(c_per_step):
        if (cc * PEER_N_KEYS) % up_rows == 0:
            rs = slice(cc * PEER_N_KEYS, cc * PEER_N_KEYS + up_rows)
            hT = jnp.dot(u_ref[rs, :], xn_ref[...], preferred_element_type=F32)
            g_next_ref[rs, :] = _gelu_sigmoid(hT).astype(BF16)
        c = c0 + cc
        w = None
        for h in range(PEER_HEADS):
            row = h * PEER_N_KEYS + c
            n_row = row_tile(cnt_ref, row)
            e1_row = row_tile(e1_ref, row)
            sl = slice(h * PEER_N_KEYS, (h + 1) * PEER_N_KEYS)
            term = jnp.where(rank2_ref[sl, :] < n_row, e2_ref[sl, :], 0.0) * e1_row
            w = term if w is None else w + term
        blocks.append(w * g_ref[cc * PEER_N_KEYS:(cc + 1) * PEER_N_KEYS, :])
    aT = jnp.concatenate(blocks, axis=0)
    acc_ref[...] += jnp.dot(vt_ref[...], aT, preferred_element_type=F32)


def peer_dense(xn, u, vt, cnt, rank2, e1, e2, res, *, tn=512, te=1024):
    D, T = xn.shape
    E = u.shape[0]
    rows = PEER_HEADS * PEER_N_KEYS
    sel_spec = pl.BlockSpec((rows, tn), lambda i, j: (0, i))
    n_tiles = E // te
    return pl.pallas_call(
        functools.partial(_peer_dense_kernel, c_per_step=te // PEER_N_KEYS),
        grid=(T // tn, n_tiles + 1),
        in_specs=[pl.BlockSpec((D, tn), lambda i, j: (0, i)),
                  pl.BlockSpec((te, D), lambda i, j: (jnp.minimum(j, n_tiles - 1), 0)),
                  pl.BlockSpec((D, te), lambda i, j: (0, jnp.maximum(j - 1, 0))),
                  sel_spec, sel_spec, sel_spec, sel_spec,
                  pl.BlockSpec((tn, D), lambda i, j: (i, 0))],
        out_specs=pl.BlockSpec((tn, D), lambda i, j: (i, 0)),
        out_shape=jax.ShapeDtypeStruct((T, D), F32),
        scratch_shapes=[pltpu.VMEM((D, tn), F32), pltpu.VMEM((te, tn), BF16), pltpu.VMEM((te, tn), BF16)],
        compiler_params=_cparams(("parallel", "arbitrary")),
    )(xn, u, vt, cnt, rank2, e1, e2, res)


def peer_layer(h, gamma, w_q, sub_keys, u, v):
    q, xn = norm_matmul(h, gamma, w_q.astype(BF16), emit_xn=True)
    keys = sub_keys.reshape(2 * PEER_HEADS, PEER_N_KEYS, PEER_HALF_DIM).astype(BF16)
    cnt, rank2, e1, e2 = peer_select(q, keys)
    return peer_dense(xn, u.astype(BF16), v.T.astype(BF16), cnt, rank2, e1, e2, h)


def _cumsum_aug_kernel(lf_ref, tri_ref, place_q_ref, place_k_ref, ones_q_ref, ones_k_ref,
                       qa_ref, ka_ref, carry_ref):
    @pl.when(pl.program_id(1) == 0)
    def _():
        carry_ref[...] = jnp.zeros_like(carry_ref)

    lf = lf_ref[0]
    c = jnp.dot(tri_ref[...], lf, preferred_element_type=F32, precision=lax.Precision.HIGHEST) + carry_ref[...]
    carry_ref[...] = c[-1:, :]
    c = c * LOG2E
    hi = c.astype(BF16)
    r1 = c - hi.astype(F32)
    mid = r1.astype(BF16)
    lo = (r1 - mid.astype(F32)).astype(BF16)
    nh = N_HEADS
    lane = lax.broadcasted_iota(jnp.int32, c.shape, 1)
    parts = jnp.where(lane < nh, hi.astype(F32),
                      jnp.where(lane < 2 * nh, pltpu.roll(mid.astype(F32), nh, axis=1),
                                pltpu.roll(lo.astype(F32), 2 * nh, axis=1)))
    parts = jnp.where(lane < 3 * nh, parts, 0.0).astype(BF16)
    qa_ref[0] = (jnp.dot(parts, place_q_ref[...], preferred_element_type=F32) + ones_q_ref[...]).astype(BF16)
    ka_ref[0] = (jnp.dot(parts, place_k_ref[...], preferred_element_type=F32) + ones_k_ref[...]).astype(BF16)


def fox_bias_operands(logf, *, tc=512):
    B, S, _ = logf.shape
    nh = N_HEADS
    tri = jnp.asarray(np.tril(np.ones((tc, tc), np.float32)))
    pq = np.zeros((LANES, nh * LANES), np.float32)
    pk = np.zeros((LANES, nh * LANES), np.float32)
    oq = np.zeros((1, nh * LANES), np.float32)
    ok = np.zeros((1, nh * LANES), np.float32)
    for h in range(nh):
        for part in range(3):
            pq[part * nh + h, h * LANES + part] = 1.0
            pk[part * nh + h, h * LANES + 3 + part] = -1.0
            oq[0, h * LANES + 3 + part] = 1.0
            ok[0, h * LANES + part] = 1.0
    const = lambda a: pl.BlockSpec(a.shape, lambda b, i: (0,) * a.ndim)
    pq, pk, oq, ok = jnp.asarray(pq, BF16), jnp.asarray(pk, BF16), jnp.asarray(oq), jnp.asarray(ok)
    out = jax.ShapeDtypeStruct((B, S, nh * LANES), BF16)
    return pl.pallas_call(
        _cumsum_aug_kernel,
        grid=(B, S // tc),
        in_specs=[pl.BlockSpec((1, tc, LANES), lambda b, i: (b, i, 0)),
                  const(tri), const(pq), const(pk), const(oq), const(ok)],
        out_specs=[pl.BlockSpec((1, tc, nh * LANES), lambda b, i: (b, i, 0))] * 2,
        out_shape=[out, out],
        scratch_shapes=[pltpu.VMEM((1, LANES), F32)],
        compiler_params=_cparams(("parallel", "arbitrary")),
    )(logf, tri, pq, pk, oq, ok)


def _fox_attn_kernel(q_ref, qa_ref, kv_ref, ka_ref, o_ref, *, tq, tk, heads_per_step):
    qi = pl.program_id(2)
    t0 = qi * tq
    n_full = t0 // tk
    n_diag = tq // tk
    lanes = [slice(hh * LANES, (hh + 1) * LANES) for hh in range(heads_per_step)]
    qs = [jnp.concatenate([q_ref[0, :, lsl], qa_ref[0, :, lsl]], axis=1) for lsl in lanes]

    def step(j, carry, masked):
        rows = pl.ds(pl.multiple_of(j * tk, tk), tk)
        klane = lax.broadcasted_iota(jnp.int32, (tk, LANES), 1)
        new = []
        for lsl, q, (m, acc) in zip(lanes, qs, carry):
            kv = kv_ref[0, rows, lsl]
            kk = jnp.concatenate([kv, ka_ref[0, rows, lsl]], axis=1)
            ones_v = jnp.where(klane < HEAD_DIM, 1.0, kv).astype(BF16)
            s = lax.dot_general(q, kk, _NT, preferred_element_type=F32)
            if masked:
                qpos = t0 + lax.broadcasted_iota(jnp.int32, (tq, 1), 0)
                kpos = j * tk + lax.broadcasted_iota(jnp.int32, (1, tk), 1)
                s = jnp.where(kpos <= qpos, s, NEG_INF)
            m_new = jnp.maximum(m, jnp.max(s, axis=1, keepdims=True))
            p = jnp.exp2(s - m_new).astype(BF16)
            acc = jnp.exp2(m - m_new) * acc + jnp.dot(p, ones_v, preferred_element_type=F32)
            new.append((m_new, acc))
        return tuple(new)

    init = (jnp.full((tq, 1), NEG_INF, F32), jnp.zeros((tq, LANES), F32))
    carry = lax.fori_loop(0, n_full, functools.partial(step, masked=False), (init,) * heads_per_step)
    for d in range(n_diag):
        carry = step(n_full + d, carry, True)
    outs = [acc / acc[:, 0:1] for (_, acc) in carry]
    lane = lax.broadcasted_iota(jnp.int32, (tq, LANES), 1)
    blocks = []
    for pair in range(heads_per_step // 2):
        a, b = outs[2 * pair], outs[2 * pair + 1]
        blocks.append(jnp.where(lane < HEAD_DIM, pltpu.roll(a, HEAD_DIM, axis=1), b))
    o_ref[0] = jnp.concatenate(blocks, axis=1).astype(o_ref.dtype) if len(blocks) > 1 else blocks[0].astype(o_ref.dtype)


def _head_slots(w, n_heads, second=None):
    D = w.shape[0]
    a = w.reshape(D, n_heads, HEAD_DIM)
    b = jnp.zeros_like(a) if second is None else second.reshape(D, n_heads, HEAD_DIM)
    return jnp.concatenate([a, b], axis=-1).reshape(D, n_heads * LANES)


def fox_layer(h, gamma, w_in, f_bias, w_out, B, S):
    aw = ATTN_WIDTH
    wq = _head_slots(w_in[:, :aw] * (HEAD_DIM ** -0.5 * LOG2E), N_HEADS)
    wkv = _head_slots(w_in[:, aw:2 * aw], N_HEADS, w_in[:, 2 * aw:3 * aw])
    w_main = jnp.concatenate([wq, wkv], axis=1).astype(BF16)
    wf = jnp.pad(w_in[:, 3 * aw:], ((0, 0), (0, LANES - N_HEADS))).astype(BF16)
    bf = jnp.pad(f_bias.astype(F32), (0, LANES - N_HEADS))
    qkv = norm_matmul(h, gamma, w_main)
    logf = norm_matmul(h, gamma, wf, bias=bf, act="log_sigmoid", out_dtype=F32)
    qa, ka = fox_bias_operands(logf.reshape(B, S, LANES))
    qkv = qkv.reshape(B, S, 2 * N_HEADS * LANES)
    o = fox_attention(qkv, qa, ka)
    return matmul_residual(o.reshape(B * S, aw), w_out.astype(BF16), h)


def fox_attention(qkv, qa, ka, *, tq=1024, tk=1024, heads_per_step=2):
    B, S, _ = qkv.shape
    hs = heads_per_step
    wq = hs * LANES
    n_qblk = N_HEADS // hs
    return pl.pallas_call(
        functools.partial(_fox_attn_kernel, tq=tq, tk=tk, heads_per_step=hs),
        grid=(B, n_qblk, S // tq),
        in_specs=[pl.BlockSpec((1, tq, wq), lambda b, h, i: (b, i, h)),
                  pl.BlockSpec((1, tq, wq), lambda b, h, i: (b, i, h)),
                  pl.BlockSpec((1, S, wq), lambda b, h, i: (b, 0, n_qblk + h)),
                  pl.BlockSpec((1, S, wq), lambda b, h, i: (b, 0, h))],
        out_specs=pl.BlockSpec((1, tq, hs * HEAD_DIM), lambda b, h, i: (b, i, h)),
        out_shape=jax.ShapeDtypeStruct((B, S, ATTN_WIDTH), BF16),
        compiler_params=_cparams(("parallel", "parallel", "arbitrary")),
    )(qkv, qa, qkv, ka)


def _rot_half_cols(w):
    D = w.shape[0]
    a = w.reshape(D, -1, HEAD_DIM)
    half = HEAD_DIM // 2
    return jnp.concatenate([-a[..., half:], a[..., :half]], axis=-1).reshape(w.shape)


def _rope_tables(S):
    half = HEAD_DIM // 2
    inv_freq = ROPE_THETA ** (-jnp.arange(half, dtype=F32) / half)
    ang = jnp.arange(S, dtype=F32)[:, None] * inv_freq[None, :]
    c, s = jnp.cos(ang), jnp.sin(ang)
    c2, s2 = jnp.concatenate([c, c], axis=1), jnp.concatenate([s, s], axis=1)
    cos = jnp.stack([jnp.concatenate([c2, jnp.ones_like(c2)], axis=1), jnp.concatenate([c2, c2], axis=1)])
    sin = jnp.stack([jnp.concatenate([s2, jnp.zeros_like(s2)], axis=1), jnp.concatenate([s2, s2], axis=1)])
    return cos, sin


def _compress_kernel(x_ref, pea_ref, peb_ref, wa_ref, wb_ref, w2_ref, o_ref, pa_ref, pb0_ref, *, n_rows):
    u = pl.program_id(1)
    x = x_ref[0].astype(F32)
    pa = jnp.dot((x + pea_ref[...]).astype(BF16), wa_ref[...], preferred_element_type=F32)
    pb = jnp.dot((x + peb_ref[...]).astype(BF16), wb_ref[...], preferred_element_type=F32)

    def emit(slab, hid):
        y = jnp.dot(_gelu_tanh(hid).astype(BF16), w2_ref[...], preferred_element_type=F32)
        o_ref[0, pl.ds(pl.multiple_of(slab * n_rows, n_rows), n_rows), :] = y

    @pl.when(u == 0)
    def _():
        pb0_ref[...] = pb

    @pl.when(u > 0)
    def _():
        emit(u - 1, pa_ref[...] + pb)

    @pl.when(u == 3)
    def _():
        emit(3, pa + pltpu.roll(pb0_ref[...], n_rows - 1, axis=0))

    pa_ref[...] = pa


def nsa_compress(src, pe, w1, w2):
    B, S, W = src.shape
    G = NSA_GROUPS
    n_rows = S // 64
    half = CMP_BLOCK // 2
    cw = half * W
    xv = src.reshape(B, n_rows, 4 * cw)
    pe_flat = jnp.transpose(pe, (1, 0, 2)).reshape(CMP_BLOCK, W).astype(F32)
    pea, peb = pe_flat[:half].reshape(1, cw), pe_flat[half:].reshape(1, cw)
    eye = jnp.eye(G, dtype=F32)
    wfull = jnp.einsum('gldh,gk->lkdgh', w1.astype(F32), eye).reshape(CMP_BLOCK, W, G * CMP_HIDDEN)
    wa = wfull[:half].reshape(cw, G * CMP_HIDDEN).astype(BF16)
    wb = wfull[half:].reshape(cw, G * CMP_HIDDEN).astype(BF16)
    w2bd = jnp.einsum('ghd,gk->ghkd', w2.astype(F32), eye).reshape(G * CMP_HIDDEN, W).astype(BF16)
    const = lambda a: pl.BlockSpec(a.shape, lambda b, u: (0,) * a.ndim)
    return pl.pallas_call(
        functools.partial(_compress_kernel, n_rows=n_rows),
        grid=(B, 4),
        in_specs=[pl.BlockSpec((1, n_rows, cw), lambda b, u: (b, 0, u)),
                  const(pea), const(peb), const(wa), const(wb), const(w2bd)],
        out_specs=pl.BlockSpec((1, 4 * n_rows, W), lambda b, u: (b, 0, 0)),
        out_shape=jax.ShapeDtypeStruct((B, 4 * n_rows, W), F32),
        scratch_shapes=[pltpu.VMEM((n_rows, G * CMP_HIDDEN), F32), pltpu.VMEM((n_rows, G * CMP_HIDDEN), F32)],
        compiler_params=_cparams(("parallel", "arbitrary")),
    )(xv, pea, peb, wa, wb, w2bd)


def _nsa_attn_kernel(q_ref, kvs_ref, kvw_ref, kvc_ref, kvct_ref, gate_ref, bmat_ref, pmat_ref, o_ref,
                     *, tq, tk, seq):
    R = NSA_Q_PER_GROUP
    M = R * tq
    n_slc = seq // SLC_BLOCK
    n_cmp = 4 * n_slc
    n_sel = min(SLC_TOPK, n_slc)
    blocks_per_chunk = tk // SLC_BLOCK
    assert tq & (tq - 1) == 0 and n_slc & (n_slc - 1) == 0 and tk % tq == 0
    log_slc = n_slc.bit_length() - 1
    qi = pl.program_id(2)
    t0 = qi * tq

    qs = jnp.concatenate([q_ref[0, :, r * LANES:(r + 1) * LANES] for r in range(R)], axis=0)

    kvc = kvc_ref[0, 0]
    sT = lax.dot_general(kvc, qs, _NT, preferred_element_type=F32)
    rowc = lax.broadcasted_iota(jnp.int32, (n_cmp, 1), 0)
    cmp_end = (rowc & (n_slc - 1)) * SLC_BLOCK + (rowc >> log_slc) * CMP_STRIDE + (CMP_BLOCK - 1)
    tcol = t0 + (lax.broadcasted_iota(jnp.int32, (1, M), 1) & (tq - 1))
    sm = jnp.where(cmp_end <= tcol, sT, NEG_INF)
    mx = jnp.max(sm, axis=0, keepdims=True)
    e = jnp.exp2(sm - mx)
    inv = jnp.where(mx > 0.5 * NEG_INF, 1.0 / jnp.sum(e, axis=0, keepdims=True), 0.0)
    pT = e * inv
    o_c = jnp.dot(kvct_ref[0, 0], pT.astype(BF16), preferred_element_type=F32).T

    psum = pT[:, 0:tq]
    for r in range(1, R):
        psum = psum + pT[:, r * tq:(r + 1) * tq]
    p0, p1, p2, p3 = (psum[u * n_slc:(u + 1) * n_slc, :] for u in range(4))
    jrow = lax.broadcasted_iota(jnp.int32, (n_slc, tq), 0)
    p3_prev = jnp.where(jrow == 0, 0.0, pltpu.roll(p3, 1, axis=0))
    imp = p0 + p1 + p2 + 0.5 * p3 + 0.5 * p3_prev
    cur = (t0 + lax.broadcasted_iota(jnp.int32, (n_slc, tq), 1)) >> (SLC_BLOCK.bit_length() - 1)
    forced = (jrow == 0) | (jrow == cur) | (jrow == cur - 1)
    vals = jnp.where(forced, -jnp.inf, jnp.where(jrow <= cur, imp, -jnp.inf))
    sel = jnp.where(forced, 1.0, 0.0)
    for _ in range(n_sel - 3):
        m = jnp.max(vals, axis=0, keepdims=True)
        idx = jnp.min(jnp.where(vals == m, jrow, n_slc), axis=0, keepdims=True)
        hit = jrow == idx
        sel = jnp.where(hit, 1.0, sel)
        vals = jnp.where(hit, -jnp.inf, vals)
    sel_bias = ((sel.T - 1.0) * (-NEG_INF)).astype(BF16)

    n_parts = 2
    hp = R // n_parts
    mp = hp * tq
    qparts = [qs[i * mp:(i + 1) * mp] for i in range(n_parts)]
    qrow = t0 + (lax.broadcasted_iota(jnp.int32, (mp, 1), 0) & (tq - 1))
    kcol = lax.broadcasted_iota(jnp.int32, (1, tk), 1)
    n_chunks = seq // tk
    j_last = t0 // tk

    klane = lax.broadcasted_iota(jnp.int32, (tk, LANES), 1)

    def slc_step(j, carry, masked):
        kv = kvs_ref[0, pl.ds(pl.multiple_of(j * tk, tk), tk), :]
        k_sel = jnp.where(klane < HEAD_DIM, kv, bmat_ref[...]).astype(BF16)
        ones_v = jnp.where(klane < HEAD_DIM, 1.0, kv).astype(BF16)
        off = pl.multiple_of(blocks_per_chunk * (n_chunks - 1 - j), blocks_per_chunk)
        place = pmat_ref[pl.ds(off, n_slc), :].astype(BF16)
        q_bias = jnp.dot(sel_bias, place, preferred_element_type=F32).astype(BF16)
        q_bias = jnp.concatenate([q_bias] * hp, axis=0)
        new = []
        for qp, (m, acc) in zip(qparts, carry):
            s = lax.dot_general(qp + q_bias, k_sel, _NT, preferred_element_type=F32)
            if masked:
                s = jnp.where(j * tk + kcol <= qrow, s, NEG_INF)
            m_new = jnp.maximum(m, jnp.max(s, axis=1, keepdims=True))
            p = jnp.exp2(s - m_new).astype(BF16)
            acc = jnp.exp2(m - m_new) * acc + jnp.dot(p, ones_v, preferred_element_type=F32)
            new.append((m_new, acc))
        return tuple(new)

    init = (jnp.full((mp, 1), NEG_INF, F32), jnp.zeros((mp, LANES), F32))
    carry = lax.fori_loop(0, j_last, functools.partial(slc_step, masked=False), (init,) * n_parts)
    carry = slc_step(j_last, carry, True)
    o_s = jnp.concatenate([acc / acc[:, 0:1] for (_, acc) in carry], axis=0)

    wlen = WINDOW + tq
    start = jnp.maximum(t0 - WINDOW, 0)
    kvw = kvw_ref[0, pl.ds(pl.multiple_of(start, tq), wlen), :]
    s_w = lax.dot_general(qs, kvw, _NT, preferred_element_type=F32)
    qpos = t0 + (lax.broadcasted_iota(jnp.int32, (M, 1), 0) & (tq - 1))
    kpos = start + lax.broadcasted_iota(jnp.int32, (1, wlen), 1)
    s_w = jnp.where(kpos <= qpos, jnp.where(kpos > qpos - WINDOW, s_w, NEG_INF), NEG_INF)
    p_w = jnp.exp2(s_w - jnp.max(s_w, axis=1, keepdims=True)).astype(BF16)
    wlane = lax.broadcasted_iota(jnp.int32, (wlen, LANES), 1)
    acc_w = jnp.dot(p_w, jnp.where(wlane < HEAD_DIM, 1.0, kvw).astype(BF16), preferred_element_type=F32)
    o_w = acc_w / acc_w[:, 0:1]

    gates = gate_ref[0]
    lane = lax.broadcasted_iota(jnp.int32, (tq, LANES), 1)
    comb = []
    for r in range(R):
        rs = slice(r * tq, (r + 1) * tq)
        comb.append(gates[:, r:r + 1] * o_c[rs] + gates[:, R + r:R + r + 1] * o_s[rs]
                    + gates[:, 2 * R + r:2 * R + r + 1] * o_w[rs])
    out = [jnp.where(lane < HEAD_DIM, pltpu.roll(comb[2 * i], HEAD_DIM, axis=1), comb[2 * i + 1])
           for i in range(R // 2)]
    o_ref[0] = jnp.concatenate(out, axis=1).astype(o_ref.dtype)


def nsa_attention(qkv, kvc, kvct, gates, *, tq=256, tk=1024):
    B, S, _ = qkv.shape
    G, R = NSA_GROUPS, NSA_Q_PER_GROUP
    tk = min(tk, S)
    n_slc = S // SLC_BLOCK
    n_cmp = kvc.shape[2]
    bpc = tk // SLC_BLOCK
    assert bpc <= LANES - HEAD_DIM
    off = bpc * (S // tk - 1)
    lane = np.arange(LANES)[None, :]
    bmat = jnp.asarray(lane - HEAD_DIM == np.arange(tk)[:, None] // SLC_BLOCK, BF16)
    pmat = jnp.asarray((np.arange(n_slc + off)[:, None] - off == lane - HEAD_DIM) & (lane >= HEAD_DIM)
                       & (lane < HEAD_DIM + bpc), F32)
    slc_blk0, win_blk0 = N_HEADS, N_HEADS + G
    return pl.pallas_call(
        functools.partial(_nsa_attn_kernel, tq=tq, tk=tk, seq=S),
        grid=(B, G, S // tq),
        in_specs=[pl.BlockSpec((1, tq, R * LANES), lambda b, g, i: (b, i, g)),
                  pl.BlockSpec((1, S, LANES), lambda b, g, i: (b, 0, slc_blk0 + g)),
                  pl.BlockSpec((1, S, LANES), lambda b, g, i: (b, 0, win_blk0 + g)),
                  pl.BlockSpec((1, 1, n_cmp, LANES), lambda b, g, i: (b, g, 0, 0)),
                  pl.BlockSpec((1, 1, LANES, n_cmp), lambda b, g, i: (b, g, 0, 0)),
                  pl.BlockSpec((1, tq, LANES), lambda b, g, i: (b, i, g)),
                  pl.BlockSpec(bmat.shape, lambda b, g, i: (0, 0)),
                  pl.BlockSpec(pmat.shape, lambda b, g, i: (0, 0))],
        out_specs=pl.BlockSpec((1, tq, R * HEAD_DIM), lambda b, g, i: (b, i, g)),
        out_shape=jax.ShapeDtypeStruct((B, S, ATTN_WIDTH), BF16),
        compiler_params=_cparams(("parallel", "parallel", "arbitrary")),
    )(qkv, qkv, qkv, kvc, kvct, gates, bmat, pmat)


def nsa_layer(h, gamma, w_in, pe_k, w1_k, w2_k, pe_v, w1_v, w2_v, w_out, B, S):
    G, R, hd, aw = NSA_GROUPS, NSA_Q_PER_GROUP, HEAD_DIM, ATTN_WIDTH
    kvd = G * hd
    sec = lambda i: w_in[:, aw + i * kvd: aw + (i + 1) * kvd]
    wq = w_in[:, :aw] * (hd ** -0.5 * LOG2E)
    wa = jnp.concatenate([_head_slots(wq, N_HEADS), _head_slots(sec(2), G, sec(3)),
                          _head_slots(sec(4), G, sec(5))], axis=1).astype(BF16)
    wb = jnp.concatenate([_head_slots(_rot_half_cols(wq), N_HEADS), _head_slots(_rot_half_cols(sec(2)), G),
                          _head_slots(_rot_half_cols(sec(4)), G)], axis=1).astype(BF16)
    cos, sin = _rope_tables(S)
    qkv = norm_matmul(h, gamma, wa, wb=wb, cos=cos, sin=sin)
    kc_src = norm_matmul(h, gamma, sec(0).astype(BF16), wb=_rot_half_cols(sec(0)).astype(BF16),
                         cos=cos, sin=sin, table_of_tile=lambda j: 1)
    vc_src = norm_matmul(h, gamma, sec(1).astype(BF16))
    wg = w_in[:, aw + 6 * kvd:].reshape(-1, 3, G, R)
    wg = jnp.transpose(wg, (0, 2, 1, 3)).reshape(-1, G, 3 * R)
    wg = jnp.pad(wg, ((0, 0), (0, 0), (0, LANES - 3 * R))).reshape(-1, G * LANES).astype(BF16)
    gates = norm_matmul(h, gamma, wg, act="sigmoid", out_dtype=F32)
    kc = nsa_compress(kc_src.reshape(B, S, kvd), pe_k, w1_k, w2_k)
    vc = nsa_compress(vc_src.reshape(B, S, kvd), pe_v, w1_v, w2_v)
    n_cmp = kc.shape[1]
    kvc = jnp.concatenate([kc.reshape(B, n_cmp, G, hd), vc.reshape(B, n_cmp, G, hd)], axis=-1)
    kvc = jnp.transpose(kvc, (0, 2, 1, 3)).astype(BF16)
    kvct = jnp.swapaxes(kvc, 2, 3)
    o = nsa_attention(qkv.reshape(B, S, -1), kvc, kvct, gates.reshape(B, S, G * LANES))
    return matmul_residual(o.reshape(B * S, aw), w_out.astype(BF16), h)


def kernel(x, l0_attn_norm, l0_w_in, l0_cmp_pe_k, l0_cmp_w1_k, l0_cmp_w2_k, l0_cmp_pe_v, l0_cmp_w1_v,
           l0_cmp_w2_v, l0_w_out, l0_ffn_norm, l0_peer_wq, l0_peer_keys, l0_peer_u, l0_peer_v,
           l1_attn_norm, l1_w_in, l1_f_bias, l1_w_out, l1_ffn_norm, l1_peer_wq, l1_peer_keys, l1_peer_u,
           l1_peer_v, final_norm):
    B, S, D = x.shape
    h = x.reshape(B * S, D)
    h = nsa_layer(h, l0_attn_norm, l0_w_in, l0_cmp_pe_k, l0_cmp_w1_k, l0_cmp_w2_k, l0_cmp_pe_v, l0_cmp_w1_v,
                  l0_cmp_w2_v, l0_w_out, B, S)
    h = peer_layer(h, l0_ffn_norm, l0_peer_wq, l0_peer_keys, l0_peer_u, l0_peer_v)
    h = fox_layer(h, l1_attn_norm, l1_w_in, l1_f_bias, l1_w_out, B, S)
    h = peer_layer(h, l1_ffn_norm, l1_peer_wq, l1_peer_keys, l1_peer_u, l1_peer_v)
    return rmsnorm(h, final_norm).reshape(B, S, D)
```

```python
import functools

import numpy as np
import jax
import jax.numpy as jnp
from jax import lax
from jax.experimental import pallas as pl
from jax.experimental.pallas import tpu as pltpu

F32 = jnp.float32
BF16 = jnp.bfloat16

D_MODEL = 1024
N_HEADS = 16
HEAD_DIM = 64
ATTN_WIDTH = N_HEADS * HEAD_DIM
NSA_GROUPS = 4
NSA_Q_PER_GROUP = N_HEADS // NSA_GROUPS
CMP_BLOCK = 32
CMP_STRIDE = 16
CMP_HIDDEN = 2 * HEAD_DIM
SLC_BLOCK = 64
SLC_TOPK = 16
WINDOW = 512
FORCE_SCORE = 1.0e4
ROPE_THETA = 10000.0
PEER_HEADS = 8
PEER_N_KEYS = 128
PEER_TOPK = 16
PEER_HALF_DIM = 128
RMS_EPS = 1e-6
NEG_INF = -1e30
LOG2E = 1.4426950408889634

LANES = 128
VMEM_LIMIT_BYTES = 56 * 1024 * 1024

_NT = (((1,), (1,)), ((), ()))


def _cparams(sem, vmem=VMEM_LIMIT_BYTES, flags=None):
    return pltpu.CompilerParams(dimension_semantics=sem, vmem_limit_bytes=vmem, flags=flags)


def _gelu_tanh(x):
    return 0.5 * x * (1.0 + jnp.tanh(0.7978845608028654 * (x + 0.044715 * (x * x * x))))


def _gelu_sigmoid(x):
    c = -2.0 * 0.7978845608028654 * LOG2E
    t = x * (c + (c * 0.044715) * (x * x))
    return x / (1.0 + jnp.exp2(t))


def _rms_rows(x, g):
    ms = jnp.mean(x * x, axis=-1, keepdims=True)
    return x * lax.rsqrt(ms + RMS_EPS) * g


def _norm_mm_kernel(*refs, act, has_bias, rope, emit_xn):
    it = iter(refs)
    x_ref, g_ref = next(it), next(it)
    wa_ref = next(it)
    wb_ref = next(it) if rope else None
    cos_ref = next(it) if rope else None
    sin_ref = next(it) if rope else None
    b_ref = next(it) if has_bias else None
    o_ref = next(it)
    xo_ref = next(it) if emit_xn else None
    xn_ref = next(it)

    @pl.when(pl.program_id(1) == 0)
    def _():
        xn = _rms_rows(x_ref[...], g_ref[...])
        xn_ref[...] = xn.astype(BF16)
        if emit_xn:
            xo_ref[...] = xn.T.astype(BF16)

    xn = xn_ref[...]
    y = jnp.dot(xn, wa_ref[...], preferred_element_type=F32)
    if rope:
        yb = jnp.dot(xn, wb_ref[...], preferred_element_type=F32)
        cos, sin = cos_ref[...], sin_ref[...]
        for s in range(y.shape[1] // LANES):
            sl = slice(s * LANES, (s + 1) * LANES)
            o_ref[:, sl] = (y[:, sl] * cos + yb[:, sl] * sin).astype(o_ref.dtype)
        return
    if has_bias:
        y = y + b_ref[...]
    if act == "sigmoid":
        y = jax.nn.sigmoid(y)
    elif act == "log_sigmoid":
        y = jax.nn.log_sigmoid(y)
    o_ref[...] = y.astype(o_ref.dtype)


def norm_matmul(x, gamma, wa, *, wb=None, cos=None, sin=None, table_of_tile=None, bias=None,
                act=None, out_dtype=BF16, emit_xn=False, tm=512, tn=512):
    T, D = x.shape
    N = wa.shape[1]
    tn = min(tn, N)
    assert T % tm == 0 and N % tn == 0 and tn % LANES == 0
    rope = wb is not None
    in_specs = [pl.BlockSpec((tm, D), lambda i, j: (i, 0)),
                pl.BlockSpec((1, D), lambda i, j: (0, 0)),
                pl.BlockSpec((D, tn), lambda i, j: (0, j))]
    args = [x, gamma.reshape(1, D).astype(F32), wa]
    if rope:
        S = cos.shape[1]
        assert S % tm == 0
        n_pos = S // tm
        tmap = table_of_tile if table_of_tile is not None else (lambda j: 0)
        in_specs += [pl.BlockSpec((D, tn), lambda i, j: (0, j)),
                     pl.BlockSpec((None, tm, LANES), lambda i, j: (tmap(j), i % n_pos, 0)),
                     pl.BlockSpec((None, tm, LANES), lambda i, j: (tmap(j), i % n_pos, 0))]
        args += [wb, cos, sin]
    if bias is not None:
        in_specs.append(pl.BlockSpec((1, tn), lambda i, j: (0, j)))
        args.append(bias.reshape(1, N).astype(F32))
    out_shape = [jax.ShapeDtypeStruct((T, N), out_dtype)]
    out_specs = [pl.BlockSpec((tm, tn), lambda i, j: (i, j))]
    if emit_xn:
        out_shape.append(jax.ShapeDtypeStruct((D, T), BF16))
        out_specs.append(pl.BlockSpec((D, tm), lambda i, j: (0, i)))
    res = pl.pallas_call(
        functools.partial(_norm_mm_kernel, act=act, has_bias=bias is not None, rope=rope, emit_xn=emit_xn),
        grid=(T // tm, N // tn),
        in_specs=in_specs,
        out_specs=out_specs,
        out_shape=out_shape,
        scratch_shapes=[pltpu.VMEM((tm, D), BF16)],
        compiler_params=_cparams(("parallel", "arbitrary")),
    )(*args)
    return res if emit_xn else res[0]


def _mm_res_kernel(a_ref, w_ref, r_ref, o_ref):
    o_ref[...] = r_ref[...] + jnp.dot(a_ref[...], w_ref[...], preferred_element_type=F32)


def matmul_residual(a, w, res, *, tm=512, tn=512):
    T, K = a.shape
    N = w.shape[1]
    assert T % tm == 0 and N % tn == 0
    return pl.pallas_call(
        _mm_res_kernel,
        grid=(T // tm, N // tn),
        in_specs=[pl.BlockSpec((tm, K), lambda i, j: (i, 0)),
                  pl.BlockSpec((K, tn), lambda i, j: (0, j)),
                  pl.BlockSpec((tm, tn), lambda i, j: (i, j))],
        out_specs=pl.BlockSpec((tm, tn), lambda i, j: (i, j)),
        out_shape=jax.ShapeDtypeStruct((T, N), F32),
        compiler_params=_cparams(("parallel", "arbitrary")),
    )(a, w, res)


def _rmsnorm_kernel(x_ref, g_ref, o_ref):
    o_ref[...] = _rms_rows(x_ref[...], g_ref[...])


def rmsnorm(x, gamma, *, tm=512):
    T, D = x.shape
    return pl.pallas_call(
        _rmsnorm_kernel,
        grid=(T // tm,),
        in_specs=[pl.BlockSpec((tm, D), lambda i: (i, 0)), pl.BlockSpec((1, D), lambda i: (0, 0))],
        out_specs=pl.BlockSpec((tm, D), lambda i: (i, 0)),
        out_shape=jax.ShapeDtypeStruct((T, D), F32),
        compiler_params=_cparams(("parallel",)),
    )(x, gamma.reshape(1, D).astype(F32))


def _peer_cand_tables(tn):
    fidx, vmask = [], []
    for k2 in range(16):
        fidx.append(k2); vmask.append(0.0)
    for k1 in range(1, 8):
        lim = PEER_TOPK // (k1 + 1)
        for k2 in range(8):
            fidx.append(k1 * 16 + k2); vmask.append(0.0 if k2 < lim else -np.inf)
    for k1 in range(8, 16):
        fidx.append(k1 * 16); vmask.append(0.0)
    fidx = np.broadcast_to(np.asarray(fidx, np.int32)[:, None], (80, tn))
    vmask = np.broadcast_to(np.asarray(vmask, np.float32)[:, None], (80, tn))
    return jnp.asarray(fidx), jnp.asarray(vmask)


def _top16_rows(s, exact_ties):
    n, tn = s.shape
    rows = lax.broadcasted_iota(jnp.int32, (n, tn), 0)
    rows16 = lax.broadcasted_iota(jnp.int32, (PEER_TOPK, tn), 0)
    rank = jnp.full((n, tn), float(PEER_TOPK), F32)
    tops = jnp.zeros((PEER_TOPK, tn), F32)
    v = s
    for k in range(PEER_TOPK):
        m = jnp.max(v, axis=0, keepdims=True)
        if exact_ties:
            hit = rows == jnp.min(jnp.where(v == m, rows, n), axis=0, keepdims=True)
        else:
            hit = v == m
        rank = jnp.where(hit, float(k), rank)
        v = jnp.where(hit, -jnp.inf, v)
        tops = jnp.where(rows16 == k, m, tops)
    return tops, rank, v


def _peer_select_head(q_ref, keys_ref, fidx, vmask, exact_ties):
    tops, ranks, es, picked = [], [], [], []
    for p in range(2):
        q = q_ref[:, p * PEER_HALF_DIM:(p + 1) * PEER_HALF_DIM]
        s = lax.dot_general(keys_ref[p], q, _NT, preferred_element_type=F32)
        t, r, v = _top16_rows(s, exact_ties)
        tops.append(t); ranks.append(r)
        es.append(jnp.exp(s - t[0:1, :]))
        picked.append(jnp.sum(jnp.where(v == -jnp.inf, 1.0, 0.0), axis=0, keepdims=True))
    ts1, ts2 = tops
    pieces = [ts1[0:1, :] + ts2]
    for k1 in range(1, 8):
        pieces.append(ts1[k1:k1 + 1, :] + ts2[0:8, :])
    pieces.append(ts1[8:16, :] + ts2[0:1, :])
    cand0 = jnp.concatenate(pieces, axis=0) + vmask
    cand = cand0
    for _ in range(PEER_TOPK):
        m = jnp.max(cand, axis=0, keepdims=True)
        if exact_ties:
            hit = fidx == jnp.min(jnp.where(cand == m, fidx, 4096), axis=0, keepdims=True)
        else:
            hit = cand == m
        cand = jnp.where(hit, -jnp.inf, cand)
    taken = jnp.logical_and(cand == -jnp.inf, vmask == 0.0)
    takenf = taken.astype(F32)
    picked.append(jnp.sum(takenf, axis=0, keepdims=True))
    unique = jnp.min(jnp.where((picked[0] == PEER_TOPK) & (picked[1] == PEER_TOPK) & (picked[2] == PEER_TOPK),
                               1.0, 0.0)) > 0.5
    best = ts1[0:1, :] + ts2[0:1, :]
    z = jnp.sum(jnp.where(taken, jnp.exp(cand0 - best), 0.0), axis=0, keepdims=True)
    counts = [jnp.sum(takenf[0:16, :], axis=0, keepdims=True)]
    for k1 in range(1, 8):
        counts.append(jnp.sum(takenf[16 + 8 * (k1 - 1):16 + 8 * k1, :], axis=0, keepdims=True))
    tail = takenf[72:80, :]
    cnt = jnp.zeros_like(ranks[0])
    for k1 in range(PEER_TOPK):
        nk = counts[k1] if k1 < 8 else tail[k1 - 8:k1 - 7, :]
        cnt = jnp.where(ranks[0] == float(k1), nk, cnt)
    return (cnt, ranks[1], es[0], es[1] / z), unique


def _peer_select_kernel(q_ref, keys_ref, fidx_ref, vmask_ref, cnt_ref, rank2_ref, e1_ref, e2_ref):
    fidx = fidx_ref[...]
    vmask = vmask_ref[...]

    def store(vals):
        for ref, val in zip((cnt_ref, rank2_ref, e1_ref, e2_ref), vals):
            ref[...] = val.astype(ref.dtype)

    vals, unique = _peer_select_head(q_ref, keys_ref, fidx, vmask, exact_ties=False)
    store(vals)

    @pl.when(jnp.logical_not(unique))
    def _():
        store(_peer_select_head(q_ref, keys_ref, fidx, vmask, exact_ties=True)[0])


def peer_select(q, keys, *, tn=512):
    T = q.shape[0]
    fidx, vmask = _peer_cand_tables(tn)
    rows = PEER_HEADS * PEER_N_KEYS
    ospec = pl.BlockSpec((PEER_N_KEYS, tn), lambda i, h: (h, i))
    return pl.pallas_call(
        _peer_select_kernel,
        grid=(T // tn, PEER_HEADS),
        in_specs=[pl.BlockSpec((tn, 2 * PEER_HALF_DIM), lambda i, h: (i, h)),
                  pl.BlockSpec((2, PEER_N_KEYS, PEER_HALF_DIM), lambda i, h: (h, 0, 0)),
                  pl.BlockSpec((80, tn), lambda i, h: (0, 0)),
                  pl.BlockSpec((80, tn), lambda i, h: (0, 0))],
        out_specs=[ospec] * 4,
        out_shape=[jax.ShapeDtypeStruct((rows, T), dt) for dt in (F32, BF16, F32, BF16)],
        compiler_params=_cparams(("parallel", "parallel")),
    )(q, keys, fidx, vmask)


def _peer_dense_kernel(xn_ref, u_ref, vt_ref, cnt_ref, rank2_ref, e1_ref, e2_ref, res_ref, o_ref,
                       acc_ref, g0_ref, g1_ref, *, c_per_step):
    j = pl.program_id(1)
    n_tiles = pl.num_programs(1) - 1

    @pl.when(j == 0)
    def _():
        acc_ref[...] = jnp.zeros_like(acc_ref)
        g1_ref[...] = jnp.zeros_like(g1_ref)

    @pl.when(j % 2 == 0)
    def _():
        _peer_dense_step(xn_ref, u_ref, vt_ref, cnt_ref, rank2_ref, e1_ref, e2_ref, acc_ref,
                         g1_ref, g0_ref, j, c_per_step)

    @pl.when(j % 2 == 1)
    def _():
        _peer_dense_step(xn_ref, u_ref, vt_ref, cnt_ref, rank2_ref, e1_ref, e2_ref, acc_ref,
                         g0_ref, g1_ref, j, c_per_step)

    @pl.when(j == n_tiles)
    def _():
        o_ref[...] = res_ref[...] + acc_ref[...].T


def _peer_dense_step(xn_ref, u_ref, vt_ref, cnt_ref, rank2_ref, e1_ref, e2_ref, acc_ref, g_ref, g_next_ref,
                     j, c_per_step):

    tn = xn_ref.shape[1]
    bf16_rows = 16
    reps = PEER_N_KEYS // bf16_rows

    def row_tile(ref, row):
        r16 = jnp.broadcast_to(ref[pl.ds(row, 1), :], (bf16_rows, tn)).astype(BF16)
        return jnp.concatenate([r16] * reps, axis=0)

    c0 = jnp.maximum(j - 1, 0) * c_per_step
    up_rows = 2 * PEER_N_KEYS
    blocks = []
    for cc in range(c_per_step):
        if (cc * PEER_N_KEYS) % up_rows == 0:
            rs = slice(cc * PEER_N_KEYS, cc * PEER_N_KEYS + up_rows)
            hT = jnp.dot(u_ref[rs, :], xn_ref[...], preferred_element_type=F32)
            g_next_ref[rs, :] = _gelu_sigmoid(hT).astype(BF16)
        c = c0 + cc
        w = None
        for h in range(PEER_HEADS):
            row = h * PEER_N_KEYS + c
            n_row = row_tile(cnt_ref, row)
            e1_row = row_tile(e1_ref, row)
            sl = slice(h * PEER_N_KEYS, (h + 1) * PEER_N_KEYS)
            term = jnp.where(rank2_ref[sl, :] < n_row, e2_ref[sl, :], 0.0) * e1_row
            w = term if w is None else w + term
        blocks.append(w * g_ref[cc * PEER_N_KEYS:(cc + 1) * PEER_N_KEYS, :])
    aT = jnp.concatenate(blocks, axis=0)
    acc_ref[...] += jnp.dot(vt_ref[...], aT, preferred_element_type=F32)


def peer_dense(xn, u, vt, cnt, rank2, e1, e2, res, *, tn=512, te=1024):
    D, T = xn.shape
    E = u.shape[0]
    rows = PEER_HEADS * PEER_N_KEYS
    sel_spec = pl.BlockSpec((rows, tn), lambda i, j: (0, i))
    n_tiles = E // te
    return pl.pallas_call(
        functools.partial(_peer_dense_kernel, c_per_step=te // PEER_N_KEYS),
        grid=(T // tn, n_tiles + 1),
        in_specs=[pl.BlockSpec((D, tn), lambda i, j: (0, i)),
                  pl.BlockSpec((te, D), lambda i, j: (jnp.minimum(j, n_tiles - 1), 0)),
                  pl.BlockSpec((D, te), lambda i, j: (0, jnp.maximum(j - 1, 0))),
                  sel_spec, sel_spec, sel_spec, sel_spec,
                  pl.BlockSpec((tn, D), lambda i, j: (i, 0))],
        out_specs=pl.BlockSpec((tn, D), lambda i, j: (i, 0)),
        out_shape=jax.ShapeDtypeStruct((T, D), F32),
        scratch_shapes=[pltpu.VMEM((D, tn), F32), pltpu.VMEM((te, tn), BF16), pltpu.VMEM((te, tn), BF16)],
        compiler_params=_cparams(("parallel", "arbitrary")),
    )(xn, u, vt, cnt, rank2, e1, e2, res)


def peer_layer(h, gamma, w_q, sub_keys, u, v):
    q, xn = norm_matmul(h, gamma, w_q.astype(BF16), emit_xn=True)
    keys = sub_keys.reshape(2 * PEER_HEADS, PEER_N_KEYS, PEER_HALF_DIM).astype(BF16)
    cnt, rank2, e1, e2 = peer_select(q, keys)
    return peer_dense(xn, u.astype(BF16), v.T.astype(BF16), cnt, rank2, e1, e2, h)


def _cumsum_aug_kernel(lf_ref, tri_ref, place_q_ref, place_k_ref, ones_q_ref, ones_k_ref,
                       qa_ref, ka_ref, carry_ref):
    @pl.when(pl.program_id(1) == 0)
    def _():
        carry_ref[...] = jnp.zeros_like(carry_ref)

    lf = lf_ref[0]
    c = jnp.dot(tri_ref[...], lf, preferred_element_type=F32, precision=lax.Precision.HIGHEST) + carry_ref[...]
    carry_ref[...] = c[-1:, :]
    c = c * LOG2E
    hi = c.astype(BF16)
    r1 = c - hi.astype(F32)
    mid = r1.astype(BF16)
    lo = (r1 - mid.astype(F32)).astype(BF16)
    nh = N_HEADS
    lane = lax.broadcasted_iota(jnp.int32, c.shape, 1)
    parts = jnp.where(lane < nh, hi.astype(F32),
                      jnp.where(lane < 2 * nh, pltpu.roll(mid.astype(F32), nh, axis=1),
                                pltpu.roll(lo.astype(F32), 2 * nh, axis=1)))
    parts = jnp.where(lane < 3 * nh, parts, 0.0).astype(BF16)
    qa_ref[0] = (jnp.dot(parts, place_q_ref[...], preferred_element_type=F32) + ones_q_ref[...]).astype(BF16)
    ka_ref[0] = (jnp.dot(parts, place_k_ref[...], preferred_element_type=F32) + ones_k_ref[...]).astype(BF16)


def fox_bias_operands(logf, *, tc=512):
    B, S, _ = logf.shape
    nh = N_HEADS
    tri = jnp.asarray(np.tril(np.ones((tc, tc), np.float32)))
    pq = np.zeros((LANES, nh * LANES), np.float32)
    pk = np.zeros((LANES, nh * LANES), np.float32)
    oq = np.zeros((1, nh * LANES), np.float32)
    ok = np.zeros((1, nh * LANES), np.float32)
    for h in range(nh):
        for part in range(3):
            pq[part * nh + h, h * LANES + part] = 1.0
            pk[part * nh + h, h * LANES + 3 + part] = -1.0
            oq[0, h * LANES + 3 + part] = 1.0
            ok[0, h * LANES + part] = 1.0
    const = lambda a: pl.BlockSpec(a.shape, lambda b, i: (0,) * a.ndim)
    pq, pk, oq, ok = jnp.asarray(pq, BF16), jnp.asarray(pk, BF16), jnp.asarray(oq), jnp.asarray(ok)
    out = jax.ShapeDtypeStruct((B, S, nh * LANES), BF16)
    return pl.pallas_call(
        _cumsum_aug_kernel,
        grid=(B, S // tc),
        in_specs=[pl.BlockSpec((1, tc, LANES), lambda b, i: (b, i, 0)),
                  const(tri), const(pq), const(pk), const(oq), const(ok)],
        out_specs=[pl.BlockSpec((1, tc, nh * LANES), lambda b, i: (b, i, 0))] * 2,
        out_shape=[out, out],
        scratch_shapes=[pltpu.VMEM((1, LANES), F32)],
        compiler_params=_cparams(("parallel", "arbitrary")),
    )(logf, tri, pq, pk, oq, ok)


def _fox_attn_kernel(q_ref, qa_ref, kv_ref, ka_ref, o_ref, *, tq, tk, heads_per_step):
    qi = pl.program_id(2)
    t0 = qi * tq
    n_full = t0 // tk
    n_diag = tq // tk
    lanes = [slice(hh * LANES, (hh + 1) * LANES) for hh in range(heads_per_step)]
    qs = [jnp.concatenate([q_ref[0, :, lsl], qa_ref[0, :, lsl]], axis=1) for lsl in lanes]

    def step(j, carry, masked):
        rows = pl.ds(pl.multiple_of(j * tk, tk), tk)
        klane = lax.broadcasted_iota(jnp.int32, (tk, LANES), 1)
        new = []
        for lsl, q, (m, acc) in zip(lanes, qs, carry):
            kv = kv_ref[0, rows, lsl]
            kk = jnp.concatenate([kv, ka_ref[0, rows, lsl]], axis=1)
            ones_v = jnp.where(klane < HEAD_DIM, 1.0, kv).astype(BF16)
            s = lax.dot_general(q, kk, _NT, preferred_element_type=F32)
            if masked:
                qpos = t0 + lax.broadcasted_iota(jnp.int32, (tq, 1), 0)
                kpos = j * tk + lax.broadcasted_iota(jnp.int32, (1, tk), 1)
                s = jnp.where(kpos <= qpos, s, NEG_INF)
            m_new = jnp.maximum(m, jnp.max(s, axis=1, keepdims=True))
            p = jnp.exp2(s - m_new).astype(BF16)
            acc = jnp.exp2(m - m_new) * acc + jnp.dot(p, ones_v, preferred_element_type=F32)
            new.append((m_new, acc))
        return tuple(new)

    init = (jnp.full((tq, 1), NEG_INF, F32), jnp.zeros((tq, LANES), F32))
    carry = lax.fori_loop(0, n_full, functools.partial(step, masked=False), (init,) * heads_per_step)
    for d in range(n_diag):
        carry = step(n_full + d, carry, True)
    outs = [acc / acc[:, 0:1] for (_, acc) in carry]
    lane = lax.broadcasted_iota(jnp.int32, (tq, LANES), 1)
    blocks = []
    for pair in range(heads_per_step // 2):
        a, b = outs[2 * pair], outs[2 * pair + 1]
        blocks.append(jnp.where(lane < HEAD_DIM, pltpu.roll(a, HEAD_DIM, axis=1), b))
    o_ref[0] = jnp.concatenate(blocks, axis=1).astype(o_ref.dtype) if len(blocks) > 1 else blocks[0].astype(o_ref.dtype)


def _head_slots(w, n_heads, second=None):
    D = w.shape[0]
    a = w.reshape(D, n_heads, HEAD_DIM)
    b = jnp.zeros_like(a) if second is None else second.reshape(D, n_heads, HEAD_DIM)
    return jnp.concatenate([a, b], axis=-1).reshape(D, n_heads * LANES)


def fox_layer(h, gamma, w_in, f_bias, w_out, B, S):
    aw = ATTN_WIDTH
    wq = _head_slots(w_in[:, :aw] * (HEAD_DIM ** -0.5 * LOG2E), N_HEADS)
    wkv = _head_slots(w_in[:, aw:2 * aw], N_HEADS, w_in[:, 2 * aw:3 * aw])
    w_main = jnp.concatenate([wq, wkv], axis=1).astype(BF16)
    wf = jnp.pad(w_in[:, 3 * aw:], ((0, 0), (0, LANES - N_HEADS))).astype(BF16)
    bf = jnp.pad(f_bias.astype(F32), (0, LANES - N_HEADS))
    qkv = norm_matmul(h, gamma, w_main)
    logf = norm_matmul(h, gamma, wf, bias=bf, act="log_sigmoid", out_dtype=F32)
    qa, ka = fox_bias_operands(logf.reshape(B, S, LANES))
    qkv = qkv.reshape(B, S, 2 * N_HEADS * LANES)
    o = fox_attention(qkv, qa, ka)
    return matmul_residual(o.reshape(B * S, aw), w_out.astype(BF16), h)


def fox_attention(qkv, qa, ka, *, tq=1024, tk=1024, heads_per_step=2):
    B, S, _ = qkv.shape
    hs = heads_per_step
    wq = hs * LANES
    n_qblk = N_HEADS // hs
    return pl.pallas_call(
        functools.partial(_fox_attn_kernel, tq=tq, tk=tk, heads_per_step=hs),
        grid=(B, n_qblk, S // tq),
        in_specs=[pl.BlockSpec((1, tq, wq), lambda b, h, i: (b, i, h)),
                  pl.BlockSpec((1, tq, wq), lambda b, h, i: (b, i, h)),
                  pl.BlockSpec((1, S, wq), lambda b, h, i: (b, 0, n_qblk + h)),
                  pl.BlockSpec((1, S, wq), lambda b, h, i: (b, 0, h))],
        out_specs=pl.BlockSpec((1, tq, hs * HEAD_DIM), lambda b, h, i: (b, i, h)),
        out_shape=jax.ShapeDtypeStruct((B, S, ATTN_WIDTH), BF16),
        compiler_params=_cparams(("parallel", "parallel", "arbitrary")),
    )(qkv, qa, qkv, ka)


def _rot_half_cols(w):
    D = w.shape[0]
    a = w.reshape(D, -1, HEAD_DIM)
    half = HEAD_DIM // 2
    return jnp.concatenate([-a[..., half:], a[..., :half]], axis=-1).reshape(w.shape)


def _rope_tables(S):
    half = HEAD_DIM // 2
    inv_freq = ROPE_THETA ** (-jnp.arange(half, dtype=F32) / half)
    ang = jnp.arange(S, dtype=F32)[:, None] * inv_freq[None, :]
    c, s = jnp.cos(ang), jnp.sin(ang)
    c2, s2 = jnp.concatenate([c, c], axis=1), jnp.concatenate([s, s], axis=1)
    cos = jnp.stack([jnp.concatenate([c2, jnp.ones_like(c2)], axis=1), jnp.concatenate([c2, c2], axis=1)])
    sin = jnp.stack([jnp.concatenate([s2, jnp.zeros_like(s2)], axis=1), jnp.concatenate([s2, s2], axis=1)])
    return cos, sin


def _compress_kernel(x_ref, pea_ref, peb_ref, wa_ref, wb_ref, w2_ref, o_ref, pa_ref, pb0_ref, *, n_rows):
    u = pl.program_id(1)
    x = x_ref[0].astype(F32)
    pa = jnp.dot((x + pea_ref[...]).astype(BF16), wa_ref[...], preferred_element_type=F32)
    pb = jnp.dot((x + peb_ref[...]).astype(BF16), wb_ref[...], preferred_element_type=F32)

    def emit(slab, hid):
        y = jnp.dot(_gelu_tanh(hid).astype(BF16), w2_ref[...], preferred_element_type=F32)
        o_ref[0, pl.ds(pl.multiple_of(slab * n_rows, n_rows), n_rows), :] = y

    @pl.when(u == 0)
    def _():
        pb0_ref[...] = pb

    @pl.when(u > 0)
    def _():
        emit(u - 1, pa_ref[...] + pb)

    @pl.when(u == 3)
    def _():
        emit(3, pa + pltpu.roll(pb0_ref[...], n_rows - 1, axis=0))

    pa_ref[...] = pa


def nsa_compress(src, pe, w1, w2):
    B, S, W = src.shape
    G = NSA_GROUPS
    n_rows = S // 64
    half = CMP_BLOCK // 2
    cw = half * W
    xv = src.reshape(B, n_rows, 4 * cw)
    pe_flat = jnp.transpose(pe, (1, 0, 2)).reshape(CMP_BLOCK, W).astype(F32)
    pea, peb = pe_flat[:half].reshape(1, cw), pe_flat[half:].reshape(1, cw)
    eye = jnp.eye(G, dtype=F32)
    wfull = jnp.einsum('gldh,gk->lkdgh', w1.astype(F32), eye).reshape(CMP_BLOCK, W, G * CMP_HIDDEN)
    wa = wfull[:half].reshape(cw, G * CMP_HIDDEN).astype(BF16)
    wb = wfull[half:].reshape(cw, G * CMP_HIDDEN).astype(BF16)
    w2bd = jnp.einsum('ghd,gk->ghkd', w2.astype(F32), eye).reshape(G * CMP_HIDDEN, W).astype(BF16)
    const = lambda a: pl.BlockSpec(a.shape, lambda b, u: (0,) * a.ndim)
    return pl.pallas_call(
        functools.partial(_compress_kernel, n_rows=n_rows),
        grid=(B, 4),
        in_specs=[pl.BlockSpec((1, n_rows, cw), lambda b, u: (b, 0, u)),
                  const(pea), const(peb), const(wa), const(wb), const(w2bd)],
        out_specs=pl.BlockSpec((1, 4 * n_rows, W), lambda b, u: (b, 0, 0)),
        out_shape=jax.ShapeDtypeStruct((B, 4 * n_rows, W), F32),
        scratch_shapes=[pltpu.VMEM((n_rows, G * CMP_HIDDEN), F32), pltpu.VMEM((n_rows, G * CMP_HIDDEN), F32)],
        compiler_params=_cparams(("parallel", "arbitrary")),
    )(xv, pea, peb, wa, wb, w2bd)


def _nsa_attn_kernel(q_ref, kvs_ref, kvw_ref, kvc_ref, kvct_ref, gate_ref, bmat_ref, pmat_ref, o_ref,
                     *, tq, tk, seq):
    R = NSA_Q_PER_GROUP
    M = R * tq
    n_slc = seq // SLC_BLOCK
    n_cmp = 4 * n_slc
    n_sel = min(SLC_TOPK, n_slc)
    blocks_per_chunk = tk // SLC_BLOCK
    assert tq & (tq - 1) == 0 and n_slc & (n_slc - 1) == 0 and tk % tq == 0
    log_slc = n_slc.bit_length() - 1
    qi = pl.program_id(2)
    t0 = qi * tq

    qs = jnp.concatenate([q_ref[0, :, r * LANES:(r + 1) * LANES] for r in range(R)], axis=0)

    kvc = kvc_ref[0, 0]
    sT = lax.dot_general(kvc, qs, _NT, preferred_element_type=F32)
    rowc = lax.broadcasted_iota(jnp.int32, (n_cmp, 1), 0)
    cmp_end = (rowc & (n_slc - 1)) * SLC_BLOCK + (rowc >> log_slc) * CMP_STRIDE + (CMP_BLOCK - 1)
    tcol = t0 + (lax.broadcasted_iota(jnp.int32, (1, M), 1) & (tq - 1))
    sm = jnp.where(cmp_end <= tcol, sT, NEG_INF)
    mx = jnp.max(sm, axis=0, keepdims=True)
    e = jnp.exp2(sm - mx)
    inv = jnp.where(mx > 0.5 * NEG_INF, 1.0 / jnp.sum(e, axis=0, keepdims=True), 0.0)
    pT = e * inv
    o_c = jnp.dot(kvct_ref[0, 0], pT.astype(BF16), preferred_element_type=F32).T

    psum = pT[:, 0:tq]
    for r in range(1, R):
        psum = psum + pT[:, r * tq:(r + 1) * tq]
    p0, p1, p2, p3 = (psum[u * n_slc:(u + 1) * n_slc, :] for u in range(4))
    jrow = lax.broadcasted_iota(jnp.int32, (n_slc, tq), 0)
    p3_prev = jnp.where(jrow == 0, 0.0, pltpu.roll(p3, 1, axis=0))
    imp = p0 + p1 + p2 + 0.5 * p3 + 0.5 * p3_prev
    cur = (t0 + lax.broadcasted_iota(jnp.int32, (n_slc, tq), 1)) >> (SLC_BLOCK.bit_length() - 1)
    forced = (jrow == 0) | (jrow == cur) | (jrow == cur - 1)
    vals = jnp.where(forced, -jnp.inf, jnp.where(jrow <= cur, imp, -jnp.inf))
    sel = jnp.where(forced, 1.0, 0.0)
    for _ in range(n_sel - 3):
        m = jnp.max(vals, axis=0, keepdims=True)
        idx = jnp.min(jnp.where(vals == m, jrow, n_slc), axis=0, keepdims=True)
        hit = jrow == idx
        sel = jnp.where(hit, 1.0, sel)
        vals = jnp.where(hit, -jnp.inf, vals)
    sel_bias = ((sel.T - 1.0) * (-NEG_INF)).astype(BF16)

    n_parts = 2
    hp = R // n_parts
    mp = hp * tq
    qparts = [qs[i * mp:(i + 1) * mp] for i in range(n_parts)]
    qrow = t0 + (lax.broadcasted_iota(jnp.int32, (mp, 1), 0) & (tq - 1))
    kcol = lax.broadcasted_iota(jnp.int32, (1, tk), 1)
    n_chunks = seq // tk
    j_last = t0 // tk

    klane = lax.broadcasted_iota(jnp.int32, (tk, LANES), 1)

    def slc_step(j, carry, masked):
        kv = kvs_ref[0, pl.ds(pl.multiple_of(j * tk, tk), tk), :]
        k_sel = jnp.where(klane < HEAD_DIM, kv, bmat_ref[...]).astype(BF16)
        ones_v = jnp.where(klane < HEAD_DIM, 1.0, kv).astype(BF16)
        off = pl.multiple_of(blocks_per_chunk * (n_chunks - 1 - j), blocks_per_chunk)
        place = pmat_ref[pl.ds(off, n_slc), :].astype(BF16)
        q_bias = jnp.dot(sel_bias, place, preferred_element_type=F32).astype(BF16)
        q_bias = jnp.concatenate([q_bias] * hp, axis=0)
        new = []
        for qp, (m, acc) in zip(qparts, carry):
            s = lax.dot_general(qp + q_bias, k_sel, _NT, preferred_element_type=F32)
            if masked:
                s = jnp.where(j * tk + kcol <= qrow, s, NEG_INF)
            m_new = jnp.maximum(m, jnp.max(s, axis=1, keepdims=True))
            p = jnp.exp2(s - m_new).astype(BF16)
            acc = jnp.exp2(m - m_new) * acc + jnp.dot(p, ones_v, preferred_element_type=F32)
            new.append((m_new, acc))
        return tuple(new)

    init = (jnp.full((mp, 1), NEG_INF, F32), jnp.zeros((mp, LANES), F32))
    n_pairs = j_last // 2
    carry = lax.fori_loop(0, n_pairs, lambda i, c: slc_step(2 * i + 1, slc_step(2 * i, c, False), False),
                          (init,) * n_parts)
    carry = lax.fori_loop(2 * n_pairs, j_last, functools.partial(slc_step, masked=False), carry)
    carry = slc_step(j_last, carry, True)
    o_s = jnp.concatenate([acc / acc[:, 0:1] for (_, acc) in carry], axis=0)

    wlen = WINDOW + tq
    start = jnp.maximum(t0 - WINDOW, 0)
    kvw = kvw_ref[0, pl.ds(pl.multiple_of(start, tq), wlen), :]
    s_w = lax.dot_general(qs, kvw, _NT, preferred_element_type=F32)
    qpos = t0 + (lax.broadcasted_iota(jnp.int32, (M, 1), 0) & (tq - 1))
    kpos = start + lax.broadcasted_iota(jnp.int32, (1, wlen), 1)
    s_w = jnp.where(kpos <= qpos, jnp.where(kpos > qpos - WINDOW, s_w, NEG_INF), NEG_INF)
    p_w = jnp.exp2(s_w - jnp.max(s_w, axis=1, keepdims=True)).astype(BF16)
    wlane = lax.broadcasted_iota(jnp.int32, (wlen, LANES), 1)
    acc_w = jnp.dot(p_w, jnp.where(wlane < HEAD_DIM, 1.0, kvw).astype(BF16), preferred_element_type=F32)
    o_w = acc_w / acc_w[:, 0:1]

    gates = gate_ref[0]
    lane = lax.broadcasted_iota(jnp.int32, (tq, LANES), 1)
    comb = []
    for r in range(R):
        rs = slice(r * tq, (r + 1) * tq)
        comb.append(gates[:, r:r + 1] * o_c[rs] + gates[:, R + r:R + r + 1] * o_s[rs]
                    + gates[:, 2 * R + r:2 * R + r + 1] * o_w[rs])
    out = [jnp.where(lane < HEAD_DIM, pltpu.roll(comb[2 * i], HEAD_DIM, axis=1), comb[2 * i + 1])
           for i in range(R // 2)]
    o_ref[0] = jnp.concatenate(out, axis=1).astype(o_ref.dtype)


def nsa_attention(qkv, kvc, kvct, gates, *, tq=256, tk=1024):
    B, S, _ = qkv.shape
    G, R = NSA_GROUPS, NSA_Q_PER_GROUP
    tk = min(tk, S)
    n_slc = S // SLC_BLOCK
    n_cmp = kvc.shape[2]
    bpc = tk // SLC_BLOCK
    assert bpc <= LANES - HEAD_DIM
    off = bpc * (S // tk - 1)
    lane = np.arange(LANES)[None, :]
    bmat = jnp.asarray(lane - HEAD_DIM == np.arange(tk)[:, None] // SLC_BLOCK, BF16)
    pmat = jnp.asarray((np.arange(n_slc + off)[:, None] - off == lane - HEAD_DIM) & (lane >= HEAD_DIM)
                       & (lane < HEAD_DIM + bpc), F32)
    slc_blk0, win_blk0 = N_HEADS, N_HEADS + G
    return pl.pallas_call(
        functools.partial(_nsa_attn_kernel, tq=tq, tk=tk, seq=S),
        grid=(B, G, S // tq),
        in_specs=[pl.BlockSpec((1, tq, R * LANES), lambda b, g, i: (b, i, g)),
                  pl.BlockSpec((1, S, LANES), lambda b, g, i: (b, 0, slc_blk0 + g)),
                  pl.BlockSpec((1, S, LANES), lambda b, g, i: (b, 0, win_blk0 + g)),
                  pl.BlockSpec((1, 1, n_cmp, LANES), lambda b, g, i: (b, g, 0, 0)),
                  pl.BlockSpec((1, 1, LANES, n_cmp), lambda b, g, i: (b, g, 0, 0)),
                  pl.BlockSpec((1, tq, LANES), lambda b, g, i: (b, i, g)),
                  pl.BlockSpec(bmat.shape, lambda b, g, i: (0, 0)),
                  pl.BlockSpec(pmat.shape, lambda b, g, i: (0, 0))],
        out_specs=pl.BlockSpec((1, tq, R * HEAD_DIM), lambda b, g, i: (b, i, g)),
        out_shape=jax.ShapeDtypeStruct((B, S, ATTN_WIDTH), BF16),
        compiler_params=_cparams(("parallel", "parallel", "arbitrary")),
    )(qkv, qkv, qkv, kvc, kvct, gates, bmat, pmat)


def nsa_layer(h, gamma, w_in, pe_k, w1_k, w2_k, pe_v, w1_v, w2_v, w_out, B, S):
    G, R, hd, aw = NSA_GROUPS, NSA_Q_PER_GROUP, HEAD_DIM, ATTN_WIDTH
    kvd = G * hd
    sec = lambda i: w_in[:, aw + i * kvd: aw + (i + 1) * kvd]
    wq = w_in[:, :aw] * (hd ** -0.5 * LOG2E)
    wa = jnp.concatenate([_head_slots(wq, N_HEADS), _head_slots(sec(2), G, sec(3)),
                          _head_slots(sec(4), G, sec(5))], axis=1).astype(BF16)
    wb = jnp.concatenate([_head_slots(_rot_half_cols(wq), N_HEADS), _head_slots(_rot_half_cols(sec(2)), G),
                          _head_slots(_rot_half_cols(sec(4)), G)], axis=1).astype(BF16)
    cos, sin = _rope_tables(S)
    qkv = norm_matmul(h, gamma, wa, wb=wb, cos=cos, sin=sin)
    kc_src = norm_matmul(h, gamma, sec(0).astype(BF16), wb=_rot_half_cols(sec(0)).astype(BF16),
                         cos=cos, sin=sin, table_of_tile=lambda j: 1)
    vc_src = norm_matmul(h, gamma, sec(1).astype(BF16))
    wg = w_in[:, aw + 6 * kvd:].reshape(-1, 3, G, R)
    wg = jnp.transpose(wg, (0, 2, 1, 3)).reshape(-1, G, 3 * R)
    wg = jnp.pad(wg, ((0, 0), (0, 0), (0, LANES - 3 * R))).reshape(-1, G * LANES).astype(BF16)
    gates = norm_matmul(h, gamma, wg, act="sigmoid", out_dtype=F32)
    kc = nsa_compress(kc_src.reshape(B, S, kvd), pe_k, w1_k, w2_k)
    vc = nsa_compress(vc_src.reshape(B, S, kvd), pe_v, w1_v, w2_v)
    n_cmp = kc.shape[1]
    kvc = jnp.concatenate([kc.reshape(B, n_cmp, G, hd), vc.reshape(B, n_cmp, G, hd)], axis=-1)
    kvc = jnp.transpose(kvc, (0, 2, 1, 3)).astype(BF16)
    kvct = jnp.swapaxes(kvc, 2, 3)
    o = nsa_attention(qkv.reshape(B, S, -1), kvc, kvct, gates.reshape(B, S, G * LANES))
    return matmul_residual(o.reshape(B * S, aw), w_out.astype(BF16), h)


def kernel(x, l0_attn_norm, l0_w_in, l0_cmp_pe_k, l0_cmp_w1_k, l0_cmp_w2_k, l0_cmp_pe_v, l0_cmp_w1_v,
           l0_cmp_w2_v, l0_w_out, l0_ffn_norm, l0_peer_wq, l0_peer_keys, l0_peer_u, l0_peer_v,
           l1_attn_norm, l1_w_in, l1_f_bias, l1_w_out, l1_ffn_norm, l1_peer_wq, l1_peer_keys, l1_peer_u,
           l1_peer_v, final_norm):
    B, S, D = x.shape
    h = x.reshape(B * S, D)
    h = nsa_layer(h, l0_attn_norm, l0_w_in, l0_cmp_pe_k, l0_cmp_w1_k, l0_cmp_w2_k, l0_cmp_pe_v, l0_cmp_w1_v,
                  l0_cmp_w2_v, l0_w_out, B, S)
    h = peer_layer(h, l0_ffn_norm, l0_peer_wq, l0_peer_keys, l0_peer_u, l0_peer_v)
    h = fox_layer(h, l1_attn_norm, l1_w_in, l1_f_bias, l1_w_out, B, S)
    h = peer_layer(h, l1_ffn_norm, l1_peer_wq, l1_peer_keys, l1_peer_u, l1_peer_v)
    return rmsnorm(h, final_norm).reshape(B, S, D)
```

```python
import functools

import numpy as np
import jax
import jax.numpy as jnp
from jax import lax
from jax.experimental import pallas as pl
from jax.experimental.pallas import tpu as pltpu

F32 = jnp.float32
BF16 = jnp.bfloat16

D_MODEL = 1024
N_HEADS = 16
HEAD_DIM = 64
ATTN_WIDTH = N_HEADS * HEAD_DIM
NSA_GROUPS = 4
NSA_Q_PER_GROUP = N_HEADS // NSA_GROUPS
CMP_BLOCK = 32
CMP_STRIDE = 16
CMP_HIDDEN = 2 * HEAD_DIM
SLC_BLOCK = 64
SLC_TOPK = 16
WINDOW = 512
FORCE_SCORE = 1.0e4
ROPE_THETA = 10000.0
PEER_HEADS = 8
PEER_N_KEYS = 128
PEER_TOPK = 16
PEER_HALF_DIM = 128
RMS_EPS = 1e-6
NEG_INF = -1e30
LOG2E = 1.4426950408889634

LANES = 128
VMEM_LIMIT_BYTES = 56 * 1024 * 1024

_NT = (((1,), (1,)), ((), ()))


def _cparams(sem, vmem=VMEM_LIMIT_BYTES, flags=None):
    return pltpu.CompilerParams(dimension_semantics=sem, vmem_limit_bytes=vmem, flags=flags)


def _gelu_tanh(x):
    return 0.5 * x * (1.0 + jnp.tanh(0.7978845608028654 * (x + 0.044715 * (x * x * x))))


def _gelu_sigmoid(x):
    c = -2.0 * 0.7978845608028654 * LOG2E
    t = x * (c + (c * 0.044715) * (x * x))
    return x / (1.0 + jnp.exp2(t))


def _rms_rows(x, g):
    ms = jnp.mean(x * x, axis=-1, keepdims=True)
    return x * lax.rsqrt(ms + RMS_EPS) * g


def _norm_mm_kernel(*refs, act, has_bias, rope, emit_xn):
    it = iter(refs)
    x_ref, g_ref = next(it), next(it)
    wa_ref = next(it)
    wb_ref = next(it) if rope else None
    cos_ref = next(it) if rope else None
    sin_ref = next(it) if rope else None
    b_ref = next(it) if has_bias else None
    o_ref = next(it)
    xo_ref = next(it) if emit_xn else None
    xn_ref = next(it)

    @pl.when(pl.program_id(1) == 0)
    def _():
        xn = _rms_rows(x_ref[...], g_ref[...])
        xn_ref[...] = xn.astype(BF16)
        if emit_xn:
            xo_ref[...] = xn.T.astype(BF16)

    xn = xn_ref[...]
    y = jnp.dot(xn, wa_ref[...], preferred_element_type=F32)
    if rope:
        yb = jnp.dot(xn, wb_ref[...], preferred_element_type=F32)
        cos, sin = cos_ref[...], sin_ref[...]
        for s in range(y.shape[1] // LANES):
            sl = slice(s * LANES, (s + 1) * LANES)
            o_ref[:, sl] = (y[:, sl] * cos + yb[:, sl] * sin).astype(o_ref.dtype)
        return
    if has_bias:
        y = y + b_ref[...]
    if act == "sigmoid":
        y = jax.nn.sigmoid(y)
    elif act == "log_sigmoid":
        y = jax.nn.log_sigmoid(y)
    o_ref[...] = y.astype(o_ref.dtype)


def norm_matmul(x, gamma, wa, *, wb=None, cos=None, sin=None, table_of_tile=None, bias=None,
                act=None, out_dtype=BF16, emit_xn=False, tm=1024, tn=1024):
    T, D = x.shape
    N = wa.shape[1]
    tm, tn = min(tm, T), min(tn, N)
    assert T % tm == 0 and N % tn == 0 and tn % LANES == 0
    rope = wb is not None
    in_specs = [pl.BlockSpec((tm, D), lambda i, j: (i, 0)),
                pl.BlockSpec((1, D), lambda i, j: (0, 0)),
                pl.BlockSpec((D, tn), lambda i, j: (0, j))]
    args = [x, gamma.reshape(1, D).astype(F32), wa]
    if rope:
        S = cos.shape[1]
        assert S % tm == 0
        n_pos = S // tm
        tmap = table_of_tile if table_of_tile is not None else (lambda j: 0)
        in_specs += [pl.BlockSpec((D, tn), lambda i, j: (0, j)),
                     pl.BlockSpec((None, tm, LANES), lambda i, j: (tmap(j), i % n_pos, 0)),
                     pl.BlockSpec((None, tm, LANES), lambda i, j: (tmap(j), i % n_pos, 0))]
        args += [wb, cos, sin]
    if bias is not None:
        in_specs.append(pl.BlockSpec((1, tn), lambda i, j: (0, j)))
        args.append(bias.reshape(1, N).astype(F32))
    out_shape = [jax.ShapeDtypeStruct((T, N), out_dtype)]
    out_specs = [pl.BlockSpec((tm, tn), lambda i, j: (i, j))]
    if emit_xn:
        out_shape.append(jax.ShapeDtypeStruct((D, T), BF16))
        out_specs.append(pl.BlockSpec((D, tm), lambda i, j: (0, i)))
    res = pl.pallas_call(
        functools.partial(_norm_mm_kernel, act=act, has_bias=bias is not None, rope=rope, emit_xn=emit_xn),
        grid=(T // tm, N // tn),
        in_specs=in_specs,
        out_specs=out_specs,
        out_shape=out_shape,
        scratch_shapes=[pltpu.VMEM((tm, D), BF16)],
        compiler_params=_cparams(("parallel", "arbitrary")),
    )(*args)
    return res if emit_xn else res[0]


def _mm_res_kernel(a_ref, w_ref, r_ref, o_ref):
    o_ref[...] = r_ref[...] + jnp.dot(a_ref[...], w_ref[...], preferred_element_type=F32)


def matmul_residual(a, w, res, *, tm=1024, tn=1024):
    T, K = a.shape
    N = w.shape[1]
    tm, tn = min(tm, T), min(tn, N)
    assert T % tm == 0 and N % tn == 0
    return pl.pallas_call(
        _mm_res_kernel,
        grid=(T // tm, N // tn),
        in_specs=[pl.BlockSpec((tm, K), lambda i, j: (i, 0)),
                  pl.BlockSpec((K, tn), lambda i, j: (0, j)),
                  pl.BlockSpec((tm, tn), lambda i, j: (i, j))],
        out_specs=pl.BlockSpec((tm, tn), lambda i, j: (i, j)),
        out_shape=jax.ShapeDtypeStruct((T, N), F32),
        compiler_params=_cparams(("parallel", "arbitrary")),
    )(a, w, res)


def _rmsnorm_kernel(x_ref, g_ref, o_ref):
    o_ref[...] = _rms_rows(x_ref[...], g_ref[...])


def rmsnorm(x, gamma, *, tm=1024):
    T, D = x.shape
    tm = min(tm, T)
    return pl.pallas_call(
        _rmsnorm_kernel,
        grid=(T // tm,),
        in_specs=[pl.BlockSpec((tm, D), lambda i: (i, 0)), pl.BlockSpec((1, D), lambda i: (0, 0))],
        out_specs=pl.BlockSpec((tm, D), lambda i: (i, 0)),
        out_shape=jax.ShapeDtypeStruct((T, D), F32),
        compiler_params=_cparams(("parallel",)),
    )(x, gamma.reshape(1, D).astype(F32))


def _peer_cand_tables(tn):
    fidx, vmask = [], []
    for k2 in range(16):
        fidx.append(k2); vmask.append(0.0)
    for k1 in range(1, 8):
        lim = PEER_TOPK // (k1 + 1)
        for k2 in range(8):
            fidx.append(k1 * 16 + k2); vmask.append(0.0 if k2 < lim else -np.inf)
    for k1 in range(8, 16):
        fidx.append(k1 * 16); vmask.append(0.0)
    fidx = np.broadcast_to(np.asarray(fidx, np.int32)[:, None], (80, tn))
    vmask = np.broadcast_to(np.asarray(vmask, np.float32)[:, None], (80, tn))
    return jnp.asarray(fidx), jnp.asarray(vmask)


def _top16_rows(s, exact_ties):
    n, tn = s.shape
    rows = lax.broadcasted_iota(jnp.int32, (n, tn), 0)
    rows16 = lax.broadcasted_iota(jnp.int32, (PEER_TOPK, tn), 0)
    rank = jnp.full((n, tn), float(PEER_TOPK), F32)
    tops = jnp.zeros((PEER_TOPK, tn), F32)
    v = s
    for k in range(PEER_TOPK):
        m = jnp.max(v, axis=0, keepdims=True)
        if exact_ties:
            hit = rows == jnp.min(jnp.where(v == m, rows, n), axis=0, keepdims=True)
        else:
            hit = v == m
        rank = jnp.where(hit, float(k), rank)
        v = jnp.where(hit, -jnp.inf, v)
        tops = jnp.where(rows16 == k, m, tops)
    return tops, rank, v


def _peer_select_head(q_ref, keys_ref, fidx, vmask, exact_ties):
    tops, ranks, es, picked = [], [], [], []
    for p in range(2):
        q = q_ref[:, p * PEER_HALF_DIM:(p + 1) * PEER_HALF_DIM]
        s = lax.dot_general(keys_ref[p], q, _NT, preferred_element_type=F32)
        t, r, v = _top16_rows(s, exact_ties)
        tops.append(t); ranks.append(r)
        es.append(jnp.exp(s - t[0:1, :]))
        picked.append(jnp.sum(jnp.where(v == -jnp.inf, 1.0, 0.0), axis=0, keepdims=True))
    ts1, ts2 = tops
    pieces = [ts1[0:1, :] + ts2]
    for k1 in range(1, 8):
        pieces.append(ts1[k1:k1 + 1, :] + ts2[0:8, :])
    pieces.append(ts1[8:16, :] + ts2[0:1, :])
    cand0 = jnp.concatenate(pieces, axis=0) + vmask
    cand = cand0
    for _ in range(PEER_TOPK):
        m = jnp.max(cand, axis=0, keepdims=True)
        if exact_ties:
            hit = fidx == jnp.min(jnp.where(cand == m, fidx, 4096), axis=0, keepdims=True)
        else:
            hit = cand == m
        cand = jnp.where(hit, -jnp.inf, cand)
    taken = jnp.logical_and(cand == -jnp.inf, vmask == 0.0)
    takenf = taken.astype(F32)
    picked.append(jnp.sum(takenf, axis=0, keepdims=True))
    unique = jnp.min(jnp.where((picked[0] == PEER_TOPK) & (picked[1] == PEER_TOPK) & (picked[2] == PEER_TOPK),
                               1.0, 0.0)) > 0.5
    best = ts1[0:1, :] + ts2[0:1, :]
    z = jnp.sum(jnp.where(taken, jnp.exp(cand0 - best), 0.0), axis=0, keepdims=True)
    counts = [jnp.sum(takenf[0:16, :], axis=0, keepdims=True)]
    for k1 in range(1, 8):
        counts.append(jnp.sum(takenf[16 + 8 * (k1 - 1):16 + 8 * k1, :], axis=0, keepdims=True))
    tail = takenf[72:80, :]
    cnt = jnp.zeros_like(ranks[0])
    for k1 in range(PEER_TOPK):
        nk = counts[k1] if k1 < 8 else tail[k1 - 8:k1 - 7, :]
        cnt = jnp.where(ranks[0] == float(k1), nk, cnt)
    return (cnt, ranks[1], es[0], es[1] / z), unique


def _peer_select_kernel(q_ref, keys_ref, fidx_ref, vmask_ref, cnt_ref, rank2_ref, e1_ref, e2_ref):
    fidx = fidx_ref[...]
    vmask = vmask_ref[...]

    def store(vals):
        for ref, val in zip((cnt_ref, rank2_ref, e1_ref, e2_ref), vals):
            ref[...] = val.astype(ref.dtype)

    vals, unique = _peer_select_head(q_ref, keys_ref, fidx, vmask, exact_ties=False)
    store(vals)

    @pl.when(jnp.logical_not(unique))
    def _():
        store(_peer_select_head(q_ref, keys_ref, fidx, vmask, exact_ties=True)[0])


def peer_select(q, keys, *, tn=512):
    T = q.shape[0]
    fidx, vmask = _peer_cand_tables(tn)
    rows = PEER_HEADS * PEER_N_KEYS
    ospec = pl.BlockSpec((PEER_N_KEYS, tn), lambda i, h: (h, i))
    return pl.pallas_call(
        _peer_select_kernel,
        grid=(T // tn, PEER_HEADS),
        in_specs=[pl.BlockSpec((tn, 2 * PEER_HALF_DIM), lambda i, h: (i, h)),
                  pl.BlockSpec((2, PEER_N_KEYS, PEER_HALF_DIM), lambda i, h: (h, 0, 0)),
                  pl.BlockSpec((80, tn), lambda i, h: (0, 0)),
                  pl.BlockSpec((80, tn), lambda i, h: (0, 0))],
        out_specs=[ospec] * 4,
        out_shape=[jax.ShapeDtypeStruct((rows, T), dt) for dt in (F32, BF16, F32, BF16)],
        compiler_params=_cparams(("parallel", "parallel")),
    )(q, keys, fidx, vmask)


def _peer_dense_kernel(xn_ref, u_ref, vt_ref, cnt_ref, rank2_ref, e1_ref, e2_ref, res_ref, o_ref,
                       acc_ref, g0_ref, g1_ref, *, c_per_step):
    j = pl.program_id(1)
    n_tiles = pl.num_programs(1) - 1

    @pl.when(j == 0)
    def _():
        acc_ref[...] = jnp.zeros_like(acc_ref)
        g1_ref[...] = jnp.zeros_like(g1_ref)

    @pl.when(j % 2 == 0)
    def _():
        _peer_dense_step(xn_ref, u_ref, vt_ref, cnt_ref, rank2_ref, e1_ref, e2_ref, acc_ref,
                         g1_ref, g0_ref, j, c_per_step)

    @pl.when(j % 2 == 1)
    def _():
        _peer_dense_step(xn_ref, u_ref, vt_ref, cnt_ref, rank2_ref, e1_ref, e2_ref, acc_ref,
                         g0_ref, g1_ref, j, c_per_step)

    @pl.when(j == n_tiles)
    def _():
        o_ref[...] = res_ref[...] + acc_ref[...].T


def _peer_dense_step(xn_ref, u_ref, vt_ref, cnt_ref, rank2_ref, e1_ref, e2_ref, acc_ref, g_ref, g_next_ref,
                     j, c_per_step):

    tn = xn_ref.shape[1]
    bf16_rows = 16
    reps = PEER_N_KEYS // bf16_rows

    def row_tile(ref, row):
        r16 = jnp.broadcast_to(ref[pl.ds(row, 1), :], (bf16_rows, tn)).astype(BF16)
        return jnp.concatenate([r16] * reps, axis=0)

    c0 = jnp.maximum(j - 1, 0) * c_per_step
    up_rows = 2 * PEER_N_KEYS
    blocks = []
    for cc in range(c_per_step):
        if (cc * PEER_N_KEYS) % up_rows == 0:
            rs = slice(cc * PEER_N_KEYS, cc * PEER_N_KEYS + up_rows)
            hT = jnp.dot(u_ref[rs, :], xn_ref[...], preferred_element_type=F32)
            g_next_ref[rs, :] = _gelu_sigmoid(hT).astype(BF16)
        c = c0 + cc
        w = None
        for h in range(PEER_HEADS):
            row = h * PEER_N_KEYS + c
            n_row = row_tile(cnt_ref, row)
            e1_row = row_tile(e1_ref, row)
            sl = slice(h * PEER_N_KEYS, (h + 1) * PEER_N_KEYS)
            term = jnp.where(rank2_ref[sl, :] < n_row, e2_ref[sl, :], 0.0) * e1_row
            w = term if w is None else w + term
        blocks.append(w * g_ref[cc * PEER_N_KEYS:(cc + 1) * PEER_N_KEYS, :])
    aT = jnp.concatenate(blocks, axis=0)
    acc_ref[...] += jnp.dot(vt_ref[...], aT, preferred_element_type=F32)


def peer_dense(xn, u, vt, cnt, rank2, e1, e2, res, *, tn=512, te=1024):
    D, T = xn.shape
    E = u.shape[0]
    rows = PEER_HEADS * PEER_N_KEYS
    sel_spec = pl.BlockSpec((rows, tn), lambda i, j: (0, i))
    n_tiles = E // te
    return pl.pallas_call(
        functools.partial(_peer_dense_kernel, c_per_step=te // PEER_N_KEYS),
        grid=(T // tn, n_tiles + 1),
        in_specs=[pl.BlockSpec((D, tn), lambda i, j: (0, i)),
                  pl.BlockSpec((te, D), lambda i, j: (jnp.minimum(j, n_tiles - 1), 0)),
                  pl.BlockSpec((D, te), lambda i, j: (0, jnp.maximum(j - 1, 0))),
                  sel_spec, sel_spec, sel_spec, sel_spec,
                  pl.BlockSpec((tn, D), lambda i, j: (i, 0))],
        out_specs=pl.BlockSpec((tn, D), lambda i, j: (i, 0)),
        out_shape=jax.ShapeDtypeStruct((T, D), F32),
        scratch_shapes=[pltpu.VMEM((D, tn), F32), pltpu.VMEM((te, tn), BF16), pltpu.VMEM((te, tn), BF16)],
        compiler_params=_cparams(("parallel", "arbitrary")),
    )(xn, u, vt, cnt, rank2, e1, e2, res)


def peer_layer(h, gamma, w_q, sub_keys, u, v):
    q, xn = norm_matmul(h, gamma, w_q.astype(BF16), emit_xn=True)
    keys = sub_keys.reshape(2 * PEER_HEADS, PEER_N_KEYS, PEER_HALF_DIM).astype(BF16)
    cnt, rank2, e1, e2 = peer_select(q, keys)
    return peer_dense(xn, u.astype(BF16), v.T.astype(BF16), cnt, rank2, e1, e2, h)


def _cumsum_aug_kernel(lf_ref, tri_ref, place_q_ref, place_k_ref, ones_q_ref, ones_k_ref,
                       qa_ref, ka_ref, carry_ref):
    @pl.when(pl.program_id(1) == 0)
    def _():
        carry_ref[...] = jnp.zeros_like(carry_ref)

    lf = lf_ref[0]
    c = jnp.dot(tri_ref[...], lf, preferred_element_type=F32, precision=lax.Precision.HIGHEST) + carry_ref[...]
    carry_ref[...] = c[-1:, :]
    c = c * LOG2E
    hi = c.astype(BF16)
    r1 = c - hi.astype(F32)
    mid = r1.astype(BF16)
    lo = (r1 - mid.astype(F32)).astype(BF16)
    nh = N_HEADS
    lane = lax.broadcasted_iota(jnp.int32, c.shape, 1)
    parts = jnp.where(lane < nh, hi.astype(F32),
                      jnp.where(lane < 2 * nh, pltpu.roll(mid.astype(F32), nh, axis=1),
                                pltpu.roll(lo.astype(F32), 2 * nh, axis=1)))
    parts = jnp.where(lane < 3 * nh, parts, 0.0).astype(BF16)
    qa_ref[0] = (jnp.dot(parts, place_q_ref[...], preferred_element_type=F32) + ones_q_ref[...]).astype(BF16)
    ka_ref[0] = (jnp.dot(parts, place_k_ref[...], preferred_element_type=F32) + ones_k_ref[...]).astype(BF16)


def fox_bias_operands(logf, *, tc=512):
    B, S, _ = logf.shape
    nh = N_HEADS
    tri = jnp.asarray(np.tril(np.ones((tc, tc), np.float32)))
    pq = np.zeros((LANES, nh * LANES), np.float32)
    pk = np.zeros((LANES, nh * LANES), np.float32)
    oq = np.zeros((1, nh * LANES), np.float32)
    ok = np.zeros((1, nh * LANES), np.float32)
    for h in range(nh):
        for part in range(3):
            pq[part * nh + h, h * LANES + part] = 1.0
            pk[part * nh + h, h * LANES + 3 + part] = -1.0
            oq[0, h * LANES + 3 + part] = 1.0
            ok[0, h * LANES + part] = 1.0
    const = lambda a: pl.BlockSpec(a.shape, lambda b, i: (0,) * a.ndim)
    pq, pk, oq, ok = jnp.asarray(pq, BF16), jnp.asarray(pk, BF16), jnp.asarray(oq), jnp.asarray(ok)
    out = jax.ShapeDtypeStruct((B, S, nh * LANES), BF16)
    return pl.pallas_call(
        _cumsum_aug_kernel,
        grid=(B, S // tc),
        in_specs=[pl.BlockSpec((1, tc, LANES), lambda b, i: (b, i, 0)),
                  const(tri), const(pq), const(pk), const(oq), const(ok)],
        out_specs=[pl.BlockSpec((1, tc, nh * LANES), lambda b, i: (b, i, 0))] * 2,
        out_shape=[out, out],
        scratch_shapes=[pltpu.VMEM((1, LANES), F32)],
        compiler_params=_cparams(("parallel", "arbitrary")),
    )(logf, tri, pq, pk, oq, ok)


def _fox_attn_kernel(q_ref, qa_ref, kv_ref, ka_ref, o_ref, *, tq, tk, heads_per_step):
    qi = pl.program_id(2)
    t0 = qi * tq
    n_full = t0 // tk
    n_diag = tq // tk
    lanes = [slice(hh * LANES, (hh + 1) * LANES) for hh in range(heads_per_step)]
    qs = [jnp.concatenate([q_ref[0, :, lsl], qa_ref[0, :, lsl]], axis=1) for lsl in lanes]

    def step(j, carry, masked):
        rows = pl.ds(pl.multiple_of(j * tk, tk), tk)
        klane = lax.broadcasted_iota(jnp.int32, (tk, LANES), 1)
        new = []
        for lsl, q, (m, acc) in zip(lanes, qs, carry):
            kv = kv_ref[0, rows, lsl]
            kk = jnp.concatenate([kv, ka_ref[0, rows, lsl]], axis=1)
            ones_v = jnp.where(klane < HEAD_DIM, 1.0, kv).astype(BF16)
            s = lax.dot_general(q, kk, _NT, preferred_element_type=F32)
            if masked:
                qpos = t0 + lax.broadcasted_iota(jnp.int32, (tq, 1), 0)
                kpos = j * tk + lax.broadcasted_iota(jnp.int32, (1, tk), 1)
                s = jnp.where(kpos <= qpos, s, NEG_INF)
            m_new = jnp.maximum(m, jnp.max(s, axis=1, keepdims=True))
            p = jnp.exp2(s - m_new).astype(BF16)
            acc = jnp.exp2(m - m_new) * acc + jnp.dot(p, ones_v, preferred_element_type=F32)
            new.append((m_new, acc))
        return tuple(new)

    init = (jnp.full((tq, 1), NEG_INF, F32), jnp.zeros((tq, LANES), F32))
    carry = lax.fori_loop(0, n_full, functools.partial(step, masked=False), (init,) * heads_per_step)
    for d in range(n_diag):
        carry = step(n_full + d, carry, True)
    outs = [acc / acc[:, 0:1] for (_, acc) in carry]
    lane = lax.broadcasted_iota(jnp.int32, (tq, LANES), 1)
    blocks = []
    for pair in range(heads_per_step // 2):
        a, b = outs[2 * pair], outs[2 * pair + 1]
        blocks.append(jnp.where(lane < HEAD_DIM, pltpu.roll(a, HEAD_DIM, axis=1), b))
    o_ref[0] = jnp.concatenate(blocks, axis=1).astype(o_ref.dtype) if len(blocks) > 1 else blocks[0].astype(o_ref.dtype)


def _head_slots(w, n_heads, second=None):
    D = w.shape[0]
    a = w.reshape(D, n_heads, HEAD_DIM)
    b = jnp.zeros_like(a) if second is None else second.reshape(D, n_heads, HEAD_DIM)
    return jnp.concatenate([a, b], axis=-1).reshape(D, n_heads * LANES)


def fox_layer(h, gamma, w_in, f_bias, w_out, B, S):
    aw = ATTN_WIDTH
    wq = _head_slots(w_in[:, :aw] * (HEAD_DIM ** -0.5 * LOG2E), N_HEADS)
    wkv = _head_slots(w_in[:, aw:2 * aw], N_HEADS, w_in[:, 2 * aw:3 * aw])
    w_main = jnp.concatenate([wq, wkv], axis=1).astype(BF16)
    wf = jnp.pad(w_in[:, 3 * aw:], ((0, 0), (0, LANES - N_HEADS))).astype(BF16)
    bf = jnp.pad(f_bias.astype(F32), (0, LANES - N_HEADS))
    qkv = norm_matmul(h, gamma, w_main)
    logf = norm_matmul(h, gamma, wf, bias=bf, act="log_sigmoid", out_dtype=F32)
    qa, ka = fox_bias_operands(logf.reshape(B, S, LANES))
    qkv = qkv.reshape(B, S, 2 * N_HEADS * LANES)
    o = fox_attention(qkv, qa, ka)
    return matmul_residual(o.reshape(B * S, aw), w_out.astype(BF16), h)


def fox_attention(qkv, qa, ka, *, tq=1024, tk=1024, heads_per_step=2):
    B, S, _ = qkv.shape
    hs = heads_per_step
    wq = hs * LANES
    n_qblk = N_HEADS // hs
    return pl.pallas_call(
        functools.partial(_fox_attn_kernel, tq=tq, tk=tk, heads_per_step=hs),
        grid=(B, n_qblk, S // tq),
        in_specs=[pl.BlockSpec((1, tq, wq), lambda b, h, i: (b, i, h)),
                  pl.BlockSpec((1, tq, wq), lambda b, h, i: (b, i, h)),
                  pl.BlockSpec((1, S, wq), lambda b, h, i: (b, 0, n_qblk + h)),
                  pl.BlockSpec((1, S, wq), lambda b, h, i: (b, 0, h))],
        out_specs=pl.BlockSpec((1, tq, hs * HEAD_DIM), lambda b, h, i: (b, i, h)),
        out_shape=jax.ShapeDtypeStruct((B, S, ATTN_WIDTH), BF16),
        compiler_params=_cparams(("parallel", "parallel", "arbitrary")),
    )(qkv, qa, qkv, ka)


def _rot_half_cols(w):
    D = w.shape[0]
    a = w.reshape(D, -1, HEAD_DIM)
    half = HEAD_DIM // 2
    return jnp.concatenate([-a[..., half:], a[..., :half]], axis=-1).reshape(w.shape)


def _rope_tables(S):
    half = HEAD_DIM // 2
    inv_freq = ROPE_THETA ** (-jnp.arange(half, dtype=F32) / half)
    ang = jnp.arange(S, dtype=F32)[:, None] * inv_freq[None, :]
    c, s = jnp.cos(ang), jnp.sin(ang)
    c2, s2 = jnp.concatenate([c, c], axis=1), jnp.concatenate([s, s], axis=1)
    cos = jnp.stack([jnp.concatenate([c2, jnp.ones_like(c2)], axis=1), jnp.concatenate([c2, c2], axis=1)])
    sin = jnp.stack([jnp.concatenate([s2, jnp.zeros_like(s2)], axis=1), jnp.concatenate([s2, s2], axis=1)])
    return cos, sin


def _compress_kernel(x_ref, pea_ref, peb_ref, wa_ref, wb_ref, w2_ref, o_ref, pa_ref, pb0_ref, *, n_rows):
    u = pl.program_id(1)
    x = x_ref[0].astype(F32)
    pa = jnp.dot((x + pea_ref[...]).astype(BF16), wa_ref[...], preferred_element_type=F32)
    pb = jnp.dot((x + peb_ref[...]).astype(BF16), wb_ref[...], preferred_element_type=F32)

    def emit(slab, hid):
        y = jnp.dot(_gelu_tanh(hid).astype(BF16), w2_ref[...], preferred_element_type=F32)
        o_ref[0, pl.ds(pl.multiple_of(slab * n_rows, n_rows), n_rows), :] = y

    @pl.when(u == 0)
    def _():
        pb0_ref[...] = pb

    @pl.when(u > 0)
    def _():
        emit(u - 1, pa_ref[...] + pb)

    @pl.when(u == 3)
    def _():
        emit(3, pa + pltpu.roll(pb0_ref[...], n_rows - 1, axis=0))

    pa_ref[...] = pa


def nsa_compress(src, pe, w1, w2):
    B, S, W = src.shape
    G = NSA_GROUPS
    n_rows = S // 64
    half = CMP_BLOCK // 2
    cw = half * W
    xv = src.reshape(B, n_rows, 4 * cw)
    pe_flat = jnp.transpose(pe, (1, 0, 2)).reshape(CMP_BLOCK, W).astype(F32)
    pea, peb = pe_flat[:half].reshape(1, cw), pe_flat[half:].reshape(1, cw)
    eye = jnp.eye(G, dtype=F32)
    wfull = jnp.einsum('gldh,gk->lkdgh', w1.astype(F32), eye).reshape(CMP_BLOCK, W, G * CMP_HIDDEN)
    wa = wfull[:half].reshape(cw, G * CMP_HIDDEN).astype(BF16)
    wb = wfull[half:].reshape(cw, G * CMP_HIDDEN).astype(BF16)
    w2bd = jnp.einsum('ghd,gk->ghkd', w2.astype(F32), eye).reshape(G * CMP_HIDDEN, W).astype(BF16)
    const = lambda a: pl.BlockSpec(a.shape, lambda b, u: (0,) * a.ndim)
    return pl.pallas_call(
        functools.partial(_compress_kernel, n_rows=n_rows),
        grid=(B, 4),
        in_specs=[pl.BlockSpec((1, n_rows, cw), lambda b, u: (b, 0, u)),
                  const(pea), const(peb), const(wa), const(wb), const(w2bd)],
        out_specs=pl.BlockSpec((1, 4 * n_rows, W), lambda b, u: (b, 0, 0)),
        out_shape=jax.ShapeDtypeStruct((B, 4 * n_rows, W), F32),
        scratch_shapes=[pltpu.VMEM((n_rows, G * CMP_HIDDEN), F32), pltpu.VMEM((n_rows, G * CMP_HIDDEN), F32)],
        compiler_params=_cparams(("parallel", "arbitrary")),
    )(xv, pea, peb, wa, wb, w2bd)


def _nsa_attn_kernel(q_ref, kvs_ref, kvw_ref, kvc_ref, kvct_ref, gate_ref, bmat_ref, pmat_ref, o_ref,
                     *, tq, tk, seq):
    R = NSA_Q_PER_GROUP
    M = R * tq
    n_slc = seq // SLC_BLOCK
    n_cmp = 4 * n_slc
    n_sel = min(SLC_TOPK, n_slc)
    blocks_per_chunk = tk // SLC_BLOCK
    assert tq & (tq - 1) == 0 and n_slc & (n_slc - 1) == 0 and tk % tq == 0
    log_slc = n_slc.bit_length() - 1
    qi = pl.program_id(2)
    t0 = qi * tq

    qs = jnp.concatenate([q_ref[0, :, r * LANES:(r + 1) * LANES] for r in range(R)], axis=0)

    kvc = kvc_ref[0, 0]
    sT = lax.dot_general(kvc, qs, _NT, preferred_element_type=F32)
    rowc = lax.broadcasted_iota(jnp.int32, (n_cmp, 1), 0)
    cmp_end = (rowc & (n_slc - 1)) * SLC_BLOCK + (rowc >> log_slc) * CMP_STRIDE + (CMP_BLOCK - 1)
    tcol = t0 + (lax.broadcasted_iota(jnp.int32, (1, M), 1) & (tq - 1))
    sm = jnp.where(cmp_end <= tcol, sT, NEG_INF)
    mx = jnp.max(sm, axis=0, keepdims=True)
    e = jnp.exp2(sm - mx)
    inv = jnp.where(mx > 0.5 * NEG_INF, 1.0 / jnp.sum(e, axis=0, keepdims=True), 0.0)
    pT = e * inv
    o_c = jnp.dot(kvct_ref[0, 0], pT.astype(BF16), preferred_element_type=F32).T

    psum = pT[:, 0:tq]
    for r in range(1, R):
        psum = psum + pT[:, r * tq:(r + 1) * tq]
    p0, p1, p2, p3 = (psum[u * n_slc:(u + 1) * n_slc, :] for u in range(4))
    jrow = lax.broadcasted_iota(jnp.int32, (n_slc, tq), 0)
    p3_prev = jnp.where(jrow == 0, 0.0, pltpu.roll(p3, 1, axis=0))
    imp = p0 + p1 + p2 + 0.5 * p3 + 0.5 * p3_prev
    cur = (t0 + lax.broadcasted_iota(jnp.int32, (n_slc, tq), 1)) >> (SLC_BLOCK.bit_length() - 1)
    forced = (jrow == 0) | (jrow == cur) | (jrow == cur - 1)
    vals = jnp.where(forced, -jnp.inf, jnp.where(jrow <= cur, imp, -jnp.inf))
    sel0 = jnp.where(forced, 1.0, 0.0)
    n_free = n_sel - 3

    def pick(exact_ties):
        v, sel = vals, sel0
        for _ in range(n_free):
            m = jnp.max(v, axis=0, keepdims=True)
            if exact_ties:
                hit = jrow == jnp.min(jnp.where(v == m, jrow, n_slc), axis=0, keepdims=True)
            else:
                hit = v == jnp.where(m == -jnp.inf, jnp.nan, m)
            sel = jnp.where(hit, 1.0, sel)
            v = jnp.where(hit, -jnp.inf, v)
        return sel

    sel_fast = pick(False)
    n_cand = jnp.sum(jnp.where(vals > -jnp.inf, 1.0, 0.0), axis=0, keepdims=True)
    n_picked = jnp.sum(sel_fast - sel0, axis=0, keepdims=True)
    unique = jnp.min(jnp.where(n_picked == jnp.minimum(n_cand, float(n_free)), 1.0, 0.0)) > 0.5
    sel = lax.cond(unique, lambda: sel_fast, lambda: pick(True))
    sel_bias = ((sel.T - 1.0) * (-NEG_INF)).astype(BF16)

    n_parts = 2
    hp = R // n_parts
    mp = hp * tq
    qparts = [qs[i * mp:(i + 1) * mp] for i in range(n_parts)]
    qrow = t0 + (lax.broadcasted_iota(jnp.int32, (mp, 1), 0) & (tq - 1))
    kcol = lax.broadcasted_iota(jnp.int32, (1, tk), 1)
    n_chunks = seq // tk
    j_last = t0 // tk

    klane = lax.broadcasted_iota(jnp.int32, (tk, LANES), 1)

    def slc_step(j, carry, masked):
        kv = kvs_ref[0, pl.ds(pl.multiple_of(j * tk, tk), tk), :]
        k_sel = jnp.where(klane < HEAD_DIM, kv, bmat_ref[...]).astype(BF16)
        ones_v = jnp.where(klane < HEAD_DIM, 1.0, kv).astype(BF16)
        off = pl.multiple_of(blocks_per_chunk * (n_chunks - 1 - j), blocks_per_chunk)
        place = pmat_ref[pl.ds(off, n_slc), :].astype(BF16)
        q_bias = jnp.dot(sel_bias, place, preferred_element_type=F32).astype(BF16)
        q_bias = jnp.concatenate([q_bias] * hp, axis=0)
        new = []
        for qp, (m, acc) in zip(qparts, carry):
            s = lax.dot_general(qp + q_bias, k_sel, _NT, preferred_element_type=F32)
            if masked:
                s = jnp.where(j * tk + kcol <= qrow, s, NEG_INF)
            m_new = jnp.maximum(m, jnp.max(s, axis=1, keepdims=True))
            p = jnp.exp2(s - m_new).astype(BF16)
            acc = jnp.exp2(m - m_new) * acc + jnp.dot(p, ones_v, preferred_element_type=F32)
            new.append((m_new, acc))
        return tuple(new)

    init = (jnp.full((mp, 1), NEG_INF, F32), jnp.zeros((mp, LANES), F32))
    n_pairs = j_last // 2
    carry = lax.fori_loop(0, n_pairs, lambda i, c: slc_step(2 * i + 1, slc_step(2 * i, c, False), False),
                          (init,) * n_parts)
    carry = lax.fori_loop(2 * n_pairs, j_last, functools.partial(slc_step, masked=False), carry)
    carry = slc_step(j_last, carry, True)
    o_s = jnp.concatenate([acc / acc[:, 0:1] for (_, acc) in carry], axis=0)

    wlen = WINDOW + tq
    start = jnp.maximum(t0 - WINDOW, 0)
    kvw = kvw_ref[0, pl.ds(pl.multiple_of(start, tq), wlen), :]
    s_w = lax.dot_general(qs, kvw, _NT, preferred_element_type=F32)
    qpos = t0 + (lax.broadcasted_iota(jnp.int32, (M, 1), 0) & (tq - 1))
    kpos = start + lax.broadcasted_iota(jnp.int32, (1, wlen), 1)
    s_w = jnp.where(kpos <= qpos, jnp.where(kpos > qpos - WINDOW, s_w, NEG_INF), NEG_INF)
    p_w = jnp.exp2(s_w - jnp.max(s_w, axis=1, keepdims=True)).astype(BF16)
    wlane = lax.broadcasted_iota(jnp.int32, (wlen, LANES), 1)
    acc_w = jnp.dot(p_w, jnp.where(wlane < HEAD_DIM, 1.0, kvw).astype(BF16), preferred_element_type=F32)
    o_w = acc_w / acc_w[:, 0:1]

    gates = gate_ref[0]
    lane = lax.broadcasted_iota(jnp.int32, (tq, LANES), 1)
    comb = []
    for r in range(R):
        rs = slice(r * tq, (r + 1) * tq)
        comb.append(gates[:, r:r + 1] * o_c[rs] + gates[:, R + r:R + r + 1] * o_s[rs]
                    + gates[:, 2 * R + r:2 * R + r + 1] * o_w[rs])
    out = [jnp.where(lane < HEAD_DIM, pltpu.roll(comb[2 * i], HEAD_DIM, axis=1), comb[2 * i + 1])
           for i in range(R // 2)]
    o_ref[0] = jnp.concatenate(out, axis=1).astype(o_ref.dtype)


def nsa_attention(qkv, kvc, kvct, gates, *, tq=256, tk=1024):
    B, S, _ = qkv.shape
    G, R = NSA_GROUPS, NSA_Q_PER_GROUP
    tk = min(tk, S)
    n_slc = S // SLC_BLOCK
    n_cmp = kvc.shape[2]
    bpc = tk // SLC_BLOCK
    assert bpc <= LANES - HEAD_DIM
    off = bpc * (S // tk - 1)
    lane = np.arange(LANES)[None, :]
    bmat = jnp.asarray(lane - HEAD_DIM == np.arange(tk)[:, None] // SLC_BLOCK, BF16)
    pmat = jnp.asarray((np.arange(n_slc + off)[:, None] - off == lane - HEAD_DIM) & (lane >= HEAD_DIM)
                       & (lane < HEAD_DIM + bpc), F32)
    slc_blk0, win_blk0 = N_HEADS, N_HEADS + G
    return pl.pallas_call(
        functools.partial(_nsa_attn_kernel, tq=tq, tk=tk, seq=S),
        grid=(B, G, S // tq),
        in_specs=[pl.BlockSpec((1, tq, R * LANES), lambda b, g, i: (b, i, g)),
                  pl.BlockSpec((1, S, LANES), lambda b, g, i: (b, 0, slc_blk0 + g)),
                  pl.BlockSpec((1, S, LANES), lambda b, g, i: (b, 0, win_blk0 + g)),
                  pl.BlockSpec((1, 1, n_cmp, LANES), lambda b, g, i: (b, g, 0, 0)),
                  pl.BlockSpec((1, 1, LANES, n_cmp), lambda b, g, i: (b, g, 0, 0)),
                  pl.BlockSpec((1, tq, LANES), lambda b, g, i: (b, i, g)),
                  pl.BlockSpec(bmat.shape, lambda b, g, i: (0, 0)),
                  pl.BlockSpec(pmat.shape, lambda b, g, i: (0, 0))],
        out_specs=pl.BlockSpec((1, tq, R * HEAD_DIM), lambda b, g, i: (b, i, g)),
        out_shape=jax.ShapeDtypeStruct((B, S, ATTN_WIDTH), BF16),
        compiler_params=_cparams(("parallel", "parallel", "arbitrary")),
    )(qkv, qkv, qkv, kvc, kvct, gates, bmat, pmat)


def nsa_layer(h, gamma, w_in, pe_k, w1_k, w2_k, pe_v, w1_v, w2_v, w_out, B, S):
    G, R, hd, aw = NSA_GROUPS, NSA_Q_PER_GROUP, HEAD_DIM, ATTN_WIDTH
    kvd = G * hd
    sec = lambda i: w_in[:, aw + i * kvd: aw + (i + 1) * kvd]
    wq = w_in[:, :aw] * (hd ** -0.5 * LOG2E)
    wa = jnp.concatenate([_head_slots(wq, N_HEADS), _head_slots(sec(2), G, sec(3)),
                          _head_slots(sec(4), G, sec(5))], axis=1).astype(BF16)
    wb = jnp.concatenate([_head_slots(_rot_half_cols(wq), N_HEADS), _head_slots(_rot_half_cols(sec(2)), G),
                          _head_slots(_rot_half_cols(sec(4)), G)], axis=1).astype(BF16)
    cos, sin = _rope_tables(S)
    qkv = norm_matmul(h, gamma, wa, wb=wb, cos=cos, sin=sin)
    kc_src = norm_matmul(h, gamma, sec(0).astype(BF16), wb=_rot_half_cols(sec(0)).astype(BF16),
                         cos=cos, sin=sin, table_of_tile=lambda j: 1)
    vc_src = norm_matmul(h, gamma, sec(1).astype(BF16))
    wg = w_in[:, aw + 6 * kvd:].reshape(-1, 3, G, R)
    wg = jnp.transpose(wg, (0, 2, 1, 3)).reshape(-1, G, 3 * R)
    wg = jnp.pad(wg, ((0, 0), (0, 0), (0, LANES - 3 * R))).reshape(-1, G * LANES).astype(BF16)
    gates = norm_matmul(h, gamma, wg, act="sigmoid", out_dtype=F32)
    kc = nsa_compress(kc_src.reshape(B, S, kvd), pe_k, w1_k, w2_k)
    vc = nsa_compress(vc_src.reshape(B, S, kvd), pe_v, w1_v, w2_v)
    n_cmp = kc.shape[1]
    kvc = jnp.concatenate([kc.reshape(B, n_cmp, G, hd), vc.reshape(B, n_cmp, G, hd)], axis=-1)
    kvc = jnp.transpose(kvc, (0, 2, 1, 3)).astype(BF16)
    kvct = jnp.swapaxes(kvc, 2, 3)
    o = nsa_attention(qkv.reshape(B, S, -1), kvc, kvct, gates.reshape(B, S, G * LANES))
    return matmul_residual(o.reshape(B * S, aw), w_out.astype(BF16), h)


def kernel(x, l0_attn_norm, l0_w_in, l0_cmp_pe_k, l0_cmp_w1_k, l0_cmp_w2_k, l0_cmp_pe_v, l0_cmp_w1_v,
           l0_cmp_w2_v, l0_w_out, l0_ffn_norm, l0_peer_wq, l0_peer_keys, l0_peer_u, l0_peer_v,
           l1_attn_norm, l1_w_in, l1_f_bias, l1_w_out, l1_ffn_norm, l1_peer_wq, l1_peer_keys, l1_peer_u,
           l1_peer_v, final_norm):
    B, S, D = x.shape
    h = x.reshape(B * S, D)
    h = nsa_layer(h, l0_attn_norm, l0_w_in, l0_cmp_pe_k, l0_cmp_w1_k, l0_cmp_w2_k, l0_cmp_pe_v, l0_cmp_w1_v,
                  l0_cmp_w2_v, l0_w_out, B, S)
    h = peer_layer(h, l0_ffn_norm, l0_peer_wq, l0_peer_keys, l0_peer_u, l0_peer_v)
    h = fox_layer(h, l1_attn_norm, l1_w_in, l1_f_bias, l1_w_out, B, S)
    h = peer_layer(h, l1_ffn_norm, l1_peer_wq, l1_peer_keys, l1_peer_u, l1_peer_v)
    return rmsnorm(h, final_norm).reshape(B, S, D)
```

```python
import functools

import numpy as np
import jax
import jax.numpy as jnp
from jax import lax
from jax.experimental import pallas as pl
from jax.experimental.pallas import tpu as pltpu

F32 = jnp.float32
BF16 = jnp.bfloat16

D_MODEL = 1024
N_HEADS = 16
HEAD_DIM = 64
ATTN_WIDTH = N_HEADS * HEAD_DIM
NSA_GROUPS = 4
NSA_Q_PER_GROUP = N_HEADS // NSA_GROUPS
CMP_BLOCK = 32
CMP_STRIDE = 16
CMP_HIDDEN = 2 * HEAD_DIM
SLC_BLOCK = 64
SLC_TOPK = 16
WINDOW = 512
FORCE_SCORE = 1.0e4
ROPE_THETA = 10000.0
PEER_HEADS = 8
PEER_N_KEYS = 128
PEER_TOPK = 16
PEER_HALF_DIM = 128
RMS_EPS = 1e-6
NEG_INF = -1e30
LOG2E = 1.4426950408889634

LANES = 128
VMEM_LIMIT_BYTES = 56 * 1024 * 1024

_NT = (((1,), (1,)), ((), ()))


def _cparams(sem, vmem=VMEM_LIMIT_BYTES, flags=None):
    return pltpu.CompilerParams(dimension_semantics=sem, vmem_limit_bytes=vmem, flags=flags)


def _gelu_tanh(x):
    return 0.5 * x * (1.0 + jnp.tanh(0.7978845608028654 * (x + 0.044715 * (x * x * x))))


def _gelu_sigmoid(x):
    c = -2.0 * 0.7978845608028654 * LOG2E
    t = x * (c + (c * 0.044715) * (x * x))
    return x / (1.0 + jnp.exp2(t))


def _rms_rows(x, g):
    ms = jnp.mean(x * x, axis=-1, keepdims=True)
    return x * lax.rsqrt(ms + RMS_EPS) * g


def _norm_mm_kernel(*refs, act, has_bias, rope, emit_xn):
    it = iter(refs)
    x_ref, g_ref = next(it), next(it)
    wa_ref = next(it)
    wb_ref = next(it) if rope else None
    cos_ref = next(it) if rope else None
    sin_ref = next(it) if rope else None
    b_ref = next(it) if has_bias else None
    o_ref = next(it)
    xo_ref = next(it) if emit_xn else None
    xn_ref = next(it)

    @pl.when(pl.program_id(1) == 0)
    def _():
        xn = _rms_rows(x_ref[...], g_ref[...])
        xn_ref[...] = xn.astype(BF16)
        if emit_xn:
            xo_ref[...] = xn.T.astype(BF16)

    xn = xn_ref[...]
    y = jnp.dot(xn, wa_ref[...], preferred_element_type=F32)
    if rope:
        yb = jnp.dot(xn, wb_ref[...], preferred_element_type=F32)
        cos, sin = cos_ref[...], sin_ref[...]
        for s in range(y.shape[1] // LANES):
            sl = slice(s * LANES, (s + 1) * LANES)
            o_ref[:, sl] = (y[:, sl] * cos + yb[:, sl] * sin).astype(o_ref.dtype)
        return
    if has_bias:
        y = y + b_ref[...]
    if act == "sigmoid":
        y = jax.nn.sigmoid(y)
    elif act == "log_sigmoid":
        y = jax.nn.log_sigmoid(y)
    o_ref[...] = y.astype(o_ref.dtype)


def norm_matmul(x, gamma, wa, *, wb=None, cos=None, sin=None, table_of_tile=None, bias=None,
                act=None, out_dtype=BF16, emit_xn=False, tm=1024, tn=1024):
    T, D = x.shape
    N = wa.shape[1]
    tm, tn = min(tm, T), min(tn, N)
    assert T % tm == 0 and N % tn == 0 and tn % LANES == 0
    rope = wb is not None
    in_specs = [pl.BlockSpec((tm, D), lambda i, j: (i, 0)),
                pl.BlockSpec((1, D), lambda i, j: (0, 0)),
                pl.BlockSpec((D, tn), lambda i, j: (0, j))]
    args = [x, gamma.reshape(1, D).astype(F32), wa]
    if rope:
        S = cos.shape[1]
        assert S % tm == 0
        n_pos = S // tm
        tmap = table_of_tile if table_of_tile is not None else (lambda j: 0)
        in_specs += [pl.BlockSpec((D, tn), lambda i, j: (0, j)),
                     pl.BlockSpec((None, tm, LANES), lambda i, j: (tmap(j), i % n_pos, 0)),
                     pl.BlockSpec((None, tm, LANES), lambda i, j: (tmap(j), i % n_pos, 0))]
        args += [wb, cos, sin]
    if bias is not None:
        in_specs.append(pl.BlockSpec((1, tn), lambda i, j: (0, j)))
        args.append(bias.reshape(1, N).astype(F32))
    out_shape = [jax.ShapeDtypeStruct((T, N), out_dtype)]
    out_specs = [pl.BlockSpec((tm, tn), lambda i, j: (i, j))]
    if emit_xn:
        out_shape.append(jax.ShapeDtypeStruct((D, T), BF16))
        out_specs.append(pl.BlockSpec((D, tm), lambda i, j: (0, i)))
    res = pl.pallas_call(
        functools.partial(_norm_mm_kernel, act=act, has_bias=bias is not None, rope=rope, emit_xn=emit_xn),
        grid=(T // tm, N // tn),
        in_specs=in_specs,
        out_specs=out_specs,
        out_shape=out_shape,
        scratch_shapes=[pltpu.VMEM((tm, D), BF16)],
        compiler_params=_cparams(("parallel", "arbitrary")),
    )(*args)
    return res if emit_xn else res[0]


def _mm_res_kernel(a_ref, w_ref, r_ref, o_ref):
    o_ref[...] = r_ref[...] + jnp.dot(a_ref[...], w_ref[...], preferred_element_type=F32)


def matmul_residual(a, w, res, *, tm=1024, tn=1024):
    T, K = a.shape
    N = w.shape[1]
    tm, tn = min(tm, T), min(tn, N)
    assert T % tm == 0 and N % tn == 0
    return pl.pallas_call(
        _mm_res_kernel,
        grid=(T // tm, N // tn),
        in_specs=[pl.BlockSpec((tm, K), lambda i, j: (i, 0)),
                  pl.BlockSpec((K, tn), lambda i, j: (0, j)),
                  pl.BlockSpec((tm, tn), lambda i, j: (i, j))],
        out_specs=pl.BlockSpec((tm, tn), lambda i, j: (i, j)),
        out_shape=jax.ShapeDtypeStruct((T, N), F32),
        compiler_params=_cparams(("parallel", "arbitrary")),
    )(a, w, res)


def _rmsnorm_kernel(x_ref, g_ref, o_ref):
    o_ref[...] = _rms_rows(x_ref[...], g_ref[...])


def rmsnorm(x, gamma, *, tm=1024):
    T, D = x.shape
    tm = min(tm, T)
    return pl.pallas_call(
        _rmsnorm_kernel,
        grid=(T // tm,),
        in_specs=[pl.BlockSpec((tm, D), lambda i: (i, 0)), pl.BlockSpec((1, D), lambda i: (0, 0))],
        out_specs=pl.BlockSpec((tm, D), lambda i: (i, 0)),
        out_shape=jax.ShapeDtypeStruct((T, D), F32),
        compiler_params=_cparams(("parallel",)),
    )(x, gamma.reshape(1, D).astype(F32))


def _peer_cand_tables(tn):
    fidx, vmask = [], []
    for k2 in range(16):
        fidx.append(k2); vmask.append(0.0)
    for k1 in range(1, 8):
        lim = PEER_TOPK // (k1 + 1)
        for k2 in range(8):
            fidx.append(k1 * 16 + k2); vmask.append(0.0 if k2 < lim else -np.inf)
    for k1 in range(8, 16):
        fidx.append(k1 * 16); vmask.append(0.0)
    fidx = np.broadcast_to(np.asarray(fidx, np.int32)[:, None], (80, tn))
    vmask = np.broadcast_to(np.asarray(vmask, np.float32)[:, None], (80, tn))
    return jnp.asarray(fidx), jnp.asarray(vmask)


def _top16_rows(s, exact_ties):
    n, tn = s.shape
    rows = lax.broadcasted_iota(jnp.int32, (n, tn), 0)
    rows16 = lax.broadcasted_iota(jnp.int32, (PEER_TOPK, tn), 0)
    rank = jnp.full((n, tn), float(PEER_TOPK), F32)
    tops = jnp.zeros((PEER_TOPK, tn), F32)
    v = s
    for k in range(PEER_TOPK):
        m = jnp.max(v, axis=0, keepdims=True)
        if exact_ties:
            hit = rows == jnp.min(jnp.where(v == m, rows, n), axis=0, keepdims=True)
        else:
            hit = v == m
        rank = jnp.where(hit, float(k), rank)
        v = jnp.where(hit, -jnp.inf, v)
        tops = jnp.where(rows16 == k, m, tops)
    return tops, rank, v


def _peer_select_head(q_ref, keys_ref, fidx, vmask, exact_ties):
    tops, ranks, es, picked = [], [], [], []
    for p in range(2):
        q = q_ref[:, p * PEER_HALF_DIM:(p + 1) * PEER_HALF_DIM]
        s = lax.dot_general(keys_ref[p], q, _NT, preferred_element_type=F32)
        t, r, v = _top16_rows(s, exact_ties)
        tops.append(t); ranks.append(r)
        es.append(jnp.exp(s - t[0:1, :]))
        picked.append(jnp.sum(jnp.where(v == -jnp.inf, 1.0, 0.0), axis=0, keepdims=True))
    ts1, ts2 = tops
    pieces = [ts1[0:1, :] + ts2]
    for k1 in range(1, 8):
        pieces.append(ts1[k1:k1 + 1, :] + ts2[0:8, :])
    pieces.append(ts1[8:16, :] + ts2[0:1, :])
    cand0 = jnp.concatenate(pieces, axis=0) + vmask
    cand = cand0
    for _ in range(PEER_TOPK):
        m = jnp.max(cand, axis=0, keepdims=True)
        if exact_ties:
            hit = fidx == jnp.min(jnp.where(cand == m, fidx, 4096), axis=0, keepdims=True)
        else:
            hit = cand == m
        cand = jnp.where(hit, -jnp.inf, cand)
    taken = jnp.logical_and(cand == -jnp.inf, vmask == 0.0)
    takenf = taken.astype(F32)
    picked.append(jnp.sum(takenf, axis=0, keepdims=True))
    unique = jnp.min(jnp.where((picked[0] == PEER_TOPK) & (picked[1] == PEER_TOPK) & (picked[2] == PEER_TOPK),
                               1.0, 0.0)) > 0.5
    best = ts1[0:1, :] + ts2[0:1, :]
    z = jnp.sum(jnp.where(taken, jnp.exp(cand0 - best), 0.0), axis=0, keepdims=True)
    counts = [jnp.sum(takenf[0:16, :], axis=0, keepdims=True)]
    for k1 in range(1, 8):
        counts.append(jnp.sum(takenf[16 + 8 * (k1 - 1):16 + 8 * k1, :], axis=0, keepdims=True))
    tail = takenf[72:80, :]
    cnt = jnp.zeros_like(ranks[0])
    for k1 in range(PEER_TOPK):
        nk = counts[k1] if k1 < 8 else tail[k1 - 8:k1 - 7, :]
        cnt = jnp.where(ranks[0] == float(k1), nk, cnt)
    return (cnt, ranks[1], es[0], es[1] / z), unique


def _peer_select_kernel(q_ref, keys_ref, fidx_ref, vmask_ref, cnt_ref, rank2_ref, e1_ref, e2_ref):
    fidx = fidx_ref[...]
    vmask = vmask_ref[...]

    def store(vals):
        for ref, val in zip((cnt_ref, rank2_ref, e1_ref, e2_ref), vals):
            ref[...] = val.astype(ref.dtype)

    vals, unique = _peer_select_head(q_ref, keys_ref, fidx, vmask, exact_ties=False)
    store(vals)

    @pl.when(jnp.logical_not(unique))
    def _():
        store(_peer_select_head(q_ref, keys_ref, fidx, vmask, exact_ties=True)[0])


def peer_select(q, keys, *, tn=512):
    T = q.shape[0]
    fidx, vmask = _peer_cand_tables(tn)
    rows = PEER_HEADS * PEER_N_KEYS
    ospec = pl.BlockSpec((PEER_N_KEYS, tn), lambda i, h: (h, i))
    return pl.pallas_call(
        _peer_select_kernel,
        grid=(T // tn, PEER_HEADS),
        in_specs=[pl.BlockSpec((tn, 2 * PEER_HALF_DIM), lambda i, h: (i, h)),
                  pl.BlockSpec((2, PEER_N_KEYS, PEER_HALF_DIM), lambda i, h: (h, 0, 0)),
                  pl.BlockSpec((80, tn), lambda i, h: (0, 0)),
                  pl.BlockSpec((80, tn), lambda i, h: (0, 0))],
        out_specs=[ospec] * 4,
        out_shape=[jax.ShapeDtypeStruct((rows, T), dt) for dt in (F32, BF16, F32, BF16)],
        compiler_params=_cparams(("parallel", "parallel")),
    )(q, keys, fidx, vmask)


def _peer_dense_kernel(xn_ref, u_ref, vt_ref, cnt_ref, rank2_ref, e1_ref, e2_ref, res_ref, o_ref,
                       acc_ref, g0_ref, g1_ref, *, c_per_step):
    j = pl.program_id(1)
    n_tiles = pl.num_programs(1) - 1

    @pl.when(j == 0)
    def _():
        acc_ref[...] = jnp.zeros_like(acc_ref)
        g1_ref[...] = jnp.zeros_like(g1_ref)

    @pl.when(j % 2 == 0)
    def _():
        _peer_dense_step(xn_ref, u_ref, vt_ref, cnt_ref, rank2_ref, e1_ref, e2_ref, acc_ref,
                         g1_ref, g0_ref, j, c_per_step)

    @pl.when(j % 2 == 1)
    def _():
        _peer_dense_step(xn_ref, u_ref, vt_ref, cnt_ref, rank2_ref, e1_ref, e2_ref, acc_ref,
                         g0_ref, g1_ref, j, c_per_step)

    @pl.when(j == n_tiles)
    def _():
        o_ref[...] = res_ref[...] + acc_ref[...].T


def _peer_dense_step(xn_ref, u_ref, vt_ref, cnt_ref, rank2_ref, e1_ref, e2_ref, acc_ref, g_ref, g_next_ref,
                     j, c_per_step):

    tn = xn_ref.shape[1]
    bf16_rows = 16
    reps = PEER_N_KEYS // bf16_rows

    def row_tile(ref, row):
        r16 = jnp.broadcast_to(ref[pl.ds(row, 1), :], (bf16_rows, tn)).astype(BF16)
        return jnp.concatenate([r16] * reps, axis=0)

    c0 = jnp.maximum(j - 1, 0) * c_per_step
    up_rows = 2 * PEER_N_KEYS
    blocks = []
    for cc in range(c_per_step):
        if (cc * PEER_N_KEYS) % up_rows == 0:
            rs = slice(cc * PEER_N_KEYS, cc * PEER_N_KEYS + up_rows)
            hT = jnp.dot(u_ref[rs, :], xn_ref[...], preferred_element_type=F32)
            g_next_ref[rs, :] = _gelu_sigmoid(hT).astype(BF16)
        c = c0 + cc
        w = None
        for h in range(PEER_HEADS):
            row = h * PEER_N_KEYS + c
            n_row = row_tile(cnt_ref, row)
            e1_row = row_tile(e1_ref, row)
            sl = slice(h * PEER_N_KEYS, (h + 1) * PEER_N_KEYS)
            term = jnp.where(rank2_ref[sl, :] < n_row, e2_ref[sl, :], 0.0) * e1_row
            w = term if w is None else w + term
        blocks.append(w * g_ref[cc * PEER_N_KEYS:(cc + 1) * PEER_N_KEYS, :])
    aT = jnp.concatenate(blocks, axis=0)
    acc_ref[...] += jnp.dot(vt_ref[...], aT, preferred_element_type=F32)


def peer_dense(xn, u, vt, cnt, rank2, e1, e2, res, *, tn=1024, te=512):
    D, T = xn.shape
    E = u.shape[0]
    tn = min(tn, T)
    rows = PEER_HEADS * PEER_N_KEYS
    once = pl.Buffered(1)
    sel_spec = pl.BlockSpec((rows, tn), lambda i, j: (0, i))
    n_tiles = E // te
    return pl.pallas_call(
        functools.partial(_peer_dense_kernel, c_per_step=te // PEER_N_KEYS),
        grid=(T // tn, n_tiles + 1),
        in_specs=[pl.BlockSpec((D, tn), lambda i, j: (0, i)),
                  pl.BlockSpec((te, D), lambda i, j: (jnp.minimum(j, n_tiles - 1), 0)),
                  pl.BlockSpec((D, te), lambda i, j: (0, jnp.maximum(j - 1, 0))),
                  sel_spec, sel_spec, sel_spec, sel_spec,
                  pl.BlockSpec((tn, D), lambda i, j: (i, 0), pipeline_mode=once)],
        out_specs=pl.BlockSpec((tn, D), lambda i, j: (i, 0), pipeline_mode=once),
        out_shape=jax.ShapeDtypeStruct((T, D), F32),
        scratch_shapes=[pltpu.VMEM((D, tn), F32), pltpu.VMEM((te, tn), BF16), pltpu.VMEM((te, tn), BF16)],
        compiler_params=_cparams(("parallel", "arbitrary")),
    )(xn, u, vt, cnt, rank2, e1, e2, res)


def peer_layer(h, gamma, w_q, sub_keys, u, v):
    q, xn = norm_matmul(h, gamma, w_q.astype(BF16), emit_xn=True)
    keys = sub_keys.reshape(2 * PEER_HEADS, PEER_N_KEYS, PEER_HALF_DIM).astype(BF16)
    cnt, rank2, e1, e2 = peer_select(q, keys)
    return peer_dense(xn, u.astype(BF16), v.T.astype(BF16), cnt, rank2, e1, e2, h)


def _cumsum_aug_kernel(lf_ref, tri_ref, place_q_ref, place_k_ref, ones_q_ref, ones_k_ref,
                       qa_ref, ka_ref, carry_ref):
    @pl.when(pl.program_id(1) == 0)
    def _():
        carry_ref[...] = jnp.zeros_like(carry_ref)

    lf = lf_ref[0]
    c = jnp.dot(tri_ref[...], lf, preferred_element_type=F32, precision=lax.Precision.HIGHEST) + carry_ref[...]
    carry_ref[...] = c[-1:, :]
    c = c * LOG2E
    hi = c.astype(BF16)
    r1 = c - hi.astype(F32)
    mid = r1.astype(BF16)
    lo = (r1 - mid.astype(F32)).astype(BF16)
    nh = N_HEADS
    lane = lax.broadcasted_iota(jnp.int32, c.shape, 1)
    parts = jnp.where(lane < nh, hi.astype(F32),
                      jnp.where(lane < 2 * nh, pltpu.roll(mid.astype(F32), nh, axis=1),
                                pltpu.roll(lo.astype(F32), 2 * nh, axis=1)))
    parts = jnp.where(lane < 3 * nh, parts, 0.0).astype(BF16)
    qa_ref[0] = (jnp.dot(parts, place_q_ref[...], preferred_element_type=F32) + ones_q_ref[...]).astype(BF16)
    ka_ref[0] = (jnp.dot(parts, place_k_ref[...], preferred_element_type=F32) + ones_k_ref[...]).astype(BF16)


def fox_bias_operands(logf, *, tc=512):
    B, S, _ = logf.shape
    nh = N_HEADS
    tri = jnp.asarray(np.tril(np.ones((tc, tc), np.float32)))
    pq = np.zeros((LANES, nh * LANES), np.float32)
    pk = np.zeros((LANES, nh * LANES), np.float32)
    oq = np.zeros((1, nh * LANES), np.float32)
    ok = np.zeros((1, nh * LANES), np.float32)
    for h in range(nh):
        for part in range(3):
            pq[part * nh + h, h * LANES + part] = 1.0
            pk[part * nh + h, h * LANES + 3 + part] = -1.0
            oq[0, h * LANES + 3 + part] = 1.0
            ok[0, h * LANES + part] = 1.0
    const = lambda a: pl.BlockSpec(a.shape, lambda b, i: (0,) * a.ndim)
    pq, pk, oq, ok = jnp.asarray(pq, BF16), jnp.asarray(pk, BF16), jnp.asarray(oq), jnp.asarray(ok)
    out = jax.ShapeDtypeStruct((B, S, nh * LANES), BF16)
    return pl.pallas_call(
        _cumsum_aug_kernel,
        grid=(B, S // tc),
        in_specs=[pl.BlockSpec((1, tc, LANES), lambda b, i: (b, i, 0)),
                  const(tri), const(pq), const(pk), const(oq), const(ok)],
        out_specs=[pl.BlockSpec((1, tc, nh * LANES), lambda b, i: (b, i, 0))] * 2,
        out_shape=[out, out],
        scratch_shapes=[pltpu.VMEM((1, LANES), F32)],
        compiler_params=_cparams(("parallel", "arbitrary")),
    )(logf, tri, pq, pk, oq, ok)


def _fox_attn_kernel(q_ref, qa_ref, kv_ref, ka_ref, o_ref, *, tq, tk, heads_per_step):
    qi = pl.program_id(2)
    t0 = qi * tq
    n_full = t0 // tk
    n_diag = tq // tk
    lanes = [slice(hh * LANES, (hh + 1) * LANES) for hh in range(heads_per_step)]
    qs = [jnp.concatenate([q_ref[0, :, lsl], qa_ref[0, :, lsl]], axis=1) for lsl in lanes]

    def step(j, carry, masked):
        rows = pl.ds(pl.multiple_of(j * tk, tk), tk)
        klane = lax.broadcasted_iota(jnp.int32, (tk, LANES), 1)
        new = []
        for lsl, q, (m, acc) in zip(lanes, qs, carry):
            kv = kv_ref[0, rows, lsl]
            kk = jnp.concatenate([kv, ka_ref[0, rows, lsl]], axis=1)
            ones_v = jnp.where(klane < HEAD_DIM, 1.0, kv).astype(BF16)
            s = lax.dot_general(q, kk, _NT, preferred_element_type=F32)
            if masked:
                qpos = t0 + lax.broadcasted_iota(jnp.int32, (tq, 1), 0)
                kpos = j * tk + lax.broadcasted_iota(jnp.int32, (1, tk), 1)
                s = jnp.where(kpos <= qpos, s, NEG_INF)
            m_new = jnp.maximum(m, jnp.max(s, axis=1, keepdims=True))
            p = jnp.exp2(s - m_new).astype(BF16)
            acc = jnp.exp2(m - m_new) * acc + jnp.dot(p, ones_v, preferred_element_type=F32)
            new.append((m_new, acc))
        return tuple(new)

    init = (jnp.full((tq, 1), NEG_INF, F32), jnp.zeros((tq, LANES), F32))
    n_pairs = n_full // 2
    carry = lax.fori_loop(0, n_pairs, lambda i, c: step(2 * i + 1, step(2 * i, c, False), False),
                          (init,) * heads_per_step)
    carry = lax.fori_loop(2 * n_pairs, n_full, functools.partial(step, masked=False), carry)
    for d in range(n_diag):
        carry = step(n_full + d, carry, True)
    outs = [acc / acc[:, 0:1] for (_, acc) in carry]
    lane = lax.broadcasted_iota(jnp.int32, (tq, LANES), 1)
    blocks = []
    for pair in range(heads_per_step // 2):
        a, b = outs[2 * pair], outs[2 * pair + 1]
        blocks.append(jnp.where(lane < HEAD_DIM, pltpu.roll(a, HEAD_DIM, axis=1), b))
    o_ref[0] = jnp.concatenate(blocks, axis=1).astype(o_ref.dtype) if len(blocks) > 1 else blocks[0].astype(o_ref.dtype)


def _head_slots(w, n_heads, second=None):
    D = w.shape[0]
    a = w.reshape(D, n_heads, HEAD_DIM)
    b = jnp.zeros_like(a) if second is None else second.reshape(D, n_heads, HEAD_DIM)
    return jnp.concatenate([a, b], axis=-1).reshape(D, n_heads * LANES)


def fox_layer(h, gamma, w_in, f_bias, w_out, B, S):
    aw = ATTN_WIDTH
    wq = _head_slots(w_in[:, :aw] * (HEAD_DIM ** -0.5 * LOG2E), N_HEADS)
    wkv = _head_slots(w_in[:, aw:2 * aw], N_HEADS, w_in[:, 2 * aw:3 * aw])
    w_main = jnp.concatenate([wq, wkv], axis=1).astype(BF16)
    wf = jnp.pad(w_in[:, 3 * aw:], ((0, 0), (0, LANES - N_HEADS))).astype(BF16)
    bf = jnp.pad(f_bias.astype(F32), (0, LANES - N_HEADS))
    qkv = norm_matmul(h, gamma, w_main)
    logf = norm_matmul(h, gamma, wf, bias=bf, act="log_sigmoid", out_dtype=F32)
    qa, ka = fox_bias_operands(logf.reshape(B, S, LANES))
    qkv = qkv.reshape(B, S, 2 * N_HEADS * LANES)
    o = fox_attention(qkv, qa, ka)
    return matmul_residual(o.reshape(B * S, aw), w_out.astype(BF16), h)


def fox_attention(qkv, qa, ka, *, tq=1024, tk=1024, heads_per_step=2):
    B, S, _ = qkv.shape
    hs = heads_per_step
    wq = hs * LANES
    n_qblk = N_HEADS // hs
    return pl.pallas_call(
        functools.partial(_fox_attn_kernel, tq=tq, tk=tk, heads_per_step=hs),
        grid=(B, n_qblk, S // tq),
        in_specs=[pl.BlockSpec((1, tq, wq), lambda b, h, i: (b, i, h)),
                  pl.BlockSpec((1, tq, wq), lambda b, h, i: (b, i, h)),
                  pl.BlockSpec((1, S, wq), lambda b, h, i: (b, 0, n_qblk + h)),
                  pl.BlockSpec((1, S, wq), lambda b, h, i: (b, 0, h))],
        out_specs=pl.BlockSpec((1, tq, hs * HEAD_DIM), lambda b, h, i: (b, i, h)),
        out_shape=jax.ShapeDtypeStruct((B, S, ATTN_WIDTH), BF16),
        compiler_params=_cparams(("parallel", "parallel", "arbitrary")),
    )(qkv, qa, qkv, ka)


def _rot_half_cols(w):
    D = w.shape[0]
    a = w.reshape(D, -1, HEAD_DIM)
    half = HEAD_DIM // 2
    return jnp.concatenate([-a[..., half:], a[..., :half]], axis=-1).reshape(w.shape)


def _rope_tables(S):
    half = HEAD_DIM // 2
    inv_freq = ROPE_THETA ** (-jnp.arange(half, dtype=F32) / half)
    ang = jnp.arange(S, dtype=F32)[:, None] * inv_freq[None, :]
    c, s = jnp.cos(ang), jnp.sin(ang)
    c2, s2 = jnp.concatenate([c, c], axis=1), jnp.concatenate([s, s], axis=1)
    cos = jnp.stack([jnp.concatenate([c2, jnp.ones_like(c2)], axis=1), jnp.concatenate([c2, c2], axis=1)])
    sin = jnp.stack([jnp.concatenate([s2, jnp.zeros_like(s2)], axis=1), jnp.concatenate([s2, s2], axis=1)])
    return cos, sin


def _compress_kernel(x_ref, pea_ref, peb_ref, wa_ref, wb_ref, w2_ref, o_ref, pa_ref, pb0_ref, *, n_rows):
    u = pl.program_id(1)
    x = x_ref[0].astype(F32)
    pa = jnp.dot((x + pea_ref[...]).astype(BF16), wa_ref[...], preferred_element_type=F32)
    pb = jnp.dot((x + peb_ref[...]).astype(BF16), wb_ref[...], preferred_element_type=F32)

    def emit(slab, hid):
        y = jnp.dot(_gelu_tanh(hid).astype(BF16), w2_ref[...], preferred_element_type=F32)
        o_ref[0, pl.ds(pl.multiple_of(slab * n_rows, n_rows), n_rows), :] = y

    @pl.when(u == 0)
    def _():
        pb0_ref[...] = pb

    @pl.when(u > 0)
    def _():
        emit(u - 1, pa_ref[...] + pb)

    @pl.when(u == 3)
    def _():
        emit(3, pa + pltpu.roll(pb0_ref[...], n_rows - 1, axis=0))

    pa_ref[...] = pa


def nsa_compress(src, pe, w1, w2):
    B, S, W = src.shape
    G = NSA_GROUPS
    n_rows = S // 64
    half = CMP_BLOCK // 2
    cw = half * W
    xv = src.reshape(B, n_rows, 4 * cw)
    pe_flat = jnp.transpose(pe, (1, 0, 2)).reshape(CMP_BLOCK, W).astype(F32)
    pea, peb = pe_flat[:half].reshape(1, cw), pe_flat[half:].reshape(1, cw)
    eye = jnp.eye(G, dtype=F32)
    wfull = jnp.einsum('gldh,gk->lkdgh', w1.astype(F32), eye).reshape(CMP_BLOCK, W, G * CMP_HIDDEN)
    wa = wfull[:half].reshape(cw, G * CMP_HIDDEN).astype(BF16)
    wb = wfull[half:].reshape(cw, G * CMP_HIDDEN).astype(BF16)
    w2bd = jnp.einsum('ghd,gk->ghkd', w2.astype(F32), eye).reshape(G * CMP_HIDDEN, W).astype(BF16)
    const = lambda a: pl.BlockSpec(a.shape, lambda b, u: (0,) * a.ndim)
    return pl.pallas_call(
        functools.partial(_compress_kernel, n_rows=n_rows),
        grid=(B, 4),
        in_specs=[pl.BlockSpec((1, n_rows, cw), lambda b, u: (b, 0, u)),
                  const(pea), const(peb), const(wa), const(wb), const(w2bd)],
        out_specs=pl.BlockSpec((1, 4 * n_rows, W), lambda b, u: (b, 0, 0)),
        out_shape=jax.ShapeDtypeStruct((B, 4 * n_rows, W), F32),
        scratch_shapes=[pltpu.VMEM((n_rows, G * CMP_HIDDEN), F32), pltpu.VMEM((n_rows, G * CMP_HIDDEN), F32)],
        compiler_params=_cparams(("parallel", "arbitrary")),
    )(xv, pea, peb, wa, wb, w2bd)


def _nsa_attn_kernel(q_ref, kvs_ref, kvw_ref, kvc_ref, kvct_ref, gate_ref, bmat_ref, pmat_ref, o_ref,
                     *, tq, tk, seq):
    R = NSA_Q_PER_GROUP
    M = R * tq
    n_slc = seq // SLC_BLOCK
    n_cmp = 4 * n_slc
    n_sel = min(SLC_TOPK, n_slc)
    blocks_per_chunk = tk // SLC_BLOCK
    assert tq & (tq - 1) == 0 and n_slc & (n_slc - 1) == 0 and tk % tq == 0
    log_slc = n_slc.bit_length() - 1
    qi = pl.program_id(2)
    t0 = qi * tq

    qs = jnp.concatenate([q_ref[0, :, r * LANES:(r + 1) * LANES] for r in range(R)], axis=0)

    kvc = kvc_ref[0, 0]
    sT = lax.dot_general(kvc, qs, _NT, preferred_element_type=F32)
    rowc = lax.broadcasted_iota(jnp.int32, (n_cmp, 1), 0)
    cmp_end = (rowc & (n_slc - 1)) * SLC_BLOCK + (rowc >> log_slc) * CMP_STRIDE + (CMP_BLOCK - 1)
    tcol = t0 + (lax.broadcasted_iota(jnp.int32, (1, M), 1) & (tq - 1))
    sm = jnp.where(cmp_end <= tcol, sT, NEG_INF)
    mx = jnp.max(sm, axis=0, keepdims=True)
    e = jnp.exp2(sm - mx)
    inv = jnp.where(mx > 0.5 * NEG_INF, 1.0 / jnp.sum(e, axis=0, keepdims=True), 0.0)
    pT = e * inv
    o_c = jnp.dot(kvct_ref[0, 0], pT.astype(BF16), preferred_element_type=F32).T

    psum = pT[:, 0:tq]
    for r in range(1, R):
        psum = psum + pT[:, r * tq:(r + 1) * tq]
    p0, p1, p2, p3 = (psum[u * n_slc:(u + 1) * n_slc, :] for u in range(4))
    jrow = lax.broadcasted_iota(jnp.int32, (n_slc, tq), 0)
    p3_prev = jnp.where(jrow == 0, 0.0, pltpu.roll(p3, 1, axis=0))
    imp = p0 + p1 + p2 + 0.5 * p3 + 0.5 * p3_prev
    cur = (t0 + lax.broadcasted_iota(jnp.int32, (n_slc, tq), 1)) >> (SLC_BLOCK.bit_length() - 1)
    forced = (jrow == 0) | (jrow == cur) | (jrow == cur - 1)
    vals = jnp.where(forced, -jnp.inf, jnp.where(jrow <= cur, imp, -jnp.inf))
    sel0 = jnp.where(forced, 1.0, 0.0)
    n_free = n_sel - 3

    def pick(exact_ties):
        v, sel = vals, sel0
        for _ in range(n_free):
            m = jnp.max(v, axis=0, keepdims=True)
            if exact_ties:
                hit = jrow == jnp.min(jnp.where(v == m, jrow, n_slc), axis=0, keepdims=True)
            else:
                hit = v == jnp.where(m == -jnp.inf, jnp.nan, m)
            sel = jnp.where(hit, 1.0, sel)
            v = jnp.where(hit, -jnp.inf, v)
        return sel

    sel_fast = pick(False)
    n_cand = jnp.sum(jnp.where(vals > -jnp.inf, 1.0, 0.0), axis=0, keepdims=True)
    n_picked = jnp.sum(sel_fast - sel0, axis=0, keepdims=True)
    unique = jnp.min(jnp.where(n_picked == jnp.minimum(n_cand, float(n_free)), 1.0, 0.0)) > 0.5
    sel = lax.cond(unique, lambda: sel_fast, lambda: pick(True))
    sel_bias = ((sel.T - 1.0) * (-NEG_INF)).astype(BF16)

    n_parts = 2
    hp = R // n_parts
    mp = hp * tq
    qparts = [qs[i * mp:(i + 1) * mp] for i in range(n_parts)]
    qrow = t0 + (lax.broadcasted_iota(jnp.int32, (mp, 1), 0) & (tq - 1))
    kcol = lax.broadcasted_iota(jnp.int32, (1, tk), 1)
    n_chunks = seq // tk
    j_last = t0 // tk

    klane = lax.broadcasted_iota(jnp.int32, (tk, LANES), 1)

    def slc_step(j, carry, masked):
        kv = kvs_ref[0, pl.ds(pl.multiple_of(j * tk, tk), tk), :]
        k_sel = jnp.where(klane < HEAD_DIM, kv, bmat_ref[...]).astype(BF16)
        ones_v = jnp.where(klane < HEAD_DIM, 1.0, kv).astype(BF16)
        off = pl.multiple_of(blocks_per_chunk * (n_chunks - 1 - j), blocks_per_chunk)
        place = pmat_ref[pl.ds(off, n_slc), :].astype(BF16)
        q_bias = jnp.dot(sel_bias, place, preferred_element_type=F32).astype(BF16)
        q_bias = jnp.concatenate([q_bias] * hp, axis=0)
        new = []
        for qp, (m, acc) in zip(qparts, carry):
            s = lax.dot_general(qp + q_bias, k_sel, _NT, preferred_element_type=F32)
            if masked:
                s = jnp.where(j * tk + kcol <= qrow, s, NEG_INF)
            m_new = jnp.maximum(m, jnp.max(s, axis=1, keepdims=True))
            p = jnp.exp2(s - m_new).astype(BF16)
            acc = jnp.exp2(m - m_new) * acc + jnp.dot(p, ones_v, preferred_element_type=F32)
            new.append((m_new, acc))
        return tuple(new)

    init = (jnp.full((mp, 1), NEG_INF, F32), jnp.zeros((mp, LANES), F32))
    n_pairs = j_last // 2
    carry = lax.fori_loop(0, n_pairs, lambda i, c: slc_step(2 * i + 1, slc_step(2 * i, c, False), False),
                          (init,) * n_parts)
    carry = lax.fori_loop(2 * n_pairs, j_last, functools.partial(slc_step, masked=False), carry)
    carry = slc_step(j_last, carry, True)
    o_s = jnp.concatenate([acc / acc[:, 0:1] for (_, acc) in carry], axis=0)

    wlen = WINDOW + tq
    start = jnp.maximum(t0 - WINDOW, 0)
    kvw = kvw_ref[0, pl.ds(pl.multiple_of(start, tq), wlen), :]
    s_w = lax.dot_general(qs, kvw, _NT, preferred_element_type=F32)
    qpos = t0 + (lax.broadcasted_iota(jnp.int32, (M, 1), 0) & (tq - 1))
    kpos = start + lax.broadcasted_iota(jnp.int32, (1, wlen), 1)
    s_w = jnp.where(kpos <= qpos, jnp.where(kpos > qpos - WINDOW, s_w, NEG_INF), NEG_INF)
    p_w = jnp.exp2(s_w - jnp.max(s_w, axis=1, keepdims=True)).astype(BF16)
    wlane = lax.broadcasted_iota(jnp.int32, (wlen, LANES), 1)
    acc_w = jnp.dot(p_w, jnp.where(wlane < HEAD_DIM, 1.0, kvw).astype(BF16), preferred_element_type=F32)
    o_w = acc_w / acc_w[:, 0:1]

    gates = gate_ref[0]
    lane = lax.broadcasted_iota(jnp.int32, (tq, LANES), 1)
    comb = []
    for r in range(R):
        rs = slice(r * tq, (r + 1) * tq)
        comb.append(gates[:, r:r + 1] * o_c[rs] + gates[:, R + r:R + r + 1] * o_s[rs]
                    + gates[:, 2 * R + r:2 * R + r + 1] * o_w[rs])
    out = [jnp.where(lane < HEAD_DIM, pltpu.roll(comb[2 * i], HEAD_DIM, axis=1), comb[2 * i + 1])
           for i in range(R // 2)]
    o_ref[0] = jnp.concatenate(out, axis=1).astype(o_ref.dtype)


def nsa_attention(qkv, kvc, kvct, gates, *, tq=256, tk=1024):
    B, S, _ = qkv.shape
    G, R = NSA_GROUPS, NSA_Q_PER_GROUP
    tk = min(tk, S)
    n_slc = S // SLC_BLOCK
    n_cmp = kvc.shape[2]
    bpc = tk // SLC_BLOCK
    assert bpc <= LANES - HEAD_DIM
    off = bpc * (S // tk - 1)
    lane = np.arange(LANES)[None, :]
    bmat = jnp.asarray(lane - HEAD_DIM == np.arange(tk)[:, None] // SLC_BLOCK, BF16)
    pmat = jnp.asarray((np.arange(n_slc + off)[:, None] - off == lane - HEAD_DIM) & (lane >= HEAD_DIM)
                       & (lane < HEAD_DIM + bpc), F32)
    slc_blk0, win_blk0 = N_HEADS, N_HEADS + G
    return pl.pallas_call(
        functools.partial(_nsa_attn_kernel, tq=tq, tk=tk, seq=S),
        grid=(B, G, S // tq),
        in_specs=[pl.BlockSpec((1, tq, R * LANES), lambda b, g, i: (b, i, g)),
                  pl.BlockSpec((1, S, LANES), lambda b, g, i: (b, 0, slc_blk0 + g)),
                  pl.BlockSpec((1, S, LANES), lambda b, g, i: (b, 0, win_blk0 + g)),
                  pl.BlockSpec((1, 1, n_cmp, LANES), lambda b, g, i: (b, g, 0, 0)),
                  pl.BlockSpec((1, 1, LANES, n_cmp), lambda b, g, i: (b, g, 0, 0)),
                  pl.BlockSpec((1, tq, LANES), lambda b, g, i: (b, i, g)),
                  pl.BlockSpec(bmat.shape, lambda b, g, i: (0, 0)),
                  pl.BlockSpec(pmat.shape, lambda b, g, i: (0, 0))],
        out_specs=pl.BlockSpec((1, tq, R * HEAD_DIM), lambda b, g, i: (b, i, g)),
        out_shape=jax.ShapeDtypeStruct((B, S, ATTN_WIDTH), BF16),
        compiler_params=_cparams(("parallel", "parallel", "arbitrary")),
    )(qkv, qkv, qkv, kvc, kvct, gates, bmat, pmat)


def nsa_layer(h, gamma, w_in, pe_k, w1_k, w2_k, pe_v, w1_v, w2_v, w_out, B, S):
    G, R, hd, aw = NSA_GROUPS, NSA_Q_PER_GROUP, HEAD_DIM, ATTN_WIDTH
    kvd = G * hd
    sec = lambda i: w_in[:, aw + i * kvd: aw + (i + 1) * kvd]
    wq = w_in[:, :aw] * (hd ** -0.5 * LOG2E)
    wa = jnp.concatenate([_head_slots(wq, N_HEADS), _head_slots(sec(2), G, sec(3)),
                          _head_slots(sec(4), G, sec(5))], axis=1).astype(BF16)
    wb = jnp.concatenate([_head_slots(_rot_half_cols(wq), N_HEADS), _head_slots(_rot_half_cols(sec(2)), G),
                          _head_slots(_rot_half_cols(sec(4)), G)], axis=1).astype(BF16)
    cos, sin = _rope_tables(S)
    qkv = norm_matmul(h, gamma, wa, wb=wb, cos=cos, sin=sin)
    kc_src = norm_matmul(h, gamma, sec(0).astype(BF16), wb=_rot_half_cols(sec(0)).astype(BF16),
                         cos=cos, sin=sin, table_of_tile=lambda j: 1)
    vc_src = norm_matmul(h, gamma, sec(1).astype(BF16))
    wg = w_in[:, aw + 6 * kvd:].reshape(-1, 3, G, R)
    wg = jnp.transpose(wg, (0, 2, 1, 3)).reshape(-1, G, 3 * R)
    wg = jnp.pad(wg, ((0, 0), (0, 0), (0, LANES - 3 * R))).reshape(-1, G * LANES).astype(BF16)
    gates = norm_matmul(h, gamma, wg, act="sigmoid", out_dtype=F32)
    kc = nsa_compress(kc_src.reshape(B, S, kvd), pe_k, w1_k, w2_k)
    vc = nsa_compress(vc_src.reshape(B, S, kvd), pe_v, w1_v, w2_v)
    n_cmp = kc.shape[1]
    kvc = jnp.concatenate([kc.reshape(B, n_cmp, G, hd), vc.reshape(B, n_cmp, G, hd)], axis=-1)
    kvc = jnp.transpose(kvc, (0, 2, 1, 3)).astype(BF16)
    kvct = jnp.swapaxes(kvc, 2, 3)
    o = nsa_attention(qkv.reshape(B, S, -1), kvc, kvct, gates.reshape(B, S, G * LANES))
    return matmul_residual(o.reshape(B * S, aw), w_out.astype(BF16), h)


def kernel(x, l0_attn_norm, l0_w_in, l0_cmp_pe_k, l0_cmp_w1_k, l0_cmp_w2_k, l0_cmp_pe_v, l0_cmp_w1_v,
           l0_cmp_w2_v, l0_w_out, l0_ffn_norm, l0_peer_wq, l0_peer_keys, l0_peer_u, l0_peer_v,
           l1_attn_norm, l1_w_in, l1_f_bias, l1_w_out, l1_ffn_norm, l1_peer_wq, l1_peer_keys, l1_peer_u,
           l1_peer_v, final_norm):
    B, S, D = x.shape
    h = x.reshape(B * S, D)
    h = nsa_layer(h, l0_attn_norm, l0_w_in, l0_cmp_pe_k, l0_cmp_w1_k, l0_cmp_w2_k, l0_cmp_pe_v, l0_cmp_w1_v,
                  l0_cmp_w2_v, l0_w_out, B, S)
    h = peer_layer(h, l0_ffn_norm, l0_peer_wq, l0_peer_keys, l0_peer_u, l0_peer_v)
    h = fox_layer(h, l1_attn_norm, l1_w_in, l1_f_bias, l1_w_out, B, S)
    h = peer_layer(h, l1_ffn_norm, l1_peer_wq, l1_peer_keys, l1_peer_u, l1_peer_v)
    return rmsnorm(h, final_norm).reshape(B, S, D)
```

```python
import functools

import numpy as np
import jax
import jax.numpy as jnp
from jax import lax
from jax.experimental import pallas as pl
from jax.experimental.pallas import tpu as pltpu

F32 = jnp.float32
BF16 = jnp.bfloat16

D_MODEL = 1024
N_HEADS = 16
HEAD_DIM = 64
ATTN_WIDTH = N_HEADS * HEAD_DIM
NSA_GROUPS = 4
NSA_Q_PER_GROUP = N_HEADS // NSA_GROUPS
CMP_BLOCK = 32
CMP_STRIDE = 16
CMP_HIDDEN = 2 * HEAD_DIM
SLC_BLOCK = 64
SLC_TOPK = 16
WINDOW = 512
FORCE_SCORE = 1.0e4
ROPE_THETA = 10000.0
PEER_HEADS = 8
PEER_N_KEYS = 128
PEER_TOPK = 16
PEER_HALF_DIM = 128
RMS_EPS = 1e-6
NEG_INF = -1e30
LOG2E = 1.4426950408889634

LANES = 128
VMEM_LIMIT_BYTES = 56 * 1024 * 1024

_NT = (((1,), (1,)), ((), ()))


def _cparams(sem, vmem=VMEM_LIMIT_BYTES, flags=None):
    return pltpu.CompilerParams(dimension_semantics=sem, vmem_limit_bytes=vmem, flags=flags)


def _gelu_tanh(x):
    return 0.5 * x * (1.0 + jnp.tanh(0.7978845608028654 * (x + 0.044715 * (x * x * x))))


def _gelu_sigmoid(x):
    c = -2.0 * 0.7978845608028654 * LOG2E
    t = x * (c + (c * 0.044715) * (x * x))
    return x / (1.0 + jnp.exp2(t))


def _rms_rows(x, g):
    ms = jnp.mean(x * x, axis=-1, keepdims=True)
    return x * lax.rsqrt(ms + RMS_EPS) * g


def _norm_mm_kernel(*refs, act, has_bias, rope, emit_xn):
    it = iter(refs)
    x_ref, g_ref = next(it), next(it)
    wa_ref = next(it)
    wb_ref = next(it) if rope else None
    cos_ref = next(it) if rope else None
    sin_ref = next(it) if rope else None
    b_ref = next(it) if has_bias else None
    o_ref = next(it)
    xo_ref = next(it) if emit_xn else None
    xn_ref = next(it)

    @pl.when(pl.program_id(1) == 0)
    def _():
        xn = _rms_rows(x_ref[...], g_ref[...])
        xn_ref[...] = xn.astype(BF16)
        if emit_xn:
            xo_ref[...] = xn.T.astype(BF16)

    xn = xn_ref[...]
    y = jnp.dot(xn, wa_ref[...], preferred_element_type=F32)
    if rope:
        yb = jnp.dot(xn, wb_ref[...], preferred_element_type=F32)
        cos, sin = cos_ref[...], sin_ref[...]
        for s in range(y.shape[1] // LANES):
            sl = slice(s * LANES, (s + 1) * LANES)
            o_ref[:, sl] = (y[:, sl] * cos + yb[:, sl] * sin).astype(o_ref.dtype)
        return
    if has_bias:
        y = y + b_ref[...]
    if act == "sigmoid":
        y = jax.nn.sigmoid(y)
    elif act == "log_sigmoid":
        y = jax.nn.log_sigmoid(y)
    o_ref[...] = y.astype(o_ref.dtype)


def norm_matmul(x, gamma, wa, *, wb=None, cos=None, sin=None, table_of_tile=None, bias=None,
                act=None, out_dtype=BF16, emit_xn=False, tm=1024, tn=1024):
    T, D = x.shape
    N = wa.shape[1]
    tm, tn = min(tm, T), min(tn, N)
    assert T % tm == 0 and N % tn == 0 and tn % LANES == 0
    rope = wb is not None
    in_specs = [pl.BlockSpec((tm, D), lambda i, j: (i, 0)),
                pl.BlockSpec((1, D), lambda i, j: (0, 0)),
                pl.BlockSpec((D, tn), lambda i, j: (0, j))]
    args = [x, gamma.reshape(1, D).astype(F32), wa]
    if rope:
        S = cos.shape[1]
        assert S % tm == 0
        n_pos = S // tm
        tmap = table_of_tile if table_of_tile is not None else (lambda j: 0)
        in_specs += [pl.BlockSpec((D, tn), lambda i, j: (0, j)),
                     pl.BlockSpec((None, tm, LANES), lambda i, j: (tmap(j), i % n_pos, 0)),
                     pl.BlockSpec((None, tm, LANES), lambda i, j: (tmap(j), i % n_pos, 0))]
        args += [wb, cos, sin]
    if bias is not None:
        in_specs.append(pl.BlockSpec((1, tn), lambda i, j: (0, j)))
        args.append(bias.reshape(1, N).astype(F32))
    out_shape = [jax.ShapeDtypeStruct((T, N), out_dtype)]
    out_specs = [pl.BlockSpec((tm, tn), lambda i, j: (i, j))]
    if emit_xn:
        out_shape.append(jax.ShapeDtypeStruct((D, T), BF16))
        out_specs.append(pl.BlockSpec((D, tm), lambda i, j: (0, i)))
    res = pl.pallas_call(
        functools.partial(_norm_mm_kernel, act=act, has_bias=bias is not None, rope=rope, emit_xn=emit_xn),
        grid=(T // tm, N // tn),
        in_specs=in_specs,
        out_specs=out_specs,
        out_shape=out_shape,
        scratch_shapes=[pltpu.VMEM((tm, D), BF16)],
        compiler_params=_cparams(("parallel", "arbitrary")),
    )(*args)
    return res if emit_xn else res[0]


def _mm_res_kernel(a_ref, w_ref, r_ref, o_ref):
    o_ref[...] = r_ref[...] + jnp.dot(a_ref[...], w_ref[...], preferred_element_type=F32)


def matmul_residual(a, w, res, *, tm=1024, tn=1024):
    T, K = a.shape
    N = w.shape[1]
    tm, tn = min(tm, T), min(tn, N)
    assert T % tm == 0 and N % tn == 0
    return pl.pallas_call(
        _mm_res_kernel,
        grid=(T // tm, N // tn),
        in_specs=[pl.BlockSpec((tm, K), lambda i, j: (i, 0)),
                  pl.BlockSpec((K, tn), lambda i, j: (0, j)),
                  pl.BlockSpec((tm, tn), lambda i, j: (i, j))],
        out_specs=pl.BlockSpec((tm, tn), lambda i, j: (i, j)),
        out_shape=jax.ShapeDtypeStruct((T, N), F32),
        compiler_params=_cparams(("parallel", "arbitrary")),
    )(a, w, res)


def _rmsnorm_kernel(x_ref, g_ref, o_ref):
    o_ref[...] = _rms_rows(x_ref[...], g_ref[...])


def rmsnorm(x, gamma, *, tm=1024):
    T, D = x.shape
    tm = min(tm, T)
    return pl.pallas_call(
        _rmsnorm_kernel,
        grid=(T // tm,),
        in_specs=[pl.BlockSpec((tm, D), lambda i: (i, 0)), pl.BlockSpec((1, D), lambda i: (0, 0))],
        out_specs=pl.BlockSpec((tm, D), lambda i: (i, 0)),
        out_shape=jax.ShapeDtypeStruct((T, D), F32),
        compiler_params=_cparams(("parallel",)),
    )(x, gamma.reshape(1, D).astype(F32))


def _peer_cand_tables(tn):
    fidx, vmask = [], []
    for k2 in range(16):
        fidx.append(k2); vmask.append(0.0)
    for k1 in range(1, 8):
        lim = PEER_TOPK // (k1 + 1)
        for k2 in range(8):
            fidx.append(k1 * 16 + k2); vmask.append(0.0 if k2 < lim else -np.inf)
    for k1 in range(8, 16):
        fidx.append(k1 * 16); vmask.append(0.0)
    fidx = np.broadcast_to(np.asarray(fidx, np.int32)[:, None], (80, tn))
    vmask = np.broadcast_to(np.asarray(vmask, np.float32)[:, None], (80, tn))
    return jnp.asarray(fidx), jnp.asarray(vmask)


def _top16_rows(s, exact_ties):
    n, tn = s.shape
    rows = lax.broadcasted_iota(jnp.int32, (n, tn), 0)
    rows16 = lax.broadcasted_iota(jnp.int32, (PEER_TOPK, tn), 0)
    rank = jnp.full((n, tn), float(PEER_TOPK), F32)
    tops = jnp.zeros((PEER_TOPK, tn), F32)
    v = s
    for k in range(PEER_TOPK):
        m = jnp.max(v, axis=0, keepdims=True)
        if exact_ties:
            hit = rows == jnp.min(jnp.where(v == m, rows, n), axis=0, keepdims=True)
        else:
            hit = v == m
        rank = jnp.where(hit, float(k), rank)
        v = jnp.where(hit, -jnp.inf, v)
        tops = jnp.where(rows16 == k, m, tops)
    return tops, rank, v


def _peer_select_head(q_ref, keys_ref, fidx, vmask, exact_ties):
    tops, ranks, es, picked = [], [], [], []
    for p in range(2):
        q = q_ref[:, p * PEER_HALF_DIM:(p + 1) * PEER_HALF_DIM]
        s = lax.dot_general(keys_ref[p], q, _NT, preferred_element_type=F32)
        t, r, v = _top16_rows(s, exact_ties)
        tops.append(t); ranks.append(r)
        es.append(jnp.exp(s - t[0:1, :]))
        picked.append(jnp.sum(jnp.where(v == -jnp.inf, 1.0, 0.0), axis=0, keepdims=True))
    ts1, ts2 = tops
    pieces = [ts1[0:1, :] + ts2]
    for k1 in range(1, 8):
        pieces.append(ts1[k1:k1 + 1, :] + ts2[0:8, :])
    pieces.append(ts1[8:16, :] + ts2[0:1, :])
    cand0 = jnp.concatenate(pieces, axis=0) + vmask
    cand = cand0
    for _ in range(PEER_TOPK):
        m = jnp.max(cand, axis=0, keepdims=True)
        if exact_ties:
            hit = fidx == jnp.min(jnp.where(cand == m, fidx, 4096), axis=0, keepdims=True)
        else:
            hit = cand == m
        cand = jnp.where(hit, -jnp.inf, cand)
    taken = jnp.logical_and(cand == -jnp.inf, vmask == 0.0)
    takenf = taken.astype(F32)
    picked.append(jnp.sum(takenf, axis=0, keepdims=True))
    unique = jnp.min(jnp.where((picked[0] == PEER_TOPK) & (picked[1] == PEER_TOPK) & (picked[2] == PEER_TOPK),
                               1.0, 0.0)) > 0.5
    best = ts1[0:1, :] + ts2[0:1, :]
    z = jnp.sum(jnp.where(taken, jnp.exp(cand0 - best), 0.0), axis=0, keepdims=True)
    counts = [jnp.sum(takenf[0:16, :], axis=0, keepdims=True)]
    for k1 in range(1, 8):
        counts.append(jnp.sum(takenf[16 + 8 * (k1 - 1):16 + 8 * k1, :], axis=0, keepdims=True))
    tail = takenf[72:80, :]
    cnt = jnp.zeros_like(ranks[0])
    for k1 in range(PEER_TOPK):
        nk = counts[k1] if k1 < 8 else tail[k1 - 8:k1 - 7, :]
        cnt = jnp.where(ranks[0] == float(k1), nk, cnt)
    return (cnt, ranks[1], es[0], es[1] / z), unique


def _peer_select_kernel(q_ref, keys_ref, fidx_ref, vmask_ref, cnt_ref, rank2_ref, e1_ref, e2_ref):
    fidx = fidx_ref[...]
    vmask = vmask_ref[...]

    def store(vals):
        for ref, val in zip((cnt_ref, rank2_ref, e1_ref, e2_ref), vals):
            ref[...] = val.astype(ref.dtype)

    vals, unique = _peer_select_head(q_ref, keys_ref, fidx, vmask, exact_ties=False)
    store(vals)

    @pl.when(jnp.logical_not(unique))
    def _():
        store(_peer_select_head(q_ref, keys_ref, fidx, vmask, exact_ties=True)[0])


def peer_select(q, keys, *, tn=512):
    T = q.shape[0]
    fidx, vmask = _peer_cand_tables(tn)
    rows = PEER_HEADS * PEER_N_KEYS
    ospec = pl.BlockSpec((PEER_N_KEYS, tn), lambda i, h: (h, i))
    return pl.pallas_call(
        _peer_select_kernel,
        grid=(T // tn, PEER_HEADS),
        in_specs=[pl.BlockSpec((tn, 2 * PEER_HALF_DIM), lambda i, h: (i, h)),
                  pl.BlockSpec((2, PEER_N_KEYS, PEER_HALF_DIM), lambda i, h: (h, 0, 0)),
                  pl.BlockSpec((80, tn), lambda i, h: (0, 0)),
                  pl.BlockSpec((80, tn), lambda i, h: (0, 0))],
        out_specs=[ospec] * 4,
        out_shape=[jax.ShapeDtypeStruct((rows, T), dt) for dt in (F32, BF16, F32, BF16)],
        compiler_params=_cparams(("parallel", "parallel")),
    )(q, keys, fidx, vmask)


def _peer_dense_kernel(xn_ref, u_ref, vt_ref, cnt_ref, rank2_ref, e1_ref, e2_ref, res_ref, o_ref,
                       acc_ref, g0_ref, g1_ref, *, c_per_step):
    j = pl.program_id(1)
    n_tiles = pl.num_programs(1) - 1

    @pl.when(j == 0)
    def _():
        acc_ref[...] = jnp.zeros_like(acc_ref)
        g1_ref[...] = jnp.zeros_like(g1_ref)

    @pl.when(j % 2 == 0)
    def _():
        _peer_dense_step(xn_ref, u_ref, vt_ref, cnt_ref, rank2_ref, e1_ref, e2_ref, acc_ref,
                         g1_ref, g0_ref, j, c_per_step)

    @pl.when(j % 2 == 1)
    def _():
        _peer_dense_step(xn_ref, u_ref, vt_ref, cnt_ref, rank2_ref, e1_ref, e2_ref, acc_ref,
                         g0_ref, g1_ref, j, c_per_step)

    @pl.when(j == n_tiles)
    def _():
        o_ref[...] = res_ref[...] + acc_ref[...].T


def _peer_dense_step(xn_ref, u_ref, vt_ref, cnt_ref, rank2_ref, e1_ref, e2_ref, acc_ref, g_ref, g_next_ref,
                     j, c_per_step):

    tn = xn_ref.shape[1]
    bf16_rows = 16
    reps = PEER_N_KEYS // bf16_rows

    def row_tile(ref, row):
        r16 = jnp.broadcast_to(ref[pl.ds(row, 1), :], (bf16_rows, tn)).astype(BF16)
        return jnp.concatenate([r16] * reps, axis=0)

    c0 = jnp.maximum(j - 1, 0) * c_per_step
    up_rows = 2 * PEER_N_KEYS
    blocks = []
    for cc in range(c_per_step):
        if (cc * PEER_N_KEYS) % up_rows == 0:
            rs = slice(cc * PEER_N_KEYS, cc * PEER_N_KEYS + up_rows)
            hT = jnp.dot(u_ref[rs, :], xn_ref[...], preferred_element_type=F32)
            g_next_ref[rs, :] = _gelu_sigmoid(hT).astype(BF16)
        c = c0 + cc
        w = None
        for h in range(PEER_HEADS):
            row = h * PEER_N_KEYS + c
            n_row = row_tile(cnt_ref, row)
            e1_row = row_tile(e1_ref, row)
            sl = slice(h * PEER_N_KEYS, (h + 1) * PEER_N_KEYS)
            term = jnp.where(rank2_ref[sl, :] < n_row, e2_ref[sl, :], 0.0) * e1_row
            w = term if w is None else w + term
        blocks.append(w * g_ref[cc * PEER_N_KEYS:(cc + 1) * PEER_N_KEYS, :])
    aT = jnp.concatenate(blocks, axis=0)
    acc_ref[...] += jnp.dot(vt_ref[...], aT, preferred_element_type=F32)


def peer_dense(xn, u, vt, cnt, rank2, e1, e2, res, *, tn=512, te=2048):
    D, T = xn.shape
    E = u.shape[0]
    tn = min(tn, T)
    rows = PEER_HEADS * PEER_N_KEYS
    once = pl.Buffered(1)
    sel_spec = pl.BlockSpec((rows, tn), lambda i, j: (0, i))
    n_tiles = E // te
    return pl.pallas_call(
        functools.partial(_peer_dense_kernel, c_per_step=te // PEER_N_KEYS),
        grid=(T // tn, n_tiles + 1),
        in_specs=[pl.BlockSpec((D, tn), lambda i, j: (0, i)),
                  pl.BlockSpec((te, D), lambda i, j: (jnp.minimum(j, n_tiles - 1), 0)),
                  pl.BlockSpec((D, te), lambda i, j: (0, jnp.maximum(j - 1, 0))),
                  sel_spec, sel_spec, sel_spec, sel_spec,
                  pl.BlockSpec((tn, D), lambda i, j: (i, 0), pipeline_mode=once)],
        out_specs=pl.BlockSpec((tn, D), lambda i, j: (i, 0), pipeline_mode=once),
        out_shape=jax.ShapeDtypeStruct((T, D), F32),
        scratch_shapes=[pltpu.VMEM((D, tn), F32), pltpu.VMEM((te, tn), BF16), pltpu.VMEM((te, tn), BF16)],
        compiler_params=_cparams(("parallel", "arbitrary")),
    )(xn, u, vt, cnt, rank2, e1, e2, res)


def peer_layer(h, gamma, w_q, sub_keys, u, v):
    q, xn = norm_matmul(h, gamma, w_q.astype(BF16), emit_xn=True)
    keys = sub_keys.reshape(2 * PEER_HEADS, PEER_N_KEYS, PEER_HALF_DIM).astype(BF16)
    cnt, rank2, e1, e2 = peer_select(q, keys)
    return peer_dense(xn, u.astype(BF16), v.T.astype(BF16), cnt, rank2, e1, e2, h)


def _cumsum_aug_kernel(lf_ref, tri_ref, place_q_ref, place_k_ref, ones_q_ref, ones_k_ref,
                       qa_ref, ka_ref, carry_ref):
    @pl.when(pl.program_id(1) == 0)
    def _():
        carry_ref[...] = jnp.zeros_like(carry_ref)

    lf = lf_ref[0]
    c = jnp.dot(tri_ref[...], lf, preferred_element_type=F32, precision=lax.Precision.HIGHEST) + carry_ref[...]
    carry_ref[...] = c[-1:, :]
    c = c * LOG2E
    hi = c.astype(BF16)
    r1 = c - hi.astype(F32)
    mid = r1.astype(BF16)
    lo = (r1 - mid.astype(F32)).astype(BF16)
    nh = N_HEADS
    lane = lax.broadcasted_iota(jnp.int32, c.shape, 1)
    parts = jnp.where(lane < nh, hi.astype(F32),
                      jnp.where(lane < 2 * nh, pltpu.roll(mid.astype(F32), nh, axis=1),
                                pltpu.roll(lo.astype(F32), 2 * nh, axis=1)))
    parts = jnp.where(lane < 3 * nh, parts, 0.0).astype(BF16)
    qa_ref[0] = (jnp.dot(parts, place_q_ref[...], preferred_element_type=F32) + ones_q_ref[...]).astype(BF16)
    ka_ref[0] = (jnp.dot(parts, place_k_ref[...], preferred_element_type=F32) + ones_k_ref[...]).astype(BF16)


def fox_bias_operands(logf, *, tc=512):
    B, S, _ = logf.shape
    nh = N_HEADS
    tri = jnp.asarray(np.tril(np.ones((tc, tc), np.float32)))
    pq = np.zeros((LANES, nh * LANES), np.float32)
    pk = np.zeros((LANES, nh * LANES), np.float32)
    oq = np.zeros((1, nh * LANES), np.float32)
    ok = np.zeros((1, nh * LANES), np.float32)
    for h in range(nh):
        for part in range(3):
            pq[part * nh + h, h * LANES + part] = 1.0
            pk[part * nh + h, h * LANES + 3 + part] = -1.0
            oq[0, h * LANES + 3 + part] = 1.0
            ok[0, h * LANES + part] = 1.0
    const = lambda a: pl.BlockSpec(a.shape, lambda b, i: (0,) * a.ndim)
    pq, pk, oq, ok = jnp.asarray(pq, BF16), jnp.asarray(pk, BF16), jnp.asarray(oq), jnp.asarray(ok)
    out = jax.ShapeDtypeStruct((B, S, nh * LANES), BF16)
    return pl.pallas_call(
        _cumsum_aug_kernel,
        grid=(B, S // tc),
        in_specs=[pl.BlockSpec((1, tc, LANES), lambda b, i: (b, i, 0)),
                  const(tri), const(pq), const(pk), const(oq), const(ok)],
        out_specs=[pl.BlockSpec((1, tc, nh * LANES), lambda b, i: (b, i, 0))] * 2,
        out_shape=[out, out],
        scratch_shapes=[pltpu.VMEM((1, LANES), F32)],
        compiler_params=_cparams(("parallel", "arbitrary")),
    )(logf, tri, pq, pk, oq, ok)


def _fox_attn_kernel(q_ref, qa_ref, kv_ref, ka_ref, o_ref, *, tq, tk, heads_per_step):
    qi = pl.program_id(2)
    t0 = qi * tq
    n_full = t0 // tk
    n_diag = tq // tk
    lanes = [slice(hh * LANES, (hh + 1) * LANES) for hh in range(heads_per_step)]
    qs = [jnp.concatenate([q_ref[0, :, lsl], qa_ref[0, :, lsl]], axis=1) for lsl in lanes]

    def step(j, carry, masked):
        rows = pl.ds(pl.multiple_of(j * tk, tk), tk)
        klane = lax.broadcasted_iota(jnp.int32, (tk, LANES), 1)
        new = []
        for lsl, q, (m, acc) in zip(lanes, qs, carry):
            kv = kv_ref[0, rows, lsl]
            kk = jnp.concatenate([kv, ka_ref[0, rows, lsl]], axis=1)
            ones_v = jnp.where(klane < HEAD_DIM, 1.0, kv).astype(BF16)
            s = lax.dot_general(q, kk, _NT, preferred_element_type=F32)
            if masked:
                qpos = t0 + lax.broadcasted_iota(jnp.int32, (tq, 1), 0)
                kpos = j * tk + lax.broadcasted_iota(jnp.int32, (1, tk), 1)
                s = jnp.where(kpos <= qpos, s, NEG_INF)
            m_new = jnp.maximum(m, jnp.max(s, axis=1, keepdims=True))
            p = jnp.exp2(s - m_new).astype(BF16)
            acc = jnp.exp2(m - m_new) * acc + jnp.dot(p, ones_v, preferred_element_type=F32)
            new.append((m_new, acc))
        return tuple(new)

    init = (jnp.full((tq, 1), NEG_INF, F32), jnp.zeros((tq, LANES), F32))
    n_pairs = n_full // 2
    carry = lax.fori_loop(0, n_pairs, lambda i, c: step(2 * i + 1, step(2 * i, c, False), False),
                          (init,) * heads_per_step)
    carry = lax.fori_loop(2 * n_pairs, n_full, functools.partial(step, masked=False), carry)
    for d in range(n_diag):
        carry = step(n_full + d, carry, True)
    outs = [acc / acc[:, 0:1] for (_, acc) in carry]
    lane = lax.broadcasted_iota(jnp.int32, (tq, LANES), 1)
    blocks = []
    for pair in range(heads_per_step // 2):
        a, b = outs[2 * pair], outs[2 * pair + 1]
        blocks.append(jnp.where(lane < HEAD_DIM, pltpu.roll(a, HEAD_DIM, axis=1), b))
    o_ref[0] = jnp.concatenate(blocks, axis=1).astype(o_ref.dtype) if len(blocks) > 1 else blocks[0].astype(o_ref.dtype)


def _head_slots(w, n_heads, second=None):
    D = w.shape[0]
    a = w.reshape(D, n_heads, HEAD_DIM)
    b = jnp.zeros_like(a) if second is None else second.reshape(D, n_heads, HEAD_DIM)
    return jnp.concatenate([a, b], axis=-1).reshape(D, n_heads * LANES)


def fox_layer(h, gamma, w_in, f_bias, w_out, B, S):
    aw = ATTN_WIDTH
    wq = _head_slots(w_in[:, :aw] * (HEAD_DIM ** -0.5 * LOG2E), N_HEADS)
    wkv = _head_slots(w_in[:, aw:2 * aw], N_HEADS, w_in[:, 2 * aw:3 * aw])
    w_main = jnp.concatenate([wq, wkv], axis=1).astype(BF16)
    wf = jnp.pad(w_in[:, 3 * aw:], ((0, 0), (0, LANES - N_HEADS))).astype(BF16)
    bf = jnp.pad(f_bias.astype(F32), (0, LANES - N_HEADS))
    qkv = norm_matmul(h, gamma, w_main)
    logf = norm_matmul(h, gamma, wf, bias=bf, act="log_sigmoid", out_dtype=F32)
    qa, ka = fox_bias_operands(logf.reshape(B, S, LANES))
    qkv = qkv.reshape(B, S, 2 * N_HEADS * LANES)
    o = fox_attention(qkv, qa, ka)
    return matmul_residual(o.reshape(B * S, aw), w_out.astype(BF16), h)


def fox_attention(qkv, qa, ka, *, tq=1024, tk=1024, heads_per_step=2):
    B, S, _ = qkv.shape
    hs = heads_per_step
    wq = hs * LANES
    n_qblk = N_HEADS // hs
    return pl.pallas_call(
        functools.partial(_fox_attn_kernel, tq=tq, tk=tk, heads_per_step=hs),
        grid=(B, n_qblk, S // tq),
        in_specs=[pl.BlockSpec((1, tq, wq), lambda b, h, i: (b, i, h)),
                  pl.BlockSpec((1, tq, wq), lambda b, h, i: (b, i, h)),
                  pl.BlockSpec((1, S, wq), lambda b, h, i: (b, 0, n_qblk + h)),
                  pl.BlockSpec((1, S, wq), lambda b, h, i: (b, 0, h))],
        out_specs=pl.BlockSpec((1, tq, hs * HEAD_DIM), lambda b, h, i: (b, i, h)),
        out_shape=jax.ShapeDtypeStruct((B, S, ATTN_WIDTH), BF16),
        compiler_params=_cparams(("parallel", "parallel", "arbitrary")),
    )(qkv, qa, qkv, ka)


def _rot_half_cols(w):
    D = w.shape[0]
    a = w.reshape(D, -1, HEAD_DIM)
    half = HEAD_DIM // 2
    return jnp.concatenate([-a[..., half:], a[..., :half]], axis=-1).reshape(w.shape)


def _rope_tables(S):
    half = HEAD_DIM // 2
    inv_freq = ROPE_THETA ** (-jnp.arange(half, dtype=F32) / half)
    ang = jnp.arange(S, dtype=F32)[:, None] * inv_freq[None, :]
    c, s = jnp.cos(ang), jnp.sin(ang)
    c2, s2 = jnp.concatenate([c, c], axis=1), jnp.concatenate([s, s], axis=1)
    cos = jnp.stack([jnp.concatenate([c2, jnp.ones_like(c2)], axis=1), jnp.concatenate([c2, c2], axis=1)])
    sin = jnp.stack([jnp.concatenate([s2, jnp.zeros_like(s2)], axis=1), jnp.concatenate([s2, s2], axis=1)])
    return cos, sin


def _compress_kernel(x_ref, pea_ref, peb_ref, wa_ref, wb_ref, w2_ref, o_ref, pa_ref, pb0_ref, *, n_rows):
    u = pl.program_id(1)
    x = x_ref[0].astype(F32)
    pa = jnp.dot((x + pea_ref[...]).astype(BF16), wa_ref[...], preferred_element_type=F32)
    pb = jnp.dot((x + peb_ref[...]).astype(BF16), wb_ref[...], preferred_element_type=F32)

    def emit(slab, hid):
        y = jnp.dot(_gelu_tanh(hid).astype(BF16), w2_ref[...], preferred_element_type=F32)
        o_ref[0, pl.ds(pl.multiple_of(slab * n_rows, n_rows), n_rows), :] = y

    @pl.when(u == 0)
    def _():
        pb0_ref[...] = pb

    @pl.when(u > 0)
    def _():
        emit(u - 1, pa_ref[...] + pb)

    @pl.when(u == 3)
    def _():
        emit(3, pa + pltpu.roll(pb0_ref[...], n_rows - 1, axis=0))

    pa_ref[...] = pa


def nsa_compress(src, pe, w1, w2):
    B, S, W = src.shape
    G = NSA_GROUPS
    n_rows = S // 64
    half = CMP_BLOCK // 2
    cw = half * W
    xv = src.reshape(B, n_rows, 4 * cw)
    pe_flat = jnp.transpose(pe, (1, 0, 2)).reshape(CMP_BLOCK, W).astype(F32)
    pea, peb = pe_flat[:half].reshape(1, cw), pe_flat[half:].reshape(1, cw)
    eye = jnp.eye(G, dtype=F32)
    wfull = jnp.einsum('gldh,gk->lkdgh', w1.astype(F32), eye).reshape(CMP_BLOCK, W, G * CMP_HIDDEN)
    wa = wfull[:half].reshape(cw, G * CMP_HIDDEN).astype(BF16)
    wb = wfull[half:].reshape(cw, G * CMP_HIDDEN).astype(BF16)
    w2bd = jnp.einsum('ghd,gk->ghkd', w2.astype(F32), eye).reshape(G * CMP_HIDDEN, W).astype(BF16)
    const = lambda a: pl.BlockSpec(a.shape, lambda b, u: (0,) * a.ndim)
    return pl.pallas_call(
        functools.partial(_compress_kernel, n_rows=n_rows),
        grid=(B, 4),
        in_specs=[pl.BlockSpec((1, n_rows, cw), lambda b, u: (b, 0, u)),
                  const(pea), const(peb), const(wa), const(wb), const(w2bd)],
        out_specs=pl.BlockSpec((1, 4 * n_rows, W), lambda b, u: (b, 0, 0)),
        out_shape=jax.ShapeDtypeStruct((B, 4 * n_rows, W), F32),
        scratch_shapes=[pltpu.VMEM((n_rows, G * CMP_HIDDEN), F32), pltpu.VMEM((n_rows, G * CMP_HIDDEN), F32)],
        compiler_params=_cparams(("parallel", "arbitrary")),
    )(xv, pea, peb, wa, wb, w2bd)


def _nsa_attn_kernel(q_ref, kvs_ref, kvw_ref, kvc_ref, kvct_ref, gate_ref, bmat_ref, pmat_ref, o_ref,
                     *, tq, tk, seq):
    R = NSA_Q_PER_GROUP
    M = R * tq
    n_slc = seq // SLC_BLOCK
    n_cmp = 4 * n_slc
    n_sel = min(SLC_TOPK, n_slc)
    blocks_per_chunk = tk // SLC_BLOCK
    assert tq & (tq - 1) == 0 and n_slc & (n_slc - 1) == 0 and tk % tq == 0
    log_slc = n_slc.bit_length() - 1
    qi = pl.program_id(2)
    t0 = qi * tq

    qs = jnp.concatenate([q_ref[0, :, r * LANES:(r + 1) * LANES] for r in range(R)], axis=0)

    kvc = kvc_ref[0, 0]
    sT = lax.dot_general(kvc, qs, _NT, preferred_element_type=F32)
    rowc = lax.broadcasted_iota(jnp.int32, (n_cmp, 1), 0)
    cmp_end = (rowc & (n_slc - 1)) * SLC_BLOCK + (rowc >> log_slc) * CMP_STRIDE + (CMP_BLOCK - 1)
    tcol = t0 + (lax.broadcasted_iota(jnp.int32, (1, M), 1) & (tq - 1))
    sm = jnp.where(cmp_end <= tcol, sT, NEG_INF)
    mx = jnp.max(sm, axis=0, keepdims=True)
    e = jnp.exp2(sm - mx)
    inv = jnp.where(mx > 0.5 * NEG_INF, 1.0 / jnp.sum(e, axis=0, keepdims=True), 0.0)
    pT = e * inv
    o_c = jnp.dot(kvct_ref[0, 0], pT.astype(BF16), preferred_element_type=F32).T

    wlen = WINDOW + tq
    start = jnp.maximum(t0 - WINDOW, 0)
    kvw = kvw_ref[0, pl.ds(pl.multiple_of(start, tq), wlen), :]
    s_w = lax.dot_general(qs, kvw, _NT, preferred_element_type=F32)
    qpos = t0 + (lax.broadcasted_iota(jnp.int32, (M, 1), 0) & (tq - 1))
    kpos = start + lax.broadcasted_iota(jnp.int32, (1, wlen), 1)
    s_w = jnp.where(kpos <= qpos, jnp.where(kpos > qpos - WINDOW, s_w, NEG_INF), NEG_INF)
    p_w = jnp.exp2(s_w - jnp.max(s_w, axis=1, keepdims=True)).astype(BF16)
    wlane = lax.broadcasted_iota(jnp.int32, (wlen, LANES), 1)
    acc_w = jnp.dot(p_w, jnp.where(wlane < HEAD_DIM, 1.0, kvw).astype(BF16), preferred_element_type=F32)
    o_w = acc_w / acc_w[:, 0:1]

    psum = pT[:, 0:tq]
    for r in range(1, R):
        psum = psum + pT[:, r * tq:(r + 1) * tq]
    p0, p1, p2, p3 = (psum[u * n_slc:(u + 1) * n_slc, :] for u in range(4))
    jrow = lax.broadcasted_iota(jnp.int32, (n_slc, tq), 0)
    p3_prev = jnp.where(jrow == 0, 0.0, pltpu.roll(p3, 1, axis=0))
    imp = p0 + p1 + p2 + 0.5 * p3 + 0.5 * p3_prev
    cur = (t0 + lax.broadcasted_iota(jnp.int32, (n_slc, tq), 1)) >> (SLC_BLOCK.bit_length() - 1)
    forced = (jrow == 0) | (jrow == cur) | (jrow == cur - 1)
    vals = jnp.where(forced, -jnp.inf, jnp.where(jrow <= cur, imp, -jnp.inf))
    sel0 = jnp.where(forced, 1.0, 0.0)
    n_free = n_sel - 3

    def pick(exact_ties):
        v, sel = vals, sel0
        for _ in range(n_free):
            m = jnp.max(v, axis=0, keepdims=True)
            if exact_ties:
                hit = jrow == jnp.min(jnp.where(v == m, jrow, n_slc), axis=0, keepdims=True)
            else:
                hit = v == jnp.where(m == -jnp.inf, jnp.nan, m)
            sel = jnp.where(hit, 1.0, sel)
            v = jnp.where(hit, -jnp.inf, v)
        return sel

    sel_fast = pick(False)
    n_cand = jnp.sum(jnp.where(vals > -jnp.inf, 1.0, 0.0), axis=0, keepdims=True)
    n_picked = jnp.sum(sel_fast - sel0, axis=0, keepdims=True)
    unique = jnp.min(jnp.where(n_picked == jnp.minimum(n_cand, float(n_free)), 1.0, 0.0)) > 0.5
    sel = lax.cond(unique, lambda: sel_fast, lambda: pick(True))
    sel_bias = ((sel.T - 1.0) * (-NEG_INF)).astype(BF16)

    n_parts = 2
    hp = R // n_parts
    mp = hp * tq
    qparts = [qs[i * mp:(i + 1) * mp] for i in range(n_parts)]
    qrow = t0 + (lax.broadcasted_iota(jnp.int32, (mp, 1), 0) & (tq - 1))
    kcol = lax.broadcasted_iota(jnp.int32, (1, tk), 1)
    n_chunks = seq // tk
    j_last = t0 // tk

    klane = lax.broadcasted_iota(jnp.int32, (tk, LANES), 1)

    def slc_step(j, carry, masked):
        kv = kvs_ref[0, pl.ds(pl.multiple_of(j * tk, tk), tk), :]
        k_sel = jnp.where(klane < HEAD_DIM, kv, bmat_ref[...]).astype(BF16)
        ones_v = jnp.where(klane < HEAD_DIM, 1.0, kv).astype(BF16)
        off = pl.multiple_of(blocks_per_chunk * (n_chunks - 1 - j), blocks_per_chunk)
        place = pmat_ref[pl.ds(off, n_slc), :].astype(BF16)
        q_bias = jnp.dot(sel_bias, place, preferred_element_type=F32).astype(BF16)
        q_bias = jnp.concatenate([q_bias] * hp, axis=0)
        new = []
        for qp, (m, acc) in zip(qparts, carry):
            s = lax.dot_general(qp + q_bias, k_sel, _NT, preferred_element_type=F32)
            if masked:
                s = jnp.where(j * tk + kcol <= qrow, s, NEG_INF)
            m_new = jnp.maximum(m, jnp.max(s, axis=1, keepdims=True))
            p = jnp.exp2(s - m_new).astype(BF16)
            acc = jnp.exp2(m - m_new) * acc + jnp.dot(p, ones_v, preferred_element_type=F32)
            new.append((m_new, acc))
        return tuple(new)

    init = (jnp.full((mp, 1), NEG_INF, F32), jnp.zeros((mp, LANES), F32))
    n_pairs = j_last // 2
    carry = lax.fori_loop(0, n_pairs, lambda i, c: slc_step(2 * i + 1, slc_step(2 * i, c, False), False),
                          (init,) * n_parts)
    carry = lax.fori_loop(2 * n_pairs, j_last, functools.partial(slc_step, masked=False), carry)
    carry = slc_step(j_last, carry, True)
    o_s = jnp.concatenate([acc / acc[:, 0:1] for (_, acc) in carry], axis=0)

    gates = gate_ref[0]
    lane = lax.broadcasted_iota(jnp.int32, (tq, LANES), 1)
    comb = []
    for r in range(R):
        rs = slice(r * tq, (r + 1) * tq)
        comb.append(gates[:, r:r + 1] * o_c[rs] + gates[:, R + r:R + r + 1] * o_s[rs]
                    + gates[:, 2 * R + r:2 * R + r + 1] * o_w[rs])
    out = [jnp.where(lane < HEAD_DIM, pltpu.roll(comb[2 * i], HEAD_DIM, axis=1), comb[2 * i + 1])
           for i in range(R // 2)]
    o_ref[0] = jnp.concatenate(out, axis=1).astype(o_ref.dtype)


def nsa_attention(qkv, kvc, kvct, gates, *, tq=256, tk=1024):
    B, S, _ = qkv.shape
    G, R = NSA_GROUPS, NSA_Q_PER_GROUP
    tk = min(tk, S)
    n_slc = S // SLC_BLOCK
    n_cmp = kvc.shape[2]
    bpc = tk // SLC_BLOCK
    assert bpc <= LANES - HEAD_DIM
    off = bpc * (S // tk - 1)
    lane = np.arange(LANES)[None, :]
    bmat = jnp.asarray(lane - HEAD_DIM == np.arange(tk)[:, None] // SLC_BLOCK, BF16)
    pmat = jnp.asarray((np.arange(n_slc + off)[:, None] - off == lane - HEAD_DIM) & (lane >= HEAD_DIM)
                       & (lane < HEAD_DIM + bpc), F32)
    slc_blk0, win_blk0 = N_HEADS, N_HEADS + G
    return pl.pallas_call(
        functools.partial(_nsa_attn_kernel, tq=tq, tk=tk, seq=S),
        grid=(B, G, S // tq),
        in_specs=[pl.BlockSpec((1, tq, R * LANES), lambda b, g, i: (b, i, g)),
                  pl.BlockSpec((1, S, LANES), lambda b, g, i: (b, 0, slc_blk0 + g)),
                  pl.BlockSpec((1, S, LANES), lambda b, g, i: (b, 0, win_blk0 + g)),
                  pl.BlockSpec((1, 1, n_cmp, LANES), lambda b, g, i: (b, g, 0, 0)),
                  pl.BlockSpec((1, 1, LANES, n_cmp), lambda b, g, i: (b, g, 0, 0)),
                  pl.BlockSpec((1, tq, LANES), lambda b, g, i: (b, i, g)),
                  pl.BlockSpec(bmat.shape, lambda b, g, i: (0, 0)),
                  pl.BlockSpec(pmat.shape, lambda b, g, i: (0, 0))],
        out_specs=pl.BlockSpec((1, tq, R * HEAD_DIM), lambda b, g, i: (b, i, g)),
        out_shape=jax.ShapeDtypeStruct((B, S, ATTN_WIDTH), BF16),
        compiler_params=_cparams(("parallel", "parallel", "arbitrary")),
    )(qkv, qkv, qkv, kvc, kvct, gates, bmat, pmat)


def nsa_layer(h, gamma, w_in, pe_k, w1_k, w2_k, pe_v, w1_v, w2_v, w_out, B, S):
    G, R, hd, aw = NSA_GROUPS, NSA_Q_PER_GROUP, HEAD_DIM, ATTN_WIDTH
    kvd = G * hd
    sec = lambda i: w_in[:, aw + i * kvd: aw + (i + 1) * kvd]
    wq = w_in[:, :aw] * (hd ** -0.5 * LOG2E)
    wa = jnp.concatenate([_head_slots(wq, N_HEADS), _head_slots(sec(2), G, sec(3)),
                          _head_slots(sec(4), G, sec(5))], axis=1).astype(BF16)
    wb = jnp.concatenate([_head_slots(_rot_half_cols(wq), N_HEADS), _head_slots(_rot_half_cols(sec(2)), G),
                          _head_slots(_rot_half_cols(sec(4)), G)], axis=1).astype(BF16)
    cos, sin = _rope_tables(S)
    qkv = norm_matmul(h, gamma, wa, wb=wb, cos=cos, sin=sin)
    kc_src = norm_matmul(h, gamma, sec(0).astype(BF16), wb=_rot_half_cols(sec(0)).astype(BF16),
                         cos=cos, sin=sin, table_of_tile=lambda j: 1)
    vc_src = norm_matmul(h, gamma, sec(1).astype(BF16))
    wg = w_in[:, aw + 6 * kvd:].reshape(-1, 3, G, R)
    wg = jnp.transpose(wg, (0, 2, 1, 3)).reshape(-1, G, 3 * R)
    wg = jnp.pad(wg, ((0, 0), (0, 0), (0, LANES - 3 * R))).reshape(-1, G * LANES).astype(BF16)
    gates = norm_matmul(h, gamma, wg, act="sigmoid", out_dtype=F32)
    kc = nsa_compress(kc_src.reshape(B, S, kvd), pe_k, w1_k, w2_k)
    vc = nsa_compress(vc_src.reshape(B, S, kvd), pe_v, w1_v, w2_v)
    n_cmp = kc.shape[1]
    kvc = jnp.concatenate([kc.reshape(B, n_cmp, G, hd), vc.reshape(B, n_cmp, G, hd)], axis=-1)
    kvc = jnp.transpose(kvc, (0, 2, 1, 3)).astype(BF16)
    kvct = jnp.swapaxes(kvc, 2, 3)
    o = nsa_attention(qkv.reshape(B, S, -1), kvc, kvct, gates.reshape(B, S, G * LANES))
    return matmul_residual(o.reshape(B * S, aw), w_out.astype(BF16), h)


def kernel(x, l0_attn_norm, l0_w_in, l0_cmp_pe_k, l0_cmp_w1_k, l0_cmp_w2_k, l0_cmp_pe_v, l0_cmp_w1_v,
           l0_cmp_w2_v, l0_w_out, l0_ffn_norm, l0_peer_wq, l0_peer_keys, l0_peer_u, l0_peer_v,
           l1_attn_norm, l1_w_in, l1_f_bias, l1_w_out, l1_ffn_norm, l1_peer_wq, l1_peer_keys, l1_peer_u,
           l1_peer_v, final_norm):
    B, S, D = x.shape
    h = x.reshape(B * S, D)
    h = nsa_layer(h, l0_attn_norm, l0_w_in, l0_cmp_pe_k, l0_cmp_w1_k, l0_cmp_w2_k, l0_cmp_pe_v, l0_cmp_w1_v,
                  l0_cmp_w2_v, l0_w_out, B, S)
    h = peer_layer(h, l0_ffn_norm, l0_peer_wq, l0_peer_keys, l0_peer_u, l0_peer_v)
    h = fox_layer(h, l1_attn_norm, l1_w_in, l1_f_bias, l1_w_out, B, S)
    h = peer_layer(h, l1_ffn_norm, l1_peer_wq, l1_peer_keys, l1_peer_u, l1_peer_v)
    return rmsnorm(h, final_norm).reshape(B, S, D)
```

```python
import functools

import numpy as np
import jax
import jax.numpy as jnp
from jax import lax
from jax.experimental import pallas as pl
from jax.experimental.pallas import tpu as pltpu

F32 = jnp.float32
BF16 = jnp.bfloat16

D_MODEL = 1024
N_HEADS = 16
HEAD_DIM = 64
ATTN_WIDTH = N_HEADS * HEAD_DIM
NSA_GROUPS = 4
NSA_Q_PER_GROUP = N_HEADS // NSA_GROUPS
CMP_BLOCK = 32
CMP_STRIDE = 16
CMP_HIDDEN = 2 * HEAD_DIM
SLC_BLOCK = 64
SLC_TOPK = 16
WINDOW = 512
FORCE_SCORE = 1.0e4
ROPE_THETA = 10000.0
PEER_HEADS = 8
PEER_N_KEYS = 128
PEER_TOPK = 16
PEER_HALF_DIM = 128
RMS_EPS = 1e-6
NEG_INF = -1e30
LOG2E = 1.4426950408889634

LANES = 128
VMEM_LIMIT_BYTES = 56 * 1024 * 1024

_NT = (((1,), (1,)), ((), ()))


def _cparams(sem, vmem=VMEM_LIMIT_BYTES, flags=None):
    return pltpu.CompilerParams(dimension_semantics=sem, vmem_limit_bytes=vmem, flags=flags)


def _gelu_tanh(x):
    return 0.5 * x * (1.0 + jnp.tanh(0.7978845608028654 * (x + 0.044715 * (x * x * x))))


def _gelu_sigmoid(x):
    c = -2.0 * 0.7978845608028654 * LOG2E
    t = x * (c + (c * 0.044715) * (x * x))
    return x / (1.0 + jnp.exp2(t))


def _rms_rows(x, g):
    ms = jnp.mean(x * x, axis=-1, keepdims=True)
    return x * lax.rsqrt(ms + RMS_EPS) * g


def _norm_mm_kernel(*refs, act, has_bias, rope, emit_xn):
    it = iter(refs)
    x_ref, g_ref = next(it), next(it)
    wa_ref = next(it)
    wb_ref = next(it) if rope else None
    cos_ref = next(it) if rope else None
    sin_ref = next(it) if rope else None
    b_ref = next(it) if has_bias else None
    o_ref = next(it)
    xo_ref = next(it) if emit_xn else None
    xn_ref = next(it)

    @pl.when(pl.program_id(1) == 0)
    def _():
        xn = _rms_rows(x_ref[...], g_ref[...])
        xn_ref[...] = xn.astype(BF16)
        if emit_xn:
            xo_ref[...] = xn.T.astype(BF16)

    xn = xn_ref[...]
    y = jnp.dot(xn, wa_ref[...], preferred_element_type=F32)
    if rope:
        yb = jnp.dot(xn, wb_ref[...], preferred_element_type=F32)
        cos, sin = cos_ref[...], sin_ref[...]
        for s in range(y.shape[1] // LANES):
            sl = slice(s * LANES, (s + 1) * LANES)
            o_ref[:, sl] = (y[:, sl] * cos + yb[:, sl] * sin).astype(o_ref.dtype)
        return
    if has_bias:
        y = y + b_ref[...]
    if act == "sigmoid":
        y = jax.nn.sigmoid(y)
    elif act == "log_sigmoid":
        y = jax.nn.log_sigmoid(y)
    o_ref[...] = y.astype(o_ref.dtype)


def norm_matmul(x, gamma, wa, *, wb=None, cos=None, sin=None, table_of_tile=None, bias=None,
                act=None, out_dtype=BF16, emit_xn=False, tm=1024, tn=1024):
    T, D = x.shape
    N = wa.shape[1]
    tm, tn = min(tm, T), min(tn, N)
    assert T % tm == 0 and N % tn == 0 and tn % LANES == 0
    rope = wb is not None
    in_specs = [pl.BlockSpec((tm, D), lambda i, j: (i, 0)),
                pl.BlockSpec((1, D), lambda i, j: (0, 0)),
                pl.BlockSpec((D, tn), lambda i, j: (0, j))]
    args = [x, gamma.reshape(1, D).astype(F32), wa]
    if rope:
        S = cos.shape[1]
        assert S % tm == 0
        n_pos = S // tm
        tmap = table_of_tile if table_of_tile is not None else (lambda j: 0)
        in_specs += [pl.BlockSpec((D, tn), lambda i, j: (0, j)),
                     pl.BlockSpec((None, tm, LANES), lambda i, j: (tmap(j), i % n_pos, 0)),
                     pl.BlockSpec((None, tm, LANES), lambda i, j: (tmap(j), i % n_pos, 0))]
        args += [wb, cos, sin]
    if bias is not None:
        in_specs.append(pl.BlockSpec((1, tn), lambda i, j: (0, j)))
        args.append(bias.reshape(1, N).astype(F32))
    out_shape = [jax.ShapeDtypeStruct((T, N), out_dtype)]
    out_specs = [pl.BlockSpec((tm, tn), lambda i, j: (i, j))]
    if emit_xn:
        out_shape.append(jax.ShapeDtypeStruct((D, T), BF16))
        out_specs.append(pl.BlockSpec((D, tm), lambda i, j: (0, i)))
    res = pl.pallas_call(
        functools.partial(_norm_mm_kernel, act=act, has_bias=bias is not None, rope=rope, emit_xn=emit_xn),
        grid=(T // tm, N // tn),
        in_specs=in_specs,
        out_specs=out_specs,
        out_shape=out_shape,
        scratch_shapes=[pltpu.VMEM((tm, D), BF16)],
        compiler_params=_cparams(("parallel", "arbitrary")),
    )(*args)
    return res if emit_xn else res[0]


def _mm_res_kernel(a_ref, w_ref, r_ref, o_ref):
    o_ref[...] = r_ref[...] + jnp.dot(a_ref[...], w_ref[...], preferred_element_type=F32)


def matmul_residual(a, w, res, *, tm=1024, tn=1024):
    T, K = a.shape
    N = w.shape[1]
    tm, tn = min(tm, T), min(tn, N)
    assert T % tm == 0 and N % tn == 0
    return pl.pallas_call(
        _mm_res_kernel,
        grid=(T // tm, N // tn),
        in_specs=[pl.BlockSpec((tm, K), lambda i, j: (i, 0)),
                  pl.BlockSpec((K, tn), lambda i, j: (0, j)),
                  pl.BlockSpec((tm, tn), lambda i, j: (i, j))],
        out_specs=pl.BlockSpec((tm, tn), lambda i, j: (i, j)),
        out_shape=jax.ShapeDtypeStruct((T, N), F32),
        compiler_params=_cparams(("parallel", "arbitrary")),
    )(a, w, res)


def _rmsnorm_kernel(x_ref, g_ref, o_ref):
    o_ref[...] = _rms_rows(x_ref[...], g_ref[...])


def rmsnorm(x, gamma, *, tm=1024):
    T, D = x.shape
    tm = min(tm, T)
    return pl.pallas_call(
        _rmsnorm_kernel,
        grid=(T // tm,),
        in_specs=[pl.BlockSpec((tm, D), lambda i: (i, 0)), pl.BlockSpec((1, D), lambda i: (0, 0))],
        out_specs=pl.BlockSpec((tm, D), lambda i: (i, 0)),
        out_shape=jax.ShapeDtypeStruct((T, D), F32),
        compiler_params=_cparams(("parallel",)),
    )(x, gamma.reshape(1, D).astype(F32))


def _peer_cand_tables(tn):
    fidx, vmask = [], []
    for k2 in range(16):
        fidx.append(k2); vmask.append(0.0)
    for k1 in range(1, 8):
        lim = PEER_TOPK // (k1 + 1)
        for k2 in range(8):
            fidx.append(k1 * 16 + k2); vmask.append(0.0 if k2 < lim else -np.inf)
    for k1 in range(8, 16):
        fidx.append(k1 * 16); vmask.append(0.0)
    fidx = np.broadcast_to(np.asarray(fidx, np.int32)[:, None], (80, tn))
    vmask = np.broadcast_to(np.asarray(vmask, np.float32)[:, None], (80, tn))
    return jnp.asarray(fidx), jnp.asarray(vmask)


def _top16_rows(s, exact_ties):
    n, tn = s.shape
    rows = lax.broadcasted_iota(jnp.int32, (n, tn), 0)
    rows16 = lax.broadcasted_iota(jnp.int32, (PEER_TOPK, tn), 0)
    rank = jnp.full((n, tn), float(PEER_TOPK), F32)
    tops = jnp.zeros((PEER_TOPK, tn), F32)
    v = s
    for k in range(PEER_TOPK):
        m = jnp.max(v, axis=0, keepdims=True)
        if exact_ties:
            hit = rows == jnp.min(jnp.where(v == m, rows, n), axis=0, keepdims=True)
        else:
            hit = v == m
        rank = jnp.where(hit, float(k), rank)
        v = jnp.where(hit, -jnp.inf, v)
        tops = jnp.where(rows16 == k, m, tops)
    return tops, rank, v


def _peer_select_head(q_ref, keys_ref, fidx, vmask, exact_ties):
    tops, ranks, es, picked = [], [], [], []
    for p in range(2):
        q = q_ref[:, p * PEER_HALF_DIM:(p + 1) * PEER_HALF_DIM]
        s = lax.dot_general(keys_ref[p], q, _NT, preferred_element_type=F32)
        t, r, v = _top16_rows(s, exact_ties)
        tops.append(t); ranks.append(r)
        es.append(jnp.exp(s - t[0:1, :]))
        picked.append(jnp.sum(jnp.where(v == -jnp.inf, 1.0, 0.0), axis=0, keepdims=True))
    ts1, ts2 = tops
    pieces = [ts1[0:1, :] + ts2]
    for k1 in range(1, 8):
        pieces.append(ts1[k1:k1 + 1, :] + ts2[0:8, :])
    pieces.append(ts1[8:16, :] + ts2[0:1, :])
    cand0 = jnp.concatenate(pieces, axis=0) + vmask
    cand = cand0
    for _ in range(PEER_TOPK):
        m = jnp.max(cand, axis=0, keepdims=True)
        if exact_ties:
            hit = fidx == jnp.min(jnp.where(cand == m, fidx, 4096), axis=0, keepdims=True)
        else:
            hit = cand == m
        cand = jnp.where(hit, -jnp.inf, cand)
    taken = jnp.logical_and(cand == -jnp.inf, vmask == 0.0)
    takenf = taken.astype(F32)
    picked.append(jnp.sum(takenf, axis=0, keepdims=True))
    unique = jnp.min(jnp.where((picked[0] == PEER_TOPK) & (picked[1] == PEER_TOPK) & (picked[2] == PEER_TOPK),
                               1.0, 0.0)) > 0.5
    best = ts1[0:1, :] + ts2[0:1, :]
    z = jnp.sum(jnp.where(taken, jnp.exp(cand0 - best), 0.0), axis=0, keepdims=True)
    counts = [jnp.sum(takenf[0:16, :], axis=0, keepdims=True)]
    for k1 in range(1, 8):
        counts.append(jnp.sum(takenf[16 + 8 * (k1 - 1):16 + 8 * k1, :], axis=0, keepdims=True))
    tail = takenf[72:80, :]
    cnt = jnp.zeros_like(ranks[0])
    for k1 in range(PEER_TOPK):
        nk = counts[k1] if k1 < 8 else tail[k1 - 8:k1 - 7, :]
        cnt = jnp.where(ranks[0] == float(k1), nk, cnt)
    return (cnt, ranks[1], es[0], es[1] / z), unique


def _peer_select_kernel(q_ref, keys_ref, fidx_ref, vmask_ref, cnt_ref, rank2_ref, e1_ref, e2_ref):
    fidx = fidx_ref[...]
    vmask = vmask_ref[...]

    def store(vals):
        for ref, val in zip((cnt_ref, rank2_ref, e1_ref, e2_ref), vals):
            ref[...] = val.astype(ref.dtype)

    vals, unique = _peer_select_head(q_ref, keys_ref, fidx, vmask, exact_ties=False)
    store(vals)

    @pl.when(jnp.logical_not(unique))
    def _():
        store(_peer_select_head(q_ref, keys_ref, fidx, vmask, exact_ties=True)[0])


def peer_select(q, keys, *, tn=1024):
    T = q.shape[0]
    tn = min(tn, T)
    fidx, vmask = _peer_cand_tables(tn)
    rows = PEER_HEADS * PEER_N_KEYS
    ospec = pl.BlockSpec((PEER_N_KEYS, tn), lambda i, h: (h, i))
    return pl.pallas_call(
        _peer_select_kernel,
        grid=(T // tn, PEER_HEADS),
        in_specs=[pl.BlockSpec((tn, 2 * PEER_HALF_DIM), lambda i, h: (i, h)),
                  pl.BlockSpec((2, PEER_N_KEYS, PEER_HALF_DIM), lambda i, h: (h, 0, 0)),
                  pl.BlockSpec((80, tn), lambda i, h: (0, 0)),
                  pl.BlockSpec((80, tn), lambda i, h: (0, 0))],
        out_specs=[ospec] * 4,
        out_shape=[jax.ShapeDtypeStruct((rows, T), dt) for dt in (F32, BF16, F32, BF16)],
        compiler_params=_cparams(("parallel", "parallel")),
    )(q, keys, fidx, vmask)


def _peer_dense_kernel(xn_ref, u_ref, vt_ref, cnt_ref, rank2_ref, e1_ref, e2_ref, res_ref, o_ref,
                       acc_ref, g0_ref, g1_ref, *, c_per_step):
    j = pl.program_id(1)
    n_tiles = pl.num_programs(1) - 1

    @pl.when(j == 0)
    def _():
        acc_ref[...] = jnp.zeros_like(acc_ref)
        g1_ref[...] = jnp.zeros_like(g1_ref)

    @pl.when(j % 2 == 0)
    def _():
        _peer_dense_step(xn_ref, u_ref, vt_ref, cnt_ref, rank2_ref, e1_ref, e2_ref, acc_ref,
                         g1_ref, g0_ref, j, c_per_step)

    @pl.when(j % 2 == 1)
    def _():
        _peer_dense_step(xn_ref, u_ref, vt_ref, cnt_ref, rank2_ref, e1_ref, e2_ref, acc_ref,
                         g0_ref, g1_ref, j, c_per_step)

    @pl.when(j == n_tiles)
    def _():
        o_ref[...] = res_ref[...] + acc_ref[...].T


def _peer_dense_step(xn_ref, u_ref, vt_ref, cnt_ref, rank2_ref, e1_ref, e2_ref, acc_ref, g_ref, g_next_ref,
                     j, c_per_step):

    tn = xn_ref.shape[1]
    bf16_rows = 16
    reps = PEER_N_KEYS // bf16_rows

    def row_tile(ref, row):
        r16 = jnp.broadcast_to(ref[pl.ds(row, 1), :], (bf16_rows, tn)).astype(BF16)
        return jnp.concatenate([r16] * reps, axis=0)

    c0 = jnp.maximum(j - 1, 0) * c_per_step
    up_rows = 2 * PEER_N_KEYS
    blocks = []
    for cc in range(c_per_step):
        if (cc * PEER_N_KEYS) % up_rows == 0:
            rs = slice(cc * PEER_N_KEYS, cc * PEER_N_KEYS + up_rows)
            hT = jnp.dot(u_ref[rs, :], xn_ref[...], preferred_element_type=F32)
            g_next_ref[rs, :] = _gelu_sigmoid(hT.astype(BF16))
        c = c0 + cc
        w = None
        for h in range(PEER_HEADS):
            row = h * PEER_N_KEYS + c
            n_row = row_tile(cnt_ref, row)
            e1_row = row_tile(e1_ref, row)
            sl = slice(h * PEER_N_KEYS, (h + 1) * PEER_N_KEYS)
            term = jnp.where(rank2_ref[sl, :] < n_row, e2_ref[sl, :], 0.0) * e1_row
            w = term if w is None else w + term
        blocks.append(w * g_ref[cc * PEER_N_KEYS:(cc + 1) * PEER_N_KEYS, :])
    aT = jnp.concatenate(blocks, axis=0)
    acc_ref[...] += jnp.dot(vt_ref[...], aT, preferred_element_type=F32)


def peer_dense(xn, u, vt, cnt, rank2, e1, e2, res, *, tn=512, te=1024):
    D, T = xn.shape
    E = u.shape[0]
    tn = min(tn, T)
    rows = PEER_HEADS * PEER_N_KEYS
    once = pl.Buffered(1)
    sel_spec = pl.BlockSpec((rows, tn), lambda i, j: (0, i))
    n_tiles = E // te
    return pl.pallas_call(
        functools.partial(_peer_dense_kernel, c_per_step=te // PEER_N_KEYS),
        grid=(T // tn, n_tiles + 1),
        in_specs=[pl.BlockSpec((D, tn), lambda i, j: (0, i)),
                  pl.BlockSpec((te, D), lambda i, j: (jnp.minimum(j, n_tiles - 1), 0)),
                  pl.BlockSpec((D, te), lambda i, j: (0, jnp.maximum(j - 1, 0))),
                  sel_spec, sel_spec, sel_spec, sel_spec,
                  pl.BlockSpec((tn, D), lambda i, j: (i, 0), pipeline_mode=once)],
        out_specs=pl.BlockSpec((tn, D), lambda i, j: (i, 0), pipeline_mode=once),
        out_shape=jax.ShapeDtypeStruct((T, D), F32),
        scratch_shapes=[pltpu.VMEM((D, tn), F32), pltpu.VMEM((te, tn), BF16), pltpu.VMEM((te, tn), BF16)],
        compiler_params=_cparams(("parallel", "arbitrary")),
    )(xn, u, vt, cnt, rank2, e1, e2, res)


def peer_layer(h, gamma, w_q, sub_keys, u, v):
    q, xn = norm_matmul(h, gamma, w_q.astype(BF16), emit_xn=True)
    keys = sub_keys.reshape(2 * PEER_HEADS, PEER_N_KEYS, PEER_HALF_DIM).astype(BF16)
    cnt, rank2, e1, e2 = peer_select(q, keys)
    return peer_dense(xn, u.astype(BF16), v.T.astype(BF16), cnt, rank2, e1, e2, h)


def _cumsum_aug_kernel(lf_ref, tri_ref, place_q_ref, place_k_ref, ones_q_ref, ones_k_ref,
                       qa_ref, ka_ref, carry_ref):
    @pl.when(pl.program_id(1) == 0)
    def _():
        carry_ref[...] = jnp.zeros_like(carry_ref)

    lf = lf_ref[0]
    c = jnp.dot(tri_ref[...], lf, preferred_element_type=F32, precision=lax.Precision.HIGHEST) + carry_ref[...]
    carry_ref[...] = c[-1:, :]
    c = c * LOG2E
    hi = c.astype(BF16)
    r1 = c - hi.astype(F32)
    mid = r1.astype(BF16)
    lo = (r1 - mid.astype(F32)).astype(BF16)
    nh = N_HEADS
    lane = lax.broadcasted_iota(jnp.int32, c.shape, 1)
    parts = jnp.where(lane < nh, hi.astype(F32),
                      jnp.where(lane < 2 * nh, pltpu.roll(mid.astype(F32), nh, axis=1),
                                pltpu.roll(lo.astype(F32), 2 * nh, axis=1)))
    parts = jnp.where(lane < 3 * nh, parts, 0.0).astype(BF16)
    qa_ref[0] = (jnp.dot(parts, place_q_ref[...], preferred_element_type=F32) + ones_q_ref[...]).astype(BF16)
    ka_ref[0] = (jnp.dot(parts, place_k_ref[...], preferred_element_type=F32) + ones_k_ref[...]).astype(BF16)


def fox_bias_operands(logf, *, tc=512):
    B, S, _ = logf.shape
    nh = N_HEADS
    tri = jnp.asarray(np.tril(np.ones((tc, tc), np.float32)))
    pq = np.zeros((LANES, nh * LANES), np.float32)
    pk = np.zeros((LANES, nh * LANES), np.float32)
    oq = np.zeros((1, nh * LANES), np.float32)
    ok = np.zeros((1, nh * LANES), np.float32)
    for h in range(nh):
        for part in range(3):
            pq[part * nh + h, h * LANES + part] = 1.0
            pk[part * nh + h, h * LANES + 3 + part] = -1.0
            oq[0, h * LANES + 3 + part] = 1.0
            ok[0, h * LANES + part] = 1.0
    const = lambda a: pl.BlockSpec(a.shape, lambda b, i: (0,) * a.ndim)
    pq, pk, oq, ok = jnp.asarray(pq, BF16), jnp.asarray(pk, BF16), jnp.asarray(oq), jnp.asarray(ok)
    out = jax.ShapeDtypeStruct((B, S, nh * LANES), BF16)
    return pl.pallas_call(
        _cumsum_aug_kernel,
        grid=(B, S // tc),
        in_specs=[pl.BlockSpec((1, tc, LANES), lambda b, i: (b, i, 0)),
                  const(tri), const(pq), const(pk), const(oq), const(ok)],
        out_specs=[pl.BlockSpec((1, tc, nh * LANES), lambda b, i: (b, i, 0))] * 2,
        out_shape=[out, out],
        scratch_shapes=[pltpu.VMEM((1, LANES), F32)],
        compiler_params=_cparams(("parallel", "arbitrary")),
    )(logf, tri, pq, pk, oq, ok)


def _fox_attn_kernel(q_ref, qa_ref, kv_ref, ka_ref, o_ref, *, tq, tk, heads_per_step):
    qi = pl.program_id(2)
    t0 = qi * tq
    n_full = t0 // tk
    n_diag = tq // tk
    lanes = [slice(hh * LANES, (hh + 1) * LANES) for hh in range(heads_per_step)]
    qs = [jnp.concatenate([q_ref[0, :, lsl], qa_ref[0, :, lsl]], axis=1) for lsl in lanes]

    def step(j, carry, masked):
        rows = pl.ds(pl.multiple_of(j * tk, tk), tk)
        klane = lax.broadcasted_iota(jnp.int32, (tk, LANES), 1)
        new = []
        for lsl, q, (m, acc) in zip(lanes, qs, carry):
            kv = kv_ref[0, rows, lsl]
            kk = jnp.concatenate([kv, ka_ref[0, rows, lsl]], axis=1)
            ones_v = jnp.where(klane < HEAD_DIM, 1.0, kv).astype(BF16)
            s = lax.dot_general(q, kk, _NT, preferred_element_type=F32)
            if masked:
                qpos = t0 + lax.broadcasted_iota(jnp.int32, (tq, 1), 0)
                kpos = j * tk + lax.broadcasted_iota(jnp.int32, (1, tk), 1)
                s = jnp.where(kpos <= qpos, s, NEG_INF)
            m_new = jnp.maximum(m, jnp.max(s, axis=1, keepdims=True))
            p = jnp.exp2(s - m_new).astype(BF16)
            acc = jnp.exp2(m - m_new) * acc + jnp.dot(p, ones_v, preferred_element_type=F32)
            new.append((m_new, acc))
        return tuple(new)

    init = (jnp.full((tq, 1), NEG_INF, F32), jnp.zeros((tq, LANES), F32))
    n_pairs = n_full // 2
    carry = lax.fori_loop(0, n_pairs, lambda i, c: step(2 * i + 1, step(2 * i, c, False), False),
                          (init,) * heads_per_step)
    carry = lax.fori_loop(2 * n_pairs, n_full, functools.partial(step, masked=False), carry)
    for d in range(n_diag):
        carry = step(n_full + d, carry, True)
    outs = [acc / acc[:, 0:1] for (_, acc) in carry]
    lane = lax.broadcasted_iota(jnp.int32, (tq, LANES), 1)
    blocks = []
    for pair in range(heads_per_step // 2):
        a, b = outs[2 * pair], outs[2 * pair + 1]
        blocks.append(jnp.where(lane < HEAD_DIM, pltpu.roll(a, HEAD_DIM, axis=1), b))
    o_ref[0] = jnp.concatenate(blocks, axis=1).astype(o_ref.dtype) if len(blocks) > 1 else blocks[0].astype(o_ref.dtype)


def _head_slots(w, n_heads, second=None):
    D = w.shape[0]
    a = w.reshape(D, n_heads, HEAD_DIM)
    b = jnp.zeros_like(a) if second is None else second.reshape(D, n_heads, HEAD_DIM)
    return jnp.concatenate([a, b], axis=-1).reshape(D, n_heads * LANES)


def fox_layer(h, gamma, w_in, f_bias, w_out, B, S):
    aw = ATTN_WIDTH
    wq = _head_slots(w_in[:, :aw] * (HEAD_DIM ** -0.5 * LOG2E), N_HEADS)
    wkv = _head_slots(w_in[:, aw:2 * aw], N_HEADS, w_in[:, 2 * aw:3 * aw])
    w_main = jnp.concatenate([wq, wkv], axis=1).astype(BF16)
    wf = jnp.pad(w_in[:, 3 * aw:], ((0, 0), (0, LANES - N_HEADS))).astype(BF16)
    bf = jnp.pad(f_bias.astype(F32), (0, LANES - N_HEADS))
    qkv = norm_matmul(h, gamma, w_main)
    logf = norm_matmul(h, gamma, wf, bias=bf, act="log_sigmoid", out_dtype=F32)
    qa, ka = fox_bias_operands(logf.reshape(B, S, LANES))
    qkv = qkv.reshape(B, S, 2 * N_HEADS * LANES)
    o = fox_attention(qkv, qa, ka)
    return matmul_residual(o.reshape(B * S, aw), w_out.astype(BF16), h)


def fox_attention(qkv, qa, ka, *, tq=1024, tk=1024, heads_per_step=2):
    B, S, _ = qkv.shape
    hs = heads_per_step
    wq = hs * LANES
    n_qblk = N_HEADS // hs
    return pl.pallas_call(
        functools.partial(_fox_attn_kernel, tq=tq, tk=tk, heads_per_step=hs),
        grid=(B, n_qblk, S // tq),
        in_specs=[pl.BlockSpec((1, tq, wq), lambda b, h, i: (b, i, h)),
                  pl.BlockSpec((1, tq, wq), lambda b, h, i: (b, i, h)),
                  pl.BlockSpec((1, S, wq), lambda b, h, i: (b, 0, n_qblk + h)),
                  pl.BlockSpec((1, S, wq), lambda b, h, i: (b, 0, h))],
        out_specs=pl.BlockSpec((1, tq, hs * HEAD_DIM), lambda b, h, i: (b, i, h)),
        out_shape=jax.ShapeDtypeStruct((B, S, ATTN_WIDTH), BF16),
        compiler_params=_cparams(("parallel", "parallel", "arbitrary")),
    )(qkv, qa, qkv, ka)


def _rot_half_cols(w):
    D = w.shape[0]
    a = w.reshape(D, -1, HEAD_DIM)
    half = HEAD_DIM // 2
    return jnp.concatenate([-a[..., half:], a[..., :half]], axis=-1).reshape(w.shape)


def _rope_tables(S):
    half = HEAD_DIM // 2
    inv_freq = ROPE_THETA ** (-jnp.arange(half, dtype=F32) / half)
    ang = jnp.arange(S, dtype=F32)[:, None] * inv_freq[None, :]
    c, s = jnp.cos(ang), jnp.sin(ang)
    c2, s2 = jnp.concatenate([c, c], axis=1), jnp.concatenate([s, s], axis=1)
    cos = jnp.stack([jnp.concatenate([c2, jnp.ones_like(c2)], axis=1), jnp.concatenate([c2, c2], axis=1)])
    sin = jnp.stack([jnp.concatenate([s2, jnp.zeros_like(s2)], axis=1), jnp.concatenate([s2, s2], axis=1)])
    return cos, sin


def _compress_kernel(x_ref, pea_ref, peb_ref, wa_ref, wb_ref, w2_ref, o_ref, pa_ref, pb0_ref, *, n_rows):
    u = pl.program_id(1)
    x = x_ref[0].astype(F32)
    pa = jnp.dot((x + pea_ref[...]).astype(BF16), wa_ref[...], preferred_element_type=F32)
    pb = jnp.dot((x + peb_ref[...]).astype(BF16), wb_ref[...], preferred_element_type=F32)

    def emit(slab, hid):
        y = jnp.dot(_gelu_tanh(hid).astype(BF16), w2_ref[...], preferred_element_type=F32)
        o_ref[0, pl.ds(pl.multiple_of(slab * n_rows, n_rows), n_rows), :] = y

    @pl.when(u == 0)
    def _():
        pb0_ref[...] = pb

    @pl.when(u > 0)
    def _():
        emit(u - 1, pa_ref[...] + pb)

    @pl.when(u == 3)
    def _():
        emit(3, pa + pltpu.roll(pb0_ref[...], n_rows - 1, axis=0))

    pa_ref[...] = pa


def nsa_compress(src, pe, w1, w2):
    B, S, W = src.shape
    G = NSA_GROUPS
    n_rows = S // 64
    half = CMP_BLOCK // 2
    cw = half * W
    xv = src.reshape(B, n_rows, 4 * cw)
    pe_flat = jnp.transpose(pe, (1, 0, 2)).reshape(CMP_BLOCK, W).astype(F32)
    pea, peb = pe_flat[:half].reshape(1, cw), pe_flat[half:].reshape(1, cw)
    eye = jnp.eye(G, dtype=F32)
    wfull = jnp.einsum('gldh,gk->lkdgh', w1.astype(F32), eye).reshape(CMP_BLOCK, W, G * CMP_HIDDEN)
    wa = wfull[:half].reshape(cw, G * CMP_HIDDEN).astype(BF16)
    wb = wfull[half:].reshape(cw, G * CMP_HIDDEN).astype(BF16)
    w2bd = jnp.einsum('ghd,gk->ghkd', w2.astype(F32), eye).reshape(G * CMP_HIDDEN, W).astype(BF16)
    const = lambda a: pl.BlockSpec(a.shape, lambda b, u: (0,) * a.ndim)
    return pl.pallas_call(
        functools.partial(_compress_kernel, n_rows=n_rows),
        grid=(B, 4),
        in_specs=[pl.BlockSpec((1, n_rows, cw), lambda b, u: (b, 0, u)),
                  const(pea), const(peb), const(wa), const(wb), const(w2bd)],
        out_specs=pl.BlockSpec((1, 4 * n_rows, W), lambda b, u: (b, 0, 0)),
        out_shape=jax.ShapeDtypeStruct((B, 4 * n_rows, W), F32),
        scratch_shapes=[pltpu.VMEM((n_rows, G * CMP_HIDDEN), F32), pltpu.VMEM((n_rows, G * CMP_HIDDEN), F32)],
        compiler_params=_cparams(("parallel", "arbitrary")),
    )(xv, pea, peb, wa, wb, w2bd)


def _nsa_attn_kernel(q_ref, kvs_ref, kvw_ref, kvc_ref, kvct_ref, gate_ref, bmat_ref, pmat_ref, o_ref,
                     *, tq, tk, seq):
    R = NSA_Q_PER_GROUP
    M = R * tq
    n_slc = seq // SLC_BLOCK
    n_cmp = 4 * n_slc
    n_sel = min(SLC_TOPK, n_slc)
    blocks_per_chunk = tk // SLC_BLOCK
    assert tq & (tq - 1) == 0 and n_slc & (n_slc - 1) == 0 and tk % tq == 0
    log_slc = n_slc.bit_length() - 1
    qi = pl.program_id(2)
    t0 = qi * tq

    qs = jnp.concatenate([q_ref[0, :, r * LANES:(r + 1) * LANES] for r in range(R)], axis=0)

    kvc = kvc_ref[0, 0]
    sT = lax.dot_general(kvc, qs, _NT, preferred_element_type=F32)
    rowc = lax.broadcasted_iota(jnp.int32, (n_cmp, 1), 0)
    cmp_end = (rowc & (n_slc - 1)) * SLC_BLOCK + (rowc >> log_slc) * CMP_STRIDE + (CMP_BLOCK - 1)
    tcol = t0 + (lax.broadcasted_iota(jnp.int32, (1, M), 1) & (tq - 1))
    sm = jnp.where(cmp_end <= tcol, sT, NEG_INF)
    mx = jnp.max(sm, axis=0, keepdims=True)
    e = jnp.exp2(sm - mx)
    inv = jnp.where(mx > 0.5 * NEG_INF, 1.0 / jnp.sum(e, axis=0, keepdims=True), 0.0)
    pT = e * inv
    o_c = jnp.dot(kvct_ref[0, 0], pT.astype(BF16), preferred_element_type=F32).T

    wlen = WINDOW + tq
    start = jnp.maximum(t0 - WINDOW, 0)
    kvw = kvw_ref[0, pl.ds(pl.multiple_of(start, tq), wlen), :]
    s_w = lax.dot_general(qs, kvw, _NT, preferred_element_type=F32)
    qpos = t0 + (lax.broadcasted_iota(jnp.int32, (M, 1), 0) & (tq - 1))
    kpos = start + lax.broadcasted_iota(jnp.int32, (1, wlen), 1)
    s_w = jnp.where(kpos <= qpos, jnp.where(kpos > qpos - WINDOW, s_w, NEG_INF), NEG_INF)
    p_w = jnp.exp2(s_w - jnp.max(s_w, axis=1, keepdims=True)).astype(BF16)
    wlane = lax.broadcasted_iota(jnp.int32, (wlen, LANES), 1)
    acc_w = jnp.dot(p_w, jnp.where(wlane < HEAD_DIM, 1.0, kvw).astype(BF16), preferred_element_type=F32)
    o_w = acc_w / acc_w[:, 0:1]

    psum = pT[:, 0:tq]
    for r in range(1, R):
        psum = psum + pT[:, r * tq:(r + 1) * tq]
    p0, p1, p2, p3 = (psum[u * n_slc:(u + 1) * n_slc, :] for u in range(4))
    jrow = lax.broadcasted_iota(jnp.int32, (n_slc, tq), 0)
    p3_prev = jnp.where(jrow == 0, 0.0, pltpu.roll(p3, 1, axis=0))
    imp = p0 + p1 + p2 + 0.5 * p3 + 0.5 * p3_prev
    cur = (t0 + lax.broadcasted_iota(jnp.int32, (n_slc, tq), 1)) >> (SLC_BLOCK.bit_length() - 1)
    forced = (jrow == 0) | (jrow == cur) | (jrow == cur - 1)
    vals = jnp.where(forced, -jnp.inf, jnp.where(jrow <= cur, imp, -jnp.inf))
    sel0 = jnp.where(forced, 1.0, 0.0)
    n_free = n_sel - 3

    def pick(exact_ties):
        v, sel = vals, sel0
        for _ in range(n_free):
            m = jnp.max(v, axis=0, keepdims=True)
            if exact_ties:
                hit = jrow == jnp.min(jnp.where(v == m, jrow, n_slc), axis=0, keepdims=True)
            else:
                hit = v == jnp.where(m == -jnp.inf, jnp.nan, m)
            sel = jnp.where(hit, 1.0, sel)
            v = jnp.where(hit, -jnp.inf, v)
        return sel

    sel_fast = pick(False)
    n_cand = jnp.sum(jnp.where(vals > -jnp.inf, 1.0, 0.0), axis=0, keepdims=True)
    n_picked = jnp.sum(sel_fast - sel0, axis=0, keepdims=True)
    unique = jnp.min(jnp.where(n_picked == jnp.minimum(n_cand, float(n_free)), 1.0, 0.0)) > 0.5
    sel = lax.cond(unique, lambda: sel_fast, lambda: pick(True))
    sel_bias = ((sel.T - 1.0) * (-NEG_INF)).astype(BF16)

    n_parts = 2
    hp = R // n_parts
    mp = hp * tq
    qparts = [qs[i * mp:(i + 1) * mp] for i in range(n_parts)]
    qrow = t0 + (lax.broadcasted_iota(jnp.int32, (mp, 1), 0) & (tq - 1))
    kcol = lax.broadcasted_iota(jnp.int32, (1, tk), 1)
    n_chunks = seq // tk
    j_last = t0 // tk

    klane = lax.broadcasted_iota(jnp.int32, (tk, LANES), 1)

    def slc_step(j, carry, masked):
        kv = kvs_ref[0, pl.ds(pl.multiple_of(j * tk, tk), tk), :]
        k_sel = jnp.where(klane < HEAD_DIM, kv, bmat_ref[...]).astype(BF16)
        ones_v = jnp.where(klane < HEAD_DIM, 1.0, kv).astype(BF16)
        off = pl.multiple_of(blocks_per_chunk * (n_chunks - 1 - j), blocks_per_chunk)
        place = pmat_ref[pl.ds(off, n_slc), :].astype(BF16)
        q_bias = jnp.dot(sel_bias, place, preferred_element_type=F32).astype(BF16)
        q_bias = jnp.concatenate([q_bias] * hp, axis=0)
        new = []
        for qp, (m, acc) in zip(qparts, carry):
            s = lax.dot_general(qp + q_bias, k_sel, _NT, preferred_element_type=F32)
            if masked:
                s = jnp.where(j * tk + kcol <= qrow, s, NEG_INF)
            m_new = jnp.maximum(m, jnp.max(s, axis=1, keepdims=True))
            p = jnp.exp2(s - m_new).astype(BF16)
            acc = jnp.exp2(m - m_new) * acc + jnp.dot(p, ones_v, preferred_element_type=F32)
            new.append((m_new, acc))
        return tuple(new)

    init = (jnp.full((mp, 1), NEG_INF, F32), jnp.zeros((mp, LANES), F32))
    n_pairs = j_last // 2
    carry = lax.fori_loop(0, n_pairs, lambda i, c: slc_step(2 * i + 1, slc_step(2 * i, c, False), False),
                          (init,) * n_parts)
    carry = lax.fori_loop(2 * n_pairs, j_last, functools.partial(slc_step, masked=False), carry)
    carry = slc_step(j_last, carry, True)
    o_s = jnp.concatenate([acc / acc[:, 0:1] for (_, acc) in carry], axis=0)

    gates = gate_ref[0]
    lane = lax.broadcasted_iota(jnp.int32, (tq, LANES), 1)
    comb = []
    for r in range(R):
        rs = slice(r * tq, (r + 1) * tq)
        comb.append(gates[:, r:r + 1] * o_c[rs] + gates[:, R + r:R + r + 1] * o_s[rs]
                    + gates[:, 2 * R + r:2 * R + r + 1] * o_w[rs])
    out = [jnp.where(lane < HEAD_DIM, pltpu.roll(comb[2 * i], HEAD_DIM, axis=1), comb[2 * i + 1])
           for i in range(R // 2)]
    o_ref[0] = jnp.concatenate(out, axis=1).astype(o_ref.dtype)


def nsa_attention(qkv, kvc, kvct, gates, *, tq=256, tk=1024):
    B, S, _ = qkv.shape
    G, R = NSA_GROUPS, NSA_Q_PER_GROUP
    tk = min(tk, S)
    n_slc = S // SLC_BLOCK
    n_cmp = kvc.shape[2]
    bpc = tk // SLC_BLOCK
    assert bpc <= LANES - HEAD_DIM
    off = bpc * (S // tk - 1)
    lane = np.arange(LANES)[None, :]
    bmat = jnp.asarray(lane - HEAD_DIM == np.arange(tk)[:, None] // SLC_BLOCK, BF16)
    pmat = jnp.asarray((np.arange(n_slc + off)[:, None] - off == lane - HEAD_DIM) & (lane >= HEAD_DIM)
                       & (lane < HEAD_DIM + bpc), F32)
    slc_blk0, win_blk0 = N_HEADS, N_HEADS + G
    return pl.pallas_call(
        functools.partial(_nsa_attn_kernel, tq=tq, tk=tk, seq=S),
        grid=(B, G, S // tq),
        in_specs=[pl.BlockSpec((1, tq, R * LANES), lambda b, g, i: (b, i, g)),
                  pl.BlockSpec((1, S, LANES), lambda b, g, i: (b, 0, slc_blk0 + g)),
                  pl.BlockSpec((1, S, LANES), lambda b, g, i: (b, 0, win_blk0 + g)),
                  pl.BlockSpec((1, 1, n_cmp, LANES), lambda b, g, i: (b, g, 0, 0)),
                  pl.BlockSpec((1, 1, LANES, n_cmp), lambda b, g, i: (b, g, 0, 0)),
                  pl.BlockSpec((1, tq, LANES), lambda b, g, i: (b, i, g)),
                  pl.BlockSpec(bmat.shape, lambda b, g, i: (0, 0)),
                  pl.BlockSpec(pmat.shape, lambda b, g, i: (0, 0))],
        out_specs=pl.BlockSpec((1, tq, R * HEAD_DIM), lambda b, g, i: (b, i, g)),
        out_shape=jax.ShapeDtypeStruct((B, S, ATTN_WIDTH), BF16),
        compiler_params=_cparams(("parallel", "parallel", "arbitrary")),
    )(qkv, qkv, qkv, kvc, kvct, gates, bmat, pmat)


def nsa_layer(h, gamma, w_in, pe_k, w1_k, w2_k, pe_v, w1_v, w2_v, w_out, B, S):
    G, R, hd, aw = NSA_GROUPS, NSA_Q_PER_GROUP, HEAD_DIM, ATTN_WIDTH
    kvd = G * hd
    sec = lambda i: w_in[:, aw + i * kvd: aw + (i + 1) * kvd]
    wq = w_in[:, :aw] * (hd ** -0.5 * LOG2E)
    wa = jnp.concatenate([_head_slots(wq, N_HEADS), _head_slots(sec(2), G, sec(3)),
                          _head_slots(sec(4), G, sec(5))], axis=1).astype(BF16)
    wb = jnp.concatenate([_head_slots(_rot_half_cols(wq), N_HEADS), _head_slots(_rot_half_cols(sec(2)), G),
                          _head_slots(_rot_half_cols(sec(4)), G)], axis=1).astype(BF16)
    cos, sin = _rope_tables(S)
    qkv = norm_matmul(h, gamma, wa, wb=wb, cos=cos, sin=sin)
    kc_src = norm_matmul(h, gamma, sec(0).astype(BF16), wb=_rot_half_cols(sec(0)).astype(BF16),
                         cos=cos, sin=sin, table_of_tile=lambda j: 1)
    vc_src = norm_matmul(h, gamma, sec(1).astype(BF16))
    wg = w_in[:, aw + 6 * kvd:].reshape(-1, 3, G, R)
    wg = jnp.transpose(wg, (0, 2, 1, 3)).reshape(-1, G, 3 * R)
    wg = jnp.pad(wg, ((0, 0), (0, 0), (0, LANES - 3 * R))).reshape(-1, G * LANES).astype(BF16)
    gates = norm_matmul(h, gamma, wg, act="sigmoid", out_dtype=F32)
    kc = nsa_compress(kc_src.reshape(B, S, kvd), pe_k, w1_k, w2_k)
    vc = nsa_compress(vc_src.reshape(B, S, kvd), pe_v, w1_v, w2_v)
    n_cmp = kc.shape[1]
    kvc = jnp.concatenate([kc.reshape(B, n_cmp, G, hd), vc.reshape(B, n_cmp, G, hd)], axis=-1)
    kvc = jnp.transpose(kvc, (0, 2, 1, 3)).astype(BF16)
    kvct = jnp.swapaxes(kvc, 2, 3)
    o = nsa_attention(qkv.reshape(B, S, -1), kvc, kvct, gates.reshape(B, S, G * LANES))
    return matmul_residual(o.reshape(B * S, aw), w_out.astype(BF16), h)


def kernel(x, l0_attn_norm, l0_w_in, l0_cmp_pe_k, l0_cmp_w1_k, l0_cmp_w2_k, l0_cmp_pe_v, l0_cmp_w1_v,
           l0_cmp_w2_v, l0_w_out, l0_ffn_norm, l0_peer_wq, l0_peer_keys, l0_peer_u, l0_peer_v,
           l1_attn_norm, l1_w_in, l1_f_bias, l1_w_out, l1_ffn_norm, l1_peer_wq, l1_peer_keys, l1_peer_u,
           l1_peer_v, final_norm):
    B, S, D = x.shape
    h = x.reshape(B * S, D)
    h = nsa_layer(h, l0_attn_norm, l0_w_in, l0_cmp_pe_k, l0_cmp_w1_k, l0_cmp_w2_k, l0_cmp_pe_v, l0_cmp_w1_v,
                  l0_cmp_w2_v, l0_w_out, B, S)
    h = peer_layer(h, l0_ffn_norm, l0_peer_wq, l0_peer_keys, l0_peer_u, l0_peer_v)
    h = fox_layer(h, l1_attn_norm, l1_w_in, l1_f_bias, l1_w_out, B, S)
    h = peer_layer(h, l1_ffn_norm, l1_peer_wq, l1_peer_keys, l1_peer_u, l1_peer_v)
    return rmsnorm(h, final_norm).reshape(B, S, D)
```

```python
import functools

import numpy as np
import jax
import jax.numpy as jnp
from jax import lax
from jax.experimental import pallas as pl
from jax.experimental.pallas import tpu as pltpu

F32 = jnp.float32
BF16 = jnp.bfloat16

D_MODEL = 1024
N_HEADS = 16
HEAD_DIM = 64
ATTN_WIDTH = N_HEADS * HEAD_DIM
NSA_GROUPS = 4
NSA_Q_PER_GROUP = N_HEADS // NSA_GROUPS
CMP_BLOCK = 32
CMP_STRIDE = 16
CMP_HIDDEN = 2 * HEAD_DIM
SLC_BLOCK = 64
SLC_TOPK = 16
WINDOW = 512
FORCE_SCORE = 1.0e4
ROPE_THETA = 10000.0
PEER_HEADS = 8
PEER_N_KEYS = 128
PEER_TOPK = 16
PEER_HALF_DIM = 128
RMS_EPS = 1e-6
NEG_INF = -1e30
LOG2E = 1.4426950408889634

LANES = 128
VMEM_LIMIT_BYTES = 56 * 1024 * 1024

_NT = (((1,), (1,)), ((), ()))


def _cparams(sem, vmem=VMEM_LIMIT_BYTES, flags=None):
    return pltpu.CompilerParams(dimension_semantics=sem, vmem_limit_bytes=vmem, flags=flags)


def _gelu_tanh(x):
    return 0.5 * x * (1.0 + jnp.tanh(0.7978845608028654 * (x + 0.044715 * (x * x * x))))


def _gelu_sigmoid(x):
    c = -2.0 * 0.7978845608028654 * LOG2E
    t = x * (c + (c * 0.044715) * (x * x))
    return x / (1.0 + jnp.exp2(t))


def _rms_rows(x, g):
    ms = jnp.mean(x * x, axis=-1, keepdims=True)
    return x * lax.rsqrt(ms + RMS_EPS) * g


def _norm_mm_kernel(*refs, act, has_bias, rope, emit_xn):
    it = iter(refs)
    x_ref, g_ref = next(it), next(it)
    wa_ref = next(it)
    wb_ref = next(it) if rope else None
    cos_ref = next(it) if rope else None
    sin_ref = next(it) if rope else None
    b_ref = next(it) if has_bias else None
    o_ref = next(it)
    xo_ref = next(it) if emit_xn else None
    xn_ref = next(it)

    @pl.when(pl.program_id(1) == 0)
    def _():
        xn = _rms_rows(x_ref[...], g_ref[...])
        xn_ref[...] = xn.astype(BF16)
        if emit_xn:
            xo_ref[...] = xn.T.astype(BF16)

    xn = xn_ref[...]
    y = jnp.dot(xn, wa_ref[...], preferred_element_type=F32)
    if rope:
        yb = jnp.dot(xn, wb_ref[...], preferred_element_type=F32)
        cos, sin = cos_ref[...], sin_ref[...]
        for s in range(y.shape[1] // LANES):
            sl = slice(s * LANES, (s + 1) * LANES)
            o_ref[:, sl] = (y[:, sl] * cos + yb[:, sl] * sin).astype(o_ref.dtype)
        return
    if has_bias:
        y = y + b_ref[...]
    if act == "sigmoid":
        y = jax.nn.sigmoid(y)
    elif act == "log_sigmoid":
        y = jax.nn.log_sigmoid(y)
    o_ref[...] = y.astype(o_ref.dtype)


def norm_matmul(x, gamma, wa, *, wb=None, cos=None, sin=None, table_of_tile=None, bias=None,
                act=None, out_dtype=BF16, emit_xn=False, tm=1024, tn=1024):
    T, D = x.shape
    N = wa.shape[1]
    tm, tn = min(tm, T), min(tn, N)
    assert T % tm == 0 and N % tn == 0 and tn % LANES == 0
    rope = wb is not None
    in_specs = [pl.BlockSpec((tm, D), lambda i, j: (i, 0)),
                pl.BlockSpec((1, D), lambda i, j: (0, 0)),
                pl.BlockSpec((D, tn), lambda i, j: (0, j))]
    args = [x, gamma.reshape(1, D).astype(F32), wa]
    if rope:
        S = cos.shape[1]
        assert S % tm == 0
        n_pos = S // tm
        tmap = table_of_tile if table_of_tile is not None else (lambda j: 0)
        in_specs += [pl.BlockSpec((D, tn), lambda i, j: (0, j)),
                     pl.BlockSpec((None, tm, LANES), lambda i, j: (tmap(j), i % n_pos, 0)),
                     pl.BlockSpec((None, tm, LANES), lambda i, j: (tmap(j), i % n_pos, 0))]
        args += [wb, cos, sin]
    if bias is not None:
        in_specs.append(pl.BlockSpec((1, tn), lambda i, j: (0, j)))
        args.append(bias.reshape(1, N).astype(F32))
    out_shape = [jax.ShapeDtypeStruct((T, N), out_dtype)]
    out_specs = [pl.BlockSpec((tm, tn), lambda i, j: (i, j))]
    if emit_xn:
        out_shape.append(jax.ShapeDtypeStruct((D, T), BF16))
        out_specs.append(pl.BlockSpec((D, tm), lambda i, j: (0, i)))
    res = pl.pallas_call(
        functools.partial(_norm_mm_kernel, act=act, has_bias=bias is not None, rope=rope, emit_xn=emit_xn),
        grid=(T // tm, N // tn),
        in_specs=in_specs,
        out_specs=out_specs,
        out_shape=out_shape,
        scratch_shapes=[pltpu.VMEM((tm, D), BF16)],
        compiler_params=_cparams(("parallel", "arbitrary")),
    )(*args)
    return res if emit_xn else res[0]


def _mm_res_kernel(a_ref, w_ref, r_ref, o_ref):
    o_ref[...] = r_ref[...] + jnp.dot(a_ref[...], w_ref[...], preferred_element_type=F32)


def matmul_residual(a, w, res, *, tm=1024, tn=1024):
    T, K = a.shape
    N = w.shape[1]
    tm, tn = min(tm, T), min(tn, N)
    assert T % tm == 0 and N % tn == 0
    return pl.pallas_call(
        _mm_res_kernel,
        grid=(T // tm, N // tn),
        in_specs=[pl.BlockSpec((tm, K), lambda i, j: (i, 0)),
                  pl.BlockSpec((K, tn), lambda i, j: (0, j)),
                  pl.BlockSpec((tm, tn), lambda i, j: (i, j))],
        out_specs=pl.BlockSpec((tm, tn), lambda i, j: (i, j)),
        out_shape=jax.ShapeDtypeStruct((T, N), F32),
        compiler_params=_cparams(("parallel", "arbitrary")),
    )(a, w, res)


def _rmsnorm_kernel(x_ref, g_ref, o_ref):
    o_ref[...] = _rms_rows(x_ref[...], g_ref[...])


def rmsnorm(x, gamma, *, tm=1024):
    T, D = x.shape
    tm = min(tm, T)
    return pl.pallas_call(
        _rmsnorm_kernel,
        grid=(T // tm,),
        in_specs=[pl.BlockSpec((tm, D), lambda i: (i, 0)), pl.BlockSpec((1, D), lambda i: (0, 0))],
        out_specs=pl.BlockSpec((tm, D), lambda i: (i, 0)),
        out_shape=jax.ShapeDtypeStruct((T, D), F32),
        compiler_params=_cparams(("parallel",)),
    )(x, gamma.reshape(1, D).astype(F32))


def _peer_cand_tables(tn):
    fidx, vmask = [], []
    for k2 in range(16):
        fidx.append(k2); vmask.append(0.0)
    for k1 in range(1, 8):
        lim = PEER_TOPK // (k1 + 1)
        for k2 in range(8):
            fidx.append(k1 * 16 + k2); vmask.append(0.0 if k2 < lim else -np.inf)
    for k1 in range(8, 16):
        fidx.append(k1 * 16); vmask.append(0.0)
    fidx = np.broadcast_to(np.asarray(fidx, np.int32)[:, None], (80, tn))
    vmask = np.broadcast_to(np.asarray(vmask, np.float32)[:, None], (80, tn))
    return jnp.asarray(fidx), jnp.asarray(vmask)


def _top16_rows(s, exact_ties):
    n, tn = s.shape
    rows = lax.broadcasted_iota(jnp.int32, (n, tn), 0)
    rows16 = lax.broadcasted_iota(jnp.int32, (PEER_TOPK, tn), 0)
    rank = jnp.full((n, tn), float(PEER_TOPK), F32)
    tops = jnp.zeros((PEER_TOPK, tn), F32)
    v = s
    for k in range(PEER_TOPK):
        m = jnp.max(v, axis=0, keepdims=True)
        if exact_ties:
            hit = rows == jnp.min(jnp.where(v == m, rows, n), axis=0, keepdims=True)
        else:
            hit = v == m
        rank = jnp.where(hit, float(k), rank)
        v = jnp.where(hit, -jnp.inf, v)
        tops = jnp.where(rows16 == k, m, tops)
    return tops, rank, v


def _peer_select_head(q_ref, keys_ref, fidx, vmask, exact_ties):
    tops, ranks, es, picked = [], [], [], []
    for p in range(2):
        q = q_ref[:, p * PEER_HALF_DIM:(p + 1) * PEER_HALF_DIM]
        s = lax.dot_general(keys_ref[p], q, _NT, preferred_element_type=F32)
        t, r, v = _top16_rows(s, exact_ties)
        tops.append(t); ranks.append(r)
        es.append(jnp.exp(s - t[0:1, :]))
        picked.append(jnp.sum(jnp.where(v == -jnp.inf, 1.0, 0.0), axis=0, keepdims=True))
    ts1, ts2 = tops
    pieces = [ts1[0:1, :] + ts2]
    for k1 in range(1, 8):
        pieces.append(ts1[k1:k1 + 1, :] + ts2[0:8, :])
    pieces.append(ts1[8:16, :] + ts2[0:1, :])
    cand0 = jnp.concatenate(pieces, axis=0) + vmask
    cand = cand0
    for _ in range(PEER_TOPK):
        m = jnp.max(cand, axis=0, keepdims=True)
        if exact_ties:
            hit = fidx == jnp.min(jnp.where(cand == m, fidx, 4096), axis=0, keepdims=True)
        else:
            hit = cand == m
        cand = jnp.where(hit, -jnp.inf, cand)
    taken = jnp.logical_and(cand == -jnp.inf, vmask == 0.0)
    takenf = taken.astype(F32)
    picked.append(jnp.sum(takenf, axis=0, keepdims=True))
    unique = jnp.min(jnp.where((picked[0] == PEER_TOPK) & (picked[1] == PEER_TOPK) & (picked[2] == PEER_TOPK),
                               1.0, 0.0)) > 0.5
    best = ts1[0:1, :] + ts2[0:1, :]
    z = jnp.sum(jnp.where(taken, jnp.exp(cand0 - best), 0.0), axis=0, keepdims=True)
    counts = [jnp.sum(takenf[0:16, :], axis=0, keepdims=True)]
    for k1 in range(1, 8):
        counts.append(jnp.sum(takenf[16 + 8 * (k1 - 1):16 + 8 * k1, :], axis=0, keepdims=True))
    tail = takenf[72:80, :]
    cnt = jnp.zeros_like(ranks[0])
    for k1 in range(PEER_TOPK):
        nk = counts[k1] if k1 < 8 else tail[k1 - 8:k1 - 7, :]
        cnt = jnp.where(ranks[0] == float(k1), nk, cnt)
    return (cnt, ranks[1], es[0], es[1] / z), unique


def _peer_select_kernel(q_ref, keys_ref, fidx_ref, vmask_ref, cnt_ref, rank2_ref, e1_ref, e2_ref):
    fidx = fidx_ref[...]
    vmask = vmask_ref[...]

    def store(vals):
        for ref, val in zip((cnt_ref, rank2_ref, e1_ref, e2_ref), vals):
            ref[...] = val.astype(ref.dtype)

    vals, unique = _peer_select_head(q_ref, keys_ref, fidx, vmask, exact_ties=False)
    store(vals)

    @pl.when(jnp.logical_not(unique))
    def _():
        store(_peer_select_head(q_ref, keys_ref, fidx, vmask, exact_ties=True)[0])


def peer_select(q, keys, *, tn=512):
    T = q.shape[0]
    tn = min(tn, T)
    fidx, vmask = _peer_cand_tables(tn)
    rows = PEER_HEADS * PEER_N_KEYS
    ospec = pl.BlockSpec((PEER_N_KEYS, tn), lambda i, h: (h, i))
    return pl.pallas_call(
        _peer_select_kernel,
        grid=(T // tn, PEER_HEADS),
        in_specs=[pl.BlockSpec((tn, 2 * PEER_HALF_DIM), lambda i, h: (i, h)),
                  pl.BlockSpec((2, PEER_N_KEYS, PEER_HALF_DIM), lambda i, h: (h, 0, 0)),
                  pl.BlockSpec((80, tn), lambda i, h: (0, 0)),
                  pl.BlockSpec((80, tn), lambda i, h: (0, 0))],
        out_specs=[ospec] * 4,
        out_shape=[jax.ShapeDtypeStruct((rows, T), dt) for dt in (F32, BF16, F32, BF16)],
        compiler_params=_cparams(("parallel", "parallel")),
    )(q, keys, fidx, vmask)


def _peer_dense_kernel(xn_ref, u_ref, vt_ref, cnt_ref, rank2_ref, e1_ref, e2_ref, res_ref, o_ref,
                       acc_ref, g0_ref, g1_ref, *, c_per_step):
    j = pl.program_id(1)
    n_tiles = pl.num_programs(1) - 1

    @pl.when(j == 0)
    def _():
        acc_ref[...] = jnp.zeros_like(acc_ref)
        g1_ref[...] = jnp.zeros_like(g1_ref)

    @pl.when(j % 2 == 0)
    def _():
        _peer_dense_step(xn_ref, u_ref, vt_ref, cnt_ref, rank2_ref, e1_ref, e2_ref, acc_ref,
                         g1_ref, g0_ref, j, c_per_step)

    @pl.when(j % 2 == 1)
    def _():
        _peer_dense_step(xn_ref, u_ref, vt_ref, cnt_ref, rank2_ref, e1_ref, e2_ref, acc_ref,
                         g0_ref, g1_ref, j, c_per_step)

    @pl.when(j == n_tiles)
    def _():
        o_ref[...] = res_ref[...] + acc_ref[...].T


def _peer_dense_step(xn_ref, u_ref, vt_ref, cnt_ref, rank2_ref, e1_ref, e2_ref, acc_ref, g_ref, g_next_ref,
                     j, c_per_step):

    tn = xn_ref.shape[1]
    bf16_rows = 16
    reps = PEER_N_KEYS // bf16_rows

    def row_tile(ref, row):
        r16 = jnp.broadcast_to(ref[pl.ds(row, 1), :], (bf16_rows, tn)).astype(BF16)
        return jnp.concatenate([r16] * reps, axis=0)

    c0 = jnp.maximum(j - 1, 0) * c_per_step
    up_rows = 2 * PEER_N_KEYS
    blocks = []
    for cc in range(c_per_step):
        if (cc * PEER_N_KEYS) % up_rows == 0:
            rs = slice(cc * PEER_N_KEYS, cc * PEER_N_KEYS + up_rows)
            hT = jnp.dot(u_ref[rs, :], xn_ref[...], preferred_element_type=F32)
            g_next_ref[rs, :] = _gelu_sigmoid(hT).astype(BF16)
        c = c0 + cc
        w = None
        for h in range(PEER_HEADS):
            row = h * PEER_N_KEYS + c
            n_row = row_tile(cnt_ref, row)
            e1_row = row_tile(e1_ref, row)
            sl = slice(h * PEER_N_KEYS, (h + 1) * PEER_N_KEYS)
            term = jnp.where(rank2_ref[sl, :] < n_row, e2_ref[sl, :], 0.0) * e1_row
            w = term if w is None else w + term
        blocks.append(w * g_ref[cc * PEER_N_KEYS:(cc + 1) * PEER_N_KEYS, :])
    aT = jnp.concatenate(blocks, axis=0)
    acc_ref[...] += jnp.dot(vt_ref[...], aT, preferred_element_type=F32)


def peer_dense(xn, u, vt, cnt, rank2, e1, e2, res, *, tn=512, te=1024):
    D, T = xn.shape
    E = u.shape[0]
    tn = min(tn, T)
    rows = PEER_HEADS * PEER_N_KEYS
    once = pl.Buffered(1)
    sel_spec = pl.BlockSpec((rows, tn), lambda i, j: (0, i))
    n_tiles = E // te
    return pl.pallas_call(
        functools.partial(_peer_dense_kernel, c_per_step=te // PEER_N_KEYS),
        grid=(T // tn, n_tiles + 1),
        in_specs=[pl.BlockSpec((D, tn), lambda i, j: (0, i)),
                  pl.BlockSpec((te, D), lambda i, j: (jnp.minimum(j, n_tiles - 1), 0)),
                  pl.BlockSpec((D, te), lambda i, j: (0, jnp.maximum(j - 1, 0))),
                  sel_spec, sel_spec, sel_spec, sel_spec,
                  pl.BlockSpec((tn, D), lambda i, j: (i, 0), pipeline_mode=once)],
        out_specs=pl.BlockSpec((tn, D), lambda i, j: (i, 0), pipeline_mode=once),
        out_shape=jax.ShapeDtypeStruct((T, D), F32),
        scratch_shapes=[pltpu.VMEM((D, tn), F32), pltpu.VMEM((te, tn), BF16), pltpu.VMEM((te, tn), BF16)],
        compiler_params=_cparams(("parallel", "arbitrary")),
    )(xn, u, vt, cnt, rank2, e1, e2, res)


def peer_layer(h, gamma, w_q, sub_keys, u, v):
    q, xn = norm_matmul(h, gamma, w_q.astype(BF16), emit_xn=True)
    keys = sub_keys.reshape(2 * PEER_HEADS, PEER_N_KEYS, PEER_HALF_DIM).astype(BF16)
    cnt, rank2, e1, e2 = peer_select(q, keys)
    return peer_dense(xn, u.astype(BF16), v.T.astype(BF16), cnt, rank2, e1, e2, h)


def _cumsum_aug_kernel(lf_ref, tri_ref, place_q_ref, place_k_ref, ones_q_ref, ones_k_ref,
                       qa_ref, ka_ref, carry_ref):
    @pl.when(pl.program_id(1) == 0)
    def _():
        carry_ref[...] = jnp.zeros_like(carry_ref)

    lf = lf_ref[0]
    c = jnp.dot(tri_ref[...], lf, preferred_element_type=F32, precision=lax.Precision.HIGHEST) + carry_ref[...]
    carry_ref[...] = c[-1:, :]
    c = c * LOG2E
    hi = c.astype(BF16)
    r1 = c - hi.astype(F32)
    mid = r1.astype(BF16)
    lo = (r1 - mid.astype(F32)).astype(BF16)
    nh = N_HEADS
    lane = lax.broadcasted_iota(jnp.int32, c.shape, 1)
    parts = jnp.where(lane < nh, hi.astype(F32),
                      jnp.where(lane < 2 * nh, pltpu.roll(mid.astype(F32), nh, axis=1),
                                pltpu.roll(lo.astype(F32), 2 * nh, axis=1)))
    parts = jnp.where(lane < 3 * nh, parts, 0.0).astype(BF16)
    qa_ref[0] = (jnp.dot(parts, place_q_ref[...], preferred_element_type=F32) + ones_q_ref[...]).astype(BF16)
    ka_ref[0] = (jnp.dot(parts, place_k_ref[...], preferred_element_type=F32) + ones_k_ref[...]).astype(BF16)


def fox_bias_operands(logf, *, tc=256):
    B, S, _ = logf.shape
    nh = N_HEADS
    tri = jnp.asarray(np.tril(np.ones((tc, tc), np.float32)))
    pq = np.zeros((LANES, nh * LANES), np.float32)
    pk = np.zeros((LANES, nh * LANES), np.float32)
    oq = np.zeros((1, nh * LANES), np.float32)
    ok = np.zeros((1, nh * LANES), np.float32)
    for h in range(nh):
        for part in range(3):
            pq[part * nh + h, h * LANES + part] = 1.0
            pk[part * nh + h, h * LANES + 3 + part] = -1.0
            oq[0, h * LANES + 3 + part] = 1.0
            ok[0, h * LANES + part] = 1.0
    const = lambda a: pl.BlockSpec(a.shape, lambda b, i: (0,) * a.ndim)
    pq, pk, oq, ok = jnp.asarray(pq, BF16), jnp.asarray(pk, BF16), jnp.asarray(oq), jnp.asarray(ok)
    out = jax.ShapeDtypeStruct((B, S, nh * LANES), BF16)
    return pl.pallas_call(
        _cumsum_aug_kernel,
        grid=(B, S // tc),
        in_specs=[pl.BlockSpec((1, tc, LANES), lambda b, i: (b, i, 0)),
                  const(tri), const(pq), const(pk), const(oq), const(ok)],
        out_specs=[pl.BlockSpec((1, tc, nh * LANES), lambda b, i: (b, i, 0))] * 2,
        out_shape=[out, out],
        scratch_shapes=[pltpu.VMEM((1, LANES), F32)],
        compiler_params=_cparams(("parallel", "arbitrary")),
    )(logf, tri, pq, pk, oq, ok)


def _fox_attn_kernel(q_ref, qa_ref, kv_ref, ka_ref, o_ref, *, tq, tk, heads_per_step):
    qi = pl.program_id(2)
    t0 = qi * tq
    n_full = t0 // tk
    n_diag = tq // tk
    lanes = [slice(hh * LANES, (hh + 1) * LANES) for hh in range(heads_per_step)]
    qs = [jnp.concatenate([q_ref[0, :, lsl], qa_ref[0, :, lsl]], axis=1) for lsl in lanes]

    def step(j, carry, masked):
        rows = pl.ds(pl.multiple_of(j * tk, tk), tk)
        klane = lax.broadcasted_iota(jnp.int32, (tk, LANES), 1)
        new = []
        for lsl, q, (m, acc) in zip(lanes, qs, carry):
            kv = kv_ref[0, rows, lsl]
            kk = jnp.concatenate([kv, ka_ref[0, rows, lsl]], axis=1)
            ones_v = jnp.where(klane < HEAD_DIM, 1.0, kv).astype(BF16)
            s = lax.dot_general(q, kk, _NT, preferred_element_type=F32)
            if masked:
                qpos = t0 + lax.broadcasted_iota(jnp.int32, (tq, 1), 0)
                kpos = j * tk + lax.broadcasted_iota(jnp.int32, (1, tk), 1)
                s = jnp.where(kpos <= qpos, s, NEG_INF)
            m_new = jnp.maximum(m, jnp.max(s, axis=1, keepdims=True))
            p = jnp.exp2(s - m_new).astype(BF16)
            acc = jnp.exp2(m - m_new) * acc + jnp.dot(p, ones_v, preferred_element_type=F32)
            new.append((m_new, acc))
        return tuple(new)

    init = (jnp.full((tq, 1), NEG_INF, F32), jnp.zeros((tq, LANES), F32))
    n_pairs = n_full // 2
    carry = lax.fori_loop(0, n_pairs, lambda i, c: step(2 * i + 1, step(2 * i, c, False), False),
                          (init,) * heads_per_step)
    carry = lax.fori_loop(2 * n_pairs, n_full, functools.partial(step, masked=False), carry)
    for d in range(n_diag):
        carry = step(n_full + d, carry, True)
    outs = [acc / acc[:, 0:1] for (_, acc) in carry]
    lane = lax.broadcasted_iota(jnp.int32, (tq, LANES), 1)
    blocks = []
    for pair in range(heads_per_step // 2):
        a, b = outs[2 * pair], outs[2 * pair + 1]
        blocks.append(jnp.where(lane < HEAD_DIM, pltpu.roll(a, HEAD_DIM, axis=1), b))
    o_ref[0] = jnp.concatenate(blocks, axis=1).astype(o_ref.dtype) if len(blocks) > 1 else blocks[0].astype(o_ref.dtype)


def _head_slots(w, n_heads, second=None):
    D = w.shape[0]
    a = w.reshape(D, n_heads, HEAD_DIM)
    b = jnp.zeros_like(a) if second is None else second.reshape(D, n_heads, HEAD_DIM)
    return jnp.concatenate([a, b], axis=-1).reshape(D, n_heads * LANES)


def fox_layer(h, gamma, w_in, f_bias, w_out, B, S):
    aw = ATTN_WIDTH
    wq = _head_slots(w_in[:, :aw] * (HEAD_DIM ** -0.5 * LOG2E), N_HEADS)
    wkv = _head_slots(w_in[:, aw:2 * aw], N_HEADS, w_in[:, 2 * aw:3 * aw])
    w_main = jnp.concatenate([wq, wkv], axis=1).astype(BF16)
    wf = jnp.pad(w_in[:, 3 * aw:], ((0, 0), (0, LANES - N_HEADS))).astype(BF16)
    bf = jnp.pad(f_bias.astype(F32), (0, LANES - N_HEADS))
    qkv = norm_matmul(h, gamma, w_main)
    logf = norm_matmul(h, gamma, wf, bias=bf, act="log_sigmoid", out_dtype=F32)
    qa, ka = fox_bias_operands(logf.reshape(B, S, LANES))
    qkv = qkv.reshape(B, S, 2 * N_HEADS * LANES)
    o = fox_attention(qkv, qa, ka)
    return matmul_residual(o.reshape(B * S, aw), w_out.astype(BF16), h)


def fox_attention(qkv, qa, ka, *, tq=1024, tk=1024, heads_per_step=2):
    B, S, _ = qkv.shape
    hs = heads_per_step
    wq = hs * LANES
    n_qblk = N_HEADS // hs
    return pl.pallas_call(
        functools.partial(_fox_attn_kernel, tq=tq, tk=tk, heads_per_step=hs),
        grid=(B, n_qblk, S // tq),
        in_specs=[pl.BlockSpec((1, tq, wq), lambda b, h, i: (b, i, h)),
                  pl.BlockSpec((1, tq, wq), lambda b, h, i: (b, i, h)),
                  pl.BlockSpec((1, S, wq), lambda b, h, i: (b, 0, n_qblk + h)),
                  pl.BlockSpec((1, S, wq), lambda b, h, i: (b, 0, h))],
        out_specs=pl.BlockSpec((1, tq, hs * HEAD_DIM), lambda b, h, i: (b, i, h)),
        out_shape=jax.ShapeDtypeStruct((B, S, ATTN_WIDTH), BF16),
        compiler_params=_cparams(("parallel", "parallel", "arbitrary")),
    )(qkv, qa, qkv, ka)


def _rot_half_cols(w):
    D = w.shape[0]
    a = w.reshape(D, -1, HEAD_DIM)
    half = HEAD_DIM // 2
    return jnp.concatenate([-a[..., half:], a[..., :half]], axis=-1).reshape(w.shape)


def _rope_tables(S):
    half = HEAD_DIM // 2
    inv_freq = ROPE_THETA ** (-jnp.arange(half, dtype=F32) / half)
    ang = jnp.arange(S, dtype=F32)[:, None] * inv_freq[None, :]
    c, s = jnp.cos(ang), jnp.sin(ang)
    c2, s2 = jnp.concatenate([c, c], axis=1), jnp.concatenate([s, s], axis=1)
    cos = jnp.stack([jnp.concatenate([c2, jnp.ones_like(c2)], axis=1), jnp.concatenate([c2, c2], axis=1)])
    sin = jnp.stack([jnp.concatenate([s2, jnp.zeros_like(s2)], axis=1), jnp.concatenate([s2, s2], axis=1)])
    return cos, sin


def _compress_kernel(x_ref, pea_ref, peb_ref, wa_ref, wb_ref, w2_ref, o_ref, pa_ref, pb0_ref, *, n_rows):
    u = pl.program_id(1)
    x = x_ref[0].astype(F32)
    pa = jnp.dot((x + pea_ref[...]).astype(BF16), wa_ref[...], preferred_element_type=F32)
    pb = jnp.dot((x + peb_ref[...]).astype(BF16), wb_ref[...], preferred_element_type=F32)

    def emit(slab, hid):
        y = jnp.dot(_gelu_tanh(hid).astype(BF16), w2_ref[...], preferred_element_type=F32)
        o_ref[0, pl.ds(pl.multiple_of(slab * n_rows, n_rows), n_rows), :] = y

    @pl.when(u == 0)
    def _():
        pb0_ref[...] = pb

    @pl.when(u > 0)
    def _():
        emit(u - 1, pa_ref[...] + pb)

    @pl.when(u == 3)
    def _():
        emit(3, pa + pltpu.roll(pb0_ref[...], n_rows - 1, axis=0))

    pa_ref[...] = pa


def nsa_compress(src, pe, w1, w2):
    B, S, W = src.shape
    G = NSA_GROUPS
    n_rows = S // 64
    half = CMP_BLOCK // 2
    cw = half * W
    xv = src.reshape(B, n_rows, 4 * cw)
    pe_flat = jnp.transpose(pe, (1, 0, 2)).reshape(CMP_BLOCK, W).astype(F32)
    pea, peb = pe_flat[:half].reshape(1, cw), pe_flat[half:].reshape(1, cw)
    eye = jnp.eye(G, dtype=F32)
    wfull = jnp.einsum('gldh,gk->lkdgh', w1.astype(F32), eye).reshape(CMP_BLOCK, W, G * CMP_HIDDEN)
    wa = wfull[:half].reshape(cw, G * CMP_HIDDEN).astype(BF16)
    wb = wfull[half:].reshape(cw, G * CMP_HIDDEN).astype(BF16)
    w2bd = jnp.einsum('ghd,gk->ghkd', w2.astype(F32), eye).reshape(G * CMP_HIDDEN, W).astype(BF16)
    const = lambda a: pl.BlockSpec(a.shape, lambda b, u: (0,) * a.ndim)
    return pl.pallas_call(
        functools.partial(_compress_kernel, n_rows=n_rows),
        grid=(B, 4),
        in_specs=[pl.BlockSpec((1, n_rows, cw), lambda b, u: (b, 0, u)),
                  const(pea), const(peb), const(wa), const(wb), const(w2bd)],
        out_specs=pl.BlockSpec((1, 4 * n_rows, W), lambda b, u: (b, 0, 0)),
        out_shape=jax.ShapeDtypeStruct((B, 4 * n_rows, W), F32),
        scratch_shapes=[pltpu.VMEM((n_rows, G * CMP_HIDDEN), F32), pltpu.VMEM((n_rows, G * CMP_HIDDEN), F32)],
        compiler_params=_cparams(("parallel", "arbitrary")),
    )(xv, pea, peb, wa, wb, w2bd)


def _nsa_attn_kernel(q_ref, kvs_ref, kvw_ref, kvc_ref, kvct_ref, gate_ref, bmat_ref, pmat_ref, o_ref,
                     *, tq, tk, seq):
    R = NSA_Q_PER_GROUP
    M = R * tq
    n_slc = seq // SLC_BLOCK
    n_cmp = 4 * n_slc
    n_sel = min(SLC_TOPK, n_slc)
    blocks_per_chunk = tk // SLC_BLOCK
    assert tq & (tq - 1) == 0 and n_slc & (n_slc - 1) == 0 and tk % tq == 0
    log_slc = n_slc.bit_length() - 1
    qi = pl.program_id(2)
    t0 = qi * tq

    qs = jnp.concatenate([q_ref[0, :, r * LANES:(r + 1) * LANES] for r in range(R)], axis=0)

    kvc = kvc_ref[0, 0]
    sT = lax.dot_general(kvc, qs, _NT, preferred_element_type=F32)
    rowc = lax.broadcasted_iota(jnp.int32, (n_cmp, 1), 0)
    cmp_end = (rowc & (n_slc - 1)) * SLC_BLOCK + (rowc >> log_slc) * CMP_STRIDE + (CMP_BLOCK - 1)
    tcol = t0 + (lax.broadcasted_iota(jnp.int32, (1, M), 1) & (tq - 1))
    sm = jnp.where(cmp_end <= tcol, sT, NEG_INF)
    mx = jnp.max(sm, axis=0, keepdims=True)
    e = jnp.exp2(sm - mx)
    inv = jnp.where(mx > 0.5 * NEG_INF, 1.0 / jnp.sum(e, axis=0, keepdims=True), 0.0)
    pT = e * inv
    o_c = jnp.dot(kvct_ref[0, 0], pT.astype(BF16), preferred_element_type=F32).T

    wlen = WINDOW + tq
    start = jnp.maximum(t0 - WINDOW, 0)
    kvw = kvw_ref[0, pl.ds(pl.multiple_of(start, tq), wlen), :]
    s_w = lax.dot_general(qs, kvw, _NT, preferred_element_type=F32)
    qpos = t0 + (lax.broadcasted_iota(jnp.int32, (M, 1), 0) & (tq - 1))
    kpos = start + lax.broadcasted_iota(jnp.int32, (1, wlen), 1)
    s_w = jnp.where(kpos <= qpos, jnp.where(kpos > qpos - WINDOW, s_w, NEG_INF), NEG_INF)
    p_w = jnp.exp2(s_w - jnp.max(s_w, axis=1, keepdims=True)).astype(BF16)
    wlane = lax.broadcasted_iota(jnp.int32, (wlen, LANES), 1)
    acc_w = jnp.dot(p_w, jnp.where(wlane < HEAD_DIM, 1.0, kvw).astype(BF16), preferred_element_type=F32)
    o_w = acc_w / acc_w[:, 0:1]

    psum = pT[:, 0:tq]
    for r in range(1, R):
        psum = psum + pT[:, r * tq:(r + 1) * tq]
    p0, p1, p2, p3 = (psum[u * n_slc:(u + 1) * n_slc, :] for u in range(4))
    jrow = lax.broadcasted_iota(jnp.int32, (n_slc, tq), 0)
    p3_prev = jnp.where(jrow == 0, 0.0, pltpu.roll(p3, 1, axis=0))
    imp = p0 + p1 + p2 + 0.5 * p3 + 0.5 * p3_prev
    cur = (t0 + lax.broadcasted_iota(jnp.int32, (n_slc, tq), 1)) >> (SLC_BLOCK.bit_length() - 1)
    forced = (jrow == 0) | (jrow == cur) | (jrow == cur - 1)
    vals = jnp.where(forced, -jnp.inf, jnp.where(jrow <= cur, imp, -jnp.inf))
    sel0 = jnp.where(forced, 1.0, 0.0)
    n_free = n_sel - 3

    def pick(exact_ties):
        v, sel = vals, sel0
        for _ in range(n_free):
            m = jnp.max(v, axis=0, keepdims=True)
            if exact_ties:
                hit = jrow == jnp.min(jnp.where(v == m, jrow, n_slc), axis=0, keepdims=True)
            else:
                hit = v == jnp.where(m == -jnp.inf, jnp.nan, m)
            sel = jnp.where(hit, 1.0, sel)
            v = jnp.where(hit, -jnp.inf, v)
        return sel

    sel_fast = pick(False)
    n_cand = jnp.sum(jnp.where(vals > -jnp.inf, 1.0, 0.0), axis=0, keepdims=True)
    n_picked = jnp.sum(sel_fast - sel0, axis=0, keepdims=True)
    unique = jnp.min(jnp.where(n_picked == jnp.minimum(n_cand, float(n_free)), 1.0, 0.0)) > 0.5
    sel = lax.cond(unique, lambda: sel_fast, lambda: pick(True))
    sel_bias = ((sel.T - 1.0) * (-NEG_INF)).astype(BF16)

    n_parts = 2
    hp = R // n_parts
    mp = hp * tq
    qparts = [qs[i * mp:(i + 1) * mp] for i in range(n_parts)]
    qrow = t0 + (lax.broadcasted_iota(jnp.int32, (mp, 1), 0) & (tq - 1))
    kcol = lax.broadcasted_iota(jnp.int32, (1, tk), 1)
    n_chunks = seq // tk
    j_last = t0 // tk

    klane = lax.broadcasted_iota(jnp.int32, (tk, LANES), 1)

    def slc_step(j, carry, masked):
        kv = kvs_ref[0, pl.ds(pl.multiple_of(j * tk, tk), tk), :]
        k_sel = jnp.where(klane < HEAD_DIM, kv, bmat_ref[...]).astype(BF16)
        ones_v = jnp.where(klane < HEAD_DIM, 1.0, kv).astype(BF16)
        off = pl.multiple_of(blocks_per_chunk * (n_chunks - 1 - j), blocks_per_chunk)
        place = pmat_ref[pl.ds(off, n_slc), :].astype(BF16)
        q_bias = jnp.dot(sel_bias, place, preferred_element_type=F32).astype(BF16)
        q_bias = jnp.concatenate([q_bias] * hp, axis=0)
        new = []
        for qp, (m, acc) in zip(qparts, carry):
            s = lax.dot_general(qp + q_bias, k_sel, _NT, preferred_element_type=F32)
            if masked:
                s = jnp.where(j * tk + kcol <= qrow, s, NEG_INF)
            m_new = jnp.maximum(m, jnp.max(s, axis=1, keepdims=True))
            p = jnp.exp2(s - m_new).astype(BF16)
            acc = jnp.exp2(m - m_new) * acc + jnp.dot(p, ones_v, preferred_element_type=F32)
            new.append((m_new, acc))
        return tuple(new)

    init = (jnp.full((mp, 1), NEG_INF, F32), jnp.zeros((mp, LANES), F32))
    n_pairs = j_last // 2
    carry = lax.fori_loop(0, n_pairs, lambda i, c: slc_step(2 * i + 1, slc_step(2 * i, c, False), False),
                          (init,) * n_parts)
    carry = lax.fori_loop(2 * n_pairs, j_last, functools.partial(slc_step, masked=False), carry)
    carry = slc_step(j_last, carry, True)
    o_s = jnp.concatenate([acc / acc[:, 0:1] for (_, acc) in carry], axis=0)

    gates = gate_ref[0]
    lane = lax.broadcasted_iota(jnp.int32, (tq, LANES), 1)
    comb = []
    for r in range(R):
        rs = slice(r * tq, (r + 1) * tq)
        comb.append(gates[:, r:r + 1] * o_c[rs] + gates[:, R + r:R + r + 1] * o_s[rs]
                    + gates[:, 2 * R + r:2 * R + r + 1] * o_w[rs])
    out = [jnp.where(lane < HEAD_DIM, pltpu.roll(comb[2 * i], HEAD_DIM, axis=1), comb[2 * i + 1])
           for i in range(R // 2)]
    o_ref[0] = jnp.concatenate(out, axis=1).astype(o_ref.dtype)


def nsa_attention(qkv, kvc, kvct, gates, *, tq=256, tk=1024):
    B, S, _ = qkv.shape
    G, R = NSA_GROUPS, NSA_Q_PER_GROUP
    tk = min(tk, S)
    n_slc = S // SLC_BLOCK
    n_cmp = kvc.shape[2]
    bpc = tk // SLC_BLOCK
    assert bpc <= LANES - HEAD_DIM
    off = bpc * (S // tk - 1)
    lane = np.arange(LANES)[None, :]
    bmat = jnp.asarray(lane - HEAD_DIM == np.arange(tk)[:, None] // SLC_BLOCK, BF16)
    pmat = jnp.asarray((np.arange(n_slc + off)[:, None] - off == lane - HEAD_DIM) & (lane >= HEAD_DIM)
                       & (lane < HEAD_DIM + bpc), F32)
    slc_blk0, win_blk0 = N_HEADS, N_HEADS + G
    return pl.pallas_call(
        functools.partial(_nsa_attn_kernel, tq=tq, tk=tk, seq=S),
        grid=(B, G, S // tq),
        in_specs=[pl.BlockSpec((1, tq, R * LANES), lambda b, g, i: (b, i, g)),
                  pl.BlockSpec((1, S, LANES), lambda b, g, i: (b, 0, slc_blk0 + g)),
                  pl.BlockSpec((1, S, LANES), lambda b, g, i: (b, 0, win_blk0 + g)),
                  pl.BlockSpec((1, 1, n_cmp, LANES), lambda b, g, i: (b, g, 0, 0)),
                  pl.BlockSpec((1, 1, LANES, n_cmp), lambda b, g, i: (b, g, 0, 0)),
                  pl.BlockSpec((1, tq, LANES), lambda b, g, i: (b, i, g)),
                  pl.BlockSpec(bmat.shape, lambda b, g, i: (0, 0)),
                  pl.BlockSpec(pmat.shape, lambda b, g, i: (0, 0))],
        out_specs=pl.BlockSpec((1, tq, R * HEAD_DIM), lambda b, g, i: (b, i, g)),
        out_shape=jax.ShapeDtypeStruct((B, S, ATTN_WIDTH), BF16),
        compiler_params=_cparams(("parallel", "parallel", "arbitrary")),
    )(qkv, qkv, qkv, kvc, kvct, gates, bmat, pmat)


def nsa_layer(h, gamma, w_in, pe_k, w1_k, w2_k, pe_v, w1_v, w2_v, w_out, B, S):
    G, R, hd, aw = NSA_GROUPS, NSA_Q_PER_GROUP, HEAD_DIM, ATTN_WIDTH
    kvd = G * hd
    sec = lambda i: w_in[:, aw + i * kvd: aw + (i + 1) * kvd]
    wq = w_in[:, :aw] * (hd ** -0.5 * LOG2E)
    wa = jnp.concatenate([_head_slots(wq, N_HEADS), _head_slots(sec(2), G, sec(3)),
                          _head_slots(sec(4), G, sec(5))], axis=1).astype(BF16)
    wb = jnp.concatenate([_head_slots(_rot_half_cols(wq), N_HEADS), _head_slots(_rot_half_cols(sec(2)), G),
                          _head_slots(_rot_half_cols(sec(4)), G)], axis=1).astype(BF16)
    cos, sin = _rope_tables(S)
    qkv = norm_matmul(h, gamma, wa, wb=wb, cos=cos, sin=sin)
    kc_src = norm_matmul(h, gamma, sec(0).astype(BF16), wb=_rot_half_cols(sec(0)).astype(BF16),
                         cos=cos, sin=sin, table_of_tile=lambda j: 1)
    vc_src = norm_matmul(h, gamma, sec(1).astype(BF16))
    wg = w_in[:, aw + 6 * kvd:].reshape(-1, 3, G, R)
    wg = jnp.transpose(wg, (0, 2, 1, 3)).reshape(-1, G, 3 * R)
    wg = jnp.pad(wg, ((0, 0), (0, 0), (0, LANES - 3 * R))).reshape(-1, G * LANES).astype(BF16)
    gates = norm_matmul(h, gamma, wg, act="sigmoid", out_dtype=F32)
    kc = nsa_compress(kc_src.reshape(B, S, kvd), pe_k, w1_k, w2_k)
    vc = nsa_compress(vc_src.reshape(B, S, kvd), pe_v, w1_v, w2_v)
    n_cmp = kc.shape[1]
    kvc = jnp.concatenate([kc.reshape(B, n_cmp, G, hd), vc.reshape(B, n_cmp, G, hd)], axis=-1)
    kvc = jnp.transpose(kvc, (0, 2, 1, 3)).astype(BF16)
    kvct = jnp.swapaxes(kvc, 2, 3)
    o = nsa_attention(qkv.reshape(B, S, -1), kvc, kvct, gates.reshape(B, S, G * LANES))
    return matmul_residual(o.reshape(B * S, aw), w_out.astype(BF16), h)


def kernel(x, l0_attn_norm, l0_w_in, l0_cmp_pe_k, l0_cmp_w1_k, l0_cmp_w2_k, l0_cmp_pe_v, l0_cmp_w1_v,
           l0_cmp_w2_v, l0_w_out, l0_ffn_norm, l0_peer_wq, l0_peer_keys, l0_peer_u, l0_peer_v,
           l1_attn_norm, l1_w_in, l1_f_bias, l1_w_out, l1_ffn_norm, l1_peer_wq, l1_peer_keys, l1_peer_u,
           l1_peer_v, final_norm):
    B, S, D = x.shape
    h = x.reshape(B * S, D)
    h = nsa_layer(h, l0_attn_norm, l0_w_in, l0_cmp_pe_k, l0_cmp_w1_k, l0_cmp_w2_k, l0_cmp_pe_v, l0_cmp_w1_v,
                  l0_cmp_w2_v, l0_w_out, B, S)
    h = peer_layer(h, l0_ffn_norm, l0_peer_wq, l0_peer_keys, l0_peer_u, l0_peer_v)
    h = fox_layer(h, l1_attn_norm, l1_w_in, l1_f_bias, l1_w_out, B, S)
    h = peer_layer(h, l1_ffn_norm, l1_peer_wq, l1_peer_keys, l1_peer_u, l1_peer_v)
    return rmsnorm(h, final_norm).reshape(B, S, D)
```

```python
import functools

import numpy as np
import jax
import jax.numpy as jnp
from jax import lax
from jax.experimental import pallas as pl
from jax.experimental.pallas import tpu as pltpu

F32 = jnp.float32
BF16 = jnp.bfloat16

D_MODEL = 1024
N_HEADS = 16
HEAD_DIM = 64
ATTN_WIDTH = N_HEADS * HEAD_DIM
NSA_GROUPS = 4
NSA_Q_PER_GROUP = N_HEADS // NSA_GROUPS
CMP_BLOCK = 32
CMP_STRIDE = 16
CMP_HIDDEN = 2 * HEAD_DIM
SLC_BLOCK = 64
SLC_TOPK = 16
WINDOW = 512
FORCE_SCORE = 1.0e4
ROPE_THETA = 10000.0
PEER_HEADS = 8
PEER_N_KEYS = 128
PEER_TOPK = 16
PEER_HALF_DIM = 128
RMS_EPS = 1e-6
NEG_INF = -1e30
LOG2E = 1.4426950408889634

LANES = 128
VMEM_LIMIT_BYTES = 56 * 1024 * 1024

_NT = (((1,), (1,)), ((), ()))


def _cparams(sem, vmem=VMEM_LIMIT_BYTES, flags=None):
    return pltpu.CompilerParams(dimension_semantics=sem, vmem_limit_bytes=vmem, flags=flags)


def _gelu_tanh(x):
    return 0.5 * x * (1.0 + jnp.tanh(0.7978845608028654 * (x + 0.044715 * (x * x * x))))


def _gelu_sigmoid(x):
    c = -2.0 * 0.7978845608028654 * LOG2E
    t = x * (c + (c * 0.044715) * (x * x))
    return x / (1.0 + jnp.exp2(t))


def _rms_rows(x, g):
    ms = jnp.mean(x * x, axis=-1, keepdims=True)
    return x * lax.rsqrt(ms + RMS_EPS) * g


def _norm_mm_kernel(*refs, act, has_bias, rope, emit_xn):
    it = iter(refs)
    x_ref, g_ref = next(it), next(it)
    wa_ref = next(it)
    wb_ref = next(it) if rope else None
    cos_ref = next(it) if rope else None
    sin_ref = next(it) if rope else None
    b_ref = next(it) if has_bias else None
    o_ref = next(it)
    xo_ref = next(it) if emit_xn else None
    xn_ref = next(it)

    @pl.when(pl.program_id(1) == 0)
    def _():
        xn = _rms_rows(x_ref[...], g_ref[...])
        xn_ref[...] = xn.astype(BF16)
        if emit_xn:
            xo_ref[...] = xn.T.astype(BF16)

    xn = xn_ref[...]
    y = jnp.dot(xn, wa_ref[...], preferred_element_type=F32)
    if rope:
        yb = jnp.dot(xn, wb_ref[...], preferred_element_type=F32)
        cos, sin = cos_ref[...], sin_ref[...]
        for s in range(y.shape[1] // LANES):
            sl = slice(s * LANES, (s + 1) * LANES)
            o_ref[:, sl] = (y[:, sl] * cos + yb[:, sl] * sin).astype(o_ref.dtype)
        return
    if has_bias:
        y = y + b_ref[...]
    if act == "sigmoid":
        y = jax.nn.sigmoid(y)
    elif act == "log_sigmoid":
        y = jax.nn.log_sigmoid(y)
    o_ref[...] = y.astype(o_ref.dtype)


def norm_matmul(x, gamma, wa, *, wb=None, cos=None, sin=None, table_of_tile=None, bias=None,
                act=None, out_dtype=BF16, emit_xn=False, tm=1024, tn=1024):
    T, D = x.shape
    N = wa.shape[1]
    tm, tn = min(tm, T), min(tn, N)
    assert T % tm == 0 and N % tn == 0 and tn % LANES == 0
    rope = wb is not None
    in_specs = [pl.BlockSpec((tm, D), lambda i, j: (i, 0)),
                pl.BlockSpec((1, D), lambda i, j: (0, 0)),
                pl.BlockSpec((D, tn), lambda i, j: (0, j))]
    args = [x, gamma.reshape(1, D).astype(F32), wa]
    if rope:
        S = cos.shape[1]
        assert S % tm == 0
        n_pos = S // tm
        tmap = table_of_tile if table_of_tile is not None else (lambda j: 0)
        in_specs += [pl.BlockSpec((D, tn), lambda i, j: (0, j)),
                     pl.BlockSpec((None, tm, LANES), lambda i, j: (tmap(j), i % n_pos, 0)),
                     pl.BlockSpec((None, tm, LANES), lambda i, j: (tmap(j), i % n_pos, 0))]
        args += [wb, cos, sin]
    if bias is not None:
        in_specs.append(pl.BlockSpec((1, tn), lambda i, j: (0, j)))
        args.append(bias.reshape(1, N).astype(F32))
    out_shape = [jax.ShapeDtypeStruct((T, N), out_dtype)]
    out_specs = [pl.BlockSpec((tm, tn), lambda i, j: (i, j))]
    if emit_xn:
        out_shape.append(jax.ShapeDtypeStruct((D, T), BF16))
        out_specs.append(pl.BlockSpec((D, tm), lambda i, j: (0, i)))
    res = pl.pallas_call(
        functools.partial(_norm_mm_kernel, act=act, has_bias=bias is not None, rope=rope, emit_xn=emit_xn),
        grid=(T // tm, N // tn),
        in_specs=in_specs,
        out_specs=out_specs,
        out_shape=out_shape,
        scratch_shapes=[pltpu.VMEM((tm, D), BF16)],
        compiler_params=_cparams(("parallel", "arbitrary")),
    )(*args)
    return res if emit_xn else res[0]


def _mm_res_kernel(a_ref, w_ref, r_ref, o_ref):
    o_ref[...] = r_ref[...] + jnp.dot(a_ref[...], w_ref[...], preferred_element_type=F32)


def matmul_residual(a, w, res, *, tm=1024, tn=1024):
    T, K = a.shape
    N = w.shape[1]
    tm, tn = min(tm, T), min(tn, N)
    assert T % tm == 0 and N % tn == 0
    return pl.pallas_call(
        _mm_res_kernel,
        grid=(T // tm, N // tn),
        in_specs=[pl.BlockSpec((tm, K), lambda i, j: (i, 0)),
                  pl.BlockSpec((K, tn), lambda i, j: (0, j)),
                  pl.BlockSpec((tm, tn), lambda i, j: (i, j))],
        out_specs=pl.BlockSpec((tm, tn), lambda i, j: (i, j)),
        out_shape=jax.ShapeDtypeStruct((T, N), F32),
        compiler_params=_cparams(("parallel", "arbitrary")),
    )(a, w, res)


def _rmsnorm_kernel(x_ref, g_ref, o_ref):
    o_ref[...] = _rms_rows(x_ref[...], g_ref[...])


def rmsnorm(x, gamma, *, tm=1024):
    T, D = x.shape
    tm = min(tm, T)
    return pl.pallas_call(
        _rmsnorm_kernel,
        grid=(T // tm,),
        in_specs=[pl.BlockSpec((tm, D), lambda i: (i, 0)), pl.BlockSpec((1, D), lambda i: (0, 0))],
        out_specs=pl.BlockSpec((tm, D), lambda i: (i, 0)),
        out_shape=jax.ShapeDtypeStruct((T, D), F32),
        compiler_params=_cparams(("parallel",)),
    )(x, gamma.reshape(1, D).astype(F32))


def _peer_cand_tables(tn):
    fidx, vmask = [], []
    for k2 in range(16):
        fidx.append(k2); vmask.append(0.0)
    for k1 in range(1, 8):
        lim = PEER_TOPK // (k1 + 1)
        for k2 in range(8):
            fidx.append(k1 * 16 + k2); vmask.append(0.0 if k2 < lim else -np.inf)
    for k1 in range(8, 16):
        fidx.append(k1 * 16); vmask.append(0.0)
    fidx = np.broadcast_to(np.asarray(fidx, np.int32)[:, None], (80, tn))
    vmask = np.broadcast_to(np.asarray(vmask, np.float32)[:, None], (80, tn))
    return jnp.asarray(fidx), jnp.asarray(vmask)


def _top16_rows(s, exact_ties):
    n, tn = s.shape
    rows = lax.broadcasted_iota(jnp.int32, (n, tn), 0)
    rows16 = lax.broadcasted_iota(jnp.int32, (PEER_TOPK, tn), 0)
    rank = jnp.full((n, tn), float(PEER_TOPK), F32)
    tops = jnp.zeros((PEER_TOPK, tn), F32)
    v = s
    for k in range(PEER_TOPK):
        m = jnp.max(v, axis=0, keepdims=True)
        if exact_ties:
            hit = rows == jnp.min(jnp.where(v == m, rows, n), axis=0, keepdims=True)
        else:
            hit = v == m
        rank = jnp.where(hit, float(k), rank)
        v = jnp.where(hit, -jnp.inf, v)
        tops = jnp.where(rows16 == k, m, tops)
    return tops, rank, v


def _peer_select_head(q_ref, keys_ref, fidx, vmask, exact_ties):
    tops, ranks, es, picked = [], [], [], []
    for p in range(2):
        q = q_ref[:, p * PEER_HALF_DIM:(p + 1) * PEER_HALF_DIM]
        s = lax.dot_general(keys_ref[p], q, _NT, preferred_element_type=F32)
        t, r, v = _top16_rows(s, exact_ties)
        tops.append(t); ranks.append(r)
        es.append(jnp.exp(s - t[0:1, :]))
        picked.append(jnp.sum(jnp.where(v == -jnp.inf, 1.0, 0.0), axis=0, keepdims=True))
    ts1, ts2 = tops
    pieces = [ts1[0:1, :] + ts2]
    for k1 in range(1, 8):
        pieces.append(ts1[k1:k1 + 1, :] + ts2[0:8, :])
    pieces.append(ts1[8:16, :] + ts2[0:1, :])
    cand0 = jnp.concatenate(pieces, axis=0) + vmask
    cand = cand0
    for _ in range(PEER_TOPK):
        m = jnp.max(cand, axis=0, keepdims=True)
        if exact_ties:
            hit = fidx == jnp.min(jnp.where(cand == m, fidx, 4096), axis=0, keepdims=True)
        else:
            hit = cand == m
        cand = jnp.where(hit, -jnp.inf, cand)
    taken = jnp.logical_and(cand == -jnp.inf, vmask == 0.0)
    takenf = taken.astype(F32)
    picked.append(jnp.sum(takenf, axis=0, keepdims=True))
    unique = jnp.min(jnp.where((picked[0] == PEER_TOPK) & (picked[1] == PEER_TOPK) & (picked[2] == PEER_TOPK),
                               1.0, 0.0)) > 0.5
    best = ts1[0:1, :] + ts2[0:1, :]
    z = jnp.sum(jnp.where(taken, jnp.exp(cand0 - best), 0.0), axis=0, keepdims=True)
    counts = [jnp.sum(takenf[0:16, :], axis=0, keepdims=True)]
    for k1 in range(1, 8):
        counts.append(jnp.sum(takenf[16 + 8 * (k1 - 1):16 + 8 * k1, :], axis=0, keepdims=True))
    tail = takenf[72:80, :]
    cnt = jnp.zeros_like(ranks[0])
    for k1 in range(PEER_TOPK):
        nk = counts[k1] if k1 < 8 else tail[k1 - 8:k1 - 7, :]
        cnt = jnp.where(ranks[0] == float(k1), nk, cnt)
    return (cnt, ranks[1], es[0], es[1] / z), unique


def _peer_select_kernel(q_ref, keys_ref, fidx_ref, vmask_ref, cnt_ref, rank2_ref, e1_ref, e2_ref):
    fidx = fidx_ref[...]
    vmask = vmask_ref[...]

    def store(vals):
        for ref, val in zip((cnt_ref, rank2_ref, e1_ref, e2_ref), vals):
            ref[...] = val.astype(ref.dtype)

    vals, unique = _peer_select_head(q_ref, keys_ref, fidx, vmask, exact_ties=False)
    store(vals)

    @pl.when(jnp.logical_not(unique))
    def _():
        store(_peer_select_head(q_ref, keys_ref, fidx, vmask, exact_ties=True)[0])


def peer_select(q, keys, *, tn=512):
    T = q.shape[0]
    tn = min(tn, T)
    fidx, vmask = _peer_cand_tables(tn)
    rows = PEER_HEADS * PEER_N_KEYS
    ospec = pl.BlockSpec((PEER_N_KEYS, tn), lambda i, h: (h, i))
    return pl.pallas_call(
        _peer_select_kernel,
        grid=(T // tn, PEER_HEADS),
        in_specs=[pl.BlockSpec((tn, 2 * PEER_HALF_DIM), lambda i, h: (i, h)),
                  pl.BlockSpec((2, PEER_N_KEYS, PEER_HALF_DIM), lambda i, h: (h, 0, 0)),
                  pl.BlockSpec((80, tn), lambda i, h: (0, 0)),
                  pl.BlockSpec((80, tn), lambda i, h: (0, 0))],
        out_specs=[ospec] * 4,
        out_shape=[jax.ShapeDtypeStruct((rows, T), dt) for dt in (F32, BF16, F32, BF16)],
        compiler_params=_cparams(("parallel", "parallel")),
    )(q, keys, fidx, vmask)


def _peer_dense_kernel(xn_ref, u_ref, vt_ref, cnt_ref, rank2_ref, e1_ref, e2_ref, res_ref, o_ref,
                       acc_ref, g0_ref, g1_ref, *, c_per_step):
    j = pl.program_id(1)
    n_tiles = pl.num_programs(1) - 1

    @pl.when(j == 0)
    def _():
        acc_ref[...] = jnp.zeros_like(acc_ref)
        g1_ref[...] = jnp.zeros_like(g1_ref)

    @pl.when(j % 2 == 0)
    def _():
        _peer_dense_step(xn_ref, u_ref, vt_ref, cnt_ref, rank2_ref, e1_ref, e2_ref, acc_ref,
                         g1_ref, g0_ref, j, c_per_step)

    @pl.when(j % 2 == 1)
    def _():
        _peer_dense_step(xn_ref, u_ref, vt_ref, cnt_ref, rank2_ref, e1_ref, e2_ref, acc_ref,
                         g0_ref, g1_ref, j, c_per_step)

    @pl.when(j == n_tiles)
    def _():
        o_ref[...] = res_ref[...] + acc_ref[...].T


def _peer_dense_step(xn_ref, u_ref, vt_ref, cnt_ref, rank2_ref, e1_ref, e2_ref, acc_ref, g_ref, g_next_ref,
                     j, c_per_step):

    tn = xn_ref.shape[1]
    bf16_rows = 16
    reps = PEER_N_KEYS // bf16_rows

    def row_tile(ref, row):
        r16 = jnp.broadcast_to(ref[pl.ds(row, 1), :], (bf16_rows, tn)).astype(BF16)
        return jnp.concatenate([r16] * reps, axis=0)

    c0 = jnp.maximum(j - 1, 0) * c_per_step
    up_rows = 2 * PEER_N_KEYS
    blocks = []
    for cc in range(c_per_step):
        if (cc * PEER_N_KEYS) % up_rows == 0:
            rs = slice(cc * PEER_N_KEYS, cc * PEER_N_KEYS + up_rows)
            hT = jnp.dot(u_ref[rs, :], xn_ref[...], preferred_element_type=F32)
            g_next_ref[rs, :] = _gelu_sigmoid(hT.astype(BF16))
        c = c0 + cc
        w = None
        for h in range(PEER_HEADS):
            row = h * PEER_N_KEYS + c
            n_row = row_tile(cnt_ref, row)
            e1_row = row_tile(e1_ref, row)
            sl = slice(h * PEER_N_KEYS, (h + 1) * PEER_N_KEYS)
            term = jnp.where(rank2_ref[sl, :] < n_row, e2_ref[sl, :], 0.0) * e1_row
            w = term if w is None else w + term
        blocks.append(w * g_ref[cc * PEER_N_KEYS:(cc + 1) * PEER_N_KEYS, :])
    aT = jnp.concatenate(blocks, axis=0)
    acc_ref[...] += jnp.dot(vt_ref[...], aT, preferred_element_type=F32)


def peer_dense(xn, u, vt, cnt, rank2, e1, e2, res, *, tn=512, te=1024):
    D, T = xn.shape
    E = u.shape[0]
    tn = min(tn, T)
    rows = PEER_HEADS * PEER_N_KEYS
    sel_spec = pl.BlockSpec((rows, tn), lambda i, j: (0, i))
    n_tiles = E // te
    return pl.pallas_call(
        functools.partial(_peer_dense_kernel, c_per_step=te // PEER_N_KEYS),
        grid=(T // tn, n_tiles + 1),
        in_specs=[pl.BlockSpec((D, tn), lambda i, j: (0, i)),
                  pl.BlockSpec((te, D), lambda i, j: (jnp.minimum(j, n_tiles - 1), 0)),
                  pl.BlockSpec((D, te), lambda i, j: (0, jnp.maximum(j - 1, 0))),
                  sel_spec, sel_spec, sel_spec, sel_spec,
                  pl.BlockSpec((tn, D), lambda i, j: (i, 0))],
        out_specs=pl.BlockSpec((tn, D), lambda i, j: (i, 0)),
        out_shape=jax.ShapeDtypeStruct((T, D), F32),
        scratch_shapes=[pltpu.VMEM((D, tn), F32), pltpu.VMEM((te, tn), BF16), pltpu.VMEM((te, tn), BF16)],
        compiler_params=_cparams(("parallel", "arbitrary")),
    )(xn, u, vt, cnt, rank2, e1, e2, res)


def peer_layer(h, gamma, w_q, sub_keys, u, v):
    q, xn = norm_matmul(h, gamma, w_q.astype(BF16), emit_xn=True)
    keys = sub_keys.reshape(2 * PEER_HEADS, PEER_N_KEYS, PEER_HALF_DIM).astype(BF16)
    cnt, rank2, e1, e2 = peer_select(q, keys)
    return peer_dense(xn, u.astype(BF16), v.T.astype(BF16), cnt, rank2, e1, e2, h)


def _cumsum_aug_kernel(lf_ref, tri_ref, place_q_ref, place_k_ref, ones_q_ref, ones_k_ref,
                       qa_ref, ka_ref, carry_ref):
    @pl.when(pl.program_id(1) == 0)
    def _():
        carry_ref[...] = jnp.zeros_like(carry_ref)

    lf = lf_ref[0]
    c = jnp.dot(tri_ref[...], lf, preferred_element_type=F32, precision=lax.Precision.HIGHEST) + carry_ref[...]
    carry_ref[...] = c[-1:, :]
    c = c * LOG2E
    hi = c.astype(BF16)
    r1 = c - hi.astype(F32)
    mid = r1.astype(BF16)
    lo = (r1 - mid.astype(F32)).astype(BF16)
    nh = N_HEADS
    lane = lax.broadcasted_iota(jnp.int32, c.shape, 1)
    parts = jnp.where(lane < nh, hi.astype(F32),
                      jnp.where(lane < 2 * nh, pltpu.roll(mid.astype(F32), nh, axis=1),
                                pltpu.roll(lo.astype(F32), 2 * nh, axis=1)))
    parts = jnp.where(lane < 3 * nh, parts, 0.0).astype(BF16)
    qa_ref[0] = (jnp.dot(parts, place_q_ref[...], preferred_element_type=F32) + ones_q_ref[...]).astype(BF16)
    ka_ref[0] = (jnp.dot(parts, place_k_ref[...], preferred_element_type=F32) + ones_k_ref[...]).astype(BF16)


def fox_bias_operands(logf, *, tc=256):
    B, S, _ = logf.shape
    nh = N_HEADS
    tri = jnp.asarray(np.tril(np.ones((tc, tc), np.float32)))
    pq = np.zeros((LANES, nh * LANES), np.float32)
    pk = np.zeros((LANES, nh * LANES), np.float32)
    oq = np.zeros((1, nh * LANES), np.float32)
    ok = np.zeros((1, nh * LANES), np.float32)
    for h in range(nh):
        for part in range(3):
            pq[part * nh + h, h * LANES + part] = 1.0
            pk[part * nh + h, h * LANES + 3 + part] = -1.0
            oq[0, h * LANES + 3 + part] = 1.0
            ok[0, h * LANES + part] = 1.0
    const = lambda a: pl.BlockSpec(a.shape, lambda b, i: (0,) * a.ndim)
    pq, pk, oq, ok = jnp.asarray(pq, BF16), jnp.asarray(pk, BF16), jnp.asarray(oq), jnp.asarray(ok)
    out = jax.ShapeDtypeStruct((B, S, nh * LANES), BF16)
    return pl.pallas_call(
        _cumsum_aug_kernel,
        grid=(B, S // tc),
        in_specs=[pl.BlockSpec((1, tc, LANES), lambda b, i: (b, i, 0)),
                  const(tri), const(pq), const(pk), const(oq), const(ok)],
        out_specs=[pl.BlockSpec((1, tc, nh * LANES), lambda b, i: (b, i, 0))] * 2,
        out_shape=[out, out],
        scratch_shapes=[pltpu.VMEM((1, LANES), F32)],
        compiler_params=_cparams(("parallel", "arbitrary")),
    )(logf, tri, pq, pk, oq, ok)


def _fox_attn_kernel(q_ref, qa_ref, kv_ref, ka_ref, o_ref, *, tq, tk, heads_per_step):
    qi = pl.program_id(2)
    t0 = qi * tq
    n_full = t0 // tk
    n_diag = tq // tk
    lanes = [slice(hh * LANES, (hh + 1) * LANES) for hh in range(heads_per_step)]
    qs = [jnp.concatenate([q_ref[0, :, lsl], qa_ref[0, :, lsl]], axis=1) for lsl in lanes]

    def step(j, carry, masked):
        rows = pl.ds(pl.multiple_of(j * tk, tk), tk)
        klane = lax.broadcasted_iota(jnp.int32, (tk, LANES), 1)
        new = []
        for lsl, q, (m, acc) in zip(lanes, qs, carry):
            kv = kv_ref[0, rows, lsl]
            kk = jnp.concatenate([kv, ka_ref[0, rows, lsl]], axis=1)
            ones_v = jnp.where(klane < HEAD_DIM, 1.0, kv).astype(BF16)
            s = lax.dot_general(q, kk, _NT, preferred_element_type=F32)
            if masked:
                qpos = t0 + lax.broadcasted_iota(jnp.int32, (tq, 1), 0)
                kpos = j * tk + lax.broadcasted_iota(jnp.int32, (1, tk), 1)
                s = jnp.where(kpos <= qpos, s, NEG_INF)
            m_new = jnp.maximum(m, jnp.max(s, axis=1, keepdims=True))
            p = jnp.exp2(s - m_new).astype(BF16)
            acc = jnp.exp2(m - m_new) * acc + jnp.dot(p, ones_v, preferred_element_type=F32)
            new.append((m_new, acc))
        return tuple(new)

    init = (jnp.full((tq, 1), NEG_INF, F32), jnp.zeros((tq, LANES), F32))
    n_pairs = n_full // 2
    carry = lax.fori_loop(0, n_pairs, lambda i, c: step(2 * i + 1, step(2 * i, c, False), False),
                          (init,) * heads_per_step)
    carry = lax.fori_loop(2 * n_pairs, n_full, functools.partial(step, masked=False), carry)
    for d in range(n_diag):
        carry = step(n_full + d, carry, True)
    outs = [acc / acc[:, 0:1] for (_, acc) in carry]
    lane = lax.broadcasted_iota(jnp.int32, (tq, LANES), 1)
    blocks = []
    for pair in range(heads_per_step // 2):
        a, b = outs[2 * pair], outs[2 * pair + 1]
        blocks.append(jnp.where(lane < HEAD_DIM, pltpu.roll(a, HEAD_DIM, axis=1), b))
    o_ref[0] = jnp.concatenate(blocks, axis=1).astype(o_ref.dtype) if len(blocks) > 1 else blocks[0].astype(o_ref.dtype)


def _head_slots(w, n_heads, second=None):
    D = w.shape[0]
    a = w.reshape(D, n_heads, HEAD_DIM)
    b = jnp.zeros_like(a) if second is None else second.reshape(D, n_heads, HEAD_DIM)
    return jnp.concatenate([a, b], axis=-1).reshape(D, n_heads * LANES)


def fox_layer(h, gamma, w_in, f_bias, w_out, B, S):
    aw = ATTN_WIDTH
    wq = _head_slots(w_in[:, :aw] * (HEAD_DIM ** -0.5 * LOG2E), N_HEADS)
    wkv = _head_slots(w_in[:, aw:2 * aw], N_HEADS, w_in[:, 2 * aw:3 * aw])
    w_main = jnp.concatenate([wq, wkv], axis=1).astype(BF16)
    wf = jnp.pad(w_in[:, 3 * aw:], ((0, 0), (0, LANES - N_HEADS))).astype(BF16)
    bf = jnp.pad(f_bias.astype(F32), (0, LANES - N_HEADS))
    qkv = norm_matmul(h, gamma, w_main)
    logf = norm_matmul(h, gamma, wf, bias=bf, act="log_sigmoid", out_dtype=F32)
    qa, ka = fox_bias_operands(logf.reshape(B, S, LANES))
    qkv = qkv.reshape(B, S, 2 * N_HEADS * LANES)
    o = fox_attention(qkv, qa, ka)
    return matmul_residual(o.reshape(B * S, aw), w_out.astype(BF16), h)


def fox_attention(qkv, qa, ka, *, tq=1024, tk=1024, heads_per_step=2):
    B, S, _ = qkv.shape
    hs = heads_per_step
    wq = hs * LANES
    n_qblk = N_HEADS // hs
    return pl.pallas_call(
        functools.partial(_fox_attn_kernel, tq=tq, tk=tk, heads_per_step=hs),
        grid=(B, n_qblk, S // tq),
        in_specs=[pl.BlockSpec((1, tq, wq), lambda b, h, i: (b, i, h)),
                  pl.BlockSpec((1, tq, wq), lambda b, h, i: (b, i, h)),
                  pl.BlockSpec((1, S, wq), lambda b, h, i: (b, 0, n_qblk + h)),
                  pl.BlockSpec((1, S, wq), lambda b, h, i: (b, 0, h))],
        out_specs=pl.BlockSpec((1, tq, hs * HEAD_DIM), lambda b, h, i: (b, i, h)),
        out_shape=jax.ShapeDtypeStruct((B, S, ATTN_WIDTH), BF16),
        compiler_params=_cparams(("parallel", "parallel", "arbitrary")),
    )(qkv, qa, qkv, ka)


def _rot_half_cols(w):
    D = w.shape[0]
    a = w.reshape(D, -1, HEAD_DIM)
    half = HEAD_DIM // 2
    return jnp.concatenate([-a[..., half:], a[..., :half]], axis=-1).reshape(w.shape)


def _rope_tables(S):
    half = HEAD_DIM // 2
    inv_freq = ROPE_THETA ** (-jnp.arange(half, dtype=F32) / half)
    ang = jnp.arange(S, dtype=F32)[:, None] * inv_freq[None, :]
    c, s = jnp.cos(ang), jnp.sin(ang)
    c2, s2 = jnp.concatenate([c, c], axis=1), jnp.concatenate([s, s], axis=1)
    cos = jnp.stack([jnp.concatenate([c2, jnp.ones_like(c2)], axis=1), jnp.concatenate([c2, c2], axis=1)])
    sin = jnp.stack([jnp.concatenate([s2, jnp.zeros_like(s2)], axis=1), jnp.concatenate([s2, s2], axis=1)])
    return cos, sin


def _compress_kernel(x_ref, pea_ref, peb_ref, wa_ref, wb_ref, w2_ref, o_ref, pa_ref, pb0_ref, *, n_rows):
    u = pl.program_id(1)
    x = x_ref[0].astype(F32)
    pa = jnp.dot((x + pea_ref[...]).astype(BF16), wa_ref[...], preferred_element_type=F32)
    pb = jnp.dot((x + peb_ref[...]).astype(BF16), wb_ref[...], preferred_element_type=F32)

    def emit(slab, hid):
        y = jnp.dot(_gelu_tanh(hid).astype(BF16), w2_ref[...], preferred_element_type=F32)
        o_ref[0, pl.ds(pl.multiple_of(slab * n_rows, n_rows), n_rows), :] = y

    @pl.when(u == 0)
    def _():
        pb0_ref[...] = pb

    @pl.when(u > 0)
    def _():
        emit(u - 1, pa_ref[...] + pb)

    @pl.when(u == 3)
    def _():
        emit(3, pa + pltpu.roll(pb0_ref[...], n_rows - 1, axis=0))

    pa_ref[...] = pa


def nsa_compress(src, pe, w1, w2):
    B, S, W = src.shape
    G = NSA_GROUPS
    n_rows = S // 64
    half = CMP_BLOCK // 2
    cw = half * W
    xv = src.reshape(B, n_rows, 4 * cw)
    pe_flat = jnp.transpose(pe, (1, 0, 2)).reshape(CMP_BLOCK, W).astype(F32)
    pea, peb = pe_flat[:half].reshape(1, cw), pe_flat[half:].reshape(1, cw)
    eye = jnp.eye(G, dtype=F32)
    wfull = jnp.einsum('gldh,gk->lkdgh', w1.astype(F32), eye).reshape(CMP_BLOCK, W, G * CMP_HIDDEN)
    wa = wfull[:half].reshape(cw, G * CMP_HIDDEN).astype(BF16)
    wb = wfull[half:].reshape(cw, G * CMP_HIDDEN).astype(BF16)
    w2bd = jnp.einsum('ghd,gk->ghkd', w2.astype(F32), eye).reshape(G * CMP_HIDDEN, W).astype(BF16)
    const = lambda a: pl.BlockSpec(a.shape, lambda b, u: (0,) * a.ndim)
    return pl.pallas_call(
        functools.partial(_compress_kernel, n_rows=n_rows),
        grid=(B, 4),
        in_specs=[pl.BlockSpec((1, n_rows, cw), lambda b, u: (b, 0, u)),
                  const(pea), const(peb), const(wa), const(wb), const(w2bd)],
        out_specs=pl.BlockSpec((1, 4 * n_rows, W), lambda b, u: (b, 0, 0)),
        out_shape=jax.ShapeDtypeStruct((B, 4 * n_rows, W), F32),
        scratch_shapes=[pltpu.VMEM((n_rows, G * CMP_HIDDEN), F32), pltpu.VMEM((n_rows, G * CMP_HIDDEN), F32)],
        compiler_params=_cparams(("parallel", "arbitrary")),
    )(xv, pea, peb, wa, wb, w2bd)


def _nsa_attn_kernel(q_ref, kvs_ref, kvw_ref, kvc_ref, kvct_ref, gate_ref, bmat_ref, pmat_ref, o_ref,
                     *, tq, tk, seq):
    R = NSA_Q_PER_GROUP
    M = R * tq
    n_slc = seq // SLC_BLOCK
    n_cmp = 4 * n_slc
    n_sel = min(SLC_TOPK, n_slc)
    blocks_per_chunk = tk // SLC_BLOCK
    assert tq & (tq - 1) == 0 and n_slc & (n_slc - 1) == 0 and tk % tq == 0
    log_slc = n_slc.bit_length() - 1
    qi = pl.program_id(2)
    t0 = qi * tq

    qs = jnp.concatenate([q_ref[0, :, r * LANES:(r + 1) * LANES] for r in range(R)], axis=0)

    kvc = kvc_ref[0, 0]
    sT = lax.dot_general(kvc, qs, _NT, preferred_element_type=F32)
    rowc = lax.broadcasted_iota(jnp.int32, (n_cmp, 1), 0)
    cmp_end = (rowc & (n_slc - 1)) * SLC_BLOCK + (rowc >> log_slc) * CMP_STRIDE + (CMP_BLOCK - 1)
    tcol = t0 + (lax.broadcasted_iota(jnp.int32, (1, M), 1) & (tq - 1))
    sm = jnp.where(cmp_end <= tcol, sT, NEG_INF)
    mx = jnp.max(sm, axis=0, keepdims=True)
    e = jnp.exp2(sm - mx)
    inv = jnp.where(mx > 0.5 * NEG_INF, 1.0 / jnp.sum(e, axis=0, keepdims=True), 0.0)
    pT = e * inv
    o_c = jnp.dot(kvct_ref[0, 0], pT.astype(BF16), preferred_element_type=F32).T

    wlen = WINDOW + tq
    start = jnp.maximum(t0 - WINDOW, 0)
    kvw = kvw_ref[0, pl.ds(pl.multiple_of(start, tq), wlen), :]
    s_w = lax.dot_general(qs, kvw, _NT, preferred_element_type=F32)
    qpos = t0 + (lax.broadcasted_iota(jnp.int32, (M, 1), 0) & (tq - 1))
    kpos = start + lax.broadcasted_iota(jnp.int32, (1, wlen), 1)
    s_w = jnp.where(kpos <= qpos, jnp.where(kpos > qpos - WINDOW, s_w, NEG_INF), NEG_INF)
    p_w = jnp.exp2(s_w - jnp.max(s_w, axis=1, keepdims=True)).astype(BF16)
    wlane = lax.broadcasted_iota(jnp.int32, (wlen, LANES), 1)
    acc_w = jnp.dot(p_w, jnp.where(wlane < HEAD_DIM, 1.0, kvw).astype(BF16), preferred_element_type=F32)
    o_w = acc_w / acc_w[:, 0:1]

    psum = pT[:, 0:tq]
    for r in range(1, R):
        psum = psum + pT[:, r * tq:(r + 1) * tq]
    p0, p1, p2, p3 = (psum[u * n_slc:(u + 1) * n_slc, :] for u in range(4))
    jrow = lax.broadcasted_iota(jnp.int32, (n_slc, tq), 0)
    p3_prev = jnp.where(jrow == 0, 0.0, pltpu.roll(p3, 1, axis=0))
    imp = p0 + p1 + p2 + 0.5 * p3 + 0.5 * p3_prev
    cur = (t0 + lax.broadcasted_iota(jnp.int32, (n_slc, tq), 1)) >> (SLC_BLOCK.bit_length() - 1)
    forced = (jrow == 0) | (jrow == cur) | (jrow == cur - 1)
    vals = jnp.where(forced, -jnp.inf, jnp.where(jrow <= cur, imp, -jnp.inf))
    sel0 = jnp.where(forced, 1.0, 0.0)
    n_free = n_sel - 3

    def pick(exact_ties):
        v, sel = vals, sel0
        for _ in range(n_free):
            m = jnp.max(v, axis=0, keepdims=True)
            if exact_ties:
                hit = jrow == jnp.min(jnp.where(v == m, jrow, n_slc), axis=0, keepdims=True)
            else:
                hit = v == jnp.where(m == -jnp.inf, jnp.nan, m)
            sel = jnp.where(hit, 1.0, sel)
            v = jnp.where(hit, -jnp.inf, v)
        return sel

    sel_fast = pick(False)
    n_cand = jnp.sum(jnp.where(vals > -jnp.inf, 1.0, 0.0), axis=0, keepdims=True)
    n_picked = jnp.sum(sel_fast - sel0, axis=0, keepdims=True)
    unique = jnp.min(jnp.where(n_picked == jnp.minimum(n_cand, float(n_free)), 1.0, 0.0)) > 0.5
    sel = lax.cond(unique, lambda: sel_fast, lambda: pick(True))
    sel_bias = ((sel.T - 1.0) * (-NEG_INF)).astype(BF16)

    n_parts = 2
    hp = R // n_parts
    mp = hp * tq
    qparts = [qs[i * mp:(i + 1) * mp] for i in range(n_parts)]
    qrow = t0 + (lax.broadcasted_iota(jnp.int32, (mp, 1), 0) & (tq - 1))
    kcol = lax.broadcasted_iota(jnp.int32, (1, tk), 1)
    n_chunks = seq // tk
    j_last = t0 // tk

    klane = lax.broadcasted_iota(jnp.int32, (tk, LANES), 1)

    def slc_step(j, carry, masked):
        kv = kvs_ref[0, pl.ds(pl.multiple_of(j * tk, tk), tk), :]
        k_sel = jnp.where(klane < HEAD_DIM, kv, bmat_ref[...]).astype(BF16)
        ones_v = jnp.where(klane < HEAD_DIM, 1.0, kv).astype(BF16)
        off = pl.multiple_of(blocks_per_chunk * (n_chunks - 1 - j), blocks_per_chunk)
        place = pmat_ref[pl.ds(off, n_slc), :].astype(BF16)
        q_bias = jnp.dot(sel_bias, place, preferred_element_type=F32).astype(BF16)
        q_bias = jnp.concatenate([q_bias] * hp, axis=0)
        new = []
        for qp, (m, acc) in zip(qparts, carry):
            s = lax.dot_general(qp + q_bias, k_sel, _NT, preferred_element_type=F32)
            if masked:
                s = jnp.where(j * tk + kcol <= qrow, s, NEG_INF)
            m_new = jnp.maximum(m, jnp.max(s, axis=1, keepdims=True))
            p = jnp.exp2(s - m_new).astype(BF16)
            acc = jnp.exp2(m - m_new) * acc + jnp.dot(p, ones_v, preferred_element_type=F32)
            new.append((m_new, acc))
        return tuple(new)

    init = (jnp.full((mp, 1), NEG_INF, F32), jnp.zeros((mp, LANES), F32))
    n_pairs = j_last // 2
    carry = lax.fori_loop(0, n_pairs, lambda i, c: slc_step(2 * i + 1, slc_step(2 * i, c, False), False),
                          (init,) * n_parts)
    carry = lax.fori_loop(2 * n_pairs, j_last, functools.partial(slc_step, masked=False), carry)
    carry = slc_step(j_last, carry, True)
    o_s = jnp.concatenate([acc / acc[:, 0:1] for (_, acc) in carry], axis=0)

    gates = gate_ref[0]
    lane = lax.broadcasted_iota(jnp.int32, (tq, LANES), 1)
    comb = []
    for r in range(R):
        rs = slice(r * tq, (r + 1) * tq)
        comb.append(gates[:, r:r + 1] * o_c[rs] + gates[:, R + r:R + r + 1] * o_s[rs]
                    + gates[:, 2 * R + r:2 * R + r + 1] * o_w[rs])
    out = [jnp.where(lane < HEAD_DIM, pltpu.roll(comb[2 * i], HEAD_DIM, axis=1), comb[2 * i + 1])
           for i in range(R // 2)]
    o_ref[0] = jnp.concatenate(out, axis=1).astype(o_ref.dtype)


def nsa_attention(qkv, kvc, kvct, gates, *, tq=256, tk=1024):
    B, S, _ = qkv.shape
    G, R = NSA_GROUPS, NSA_Q_PER_GROUP
    tk = min(tk, S)
    n_slc = S // SLC_BLOCK
    n_cmp = kvc.shape[2]
    bpc = tk // SLC_BLOCK
    assert bpc <= LANES - HEAD_DIM
    off = bpc * (S // tk - 1)
    lane = np.arange(LANES)[None, :]
    bmat = jnp.asarray(lane - HEAD_DIM == np.arange(tk)[:, None] // SLC_BLOCK, BF16)
    pmat = jnp.asarray((np.arange(n_slc + off)[:, None] - off == lane - HEAD_DIM) & (lane >= HEAD_DIM)
                       & (lane < HEAD_DIM + bpc), F32)
    slc_blk0, win_blk0 = N_HEADS, N_HEADS + G
    return pl.pallas_call(
        functools.partial(_nsa_attn_kernel, tq=tq, tk=tk, seq=S),
        grid=(B, G, S // tq),
        in_specs=[pl.BlockSpec((1, tq, R * LANES), lambda b, g, i: (b, i, g)),
                  pl.BlockSpec((1, S, LANES), lambda b, g, i: (b, 0, slc_blk0 + g)),
                  pl.BlockSpec((1, S, LANES), lambda b, g, i: (b, 0, win_blk0 + g)),
                  pl.BlockSpec((1, 1, n_cmp, LANES), lambda b, g, i: (b, g, 0, 0)),
                  pl.BlockSpec((1, 1, LANES, n_cmp), lambda b, g, i: (b, g, 0, 0)),
                  pl.BlockSpec((1, tq, LANES), lambda b, g, i: (b, i, g)),
                  pl.BlockSpec(bmat.shape, lambda b, g, i: (0, 0)),
                  pl.BlockSpec(pmat.shape, lambda b, g, i: (0, 0))],
        out_specs=pl.BlockSpec((1, tq, R * HEAD_DIM), lambda b, g, i: (b, i, g)),
        out_shape=jax.ShapeDtypeStruct((B, S, ATTN_WIDTH), BF16),
        compiler_params=_cparams(("parallel", "parallel", "arbitrary")),
    )(qkv, qkv, qkv, kvc, kvct, gates, bmat, pmat)


def nsa_layer(h, gamma, w_in, pe_k, w1_k, w2_k, pe_v, w1_v, w2_v, w_out, B, S):
    G, R, hd, aw = NSA_GROUPS, NSA_Q_PER_GROUP, HEAD_DIM, ATTN_WIDTH
    kvd = G * hd
    sec = lambda i: w_in[:, aw + i * kvd: aw + (i + 1) * kvd]
    wq = w_in[:, :aw] * (hd ** -0.5 * LOG2E)
    wa = jnp.concatenate([_head_slots(wq, N_HEADS), _head_slots(sec(2), G, sec(3)),
                          _head_slots(sec(4), G, sec(5))], axis=1).astype(BF16)
    wb = jnp.concatenate([_head_slots(_rot_half_cols(wq), N_HEADS), _head_slots(_rot_half_cols(sec(2)), G),
                          _head_slots(_rot_half_cols(sec(4)), G)], axis=1).astype(BF16)
    cos, sin = _rope_tables(S)
    qkv = norm_matmul(h, gamma, wa, wb=wb, cos=cos, sin=sin)
    kc_src = norm_matmul(h, gamma, sec(0).astype(BF16), wb=_rot_half_cols(sec(0)).astype(BF16),
                         cos=cos, sin=sin, table_of_tile=lambda j: 1)
    vc_src = norm_matmul(h, gamma, sec(1).astype(BF16))
    wg = w_in[:, aw + 6 * kvd:].reshape(-1, 3, G, R)
    wg = jnp.transpose(wg, (0, 2, 1, 3)).reshape(-1, G, 3 * R)
    wg = jnp.pad(wg, ((0, 0), (0, 0), (0, LANES - 3 * R))).reshape(-1, G * LANES).astype(BF16)
    gates = norm_matmul(h, gamma, wg, act="sigmoid", out_dtype=F32)
    kc = nsa_compress(kc_src.reshape(B, S, kvd), pe_k, w1_k, w2_k)
    vc = nsa_compress(vc_src.reshape(B, S, kvd), pe_v, w1_v, w2_v)
    n_cmp = kc.shape[1]
    kvc = jnp.concatenate([kc.reshape(B, n_cmp, G, hd), vc.reshape(B, n_cmp, G, hd)], axis=-1)
    kvc = jnp.transpose(kvc, (0, 2, 1, 3)).astype(BF16)
    kvct = jnp.swapaxes(kvc, 2, 3)
    o = nsa_attention(qkv.reshape(B, S, -1), kvc, kvct, gates.reshape(B, S, G * LANES))
    return matmul_residual(o.reshape(B * S, aw), w_out.astype(BF16), h)


def kernel(x, l0_attn_norm, l0_w_in, l0_cmp_pe_k, l0_cmp_w1_k, l0_cmp_w2_k, l0_cmp_pe_v, l0_cmp_w1_v,
           l0_cmp_w2_v, l0_w_out, l0_ffn_norm, l0_peer_wq, l0_peer_keys, l0_peer_u, l0_peer_v,
           l1_attn_norm, l1_w_in, l1_f_bias, l1_w_out, l1_ffn_norm, l1_peer_wq, l1_peer_keys, l1_peer_u,
           l1_peer_v, final_norm):
    B, S, D = x.shape
    h = x.reshape(B * S, D)
    h = nsa_layer(h, l0_attn_norm, l0_w_in, l0_cmp_pe_k, l0_cmp_w1_k, l0_cmp_w2_k, l0_cmp_pe_v, l0_cmp_w1_v,
                  l0_cmp_w2_v, l0_w_out, B, S)
    h = peer_layer(h, l0_ffn_norm, l0_peer_wq, l0_peer_keys, l0_peer_u, l0_peer_v)
    h = fox_layer(h, l1_attn_norm, l1_w_in, l1_f_bias, l1_w_out, B, S)
    h = peer_layer(h, l1_ffn_norm, l1_peer_wq, l1_peer_keys, l1_peer_u, l1_peer_v)
    return rmsnorm(h, final_norm).reshape(B, S, D)
```

```python
import functools

import numpy as np
import jax
import jax.numpy as jnp
from jax import lax
from jax.experimental import pallas as pl
from jax.experimental.pallas import tpu as pltpu

F32 = jnp.float32
BF16 = jnp.bfloat16

D_MODEL = 1024
N_HEADS = 16
HEAD_DIM = 64
ATTN_WIDTH = N_HEADS * HEAD_DIM
NSA_GROUPS = 4
NSA_Q_PER_GROUP = N_HEADS // NSA_GROUPS
CMP_BLOCK = 32
CMP_STRIDE = 16
CMP_HIDDEN = 2 * HEAD_DIM
SLC_BLOCK = 64
SLC_TOPK = 16
WINDOW = 512
FORCE_SCORE = 1.0e4
ROPE_THETA = 10000.0
PEER_HEADS = 8
PEER_N_KEYS = 128
PEER_TOPK = 16
PEER_HALF_DIM = 128
RMS_EPS = 1e-6
NEG_INF = -1e30
LOG2E = 1.4426950408889634

LANES = 128
VMEM_LIMIT_BYTES = 56 * 1024 * 1024

_NT = (((1,), (1,)), ((), ()))


def _cparams(sem, vmem=VMEM_LIMIT_BYTES, flags=None):
    return pltpu.CompilerParams(dimension_semantics=sem, vmem_limit_bytes=vmem, flags=flags)


def _gelu_tanh(x):
    return 0.5 * x * (1.0 + jnp.tanh(0.7978845608028654 * (x + 0.044715 * (x * x * x))))


def _gelu_sigmoid(x):
    c = -2.0 * 0.7978845608028654 * LOG2E
    t = x * (c + (c * 0.044715) * (x * x))
    return x / (1.0 + jnp.exp2(t))


def _rms_rows(x, g):
    ms = jnp.mean(x * x, axis=-1, keepdims=True)
    return x * lax.rsqrt(ms + RMS_EPS) * g


def _norm_mm_kernel(*refs, act, has_bias, rope, emit_xn):
    it = iter(refs)
    x_ref, g_ref = next(it), next(it)
    wa_ref = next(it)
    wb_ref = next(it) if rope else None
    cos_ref = next(it) if rope else None
    sin_ref = next(it) if rope else None
    b_ref = next(it) if has_bias else None
    o_ref = next(it)
    xo_ref = next(it) if emit_xn else None
    xn_ref = next(it)

    @pl.when(pl.program_id(1) == 0)
    def _():
        xn = _rms_rows(x_ref[...], g_ref[...])
        xn_ref[...] = xn.astype(BF16)
        if emit_xn:
            xo_ref[...] = xn.T.astype(BF16)

    xn = xn_ref[...]
    y = jnp.dot(xn, wa_ref[...], preferred_element_type=F32)
    if rope:
        yb = jnp.dot(xn, wb_ref[...], preferred_element_type=F32)
        cos, sin = cos_ref[...], sin_ref[...]
        for s in range(y.shape[1] // LANES):
            sl = slice(s * LANES, (s + 1) * LANES)
            o_ref[:, sl] = (y[:, sl] * cos + yb[:, sl] * sin).astype(o_ref.dtype)
        return
    if has_bias:
        y = y + b_ref[...]
    if act == "sigmoid":
        y = jax.nn.sigmoid(y)
    elif act == "log_sigmoid":
        y = jax.nn.log_sigmoid(y)
    o_ref[...] = y.astype(o_ref.dtype)


def norm_matmul(x, gamma, wa, *, wb=None, cos=None, sin=None, table_of_tile=None, bias=None,
                act=None, out_dtype=BF16, emit_xn=False, tm=1024, tn=1024):
    T, D = x.shape
    N = wa.shape[1]
    tm, tn = min(tm, T), min(tn, N)
    assert T % tm == 0 and N % tn == 0 and tn % LANES == 0
    rope = wb is not None
    in_specs = [pl.BlockSpec((tm, D), lambda i, j: (i, 0)),
                pl.BlockSpec((1, D), lambda i, j: (0, 0)),
                pl.BlockSpec((D, tn), lambda i, j: (0, j))]
    args = [x, gamma.reshape(1, D).astype(F32), wa]
    if rope:
        S = cos.shape[1]
        assert S % tm == 0
        n_pos = S // tm
        tmap = table_of_tile if table_of_tile is not None else (lambda j: 0)
        in_specs += [pl.BlockSpec((D, tn), lambda i, j: (0, j)),
                     pl.BlockSpec((None, tm, LANES), lambda i, j: (tmap(j), i % n_pos, 0)),
                     pl.BlockSpec((None, tm, LANES), lambda i, j: (tmap(j), i % n_pos, 0))]
        args += [wb, cos, sin]
    if bias is not None:
        in_specs.append(pl.BlockSpec((1, tn), lambda i, j: (0, j)))
        args.append(bias.reshape(1, N).astype(F32))
    out_shape = [jax.ShapeDtypeStruct((T, N), out_dtype)]
    out_specs = [pl.BlockSpec((tm, tn), lambda i, j: (i, j))]
    if emit_xn:
        out_shape.append(jax.ShapeDtypeStruct((D, T), BF16))
        out_specs.append(pl.BlockSpec((D, tm), lambda i, j: (0, i)))
    res = pl.pallas_call(
        functools.partial(_norm_mm_kernel, act=act, has_bias=bias is not None, rope=rope, emit_xn=emit_xn),
        grid=(T // tm, N // tn),
        in_specs=in_specs,
        out_specs=out_specs,
        out_shape=out_shape,
        scratch_shapes=[pltpu.VMEM((tm, D), BF16)],
        compiler_params=_cparams(("parallel", "arbitrary")),
    )(*args)
    return res if emit_xn else res[0]


def _mm_res_kernel(a_ref, w_ref, r_ref, o_ref):
    o_ref[...] = r_ref[...] + jnp.dot(a_ref[...], w_ref[...], preferred_element_type=F32)


def matmul_residual(a, w, res, *, tm=1024, tn=1024):
    T, K = a.shape
    N = w.shape[1]
    tm, tn = min(tm, T), min(tn, N)
    assert T % tm == 0 and N % tn == 0
    return pl.pallas_call(
        _mm_res_kernel,
        grid=(T // tm, N // tn),
        in_specs=[pl.BlockSpec((tm, K), lambda i, j: (i, 0)),
                  pl.BlockSpec((K, tn), lambda i, j: (0, j)),
                  pl.BlockSpec((tm, tn), lambda i, j: (i, j))],
        out_specs=pl.BlockSpec((tm, tn), lambda i, j: (i, j)),
        out_shape=jax.ShapeDtypeStruct((T, N), F32),
        compiler_params=_cparams(("parallel", "arbitrary")),
    )(a, w, res)


def _rmsnorm_kernel(x_ref, g_ref, o_ref):
    o_ref[...] = _rms_rows(x_ref[...], g_ref[...])


def rmsnorm(x, gamma, *, tm=1024):
    T, D = x.shape
    tm = min(tm, T)
    return pl.pallas_call(
        _rmsnorm_kernel,
        grid=(T // tm,),
        in_specs=[pl.BlockSpec((tm, D), lambda i: (i, 0)), pl.BlockSpec((1, D), lambda i: (0, 0))],
        out_specs=pl.BlockSpec((tm, D), lambda i: (i, 0)),
        out_shape=jax.ShapeDtypeStruct((T, D), F32),
        compiler_params=_cparams(("parallel",)),
    )(x, gamma.reshape(1, D).astype(F32))


def _peer_cand_tables(tn):
    fidx, vmask = [], []
    for k2 in range(16):
        fidx.append(k2); vmask.append(0.0)
    for k1 in range(1, 8):
        lim = PEER_TOPK // (k1 + 1)
        for k2 in range(8):
            fidx.append(k1 * 16 + k2); vmask.append(0.0 if k2 < lim else -np.inf)
    for k1 in range(8, 16):
        fidx.append(k1 * 16); vmask.append(0.0)
    fidx = np.broadcast_to(np.asarray(fidx, np.int32)[:, None], (80, tn))
    vmask = np.broadcast_to(np.asarray(vmask, np.float32)[:, None], (80, tn))
    return jnp.asarray(fidx), jnp.asarray(vmask)


def _top16_rows(s, exact_ties):
    n, tn = s.shape
    rows = lax.broadcasted_iota(jnp.int32, (n, tn), 0)
    rows16 = lax.broadcasted_iota(jnp.int32, (PEER_TOPK, tn), 0)
    rank = jnp.full((n, tn), float(PEER_TOPK), F32)
    tops = jnp.zeros((PEER_TOPK, tn), F32)
    v = s
    for k in range(PEER_TOPK):
        m = jnp.max(v, axis=0, keepdims=True)
        if exact_ties:
            hit = rows == jnp.min(jnp.where(v == m, rows, n), axis=0, keepdims=True)
        else:
            hit = v == m
        rank = jnp.where(hit, float(k), rank)
        v = jnp.where(hit, -jnp.inf, v)
        tops = jnp.where(rows16 == k, m, tops)
    return tops, rank, v


def _peer_select_head(q_ref, keys_ref, fidx, vmask, exact_ties):
    tops, ranks, es, picked = [], [], [], []
    for p in range(2):
        q = q_ref[:, p * PEER_HALF_DIM:(p + 1) * PEER_HALF_DIM]
        s = lax.dot_general(keys_ref[p], q, _NT, preferred_element_type=F32)
        t, r, v = _top16_rows(s, exact_ties)
        tops.append(t); ranks.append(r)
        es.append(jnp.exp(s - t[0:1, :]))
        picked.append(jnp.sum(jnp.where(v == -jnp.inf, 1.0, 0.0), axis=0, keepdims=True))
    ts1, ts2 = tops
    pieces = [ts1[0:1, :] + ts2]
    for k1 in range(1, 8):
        pieces.append(ts1[k1:k1 + 1, :] + ts2[0:8, :])
    pieces.append(ts1[8:16, :] + ts2[0:1, :])
    cand0 = jnp.concatenate(pieces, axis=0) + vmask
    cand = cand0
    for _ in range(PEER_TOPK):
        m = jnp.max(cand, axis=0, keepdims=True)
        if exact_ties:
            hit = fidx == jnp.min(jnp.where(cand == m, fidx, 4096), axis=0, keepdims=True)
        else:
            hit = cand == m
        cand = jnp.where(hit, -jnp.inf, cand)
    taken = jnp.logical_and(cand == -jnp.inf, vmask == 0.0)
    takenf = taken.astype(F32)
    picked.append(jnp.sum(takenf, axis=0, keepdims=True))
    unique = jnp.min(jnp.where((picked[0] == PEER_TOPK) & (picked[1] == PEER_TOPK) & (picked[2] == PEER_TOPK),
                               1.0, 0.0)) > 0.5
    best = ts1[0:1, :] + ts2[0:1, :]
    z = jnp.sum(jnp.where(taken, jnp.exp(cand0 - best), 0.0), axis=0, keepdims=True)
    counts = [jnp.sum(takenf[0:16, :], axis=0, keepdims=True)]
    for k1 in range(1, 8):
        counts.append(jnp.sum(takenf[16 + 8 * (k1 - 1):16 + 8 * k1, :], axis=0, keepdims=True))
    tail = takenf[72:80, :]
    cnt = jnp.zeros_like(ranks[0])
    for k1 in range(PEER_TOPK):
        nk = counts[k1] if k1 < 8 else tail[k1 - 8:k1 - 7, :]
        cnt = jnp.where(ranks[0] == float(k1), nk, cnt)
    return (cnt, ranks[1], es[0], es[1] / z), unique


def _peer_select_kernel(q_ref, keys_ref, fidx_ref, vmask_ref, cnt_ref, rank2_ref, e1_ref, e2_ref):
    fidx = fidx_ref[...]
    vmask = vmask_ref[...]

    def store(vals):
        for ref, val in zip((cnt_ref, rank2_ref, e1_ref, e2_ref), vals):
            ref[...] = val.astype(ref.dtype)

    vals, unique = _peer_select_head(q_ref, keys_ref, fidx, vmask, exact_ties=False)
    store(vals)

    @pl.when(jnp.logical_not(unique))
    def _():
        store(_peer_select_head(q_ref, keys_ref, fidx, vmask, exact_ties=True)[0])


def peer_select(q, keys, *, tn=512):
    T = q.shape[0]
    tn = min(tn, T)
    fidx, vmask = _peer_cand_tables(tn)
    rows = PEER_HEADS * PEER_N_KEYS
    ospec = pl.BlockSpec((PEER_N_KEYS, tn), lambda i, h: (h, i))
    return pl.pallas_call(
        _peer_select_kernel,
        grid=(T // tn, PEER_HEADS),
        in_specs=[pl.BlockSpec((tn, 2 * PEER_HALF_DIM), lambda i, h: (i, h)),
                  pl.BlockSpec((2, PEER_N_KEYS, PEER_HALF_DIM), lambda i, h: (h, 0, 0)),
                  pl.BlockSpec((80, tn), lambda i, h: (0, 0)),
                  pl.BlockSpec((80, tn), lambda i, h: (0, 0))],
        out_specs=[ospec] * 4,
        out_shape=[jax.ShapeDtypeStruct((rows, T), dt) for dt in (F32, BF16, F32, BF16)],
        compiler_params=_cparams(("parallel", "parallel")),
    )(q, keys, fidx, vmask)


def _peer_dense_kernel(xn_ref, u_ref, vt_ref, cnt_ref, rank2_ref, e1_ref, e2_ref, res_ref, o_ref,
                       acc_ref, g0_ref, g1_ref, *, c_per_step):
    j = pl.program_id(1)
    n_tiles = pl.num_programs(1) - 1

    @pl.when(j == 0)
    def _():
        acc_ref[...] = jnp.zeros_like(acc_ref)
        g1_ref[...] = jnp.zeros_like(g1_ref)

    @pl.when(j % 2 == 0)
    def _():
        _peer_dense_step(xn_ref, u_ref, vt_ref, cnt_ref, rank2_ref, e1_ref, e2_ref, acc_ref,
                         g1_ref, g0_ref, j, c_per_step)

    @pl.when(j % 2 == 1)
    def _():
        _peer_dense_step(xn_ref, u_ref, vt_ref, cnt_ref, rank2_ref, e1_ref, e2_ref, acc_ref,
                         g0_ref, g1_ref, j, c_per_step)

    @pl.when(j == n_tiles)
    def _():
        o_ref[...] = res_ref[...] + acc_ref[...].T


def _peer_dense_step(xn_ref, u_ref, vt_ref, cnt_ref, rank2_ref, e1_ref, e2_ref, acc_ref, g_ref, g_next_ref,
                     j, c_per_step):

    tn = xn_ref.shape[1]
    bf16_rows = 16
    reps = PEER_N_KEYS // bf16_rows

    def row_tile(ref, row):
        r16 = jnp.broadcast_to(ref[pl.ds(row, 1), :], (bf16_rows, tn)).astype(BF16)
        return jnp.concatenate([r16] * reps, axis=0)

    c0 = jnp.maximum(j - 1, 0) * c_per_step
    up_rows = 2 * PEER_N_KEYS
    blocks = []
    for cc in range(c_per_step):
        if (cc * PEER_N_KEYS) % up_rows == 0:
            rs = slice(cc * PEER_N_KEYS, cc * PEER_N_KEYS + up_rows)
            hT = jnp.dot(u_ref[rs, :], xn_ref[...], preferred_element_type=F32)
            g_next_ref[rs, :] = _gelu_sigmoid(hT.astype(BF16))
        c = c0 + cc
        w = None
        for h in range(PEER_HEADS):
            row = h * PEER_N_KEYS + c
            n_row = row_tile(cnt_ref, row)
            e1_row = row_tile(e1_ref, row)
            sl = slice(h * PEER_N_KEYS, (h + 1) * PEER_N_KEYS)
            term = jnp.where(rank2_ref[sl, :] < n_row, e2_ref[sl, :], 0.0) * e1_row
            w = term if w is None else w + term
        blocks.append(w * g_ref[cc * PEER_N_KEYS:(cc + 1) * PEER_N_KEYS, :])
    aT = jnp.concatenate(blocks, axis=0)
    acc_ref[...] += jnp.dot(vt_ref[...], aT, preferred_element_type=F32)


def peer_dense(xn, u, vt, cnt, rank2, e1, e2, res, *, tn=512, te=1024):
    D, T = xn.shape
    E = u.shape[0]
    tn = min(tn, T)
    rows = PEER_HEADS * PEER_N_KEYS
    sel_spec = pl.BlockSpec((rows, tn), lambda i, j: (0, i))
    n_tiles = E // te
    return pl.pallas_call(
        functools.partial(_peer_dense_kernel, c_per_step=te // PEER_N_KEYS),
        grid=(T // tn, n_tiles + 1),
        in_specs=[pl.BlockSpec((D, tn), lambda i, j: (0, i)),
                  pl.BlockSpec((te, D), lambda i, j: (jnp.minimum(j, n_tiles - 1), 0)),
                  pl.BlockSpec((D, te), lambda i, j: (0, jnp.maximum(j - 1, 0))),
                  sel_spec, sel_spec, sel_spec, sel_spec,
                  pl.BlockSpec((tn, D), lambda i, j: (i, 0))],
        out_specs=pl.BlockSpec((tn, D), lambda i, j: (i, 0)),
        out_shape=jax.ShapeDtypeStruct((T, D), F32),
        scratch_shapes=[pltpu.VMEM((D, tn), F32), pltpu.VMEM((te, tn), BF16), pltpu.VMEM((te, tn), BF16)],
        compiler_params=_cparams(("parallel", "arbitrary")),
    )(xn, u, vt, cnt, rank2, e1, e2, res)


def peer_layer(h, gamma, w_q, sub_keys, u, v):
    q, xn = norm_matmul(h, gamma, w_q.astype(BF16), emit_xn=True)
    keys = sub_keys.reshape(2 * PEER_HEADS, PEER_N_KEYS, PEER_HALF_DIM).astype(BF16)
    cnt, rank2, e1, e2 = peer_select(q, keys)
    return peer_dense(xn, u.astype(BF16), v.T.astype(BF16), cnt, rank2, e1, e2, h)


def _cumsum_aug_kernel(lf_ref, tri_ref, place_q_ref, place_k_ref, ones_q_ref, ones_k_ref,
                       qa_ref, ka_ref, c_ref, carry_ref):
    @pl.when(pl.program_id(1) == 0)
    def _():
        carry_ref[...] = jnp.zeros_like(carry_ref)

    lf = lf_ref[0]
    c = jnp.dot(tri_ref[...], lf, preferred_element_type=F32, precision=lax.Precision.HIGHEST) + carry_ref[...]
    carry_ref[...] = c[-1:, :]
    c = c * LOG2E
    c_ref[0] = c
    hi = c.astype(BF16)
    r1 = c - hi.astype(F32)
    mid = r1.astype(BF16)
    lo = (r1 - mid.astype(F32)).astype(BF16)
    nh = N_HEADS
    lane = lax.broadcasted_iota(jnp.int32, c.shape, 1)
    parts = jnp.where(lane < nh, hi.astype(F32),
                      jnp.where(lane < 2 * nh, pltpu.roll(mid.astype(F32), nh, axis=1),
                                pltpu.roll(lo.astype(F32), 2 * nh, axis=1)))
    parts = jnp.where(lane < 3 * nh, parts, 0.0).astype(BF16)
    qa_ref[0] = (jnp.dot(parts, place_q_ref[...], preferred_element_type=F32) + ones_q_ref[...]).astype(BF16)
    ka_ref[0] = (jnp.dot(parts, place_k_ref[...], preferred_element_type=F32) + ones_k_ref[...]).astype(BF16)


def fox_bias_operands(logf, *, tc=256):
    B, S, _ = logf.shape
    nh = N_HEADS
    tri = jnp.asarray(np.tril(np.ones((tc, tc), np.float32)))
    pq = np.zeros((LANES, nh * LANES), np.float32)
    pk = np.zeros((LANES, nh * LANES), np.float32)
    oq = np.zeros((1, nh * LANES), np.float32)
    ok = np.zeros((1, nh * LANES), np.float32)
    for h in range(nh):
        for part in range(3):
            pq[part * nh + h, h * LANES + part] = 1.0
            pk[part * nh + h, h * LANES + 3 + part] = -1.0
            oq[0, h * LANES + 3 + part] = 1.0
            ok[0, h * LANES + part] = 1.0
    const = lambda a: pl.BlockSpec(a.shape, lambda b, i: (0,) * a.ndim)
    pq, pk, oq, ok = jnp.asarray(pq, BF16), jnp.asarray(pk, BF16), jnp.asarray(oq), jnp.asarray(ok)
    out = jax.ShapeDtypeStruct((B, S, nh * LANES), BF16)
    return pl.pallas_call(
        _cumsum_aug_kernel,
        grid=(B, S // tc),
        in_specs=[pl.BlockSpec((1, tc, LANES), lambda b, i: (b, i, 0)),
                  const(tri), const(pq), const(pk), const(oq), const(ok)],
        out_specs=[pl.BlockSpec((1, tc, nh * LANES), lambda b, i: (b, i, 0))] * 2
        + [pl.BlockSpec((1, tc, LANES), lambda b, i: (b, i, 0))],
        out_shape=[out, out, jax.ShapeDtypeStruct((B, S, LANES), F32)],
        scratch_shapes=[pltpu.VMEM((1, LANES), F32)],
        compiler_params=_cparams(("parallel", "arbitrary")),
    )(logf, tri, pq, pk, oq, ok)


def _fox_attn_kernel(first_ref, q_ref, qa_ref, kv_ref, ka_ref, o_ref, *, tq, tk, heads_per_step):
    qi = pl.program_id(2)
    t0 = qi * tq
    n_full = t0 // tk
    j_first = first_ref[(pl.program_id(0) * pl.num_programs(1) + pl.program_id(1)) * pl.num_programs(2) + qi]
    n_diag = tq // tk
    lanes = [slice(hh * LANES, (hh + 1) * LANES) for hh in range(heads_per_step)]
    qs = [jnp.concatenate([q_ref[0, :, lsl], qa_ref[0, :, lsl]], axis=1) for lsl in lanes]

    def step(j, carry, masked):
        rows = pl.ds(pl.multiple_of(j * tk, tk), tk)
        klane = lax.broadcasted_iota(jnp.int32, (tk, LANES), 1)
        new = []
        for lsl, q, (m, acc) in zip(lanes, qs, carry):
            kv = kv_ref[0, rows, lsl]
            kk = jnp.concatenate([kv, ka_ref[0, rows, lsl]], axis=1)
            ones_v = jnp.where(klane < HEAD_DIM, 1.0, kv).astype(BF16)
            s = lax.dot_general(q, kk, _NT, preferred_element_type=F32)
            if masked:
                qpos = t0 + lax.broadcasted_iota(jnp.int32, (tq, 1), 0)
                kpos = j * tk + lax.broadcasted_iota(jnp.int32, (1, tk), 1)
                s = jnp.where(kpos <= qpos, s, NEG_INF)
            m_new = jnp.maximum(m, jnp.max(s, axis=1, keepdims=True))
            p = jnp.exp2(s - m_new).astype(BF16)
            acc = jnp.exp2(m - m_new) * acc + jnp.dot(p, ones_v, preferred_element_type=F32)
            new.append((m_new, acc))
        return tuple(new)

    init = (jnp.full((tq, 1), NEG_INF, F32), jnp.zeros((tq, LANES), F32))
    n_pairs = (n_full - j_first) // 2
    carry = lax.fori_loop(
        0, n_pairs, lambda i, c: step(j_first + 2 * i + 1, step(j_first + 2 * i, c, False), False),
        (init,) * heads_per_step)
    carry = lax.fori_loop(j_first + 2 * n_pairs, n_full, functools.partial(step, masked=False), carry)
    for d in range(n_diag):
        carry = step(n_full + d, carry, True)
    outs = [acc / acc[:, 0:1] for (_, acc) in carry]
    lane = lax.broadcasted_iota(jnp.int32, (tq, LANES), 1)
    blocks = []
    for pair in range(heads_per_step // 2):
        a, b = outs[2 * pair], outs[2 * pair + 1]
        blocks.append(jnp.where(lane < HEAD_DIM, pltpu.roll(a, HEAD_DIM, axis=1), b))
    o_ref[0] = jnp.concatenate(blocks, axis=1).astype(o_ref.dtype) if len(blocks) > 1 else blocks[0].astype(o_ref.dtype)


def _head_slots(w, n_heads, second=None):
    D = w.shape[0]
    a = w.reshape(D, n_heads, HEAD_DIM)
    b = jnp.zeros_like(a) if second is None else second.reshape(D, n_heads, HEAD_DIM)
    return jnp.concatenate([a, b], axis=-1).reshape(D, n_heads * LANES)


def fox_layer(h, gamma, w_in, f_bias, w_out, B, S):
    aw = ATTN_WIDTH
    wq = _head_slots(w_in[:, :aw] * (HEAD_DIM ** -0.5 * LOG2E), N_HEADS)
    wkv = _head_slots(w_in[:, aw:2 * aw], N_HEADS, w_in[:, 2 * aw:3 * aw])
    w_main = jnp.concatenate([wq, wkv], axis=1).astype(BF16)
    wf = jnp.pad(w_in[:, 3 * aw:], ((0, 0), (0, LANES - N_HEADS))).astype(BF16)
    bf = jnp.pad(f_bias.astype(F32), (0, LANES - N_HEADS))
    qkv = norm_matmul(h, gamma, w_main)
    logf = norm_matmul(h, gamma, wf, bias=bf, act="log_sigmoid", out_dtype=F32)
    qa, ka, c2 = fox_bias_operands(logf.reshape(B, S, LANES))
    qkv = qkv.reshape(B, S, 2 * N_HEADS * LANES)
    o = fox_attention(qkv, qa, ka, c2)
    return matmul_residual(o.reshape(B * S, aw), w_out.astype(BF16), h)


FOX_NEGLIGIBLE_LOG2 = 200.0


def fox_first_chunk(qkv, c2, tile, heads_per_step):
    B, S, _ = qkv.shape
    n = S // tile
    x = qkv.astype(F32).reshape(B, n, tile, 2, N_HEADS, LANES)
    qn = jnp.sqrt(jnp.sum(jnp.square(x[:, :, :, 0]), axis=-1))
    kn = jnp.sqrt(jnp.sum(jnp.square(x[:, :, :, 1, :, :HEAD_DIM]), axis=-1))
    qmax, kmax = jnp.max(qn, axis=2), jnp.max(kn, axis=2)
    c = c2[..., :N_HEADS].reshape(B, n, tile, N_HEADS)
    cmax, cmin = jnp.max(c, axis=2), jnp.min(c, axis=2)
    upper = qmax[:, :, None] * kmax[:, None, :] + cmax[:, :, None] - cmin[:, None, :]
    own = -(qmax * kmax)
    earlier = jnp.arange(n)[None, :, None, None] > jnp.arange(n)[None, None, :, None]
    skip = (upper < own[:, :, None] - FOX_NEGLIGIBLE_LOG2) & earlier
    first = jnp.sum(jnp.cumprod(skip.astype(jnp.int32), axis=2), axis=2)
    first = jnp.min(first.reshape(B, n, N_HEADS // heads_per_step, heads_per_step), axis=-1)
    return jnp.transpose(first, (0, 2, 1)).reshape(-1).astype(jnp.int32)


def fox_attention(qkv, qa, ka, c2, *, tq=1024, tk=1024, heads_per_step=2):
    B, S, _ = qkv.shape
    tq = tk = min(tq, S)
    hs = heads_per_step
    wq = hs * LANES
    n_qblk = N_HEADS // hs
    first = fox_first_chunk(qkv, c2, tk, hs)
    grid_spec = pltpu.PrefetchScalarGridSpec(
        num_scalar_prefetch=1,
        grid=(B, n_qblk, S // tq),
        in_specs=[pl.BlockSpec((1, tq, wq), lambda b, h, i, first: (b, i, h)),
                  pl.BlockSpec((1, tq, wq), lambda b, h, i, first: (b, i, h)),
                  pl.BlockSpec((1, S, wq), lambda b, h, i, first: (b, 0, n_qblk + h)),
                  pl.BlockSpec((1, S, wq), lambda b, h, i, first: (b, 0, h))],
        out_specs=pl.BlockSpec((1, tq, hs * HEAD_DIM), lambda b, h, i, first: (b, i, h)))
    return pl.pallas_call(
        functools.partial(_fox_attn_kernel, tq=tq, tk=tk, heads_per_step=hs),
        grid_spec=grid_spec,
        out_shape=jax.ShapeDtypeStruct((B, S, ATTN_WIDTH), BF16),
        compiler_params=_cparams(("parallel", "parallel", "arbitrary")),
    )(first, qkv, qa, qkv, ka)


def _rot_half_cols(w):
    D = w.shape[0]
    a = w.reshape(D, -1, HEAD_DIM)
    half = HEAD_DIM // 2
    return jnp.concatenate([-a[..., half:], a[..., :half]], axis=-1).reshape(w.shape)


def _rope_tables(S):
    half = HEAD_DIM // 2
    inv_freq = ROPE_THETA ** (-jnp.arange(half, dtype=F32) / half)
    ang = jnp.arange(S, dtype=F32)[:, None] * inv_freq[None, :]
    c, s = jnp.cos(ang), jnp.sin(ang)
    c2, s2 = jnp.concatenate([c, c], axis=1), jnp.concatenate([s, s], axis=1)
    cos = jnp.stack([jnp.concatenate([c2, jnp.ones_like(c2)], axis=1), jnp.concatenate([c2, c2], axis=1)])
    sin = jnp.stack([jnp.concatenate([s2, jnp.zeros_like(s2)], axis=1), jnp.concatenate([s2, s2], axis=1)])
    return cos, sin


def _compress_kernel(x_ref, pea_ref, peb_ref, wa_ref, wb_ref, w2_ref, o_ref, pa_ref, pb0_ref, *, n_rows):
    u = pl.program_id(1)
    x = x_ref[0].astype(F32)
    pa = jnp.dot((x + pea_ref[...]).astype(BF16), wa_ref[...], preferred_element_type=F32)
    pb = jnp.dot((x + peb_ref[...]).astype(BF16), wb_ref[...], preferred_element_type=F32)

    def emit(slab, hid):
        y = jnp.dot(_gelu_tanh(hid).astype(BF16), w2_ref[...], preferred_element_type=F32)
        o_ref[0, pl.ds(pl.multiple_of(slab * n_rows, n_rows), n_rows), :] = y

    @pl.when(u == 0)
    def _():
        pb0_ref[...] = pb

    @pl.when(u > 0)
    def _():
        emit(u - 1, pa_ref[...] + pb)

    @pl.when(u == 3)
    def _():
        emit(3, pa + pltpu.roll(pb0_ref[...], n_rows - 1, axis=0))

    pa_ref[...] = pa


def nsa_compress(src, pe, w1, w2):
    B, S, W = src.shape
    G = NSA_GROUPS
    n_rows = S // 64
    half = CMP_BLOCK // 2
    cw = half * W
    xv = src.reshape(B, n_rows, 4 * cw)
    pe_flat = jnp.transpose(pe, (1, 0, 2)).reshape(CMP_BLOCK, W).astype(F32)
    pea, peb = pe_flat[:half].reshape(1, cw), pe_flat[half:].reshape(1, cw)
    eye = jnp.eye(G, dtype=F32)
    wfull = jnp.einsum('gldh,gk->lkdgh', w1.astype(F32), eye).reshape(CMP_BLOCK, W, G * CMP_HIDDEN)
    wa = wfull[:half].reshape(cw, G * CMP_HIDDEN).astype(BF16)
    wb = wfull[half:].reshape(cw, G * CMP_HIDDEN).astype(BF16)
    w2bd = jnp.einsum('ghd,gk->ghkd', w2.astype(F32), eye).reshape(G * CMP_HIDDEN, W).astype(BF16)
    const = lambda a: pl.BlockSpec(a.shape, lambda b, u: (0,) * a.ndim)
    return pl.pallas_call(
        functools.partial(_compress_kernel, n_rows=n_rows),
        grid=(B, 4),
        in_specs=[pl.BlockSpec((1, n_rows, cw), lambda b, u: (b, 0, u)),
                  const(pea), const(peb), const(wa), const(wb), const(w2bd)],
        out_specs=pl.BlockSpec((1, 4 * n_rows, W), lambda b, u: (b, 0, 0)),
        out_shape=jax.ShapeDtypeStruct((B, 4 * n_rows, W), F32),
        scratch_shapes=[pltpu.VMEM((n_rows, G * CMP_HIDDEN), F32), pltpu.VMEM((n_rows, G * CMP_HIDDEN), F32)],
        compiler_params=_cparams(("parallel", "arbitrary")),
    )(xv, pea, peb, wa, wb, w2bd)


def _nsa_attn_kernel(q_ref, kvs_ref, kvw_ref, kvc_ref, kvct_ref, gate_ref, bmat_ref, pmat_ref, o_ref,
                     *, tq, tk, seq):
    R = NSA_Q_PER_GROUP
    M = R * tq
    n_slc = seq // SLC_BLOCK
    n_cmp = 4 * n_slc
    n_sel = min(SLC_TOPK, n_slc)
    blocks_per_chunk = tk // SLC_BLOCK
    assert tq & (tq - 1) == 0 and n_slc & (n_slc - 1) == 0 and tk % tq == 0
    log_slc = n_slc.bit_length() - 1
    qi = pl.program_id(2)
    t0 = qi * tq

    qs = jnp.concatenate([q_ref[0, :, r * LANES:(r + 1) * LANES] for r in range(R)], axis=0)

    kvc = kvc_ref[0, 0]
    sT = lax.dot_general(kvc, qs, _NT, preferred_element_type=F32)
    rowc = lax.broadcasted_iota(jnp.int32, (n_cmp, 1), 0)
    cmp_end = (rowc & (n_slc - 1)) * SLC_BLOCK + (rowc >> log_slc) * CMP_STRIDE + (CMP_BLOCK - 1)
    tcol = t0 + (lax.broadcasted_iota(jnp.int32, (1, M), 1) & (tq - 1))
    sm = jnp.where(cmp_end <= tcol, sT, NEG_INF)
    mx = jnp.max(sm, axis=0, keepdims=True)
    e = jnp.exp2(sm - mx)
    inv = jnp.where(mx > 0.5 * NEG_INF, 1.0 / jnp.sum(e, axis=0, keepdims=True), 0.0)
    pT = e * inv
    o_c = jnp.dot(kvct_ref[0, 0], pT.astype(BF16), preferred_element_type=F32).T

    wlen = WINDOW + tq
    start = jnp.maximum(t0 - WINDOW, 0)
    kvw = kvw_ref[0, pl.ds(pl.multiple_of(start, tq), wlen), :]
    s_w = lax.dot_general(qs, kvw, _NT, preferred_element_type=F32)
    qpos = t0 + (lax.broadcasted_iota(jnp.int32, (M, 1), 0) & (tq - 1))
    kpos = start + lax.broadcasted_iota(jnp.int32, (1, wlen), 1)
    s_w = jnp.where(kpos <= qpos, jnp.where(kpos > qpos - WINDOW, s_w, NEG_INF), NEG_INF)
    p_w = jnp.exp2(s_w - jnp.max(s_w, axis=1, keepdims=True)).astype(BF16)
    wlane = lax.broadcasted_iota(jnp.int32, (wlen, LANES), 1)
    acc_w = jnp.dot(p_w, jnp.where(wlane < HEAD_DIM, 1.0, kvw).astype(BF16), preferred_element_type=F32)
    o_w = acc_w / acc_w[:, 0:1]

    psum = pT[:, 0:tq]
    for r in range(1, R):
        psum = psum + pT[:, r * tq:(r + 1) * tq]
    p0, p1, p2, p3 = (psum[u * n_slc:(u + 1) * n_slc, :] for u in range(4))
    jrow = lax.broadcasted_iota(jnp.int32, (n_slc, tq), 0)
    p3_prev = jnp.where(jrow == 0, 0.0, pltpu.roll(p3, 1, axis=0))
    imp = p0 + p1 + p2 + 0.5 * p3 + 0.5 * p3_prev
    cur = (t0 + lax.broadcasted_iota(jnp.int32, (n_slc, tq), 1)) >> (SLC_BLOCK.bit_length() - 1)
    forced = (jrow == 0) | (jrow == cur) | (jrow == cur - 1)
    vals = jnp.where(forced, -jnp.inf, jnp.where(jrow <= cur, imp, -jnp.inf))
    sel0 = jnp.where(forced, 1.0, 0.0)
    n_free = n_sel - 3

    def pick(exact_ties):
        v, sel = vals, sel0
        for _ in range(n_free):
            m = jnp.max(v, axis=0, keepdims=True)
            if exact_ties:
                hit = jrow == jnp.min(jnp.where(v == m, jrow, n_slc), axis=0, keepdims=True)
            else:
                hit = v == jnp.where(m == -jnp.inf, jnp.nan, m)
            sel = jnp.where(hit, 1.0, sel)
            v = jnp.where(hit, -jnp.inf, v)
        return sel

    sel_fast = pick(False)
    n_cand = jnp.sum(jnp.where(vals > -jnp.inf, 1.0, 0.0), axis=0, keepdims=True)
    n_picked = jnp.sum(sel_fast - sel0, axis=0, keepdims=True)
    unique = jnp.min(jnp.where(n_picked == jnp.minimum(n_cand, float(n_free)), 1.0, 0.0)) > 0.5
    sel = lax.cond(unique, lambda: sel_fast, lambda: pick(True))
    sel_bias = ((sel.T - 1.0) * (-NEG_INF)).astype(BF16)

    n_parts = 2
    hp = R // n_parts
    mp = hp * tq
    qparts = [qs[i * mp:(i + 1) * mp] for i in range(n_parts)]
    qrow = t0 + (lax.broadcasted_iota(jnp.int32, (mp, 1), 0) & (tq - 1))
    kcol = lax.broadcasted_iota(jnp.int32, (1, tk), 1)
    n_chunks = seq // tk
    j_last = t0 // tk

    klane = lax.broadcasted_iota(jnp.int32, (tk, LANES), 1)

    def slc_step(j, carry, masked):
        kv = kvs_ref[0, pl.ds(pl.multiple_of(j * tk, tk), tk), :]
        k_sel = jnp.where(klane < HEAD_DIM, kv, bmat_ref[...]).astype(BF16)
        ones_v = jnp.where(klane < HEAD_DIM, 1.0, kv).astype(BF16)
        off = pl.multiple_of(blocks_per_chunk * (n_chunks - 1 - j), blocks_per_chunk)
        place = pmat_ref[pl.ds(off, n_slc), :].astype(BF16)
        q_bias = jnp.dot(sel_bias, place, preferred_element_type=F32).astype(BF16)
        q_bias = jnp.concatenate([q_bias] * hp, axis=0)
        new = []
        for qp, (m, acc) in zip(qparts, carry):
            s = lax.dot_general(qp + q_bias, k_sel, _NT, preferred_element_type=F32)
            if masked:
                s = jnp.where(j * tk + kcol <= qrow, s, NEG_INF)
            m_new = jnp.maximum(m, jnp.max(s, axis=1, keepdims=True))
            p = jnp.exp2(s - m_new).astype(BF16)
            acc = jnp.exp2(m - m_new) * acc + jnp.dot(p, ones_v, preferred_element_type=F32)
            new.append((m_new, acc))
        return tuple(new)

    init = (jnp.full((mp, 1), NEG_INF, F32), jnp.zeros((mp, LANES), F32))
    n_pairs = j_last // 2
    carry = lax.fori_loop(0, n_pairs, lambda i, c: slc_step(2 * i + 1, slc_step(2 * i, c, False), False),
                          (init,) * n_parts)
    carry = lax.fori_loop(2 * n_pairs, j_last, functools.partial(slc_step, masked=False), carry)
    carry = slc_step(j_last, carry, True)
    o_s = jnp.concatenate([acc / acc[:, 0:1] for (_, acc) in carry], axis=0)

    gates = gate_ref[0]
    lane = lax.broadcasted_iota(jnp.int32, (tq, LANES), 1)
    comb = []
    for r in range(R):
        rs = slice(r * tq, (r + 1) * tq)
        comb.append(gates[:, r:r + 1] * o_c[rs] + gates[:, R + r:R + r + 1] * o_s[rs]
                    + gates[:, 2 * R + r:2 * R + r + 1] * o_w[rs])
    out = [jnp.where(lane < HEAD_DIM, pltpu.roll(comb[2 * i], HEAD_DIM, axis=1), comb[2 * i + 1])
           for i in range(R // 2)]
    o_ref[0] = jnp.concatenate(out, axis=1).astype(o_ref.dtype)


def nsa_attention(qkv, kvc, kvct, gates, *, tq=256, tk=1024):
    B, S, _ = qkv.shape
    G, R = NSA_GROUPS, NSA_Q_PER_GROUP
    tk = min(tk, S)
    n_slc = S // SLC_BLOCK
    n_cmp = kvc.shape[2]
    bpc = tk // SLC_BLOCK
    assert bpc <= LANES - HEAD_DIM
    off = bpc * (S // tk - 1)
    lane = np.arange(LANES)[None, :]
    bmat = jnp.asarray(lane - HEAD_DIM == np.arange(tk)[:, None] // SLC_BLOCK, BF16)
    pmat = jnp.asarray((np.arange(n_slc + off)[:, None] - off == lane - HEAD_DIM) & (lane >= HEAD_DIM)
                       & (lane < HEAD_DIM + bpc), F32)
    slc_blk0, win_blk0 = N_HEADS, N_HEADS + G
    return pl.pallas_call(
        functools.partial(_nsa_attn_kernel, tq=tq, tk=tk, seq=S),
        grid=(B, G, S // tq),
        in_specs=[pl.BlockSpec((1, tq, R * LANES), lambda b, g, i: (b, i, g)),
                  pl.BlockSpec((1, S, LANES), lambda b, g, i: (b, 0, slc_blk0 + g)),
                  pl.BlockSpec((1, S, LANES), lambda b, g, i: (b, 0, win_blk0 + g)),
                  pl.BlockSpec((1, 1, n_cmp, LANES), lambda b, g, i: (b, g, 0, 0)),
                  pl.BlockSpec((1, 1, LANES, n_cmp), lambda b, g, i: (b, g, 0, 0)),
                  pl.BlockSpec((1, tq, LANES), lambda b, g, i: (b, i, g)),
                  pl.BlockSpec(bmat.shape, lambda b, g, i: (0, 0)),
                  pl.BlockSpec(pmat.shape, lambda b, g, i: (0, 0))],
        out_specs=pl.BlockSpec((1, tq, R * HEAD_DIM), lambda b, g, i: (b, i, g)),
        out_shape=jax.ShapeDtypeStruct((B, S, ATTN_WIDTH), BF16),
        compiler_params=_cparams(("parallel", "parallel", "arbitrary")),
    )(qkv, qkv, qkv, kvc, kvct, gates, bmat, pmat)


def nsa_layer(h, gamma, w_in, pe_k, w1_k, w2_k, pe_v, w1_v, w2_v, w_out, B, S):
    G, R, hd, aw = NSA_GROUPS, NSA_Q_PER_GROUP, HEAD_DIM, ATTN_WIDTH
    kvd = G * hd
    sec = lambda i: w_in[:, aw + i * kvd: aw + (i + 1) * kvd]
    wq = w_in[:, :aw] * (hd ** -0.5 * LOG2E)
    wa = jnp.concatenate([_head_slots(wq, N_HEADS), _head_slots(sec(2), G, sec(3)),
                          _head_slots(sec(4), G, sec(5))], axis=1).astype(BF16)
    wb = jnp.concatenate([_head_slots(_rot_half_cols(wq), N_HEADS), _head_slots(_rot_half_cols(sec(2)), G),
                          _head_slots(_rot_half_cols(sec(4)), G)], axis=1).astype(BF16)
    cos, sin = _rope_tables(S)
    qkv = norm_matmul(h, gamma, wa, wb=wb, cos=cos, sin=sin)
    kc_src = norm_matmul(h, gamma, sec(0).astype(BF16), wb=_rot_half_cols(sec(0)).astype(BF16),
                         cos=cos, sin=sin, table_of_tile=lambda j: 1)
    vc_src = norm_matmul(h, gamma, sec(1).astype(BF16))
    wg = w_in[:, aw + 6 * kvd:].reshape(-1, 3, G, R)
    wg = jnp.transpose(wg, (0, 2, 1, 3)).reshape(-1, G, 3 * R)
    wg = jnp.pad(wg, ((0, 0), (0, 0), (0, LANES - 3 * R))).reshape(-1, G * LANES).astype(BF16)
    gates = norm_matmul(h, gamma, wg, act="sigmoid", out_dtype=F32)
    kc = nsa_compress(kc_src.reshape(B, S, kvd), pe_k, w1_k, w2_k)
    vc = nsa_compress(vc_src.reshape(B, S, kvd), pe_v, w1_v, w2_v)
    n_cmp = kc.shape[1]
    kvc = jnp.concatenate([kc.reshape(B, n_cmp, G, hd), vc.reshape(B, n_cmp, G, hd)], axis=-1)
    kvc = jnp.transpose(kvc, (0, 2, 1, 3)).astype(BF16)
    kvct = jnp.swapaxes(kvc, 2, 3)
    o = nsa_attention(qkv.reshape(B, S, -1), kvc, kvct, gates.reshape(B, S, G * LANES))
    return matmul_residual(o.reshape(B * S, aw), w_out.astype(BF16), h)


def kernel(x, l0_attn_norm, l0_w_in, l0_cmp_pe_k, l0_cmp_w1_k, l0_cmp_w2_k, l0_cmp_pe_v, l0_cmp_w1_v,
           l0_cmp_w2_v, l0_w_out, l0_ffn_norm, l0_peer_wq, l0_peer_keys, l0_peer_u, l0_peer_v,
           l1_attn_norm, l1_w_in, l1_f_bias, l1_w_out, l1_ffn_norm, l1_peer_wq, l1_peer_keys, l1_peer_u,
           l1_peer_v, final_norm):
    B, S, D = x.shape
    h = x.reshape(B * S, D)
    h = nsa_layer(h, l0_attn_norm, l0_w_in, l0_cmp_pe_k, l0_cmp_w1_k, l0_cmp_w2_k, l0_cmp_pe_v, l0_cmp_w1_v,
                  l0_cmp_w2_v, l0_w_out, B, S)
    h = peer_layer(h, l0_ffn_norm, l0_peer_wq, l0_peer_keys, l0_peer_u, l0_peer_v)
    h = fox_layer(h, l1_attn_norm, l1_w_in, l1_f_bias, l1_w_out, B, S)
    h = peer_layer(h, l1_ffn_norm, l1_peer_wq, l1_peer_keys, l1_peer_u, l1_peer_v)
    return rmsnorm(h, final_norm).reshape(B, S, D)
```

```python
import functools

import numpy as np
import jax
import jax.numpy as jnp
from jax import lax
from jax.experimental import pallas as pl
from jax.experimental.pallas import tpu as pltpu

F32 = jnp.float32
BF16 = jnp.bfloat16

D_MODEL = 1024
N_HEADS = 16
HEAD_DIM = 64
ATTN_WIDTH = N_HEADS * HEAD_DIM
NSA_GROUPS = 4
NSA_Q_PER_GROUP = N_HEADS // NSA_GROUPS
CMP_BLOCK = 32
CMP_STRIDE = 16
CMP_HIDDEN = 2 * HEAD_DIM
SLC_BLOCK = 64
SLC_TOPK = 16
WINDOW = 512
FORCE_SCORE = 1.0e4
ROPE_THETA = 10000.0
PEER_HEADS = 8
PEER_N_KEYS = 128
PEER_TOPK = 16
PEER_HALF_DIM = 128
RMS_EPS = 1e-6
NEG_INF = -1e30
LOG2E = 1.4426950408889634

LANES = 128
VMEM_LIMIT_BYTES = 56 * 1024 * 1024

_NT = (((1,), (1,)), ((), ()))


def _cparams(sem, vmem=VMEM_LIMIT_BYTES, flags=None):
    return pltpu.CompilerParams(dimension_semantics=sem, vmem_limit_bytes=vmem, flags=flags)


def _gelu_tanh(x):
    return 0.5 * x * (1.0 + jnp.tanh(0.7978845608028654 * (x + 0.044715 * (x * x * x))))


def _gelu_sigmoid(x):
    c = -2.0 * 0.7978845608028654 * LOG2E
    t = x * (c + (c * 0.044715) * (x * x))
    return x / (1.0 + jnp.exp2(t))


def _rms_rows(x, g):
    ms = jnp.mean(x * x, axis=-1, keepdims=True)
    return x * lax.rsqrt(ms + RMS_EPS) * g


def _norm_mm_kernel(*refs, act, has_bias, rope, emit_xn):
    it = iter(refs)
    x_ref, g_ref = next(it), next(it)
    wa_ref = next(it)
    wb_ref = next(it) if rope else None
    cos_ref = next(it) if rope else None
    sin_ref = next(it) if rope else None
    b_ref = next(it) if has_bias else None
    o_ref = next(it)
    xo_ref = next(it) if emit_xn else None
    xn_ref = next(it)

    @pl.when(pl.program_id(1) == 0)
    def _():
        xn = _rms_rows(x_ref[...], g_ref[...])
        xn_ref[...] = xn.astype(BF16)
        if emit_xn:
            xo_ref[...] = xn.T.astype(BF16)

    xn = xn_ref[...]
    y = jnp.dot(xn, wa_ref[...], preferred_element_type=F32)
    if rope:
        yb = jnp.dot(xn, wb_ref[...], preferred_element_type=F32)
        cos, sin = cos_ref[...], sin_ref[...]
        for s in range(y.shape[1] // LANES):
            sl = slice(s * LANES, (s + 1) * LANES)
            o_ref[:, sl] = (y[:, sl] * cos + yb[:, sl] * sin).astype(o_ref.dtype)
        return
    if has_bias:
        y = y + b_ref[...]
    if act == "sigmoid":
        y = jax.nn.sigmoid(y)
    elif act == "log_sigmoid":
        y = jax.nn.log_sigmoid(y)
    o_ref[...] = y.astype(o_ref.dtype)


def norm_matmul(x, gamma, wa, *, wb=None, cos=None, sin=None, table_of_tile=None, bias=None,
                act=None, out_dtype=BF16, emit_xn=False, tm=1024, tn=1024):
    T, D = x.shape
    N = wa.shape[1]
    tm, tn = min(tm, T), min(tn, N)
    assert T % tm == 0 and N % tn == 0 and tn % LANES == 0
    rope = wb is not None
    in_specs = [pl.BlockSpec((tm, D), lambda i, j: (i, 0)),
                pl.BlockSpec((1, D), lambda i, j: (0, 0)),
                pl.BlockSpec((D, tn), lambda i, j: (0, j))]
    args = [x, gamma.reshape(1, D).astype(F32), wa]
    if rope:
        S = cos.shape[1]
        assert S % tm == 0
        n_pos = S // tm
        tmap = table_of_tile if table_of_tile is not None else (lambda j: 0)
        in_specs += [pl.BlockSpec((D, tn), lambda i, j: (0, j)),
                     pl.BlockSpec((None, tm, LANES), lambda i, j: (tmap(j), i % n_pos, 0)),
                     pl.BlockSpec((None, tm, LANES), lambda i, j: (tmap(j), i % n_pos, 0))]
        args += [wb, cos, sin]
    if bias is not None:
        in_specs.append(pl.BlockSpec((1, tn), lambda i, j: (0, j)))
        args.append(bias.reshape(1, N).astype(F32))
    out_shape = [jax.ShapeDtypeStruct((T, N), out_dtype)]
    out_specs = [pl.BlockSpec((tm, tn), lambda i, j: (i, j))]
    if emit_xn:
        out_shape.append(jax.ShapeDtypeStruct((D, T), BF16))
        out_specs.append(pl.BlockSpec((D, tm), lambda i, j: (0, i)))
    res = pl.pallas_call(
        functools.partial(_norm_mm_kernel, act=act, has_bias=bias is not None, rope=rope, emit_xn=emit_xn),
        grid=(T // tm, N // tn),
        in_specs=in_specs,
        out_specs=out_specs,
        out_shape=out_shape,
        scratch_shapes=[pltpu.VMEM((tm, D), BF16)],
        compiler_params=_cparams(("parallel", "arbitrary")),
    )(*args)
    return res if emit_xn else res[0]


def _mm_res_kernel(a_ref, w_ref, r_ref, o_ref):
    o_ref[...] = r_ref[...] + jnp.dot(a_ref[...], w_ref[...], preferred_element_type=F32)


def matmul_residual(a, w, res, *, tm=1024, tn=1024):
    T, K = a.shape
    N = w.shape[1]
    tm, tn = min(tm, T), min(tn, N)
    assert T % tm == 0 and N % tn == 0
    return pl.pallas_call(
        _mm_res_kernel,
        grid=(T // tm, N // tn),
        in_specs=[pl.BlockSpec((tm, K), lambda i, j: (i, 0)),
                  pl.BlockSpec((K, tn), lambda i, j: (0, j)),
                  pl.BlockSpec((tm, tn), lambda i, j: (i, j))],
        out_specs=pl.BlockSpec((tm, tn), lambda i, j: (i, j)),
        out_shape=jax.ShapeDtypeStruct((T, N), F32),
        compiler_params=_cparams(("parallel", "arbitrary")),
    )(a, w, res)


def _rmsnorm_kernel(x_ref, g_ref, o_ref):
    o_ref[...] = _rms_rows(x_ref[...], g_ref[...])


def rmsnorm(x, gamma, *, tm=1024):
    T, D = x.shape
    tm = min(tm, T)
    return pl.pallas_call(
        _rmsnorm_kernel,
        grid=(T // tm,),
        in_specs=[pl.BlockSpec((tm, D), lambda i: (i, 0)), pl.BlockSpec((1, D), lambda i: (0, 0))],
        out_specs=pl.BlockSpec((tm, D), lambda i: (i, 0)),
        out_shape=jax.ShapeDtypeStruct((T, D), F32),
        compiler_params=_cparams(("parallel",)),
    )(x, gamma.reshape(1, D).astype(F32))


def _peer_cand_tables(tn):
    fidx, vmask = [], []
    for k2 in range(16):
        fidx.append(k2); vmask.append(0.0)
    for k1 in range(1, 8):
        lim = PEER_TOPK // (k1 + 1)
        for k2 in range(8):
            fidx.append(k1 * 16 + k2); vmask.append(0.0 if k2 < lim else -np.inf)
    for k1 in range(8, 16):
        fidx.append(k1 * 16); vmask.append(0.0)
    fidx = np.broadcast_to(np.asarray(fidx, np.int32)[:, None], (80, tn))
    vmask = np.broadcast_to(np.asarray(vmask, np.float32)[:, None], (80, tn))
    return jnp.asarray(fidx), jnp.asarray(vmask)


def _top16_rows(s, exact_ties):
    n, tn = s.shape
    rows = lax.broadcasted_iota(jnp.int32, (n, tn), 0)
    rows16 = lax.broadcasted_iota(jnp.int32, (PEER_TOPK, tn), 0)
    rank = jnp.full((n, tn), float(PEER_TOPK), F32)
    tops = jnp.zeros((PEER_TOPK, tn), F32)
    v = s
    for k in range(PEER_TOPK):
        m = jnp.max(v, axis=0, keepdims=True)
        if exact_ties:
            hit = rows == jnp.min(jnp.where(v == m, rows, n), axis=0, keepdims=True)
        else:
            hit = v == m
        rank = jnp.where(hit, float(k), rank)
        v = jnp.where(hit, -jnp.inf, v)
        tops = jnp.where(rows16 == k, m, tops)
    return tops, rank, v


def _peer_select_head(q_ref, keys_ref, fidx, vmask, exact_ties):
    tops, ranks, es, picked = [], [], [], []
    for p in range(2):
        q = q_ref[:, p * PEER_HALF_DIM:(p + 1) * PEER_HALF_DIM]
        s = lax.dot_general(keys_ref[p], q, _NT, preferred_element_type=F32)
        t, r, v = _top16_rows(s, exact_ties)
        tops.append(t); ranks.append(r)
        es.append(jnp.exp(s - t[0:1, :]))
        picked.append(jnp.sum(jnp.where(v == -jnp.inf, 1.0, 0.0), axis=0, keepdims=True))
    ts1, ts2 = tops
    pieces = [ts1[0:1, :] + ts2]
    for k1 in range(1, 8):
        pieces.append(ts1[k1:k1 + 1, :] + ts2[0:8, :])
    pieces.append(ts1[8:16, :] + ts2[0:1, :])
    cand0 = jnp.concatenate(pieces, axis=0) + vmask
    cand = cand0
    for _ in range(PEER_TOPK):
        m = jnp.max(cand, axis=0, keepdims=True)
        if exact_ties:
            hit = fidx == jnp.min(jnp.where(cand == m, fidx, 4096), axis=0, keepdims=True)
        else:
            hit = cand == m
        cand = jnp.where(hit, -jnp.inf, cand)
    taken = jnp.logical_and(cand == -jnp.inf, vmask == 0.0)
    takenf = taken.astype(F32)
    picked.append(jnp.sum(takenf, axis=0, keepdims=True))
    unique = jnp.min(jnp.where((picked[0] == PEER_TOPK) & (picked[1] == PEER_TOPK) & (picked[2] == PEER_TOPK),
                               1.0, 0.0)) > 0.5
    best = ts1[0:1, :] + ts2[0:1, :]
    z = jnp.sum(jnp.where(taken, jnp.exp(cand0 - best), 0.0), axis=0, keepdims=True)
    counts = [jnp.sum(takenf[0:16, :], axis=0, keepdims=True)]
    for k1 in range(1, 8):
        counts.append(jnp.sum(takenf[16 + 8 * (k1 - 1):16 + 8 * k1, :], axis=0, keepdims=True))
    tail = takenf[72:80, :]
    cnt = jnp.zeros_like(ranks[0])
    for k1 in range(PEER_TOPK):
        nk = counts[k1] if k1 < 8 else tail[k1 - 8:k1 - 7, :]
        cnt = jnp.where(ranks[0] == float(k1), nk, cnt)
    return (cnt, ranks[1], es[0], es[1] / z), unique


def _peer_select_kernel(q_ref, keys_ref, fidx_ref, vmask_ref, cnt_ref, rank2_ref, e1_ref, e2_ref):
    fidx = fidx_ref[...]
    vmask = vmask_ref[...]

    def store(vals):
        for ref, val in zip((cnt_ref, rank2_ref, e1_ref, e2_ref), vals):
            ref[...] = val.astype(ref.dtype)

    vals, unique = _peer_select_head(q_ref, keys_ref, fidx, vmask, exact_ties=False)
    store(vals)

    @pl.when(jnp.logical_not(unique))
    def _():
        store(_peer_select_head(q_ref, keys_ref, fidx, vmask, exact_ties=True)[0])


def peer_select(q, keys, *, tn=512):
    T = q.shape[0]
    tn = min(tn, T)
    fidx, vmask = _peer_cand_tables(tn)
    rows = PEER_HEADS * PEER_N_KEYS
    ospec = pl.BlockSpec((PEER_N_KEYS, tn), lambda i, h: (h, i))
    return pl.pallas_call(
        _peer_select_kernel,
        grid=(T // tn, PEER_HEADS),
        in_specs=[pl.BlockSpec((tn, 2 * PEER_HALF_DIM), lambda i, h: (i, h)),
                  pl.BlockSpec((2, PEER_N_KEYS, PEER_HALF_DIM), lambda i, h: (h, 0, 0)),
                  pl.BlockSpec((80, tn), lambda i, h: (0, 0)),
                  pl.BlockSpec((80, tn), lambda i, h: (0, 0))],
        out_specs=[ospec] * 4,
        out_shape=[jax.ShapeDtypeStruct((rows, T), dt) for dt in (F32, BF16, F32, BF16)],
        compiler_params=_cparams(("parallel", "parallel")),
    )(q, keys, fidx, vmask)


def _peer_dense_kernel(xn_ref, u_ref, vt_ref, cnt_ref, rank2_ref, e1_ref, e2_ref, res_ref, o_ref,
                       acc_ref, g0_ref, g1_ref, *, c_per_step):
    j = pl.program_id(1)
    n_tiles = pl.num_programs(1) - 1

    @pl.when(j == 0)
    def _():
        acc_ref[...] = jnp.zeros_like(acc_ref)
        g1_ref[...] = jnp.zeros_like(g1_ref)

    @pl.when(j % 2 == 0)
    def _():
        _peer_dense_step(xn_ref, u_ref, vt_ref, cnt_ref, rank2_ref, e1_ref, e2_ref, acc_ref,
                         g1_ref, g0_ref, j, c_per_step)

    @pl.when(j % 2 == 1)
    def _():
        _peer_dense_step(xn_ref, u_ref, vt_ref, cnt_ref, rank2_ref, e1_ref, e2_ref, acc_ref,
                         g0_ref, g1_ref, j, c_per_step)

    @pl.when(j == n_tiles)
    def _():
        o_ref[...] = res_ref[...] + acc_ref[...].T


def _peer_dense_step(xn_ref, u_ref, vt_ref, cnt_ref, rank2_ref, e1_ref, e2_ref, acc_ref, g_ref, g_next_ref,
                     j, c_per_step):

    tn = xn_ref.shape[1]
    bf16_rows = 16
    reps = PEER_N_KEYS // bf16_rows

    def row_tile(ref, row):
        r16 = jnp.broadcast_to(ref[pl.ds(row, 1), :], (bf16_rows, tn)).astype(BF16)
        return jnp.concatenate([r16] * reps, axis=0)

    c0 = jnp.maximum(j - 1, 0) * c_per_step
    up_rows = 2 * PEER_N_KEYS
    blocks = []
    for cc in range(c_per_step):
        if (cc * PEER_N_KEYS) % up_rows == 0:
            rs = slice(cc * PEER_N_KEYS, cc * PEER_N_KEYS + up_rows)
            hT = jnp.dot(u_ref[rs, :], xn_ref[...], preferred_element_type=F32)
            g_next_ref[rs, :] = _gelu_sigmoid(hT.astype(BF16))
        c = c0 + cc
        w = None
        for h in range(PEER_HEADS):
            row = h * PEER_N_KEYS + c
            n_row = row_tile(cnt_ref, row)
            e1_row = row_tile(e1_ref, row)
            sl = slice(h * PEER_N_KEYS, (h + 1) * PEER_N_KEYS)
            term = jnp.where(rank2_ref[sl, :] < n_row, e2_ref[sl, :], 0.0) * e1_row
            w = term if w is None else w + term
        blocks.append(w * g_ref[cc * PEER_N_KEYS:(cc + 1) * PEER_N_KEYS, :])
    aT = jnp.concatenate(blocks, axis=0)
    acc_ref[...] += jnp.dot(vt_ref[...], aT, preferred_element_type=F32)


def peer_dense(xn, u, vt, cnt, rank2, e1, e2, res, *, tn=512, te=1024):
    D, T = xn.shape
    E = u.shape[0]
    tn = min(tn, T)
    rows = PEER_HEADS * PEER_N_KEYS
    sel_spec = pl.BlockSpec((rows, tn), lambda i, j: (0, i))
    n_tiles = E // te
    return pl.pallas_call(
        functools.partial(_peer_dense_kernel, c_per_step=te // PEER_N_KEYS),
        grid=(T // tn, n_tiles + 1),
        in_specs=[pl.BlockSpec((D, tn), lambda i, j: (0, i)),
                  pl.BlockSpec((te, D), lambda i, j: (jnp.minimum(j, n_tiles - 1), 0)),
                  pl.BlockSpec((D, te), lambda i, j: (0, jnp.maximum(j - 1, 0))),
                  sel_spec, sel_spec, sel_spec, sel_spec,
                  pl.BlockSpec((tn, D), lambda i, j: (i, 0))],
        out_specs=pl.BlockSpec((tn, D), lambda i, j: (i, 0)),
        out_shape=jax.ShapeDtypeStruct((T, D), F32),
        scratch_shapes=[pltpu.VMEM((D, tn), F32), pltpu.VMEM((te, tn), BF16), pltpu.VMEM((te, tn), BF16)],
        compiler_params=_cparams(("parallel", "arbitrary")),
    )(xn, u, vt, cnt, rank2, e1, e2, res)


def peer_layer(h, gamma, w_q, sub_keys, u, v):
    q, xn = norm_matmul(h, gamma, w_q.astype(BF16), emit_xn=True)
    keys = sub_keys.reshape(2 * PEER_HEADS, PEER_N_KEYS, PEER_HALF_DIM).astype(BF16)
    cnt, rank2, e1, e2 = peer_select(q, keys)
    return peer_dense(xn, u.astype(BF16), v.T.astype(BF16), cnt, rank2, e1, e2, h)


def _cumsum_aug_kernel(lf_ref, tri_ref, place_q_ref, place_k_ref, ones_q_ref, ones_k_ref,
                       qa_ref, ka_ref, c_ref, carry_ref):
    @pl.when(pl.program_id(1) == 0)
    def _():
        carry_ref[...] = jnp.zeros_like(carry_ref)

    lf = lf_ref[0]
    c = jnp.dot(tri_ref[...], lf, preferred_element_type=F32, precision=lax.Precision.HIGHEST) + carry_ref[...]
    carry_ref[...] = c[-1:, :]
    c = c * LOG2E
    c_ref[0] = c
    hi = c.astype(BF16)
    r1 = c - hi.astype(F32)
    mid = r1.astype(BF16)
    lo = (r1 - mid.astype(F32)).astype(BF16)
    nh = N_HEADS
    lane = lax.broadcasted_iota(jnp.int32, c.shape, 1)
    parts = jnp.where(lane < nh, hi.astype(F32),
                      jnp.where(lane < 2 * nh, pltpu.roll(mid.astype(F32), nh, axis=1),
                                pltpu.roll(lo.astype(F32), 2 * nh, axis=1)))
    parts = jnp.where(lane < 3 * nh, parts, 0.0).astype(BF16)
    qa_ref[0] = (jnp.dot(parts, place_q_ref[...], preferred_element_type=F32) + ones_q_ref[...]).astype(BF16)
    ka_ref[0] = (jnp.dot(parts, place_k_ref[...], preferred_element_type=F32) + ones_k_ref[...]).astype(BF16)


def fox_bias_operands(logf, *, tc=256):
    B, S, _ = logf.shape
    nh = N_HEADS
    tri = jnp.asarray(np.tril(np.ones((tc, tc), np.float32)))
    pq = np.zeros((LANES, nh * LANES), np.float32)
    pk = np.zeros((LANES, nh * LANES), np.float32)
    oq = np.zeros((1, nh * LANES), np.float32)
    ok = np.zeros((1, nh * LANES), np.float32)
    for h in range(nh):
        for part in range(3):
            pq[part * nh + h, h * LANES + part] = 1.0
            pk[part * nh + h, h * LANES + 3 + part] = -1.0
            oq[0, h * LANES + 3 + part] = 1.0
            ok[0, h * LANES + part] = 1.0
    const = lambda a: pl.BlockSpec(a.shape, lambda b, i: (0,) * a.ndim)
    pq, pk, oq, ok = jnp.asarray(pq, BF16), jnp.asarray(pk, BF16), jnp.asarray(oq), jnp.asarray(ok)
    out = jax.ShapeDtypeStruct((B, S, nh * LANES), BF16)
    return pl.pallas_call(
        _cumsum_aug_kernel,
        grid=(B, S // tc),
        in_specs=[pl.BlockSpec((1, tc, LANES), lambda b, i: (b, i, 0)),
                  const(tri), const(pq), const(pk), const(oq), const(ok)],
        out_specs=[pl.BlockSpec((1, tc, nh * LANES), lambda b, i: (b, i, 0))] * 2
        + [pl.BlockSpec((1, tc, LANES), lambda b, i: (b, i, 0))],
        out_shape=[out, out, jax.ShapeDtypeStruct((B, S, LANES), F32)],
        scratch_shapes=[pltpu.VMEM((1, LANES), F32)],
        compiler_params=_cparams(("parallel", "arbitrary")),
    )(logf, tri, pq, pk, oq, ok)


def _fox_attn_kernel(first_ref, q_ref, qa_ref, kv_ref, ka_ref, o_ref, *, tq, tk, heads_per_step):
    qi = pl.program_id(2)
    t0 = qi * tq
    n_full = t0 // tk
    j_first = first_ref[(pl.program_id(0) * pl.num_programs(1) + pl.program_id(1)) * pl.num_programs(2) + qi]
    n_diag = tq // tk
    lanes = [slice(hh * LANES, (hh + 1) * LANES) for hh in range(heads_per_step)]
    qs = [jnp.concatenate([q_ref[0, :, lsl], qa_ref[0, :, lsl]], axis=1) for lsl in lanes]

    def step(j, carry, masked):
        rows = pl.ds(pl.multiple_of(j * tk, tk), tk)
        klane = lax.broadcasted_iota(jnp.int32, (tk, LANES), 1)
        new = []
        for lsl, q, (m, acc) in zip(lanes, qs, carry):
            kv = kv_ref[0, rows, lsl]
            kk = jnp.concatenate([kv, ka_ref[0, rows, lsl]], axis=1)
            ones_v = jnp.where(klane < HEAD_DIM, 1.0, kv).astype(BF16)
            s = lax.dot_general(q, kk, _NT, preferred_element_type=F32)
            if masked:
                qpos = t0 + lax.broadcasted_iota(jnp.int32, (tq, 1), 0)
                kpos = j * tk + lax.broadcasted_iota(jnp.int32, (1, tk), 1)
                s = jnp.where(kpos <= qpos, s, NEG_INF)
            m_new = jnp.maximum(m, jnp.max(s, axis=1, keepdims=True))
            p = jnp.exp2(s - m_new).astype(BF16)
            acc = jnp.exp2(m - m_new) * acc + jnp.dot(p, ones_v, preferred_element_type=F32)
            new.append((m_new, acc))
        return tuple(new)

    init = (jnp.full((tq, 1), NEG_INF, F32), jnp.zeros((tq, LANES), F32))
    n_pairs = (n_full - j_first) // 2
    carry = lax.fori_loop(
        0, n_pairs, lambda i, c: step(j_first + 2 * i + 1, step(j_first + 2 * i, c, False), False),
        (init,) * heads_per_step)
    carry = lax.fori_loop(j_first + 2 * n_pairs, n_full, functools.partial(step, masked=False), carry)
    for d in range(n_diag):
        carry = step(n_full + d, carry, True)
    outs = [acc / acc[:, 0:1] for (_, acc) in carry]
    lane = lax.broadcasted_iota(jnp.int32, (tq, LANES), 1)
    blocks = []
    for pair in range(heads_per_step // 2):
        a, b = outs[2 * pair], outs[2 * pair + 1]
        blocks.append(jnp.where(lane < HEAD_DIM, pltpu.roll(a, HEAD_DIM, axis=1), b))
    o_ref[0] = jnp.concatenate(blocks, axis=1).astype(o_ref.dtype) if len(blocks) > 1 else blocks[0].astype(o_ref.dtype)


def _head_slots(w, n_heads, second=None):
    D = w.shape[0]
    a = w.reshape(D, n_heads, HEAD_DIM)
    b = jnp.zeros_like(a) if second is None else second.reshape(D, n_heads, HEAD_DIM)
    return jnp.concatenate([a, b], axis=-1).reshape(D, n_heads * LANES)


def fox_layer(h, gamma, w_in, f_bias, w_out, B, S):
    aw = ATTN_WIDTH
    wq = _head_slots(w_in[:, :aw] * (HEAD_DIM ** -0.5 * LOG2E), N_HEADS)
    wkv = _head_slots(w_in[:, aw:2 * aw], N_HEADS, w_in[:, 2 * aw:3 * aw])
    w_main = jnp.concatenate([wq, wkv], axis=1).astype(BF16)
    wf = jnp.pad(w_in[:, 3 * aw:], ((0, 0), (0, LANES - N_HEADS))).astype(BF16)
    bf = jnp.pad(f_bias.astype(F32), (0, LANES - N_HEADS))
    qkv = norm_matmul(h, gamma, w_main)
    logf = norm_matmul(h, gamma, wf, bias=bf, act="log_sigmoid", out_dtype=F32)
    qa, ka, c2 = fox_bias_operands(logf.reshape(B, S, LANES))
    qkv = qkv.reshape(B, S, 2 * N_HEADS * LANES)
    o = fox_attention(qkv, qa, ka, c2)
    return matmul_residual(o.reshape(B * S, aw), w_out.astype(BF16), h)


FOX_NEGLIGIBLE_LOG2 = 200.0


def _norm_maxima_kernel(x_ref, ind_ref, o_ref):
    x = x_ref[0].astype(F32)
    ss = jnp.dot((x * x).astype(BF16), ind_ref[...], preferred_element_type=F32)
    o_ref[0, 0] = jnp.broadcast_to(jnp.max(ss, axis=0, keepdims=True), o_ref.shape[2:])


def fox_norm_maxima(qkv, tile):
    B, S, W = qkv.shape
    n = S // tile
    col = np.arange(W)
    slot, lane = col // LANES, col % LANES
    used = (slot < N_HEADS) | (lane < HEAD_DIM)
    ind = jnp.asarray((slot[:, None] == np.arange(LANES)[None, :]) & used[:, None], BF16)
    rows = min(256, tile)
    ss = pl.pallas_call(
        _norm_maxima_kernel,
        grid=(B, S // rows),
        in_specs=[pl.BlockSpec((1, rows, W), lambda b, i: (b, i, 0)),
                  pl.BlockSpec(ind.shape, lambda b, i: (0, 0))],
        out_specs=pl.BlockSpec((1, 1, 8, LANES), lambda b, i: (b, i, 0, 0)),
        out_shape=jax.ShapeDtypeStruct((B, S // rows, 8, LANES), F32),
        compiler_params=_cparams(("parallel", "parallel")),
    )(qkv, ind)
    ss = jnp.max(ss[:, :, 0, :2 * N_HEADS].reshape(B, n, tile // rows, 2 * N_HEADS), axis=2)
    return jnp.sqrt(ss * 1.01)


def fox_first_chunk(qkv, c2, tile, heads_per_step):
    B, S, _ = qkv.shape
    n = S // tile
    norms = fox_norm_maxima(qkv, tile)
    qmax, kmax = norms[..., :N_HEADS], norms[..., N_HEADS:]
    c = c2[..., :N_HEADS].reshape(B, n, tile, N_HEADS)
    cmax, cmin = jnp.max(c, axis=2), jnp.min(c, axis=2)
    upper = qmax[:, :, None] * kmax[:, None, :] + cmax[:, :, None] - cmin[:, None, :]
    own = -(qmax * kmax)
    earlier = jnp.arange(n)[None, :, None, None] > jnp.arange(n)[None, None, :, None]
    skip = (upper < own[:, :, None] - FOX_NEGLIGIBLE_LOG2) & earlier
    first = jnp.sum(jnp.cumprod(skip.astype(jnp.int32), axis=2), axis=2)
    first = jnp.min(first.reshape(B, n, N_HEADS // heads_per_step, heads_per_step), axis=-1)
    return jnp.transpose(first, (0, 2, 1)).reshape(-1).astype(jnp.int32)


def fox_attention(qkv, qa, ka, c2, *, tq=1024, tk=1024, heads_per_step=2):
    B, S, _ = qkv.shape
    tq = tk = min(tq, S)
    hs = heads_per_step
    wq = hs * LANES
    n_qblk = N_HEADS // hs
    first = fox_first_chunk(qkv, c2, tk, hs)
    grid_spec = pltpu.PrefetchScalarGridSpec(
        num_scalar_prefetch=1,
        grid=(B, n_qblk, S // tq),
        in_specs=[pl.BlockSpec((1, tq, wq), lambda b, h, i, first: (b, i, h)),
                  pl.BlockSpec((1, tq, wq), lambda b, h, i, first: (b, i, h)),
                  pl.BlockSpec((1, S, wq), lambda b, h, i, first: (b, 0, n_qblk + h)),
                  pl.BlockSpec((1, S, wq), lambda b, h, i, first: (b, 0, h))],
        out_specs=pl.BlockSpec((1, tq, hs * HEAD_DIM), lambda b, h, i, first: (b, i, h)))
    return pl.pallas_call(
        functools.partial(_fox_attn_kernel, tq=tq, tk=tk, heads_per_step=hs),
        grid_spec=grid_spec,
        out_shape=jax.ShapeDtypeStruct((B, S, ATTN_WIDTH), BF16),
        compiler_params=_cparams(("parallel", "parallel", "arbitrary")),
    )(first, qkv, qa, qkv, ka)


def _rot_half_cols(w):
    D = w.shape[0]
    a = w.reshape(D, -1, HEAD_DIM)
    half = HEAD_DIM // 2
    return jnp.concatenate([-a[..., half:], a[..., :half]], axis=-1).reshape(w.shape)


def _rope_tables(S):
    half = HEAD_DIM // 2
    inv_freq = ROPE_THETA ** (-jnp.arange(half, dtype=F32) / half)
    ang = jnp.arange(S, dtype=F32)[:, None] * inv_freq[None, :]
    c, s = jnp.cos(ang), jnp.sin(ang)
    c2, s2 = jnp.concatenate([c, c], axis=1), jnp.concatenate([s, s], axis=1)
    cos = jnp.stack([jnp.concatenate([c2, jnp.ones_like(c2)], axis=1), jnp.concatenate([c2, c2], axis=1)])
    sin = jnp.stack([jnp.concatenate([s2, jnp.zeros_like(s2)], axis=1), jnp.concatenate([s2, s2], axis=1)])
    return cos, sin


def _compress_kernel(x_ref, pea_ref, peb_ref, wa_ref, wb_ref, w2_ref, o_ref, pa_ref, pb0_ref, *, n_rows):
    u = pl.program_id(1)
    x = x_ref[0].astype(F32)
    pa = jnp.dot((x + pea_ref[...]).astype(BF16), wa_ref[...], preferred_element_type=F32)
    pb = jnp.dot((x + peb_ref[...]).astype(BF16), wb_ref[...], preferred_element_type=F32)

    def emit(slab, hid):
        y = jnp.dot(_gelu_tanh(hid).astype(BF16), w2_ref[...], preferred_element_type=F32)
        o_ref[0, pl.ds(pl.multiple_of(slab * n_rows, n_rows), n_rows), :] = y

    @pl.when(u == 0)
    def _():
        pb0_ref[...] = pb

    @pl.when(u > 0)
    def _():
        emit(u - 1, pa_ref[...] + pb)

    @pl.when(u == 3)
    def _():
        emit(3, pa + pltpu.roll(pb0_ref[...], n_rows - 1, axis=0))

    pa_ref[...] = pa


def nsa_compress(src, pe, w1, w2):
    B, S, W = src.shape
    G = NSA_GROUPS
    n_rows = S // 64
    half = CMP_BLOCK // 2
    cw = half * W
    xv = src.reshape(B, n_rows, 4 * cw)
    pe_flat = jnp.transpose(pe, (1, 0, 2)).reshape(CMP_BLOCK, W).astype(F32)
    pea, peb = pe_flat[:half].reshape(1, cw), pe_flat[half:].reshape(1, cw)
    eye = jnp.eye(G, dtype=F32)
    wfull = jnp.einsum('gldh,gk->lkdgh', w1.astype(F32), eye).reshape(CMP_BLOCK, W, G * CMP_HIDDEN)
    wa = wfull[:half].reshape(cw, G * CMP_HIDDEN).astype(BF16)
    wb = wfull[half:].reshape(cw, G * CMP_HIDDEN).astype(BF16)
    w2bd = jnp.einsum('ghd,gk->ghkd', w2.astype(F32), eye).reshape(G * CMP_HIDDEN, W).astype(BF16)
    const = lambda a: pl.BlockSpec(a.shape, lambda b, u: (0,) * a.ndim)
    return pl.pallas_call(
        functools.partial(_compress_kernel, n_rows=n_rows),
        grid=(B, 4),
        in_specs=[pl.BlockSpec((1, n_rows, cw), lambda b, u: (b, 0, u)),
                  const(pea), const(peb), const(wa), const(wb), const(w2bd)],
        out_specs=pl.BlockSpec((1, 4 * n_rows, W), lambda b, u: (b, 0, 0)),
        out_shape=jax.ShapeDtypeStruct((B, 4 * n_rows, W), F32),
        scratch_shapes=[pltpu.VMEM((n_rows, G * CMP_HIDDEN), F32), pltpu.VMEM((n_rows, G * CMP_HIDDEN), F32)],
        compiler_params=_cparams(("parallel", "arbitrary")),
    )(xv, pea, peb, wa, wb, w2bd)


def _nsa_attn_kernel(q_ref, kvs_ref, kvw_ref, kvc_ref, kvct_ref, gate_ref, bmat_ref, pmat_ref, o_ref,
                     *, tq, tk, seq):
    R = NSA_Q_PER_GROUP
    M = R * tq
    n_slc = seq // SLC_BLOCK
    n_cmp = 4 * n_slc
    n_sel = min(SLC_TOPK, n_slc)
    blocks_per_chunk = tk // SLC_BLOCK
    assert tq & (tq - 1) == 0 and n_slc & (n_slc - 1) == 0 and tk % tq == 0
    log_slc = n_slc.bit_length() - 1
    qi = pl.program_id(2)
    t0 = qi * tq

    qs = jnp.concatenate([q_ref[0, :, r * LANES:(r + 1) * LANES] for r in range(R)], axis=0)

    kvc = kvc_ref[0, 0]
    sT = lax.dot_general(kvc, qs, _NT, preferred_element_type=F32)
    rowc = lax.broadcasted_iota(jnp.int32, (n_cmp, 1), 0)
    cmp_end = (rowc & (n_slc - 1)) * SLC_BLOCK + (rowc >> log_slc) * CMP_STRIDE + (CMP_BLOCK - 1)
    tcol = t0 + (lax.broadcasted_iota(jnp.int32, (1, M), 1) & (tq - 1))
    sm = jnp.where(cmp_end <= tcol, sT, NEG_INF)
    mx = jnp.max(sm, axis=0, keepdims=True)
    e = jnp.exp2(sm - mx)
    inv = jnp.where(mx > 0.5 * NEG_INF, 1.0 / jnp.sum(e, axis=0, keepdims=True), 0.0)
    pT = e * inv
    o_c = jnp.dot(kvct_ref[0, 0], pT.astype(BF16), preferred_element_type=F32).T

    wlen = WINDOW + tq
    start = jnp.maximum(t0 - WINDOW, 0)
    kvw = kvw_ref[0, pl.ds(pl.multiple_of(start, tq), wlen), :]
    s_w = lax.dot_general(qs, kvw, _NT, preferred_element_type=F32)
    qpos = t0 + (lax.broadcasted_iota(jnp.int32, (M, 1), 0) & (tq - 1))
    kpos = start + lax.broadcasted_iota(jnp.int32, (1, wlen), 1)
    s_w = jnp.where(kpos <= qpos, jnp.where(kpos > qpos - WINDOW, s_w, NEG_INF), NEG_INF)
    p_w = jnp.exp2(s_w - jnp.max(s_w, axis=1, keepdims=True)).astype(BF16)
    wlane = lax.broadcasted_iota(jnp.int32, (wlen, LANES), 1)
    acc_w = jnp.dot(p_w, jnp.where(wlane < HEAD_DIM, 1.0, kvw).astype(BF16), preferred_element_type=F32)
    o_w = acc_w / acc_w[:, 0:1]

    psum = pT[:, 0:tq]
    for r in range(1, R):
        psum = psum + pT[:, r * tq:(r + 1) * tq]
    p0, p1, p2, p3 = (psum[u * n_slc:(u + 1) * n_slc, :] for u in range(4))
    jrow = lax.broadcasted_iota(jnp.int32, (n_slc, tq), 0)
    p3_prev = jnp.where(jrow == 0, 0.0, pltpu.roll(p3, 1, axis=0))
    imp = p0 + p1 + p2 + 0.5 * p3 + 0.5 * p3_prev
    cur = (t0 + lax.broadcasted_iota(jnp.int32, (n_slc, tq), 1)) >> (SLC_BLOCK.bit_length() - 1)
    forced = (jrow == 0) | (jrow == cur) | (jrow == cur - 1)
    vals = jnp.where(forced, -jnp.inf, jnp.where(jrow <= cur, imp, -jnp.inf))
    sel0 = jnp.where(forced, 1.0, 0.0)
    n_free = n_sel - 3

    def pick(exact_ties):
        v, sel = vals, sel0
        for _ in range(n_free):
            m = jnp.max(v, axis=0, keepdims=True)
            if exact_ties:
                hit = jrow == jnp.min(jnp.where(v == m, jrow, n_slc), axis=0, keepdims=True)
            else:
                hit = v == jnp.where(m == -jnp.inf, jnp.nan, m)
            sel = jnp.where(hit, 1.0, sel)
            v = jnp.where(hit, -jnp.inf, v)
        return sel

    sel_fast = pick(False)
    n_cand = jnp.sum(jnp.where(vals > -jnp.inf, 1.0, 0.0), axis=0, keepdims=True)
    n_picked = jnp.sum(sel_fast - sel0, axis=0, keepdims=True)
    unique = jnp.min(jnp.where(n_picked == jnp.minimum(n_cand, float(n_free)), 1.0, 0.0)) > 0.5
    sel = lax.cond(unique, lambda: sel_fast, lambda: pick(True))
    sel_bias = ((sel.T - 1.0) * (-NEG_INF)).astype(BF16)

    n_parts = 2
    hp = R // n_parts
    mp = hp * tq
    qparts = [qs[i * mp:(i + 1) * mp] for i in range(n_parts)]
    qrow = t0 + (lax.broadcasted_iota(jnp.int32, (mp, 1), 0) & (tq - 1))
    kcol = lax.broadcasted_iota(jnp.int32, (1, tk), 1)
    n_chunks = seq // tk
    j_last = t0 // tk

    klane = lax.broadcasted_iota(jnp.int32, (tk, LANES), 1)

    def slc_step(j, carry, masked):
        kv = kvs_ref[0, pl.ds(pl.multiple_of(j * tk, tk), tk), :]
        k_sel = jnp.where(klane < HEAD_DIM, kv, bmat_ref[...]).astype(BF16)
        ones_v = jnp.where(klane < HEAD_DIM, 1.0, kv).astype(BF16)
        off = pl.multiple_of(blocks_per_chunk * (n_chunks - 1 - j), blocks_per_chunk)
        place = pmat_ref[pl.ds(off, n_slc), :].astype(BF16)
        q_bias = jnp.dot(sel_bias, place, preferred_element_type=F32).astype(BF16)
        q_bias = jnp.concatenate([q_bias] * hp, axis=0)
        new = []
        for qp, (m, acc) in zip(qparts, carry):
            s = lax.dot_general(qp + q_bias, k_sel, _NT, preferred_element_type=F32)
            if masked:
                s = jnp.where(j * tk + kcol <= qrow, s, NEG_INF)
            m_new = jnp.maximum(m, jnp.max(s, axis=1, keepdims=True))
            p = jnp.exp2(s - m_new).astype(BF16)
            acc = jnp.exp2(m - m_new) * acc + jnp.dot(p, ones_v, preferred_element_type=F32)
            new.append((m_new, acc))
        return tuple(new)

    init = (jnp.full((mp, 1), NEG_INF, F32), jnp.zeros((mp, LANES), F32))
    n_pairs = j_last // 2
    carry = lax.fori_loop(0, n_pairs, lambda i, c: slc_step(2 * i + 1, slc_step(2 * i, c, False), False),
                          (init,) * n_parts)
    carry = lax.fori_loop(2 * n_pairs, j_last, functools.partial(slc_step, masked=False), carry)
    carry = slc_step(j_last, carry, True)
    o_s = jnp.concatenate([acc / acc[:, 0:1] for (_, acc) in carry], axis=0)

    gates = gate_ref[0]
    lane = lax.broadcasted_iota(jnp.int32, (tq, LANES), 1)
    comb = []
    for r in range(R):
        rs = slice(r * tq, (r + 1) * tq)
        comb.append(gates[:, r:r + 1] * o_c[rs] + gates[:, R + r:R + r + 1] * o_s[rs]
                    + gates[:, 2 * R + r:2 * R + r + 1] * o_w[rs])
    out = [jnp.where(lane < HEAD_DIM, pltpu.roll(comb[2 * i], HEAD_DIM, axis=1), comb[2 * i + 1])
           for i in range(R // 2)]
    o_ref[0] = jnp.concatenate(out, axis=1).astype(o_ref.dtype)


def nsa_attention(qkv, kvc, kvct, gates, *, tq=256, tk=1024):
    B, S, _ = qkv.shape
    G, R = NSA_GROUPS, NSA_Q_PER_GROUP
    tk = min(tk, S)
    n_slc = S // SLC_BLOCK
    n_cmp = kvc.shape[2]
    bpc = tk // SLC_BLOCK
    assert bpc <= LANES - HEAD_DIM
    off = bpc * (S // tk - 1)
    lane = np.arange(LANES)[None, :]
    bmat = jnp.asarray(lane - HEAD_DIM == np.arange(tk)[:, None] // SLC_BLOCK, BF16)
    pmat = jnp.asarray((np.arange(n_slc + off)[:, None] - off == lane - HEAD_DIM) & (lane >= HEAD_DIM)
                       & (lane < HEAD_DIM + bpc), F32)
    slc_blk0, win_blk0 = N_HEADS, N_HEADS + G
    return pl.pallas_call(
        functools.partial(_nsa_attn_kernel, tq=tq, tk=tk, seq=S),
        grid=(B, G, S // tq),
        in_specs=[pl.BlockSpec((1, tq, R * LANES), lambda b, g, i: (b, i, g)),
                  pl.BlockSpec((1, S, LANES), lambda b, g, i: (b, 0, slc_blk0 + g)),
                  pl.BlockSpec((1, S, LANES), lambda b, g, i: (b, 0, win_blk0 + g)),
                  pl.BlockSpec((1, 1, n_cmp, LANES), lambda b, g, i: (b, g, 0, 0)),
                  pl.BlockSpec((1, 1, LANES, n_cmp), lambda b, g, i: (b, g, 0, 0)),
                  pl.BlockSpec((1, tq, LANES), lambda b, g, i: (b, i, g)),
                  pl.BlockSpec(bmat.shape, lambda b, g, i: (0, 0)),
                  pl.BlockSpec(pmat.shape, lambda b, g, i: (0, 0))],
        out_specs=pl.BlockSpec((1, tq, R * HEAD_DIM), lambda b, g, i: (b, i, g)),
        out_shape=jax.ShapeDtypeStruct((B, S, ATTN_WIDTH), BF16),
        compiler_params=_cparams(("parallel", "parallel", "arbitrary")),
    )(qkv, qkv, qkv, kvc, kvct, gates, bmat, pmat)


def nsa_layer(h, gamma, w_in, pe_k, w1_k, w2_k, pe_v, w1_v, w2_v, w_out, B, S):
    G, R, hd, aw = NSA_GROUPS, NSA_Q_PER_GROUP, HEAD_DIM, ATTN_WIDTH
    kvd = G * hd
    sec = lambda i: w_in[:, aw + i * kvd: aw + (i + 1) * kvd]
    wq = w_in[:, :aw] * (hd ** -0.5 * LOG2E)
    wa = jnp.concatenate([_head_slots(wq, N_HEADS), _head_slots(sec(2), G, sec(3)),
                          _head_slots(sec(4), G, sec(5))], axis=1).astype(BF16)
    wb = jnp.concatenate([_head_slots(_rot_half_cols(wq), N_HEADS), _head_slots(_rot_half_cols(sec(2)), G),
                          _head_slots(_rot_half_cols(sec(4)), G)], axis=1).astype(BF16)
    cos, sin = _rope_tables(S)
    qkv = norm_matmul(h, gamma, wa, wb=wb, cos=cos, sin=sin)
    kc_src = norm_matmul(h, gamma, sec(0).astype(BF16), wb=_rot_half_cols(sec(0)).astype(BF16),
                         cos=cos, sin=sin, table_of_tile=lambda j: 1)
    vc_src = norm_matmul(h, gamma, sec(1).astype(BF16))
    wg = w_in[:, aw + 6 * kvd:].reshape(-1, 3, G, R)
    wg = jnp.transpose(wg, (0, 2, 1, 3)).reshape(-1, G, 3 * R)
    wg = jnp.pad(wg, ((0, 0), (0, 0), (0, LANES - 3 * R))).reshape(-1, G * LANES).astype(BF16)
    gates = norm_matmul(h, gamma, wg, act="sigmoid", out_dtype=F32)
    kc = nsa_compress(kc_src.reshape(B, S, kvd), pe_k, w1_k, w2_k)
    vc = nsa_compress(vc_src.reshape(B, S, kvd), pe_v, w1_v, w2_v)
    n_cmp = kc.shape[1]
    kvc = jnp.concatenate([kc.reshape(B, n_cmp, G, hd), vc.reshape(B, n_cmp, G, hd)], axis=-1)
    kvc = jnp.transpose(kvc, (0, 2, 1, 3)).astype(BF16)
    kvct = jnp.swapaxes(kvc, 2, 3)
    o = nsa_attention(qkv.reshape(B, S, -1), kvc, kvct, gates.reshape(B, S, G * LANES))
    return matmul_residual(o.reshape(B * S, aw), w_out.astype(BF16), h)


def kernel(x, l0_attn_norm, l0_w_in, l0_cmp_pe_k, l0_cmp_w1_k, l0_cmp_w2_k, l0_cmp_pe_v, l0_cmp_w1_v,
           l0_cmp_w2_v, l0_w_out, l0_ffn_norm, l0_peer_wq, l0_peer_keys, l0_peer_u, l0_peer_v,
           l1_attn_norm, l1_w_in, l1_f_bias, l1_w_out, l1_ffn_norm, l1_peer_wq, l1_peer_keys, l1_peer_u,
           l1_peer_v, final_norm):
    B, S, D = x.shape
    h = x.reshape(B * S, D)
    h = nsa_layer(h, l0_attn_norm, l0_w_in, l0_cmp_pe_k, l0_cmp_w1_k, l0_cmp_w2_k, l0_cmp_pe_v, l0_cmp_w1_v,
                  l0_cmp_w2_v, l0_w_out, B, S)
    h = peer_layer(h, l0_ffn_norm, l0_peer_wq, l0_peer_keys, l0_peer_u, l0_peer_v)
    h = fox_layer(h, l1_attn_norm, l1_w_in, l1_f_bias, l1_w_out, B, S)
    h = peer_layer(h, l1_ffn_norm, l1_peer_wq, l1_peer_keys, l1_peer_u, l1_peer_v)
    return rmsnorm(h, final_norm).reshape(B, S, D)
```

```python
import functools

import numpy as np
import jax
import jax.numpy as jnp
from jax import lax
from jax.experimental import pallas as pl
from jax.experimental.pallas import tpu as pltpu

F32 = jnp.float32
BF16 = jnp.bfloat16

D_MODEL = 1024
N_HEADS = 16
HEAD_DIM = 64
ATTN_WIDTH = N_HEADS * HEAD_DIM
NSA_GROUPS = 4
NSA_Q_PER_GROUP = N_HEADS // NSA_GROUPS
CMP_BLOCK = 32
CMP_STRIDE = 16
CMP_HIDDEN = 2 * HEAD_DIM
SLC_BLOCK = 64
SLC_TOPK = 16
WINDOW = 512
FORCE_SCORE = 1.0e4
ROPE_THETA = 10000.0
PEER_HEADS = 8
PEER_N_KEYS = 128
PEER_TOPK = 16
PEER_HALF_DIM = 128
RMS_EPS = 1e-6
NEG_INF = -1e30
LOG2E = 1.4426950408889634

LANES = 128
VMEM_LIMIT_BYTES = 56 * 1024 * 1024

_NT = (((1,), (1,)), ((), ()))


def _cparams(sem, vmem=VMEM_LIMIT_BYTES, flags=None):
    return pltpu.CompilerParams(dimension_semantics=sem, vmem_limit_bytes=vmem, flags=flags)


def _gelu_tanh(x):
    return 0.5 * x * (1.0 + jnp.tanh(0.7978845608028654 * (x + 0.044715 * (x * x * x))))


def _gelu_sigmoid(x):
    c = -2.0 * 0.7978845608028654 * LOG2E
    t = x * (c + (c * 0.044715) * (x * x))
    return x / (1.0 + jnp.exp2(t))


def _rms_rows(x, g):
    ms = jnp.mean(x * x, axis=-1, keepdims=True)
    return x * lax.rsqrt(ms + RMS_EPS) * g


def _norm_mm_kernel(*refs, act, has_bias, rope, emit_xn):
    it = iter(refs)
    x_ref, g_ref = next(it), next(it)
    wa_ref = next(it)
    wb_ref = next(it) if rope else None
    cos_ref = next(it) if rope else None
    sin_ref = next(it) if rope else None
    b_ref = next(it) if has_bias else None
    o_ref = next(it)
    xo_ref = next(it) if emit_xn else None
    xn_ref = next(it)

    @pl.when(pl.program_id(1) == 0)
    def _():
        xn = _rms_rows(x_ref[...], g_ref[...])
        xn_ref[...] = xn.astype(BF16)
        if emit_xn:
            xo_ref[...] = xn.T.astype(BF16)

    xn = xn_ref[...]
    y = jnp.dot(xn, wa_ref[...], preferred_element_type=F32)
    if rope:
        yb = jnp.dot(xn, wb_ref[...], preferred_element_type=F32)
        cos, sin = cos_ref[...], sin_ref[...]
        for s in range(y.shape[1] // LANES):
            sl = slice(s * LANES, (s + 1) * LANES)
            o_ref[:, sl] = (y[:, sl] * cos + yb[:, sl] * sin).astype(o_ref.dtype)
        return
    if has_bias:
        y = y + b_ref[...]
    if act == "sigmoid":
        y = jax.nn.sigmoid(y)
    elif act == "log_sigmoid":
        y = jax.nn.log_sigmoid(y)
    o_ref[...] = y.astype(o_ref.dtype)


def norm_matmul(x, gamma, wa, *, wb=None, cos=None, sin=None, table_of_tile=None, bias=None,
                act=None, out_dtype=BF16, emit_xn=False, tm=1024, tn=1024):
    T, D = x.shape
    N = wa.shape[1]
    tm, tn = min(tm, T), min(tn, N)
    assert T % tm == 0 and N % tn == 0 and tn % LANES == 0
    rope = wb is not None
    in_specs = [pl.BlockSpec((tm, D), lambda i, j: (i, 0)),
                pl.BlockSpec((1, D), lambda i, j: (0, 0)),
                pl.BlockSpec((D, tn), lambda i, j: (0, j))]
    args = [x, gamma.reshape(1, D).astype(F32), wa]
    if rope:
        S = cos.shape[1]
        assert S % tm == 0
        n_pos = S // tm
        tmap = table_of_tile if table_of_tile is not None else (lambda j: 0)
        in_specs += [pl.BlockSpec((D, tn), lambda i, j: (0, j)),
                     pl.BlockSpec((None, tm, LANES), lambda i, j: (tmap(j), i % n_pos, 0)),
                     pl.BlockSpec((None, tm, LANES), lambda i, j: (tmap(j), i % n_pos, 0))]
        args += [wb, cos, sin]
    if bias is not None:
        in_specs.append(pl.BlockSpec((1, tn), lambda i, j: (0, j)))
        args.append(bias.reshape(1, N).astype(F32))
    out_shape = [jax.ShapeDtypeStruct((T, N), out_dtype)]
    out_specs = [pl.BlockSpec((tm, tn), lambda i, j: (i, j))]
    if emit_xn:
        out_shape.append(jax.ShapeDtypeStruct((D, T), BF16))
        out_specs.append(pl.BlockSpec((D, tm), lambda i, j: (0, i)))
    res = pl.pallas_call(
        functools.partial(_norm_mm_kernel, act=act, has_bias=bias is not None, rope=rope, emit_xn=emit_xn),
        grid=(T // tm, N // tn),
        in_specs=in_specs,
        out_specs=out_specs,
        out_shape=out_shape,
        scratch_shapes=[pltpu.VMEM((tm, D), BF16)],
        compiler_params=_cparams(("parallel", "arbitrary")),
    )(*args)
    return res if emit_xn else res[0]


def _mm_res_kernel(a_ref, w_ref, r_ref, o_ref):
    o_ref[...] = r_ref[...] + jnp.dot(a_ref[...], w_ref[...], preferred_element_type=F32)


def matmul_residual(a, w, res, *, tm=1024, tn=1024):
    T, K = a.shape
    N = w.shape[1]
    tm, tn = min(tm, T), min(tn, N)
    assert T % tm == 0 and N % tn == 0
    return pl.pallas_call(
        _mm_res_kernel,
        grid=(T // tm, N // tn),
        in_specs=[pl.BlockSpec((tm, K), lambda i, j: (i, 0)),
                  pl.BlockSpec((K, tn), lambda i, j: (0, j)),
                  pl.BlockSpec((tm, tn), lambda i, j: (i, j))],
        out_specs=pl.BlockSpec((tm, tn), lambda i, j: (i, j)),
        out_shape=jax.ShapeDtypeStruct((T, N), F32),
        compiler_params=_cparams(("parallel", "arbitrary")),
    )(a, w, res)


def _rmsnorm_kernel(x_ref, g_ref, o_ref):
    o_ref[...] = _rms_rows(x_ref[...], g_ref[...])


def rmsnorm(x, gamma, *, tm=1024):
    T, D = x.shape
    tm = min(tm, T)
    return pl.pallas_call(
        _rmsnorm_kernel,
        grid=(T // tm,),
        in_specs=[pl.BlockSpec((tm, D), lambda i: (i, 0)), pl.BlockSpec((1, D), lambda i: (0, 0))],
        out_specs=pl.BlockSpec((tm, D), lambda i: (i, 0)),
        out_shape=jax.ShapeDtypeStruct((T, D), F32),
        compiler_params=_cparams(("parallel",)),
    )(x, gamma.reshape(1, D).astype(F32))


def _peer_cand_tables(tn):
    fidx, vmask = [], []
    for k2 in range(16):
        fidx.append(k2); vmask.append(0.0)
    for k1 in range(1, 8):
        lim = PEER_TOPK // (k1 + 1)
        for k2 in range(8):
            fidx.append(k1 * 16 + k2); vmask.append(0.0 if k2 < lim else -np.inf)
    for k1 in range(8, 16):
        fidx.append(k1 * 16); vmask.append(0.0)
    fidx = np.broadcast_to(np.asarray(fidx, np.int32)[:, None], (80, tn))
    vmask = np.broadcast_to(np.asarray(vmask, np.float32)[:, None], (80, tn))
    return jnp.asarray(fidx), jnp.asarray(vmask)


def _top16_rows(s, exact_ties):
    n, tn = s.shape
    rows = lax.broadcasted_iota(jnp.int32, (n, tn), 0)
    rows16 = lax.broadcasted_iota(jnp.int32, (PEER_TOPK, tn), 0)
    rank = jnp.full((n, tn), float(PEER_TOPK), F32)
    tops = jnp.zeros((PEER_TOPK, tn), F32)
    v = s
    for k in range(PEER_TOPK):
        m = jnp.max(v, axis=0, keepdims=True)
        if exact_ties:
            hit = rows == jnp.min(jnp.where(v == m, rows, n), axis=0, keepdims=True)
        else:
            hit = v == m
        rank = jnp.where(hit, float(k), rank)
        v = jnp.where(hit, -jnp.inf, v)
        tops = jnp.where(rows16 == k, m, tops)
    return tops, rank, v


def _peer_select_head(q_ref, keys_ref, fidx, vmask, exact_ties):
    tops, ranks, es, picked = [], [], [], []
    for p in range(2):
        q = q_ref[:, p * PEER_HALF_DIM:(p + 1) * PEER_HALF_DIM]
        s = lax.dot_general(keys_ref[p], q, _NT, preferred_element_type=F32)
        t, r, v = _top16_rows(s, exact_ties)
        tops.append(t); ranks.append(r)
        es.append(jnp.exp(s - t[0:1, :]))
        picked.append(jnp.sum(jnp.where(v == -jnp.inf, 1.0, 0.0), axis=0, keepdims=True))
    ts1, ts2 = tops
    pieces = [ts1[0:1, :] + ts2]
    for k1 in range(1, 8):
        pieces.append(ts1[k1:k1 + 1, :] + ts2[0:8, :])
    pieces.append(ts1[8:16, :] + ts2[0:1, :])
    cand0 = jnp.concatenate(pieces, axis=0) + vmask
    cand = cand0
    for _ in range(PEER_TOPK):
        m = jnp.max(cand, axis=0, keepdims=True)
        if exact_ties:
            hit = fidx == jnp.min(jnp.where(cand == m, fidx, 4096), axis=0, keepdims=True)
        else:
            hit = cand == m
        cand = jnp.where(hit, -jnp.inf, cand)
    taken = jnp.logical_and(cand == -jnp.inf, vmask == 0.0)
    takenf = taken.astype(F32)
    picked.append(jnp.sum(takenf, axis=0, keepdims=True))
    unique = jnp.min(jnp.where((picked[0] == PEER_TOPK) & (picked[1] == PEER_TOPK) & (picked[2] == PEER_TOPK),
                               1.0, 0.0)) > 0.5
    best = ts1[0:1, :] + ts2[0:1, :]
    z = jnp.sum(jnp.where(taken, jnp.exp(cand0 - best), 0.0), axis=0, keepdims=True)
    counts = [jnp.sum(takenf[0:16, :], axis=0, keepdims=True)]
    for k1 in range(1, 8):
        counts.append(jnp.sum(takenf[16 + 8 * (k1 - 1):16 + 8 * k1, :], axis=0, keepdims=True))
    tail = takenf[72:80, :]
    cnt = jnp.zeros_like(ranks[0])
    for k1 in range(PEER_TOPK):
        nk = counts[k1] if k1 < 8 else tail[k1 - 8:k1 - 7, :]
        cnt = jnp.where(ranks[0] == float(k1), nk, cnt)
    return (cnt, ranks[1], es[0], es[1] / z), unique


def _peer_select_kernel(q_ref, keys_ref, fidx_ref, vmask_ref, cnt_ref, rank2_ref, e1_ref, e2_ref):
    fidx = fidx_ref[...]
    vmask = vmask_ref[...]

    def store(vals):
        for ref, val in zip((cnt_ref, rank2_ref, e1_ref, e2_ref), vals):
            ref[...] = val.astype(ref.dtype)

    vals, unique = _peer_select_head(q_ref, keys_ref, fidx, vmask, exact_ties=False)
    store(vals)

    @pl.when(jnp.logical_not(unique))
    def _():
        store(_peer_select_head(q_ref, keys_ref, fidx, vmask, exact_ties=True)[0])


def peer_select(q, keys, *, tn=512):
    T = q.shape[0]
    tn = min(tn, T)
    fidx, vmask = _peer_cand_tables(tn)
    rows = PEER_HEADS * PEER_N_KEYS
    ospec = pl.BlockSpec((PEER_N_KEYS, tn), lambda i, h: (h, i))
    return pl.pallas_call(
        _peer_select_kernel,
        grid=(T // tn, PEER_HEADS),
        in_specs=[pl.BlockSpec((tn, 2 * PEER_HALF_DIM), lambda i, h: (i, h)),
                  pl.BlockSpec((2, PEER_N_KEYS, PEER_HALF_DIM), lambda i, h: (h, 0, 0)),
                  pl.BlockSpec((80, tn), lambda i, h: (0, 0)),
                  pl.BlockSpec((80, tn), lambda i, h: (0, 0))],
        out_specs=[ospec] * 4,
        out_shape=[jax.ShapeDtypeStruct((rows, T), dt) for dt in (F32, BF16, F32, BF16)],
        compiler_params=_cparams(("parallel", "parallel")),
    )(q, keys, fidx, vmask)


def _peer_dense_kernel(xn_ref, u_ref, vt_ref, cnt_ref, rank2_ref, e1_ref, e2_ref, res_ref, o_ref,
                       acc_ref, g0_ref, g1_ref, *, c_per_step):
    j = pl.program_id(1)
    n_tiles = pl.num_programs(1) - 1

    @pl.when(j == 0)
    def _():
        acc_ref[...] = jnp.zeros_like(acc_ref)
        g1_ref[...] = jnp.zeros_like(g1_ref)

    @pl.when(j % 2 == 0)
    def _():
        _peer_dense_step(xn_ref, u_ref, vt_ref, cnt_ref, rank2_ref, e1_ref, e2_ref, acc_ref,
                         g1_ref, g0_ref, j, c_per_step)

    @pl.when(j % 2 == 1)
    def _():
        _peer_dense_step(xn_ref, u_ref, vt_ref, cnt_ref, rank2_ref, e1_ref, e2_ref, acc_ref,
                         g0_ref, g1_ref, j, c_per_step)

    @pl.when(j == n_tiles)
    def _():
        o_ref[...] = res_ref[...] + acc_ref[...].T


def _peer_dense_step(xn_ref, u_ref, vt_ref, cnt_ref, rank2_ref, e1_ref, e2_ref, acc_ref, g_ref, g_next_ref,
                     j, c_per_step):

    tn = xn_ref.shape[1]
    bf16_rows = 16
    reps = PEER_N_KEYS // bf16_rows

    def row_tile(ref, row):
        r16 = jnp.broadcast_to(ref[pl.ds(row, 1), :], (bf16_rows, tn)).astype(BF16)
        return jnp.concatenate([r16] * reps, axis=0)

    c0 = jnp.maximum(j - 1, 0) * c_per_step
    up_rows = 2 * PEER_N_KEYS
    blocks = []
    for cc in range(c_per_step):
        if (cc * PEER_N_KEYS) % up_rows == 0:
            rs = slice(cc * PEER_N_KEYS, cc * PEER_N_KEYS + up_rows)
            hT = jnp.dot(u_ref[rs, :], xn_ref[...], preferred_element_type=F32)
            g_next_ref[rs, :] = _gelu_sigmoid(hT.astype(BF16))
        c = c0 + cc
        w = None
        for h in range(PEER_HEADS):
            row = h * PEER_N_KEYS + c
            n_row = row_tile(cnt_ref, row)
            e1_row = row_tile(e1_ref, row)
            sl = slice(h * PEER_N_KEYS, (h + 1) * PEER_N_KEYS)
            term = jnp.where(rank2_ref[sl, :] < n_row, e2_ref[sl, :], 0.0) * e1_row
            w = term if w is None else w + term
        blocks.append(w * g_ref[cc * PEER_N_KEYS:(cc + 1) * PEER_N_KEYS, :])
    aT = jnp.concatenate(blocks, axis=0)
    acc_ref[...] += jnp.dot(vt_ref[...], aT, preferred_element_type=F32)


def peer_dense(xn, u, vt, cnt, rank2, e1, e2, res, *, tn=512, te=1024):
    D, T = xn.shape
    E = u.shape[0]
    tn = min(tn, T)
    rows = PEER_HEADS * PEER_N_KEYS
    sel_spec = pl.BlockSpec((rows, tn), lambda i, j: (0, i))
    n_tiles = E // te
    return pl.pallas_call(
        functools.partial(_peer_dense_kernel, c_per_step=te // PEER_N_KEYS),
        grid=(T // tn, n_tiles + 1),
        in_specs=[pl.BlockSpec((D, tn), lambda i, j: (0, i)),
                  pl.BlockSpec((te, D), lambda i, j: (jnp.minimum(j, n_tiles - 1), 0)),
                  pl.BlockSpec((D, te), lambda i, j: (0, jnp.maximum(j - 1, 0))),
                  sel_spec, sel_spec, sel_spec, sel_spec,
                  pl.BlockSpec((tn, D), lambda i, j: (i, 0))],
        out_specs=pl.BlockSpec((tn, D), lambda i, j: (i, 0)),
        out_shape=jax.ShapeDtypeStruct((T, D), F32),
        scratch_shapes=[pltpu.VMEM((D, tn), F32), pltpu.VMEM((te, tn), BF16), pltpu.VMEM((te, tn), BF16)],
        compiler_params=_cparams(("parallel", "arbitrary")),
    )(xn, u, vt, cnt, rank2, e1, e2, res)


def peer_layer(h, gamma, w_q, sub_keys, u, v):
    q, xn = norm_matmul(h, gamma, w_q.astype(BF16), emit_xn=True)
    keys = sub_keys.reshape(2 * PEER_HEADS, PEER_N_KEYS, PEER_HALF_DIM).astype(BF16)
    cnt, rank2, e1, e2 = peer_select(q, keys)
    return peer_dense(xn, u.astype(BF16), v.T.astype(BF16), cnt, rank2, e1, e2, h)


def _cumsum_aug_kernel(lf_ref, tri_ref, place_q_ref, place_k_ref, ones_q_ref, ones_k_ref,
                       qa_ref, ka_ref, c_ref, carry_ref):
    @pl.when(pl.program_id(1) == 0)
    def _():
        carry_ref[...] = jnp.zeros_like(carry_ref)

    lf = lf_ref[0]
    c = jnp.dot(tri_ref[...], lf, preferred_element_type=F32, precision=lax.Precision.HIGHEST) + carry_ref[...]
    carry_ref[...] = c[-1:, :]
    c = c * LOG2E
    c_ref[0] = c
    hi = c.astype(BF16)
    r1 = c - hi.astype(F32)
    mid = r1.astype(BF16)
    lo = (r1 - mid.astype(F32)).astype(BF16)
    nh = N_HEADS
    lane = lax.broadcasted_iota(jnp.int32, c.shape, 1)
    parts = jnp.where(lane < nh, hi.astype(F32),
                      jnp.where(lane < 2 * nh, pltpu.roll(mid.astype(F32), nh, axis=1),
                                pltpu.roll(lo.astype(F32), 2 * nh, axis=1)))
    parts = jnp.where(lane < 3 * nh, parts, 0.0).astype(BF16)
    qa_ref[0] = (jnp.dot(parts, place_q_ref[...], preferred_element_type=F32) + ones_q_ref[...]).astype(BF16)
    ka_ref[0] = (jnp.dot(parts, place_k_ref[...], preferred_element_type=F32) + ones_k_ref[...]).astype(BF16)


def fox_bias_operands(logf, *, tc=256):
    B, S, _ = logf.shape
    nh = N_HEADS
    tri = jnp.asarray(np.tril(np.ones((tc, tc), np.float32)))
    pq = np.zeros((LANES, nh * LANES), np.float32)
    pk = np.zeros((LANES, nh * LANES), np.float32)
    oq = np.zeros((1, nh * LANES), np.float32)
    ok = np.zeros((1, nh * LANES), np.float32)
    for h in range(nh):
        for part in range(3):
            pq[part * nh + h, h * LANES + part] = 1.0
            pk[part * nh + h, h * LANES + 3 + part] = -1.0
            oq[0, h * LANES + 3 + part] = 1.0
            ok[0, h * LANES + part] = 1.0
    const = lambda a: pl.BlockSpec(a.shape, lambda b, i: (0,) * a.ndim)
    pq, pk, oq, ok = jnp.asarray(pq, BF16), jnp.asarray(pk, BF16), jnp.asarray(oq), jnp.asarray(ok)
    out = jax.ShapeDtypeStruct((B, S, nh * LANES), BF16)
    return pl.pallas_call(
        _cumsum_aug_kernel,
        grid=(B, S // tc),
        in_specs=[pl.BlockSpec((1, tc, LANES), lambda b, i: (b, i, 0)),
                  const(tri), const(pq), const(pk), const(oq), const(ok)],
        out_specs=[pl.BlockSpec((1, tc, nh * LANES), lambda b, i: (b, i, 0))] * 2
        + [pl.BlockSpec((1, tc, LANES), lambda b, i: (b, i, 0))],
        out_shape=[out, out, jax.ShapeDtypeStruct((B, S, LANES), F32)],
        scratch_shapes=[pltpu.VMEM((1, LANES), F32)],
        compiler_params=_cparams(("parallel", "arbitrary")),
    )(logf, tri, pq, pk, oq, ok)


def _fox_attn_kernel(first_ref, q_ref, qa_ref, kv_ref, ka_ref, o_ref, *, tq, tk, heads_per_step):
    qi = pl.program_id(2)
    t0 = qi * tq
    n_full = t0 // tk
    j_first = first_ref[(pl.program_id(0) * pl.num_programs(1) + pl.program_id(1)) * pl.num_programs(2) + qi]
    n_diag = tq // tk
    lanes = [slice(hh * LANES, (hh + 1) * LANES) for hh in range(heads_per_step)]
    qs = [jnp.concatenate([q_ref[0, :, lsl], qa_ref[0, :, lsl]], axis=1) for lsl in lanes]

    def step(j, carry, masked):
        rows = pl.ds(pl.multiple_of(j * tk, tk), tk)
        klane = lax.broadcasted_iota(jnp.int32, (tk, LANES), 1)
        new = []
        for lsl, q, (m, acc) in zip(lanes, qs, carry):
            kv = kv_ref[0, rows, lsl]
            kk = jnp.concatenate([kv, ka_ref[0, rows, lsl]], axis=1)
            ones_v = jnp.where(klane < HEAD_DIM, 1.0, kv).astype(BF16)
            s = lax.dot_general(q, kk, _NT, preferred_element_type=F32)
            if masked:
                qpos = t0 + lax.broadcasted_iota(jnp.int32, (tq, 1), 0)
                kpos = j * tk + lax.broadcasted_iota(jnp.int32, (1, tk), 1)
                s = jnp.where(kpos <= qpos, s, NEG_INF)
            m_new = jnp.maximum(m, jnp.max(s, axis=1, keepdims=True))
            p = jnp.exp2(s - m_new).astype(BF16)
            acc = jnp.exp2(m - m_new) * acc + jnp.dot(p, ones_v, preferred_element_type=F32)
            new.append((m_new, acc))
        return tuple(new)

    init = (jnp.full((tq, 1), NEG_INF, F32), jnp.zeros((tq, LANES), F32))
    n_pairs = (n_full - j_first) // 2
    carry = lax.fori_loop(
        0, n_pairs, lambda i, c: step(j_first + 2 * i + 1, step(j_first + 2 * i, c, False), False),
        (init,) * heads_per_step)
    carry = lax.fori_loop(j_first + 2 * n_pairs, n_full, functools.partial(step, masked=False), carry)
    for d in range(n_diag):
        carry = step(n_full + d, carry, True)
    outs = [acc / acc[:, 0:1] for (_, acc) in carry]
    lane = lax.broadcasted_iota(jnp.int32, (tq, LANES), 1)
    blocks = []
    for pair in range(heads_per_step // 2):
        a, b = outs[2 * pair], outs[2 * pair + 1]
        blocks.append(jnp.where(lane < HEAD_DIM, pltpu.roll(a, HEAD_DIM, axis=1), b))
    o_ref[0] = jnp.concatenate(blocks, axis=1).astype(o_ref.dtype) if len(blocks) > 1 else blocks[0].astype(o_ref.dtype)


def _head_slots(w, n_heads, second=None):
    D = w.shape[0]
    a = w.reshape(D, n_heads, HEAD_DIM)
    b = jnp.zeros_like(a) if second is None else second.reshape(D, n_heads, HEAD_DIM)
    return jnp.concatenate([a, b], axis=-1).reshape(D, n_heads * LANES)


def fox_layer(h, gamma, w_in, f_bias, w_out, B, S):
    aw = ATTN_WIDTH
    wq = _head_slots(w_in[:, :aw] * (HEAD_DIM ** -0.5 * LOG2E), N_HEADS)
    wkv = _head_slots(w_in[:, aw:2 * aw], N_HEADS, w_in[:, 2 * aw:3 * aw])
    w_main = jnp.concatenate([wq, wkv], axis=1).astype(BF16)
    wf = jnp.pad(w_in[:, 3 * aw:], ((0, 0), (0, LANES - N_HEADS))).astype(BF16)
    bf = jnp.pad(f_bias.astype(F32), (0, LANES - N_HEADS))
    qkv = norm_matmul(h, gamma, w_main)
    logf = norm_matmul(h, gamma, wf, bias=bf, act="log_sigmoid", out_dtype=F32)
    qa, ka, c2 = fox_bias_operands(logf.reshape(B, S, LANES))
    qkv = qkv.reshape(B, S, 2 * N_HEADS * LANES)
    o = fox_attention(qkv, qa, ka, c2)
    return matmul_residual(o.reshape(B * S, aw), w_out.astype(BF16), h)


FOX_NEGLIGIBLE_LOG2 = 200.0


def _norm_maxima_kernel(x_ref, ind_ref, o_ref):
    x = x_ref[0].astype(F32)
    ss = jnp.dot((x * x).astype(BF16), ind_ref[...], preferred_element_type=F32)
    o_ref[0, 0] = jnp.broadcast_to(jnp.max(ss, axis=0, keepdims=True), o_ref.shape[2:])


def fox_norm_maxima(qkv, tile):
    B, S, W = qkv.shape
    n = S // tile
    col = np.arange(W)
    slot, lane = col // LANES, col % LANES
    used = (slot < N_HEADS) | (lane < HEAD_DIM)
    ind = jnp.asarray((slot[:, None] == np.arange(LANES)[None, :]) & used[:, None], BF16)
    rows = min(256, tile)
    ss = pl.pallas_call(
        _norm_maxima_kernel,
        grid=(B, S // rows),
        in_specs=[pl.BlockSpec((1, rows, W), lambda b, i: (b, i, 0)),
                  pl.BlockSpec(ind.shape, lambda b, i: (0, 0))],
        out_specs=pl.BlockSpec((1, 1, 8, LANES), lambda b, i: (b, i, 0, 0)),
        out_shape=jax.ShapeDtypeStruct((B, S // rows, 8, LANES), F32),
        compiler_params=_cparams(("parallel", "parallel")),
    )(qkv, ind)
    ss = jnp.max(ss[:, :, 0, :2 * N_HEADS].reshape(B, n, tile // rows, 2 * N_HEADS), axis=2)
    return jnp.sqrt(ss * 1.01)


def fox_first_chunk(qkv, c2, tile, heads_per_step):
    B, S, _ = qkv.shape
    n = S // tile
    norms = fox_norm_maxima(qkv, tile)
    qmax, kmax = norms[..., :N_HEADS], norms[..., N_HEADS:]
    c = c2[..., :N_HEADS].reshape(B, n, tile, N_HEADS)
    cmax, cmin = jnp.max(c, axis=2), jnp.min(c, axis=2)
    upper = qmax[:, :, None] * kmax[:, None, :] + cmax[:, :, None] - cmin[:, None, :]
    own = -(qmax * kmax)
    earlier = jnp.arange(n)[None, :, None, None] > jnp.arange(n)[None, None, :, None]
    skip = (upper < own[:, :, None] - FOX_NEGLIGIBLE_LOG2) & earlier
    first = jnp.sum(jnp.cumprod(skip.astype(jnp.int32), axis=2), axis=2)
    first = jnp.min(first.reshape(B, n, N_HEADS // heads_per_step, heads_per_step), axis=-1)
    return jnp.transpose(first, (0, 2, 1)).reshape(-1).astype(jnp.int32)


def fox_attention(qkv, qa, ka, c2, *, tq=1024, tk=1024, heads_per_step=2):
    B, S, _ = qkv.shape
    tq = tk = min(tq, S)
    hs = heads_per_step
    wq = hs * LANES
    n_qblk = N_HEADS // hs
    first = fox_first_chunk(qkv, c2, tk, hs)
    grid_spec = pltpu.PrefetchScalarGridSpec(
        num_scalar_prefetch=1,
        grid=(B, n_qblk, S // tq),
        in_specs=[pl.BlockSpec((1, tq, wq), lambda b, h, i, first: (b, i, h)),
                  pl.BlockSpec((1, tq, wq), lambda b, h, i, first: (b, i, h)),
                  pl.BlockSpec((1, S, wq), lambda b, h, i, first: (b, 0, n_qblk + h)),
                  pl.BlockSpec((1, S, wq), lambda b, h, i, first: (b, 0, h))],
        out_specs=pl.BlockSpec((1, tq, hs * HEAD_DIM), lambda b, h, i, first: (b, i, h)))
    return pl.pallas_call(
        functools.partial(_fox_attn_kernel, tq=tq, tk=tk, heads_per_step=hs),
        grid_spec=grid_spec,
        out_shape=jax.ShapeDtypeStruct((B, S, ATTN_WIDTH), BF16),
        compiler_params=_cparams(("parallel", "parallel", "arbitrary")),
    )(first, qkv, qa, qkv, ka)


def _rot_half_cols(w):
    D = w.shape[0]
    a = w.reshape(D, -1, HEAD_DIM)
    half = HEAD_DIM // 2
    return jnp.concatenate([-a[..., half:], a[..., :half]], axis=-1).reshape(w.shape)


def _rope_tables(S):
    half = HEAD_DIM // 2
    inv_freq = ROPE_THETA ** (-jnp.arange(half, dtype=F32) / half)
    ang = jnp.arange(S, dtype=F32)[:, None] * inv_freq[None, :]
    c, s = jnp.cos(ang), jnp.sin(ang)
    c2, s2 = jnp.concatenate([c, c], axis=1), jnp.concatenate([s, s], axis=1)
    cos = jnp.stack([jnp.concatenate([c2, jnp.ones_like(c2)], axis=1), jnp.concatenate([c2, c2], axis=1)])
    sin = jnp.stack([jnp.concatenate([s2, jnp.zeros_like(s2)], axis=1), jnp.concatenate([s2, s2], axis=1)])
    return cos, sin


def _compress_kernel(x_ref, pea_ref, peb_ref, wa_ref, wb_ref, w2_ref, o_ref, pa_ref, pb0_ref, *, n_rows):
    u = pl.program_id(1)
    x = x_ref[0].astype(F32)
    pa = jnp.dot((x + pea_ref[...]).astype(BF16), wa_ref[...], preferred_element_type=F32)
    pb = jnp.dot((x + peb_ref[...]).astype(BF16), wb_ref[...], preferred_element_type=F32)

    def emit(slab, hid):
        y = jnp.dot(_gelu_tanh(hid).astype(BF16), w2_ref[...], preferred_element_type=F32)
        o_ref[0, pl.ds(pl.multiple_of(slab * n_rows, n_rows), n_rows), :] = y

    @pl.when(u == 0)
    def _():
        pb0_ref[...] = pb

    @pl.when(u > 0)
    def _():
        emit(u - 1, pa_ref[...] + pb)

    @pl.when(u == 3)
    def _():
        emit(3, pa + pltpu.roll(pb0_ref[...], n_rows - 1, axis=0))

    pa_ref[...] = pa


def nsa_compress(src, pe, w1, w2):
    B, S, W = src.shape
    G = NSA_GROUPS
    n_rows = S // 64
    half = CMP_BLOCK // 2
    cw = half * W
    xv = src.reshape(B, n_rows, 4 * cw)
    pe_flat = jnp.transpose(pe, (1, 0, 2)).reshape(CMP_BLOCK, W).astype(F32)
    pea, peb = pe_flat[:half].reshape(1, cw), pe_flat[half:].reshape(1, cw)
    eye = jnp.eye(G, dtype=F32)
    wfull = jnp.einsum('gldh,gk->lkdgh', w1.astype(F32), eye).reshape(CMP_BLOCK, W, G * CMP_HIDDEN)
    wa = wfull[:half].reshape(cw, G * CMP_HIDDEN).astype(BF16)
    wb = wfull[half:].reshape(cw, G * CMP_HIDDEN).astype(BF16)
    w2bd = jnp.einsum('ghd,gk->ghkd', w2.astype(F32), eye).reshape(G * CMP_HIDDEN, W).astype(BF16)
    const = lambda a: pl.BlockSpec(a.shape, lambda b, u: (0,) * a.ndim)
    return pl.pallas_call(
        functools.partial(_compress_kernel, n_rows=n_rows),
        grid=(B, 4),
        in_specs=[pl.BlockSpec((1, n_rows, cw), lambda b, u: (b, 0, u)),
                  const(pea), const(peb), const(wa), const(wb), const(w2bd)],
        out_specs=pl.BlockSpec((1, 4 * n_rows, W), lambda b, u: (b, 0, 0)),
        out_shape=jax.ShapeDtypeStruct((B, 4 * n_rows, W), F32),
        scratch_shapes=[pltpu.VMEM((n_rows, G * CMP_HIDDEN), F32), pltpu.VMEM((n_rows, G * CMP_HIDDEN), F32)],
        compiler_params=_cparams(("parallel", "arbitrary")),
    )(xv, pea, peb, wa, wb, w2bd)


def _nsa_attn_kernel(q_ref, kvs_ref, kvw_ref, kvc_ref, kvct_ref, gate_ref, bmat_ref, pmat_ref, o_ref,
                     *, tq, ts, tk, seq):
    R = NSA_Q_PER_GROUP
    n_slc = seq // SLC_BLOCK
    n_sel = min(SLC_TOPK, n_slc)
    assert ts & (ts - 1) == 0 and n_slc & (n_slc - 1) == 0 and tk % tq == 0 and tq % ts == 0
    log_slc = n_slc.bit_length() - 1
    qi = pl.program_id(2)
    t0 = qi * tq

    local = [_nsa_local_branches(q_ref, kvw_ref, kvc_ref, kvct_ref, t0 + i * ts, i * ts, ts, n_slc, n_sel, log_slc)
             for i in range(tq // ts)]
    o_c = jnp.concatenate([o[0][r * ts:(r + 1) * ts] for r in range(R) for o in local], axis=0)
    o_w = jnp.concatenate([o[1][r * ts:(r + 1) * ts] for r in range(R) for o in local], axis=0)
    sel_bias = jnp.concatenate([o[2] for o in local], axis=0)
    qs = jnp.concatenate([q_ref[0, :, r * LANES:(r + 1) * LANES] for r in range(R)], axis=0)
    _nsa_selected_and_combine(qs, o_c, o_w, sel_bias, kvs_ref, gate_ref, bmat_ref, pmat_ref, o_ref,
                              t0=t0, tq=tq, tk=tk, seq=seq)


def _nsa_local_branches(q_ref, kvw_ref, kvc_ref, kvct_ref, t0, row0, tq, n_slc, n_sel, log_slc):
    R = NSA_Q_PER_GROUP
    M = R * tq
    n_cmp = 4 * n_slc
    qs = jnp.concatenate([q_ref[0, row0:row0 + tq, r * LANES:(r + 1) * LANES] for r in range(R)], axis=0)

    kvc = kvc_ref[0, 0]
    sT = lax.dot_general(kvc, qs, _NT, preferred_element_type=F32)
    rowc = lax.broadcasted_iota(jnp.int32, (n_cmp, 1), 0)
    cmp_end = (rowc & (n_slc - 1)) * SLC_BLOCK + (rowc >> log_slc) * CMP_STRIDE + (CMP_BLOCK - 1)
    tcol = t0 + (lax.broadcasted_iota(jnp.int32, (1, M), 1) & (tq - 1))
    sm = jnp.where(cmp_end <= tcol, sT, NEG_INF)
    mx = jnp.max(sm, axis=0, keepdims=True)
    e = jnp.exp2(sm - mx)
    inv = jnp.where(mx > 0.5 * NEG_INF, 1.0 / jnp.sum(e, axis=0, keepdims=True), 0.0)
    pT = e * inv
    o_c = jnp.dot(kvct_ref[0, 0], pT.astype(BF16), preferred_element_type=F32).T

    wlen = WINDOW + tq
    start = jnp.maximum(t0 - WINDOW, 0)
    kvw = kvw_ref[0, pl.ds(pl.multiple_of(start, tq), wlen), :]
    s_w = lax.dot_general(qs, kvw, _NT, preferred_element_type=F32)
    qpos = t0 + (lax.broadcasted_iota(jnp.int32, (M, 1), 0) & (tq - 1))
    kpos = start + lax.broadcasted_iota(jnp.int32, (1, wlen), 1)
    s_w = jnp.where(kpos <= qpos, jnp.where(kpos > qpos - WINDOW, s_w, NEG_INF), NEG_INF)
    p_w = jnp.exp2(s_w - jnp.max(s_w, axis=1, keepdims=True)).astype(BF16)
    wlane = lax.broadcasted_iota(jnp.int32, (wlen, LANES), 1)
    acc_w = jnp.dot(p_w, jnp.where(wlane < HEAD_DIM, 1.0, kvw).astype(BF16), preferred_element_type=F32)
    o_w = acc_w / acc_w[:, 0:1]

    psum = pT[:, 0:tq]
    for r in range(1, R):
        psum = psum + pT[:, r * tq:(r + 1) * tq]
    p0, p1, p2, p3 = (psum[u * n_slc:(u + 1) * n_slc, :] for u in range(4))
    jrow = lax.broadcasted_iota(jnp.int32, (n_slc, tq), 0)
    p3_prev = jnp.where(jrow == 0, 0.0, pltpu.roll(p3, 1, axis=0))
    imp = p0 + p1 + p2 + 0.5 * p3 + 0.5 * p3_prev
    cur = (t0 + lax.broadcasted_iota(jnp.int32, (n_slc, tq), 1)) >> (SLC_BLOCK.bit_length() - 1)
    forced = (jrow == 0) | (jrow == cur) | (jrow == cur - 1)
    vals = jnp.where(forced, -jnp.inf, jnp.where(jrow <= cur, imp, -jnp.inf))
    sel0 = jnp.where(forced, 1.0, 0.0)
    n_free = n_sel - 3

    def pick(exact_ties):
        v, sel = vals, sel0
        for _ in range(n_free):
            m = jnp.max(v, axis=0, keepdims=True)
            if exact_ties:
                hit = jrow == jnp.min(jnp.where(v == m, jrow, n_slc), axis=0, keepdims=True)
            else:
                hit = v == jnp.where(m == -jnp.inf, jnp.nan, m)
            sel = jnp.where(hit, 1.0, sel)
            v = jnp.where(hit, -jnp.inf, v)
        return sel

    sel_fast = pick(False)
    n_cand = jnp.sum(jnp.where(vals > -jnp.inf, 1.0, 0.0), axis=0, keepdims=True)
    n_picked = jnp.sum(sel_fast - sel0, axis=0, keepdims=True)
    unique = jnp.min(jnp.where(n_picked == jnp.minimum(n_cand, float(n_free)), 1.0, 0.0)) > 0.5
    sel = lax.cond(unique, lambda: sel_fast, lambda: pick(True))
    sel_bias = ((sel.T - 1.0) * (-NEG_INF)).astype(BF16)
    return o_c, o_w, sel_bias


def _nsa_selected_and_combine(qs, o_c, o_w, sel_bias, kvs_ref, gate_ref, bmat_ref, pmat_ref, o_ref,
                              *, t0, tq, tk, seq):
    R = NSA_Q_PER_GROUP
    n_slc = seq // SLC_BLOCK
    blocks_per_chunk = tk // SLC_BLOCK
    n_parts = 2
    hp = R // n_parts
    mp = hp * tq
    qparts = [qs[i * mp:(i + 1) * mp] for i in range(n_parts)]
    qrow = t0 + (lax.broadcasted_iota(jnp.int32, (mp, 1), 0) & (tq - 1))
    kcol = lax.broadcasted_iota(jnp.int32, (1, tk), 1)
    n_chunks = seq // tk
    j_last = t0 // tk

    klane = lax.broadcasted_iota(jnp.int32, (tk, LANES), 1)

    def slc_step(j, carry, masked):
        kv = kvs_ref[0, pl.ds(pl.multiple_of(j * tk, tk), tk), :]
        k_sel = jnp.where(klane < HEAD_DIM, kv, bmat_ref[...]).astype(BF16)
        ones_v = jnp.where(klane < HEAD_DIM, 1.0, kv).astype(BF16)
        off = pl.multiple_of(blocks_per_chunk * (n_chunks - 1 - j), blocks_per_chunk)
        place = pmat_ref[pl.ds(off, n_slc), :].astype(BF16)
        q_bias = jnp.dot(sel_bias, place, preferred_element_type=F32).astype(BF16)
        q_bias = jnp.concatenate([q_bias] * hp, axis=0)
        new = []
        for qp, (m, acc) in zip(qparts, carry):
            s = lax.dot_general(qp + q_bias, k_sel, _NT, preferred_element_type=F32)
            if masked:
                s = jnp.where(j * tk + kcol <= qrow, s, NEG_INF)
            m_new = jnp.maximum(m, jnp.max(s, axis=1, keepdims=True))
            p = jnp.exp2(s - m_new).astype(BF16)
            acc = jnp.exp2(m - m_new) * acc + jnp.dot(p, ones_v, preferred_element_type=F32)
            new.append((m_new, acc))
        return tuple(new)

    init = (jnp.full((mp, 1), NEG_INF, F32), jnp.zeros((mp, LANES), F32))
    n_pairs = j_last // 2
    carry = lax.fori_loop(0, n_pairs, lambda i, c: slc_step(2 * i + 1, slc_step(2 * i, c, False), False),
                          (init,) * n_parts)
    carry = lax.fori_loop(2 * n_pairs, j_last, functools.partial(slc_step, masked=False), carry)
    carry = slc_step(j_last, carry, True)
    o_s = jnp.concatenate([acc / acc[:, 0:1] for (_, acc) in carry], axis=0)

    gates = gate_ref[0]
    lane = lax.broadcasted_iota(jnp.int32, (tq, LANES), 1)
    comb = []
    for r in range(R):
        rs = slice(r * tq, (r + 1) * tq)
        comb.append(gates[:, r:r + 1] * o_c[rs] + gates[:, R + r:R + r + 1] * o_s[rs]
                    + gates[:, 2 * R + r:2 * R + r + 1] * o_w[rs])
    out = [jnp.where(lane < HEAD_DIM, pltpu.roll(comb[2 * i], HEAD_DIM, axis=1), comb[2 * i + 1])
           for i in range(R // 2)]
    o_ref[0] = jnp.concatenate(out, axis=1).astype(o_ref.dtype)


def nsa_attention(qkv, kvc, kvct, gates, *, tq=512, ts=256, tk=1024):
    B, S, _ = qkv.shape
    G, R = NSA_GROUPS, NSA_Q_PER_GROUP
    tk = min(tk, S)
    n_slc = S // SLC_BLOCK
    n_cmp = kvc.shape[2]
    bpc = tk // SLC_BLOCK
    assert bpc <= LANES - HEAD_DIM
    off = bpc * (S // tk - 1)
    lane = np.arange(LANES)[None, :]
    bmat = jnp.asarray(lane - HEAD_DIM == np.arange(tk)[:, None] // SLC_BLOCK, BF16)
    pmat = jnp.asarray((np.arange(n_slc + off)[:, None] - off == lane - HEAD_DIM) & (lane >= HEAD_DIM)
                       & (lane < HEAD_DIM + bpc), F32)
    slc_blk0, win_blk0 = N_HEADS, N_HEADS + G
    return pl.pallas_call(
        functools.partial(_nsa_attn_kernel, tq=tq, ts=ts, tk=tk, seq=S),
        grid=(B, G, S // tq),
        in_specs=[pl.BlockSpec((1, tq, R * LANES), lambda b, g, i: (b, i, g)),
                  pl.BlockSpec((1, S, LANES), lambda b, g, i: (b, 0, slc_blk0 + g)),
                  pl.BlockSpec((1, S, LANES), lambda b, g, i: (b, 0, win_blk0 + g)),
                  pl.BlockSpec((1, 1, n_cmp, LANES), lambda b, g, i: (b, g, 0, 0)),
                  pl.BlockSpec((1, 1, LANES, n_cmp), lambda b, g, i: (b, g, 0, 0)),
                  pl.BlockSpec((1, tq, LANES), lambda b, g, i: (b, i, g)),
                  pl.BlockSpec(bmat.shape, lambda b, g, i: (0, 0)),
                  pl.BlockSpec(pmat.shape, lambda b, g, i: (0, 0))],
        out_specs=pl.BlockSpec((1, tq, R * HEAD_DIM), lambda b, g, i: (b, i, g)),
        out_shape=jax.ShapeDtypeStruct((B, S, ATTN_WIDTH), BF16),
        compiler_params=_cparams(("parallel", "parallel", "arbitrary")),
    )(qkv, qkv, qkv, kvc, kvct, gates, bmat, pmat)


def nsa_layer(h, gamma, w_in, pe_k, w1_k, w2_k, pe_v, w1_v, w2_v, w_out, B, S):
    G, R, hd, aw = NSA_GROUPS, NSA_Q_PER_GROUP, HEAD_DIM, ATTN_WIDTH
    kvd = G * hd
    sec = lambda i: w_in[:, aw + i * kvd: aw + (i + 1) * kvd]
    wq = w_in[:, :aw] * (hd ** -0.5 * LOG2E)
    wa = jnp.concatenate([_head_slots(wq, N_HEADS), _head_slots(sec(2), G, sec(3)),
                          _head_slots(sec(4), G, sec(5))], axis=1).astype(BF16)
    wb = jnp.concatenate([_head_slots(_rot_half_cols(wq), N_HEADS), _head_slots(_rot_half_cols(sec(2)), G),
                          _head_slots(_rot_half_cols(sec(4)), G)], axis=1).astype(BF16)
    cos, sin = _rope_tables(S)
    qkv = norm_matmul(h, gamma, wa, wb=wb, cos=cos, sin=sin)
    kc_src = norm_matmul(h, gamma, sec(0).astype(BF16), wb=_rot_half_cols(sec(0)).astype(BF16),
                         cos=cos, sin=sin, table_of_tile=lambda j: 1)
    vc_src = norm_matmul(h, gamma, sec(1).astype(BF16))
    wg = w_in[:, aw + 6 * kvd:].reshape(-1, 3, G, R)
    wg = jnp.transpose(wg, (0, 2, 1, 3)).reshape(-1, G, 3 * R)
    wg = jnp.pad(wg, ((0, 0), (0, 0), (0, LANES - 3 * R))).reshape(-1, G * LANES).astype(BF16)
    gates = norm_matmul(h, gamma, wg, act="sigmoid", out_dtype=F32)
    kc = nsa_compress(kc_src.reshape(B, S, kvd), pe_k, w1_k, w2_k)
    vc = nsa_compress(vc_src.reshape(B, S, kvd), pe_v, w1_v, w2_v)
    n_cmp = kc.shape[1]
    kvc = jnp.concatenate([kc.reshape(B, n_cmp, G, hd), vc.reshape(B, n_cmp, G, hd)], axis=-1)
    kvc = jnp.transpose(kvc, (0, 2, 1, 3)).astype(BF16)
    kvct = jnp.swapaxes(kvc, 2, 3)
    o = nsa_attention(qkv.reshape(B, S, -1), kvc, kvct, gates.reshape(B, S, G * LANES))
    return matmul_residual(o.reshape(B * S, aw), w_out.astype(BF16), h)


def kernel(x, l0_attn_norm, l0_w_in, l0_cmp_pe_k, l0_cmp_w1_k, l0_cmp_w2_k, l0_cmp_pe_v, l0_cmp_w1_v,
           l0_cmp_w2_v, l0_w_out, l0_ffn_norm, l0_peer_wq, l0_peer_keys, l0_peer_u, l0_peer_v,
           l1_attn_norm, l1_w_in, l1_f_bias, l1_w_out, l1_ffn_norm, l1_peer_wq, l1_peer_keys, l1_peer_u,
           l1_peer_v, final_norm):
    B, S, D = x.shape
    h = x.reshape(B * S, D)
    h = nsa_layer(h, l0_attn_norm, l0_w_in, l0_cmp_pe_k, l0_cmp_w1_k, l0_cmp_w2_k, l0_cmp_pe_v, l0_cmp_w1_v,
                  l0_cmp_w2_v, l0_w_out, B, S)
    h = peer_layer(h, l0_ffn_norm, l0_peer_wq, l0_peer_keys, l0_peer_u, l0_peer_v)
    h = fox_layer(h, l1_attn_norm, l1_w_in, l1_f_bias, l1_w_out, B, S)
    h = peer_layer(h, l1_ffn_norm, l1_peer_wq, l1_peer_keys, l1_peer_u, l1_peer_v)
    return rmsnorm(h, final_norm).reshape(B, S, D)
```

```python
import functools

import numpy as np
import jax
import jax.numpy as jnp
from jax import lax
from jax.experimental import pallas as pl
from jax.experimental.pallas import tpu as pltpu

F32 = jnp.float32
BF16 = jnp.bfloat16

D_MODEL = 1024
N_HEADS = 16
HEAD_DIM = 64
ATTN_WIDTH = N_HEADS * HEAD_DIM
NSA_GROUPS = 4
NSA_Q_PER_GROUP = N_HEADS // NSA_GROUPS
CMP_BLOCK = 32
CMP_STRIDE = 16
CMP_HIDDEN = 2 * HEAD_DIM
SLC_BLOCK = 64
SLC_TOPK = 16
WINDOW = 512
FORCE_SCORE = 1.0e4
ROPE_THETA = 10000.0
PEER_HEADS = 8
PEER_N_KEYS = 128
PEER_TOPK = 16
PEER_HALF_DIM = 128
RMS_EPS = 1e-6
NEG_INF = -1e30
LOG2E = 1.4426950408889634

LANES = 128
VMEM_LIMIT_BYTES = 56 * 1024 * 1024

_NT = (((1,), (1,)), ((), ()))


def _cparams(sem, vmem=VMEM_LIMIT_BYTES, flags=None):
    return pltpu.CompilerParams(dimension_semantics=sem, vmem_limit_bytes=vmem, flags=flags)


def _gelu_tanh(x):
    return 0.5 * x * (1.0 + jnp.tanh(0.7978845608028654 * (x + 0.044715 * (x * x * x))))


def _gelu_sigmoid(x):
    c = -2.0 * 0.7978845608028654 * LOG2E
    t = x * (c + (c * 0.044715) * (x * x))
    return x / (1.0 + jnp.exp2(t))


def _rms_rows(x, g):
    ms = jnp.mean(x * x, axis=-1, keepdims=True)
    return x * lax.rsqrt(ms + RMS_EPS) * g


def _norm_mm_kernel(*refs, act, has_bias, rope, emit_xn):
    it = iter(refs)
    x_ref, g_ref = next(it), next(it)
    wa_ref = next(it)
    wb_ref = next(it) if rope else None
    cos_ref = next(it) if rope else None
    sin_ref = next(it) if rope else None
    b_ref = next(it) if has_bias else None
    o_ref = next(it)
    xo_ref = next(it) if emit_xn else None
    xn_ref = next(it)

    @pl.when(pl.program_id(1) == 0)
    def _():
        xn = _rms_rows(x_ref[...], g_ref[...])
        xn_ref[...] = xn.astype(BF16)
        if emit_xn:
            xo_ref[...] = xn.T.astype(BF16)

    xn = xn_ref[...]
    y = jnp.dot(xn, wa_ref[...], preferred_element_type=F32)
    if rope:
        yb = jnp.dot(xn, wb_ref[...], preferred_element_type=F32)
        cos, sin = cos_ref[...], sin_ref[...]
        for s in range(y.shape[1] // LANES):
            sl = slice(s * LANES, (s + 1) * LANES)
            o_ref[:, sl] = (y[:, sl] * cos + yb[:, sl] * sin).astype(o_ref.dtype)
        return
    if has_bias:
        y = y + b_ref[...]
    if act == "sigmoid":
        y = jax.nn.sigmoid(y)
    elif act == "log_sigmoid":
        y = jax.nn.log_sigmoid(y)
    o_ref[...] = y.astype(o_ref.dtype)


def norm_matmul(x, gamma, wa, *, wb=None, cos=None, sin=None, table_of_tile=None, bias=None,
                act=None, out_dtype=BF16, emit_xn=False, tm=1024, tn=1024):
    T, D = x.shape
    N = wa.shape[1]
    tm, tn = min(tm, T), min(tn, N)
    assert T % tm == 0 and N % tn == 0 and tn % LANES == 0
    rope = wb is not None
    in_specs = [pl.BlockSpec((tm, D), lambda i, j: (i, 0)),
                pl.BlockSpec((1, D), lambda i, j: (0, 0)),
                pl.BlockSpec((D, tn), lambda i, j: (0, j))]
    args = [x, gamma.reshape(1, D).astype(F32), wa]
    if rope:
        S = cos.shape[1]
        assert S % tm == 0
        n_pos = S // tm
        tmap = table_of_tile if table_of_tile is not None else (lambda j: 0)
        in_specs += [pl.BlockSpec((D, tn), lambda i, j: (0, j)),
                     pl.BlockSpec((None, tm, LANES), lambda i, j: (tmap(j), i % n_pos, 0)),
                     pl.BlockSpec((None, tm, LANES), lambda i, j: (tmap(j), i % n_pos, 0))]
        args += [wb, cos, sin]
    if bias is not None:
        in_specs.append(pl.BlockSpec((1, tn), lambda i, j: (0, j)))
        args.append(bias.reshape(1, N).astype(F32))
    out_shape = [jax.ShapeDtypeStruct((T, N), out_dtype)]
    out_specs = [pl.BlockSpec((tm, tn), lambda i, j: (i, j))]
    if emit_xn:
        out_shape.append(jax.ShapeDtypeStruct((D, T), BF16))
        out_specs.append(pl.BlockSpec((D, tm), lambda i, j: (0, i)))
    res = pl.pallas_call(
        functools.partial(_norm_mm_kernel, act=act, has_bias=bias is not None, rope=rope, emit_xn=emit_xn),
        grid=(T // tm, N // tn),
        in_specs=in_specs,
        out_specs=out_specs,
        out_shape=out_shape,
        scratch_shapes=[pltpu.VMEM((tm, D), BF16)],
        compiler_params=_cparams(("parallel", "arbitrary")),
    )(*args)
    return res if emit_xn else res[0]


def _mm_res_kernel(a_ref, w_ref, r_ref, o_ref):
    o_ref[...] = r_ref[...] + jnp.dot(a_ref[...], w_ref[...], preferred_element_type=F32)


def matmul_residual(a, w, res, *, tm=1024, tn=1024):
    T, K = a.shape
    N = w.shape[1]
    tm, tn = min(tm, T), min(tn, N)
    assert T % tm == 0 and N % tn == 0
    return pl.pallas_call(
        _mm_res_kernel,
        grid=(T // tm, N // tn),
        in_specs=[pl.BlockSpec((tm, K), lambda i, j: (i, 0)),
                  pl.BlockSpec((K, tn), lambda i, j: (0, j)),
                  pl.BlockSpec((tm, tn), lambda i, j: (i, j))],
        out_specs=pl.BlockSpec((tm, tn), lambda i, j: (i, j)),
        out_shape=jax.ShapeDtypeStruct((T, N), F32),
        compiler_params=_cparams(("parallel", "arbitrary")),
    )(a, w, res)


def _rmsnorm_kernel(x_ref, g_ref, o_ref):
    o_ref[...] = _rms_rows(x_ref[...], g_ref[...])


def rmsnorm(x, gamma, *, tm=1024):
    T, D = x.shape
    tm = min(tm, T)
    return pl.pallas_call(
        _rmsnorm_kernel,
        grid=(T // tm,),
        in_specs=[pl.BlockSpec((tm, D), lambda i: (i, 0)), pl.BlockSpec((1, D), lambda i: (0, 0))],
        out_specs=pl.BlockSpec((tm, D), lambda i: (i, 0)),
        out_shape=jax.ShapeDtypeStruct((T, D), F32),
        compiler_params=_cparams(("parallel",)),
    )(x, gamma.reshape(1, D).astype(F32))


def _peer_cand_tables(tn):
    fidx, vmask = [], []
    for k2 in range(16):
        fidx.append(k2); vmask.append(0.0)
    for k1 in range(1, 8):
        lim = PEER_TOPK // (k1 + 1)
        for k2 in range(8):
            fidx.append(k1 * 16 + k2); vmask.append(0.0 if k2 < lim else -np.inf)
    for k1 in range(8, 16):
        fidx.append(k1 * 16); vmask.append(0.0)
    fidx = np.broadcast_to(np.asarray(fidx, np.int32)[:, None], (80, tn))
    vmask = np.broadcast_to(np.asarray(vmask, np.float32)[:, None], (80, tn))
    return jnp.asarray(fidx), jnp.asarray(vmask)


def _top16_rows(s, exact_ties):
    n, tn = s.shape
    rows = lax.broadcasted_iota(jnp.int32, (n, tn), 0)
    rows16 = lax.broadcasted_iota(jnp.int32, (PEER_TOPK, tn), 0)
    rank = jnp.full((n, tn), float(PEER_TOPK), F32)
    tops = jnp.zeros((PEER_TOPK, tn), F32)
    v = s
    for k in range(PEER_TOPK):
        m = jnp.max(v, axis=0, keepdims=True)
        if exact_ties:
            hit = rows == jnp.min(jnp.where(v == m, rows, n), axis=0, keepdims=True)
        else:
            hit = v == m
        rank = jnp.where(hit, float(k), rank)
        v = jnp.where(hit, -jnp.inf, v)
        tops = jnp.where(rows16 == k, m, tops)
    return tops, rank, v


def _peer_select_head(q_ref, keys_ref, fidx, vmask, exact_ties):
    tops, ranks, es, picked = [], [], [], []
    for p in range(2):
        q = q_ref[:, p * PEER_HALF_DIM:(p + 1) * PEER_HALF_DIM]
        s = lax.dot_general(keys_ref[p], q, _NT, preferred_element_type=F32)
        t, r, v = _top16_rows(s, exact_ties)
        tops.append(t); ranks.append(r)
        es.append(jnp.exp(s - t[0:1, :]))
        picked.append(jnp.sum(jnp.where(v == -jnp.inf, 1.0, 0.0), axis=0, keepdims=True))
    ts1, ts2 = tops
    pieces = [ts1[0:1, :] + ts2]
    for k1 in range(1, 8):
        pieces.append(ts1[k1:k1 + 1, :] + ts2[0:8, :])
    pieces.append(ts1[8:16, :] + ts2[0:1, :])
    cand0 = jnp.concatenate(pieces, axis=0) + vmask
    cand = cand0
    for _ in range(PEER_TOPK):
        m = jnp.max(cand, axis=0, keepdims=True)
        if exact_ties:
            hit = fidx == jnp.min(jnp.where(cand == m, fidx, 4096), axis=0, keepdims=True)
        else:
            hit = cand == m
        cand = jnp.where(hit, -jnp.inf, cand)
    taken = jnp.logical_and(cand == -jnp.inf, vmask == 0.0)
    takenf = taken.astype(F32)
    picked.append(jnp.sum(takenf, axis=0, keepdims=True))
    unique = jnp.min(jnp.where((picked[0] == PEER_TOPK) & (picked[1] == PEER_TOPK) & (picked[2] == PEER_TOPK),
                               1.0, 0.0)) > 0.5
    best = ts1[0:1, :] + ts2[0:1, :]
    z = jnp.sum(jnp.where(taken, jnp.exp(cand0 - best), 0.0), axis=0, keepdims=True)
    counts = [jnp.sum(takenf[0:16, :], axis=0, keepdims=True)]
    for k1 in range(1, 8):
        counts.append(jnp.sum(takenf[16 + 8 * (k1 - 1):16 + 8 * k1, :], axis=0, keepdims=True))
    tail = takenf[72:80, :]
    cnt = jnp.zeros_like(ranks[0])
    for k1 in range(PEER_TOPK):
        nk = counts[k1] if k1 < 8 else tail[k1 - 8:k1 - 7, :]
        cnt = jnp.where(ranks[0] == float(k1), nk, cnt)
    return (cnt, ranks[1], es[0], es[1] / z), unique


def _peer_select_kernel(q_ref, keys_ref, fidx_ref, vmask_ref, cnt_ref, rank2_ref, e1_ref, e2_ref):
    fidx = fidx_ref[...]
    vmask = vmask_ref[...]

    def store(vals):
        for ref, val in zip((cnt_ref, rank2_ref, e1_ref, e2_ref), vals):
            ref[...] = val.astype(ref.dtype)

    vals, unique = _peer_select_head(q_ref, keys_ref, fidx, vmask, exact_ties=False)
    store(vals)

    @pl.when(jnp.logical_not(unique))
    def _():
        store(_peer_select_head(q_ref, keys_ref, fidx, vmask, exact_ties=True)[0])


def peer_select(q, keys, *, tn=512):
    T = q.shape[0]
    tn = min(tn, T)
    fidx, vmask = _peer_cand_tables(tn)
    rows = PEER_HEADS * PEER_N_KEYS
    ospec = pl.BlockSpec((PEER_N_KEYS, tn), lambda i, h: (h, i))
    return pl.pallas_call(
        _peer_select_kernel,
        grid=(T // tn, PEER_HEADS),
        in_specs=[pl.BlockSpec((tn, 2 * PEER_HALF_DIM), lambda i, h: (i, h)),
                  pl.BlockSpec((2, PEER_N_KEYS, PEER_HALF_DIM), lambda i, h: (h, 0, 0)),
                  pl.BlockSpec((80, tn), lambda i, h: (0, 0)),
                  pl.BlockSpec((80, tn), lambda i, h: (0, 0))],
        out_specs=[ospec] * 4,
        out_shape=[jax.ShapeDtypeStruct((rows, T), dt) for dt in (F32, BF16, F32, BF16)],
        compiler_params=_cparams(("parallel", "parallel")),
    )(q, keys, fidx, vmask)


def _peer_dense_kernel(xn_ref, u_ref, vt_ref, cnt_ref, rank2_ref, e1_ref, e2_ref, res_ref, o_ref,
                       acc_ref, g0_ref, g1_ref, *, c_per_step):
    j = pl.program_id(1)
    n_tiles = pl.num_programs(1) - 1

    @pl.when(j == 0)
    def _():
        acc_ref[...] = jnp.zeros_like(acc_ref)
        g1_ref[...] = jnp.zeros_like(g1_ref)

    @pl.when(j % 2 == 0)
    def _():
        _peer_dense_step(xn_ref, u_ref, vt_ref, cnt_ref, rank2_ref, e1_ref, e2_ref, acc_ref,
                         g1_ref, g0_ref, j, c_per_step)

    @pl.when(j % 2 == 1)
    def _():
        _peer_dense_step(xn_ref, u_ref, vt_ref, cnt_ref, rank2_ref, e1_ref, e2_ref, acc_ref,
                         g0_ref, g1_ref, j, c_per_step)

    @pl.when(j == n_tiles)
    def _():
        o_ref[...] = res_ref[...] + acc_ref[...].T


def _peer_dense_step(xn_ref, u_ref, vt_ref, cnt_ref, rank2_ref, e1_ref, e2_ref, acc_ref, g_ref, g_next_ref,
                     j, c_per_step):

    tn = xn_ref.shape[1]
    bf16_rows = 16
    reps = PEER_N_KEYS // bf16_rows

    def row_tile(ref, row):
        r16 = jnp.broadcast_to(ref[pl.ds(row, 1), :], (bf16_rows, tn)).astype(BF16)
        return jnp.concatenate([r16] * reps, axis=0)

    c0 = jnp.maximum(j - 1, 0) * c_per_step
    up_rows = 2 * PEER_N_KEYS
    blocks = []
    for cc in range(c_per_step):
        if (cc * PEER_N_KEYS) % up_rows == 0:
            rs = slice(cc * PEER_N_KEYS, cc * PEER_N_KEYS + up_rows)
            hT = jnp.dot(u_ref[rs, :], xn_ref[...], preferred_element_type=F32)
            g_next_ref[rs, :] = _gelu_sigmoid(hT.astype(BF16))
        c = c0 + cc
        w = None
        for h in range(PEER_HEADS):
            row = h * PEER_N_KEYS + c
            n_row = row_tile(cnt_ref, row)
            e1_row = row_tile(e1_ref, row)
            sl = slice(h * PEER_N_KEYS, (h + 1) * PEER_N_KEYS)
            term = jnp.where(rank2_ref[sl, :] < n_row, e2_ref[sl, :], 0.0) * e1_row
            w = term if w is None else w + term
        blocks.append(w * g_ref[cc * PEER_N_KEYS:(cc + 1) * PEER_N_KEYS, :])
    aT = jnp.concatenate(blocks, axis=0)
    acc_ref[...] += jnp.dot(vt_ref[...], aT, preferred_element_type=F32)


def peer_dense(xn, u, vt, cnt, rank2, e1, e2, res, *, tn=512, te=1024):
    D, T = xn.shape
    E = u.shape[0]
    tn = min(tn, T)
    rows = PEER_HEADS * PEER_N_KEYS
    sel_spec = pl.BlockSpec((rows, tn), lambda i, j: (0, i))
    n_tiles = E // te
    return pl.pallas_call(
        functools.partial(_peer_dense_kernel, c_per_step=te // PEER_N_KEYS),
        grid=(T // tn, n_tiles + 1),
        in_specs=[pl.BlockSpec((D, tn), lambda i, j: (0, i)),
                  pl.BlockSpec((te, D), lambda i, j: (jnp.minimum(j, n_tiles - 1), 0)),
                  pl.BlockSpec((D, te), lambda i, j: (0, jnp.maximum(j - 1, 0))),
                  sel_spec, sel_spec, sel_spec, sel_spec,
                  pl.BlockSpec((tn, D), lambda i, j: (i, 0))],
        out_specs=pl.BlockSpec((tn, D), lambda i, j: (i, 0)),
        out_shape=jax.ShapeDtypeStruct((T, D), F32),
        scratch_shapes=[pltpu.VMEM((D, tn), F32), pltpu.VMEM((te, tn), BF16), pltpu.VMEM((te, tn), BF16)],
        compiler_params=_cparams(("parallel", "arbitrary")),
    )(xn, u, vt, cnt, rank2, e1, e2, res)


def peer_layer(h, gamma, w_q, sub_keys, u, v):
    q, xn = norm_matmul(h, gamma, w_q.astype(BF16), emit_xn=True)
    keys = sub_keys.reshape(2 * PEER_HEADS, PEER_N_KEYS, PEER_HALF_DIM).astype(BF16)
    cnt, rank2, e1, e2 = peer_select(q, keys)
    return peer_dense(xn, u.astype(BF16), v.T.astype(BF16), cnt, rank2, e1, e2, h)


def _cumsum_aug_kernel(lf_ref, tri_ref, place_q_ref, place_k_ref, ones_q_ref, ones_k_ref,
                       qa_ref, ka_ref, c_ref, carry_ref):
    @pl.when(pl.program_id(1) == 0)
    def _():
        carry_ref[...] = jnp.zeros_like(carry_ref)

    lf = lf_ref[0]
    c = jnp.dot(tri_ref[...], lf, preferred_element_type=F32, precision=lax.Precision.HIGHEST) + carry_ref[...]
    carry_ref[...] = c[-1:, :]
    c = c * LOG2E
    c_ref[0] = c
    hi = c.astype(BF16)
    r1 = c - hi.astype(F32)
    mid = r1.astype(BF16)
    lo = (r1 - mid.astype(F32)).astype(BF16)
    nh = N_HEADS
    lane = lax.broadcasted_iota(jnp.int32, c.shape, 1)
    parts = jnp.where(lane < nh, hi.astype(F32),
                      jnp.where(lane < 2 * nh, pltpu.roll(mid.astype(F32), nh, axis=1),
                                pltpu.roll(lo.astype(F32), 2 * nh, axis=1)))
    parts = jnp.where(lane < 3 * nh, parts, 0.0).astype(BF16)
    qa_ref[0] = (jnp.dot(parts, place_q_ref[...], preferred_element_type=F32) + ones_q_ref[...]).astype(BF16)
    ka_ref[0] = (jnp.dot(parts, place_k_ref[...], preferred_element_type=F32) + ones_k_ref[...]).astype(BF16)


def fox_bias_operands(logf, *, tc=256):
    B, S, _ = logf.shape
    nh = N_HEADS
    tri = jnp.asarray(np.tril(np.ones((tc, tc), np.float32)))
    pq = np.zeros((LANES, nh * LANES), np.float32)
    pk = np.zeros((LANES, nh * LANES), np.float32)
    oq = np.zeros((1, nh * LANES), np.float32)
    ok = np.zeros((1, nh * LANES), np.float32)
    for h in range(nh):
        for part in range(3):
            pq[part * nh + h, h * LANES + part] = 1.0
            pk[part * nh + h, h * LANES + 3 + part] = -1.0
            oq[0, h * LANES + 3 + part] = 1.0
            ok[0, h * LANES + part] = 1.0
    const = lambda a: pl.BlockSpec(a.shape, lambda b, i: (0,) * a.ndim)
    pq, pk, oq, ok = jnp.asarray(pq, BF16), jnp.asarray(pk, BF16), jnp.asarray(oq), jnp.asarray(ok)
    out = jax.ShapeDtypeStruct((B, S, nh * LANES), BF16)
    return pl.pallas_call(
        _cumsum_aug_kernel,
        grid=(B, S // tc),
        in_specs=[pl.BlockSpec((1, tc, LANES), lambda b, i: (b, i, 0)),
                  const(tri), const(pq), const(pk), const(oq), const(ok)],
        out_specs=[pl.BlockSpec((1, tc, nh * LANES), lambda b, i: (b, i, 0))] * 2
        + [pl.BlockSpec((1, tc, LANES), lambda b, i: (b, i, 0))],
        out_shape=[out, out, jax.ShapeDtypeStruct((B, S, LANES), F32)],
        scratch_shapes=[pltpu.VMEM((1, LANES), F32)],
        compiler_params=_cparams(("parallel", "arbitrary")),
    )(logf, tri, pq, pk, oq, ok)


def _fox_attn_kernel(first_ref, q_ref, qa_ref, kv_ref, ka_ref, o_ref, *, tq, tk, heads_per_step):
    qi = pl.program_id(2)
    t0 = qi * tq
    n_full = t0 // tk
    j_first = first_ref[(pl.program_id(0) * pl.num_programs(1) + pl.program_id(1)) * pl.num_programs(2) + qi]
    n_diag = tq // tk
    lanes = [slice(hh * LANES, (hh + 1) * LANES) for hh in range(heads_per_step)]
    qs = [jnp.concatenate([q_ref[0, :, lsl], qa_ref[0, :, lsl]], axis=1) for lsl in lanes]

    def step(j, carry, masked):
        rows = pl.ds(pl.multiple_of(j * tk, tk), tk)
        klane = lax.broadcasted_iota(jnp.int32, (tk, LANES), 1)
        new = []
        for lsl, q, (m, acc) in zip(lanes, qs, carry):
            kv = kv_ref[0, rows, lsl]
            kk = jnp.concatenate([kv, ka_ref[0, rows, lsl]], axis=1)
            ones_v = jnp.where(klane < HEAD_DIM, 1.0, kv).astype(BF16)
            s = lax.dot_general(q, kk, _NT, preferred_element_type=F32)
            if masked:
                qpos = t0 + lax.broadcasted_iota(jnp.int32, (tq, 1), 0)
                kpos = j * tk + lax.broadcasted_iota(jnp.int32, (1, tk), 1)
                s = jnp.where(kpos <= qpos, s, NEG_INF)
            m_new = jnp.maximum(m, jnp.max(s, axis=1, keepdims=True))
            p = jnp.exp2(s - m_new).astype(BF16)
            acc = jnp.exp2(m - m_new) * acc + jnp.dot(p, ones_v, preferred_element_type=F32)
            new.append((m_new, acc))
        return tuple(new)

    init = (jnp.full((tq, 1), NEG_INF, F32), jnp.zeros((tq, LANES), F32))
    n_pairs = (n_full - j_first) // 2
    carry = lax.fori_loop(
        0, n_pairs, lambda i, c: step(j_first + 2 * i + 1, step(j_first + 2 * i, c, False), False),
        (init,) * heads_per_step)
    carry = lax.fori_loop(j_first + 2 * n_pairs, n_full, functools.partial(step, masked=False), carry)
    for d in range(n_diag):
        carry = step(n_full + d, carry, True)
    outs = [acc / acc[:, 0:1] for (_, acc) in carry]
    lane = lax.broadcasted_iota(jnp.int32, (tq, LANES), 1)
    blocks = []
    for pair in range(heads_per_step // 2):
        a, b = outs[2 * pair], outs[2 * pair + 1]
        blocks.append(jnp.where(lane < HEAD_DIM, pltpu.roll(a, HEAD_DIM, axis=1), b))
    o_ref[0] = jnp.concatenate(blocks, axis=1).astype(o_ref.dtype) if len(blocks) > 1 else blocks[0].astype(o_ref.dtype)


def _head_slots(w, n_heads, second=None):
    D = w.shape[0]
    a = w.reshape(D, n_heads, HEAD_DIM)
    b = jnp.zeros_like(a) if second is None else second.reshape(D, n_heads, HEAD_DIM)
    return jnp.concatenate([a, b], axis=-1).reshape(D, n_heads * LANES)


def fox_layer(h, gamma, w_in, f_bias, w_out, B, S):
    aw = ATTN_WIDTH
    wq = _head_slots(w_in[:, :aw] * (HEAD_DIM ** -0.5 * LOG2E), N_HEADS)
    wkv = _head_slots(w_in[:, aw:2 * aw], N_HEADS, w_in[:, 2 * aw:3 * aw])
    w_main = jnp.concatenate([wq, wkv], axis=1).astype(BF16)
    wf = jnp.pad(w_in[:, 3 * aw:], ((0, 0), (0, LANES - N_HEADS))).astype(BF16)
    bf = jnp.pad(f_bias.astype(F32), (0, LANES - N_HEADS))
    qkv = norm_matmul(h, gamma, w_main)
    logf = norm_matmul(h, gamma, wf, bias=bf, act="log_sigmoid", out_dtype=F32)
    qa, ka, c2 = fox_bias_operands(logf.reshape(B, S, LANES))
    qkv = qkv.reshape(B, S, 2 * N_HEADS * LANES)
    o = fox_attention(qkv, qa, ka, c2)
    return matmul_residual(o.reshape(B * S, aw), w_out.astype(BF16), h)


FOX_NEGLIGIBLE_LOG2 = 200.0


def _norm_maxima_kernel(x_ref, ind_ref, o_ref):
    x = x_ref[0].astype(F32)
    ss = jnp.dot((x * x).astype(BF16), ind_ref[...], preferred_element_type=F32)
    o_ref[0, 0] = jnp.broadcast_to(jnp.max(ss, axis=0, keepdims=True), o_ref.shape[2:])


def fox_norm_maxima(qkv, tile):
    B, S, W = qkv.shape
    n = S // tile
    col = np.arange(W)
    slot, lane = col // LANES, col % LANES
    used = (slot < N_HEADS) | (lane < HEAD_DIM)
    ind = jnp.asarray((slot[:, None] == np.arange(LANES)[None, :]) & used[:, None], BF16)
    rows = min(256, tile)
    ss = pl.pallas_call(
        _norm_maxima_kernel,
        grid=(B, S // rows),
        in_specs=[pl.BlockSpec((1, rows, W), lambda b, i: (b, i, 0)),
                  pl.BlockSpec(ind.shape, lambda b, i: (0, 0))],
        out_specs=pl.BlockSpec((1, 1, 8, LANES), lambda b, i: (b, i, 0, 0)),
        out_shape=jax.ShapeDtypeStruct((B, S // rows, 8, LANES), F32),
        compiler_params=_cparams(("parallel", "parallel")),
    )(qkv, ind)
    ss = jnp.max(ss[:, :, 0, :2 * N_HEADS].reshape(B, n, tile // rows, 2 * N_HEADS), axis=2)
    return jnp.sqrt(ss * 1.01)


def fox_first_chunk(qkv, c2, tile, heads_per_step):
    B, S, _ = qkv.shape
    n = S // tile
    norms = fox_norm_maxima(qkv, tile)
    qmax, kmax = norms[..., :N_HEADS], norms[..., N_HEADS:]
    c = c2[..., :N_HEADS].reshape(B, n, tile, N_HEADS)
    cmax, cmin = jnp.max(c, axis=2), jnp.min(c, axis=2)
    upper = qmax[:, :, None] * kmax[:, None, :] + cmax[:, :, None] - cmin[:, None, :]
    own = -(qmax * kmax)
    earlier = jnp.arange(n)[None, :, None, None] > jnp.arange(n)[None, None, :, None]
    skip = (upper < own[:, :, None] - FOX_NEGLIGIBLE_LOG2) & earlier
    first = jnp.sum(jnp.cumprod(skip.astype(jnp.int32), axis=2), axis=2)
    first = jnp.min(first.reshape(B, n, N_HEADS // heads_per_step, heads_per_step), axis=-1)
    return jnp.transpose(first, (0, 2, 1)).reshape(-1).astype(jnp.int32)


def fox_attention(qkv, qa, ka, c2, *, tq=1024, tk=1024, heads_per_step=2):
    B, S, _ = qkv.shape
    tq = tk = min(tq, S)
    hs = heads_per_step
    wq = hs * LANES
    n_qblk = N_HEADS // hs
    first = fox_first_chunk(qkv, c2, tk, hs)
    grid_spec = pltpu.PrefetchScalarGridSpec(
        num_scalar_prefetch=1,
        grid=(B, n_qblk, S // tq),
        in_specs=[pl.BlockSpec((1, tq, wq), lambda b, h, i, first: (b, i, h)),
                  pl.BlockSpec((1, tq, wq), lambda b, h, i, first: (b, i, h)),
                  pl.BlockSpec((1, S, wq), lambda b, h, i, first: (b, 0, n_qblk + h)),
                  pl.BlockSpec((1, S, wq), lambda b, h, i, first: (b, 0, h))],
        out_specs=pl.BlockSpec((1, tq, hs * HEAD_DIM), lambda b, h, i, first: (b, i, h)))
    return pl.pallas_call(
        functools.partial(_fox_attn_kernel, tq=tq, tk=tk, heads_per_step=hs),
        grid_spec=grid_spec,
        out_shape=jax.ShapeDtypeStruct((B, S, ATTN_WIDTH), BF16),
        compiler_params=_cparams(("parallel", "parallel", "arbitrary")),
    )(first, qkv, qa, qkv, ka)


def _rot_half_cols(w):
    D = w.shape[0]
    a = w.reshape(D, -1, HEAD_DIM)
    half = HEAD_DIM // 2
    return jnp.concatenate([-a[..., half:], a[..., :half]], axis=-1).reshape(w.shape)


def _rope_tables(S):
    half = HEAD_DIM // 2
    inv_freq = ROPE_THETA ** (-jnp.arange(half, dtype=F32) / half)
    ang = jnp.arange(S, dtype=F32)[:, None] * inv_freq[None, :]
    c, s = jnp.cos(ang), jnp.sin(ang)
    c2, s2 = jnp.concatenate([c, c], axis=1), jnp.concatenate([s, s], axis=1)
    cos = jnp.stack([jnp.concatenate([c2, jnp.ones_like(c2)], axis=1), jnp.concatenate([c2, c2], axis=1)])
    sin = jnp.stack([jnp.concatenate([s2, jnp.zeros_like(s2)], axis=1), jnp.concatenate([s2, s2], axis=1)])
    return cos, sin


def _compress_kernel(x_ref, pea_ref, peb_ref, wa_ref, wb_ref, w2_ref, o_ref, pa_ref, pb0_ref, *, n_rows):
    u = pl.program_id(1)
    x = x_ref[0].astype(F32)
    pa = jnp.dot((x + pea_ref[...]).astype(BF16), wa_ref[...], preferred_element_type=F32)
    pb = jnp.dot((x + peb_ref[...]).astype(BF16), wb_ref[...], preferred_element_type=F32)

    def emit(slab, hid):
        y = jnp.dot(_gelu_tanh(hid).astype(BF16), w2_ref[...], preferred_element_type=F32)
        o_ref[0, pl.ds(pl.multiple_of(slab * n_rows, n_rows), n_rows), :] = y

    @pl.when(u == 0)
    def _():
        pb0_ref[...] = pb

    @pl.when(u > 0)
    def _():
        emit(u - 1, pa_ref[...] + pb)

    @pl.when(u == 3)
    def _():
        emit(3, pa + pltpu.roll(pb0_ref[...], n_rows - 1, axis=0))

    pa_ref[...] = pa


def nsa_compress(src, pe, w1, w2):
    B, S, W = src.shape
    G = NSA_GROUPS
    n_rows = S // 64
    half = CMP_BLOCK // 2
    cw = half * W
    xv = src.reshape(B, n_rows, 4 * cw)
    pe_flat = jnp.transpose(pe, (1, 0, 2)).reshape(CMP_BLOCK, W).astype(F32)
    pea, peb = pe_flat[:half].reshape(1, cw), pe_flat[half:].reshape(1, cw)
    eye = jnp.eye(G, dtype=F32)
    wfull = jnp.einsum('gldh,gk->lkdgh', w1.astype(F32), eye).reshape(CMP_BLOCK, W, G * CMP_HIDDEN)
    wa = wfull[:half].reshape(cw, G * CMP_HIDDEN).astype(BF16)
    wb = wfull[half:].reshape(cw, G * CMP_HIDDEN).astype(BF16)
    w2bd = jnp.einsum('ghd,gk->ghkd', w2.astype(F32), eye).reshape(G * CMP_HIDDEN, W).astype(BF16)
    const = lambda a: pl.BlockSpec(a.shape, lambda b, u: (0,) * a.ndim)
    return pl.pallas_call(
        functools.partial(_compress_kernel, n_rows=n_rows),
        grid=(B, 4),
        in_specs=[pl.BlockSpec((1, n_rows, cw), lambda b, u: (b, 0, u)),
                  const(pea), const(peb), const(wa), const(wb), const(w2bd)],
        out_specs=pl.BlockSpec((1, 4 * n_rows, W), lambda b, u: (b, 0, 0)),
        out_shape=jax.ShapeDtypeStruct((B, 4 * n_rows, W), F32),
        scratch_shapes=[pltpu.VMEM((n_rows, G * CMP_HIDDEN), F32), pltpu.VMEM((n_rows, G * CMP_HIDDEN), F32)],
        compiler_params=_cparams(("parallel", "arbitrary")),
    )(xv, pea, peb, wa, wb, w2bd)


def _nsa_attn_kernel(q_ref, kvs_ref, kvw_ref, kvc_ref, kvct_ref, gate_ref, bmat_ref, pmat_ref, o_ref,
                     *, tq, ts, tk, seq):
    R = NSA_Q_PER_GROUP
    n_slc = seq // SLC_BLOCK
    n_sel = min(SLC_TOPK, n_slc)
    assert ts & (ts - 1) == 0 and n_slc & (n_slc - 1) == 0 and tk % tq == 0 and tq % ts == 0
    log_slc = n_slc.bit_length() - 1
    qi = pl.program_id(2)
    t0 = qi * tq

    local = [_nsa_local_branches(q_ref, kvw_ref, kvc_ref, kvct_ref, t0 + i * ts, i * ts, ts, n_slc, n_sel, log_slc)
             for i in range(tq // ts)]
    o_c = jnp.concatenate([o[0][r * ts:(r + 1) * ts] for r in range(R) for o in local], axis=0)
    o_w = jnp.concatenate([o[1][r * ts:(r + 1) * ts] for r in range(R) for o in local], axis=0)
    sel_bias = jnp.concatenate([o[2] for o in local], axis=0)
    qs = jnp.concatenate([q_ref[0, :, r * LANES:(r + 1) * LANES] for r in range(R)], axis=0)
    _nsa_selected_and_combine(qs, o_c, o_w, sel_bias, kvs_ref, gate_ref, bmat_ref, pmat_ref, o_ref,
                              t0=t0, tq=tq, tk=tk, seq=seq)


def _nsa_local_branches(q_ref, kvw_ref, kvc_ref, kvct_ref, t0, row0, tq, n_slc, n_sel, log_slc):
    R = NSA_Q_PER_GROUP
    M = R * tq
    qs = jnp.concatenate([q_ref[0, row0:row0 + tq, r * LANES:(r + 1) * LANES] for r in range(R)], axis=0)

    wlen = WINDOW + tq
    start = jnp.maximum(t0 - WINDOW, 0)
    kvw = kvw_ref[0, pl.ds(pl.multiple_of(start, tq), wlen), :]
    s_w = lax.dot_general(qs, kvw, _NT, preferred_element_type=F32)
    qpos = t0 + (lax.broadcasted_iota(jnp.int32, (M, 1), 0) & (tq - 1))
    kpos = start + lax.broadcasted_iota(jnp.int32, (1, wlen), 1)
    s_w = jnp.where(kpos <= qpos, jnp.where(kpos > qpos - WINDOW, s_w, NEG_INF), NEG_INF)
    p_w = jnp.exp2(s_w - jnp.max(s_w, axis=1, keepdims=True)).astype(BF16)
    wlane = lax.broadcasted_iota(jnp.int32, (wlen, LANES), 1)
    acc_w = jnp.dot(p_w, jnp.where(wlane < HEAD_DIM, 1.0, kvw).astype(BF16), preferred_element_type=F32)
    o_w = acc_w / acc_w[:, 0:1]

    n_groups_total, gs = _nsa_row_groups(n_slc)
    log_gs = gs.bit_length() - 1
    last_block = (t0 + tq - 1) >> (SLC_BLOCK.bit_length() - 1)
    branches = [functools.partial(_nsa_compressed_and_select, qs, kvc_ref, kvct_ref, t0, tq, n_slc, n_sel,
                                  gs, g + 1) for g in range(n_groups_total)]
    o_c, sel_bias = lax.switch(last_block >> log_gs, branches)
    return o_c, o_w, sel_bias


def _nsa_row_groups(n_slc):
    n_groups = max(1, min(4, n_slc // 8))
    assert n_slc % n_groups == 0 and (n_slc // n_groups) % 8 == 0
    return n_groups, n_slc // n_groups


def _nsa_compressed_and_select(qs, kvc_ref, kvct_ref, t0, tq, n_slc, n_sel, gs, n_groups):
    R = NSA_Q_PER_GROUP
    M = R * tq
    n_rows, n_j = n_groups * 4 * gs, n_groups * gs
    log_gs = gs.bit_length() - 1
    sT = lax.dot_general(kvc_ref[0, 0, 0:n_rows, :], qs, _NT, preferred_element_type=F32)
    rowc = lax.broadcasted_iota(jnp.int32, (n_rows, 1), 0)
    blk_j = ((rowc >> (log_gs + 2)) << log_gs) + (rowc & (gs - 1))
    blk_u = (rowc >> log_gs) & 3
    cmp_end = blk_j * SLC_BLOCK + blk_u * CMP_STRIDE + (CMP_BLOCK - 1)
    tcol = t0 + (lax.broadcasted_iota(jnp.int32, (1, M), 1) & (tq - 1))
    sm = jnp.where(cmp_end <= tcol, sT, NEG_INF)
    mx = jnp.max(sm, axis=0, keepdims=True)
    e = jnp.exp2(sm - mx)
    inv = jnp.where(mx > 0.5 * NEG_INF, 1.0 / jnp.sum(e, axis=0, keepdims=True), 0.0)
    pT = e * inv
    o_c = jnp.dot(kvct_ref[0, 0, :, 0:n_rows], pT.astype(BF16), preferred_element_type=F32).T

    psum = pT[:, 0:tq]
    for r in range(1, R):
        psum = psum + pT[:, r * tq:(r + 1) * tq]
    p0, p1, p2, p3 = (jnp.concatenate([psum[(4 * g + u) * gs:(4 * g + u + 1) * gs, :] for g in range(n_groups)],
                                      axis=0) for u in range(4))
    jrow = lax.broadcasted_iota(jnp.int32, (n_j, tq), 0)
    p3_prev = jnp.where(jrow == 0, 0.0, pltpu.roll(p3, 1, axis=0))
    imp = p0 + p1 + p2 + 0.5 * p3 + 0.5 * p3_prev
    cur = (t0 + lax.broadcasted_iota(jnp.int32, (n_j, tq), 1)) >> (SLC_BLOCK.bit_length() - 1)
    forced = (jrow == 0) | (jrow == cur) | (jrow == cur - 1)
    vals = jnp.where(forced, -jnp.inf, jnp.where(jrow <= cur, imp, -jnp.inf))
    sel0 = jnp.where(forced, 1.0, 0.0)
    n_free = n_sel - 3

    def pick(exact_ties):
        v, sel = vals, sel0
        for _ in range(n_free):
            m = jnp.max(v, axis=0, keepdims=True)
            if exact_ties:
                hit = jrow == jnp.min(jnp.where(v == m, jrow, n_slc), axis=0, keepdims=True)
            else:
                hit = v == jnp.where(m == -jnp.inf, jnp.nan, m)
            sel = jnp.where(hit, 1.0, sel)
            v = jnp.where(hit, -jnp.inf, v)
        return sel

    sel_fast = pick(False)
    n_cand = jnp.sum(jnp.where(vals > -jnp.inf, 1.0, 0.0), axis=0, keepdims=True)
    n_picked = jnp.sum(sel_fast - sel0, axis=0, keepdims=True)
    unique = jnp.min(jnp.where(n_picked == jnp.minimum(n_cand, float(n_free)), 1.0, 0.0)) > 0.5
    sel = lax.cond(unique, lambda: sel_fast, lambda: pick(True))
    if n_j < n_slc:
        sel = jnp.concatenate([sel, jnp.zeros((n_slc - n_j, tq), F32)], axis=0)
    sel_bias = ((sel.T - 1.0) * (-NEG_INF)).astype(BF16)
    return o_c, sel_bias


def _nsa_selected_and_combine(qs, o_c, o_w, sel_bias, kvs_ref, gate_ref, bmat_ref, pmat_ref, o_ref,
                              *, t0, tq, tk, seq):
    R = NSA_Q_PER_GROUP
    n_slc = seq // SLC_BLOCK
    blocks_per_chunk = tk // SLC_BLOCK
    n_parts = 2
    hp = R // n_parts
    mp = hp * tq
    qparts = [qs[i * mp:(i + 1) * mp] for i in range(n_parts)]
    qrow = t0 + (lax.broadcasted_iota(jnp.int32, (mp, 1), 0) & (tq - 1))
    kcol = lax.broadcasted_iota(jnp.int32, (1, tk), 1)
    n_chunks = seq // tk
    j_last = t0 // tk

    klane = lax.broadcasted_iota(jnp.int32, (tk, LANES), 1)

    def slc_step(j, carry, masked):
        kv = kvs_ref[0, pl.ds(pl.multiple_of(j * tk, tk), tk), :]
        k_sel = jnp.where(klane < HEAD_DIM, kv, bmat_ref[...]).astype(BF16)
        ones_v = jnp.where(klane < HEAD_DIM, 1.0, kv).astype(BF16)
        off = pl.multiple_of(blocks_per_chunk * (n_chunks - 1 - j), blocks_per_chunk)
        place = pmat_ref[pl.ds(off, n_slc), :].astype(BF16)
        q_bias = jnp.dot(sel_bias, place, preferred_element_type=F32).astype(BF16)
        q_bias = jnp.concatenate([q_bias] * hp, axis=0)
        new = []
        for qp, (m, acc) in zip(qparts, carry):
            s = lax.dot_general(qp + q_bias, k_sel, _NT, preferred_element_type=F32)
            if masked:
                s = jnp.where(j * tk + kcol <= qrow, s, NEG_INF)
            m_new = jnp.maximum(m, jnp.max(s, axis=1, keepdims=True))
            p = jnp.exp2(s - m_new).astype(BF16)
            acc = jnp.exp2(m - m_new) * acc + jnp.dot(p, ones_v, preferred_element_type=F32)
            new.append((m_new, acc))
        return tuple(new)

    init = (jnp.full((mp, 1), NEG_INF, F32), jnp.zeros((mp, LANES), F32))
    n_pairs = j_last // 2
    carry = lax.fori_loop(0, n_pairs, lambda i, c: slc_step(2 * i + 1, slc_step(2 * i, c, False), False),
                          (init,) * n_parts)
    carry = lax.fori_loop(2 * n_pairs, j_last, functools.partial(slc_step, masked=False), carry)
    carry = slc_step(j_last, carry, True)
    o_s = jnp.concatenate([acc / acc[:, 0:1] for (_, acc) in carry], axis=0)

    gates = gate_ref[0]
    lane = lax.broadcasted_iota(jnp.int32, (tq, LANES), 1)
    comb = []
    for r in range(R):
        rs = slice(r * tq, (r + 1) * tq)
        comb.append(gates[:, r:r + 1] * o_c[rs] + gates[:, R + r:R + r + 1] * o_s[rs]
                    + gates[:, 2 * R + r:2 * R + r + 1] * o_w[rs])
    out = [jnp.where(lane < HEAD_DIM, pltpu.roll(comb[2 * i], HEAD_DIM, axis=1), comb[2 * i + 1])
           for i in range(R // 2)]
    o_ref[0] = jnp.concatenate(out, axis=1).astype(o_ref.dtype)


def nsa_attention(qkv, kvc, kvct, gates, *, tq=512, ts=256, tk=1024):
    B, S, _ = qkv.shape
    G, R = NSA_GROUPS, NSA_Q_PER_GROUP
    tk = min(tk, S)
    n_slc = S // SLC_BLOCK
    n_cmp = kvc.shape[2]
    bpc = tk // SLC_BLOCK
    assert bpc <= LANES - HEAD_DIM
    off = bpc * (S // tk - 1)
    lane = np.arange(LANES)[None, :]
    bmat = jnp.asarray(lane - HEAD_DIM == np.arange(tk)[:, None] // SLC_BLOCK, BF16)
    pmat = jnp.asarray((np.arange(n_slc + off)[:, None] - off == lane - HEAD_DIM) & (lane >= HEAD_DIM)
                       & (lane < HEAD_DIM + bpc), F32)
    slc_blk0, win_blk0 = N_HEADS, N_HEADS + G
    return pl.pallas_call(
        functools.partial(_nsa_attn_kernel, tq=tq, ts=ts, tk=tk, seq=S),
        grid=(B, G, S // tq),
        in_specs=[pl.BlockSpec((1, tq, R * LANES), lambda b, g, i: (b, i, g)),
                  pl.BlockSpec((1, S, LANES), lambda b, g, i: (b, 0, slc_blk0 + g)),
                  pl.BlockSpec((1, S, LANES), lambda b, g, i: (b, 0, win_blk0 + g)),
                  pl.BlockSpec((1, 1, n_cmp, LANES), lambda b, g, i: (b, g, 0, 0)),
                  pl.BlockSpec((1, 1, LANES, n_cmp), lambda b, g, i: (b, g, 0, 0)),
                  pl.BlockSpec((1, tq, LANES), lambda b, g, i: (b, i, g)),
                  pl.BlockSpec(bmat.shape, lambda b, g, i: (0, 0)),
                  pl.BlockSpec(pmat.shape, lambda b, g, i: (0, 0))],
        out_specs=pl.BlockSpec((1, tq, R * HEAD_DIM), lambda b, g, i: (b, i, g)),
        out_shape=jax.ShapeDtypeStruct((B, S, ATTN_WIDTH), BF16),
        compiler_params=_cparams(("parallel", "parallel", "arbitrary")),
    )(qkv, qkv, qkv, kvc, kvct, gates, bmat, pmat)


def nsa_layer(h, gamma, w_in, pe_k, w1_k, w2_k, pe_v, w1_v, w2_v, w_out, B, S):
    G, R, hd, aw = NSA_GROUPS, NSA_Q_PER_GROUP, HEAD_DIM, ATTN_WIDTH
    kvd = G * hd
    sec = lambda i: w_in[:, aw + i * kvd: aw + (i + 1) * kvd]
    wq = w_in[:, :aw] * (hd ** -0.5 * LOG2E)
    wa = jnp.concatenate([_head_slots(wq, N_HEADS), _head_slots(sec(2), G, sec(3)),
                          _head_slots(sec(4), G, sec(5))], axis=1).astype(BF16)
    wb = jnp.concatenate([_head_slots(_rot_half_cols(wq), N_HEADS), _head_slots(_rot_half_cols(sec(2)), G),
                          _head_slots(_rot_half_cols(sec(4)), G)], axis=1).astype(BF16)
    cos, sin = _rope_tables(S)
    qkv = norm_matmul(h, gamma, wa, wb=wb, cos=cos, sin=sin)
    kc_src = norm_matmul(h, gamma, sec(0).astype(BF16), wb=_rot_half_cols(sec(0)).astype(BF16),
                         cos=cos, sin=sin, table_of_tile=lambda j: 1)
    vc_src = norm_matmul(h, gamma, sec(1).astype(BF16))
    wg = w_in[:, aw + 6 * kvd:].reshape(-1, 3, G, R)
    wg = jnp.transpose(wg, (0, 2, 1, 3)).reshape(-1, G, 3 * R)
    wg = jnp.pad(wg, ((0, 0), (0, 0), (0, LANES - 3 * R))).reshape(-1, G * LANES).astype(BF16)
    gates = norm_matmul(h, gamma, wg, act="sigmoid", out_dtype=F32)
    kc = nsa_compress(kc_src.reshape(B, S, kvd), pe_k, w1_k, w2_k)
    vc = nsa_compress(vc_src.reshape(B, S, kvd), pe_v, w1_v, w2_v)
    n_cmp = kc.shape[1]
    kvc = jnp.concatenate([kc.reshape(B, n_cmp, G, hd), vc.reshape(B, n_cmp, G, hd)], axis=-1)
    n_groups, gs = _nsa_row_groups(S // SLC_BLOCK)
    kvc = kvc.reshape(B, 4, n_groups, gs, G, 2 * hd)
    kvc = jnp.transpose(kvc, (0, 4, 2, 1, 3, 5)).reshape(B, G, n_cmp, 2 * hd).astype(BF16)
    kvct = jnp.swapaxes(kvc, 2, 3)
    o = nsa_attention(qkv.reshape(B, S, -1), kvc, kvct, gates.reshape(B, S, G * LANES))
    return matmul_residual(o.reshape(B * S, aw), w_out.astype(BF16), h)


def kernel(x, l0_attn_norm, l0_w_in, l0_cmp_pe_k, l0_cmp_w1_k, l0_cmp_w2_k, l0_cmp_pe_v, l0_cmp_w1_v,
           l0_cmp_w2_v, l0_w_out, l0_ffn_norm, l0_peer_wq, l0_peer_keys, l0_peer_u, l0_peer_v,
           l1_attn_norm, l1_w_in, l1_f_bias, l1_w_out, l1_ffn_norm, l1_peer_wq, l1_peer_keys, l1_peer_u,
           l1_peer_v, final_norm):
    B, S, D = x.shape
    h = x.reshape(B * S, D)
    h = nsa_layer(h, l0_attn_norm, l0_w_in, l0_cmp_pe_k, l0_cmp_w1_k, l0_cmp_w2_k, l0_cmp_pe_v, l0_cmp_w1_v,
                  l0_cmp_w2_v, l0_w_out, B, S)
    h = peer_layer(h, l0_ffn_norm, l0_peer_wq, l0_peer_keys, l0_peer_u, l0_peer_v)
    h = fox_layer(h, l1_attn_norm, l1_w_in, l1_f_bias, l1_w_out, B, S)
    h = peer_layer(h, l1_ffn_norm, l1_peer_wq, l1_peer_keys, l1_peer_u, l1_peer_v)
    return rmsnorm(h, final_norm).reshape(B, S, D)
```

```python
import functools

import numpy as np
import jax
import jax.numpy as jnp
from jax import lax
from jax.experimental import pallas as pl
from jax.experimental.pallas import tpu as pltpu

F32 = jnp.float32
BF16 = jnp.bfloat16

D_MODEL = 1024
N_HEADS = 16
HEAD_DIM = 64
ATTN_WIDTH = N_HEADS * HEAD_DIM
NSA_GROUPS = 4
NSA_Q_PER_GROUP = N_HEADS // NSA_GROUPS
CMP_BLOCK = 32
CMP_STRIDE = 16
CMP_HIDDEN = 2 * HEAD_DIM
SLC_BLOCK = 64
SLC_TOPK = 16
WINDOW = 512
FORCE_SCORE = 1.0e4
ROPE_THETA = 10000.0
PEER_HEADS = 8
PEER_N_KEYS = 128
PEER_TOPK = 16
PEER_HALF_DIM = 128
RMS_EPS = 1e-6
NEG_INF = -1e30
LOG2E = 1.4426950408889634

LANES = 128
VMEM_LIMIT_BYTES = 56 * 1024 * 1024

_NT = (((1,), (1,)), ((), ()))


def _cparams(sem, vmem=VMEM_LIMIT_BYTES, flags=None):
    return pltpu.CompilerParams(dimension_semantics=sem, vmem_limit_bytes=vmem, flags=flags)


def _gelu_tanh(x):
    return 0.5 * x * (1.0 + jnp.tanh(0.7978845608028654 * (x + 0.044715 * (x * x * x))))


def _gelu_sigmoid(x):
    c = -2.0 * 0.7978845608028654 * LOG2E
    t = x * (c + (c * 0.044715) * (x * x))
    return x / (1.0 + jnp.exp2(t))


def _rms_rows(x, g):
    ms = jnp.mean(x * x, axis=-1, keepdims=True)
    return x * lax.rsqrt(ms + RMS_EPS) * g


def _norm_mm_kernel(*refs, act, has_bias, rope, emit_xn):
    it = iter(refs)
    x_ref, g_ref = next(it), next(it)
    wa_ref = next(it)
    wb_ref = next(it) if rope else None
    cos_ref = next(it) if rope else None
    sin_ref = next(it) if rope else None
    b_ref = next(it) if has_bias else None
    o_ref = next(it)
    xo_ref = next(it) if emit_xn else None
    xn_ref = next(it)

    @pl.when(pl.program_id(1) == 0)
    def _():
        xn = _rms_rows(x_ref[...], g_ref[...])
        xn_ref[...] = xn.astype(BF16)
        if emit_xn:
            xo_ref[...] = xn.T.astype(BF16)

    xn = xn_ref[...]
    y = jnp.dot(xn, wa_ref[...], preferred_element_type=F32)
    if rope:
        yb = jnp.dot(xn, wb_ref[...], preferred_element_type=F32)
        cos, sin = cos_ref[...], sin_ref[...]
        for s in range(y.shape[1] // LANES):
            sl = slice(s * LANES, (s + 1) * LANES)
            o_ref[:, sl] = (y[:, sl] * cos + yb[:, sl] * sin).astype(o_ref.dtype)
        return
    if has_bias:
        y = y + b_ref[...]
    if act == "sigmoid":
        y = jax.nn.sigmoid(y)
    elif act == "log_sigmoid":
        y = jax.nn.log_sigmoid(y)
    o_ref[...] = y.astype(o_ref.dtype)


def norm_matmul(x, gamma, wa, *, wb=None, cos=None, sin=None, table_of_tile=None, bias=None,
                act=None, out_dtype=BF16, emit_xn=False, tm=1024, tn=1024):
    T, D = x.shape
    N = wa.shape[1]
    tm, tn = min(tm, T), min(tn, N)
    assert T % tm == 0 and N % tn == 0 and tn % LANES == 0
    rope = wb is not None
    in_specs = [pl.BlockSpec((tm, D), lambda i, j: (i, 0)),
                pl.BlockSpec((1, D), lambda i, j: (0, 0)),
                pl.BlockSpec((D, tn), lambda i, j: (0, j))]
    args = [x, gamma.reshape(1, D).astype(F32), wa]
    if rope:
        S = cos.shape[1]
        assert S % tm == 0
        n_pos = S // tm
        tmap = table_of_tile if table_of_tile is not None else (lambda j: 0)
        in_specs += [pl.BlockSpec((D, tn), lambda i, j: (0, j)),
                     pl.BlockSpec((None, tm, LANES), lambda i, j: (tmap(j), i % n_pos, 0)),
                     pl.BlockSpec((None, tm, LANES), lambda i, j: (tmap(j), i % n_pos, 0))]
        args += [wb, cos, sin]
    if bias is not None:
        in_specs.append(pl.BlockSpec((1, tn), lambda i, j: (0, j)))
        args.append(bias.reshape(1, N).astype(F32))
    out_shape = [jax.ShapeDtypeStruct((T, N), out_dtype)]
    out_specs = [pl.BlockSpec((tm, tn), lambda i, j: (i, j))]
    if emit_xn:
        out_shape.append(jax.ShapeDtypeStruct((D, T), BF16))
        out_specs.append(pl.BlockSpec((D, tm), lambda i, j: (0, i)))
    res = pl.pallas_call(
        functools.partial(_norm_mm_kernel, act=act, has_bias=bias is not None, rope=rope, emit_xn=emit_xn),
        grid=(T // tm, N // tn),
        in_specs=in_specs,
        out_specs=out_specs,
        out_shape=out_shape,
        scratch_shapes=[pltpu.VMEM((tm, D), BF16)],
        compiler_params=_cparams(("parallel", "arbitrary")),
    )(*args)
    return res if emit_xn else res[0]


def _mm_res_kernel(a_ref, w_ref, r_ref, o_ref):
    o_ref[...] = r_ref[...] + jnp.dot(a_ref[...], w_ref[...], preferred_element_type=F32)


def matmul_residual(a, w, res, *, tm=1024, tn=1024):
    T, K = a.shape
    N = w.shape[1]
    tm, tn = min(tm, T), min(tn, N)
    assert T % tm == 0 and N % tn == 0
    return pl.pallas_call(
        _mm_res_kernel,
        grid=(T // tm, N // tn),
        in_specs=[pl.BlockSpec((tm, K), lambda i, j: (i, 0)),
                  pl.BlockSpec((K, tn), lambda i, j: (0, j)),
                  pl.BlockSpec((tm, tn), lambda i, j: (i, j))],
        out_specs=pl.BlockSpec((tm, tn), lambda i, j: (i, j)),
        out_shape=jax.ShapeDtypeStruct((T, N), F32),
        compiler_params=_cparams(("parallel", "arbitrary")),
    )(a, w, res)


def _rmsnorm_kernel(x_ref, g_ref, o_ref):
    o_ref[...] = _rms_rows(x_ref[...], g_ref[...])


def rmsnorm(x, gamma, *, tm=1024):
    T, D = x.shape
    tm = min(tm, T)
    return pl.pallas_call(
        _rmsnorm_kernel,
        grid=(T // tm,),
        in_specs=[pl.BlockSpec((tm, D), lambda i: (i, 0)), pl.BlockSpec((1, D), lambda i: (0, 0))],
        out_specs=pl.BlockSpec((tm, D), lambda i: (i, 0)),
        out_shape=jax.ShapeDtypeStruct((T, D), F32),
        compiler_params=_cparams(("parallel",)),
    )(x, gamma.reshape(1, D).astype(F32))


def _peer_cand_tables(tn):
    fidx, vmask = [], []
    for k2 in range(16):
        fidx.append(k2); vmask.append(0.0)
    for k1 in range(1, 8):
        lim = PEER_TOPK // (k1 + 1)
        for k2 in range(8):
            fidx.append(k1 * 16 + k2); vmask.append(0.0 if k2 < lim else -np.inf)
    for k1 in range(8, 16):
        fidx.append(k1 * 16); vmask.append(0.0)
    fidx = np.broadcast_to(np.asarray(fidx, np.int32)[:, None], (80, tn))
    vmask = np.broadcast_to(np.asarray(vmask, np.float32)[:, None], (80, tn))
    return jnp.asarray(fidx), jnp.asarray(vmask)


def _top16_rows(s, exact_ties):
    n, tn = s.shape
    rows = lax.broadcasted_iota(jnp.int32, (n, tn), 0)
    rows16 = lax.broadcasted_iota(jnp.int32, (PEER_TOPK, tn), 0)
    rank = jnp.full((n, tn), float(PEER_TOPK), F32)
    tops = jnp.zeros((PEER_TOPK, tn), F32)
    v = s
    for k in range(PEER_TOPK):
        m = jnp.max(v, axis=0, keepdims=True)
        if exact_ties:
            hit = rows == jnp.min(jnp.where(v == m, rows, n), axis=0, keepdims=True)
        else:
            hit = v == m
        rank = jnp.where(hit, float(k), rank)
        v = jnp.where(hit, -jnp.inf, v)
        tops = jnp.where(rows16 == k, m, tops)
    return tops, rank, v


def _peer_select_head(q_ref, keys_ref, fidx, vmask, exact_ties):
    tops, ranks, es, picked = [], [], [], []
    for p in range(2):
        q = q_ref[:, p * PEER_HALF_DIM:(p + 1) * PEER_HALF_DIM]
        s = lax.dot_general(keys_ref[p], q, _NT, preferred_element_type=F32)
        t, r, v = _top16_rows(s, exact_ties)
        tops.append(t); ranks.append(r)
        es.append(jnp.exp(s - t[0:1, :]))
        picked.append(jnp.sum(jnp.where(v == -jnp.inf, 1.0, 0.0), axis=0, keepdims=True))
    ts1, ts2 = tops
    pieces = [ts1[0:1, :] + ts2]
    for k1 in range(1, 8):
        pieces.append(ts1[k1:k1 + 1, :] + ts2[0:8, :])
    pieces.append(ts1[8:16, :] + ts2[0:1, :])
    cand0 = jnp.concatenate(pieces, axis=0) + vmask
    cand = cand0
    for _ in range(PEER_TOPK):
        m = jnp.max(cand, axis=0, keepdims=True)
        if exact_ties:
            hit = fidx == jnp.min(jnp.where(cand == m, fidx, 4096), axis=0, keepdims=True)
        else:
            hit = cand == m
        cand = jnp.where(hit, -jnp.inf, cand)
    taken = jnp.logical_and(cand == -jnp.inf, vmask == 0.0)
    takenf = taken.astype(F32)
    picked.append(jnp.sum(takenf, axis=0, keepdims=True))
    unique = jnp.min(jnp.where((picked[0] == PEER_TOPK) & (picked[1] == PEER_TOPK) & (picked[2] == PEER_TOPK),
                               1.0, 0.0)) > 0.5
    best = ts1[0:1, :] + ts2[0:1, :]
    z = jnp.sum(jnp.where(taken, jnp.exp(cand0 - best), 0.0), axis=0, keepdims=True)
    counts = [jnp.sum(takenf[0:16, :], axis=0, keepdims=True)]
    for k1 in range(1, 8):
        counts.append(jnp.sum(takenf[16 + 8 * (k1 - 1):16 + 8 * k1, :], axis=0, keepdims=True))
    tail = takenf[72:80, :]
    cnt = jnp.zeros_like(ranks[0])
    for k1 in range(PEER_TOPK):
        nk = counts[k1] if k1 < 8 else tail[k1 - 8:k1 - 7, :]
        cnt = jnp.where(ranks[0] == float(k1), nk, cnt)
    return (cnt, ranks[1], es[0], es[1] / z), unique


def _peer_select_kernel(q_ref, keys_ref, fidx_ref, vmask_ref, cnt_ref, rank2_ref, e1_ref, e2_ref):
    fidx = fidx_ref[...]
    vmask = vmask_ref[...]

    def store(vals):
        for ref, val in zip((cnt_ref, rank2_ref, e1_ref, e2_ref), vals):
            ref[...] = val.astype(ref.dtype)

    vals, unique = _peer_select_head(q_ref, keys_ref, fidx, vmask, exact_ties=False)
    store(vals)

    @pl.when(jnp.logical_not(unique))
    def _():
        store(_peer_select_head(q_ref, keys_ref, fidx, vmask, exact_ties=True)[0])


def peer_select(q, keys, *, tn=512):
    T = q.shape[0]
    tn = min(tn, T)
    fidx, vmask = _peer_cand_tables(tn)
    rows = PEER_HEADS * PEER_N_KEYS
    ospec = pl.BlockSpec((PEER_N_KEYS, tn), lambda i, h: (h, i))
    return pl.pallas_call(
        _peer_select_kernel,
        grid=(T // tn, PEER_HEADS),
        in_specs=[pl.BlockSpec((tn, 2 * PEER_HALF_DIM), lambda i, h: (i, h)),
                  pl.BlockSpec((2, PEER_N_KEYS, PEER_HALF_DIM), lambda i, h: (h, 0, 0)),
                  pl.BlockSpec((80, tn), lambda i, h: (0, 0)),
                  pl.BlockSpec((80, tn), lambda i, h: (0, 0))],
        out_specs=[ospec] * 4,
        out_shape=[jax.ShapeDtypeStruct((rows, T), dt) for dt in (F32, BF16, F32, BF16)],
        compiler_params=_cparams(("parallel", "parallel")),
    )(q, keys, fidx, vmask)


def _peer_dense_kernel(xn_ref, u_ref, vt_ref, cnt_ref, rank2_ref, e1_ref, e2_ref, res_ref, o_ref,
                       acc_ref, g0_ref, g1_ref, *, c_per_step):
    j = pl.program_id(1)
    n_tiles = pl.num_programs(1) - 1

    @pl.when(j == 0)
    def _():
        acc_ref[...] = jnp.zeros_like(acc_ref)
        g1_ref[...] = jnp.zeros_like(g1_ref)

    @pl.when(j % 2 == 0)
    def _():
        _peer_dense_step(xn_ref, u_ref, vt_ref, cnt_ref, rank2_ref, e1_ref, e2_ref, acc_ref,
                         g1_ref, g0_ref, j, c_per_step)

    @pl.when(j % 2 == 1)
    def _():
        _peer_dense_step(xn_ref, u_ref, vt_ref, cnt_ref, rank2_ref, e1_ref, e2_ref, acc_ref,
                         g0_ref, g1_ref, j, c_per_step)

    @pl.when(j == n_tiles)
    def _():
        o_ref[...] = res_ref[...] + acc_ref[...].T


def _peer_dense_step(xn_ref, u_ref, vt_ref, cnt_ref, rank2_ref, e1_ref, e2_ref, acc_ref, g_ref, g_next_ref,
                     j, c_per_step):

    tn = xn_ref.shape[1]
    bf16_rows = 16
    reps = PEER_N_KEYS // bf16_rows

    def row_tile(ref, row):
        r16 = jnp.broadcast_to(ref[pl.ds(row, 1), :], (bf16_rows, tn)).astype(BF16)
        return jnp.concatenate([r16] * reps, axis=0)

    c0 = jnp.maximum(j - 1, 0) * c_per_step
    up_rows = 2 * PEER_N_KEYS
    blocks = []
    for cc in range(c_per_step):
        if (cc * PEER_N_KEYS) % up_rows == 0:
            rs = slice(cc * PEER_N_KEYS, cc * PEER_N_KEYS + up_rows)
            hT = jnp.dot(u_ref[rs, :], xn_ref[...], preferred_element_type=F32)
            g_next_ref[rs, :] = _gelu_sigmoid(hT.astype(BF16))
        c = c0 + cc
        w = None
        for h in range(PEER_HEADS):
            row = h * PEER_N_KEYS + c
            n_row = row_tile(cnt_ref, row)
            e1_row = row_tile(e1_ref, row)
            sl = slice(h * PEER_N_KEYS, (h + 1) * PEER_N_KEYS)
            term = jnp.where(rank2_ref[sl, :] < n_row, e2_ref[sl, :], 0.0) * e1_row
            w = term if w is None else w + term
        blocks.append(w * g_ref[cc * PEER_N_KEYS:(cc + 1) * PEER_N_KEYS, :])
    aT = jnp.concatenate(blocks, axis=0)
    acc_ref[...] += jnp.dot(vt_ref[...], aT, preferred_element_type=F32)


def peer_dense(xn, u, vt, cnt, rank2, e1, e2, res, *, tn=512, te=1024):
    D, T = xn.shape
    E = u.shape[0]
    tn = min(tn, T)
    rows = PEER_HEADS * PEER_N_KEYS
    sel_spec = pl.BlockSpec((rows, tn), lambda i, j: (0, i))
    n_tiles = E // te
    return pl.pallas_call(
        functools.partial(_peer_dense_kernel, c_per_step=te // PEER_N_KEYS),
        grid=(T // tn, n_tiles + 1),
        in_specs=[pl.BlockSpec((D, tn), lambda i, j: (0, i)),
                  pl.BlockSpec((te, D), lambda i, j: (jnp.minimum(j, n_tiles - 1), 0)),
                  pl.BlockSpec((D, te), lambda i, j: (0, jnp.maximum(j - 1, 0))),
                  sel_spec, sel_spec, sel_spec, sel_spec,
                  pl.BlockSpec((tn, D), lambda i, j: (i, 0))],
        out_specs=pl.BlockSpec((tn, D), lambda i, j: (i, 0)),
        out_shape=jax.ShapeDtypeStruct((T, D), F32),
        scratch_shapes=[pltpu.VMEM((D, tn), F32), pltpu.VMEM((te, tn), BF16), pltpu.VMEM((te, tn), BF16)],
        compiler_params=_cparams(("parallel", "arbitrary")),
    )(xn, u, vt, cnt, rank2, e1, e2, res)


def peer_layer(h, gamma, w_q, sub_keys, u, v):
    q, xn = norm_matmul(h, gamma, w_q.astype(BF16), emit_xn=True)
    keys = sub_keys.reshape(2 * PEER_HEADS, PEER_N_KEYS, PEER_HALF_DIM).astype(BF16)
    cnt, rank2, e1, e2 = peer_select(q, keys)
    return peer_dense(xn, u.astype(BF16), v.T.astype(BF16), cnt, rank2, e1, e2, h)


def _cumsum_aug_kernel(lf_ref, tri_ref, place_q_ref, place_k_ref, ones_q_ref, ones_k_ref,
                       qa_ref, ka_ref, c_ref, carry_ref):
    @pl.when(pl.program_id(1) == 0)
    def _():
        carry_ref[...] = jnp.zeros_like(carry_ref)

    lf = lf_ref[0]
    c = jnp.dot(tri_ref[...], lf, preferred_element_type=F32, precision=lax.Precision.HIGHEST) + carry_ref[...]
    carry_ref[...] = c[-1:, :]
    c = c * LOG2E
    c_ref[0] = c
    hi = c.astype(BF16)
    r1 = c - hi.astype(F32)
    mid = r1.astype(BF16)
    lo = (r1 - mid.astype(F32)).astype(BF16)
    nh = N_HEADS
    lane = lax.broadcasted_iota(jnp.int32, c.shape, 1)
    parts = jnp.where(lane < nh, hi.astype(F32),
                      jnp.where(lane < 2 * nh, pltpu.roll(mid.astype(F32), nh, axis=1),
                                pltpu.roll(lo.astype(F32), 2 * nh, axis=1)))
    parts = jnp.where(lane < 3 * nh, parts, 0.0).astype(BF16)
    qa_ref[0] = (jnp.dot(parts, place_q_ref[...], preferred_element_type=F32) + ones_q_ref[...]).astype(BF16)
    ka_ref[0] = (jnp.dot(parts, place_k_ref[...], preferred_element_type=F32) + ones_k_ref[...]).astype(BF16)


def fox_bias_operands(logf, *, tc=256):
    B, S, _ = logf.shape
    nh = N_HEADS
    tri = jnp.asarray(np.tril(np.ones((tc, tc), np.float32)))
    pq = np.zeros((LANES, nh * LANES), np.float32)
    pk = np.zeros((LANES, nh * LANES), np.float32)
    oq = np.zeros((1, nh * LANES), np.float32)
    ok = np.zeros((1, nh * LANES), np.float32)
    for h in range(nh):
        for part in range(3):
            pq[part * nh + h, h * LANES + part] = 1.0
            pk[part * nh + h, h * LANES + 3 + part] = -1.0
            oq[0, h * LANES + 3 + part] = 1.0
            ok[0, h * LANES + part] = 1.0
    const = lambda a: pl.BlockSpec(a.shape, lambda b, i: (0,) * a.ndim)
    pq, pk, oq, ok = jnp.asarray(pq, BF16), jnp.asarray(pk, BF16), jnp.asarray(oq), jnp.asarray(ok)
    out = jax.ShapeDtypeStruct((B, S, nh * LANES), BF16)
    return pl.pallas_call(
        _cumsum_aug_kernel,
        grid=(B, S // tc),
        in_specs=[pl.BlockSpec((1, tc, LANES), lambda b, i: (b, i, 0)),
                  const(tri), const(pq), const(pk), const(oq), const(ok)],
        out_specs=[pl.BlockSpec((1, tc, nh * LANES), lambda b, i: (b, i, 0))] * 2
        + [pl.BlockSpec((1, tc, LANES), lambda b, i: (b, i, 0))],
        out_shape=[out, out, jax.ShapeDtypeStruct((B, S, LANES), F32)],
        scratch_shapes=[pltpu.VMEM((1, LANES), F32)],
        compiler_params=_cparams(("parallel", "arbitrary")),
    )(logf, tri, pq, pk, oq, ok)


def _fox_attn_kernel(first_ref, q_ref, qa_ref, kv_ref, ka_ref, o_ref, *, tq, tk, heads_per_step):
    qi = pl.program_id(2)
    t0 = qi * tq
    n_full = t0 // tk
    j_first = first_ref[(pl.program_id(0) * pl.num_programs(1) + pl.program_id(1)) * pl.num_programs(2) + qi]
    n_diag = tq // tk
    lanes = [slice(hh * LANES, (hh + 1) * LANES) for hh in range(heads_per_step)]
    qs = [jnp.concatenate([q_ref[0, :, lsl], qa_ref[0, :, lsl]], axis=1) for lsl in lanes]

    def step(j, carry, masked):
        rows = pl.ds(pl.multiple_of(j * tk, tk), tk)
        klane = lax.broadcasted_iota(jnp.int32, (tk, LANES), 1)
        new = []
        for lsl, q, (m, acc) in zip(lanes, qs, carry):
            kv = kv_ref[0, rows, lsl]
            kk = jnp.concatenate([kv, ka_ref[0, rows, lsl]], axis=1)
            ones_v = jnp.where(klane < HEAD_DIM, 1.0, kv).astype(BF16)
            s = lax.dot_general(q, kk, _NT, preferred_element_type=F32)
            if masked:
                qpos = t0 + lax.broadcasted_iota(jnp.int32, (tq, 1), 0)
                kpos = j * tk + lax.broadcasted_iota(jnp.int32, (1, tk), 1)
                s = jnp.where(kpos <= qpos, s, NEG_INF)
            m_new = jnp.maximum(m, jnp.max(s, axis=1, keepdims=True))
            p = jnp.exp2(s - m_new).astype(BF16)
            acc = jnp.exp2(m - m_new) * acc + jnp.dot(p, ones_v, preferred_element_type=F32)
            new.append((m_new, acc))
        return tuple(new)

    init = (jnp.full((tq, 1), NEG_INF, F32), jnp.zeros((tq, LANES), F32))
    n_pairs = (n_full - j_first) // 2
    carry = lax.fori_loop(
        0, n_pairs, lambda i, c: step(j_first + 2 * i + 1, step(j_first + 2 * i, c, False), False),
        (init,) * heads_per_step)
    carry = lax.fori_loop(j_first + 2 * n_pairs, n_full, functools.partial(step, masked=False), carry)
    for d in range(n_diag):
        carry = step(n_full + d, carry, True)
    outs = [acc / acc[:, 0:1] for (_, acc) in carry]
    lane = lax.broadcasted_iota(jnp.int32, (tq, LANES), 1)
    blocks = []
    for pair in range(heads_per_step // 2):
        a, b = outs[2 * pair], outs[2 * pair + 1]
        blocks.append(jnp.where(lane < HEAD_DIM, pltpu.roll(a, HEAD_DIM, axis=1), b))
    o_ref[0] = jnp.concatenate(blocks, axis=1).astype(o_ref.dtype) if len(blocks) > 1 else blocks[0].astype(o_ref.dtype)


def _head_slots(w, n_heads, second=None):
    D = w.shape[0]
    a = w.reshape(D, n_heads, HEAD_DIM)
    b = jnp.zeros_like(a) if second is None else second.reshape(D, n_heads, HEAD_DIM)
    return jnp.concatenate([a, b], axis=-1).reshape(D, n_heads * LANES)


def fox_layer(h, gamma, w_in, f_bias, w_out, B, S):
    aw = ATTN_WIDTH
    wq = _head_slots(w_in[:, :aw] * (HEAD_DIM ** -0.5 * LOG2E), N_HEADS)
    wkv = _head_slots(w_in[:, aw:2 * aw], N_HEADS, w_in[:, 2 * aw:3 * aw])
    w_main = jnp.concatenate([wq, wkv], axis=1).astype(BF16)
    wf = jnp.pad(w_in[:, 3 * aw:], ((0, 0), (0, LANES - N_HEADS))).astype(BF16)
    bf = jnp.pad(f_bias.astype(F32), (0, LANES - N_HEADS))
    qkv = norm_matmul(h, gamma, w_main)
    logf = norm_matmul(h, gamma, wf, bias=bf, act="log_sigmoid", out_dtype=F32)
    qa, ka, c2 = fox_bias_operands(logf.reshape(B, S, LANES))
    qkv = qkv.reshape(B, S, 2 * N_HEADS * LANES)
    o = fox_attention(qkv, qa, ka, c2)
    return matmul_residual(o.reshape(B * S, aw), w_out.astype(BF16), h)


FOX_NEGLIGIBLE_LOG2 = 160.0


def _norm_maxima_kernel(x_ref, ind_ref, o_ref):
    x = x_ref[0].astype(F32)
    ss = jnp.dot((x * x).astype(BF16), ind_ref[...], preferred_element_type=F32)
    o_ref[0, 0] = jnp.broadcast_to(jnp.max(ss, axis=0, keepdims=True), o_ref.shape[2:])


def fox_norm_maxima(qkv, tile):
    B, S, W = qkv.shape
    n = S // tile
    col = np.arange(W)
    slot, lane = col // LANES, col % LANES
    used = (slot < N_HEADS) | (lane < HEAD_DIM)
    ind = jnp.asarray((slot[:, None] == np.arange(LANES)[None, :]) & used[:, None], BF16)
    rows = min(256, tile)
    ss = pl.pallas_call(
        _norm_maxima_kernel,
        grid=(B, S // rows),
        in_specs=[pl.BlockSpec((1, rows, W), lambda b, i: (b, i, 0)),
                  pl.BlockSpec(ind.shape, lambda b, i: (0, 0))],
        out_specs=pl.BlockSpec((1, 1, 8, LANES), lambda b, i: (b, i, 0, 0)),
        out_shape=jax.ShapeDtypeStruct((B, S // rows, 8, LANES), F32),
        compiler_params=_cparams(("parallel", "parallel")),
    )(qkv, ind)
    ss = jnp.max(ss[:, :, 0, :2 * N_HEADS].reshape(B, n, tile // rows, 2 * N_HEADS), axis=2)
    return jnp.sqrt(ss * 1.01)


def fox_first_chunk(qkv, c2, tile, heads_per_step):
    B, S, _ = qkv.shape
    n = S // tile
    norms = fox_norm_maxima(qkv, tile)
    qmax, kmax = norms[..., :N_HEADS], norms[..., N_HEADS:]
    c = c2[..., :N_HEADS].reshape(B, n, tile, N_HEADS)
    cmax, cmin = jnp.max(c, axis=2), jnp.min(c, axis=2)
    upper = qmax[:, :, None] * kmax[:, None, :] + cmax[:, :, None] - cmin[:, None, :]
    own = -(qmax * kmax)
    earlier = jnp.arange(n)[None, :, None, None] > jnp.arange(n)[None, None, :, None]
    skip = (upper < own[:, :, None] - FOX_NEGLIGIBLE_LOG2) & earlier
    first = jnp.sum(jnp.cumprod(skip.astype(jnp.int32), axis=2), axis=2)
    first = jnp.min(first.reshape(B, n, N_HEADS // heads_per_step, heads_per_step), axis=-1)
    return jnp.transpose(first, (0, 2, 1)).reshape(-1).astype(jnp.int32)


def fox_attention(qkv, qa, ka, c2, *, tq=1024, tk=1024, heads_per_step=2):
    B, S, _ = qkv.shape
    tq = tk = min(tq, S)
    hs = heads_per_step
    wq = hs * LANES
    n_qblk = N_HEADS // hs
    first = fox_first_chunk(qkv, c2, tk, hs)
    grid_spec = pltpu.PrefetchScalarGridSpec(
        num_scalar_prefetch=1,
        grid=(B, n_qblk, S // tq),
        in_specs=[pl.BlockSpec((1, tq, wq), lambda b, h, i, first: (b, i, h)),
                  pl.BlockSpec((1, tq, wq), lambda b, h, i, first: (b, i, h)),
                  pl.BlockSpec((1, S, wq), lambda b, h, i, first: (b, 0, n_qblk + h)),
                  pl.BlockSpec((1, S, wq), lambda b, h, i, first: (b, 0, h))],
        out_specs=pl.BlockSpec((1, tq, hs * HEAD_DIM), lambda b, h, i, first: (b, i, h)))
    return pl.pallas_call(
        functools.partial(_fox_attn_kernel, tq=tq, tk=tk, heads_per_step=hs),
        grid_spec=grid_spec,
        out_shape=jax.ShapeDtypeStruct((B, S, ATTN_WIDTH), BF16),
        compiler_params=_cparams(("parallel", "parallel", "arbitrary")),
    )(first, qkv, qa, qkv, ka)


def _rot_half_cols(w):
    D = w.shape[0]
    a = w.reshape(D, -1, HEAD_DIM)
    half = HEAD_DIM // 2
    return jnp.concatenate([-a[..., half:], a[..., :half]], axis=-1).reshape(w.shape)


def _rope_tables(S):
    half = HEAD_DIM // 2
    inv_freq = ROPE_THETA ** (-jnp.arange(half, dtype=F32) / half)
    ang = jnp.arange(S, dtype=F32)[:, None] * inv_freq[None, :]
    c, s = jnp.cos(ang), jnp.sin(ang)
    c2, s2 = jnp.concatenate([c, c], axis=1), jnp.concatenate([s, s], axis=1)
    cos = jnp.stack([jnp.concatenate([c2, jnp.ones_like(c2)], axis=1), jnp.concatenate([c2, c2], axis=1)])
    sin = jnp.stack([jnp.concatenate([s2, jnp.zeros_like(s2)], axis=1), jnp.concatenate([s2, s2], axis=1)])
    return cos, sin


def _compress_kernel(x_ref, pea_ref, peb_ref, wa_ref, wb_ref, w2_ref, o_ref, pa_ref, pb0_ref, *, n_rows):
    u = pl.program_id(1)
    x = x_ref[0].astype(F32)
    pa = jnp.dot((x + pea_ref[...]).astype(BF16), wa_ref[...], preferred_element_type=F32)
    pb = jnp.dot((x + peb_ref[...]).astype(BF16), wb_ref[...], preferred_element_type=F32)

    def emit(slab, hid):
        y = jnp.dot(_gelu_tanh(hid).astype(BF16), w2_ref[...], preferred_element_type=F32)
        o_ref[0, pl.ds(pl.multiple_of(slab * n_rows, n_rows), n_rows), :] = y

    @pl.when(u == 0)
    def _():
        pb0_ref[...] = pb

    @pl.when(u > 0)
    def _():
        emit(u - 1, pa_ref[...] + pb)

    @pl.when(u == 3)
    def _():
        emit(3, pa + pltpu.roll(pb0_ref[...], n_rows - 1, axis=0))

    pa_ref[...] = pa


def nsa_compress(src, pe, w1, w2):
    B, S, W = src.shape
    G = NSA_GROUPS
    n_rows = S // 64
    half = CMP_BLOCK // 2
    cw = half * W
    xv = src.reshape(B, n_rows, 4 * cw)
    pe_flat = jnp.transpose(pe, (1, 0, 2)).reshape(CMP_BLOCK, W).astype(F32)
    pea, peb = pe_flat[:half].reshape(1, cw), pe_flat[half:].reshape(1, cw)
    eye = jnp.eye(G, dtype=F32)
    wfull = jnp.einsum('gldh,gk->lkdgh', w1.astype(F32), eye).reshape(CMP_BLOCK, W, G * CMP_HIDDEN)
    wa = wfull[:half].reshape(cw, G * CMP_HIDDEN).astype(BF16)
    wb = wfull[half:].reshape(cw, G * CMP_HIDDEN).astype(BF16)
    w2bd = jnp.einsum('ghd,gk->ghkd', w2.astype(F32), eye).reshape(G * CMP_HIDDEN, W).astype(BF16)
    const = lambda a: pl.BlockSpec(a.shape, lambda b, u: (0,) * a.ndim)
    return pl.pallas_call(
        functools.partial(_compress_kernel, n_rows=n_rows),
        grid=(B, 4),
        in_specs=[pl.BlockSpec((1, n_rows, cw), lambda b, u: (b, 0, u)),
                  const(pea), const(peb), const(wa), const(wb), const(w2bd)],
        out_specs=pl.BlockSpec((1, 4 * n_rows, W), lambda b, u: (b, 0, 0)),
        out_shape=jax.ShapeDtypeStruct((B, 4 * n_rows, W), F32),
        scratch_shapes=[pltpu.VMEM((n_rows, G * CMP_HIDDEN), F32), pltpu.VMEM((n_rows, G * CMP_HIDDEN), F32)],
        compiler_params=_cparams(("parallel", "arbitrary")),
    )(xv, pea, peb, wa, wb, w2bd)


def _nsa_attn_kernel(q_ref, kvs_ref, kvw_ref, kvc_ref, kvct_ref, gate_ref, bmat_ref, pmat_ref, o_ref,
                     *, tq, ts, tk, seq):
    R = NSA_Q_PER_GROUP
    n_slc = seq // SLC_BLOCK
    n_sel = min(SLC_TOPK, n_slc)
    assert ts & (ts - 1) == 0 and n_slc & (n_slc - 1) == 0 and tk % tq == 0 and tq % ts == 0
    log_slc = n_slc.bit_length() - 1
    qi = pl.program_id(2)
    t0 = qi * tq

    local = [_nsa_local_branches(q_ref, kvw_ref, kvc_ref, kvct_ref, t0 + i * ts, i * ts, ts, n_slc, n_sel, log_slc)
             for i in range(tq // ts)]
    o_c = jnp.concatenate([o[0][r * ts:(r + 1) * ts] for r in range(R) for o in local], axis=0)
    o_w = jnp.concatenate([o[1][r * ts:(r + 1) * ts] for r in range(R) for o in local], axis=0)
    sel_bias = jnp.concatenate([o[2] for o in local], axis=0)
    qs = jnp.concatenate([q_ref[0, :, r * LANES:(r + 1) * LANES] for r in range(R)], axis=0)
    _nsa_selected_and_combine(qs, o_c, o_w, sel_bias, kvs_ref, gate_ref, bmat_ref, pmat_ref, o_ref,
                              t0=t0, tq=tq, tk=tk, seq=seq)


def _nsa_local_branches(q_ref, kvw_ref, kvc_ref, kvct_ref, t0, row0, tq, n_slc, n_sel, log_slc):
    R = NSA_Q_PER_GROUP
    M = R * tq
    qs = jnp.concatenate([q_ref[0, row0:row0 + tq, r * LANES:(r + 1) * LANES] for r in range(R)], axis=0)

    wlen = WINDOW + tq
    start = jnp.maximum(t0 - WINDOW, 0)
    kvw = kvw_ref[0, pl.ds(pl.multiple_of(start, tq), wlen), :]
    s_w = lax.dot_general(qs, kvw, _NT, preferred_element_type=F32)
    qpos = t0 + (lax.broadcasted_iota(jnp.int32, (M, 1), 0) & (tq - 1))
    kpos = start + lax.broadcasted_iota(jnp.int32, (1, wlen), 1)
    s_w = jnp.where(kpos <= qpos, jnp.where(kpos > qpos - WINDOW, s_w, NEG_INF), NEG_INF)
    p_w = jnp.exp2(s_w - jnp.max(s_w, axis=1, keepdims=True)).astype(BF16)
    wlane = lax.broadcasted_iota(jnp.int32, (wlen, LANES), 1)
    acc_w = jnp.dot(p_w, jnp.where(wlane < HEAD_DIM, 1.0, kvw).astype(BF16), preferred_element_type=F32)
    o_w = acc_w / acc_w[:, 0:1]

    n_groups, gs = _nsa_row_groups(n_slc)
    o_c, sel_bias = _nsa_compressed_and_select(qs, kvc_ref, kvct_ref, t0, tq, n_slc, n_sel, gs, n_groups)
    return o_c, o_w, sel_bias


def _nsa_row_groups(n_slc):
    n_groups = max(1, min(4, n_slc // 8))
    assert n_slc % n_groups == 0 and (n_slc // n_groups) % 8 == 0
    return n_groups, n_slc // n_groups


def _nsa_compressed_and_select(qs, kvc_ref, kvct_ref, t0, tq, n_slc, n_sel, gs, n_groups):
    R = NSA_Q_PER_GROUP
    M = R * tq
    n_rows, n_j = n_groups * 4 * gs, n_groups * gs
    log_gs = gs.bit_length() - 1
    sT = lax.dot_general(kvc_ref[0, 0, 0:n_rows, :], qs, _NT, preferred_element_type=F32)
    rowc = lax.broadcasted_iota(jnp.int32, (n_rows, 1), 0)
    blk_j = ((rowc >> (log_gs + 2)) << log_gs) + (rowc & (gs - 1))
    blk_u = (rowc >> log_gs) & 3
    cmp_end = blk_j * SLC_BLOCK + blk_u * CMP_STRIDE + (CMP_BLOCK - 1)
    tcol = t0 + (lax.broadcasted_iota(jnp.int32, (1, M), 1) & (tq - 1))
    sm = jnp.where(cmp_end <= tcol, sT, NEG_INF)
    mx = jnp.max(sm, axis=0, keepdims=True)
    e = jnp.exp2(sm - mx)
    inv = jnp.where(mx > 0.5 * NEG_INF, 1.0 / jnp.sum(e, axis=0, keepdims=True), 0.0)
    pT = e * inv
    o_c = jnp.dot(kvct_ref[0, 0, :, 0:n_rows], pT.astype(BF16), preferred_element_type=F32).T

    psum = pT[:, 0:tq]
    for r in range(1, R):
        psum = psum + pT[:, r * tq:(r + 1) * tq]
    p0, p1, p2, p3 = (jnp.concatenate([psum[(4 * g + u) * gs:(4 * g + u + 1) * gs, :] for g in range(n_groups)],
                                      axis=0) for u in range(4))
    jrow = lax.broadcasted_iota(jnp.int32, (n_j, tq), 0)
    p3_prev = jnp.where(jrow == 0, 0.0, pltpu.roll(p3, 1, axis=0))
    imp = p0 + p1 + p2 + 0.5 * p3 + 0.5 * p3_prev
    cur = (t0 + lax.broadcasted_iota(jnp.int32, (n_j, tq), 1)) >> (SLC_BLOCK.bit_length() - 1)
    forced = (jrow == 0) | (jrow == cur) | (jrow == cur - 1)
    vals = jnp.where(forced, -jnp.inf, jnp.where(jrow <= cur, imp, -jnp.inf))
    sel0 = jnp.where(forced, 1.0, 0.0)
    n_free = n_sel - 3

    def pick(exact_ties):
        v, sel = vals, sel0
        for _ in range(n_free):
            m = jnp.max(v, axis=0, keepdims=True)
            if exact_ties:
                hit = jrow == jnp.min(jnp.where(v == m, jrow, n_slc), axis=0, keepdims=True)
            else:
                hit = v == jnp.where(m == -jnp.inf, jnp.nan, m)
            sel = jnp.where(hit, 1.0, sel)
            v = jnp.where(hit, -jnp.inf, v)
        return sel

    sel_fast = pick(False)
    n_cand = jnp.sum(jnp.where(vals > -jnp.inf, 1.0, 0.0), axis=0, keepdims=True)
    n_picked = jnp.sum(sel_fast - sel0, axis=0, keepdims=True)
    unique = jnp.min(jnp.where(n_picked == jnp.minimum(n_cand, float(n_free)), 1.0, 0.0)) > 0.5
    sel = lax.cond(unique, lambda: sel_fast, lambda: pick(True))
    if n_j < n_slc:
        sel = jnp.concatenate([sel, jnp.zeros((n_slc - n_j, tq), F32)], axis=0)
    sel_bias = ((sel.T - 1.0) * (-NEG_INF)).astype(BF16)
    return o_c, sel_bias


def _nsa_selected_and_combine(qs, o_c, o_w, sel_bias, kvs_ref, gate_ref, bmat_ref, pmat_ref, o_ref,
                              *, t0, tq, tk, seq):
    R = NSA_Q_PER_GROUP
    n_slc = seq // SLC_BLOCK
    blocks_per_chunk = tk // SLC_BLOCK
    n_parts = 2
    hp = R // n_parts
    mp = hp * tq
    qparts = [qs[i * mp:(i + 1) * mp] for i in range(n_parts)]
    qrow = t0 + (lax.broadcasted_iota(jnp.int32, (mp, 1), 0) & (tq - 1))
    kcol = lax.broadcasted_iota(jnp.int32, (1, tk), 1)
    n_chunks = seq // tk
    j_last = t0 // tk

    klane = lax.broadcasted_iota(jnp.int32, (tk, LANES), 1)

    def slc_step(j, carry, masked):
        kv = kvs_ref[0, pl.ds(pl.multiple_of(j * tk, tk), tk), :]
        k_sel = jnp.where(klane < HEAD_DIM, kv, bmat_ref[...]).astype(BF16)
        ones_v = jnp.where(klane < HEAD_DIM, 1.0, kv).astype(BF16)
        off = pl.multiple_of(blocks_per_chunk * (n_chunks - 1 - j), blocks_per_chunk)
        place = pmat_ref[pl.ds(off, n_slc), :].astype(BF16)
        q_bias = jnp.dot(sel_bias, place, preferred_element_type=F32).astype(BF16)
        q_bias = jnp.concatenate([q_bias] * hp, axis=0)
        new = []
        for qp, (m, acc) in zip(qparts, carry):
            s = lax.dot_general(qp + q_bias, k_sel, _NT, preferred_element_type=F32)
            if masked:
                s = jnp.where(j * tk + kcol <= qrow, s, NEG_INF)
            m_new = jnp.maximum(m, jnp.max(s, axis=1, keepdims=True))
            p = jnp.exp2(s - m_new).astype(BF16)
            acc = jnp.exp2(m - m_new) * acc + jnp.dot(p, ones_v, preferred_element_type=F32)
            new.append((m_new, acc))
        return tuple(new)

    init = (jnp.full((mp, 1), NEG_INF, F32), jnp.zeros((mp, LANES), F32))
    n_pairs = j_last // 2
    carry = lax.fori_loop(0, n_pairs, lambda i, c: slc_step(2 * i + 1, slc_step(2 * i, c, False), False),
                          (init,) * n_parts)
    carry = lax.fori_loop(2 * n_pairs, j_last, functools.partial(slc_step, masked=False), carry)
    carry = slc_step(j_last, carry, True)
    o_s = jnp.concatenate([acc / acc[:, 0:1] for (_, acc) in carry], axis=0)

    gates = gate_ref[0]
    lane = lax.broadcasted_iota(jnp.int32, (tq, LANES), 1)
    comb = []
    for r in range(R):
        rs = slice(r * tq, (r + 1) * tq)
        comb.append(gates[:, r:r + 1] * o_c[rs] + gates[:, R + r:R + r + 1] * o_s[rs]
                    + gates[:, 2 * R + r:2 * R + r + 1] * o_w[rs])
    out = [jnp.where(lane < HEAD_DIM, pltpu.roll(comb[2 * i], HEAD_DIM, axis=1), comb[2 * i + 1])
           for i in range(R // 2)]
    o_ref[0] = jnp.concatenate(out, axis=1).astype(o_ref.dtype)


def nsa_attention(qkv, kvc, kvct, gates, *, tq=512, ts=256, tk=1024):
    B, S, _ = qkv.shape
    G, R = NSA_GROUPS, NSA_Q_PER_GROUP
    tk = min(tk, S)
    n_slc = S // SLC_BLOCK
    n_cmp = kvc.shape[2]
    bpc = tk // SLC_BLOCK
    assert bpc <= LANES - HEAD_DIM
    off = bpc * (S // tk - 1)
    lane = np.arange(LANES)[None, :]
    bmat = jnp.asarray(lane - HEAD_DIM == np.arange(tk)[:, None] // SLC_BLOCK, BF16)
    pmat = jnp.asarray((np.arange(n_slc + off)[:, None] - off == lane - HEAD_DIM) & (lane >= HEAD_DIM)
                       & (lane < HEAD_DIM + bpc), F32)
    slc_blk0, win_blk0 = N_HEADS, N_HEADS + G
    return pl.pallas_call(
        functools.partial(_nsa_attn_kernel, tq=tq, ts=ts, tk=tk, seq=S),
        grid=(B, G, S // tq),
        in_specs=[pl.BlockSpec((1, tq, R * LANES), lambda b, g, i: (b, i, g)),
                  pl.BlockSpec((1, S, LANES), lambda b, g, i: (b, 0, slc_blk0 + g)),
                  pl.BlockSpec((1, S, LANES), lambda b, g, i: (b, 0, win_blk0 + g)),
                  pl.BlockSpec((1, 1, n_cmp, LANES), lambda b, g, i: (b, g, 0, 0)),
                  pl.BlockSpec((1, 1, LANES, n_cmp), lambda b, g, i: (b, g, 0, 0)),
                  pl.BlockSpec((1, tq, LANES), lambda b, g, i: (b, i, g)),
                  pl.BlockSpec(bmat.shape, lambda b, g, i: (0, 0)),
                  pl.BlockSpec(pmat.shape, lambda b, g, i: (0, 0))],
        out_specs=pl.BlockSpec((1, tq, R * HEAD_DIM), lambda b, g, i: (b, i, g)),
        out_shape=jax.ShapeDtypeStruct((B, S, ATTN_WIDTH), BF16),
        compiler_params=_cparams(("parallel", "parallel", "arbitrary")),
    )(qkv, qkv, qkv, kvc, kvct, gates, bmat, pmat)


def nsa_layer(h, gamma, w_in, pe_k, w1_k, w2_k, pe_v, w1_v, w2_v, w_out, B, S):
    G, R, hd, aw = NSA_GROUPS, NSA_Q_PER_GROUP, HEAD_DIM, ATTN_WIDTH
    kvd = G * hd
    sec = lambda i: w_in[:, aw + i * kvd: aw + (i + 1) * kvd]
    wq = w_in[:, :aw] * (hd ** -0.5 * LOG2E)
    wa = jnp.concatenate([_head_slots(wq, N_HEADS), _head_slots(sec(2), G, sec(3)),
                          _head_slots(sec(4), G, sec(5))], axis=1).astype(BF16)
    wb = jnp.concatenate([_head_slots(_rot_half_cols(wq), N_HEADS), _head_slots(_rot_half_cols(sec(2)), G),
                          _head_slots(_rot_half_cols(sec(4)), G)], axis=1).astype(BF16)
    cos, sin = _rope_tables(S)
    qkv = norm_matmul(h, gamma, wa, wb=wb, cos=cos, sin=sin)
    kc_src = norm_matmul(h, gamma, sec(0).astype(BF16), wb=_rot_half_cols(sec(0)).astype(BF16),
                         cos=cos, sin=sin, table_of_tile=lambda j: 1)
    vc_src = norm_matmul(h, gamma, sec(1).astype(BF16))
    wg = w_in[:, aw + 6 * kvd:].reshape(-1, 3, G, R)
    wg = jnp.transpose(wg, (0, 2, 1, 3)).reshape(-1, G, 3 * R)
    wg = jnp.pad(wg, ((0, 0), (0, 0), (0, LANES - 3 * R))).reshape(-1, G * LANES).astype(BF16)
    gates = norm_matmul(h, gamma, wg, act="sigmoid", out_dtype=F32)
    kc = nsa_compress(kc_src.reshape(B, S, kvd), pe_k, w1_k, w2_k)
    vc = nsa_compress(vc_src.reshape(B, S, kvd), pe_v, w1_v, w2_v)
    n_cmp = kc.shape[1]
    kvc = jnp.concatenate([kc.reshape(B, n_cmp, G, hd), vc.reshape(B, n_cmp, G, hd)], axis=-1)
    n_groups, gs = _nsa_row_groups(S // SLC_BLOCK)
    kvc = kvc.reshape(B, 4, n_groups, gs, G, 2 * hd)
    kvc = jnp.transpose(kvc, (0, 4, 2, 1, 3, 5)).reshape(B, G, n_cmp, 2 * hd).astype(BF16)
    kvct = jnp.swapaxes(kvc, 2, 3)
    o = nsa_attention(qkv.reshape(B, S, -1), kvc, kvct, gates.reshape(B, S, G * LANES))
    return matmul_residual(o.reshape(B * S, aw), w_out.astype(BF16), h)


def kernel(x, l0_attn_norm, l0_w_in, l0_cmp_pe_k, l0_cmp_w1_k, l0_cmp_w2_k, l0_cmp_pe_v, l0_cmp_w1_v,
           l0_cmp_w2_v, l0_w_out, l0_ffn_norm, l0_peer_wq, l0_peer_keys, l0_peer_u, l0_peer_v,
           l1_attn_norm, l1_w_in, l1_f_bias, l1_w_out, l1_ffn_norm, l1_peer_wq, l1_peer_keys, l1_peer_u,
           l1_peer_v, final_norm):
    B, S, D = x.shape
    h = x.reshape(B * S, D)
    h = nsa_layer(h, l0_attn_norm, l0_w_in, l0_cmp_pe_k, l0_cmp_w1_k, l0_cmp_w2_k, l0_cmp_pe_v, l0_cmp_w1_v,
                  l0_cmp_w2_v, l0_w_out, B, S)
    h = peer_layer(h, l0_ffn_norm, l0_peer_wq, l0_peer_keys, l0_peer_u, l0_peer_v)
    h = fox_layer(h, l1_attn_norm, l1_w_in, l1_f_bias, l1_w_out, B, S)
    h = peer_layer(h, l1_ffn_norm, l1_peer_wq, l1_peer_keys, l1_peer_u, l1_peer_v)
    return rmsnorm(h, final_norm).reshape(B, S, D)
```

```python
import functools

import numpy as np
import jax
import jax.numpy as jnp
from jax import lax
from jax.experimental import pallas as pl
from jax.experimental.pallas import tpu as pltpu

F32 = jnp.float32
BF16 = jnp.bfloat16

D_MODEL = 1024
N_HEADS = 16
HEAD_DIM = 64
ATTN_WIDTH = N_HEADS * HEAD_DIM
NSA_GROUPS = 4
NSA_Q_PER_GROUP = N_HEADS // NSA_GROUPS
CMP_BLOCK = 32
CMP_STRIDE = 16
CMP_HIDDEN = 2 * HEAD_DIM
SLC_BLOCK = 64
SLC_TOPK = 16
WINDOW = 512
FORCE_SCORE = 1.0e4
ROPE_THETA = 10000.0
PEER_HEADS = 8
PEER_N_KEYS = 128
PEER_TOPK = 16
PEER_HALF_DIM = 128
RMS_EPS = 1e-6
NEG_INF = -1e30
LOG2E = 1.4426950408889634

LANES = 128
VMEM_LIMIT_BYTES = 56 * 1024 * 1024

_NT = (((1,), (1,)), ((), ()))


def _cparams(sem, vmem=VMEM_LIMIT_BYTES, flags=None):
    return pltpu.CompilerParams(dimension_semantics=sem, vmem_limit_bytes=vmem, flags=flags)


def _gelu_tanh(x):
    return 0.5 * x * (1.0 + jnp.tanh(0.7978845608028654 * (x + 0.044715 * (x * x * x))))


def _gelu_sigmoid(x):
    c = -2.0 * 0.7978845608028654 * LOG2E
    t = x * (c + (c * 0.044715) * (x * x))
    return x / (1.0 + jnp.exp2(t))


def _rms_rows(x, g):
    ms = jnp.mean(x * x, axis=-1, keepdims=True)
    return x * lax.rsqrt(ms + RMS_EPS) * g


def _norm_mm_kernel(*refs, act, has_bias, rope, emit_xn):
    it = iter(refs)
    x_ref, g_ref = next(it), next(it)
    wa_ref = next(it)
    wb_ref = next(it) if rope else None
    cos_ref = next(it) if rope else None
    sin_ref = next(it) if rope else None
    b_ref = next(it) if has_bias else None
    o_ref = next(it)
    xo_ref = next(it) if emit_xn else None
    xn_ref = next(it)

    @pl.when(pl.program_id(1) == 0)
    def _():
        xn = _rms_rows(x_ref[...], g_ref[...])
        xn_ref[...] = xn.astype(BF16)
        if emit_xn:
            xo_ref[...] = xn.T.astype(BF16)

    xn = xn_ref[...]
    y = jnp.dot(xn, wa_ref[...], preferred_element_type=F32)
    if rope:
        yb = jnp.dot(xn, wb_ref[...], preferred_element_type=F32)
        cos, sin = cos_ref[...], sin_ref[...]
        for s in range(y.shape[1] // LANES):
            sl = slice(s * LANES, (s + 1) * LANES)
            o_ref[:, sl] = (y[:, sl] * cos + yb[:, sl] * sin).astype(o_ref.dtype)
        return
    if has_bias:
        y = y + b_ref[...]
    if act == "sigmoid":
        y = jax.nn.sigmoid(y)
    elif act == "log_sigmoid":
        y = jax.nn.log_sigmoid(y)
    o_ref[...] = y.astype(o_ref.dtype)


def norm_matmul(x, gamma, wa, *, wb=None, cos=None, sin=None, table_of_tile=None, bias=None,
                act=None, out_dtype=BF16, emit_xn=False, tm=1024, tn=1024):
    T, D = x.shape
    N = wa.shape[1]
    tm, tn = min(tm, T), min(tn, N)
    assert T % tm == 0 and N % tn == 0 and tn % LANES == 0
    rope = wb is not None
    in_specs = [pl.BlockSpec((tm, D), lambda i, j: (i, 0)),
                pl.BlockSpec((1, D), lambda i, j: (0, 0)),
                pl.BlockSpec((D, tn), lambda i, j: (0, j))]
    args = [x, gamma.reshape(1, D).astype(F32), wa]
    if rope:
        S = cos.shape[1]
        assert S % tm == 0
        n_pos = S // tm
        tmap = table_of_tile if table_of_tile is not None else (lambda j: 0)
        in_specs += [pl.BlockSpec((D, tn), lambda i, j: (0, j)),
                     pl.BlockSpec((None, tm, LANES), lambda i, j: (tmap(j), i % n_pos, 0)),
                     pl.BlockSpec((None, tm, LANES), lambda i, j: (tmap(j), i % n_pos, 0))]
        args += [wb, cos, sin]
    if bias is not None:
        in_specs.append(pl.BlockSpec((1, tn), lambda i, j: (0, j)))
        args.append(bias.reshape(1, N).astype(F32))
    out_shape = [jax.ShapeDtypeStruct((T, N), out_dtype)]
    out_specs = [pl.BlockSpec((tm, tn), lambda i, j: (i, j))]
    if emit_xn:
        out_shape.append(jax.ShapeDtypeStruct((D, T), BF16))
        out_specs.append(pl.BlockSpec((D, tm), lambda i, j: (0, i)))
    res = pl.pallas_call(
        functools.partial(_norm_mm_kernel, act=act, has_bias=bias is not None, rope=rope, emit_xn=emit_xn),
        grid=(T // tm, N // tn),
        in_specs=in_specs,
        out_specs=out_specs,
        out_shape=out_shape,
        scratch_shapes=[pltpu.VMEM((tm, D), BF16)],
        compiler_params=_cparams(("parallel", "arbitrary")),
    )(*args)
    return res if emit_xn else res[0]


def _mm_res_kernel(a_ref, w_ref, r_ref, o_ref):
    o_ref[...] = r_ref[...] + jnp.dot(a_ref[...], w_ref[...], preferred_element_type=F32)


def matmul_residual(a, w, res, *, tm=1024, tn=1024):
    T, K = a.shape
    N = w.shape[1]
    tm, tn = min(tm, T), min(tn, N)
    assert T % tm == 0 and N % tn == 0
    return pl.pallas_call(
        _mm_res_kernel,
        grid=(T // tm, N // tn),
        in_specs=[pl.BlockSpec((tm, K), lambda i, j: (i, 0)),
                  pl.BlockSpec((K, tn), lambda i, j: (0, j)),
                  pl.BlockSpec((tm, tn), lambda i, j: (i, j))],
        out_specs=pl.BlockSpec((tm, tn), lambda i, j: (i, j)),
        out_shape=jax.ShapeDtypeStruct((T, N), F32),
        compiler_params=_cparams(("parallel", "arbitrary")),
    )(a, w, res)


def _rmsnorm_kernel(x_ref, g_ref, o_ref):
    o_ref[...] = _rms_rows(x_ref[...], g_ref[...])


def rmsnorm(x, gamma, *, tm=1024):
    T, D = x.shape
    tm = min(tm, T)
    return pl.pallas_call(
        _rmsnorm_kernel,
        grid=(T // tm,),
        in_specs=[pl.BlockSpec((tm, D), lambda i: (i, 0)), pl.BlockSpec((1, D), lambda i: (0, 0))],
        out_specs=pl.BlockSpec((tm, D), lambda i: (i, 0)),
        out_shape=jax.ShapeDtypeStruct((T, D), F32),
        compiler_params=_cparams(("parallel",)),
    )(x, gamma.reshape(1, D).astype(F32))


def _peer_cand_tables(tn):
    fidx, vmask = [], []
    for k2 in range(16):
        fidx.append(k2); vmask.append(0.0)
    for k1 in range(1, 8):
        lim = PEER_TOPK // (k1 + 1)
        for k2 in range(8):
            fidx.append(k1 * 16 + k2); vmask.append(0.0 if k2 < lim else -np.inf)
    for k1 in range(8, 16):
        fidx.append(k1 * 16); vmask.append(0.0)
    fidx = np.broadcast_to(np.asarray(fidx, np.int32)[:, None], (80, tn))
    vmask = np.broadcast_to(np.asarray(vmask, np.float32)[:, None], (80, tn))
    return jnp.asarray(fidx), jnp.asarray(vmask)


def _top16_rows(s, exact_ties):
    n, tn = s.shape
    rows = lax.broadcasted_iota(jnp.int32, (n, tn), 0)
    rows16 = lax.broadcasted_iota(jnp.int32, (PEER_TOPK, tn), 0)
    rank = jnp.full((n, tn), float(PEER_TOPK), F32)
    tops = jnp.zeros((PEER_TOPK, tn), F32)
    v = s
    for k in range(PEER_TOPK):
        m = jnp.max(v, axis=0, keepdims=True)
        if exact_ties:
            hit = rows == jnp.min(jnp.where(v == m, rows, n), axis=0, keepdims=True)
        else:
            hit = v == m
        rank = jnp.where(hit, float(k), rank)
        v = jnp.where(hit, -jnp.inf, v)
        tops = jnp.where(rows16 == k, m, tops)
    return tops, rank, v


def _peer_select_head(q_ref, keys_ref, fidx, vmask, exact_ties):
    tops, ranks, es, picked = [], [], [], []
    for p in range(2):
        q = q_ref[:, p * PEER_HALF_DIM:(p + 1) * PEER_HALF_DIM]
        s = lax.dot_general(keys_ref[p], q, _NT, preferred_element_type=F32)
        t, r, v = _top16_rows(s, exact_ties)
        tops.append(t); ranks.append(r)
        es.append(jnp.exp(s - t[0:1, :]))
        picked.append(jnp.sum(jnp.where(v == -jnp.inf, 1.0, 0.0), axis=0, keepdims=True))
    ts1, ts2 = tops
    pieces = [ts1[0:1, :] + ts2]
    for k1 in range(1, 8):
        pieces.append(ts1[k1:k1 + 1, :] + ts2[0:8, :])
    pieces.append(ts1[8:16, :] + ts2[0:1, :])
    cand0 = jnp.concatenate(pieces, axis=0) + vmask
    cand = cand0
    for _ in range(PEER_TOPK):
        m = jnp.max(cand, axis=0, keepdims=True)
        if exact_ties:
            hit = fidx == jnp.min(jnp.where(cand == m, fidx, 4096), axis=0, keepdims=True)
        else:
            hit = cand == m
        cand = jnp.where(hit, -jnp.inf, cand)
    taken = jnp.logical_and(cand == -jnp.inf, vmask == 0.0)
    takenf = taken.astype(F32)
    picked.append(jnp.sum(takenf, axis=0, keepdims=True))
    unique = jnp.min(jnp.where((picked[0] == PEER_TOPK) & (picked[1] == PEER_TOPK) & (picked[2] == PEER_TOPK),
                               1.0, 0.0)) > 0.5
    best = ts1[0:1, :] + ts2[0:1, :]
    z = jnp.sum(jnp.where(taken, jnp.exp(cand0 - best), 0.0), axis=0, keepdims=True)
    counts = [jnp.sum(takenf[0:16, :], axis=0, keepdims=True)]
    for k1 in range(1, 8):
        counts.append(jnp.sum(takenf[16 + 8 * (k1 - 1):16 + 8 * k1, :], axis=0, keepdims=True))
    tail = takenf[72:80, :]
    cnt = jnp.zeros_like(ranks[0])
    for k1 in range(PEER_TOPK):
        nk = counts[k1] if k1 < 8 else tail[k1 - 8:k1 - 7, :]
        cnt = jnp.where(ranks[0] == float(k1), nk, cnt)
    return (cnt, ranks[1], es[0], es[1] / z), unique


def _peer_select_kernel(q_ref, keys_ref, fidx_ref, vmask_ref, cnt_ref, rank2_ref, e1_ref, e2_ref):
    fidx = fidx_ref[...]
    vmask = vmask_ref[...]

    def store(vals):
        for ref, val in zip((cnt_ref, rank2_ref, e1_ref, e2_ref), vals):
            ref[...] = val.astype(ref.dtype)

    vals, unique = _peer_select_head(q_ref, keys_ref, fidx, vmask, exact_ties=False)
    store(vals)

    @pl.when(jnp.logical_not(unique))
    def _():
        store(_peer_select_head(q_ref, keys_ref, fidx, vmask, exact_ties=True)[0])


def peer_select(q, keys, *, tn=512):
    T = q.shape[0]
    tn = min(tn, T)
    fidx, vmask = _peer_cand_tables(tn)
    rows = PEER_HEADS * PEER_N_KEYS
    ospec = pl.BlockSpec((PEER_N_KEYS, tn), lambda i, h: (h, i))
    return pl.pallas_call(
        _peer_select_kernel,
        grid=(T // tn, PEER_HEADS),
        in_specs=[pl.BlockSpec((tn, 2 * PEER_HALF_DIM), lambda i, h: (i, h)),
                  pl.BlockSpec((2, PEER_N_KEYS, PEER_HALF_DIM), lambda i, h: (h, 0, 0)),
                  pl.BlockSpec((80, tn), lambda i, h: (0, 0)),
                  pl.BlockSpec((80, tn), lambda i, h: (0, 0))],
        out_specs=[ospec] * 4,
        out_shape=[jax.ShapeDtypeStruct((rows, T), dt) for dt in (F32, BF16, F32, BF16)],
        compiler_params=_cparams(("parallel", "parallel")),
    )(q, keys, fidx, vmask)


def _peer_dense_kernel(xn_ref, u_ref, vt_ref, cnt_ref, rank2_ref, e1_ref, e2_ref, res_ref, o_ref,
                       acc_ref, g0_ref, g1_ref, *, c_per_step):
    j = pl.program_id(1)
    n_tiles = pl.num_programs(1) - 1

    @pl.when(j == 0)
    def _():
        acc_ref[...] = jnp.zeros_like(acc_ref)
        g1_ref[...] = jnp.zeros_like(g1_ref)

    @pl.when((j % 2 == 0) & (j < n_tiles))
    def _():
        _peer_dense_step(xn_ref, u_ref, vt_ref, cnt_ref, rank2_ref, e1_ref, e2_ref, acc_ref,
                         g1_ref, g0_ref, j, c_per_step)

    @pl.when(j % 2 == 1)
    def _():
        _peer_dense_step(xn_ref, u_ref, vt_ref, cnt_ref, rank2_ref, e1_ref, e2_ref, acc_ref,
                         g0_ref, g1_ref, j, c_per_step)

    @pl.when(j == n_tiles)
    def _():
        _peer_dense_step(xn_ref, u_ref, vt_ref, cnt_ref, rank2_ref, e1_ref, e2_ref, acc_ref,
                         g1_ref, None, j, c_per_step)
        o_ref[...] = res_ref[...] + acc_ref[...].T


def _peer_dense_step(xn_ref, u_ref, vt_ref, cnt_ref, rank2_ref, e1_ref, e2_ref, acc_ref, g_ref, g_next_ref,
                     j, c_per_step):
    tn = xn_ref.shape[1]
    bf16_rows = 16
    reps = PEER_N_KEYS // bf16_rows

    def row_tile(ref, row):
        r16 = jnp.broadcast_to(ref[pl.ds(row, 1), :], (bf16_rows, tn)).astype(BF16)
        return jnp.concatenate([r16] * reps, axis=0)

    c0 = jnp.maximum(j - 1, 0) * c_per_step
    up_rows = 2 * PEER_N_KEYS
    blocks = []
    for cc in range(c_per_step):
        if g_next_ref is not None and (cc * PEER_N_KEYS) % up_rows == 0:
            rs = slice(cc * PEER_N_KEYS, cc * PEER_N_KEYS + up_rows)
            hT = jnp.dot(u_ref[rs, :], xn_ref[...], preferred_element_type=F32)
            g_next_ref[rs, :] = _gelu_sigmoid(hT.astype(BF16))
        c = c0 + cc
        w = None
        for h in range(PEER_HEADS):
            row = h * PEER_N_KEYS + c
            n_row = row_tile(cnt_ref, row)
            e1_row = row_tile(e1_ref, row)
            sl = slice(h * PEER_N_KEYS, (h + 1) * PEER_N_KEYS)
            term = jnp.where(rank2_ref[sl, :] < n_row, e2_ref[sl, :], 0.0) * e1_row
            w = term if w is None else w + term
        blocks.append(w * g_ref[cc * PEER_N_KEYS:(cc + 1) * PEER_N_KEYS, :])
    aT = jnp.concatenate(blocks, axis=0)
    acc_ref[...] += jnp.dot(vt_ref[...], aT, preferred_element_type=F32)


def peer_dense(xn, u, vt, cnt, rank2, e1, e2, res, *, tn=512, te=2048):
    D, T = xn.shape
    E = u.shape[0]
    tn = min(tn, T)
    rows = PEER_HEADS * PEER_N_KEYS
    sel_spec = pl.BlockSpec((rows, tn), lambda i, j: (0, i))
    n_tiles = E // te
    assert n_tiles % 2 == 0
    return pl.pallas_call(
        functools.partial(_peer_dense_kernel, c_per_step=te // PEER_N_KEYS),
        grid=(T // tn, n_tiles + 1),
        in_specs=[pl.BlockSpec((D, tn), lambda i, j: (0, i)),
                  pl.BlockSpec((te, D), lambda i, j: (jnp.minimum(j, n_tiles - 1), 0)),
                  pl.BlockSpec((D, te), lambda i, j: (0, jnp.maximum(j - 1, 0))),
                  sel_spec, sel_spec, sel_spec, sel_spec,
                  pl.BlockSpec((tn, D), lambda i, j: (i, 0))],
        out_specs=pl.BlockSpec((tn, D), lambda i, j: (i, 0)),
        out_shape=jax.ShapeDtypeStruct((T, D), F32),
        scratch_shapes=[pltpu.VMEM((D, tn), F32), pltpu.VMEM((te, tn), BF16), pltpu.VMEM((te, tn), BF16)],
        compiler_params=_cparams(("parallel", "arbitrary")),
    )(xn, u, vt, cnt, rank2, e1, e2, res)


def peer_layer(h, gamma, w_q, sub_keys, u, v):
    q, xn = norm_matmul(h, gamma, w_q.astype(BF16), emit_xn=True)
    keys = sub_keys.reshape(2 * PEER_HEADS, PEER_N_KEYS, PEER_HALF_DIM).astype(BF16)
    cnt, rank2, e1, e2 = peer_select(q, keys)
    return peer_dense(xn, u.astype(BF16), v.T.astype(BF16), cnt, rank2, e1, e2, h)


def _cumsum_aug_kernel(lf_ref, tri_ref, place_q_ref, place_k_ref, ones_q_ref, ones_k_ref,
                       qa_ref, ka_ref, c_ref, carry_ref):
    @pl.when(pl.program_id(1) == 0)
    def _():
        carry_ref[...] = jnp.zeros_like(carry_ref)

    lf = lf_ref[0]
    c = jnp.dot(tri_ref[...], lf, preferred_element_type=F32, precision=lax.Precision.HIGHEST) + carry_ref[...]
    carry_ref[...] = c[-1:, :]
    c = c * LOG2E
    c_ref[0] = c
    hi = c.astype(BF16)
    r1 = c - hi.astype(F32)
    mid = r1.astype(BF16)
    lo = (r1 - mid.astype(F32)).astype(BF16)
    nh = N_HEADS
    lane = lax.broadcasted_iota(jnp.int32, c.shape, 1)
    parts = jnp.where(lane < nh, hi.astype(F32),
                      jnp.where(lane < 2 * nh, pltpu.roll(mid.astype(F32), nh, axis=1),
                                pltpu.roll(lo.astype(F32), 2 * nh, axis=1)))
    parts = jnp.where(lane < 3 * nh, parts, 0.0).astype(BF16)
    qa_ref[0] = (jnp.dot(parts, place_q_ref[...], preferred_element_type=F32) + ones_q_ref[...]).astype(BF16)
    ka_ref[0] = (jnp.dot(parts, place_k_ref[...], preferred_element_type=F32) + ones_k_ref[...]).astype(BF16)


def fox_bias_operands(logf, *, tc=256):
    B, S, _ = logf.shape
    nh = N_HEADS
    tri = jnp.asarray(np.tril(np.ones((tc, tc), np.float32)))
    pq = np.zeros((LANES, nh * LANES), np.float32)
    pk = np.zeros((LANES, nh * LANES), np.float32)
    oq = np.zeros((1, nh * LANES), np.float32)
    ok = np.zeros((1, nh * LANES), np.float32)
    for h in range(nh):
        for part in range(3):
            pq[part * nh + h, h * LANES + part] = 1.0
            pk[part * nh + h, h * LANES + 3 + part] = -1.0
            oq[0, h * LANES + 3 + part] = 1.0
            ok[0, h * LANES + part] = 1.0
    const = lambda a: pl.BlockSpec(a.shape, lambda b, i: (0,) * a.ndim)
    pq, pk, oq, ok = jnp.asarray(pq, BF16), jnp.asarray(pk, BF16), jnp.asarray(oq), jnp.asarray(ok)
    out = jax.ShapeDtypeStruct((B, S, nh * LANES), BF16)
    return pl.pallas_call(
        _cumsum_aug_kernel,
        grid=(B, S // tc),
        in_specs=[pl.BlockSpec((1, tc, LANES), lambda b, i: (b, i, 0)),
                  const(tri), const(pq), const(pk), const(oq), const(ok)],
        out_specs=[pl.BlockSpec((1, tc, nh * LANES), lambda b, i: (b, i, 0))] * 2
        + [pl.BlockSpec((1, tc, LANES), lambda b, i: (b, i, 0))],
        out_shape=[out, out, jax.ShapeDtypeStruct((B, S, LANES), F32)],
        scratch_shapes=[pltpu.VMEM((1, LANES), F32)],
        compiler_params=_cparams(("parallel", "arbitrary")),
    )(logf, tri, pq, pk, oq, ok)


def _fox_attn_kernel(first_ref, q_ref, qa_ref, kv_ref, ka_ref, o_ref, *, tq, tk, heads_per_step):
    qi = pl.program_id(2)
    t0 = qi * tq
    n_full = t0 // tk
    j_first = first_ref[(pl.program_id(0) * pl.num_programs(1) + pl.program_id(1)) * pl.num_programs(2) + qi]
    n_diag = tq // tk
    lanes = [slice(hh * LANES, (hh + 1) * LANES) for hh in range(heads_per_step)]
    qs = [jnp.concatenate([q_ref[0, :, lsl], qa_ref[0, :, lsl]], axis=1) for lsl in lanes]

    def step(j, carry, masked):
        rows = pl.ds(pl.multiple_of(j * tk, tk), tk)
        klane = lax.broadcasted_iota(jnp.int32, (tk, LANES), 1)
        new = []
        for lsl, q, (m, acc) in zip(lanes, qs, carry):
            kv = kv_ref[0, rows, lsl]
            kk = jnp.concatenate([kv, ka_ref[0, rows, lsl]], axis=1)
            ones_v = jnp.where(klane < HEAD_DIM, 1.0, kv).astype(BF16)
            s = lax.dot_general(q, kk, _NT, preferred_element_type=F32)
            if masked:
                qpos = t0 + lax.broadcasted_iota(jnp.int32, (tq, 1), 0)
                kpos = j * tk + lax.broadcasted_iota(jnp.int32, (1, tk), 1)
                s = jnp.where(kpos <= qpos, s, NEG_INF)
            m_new = jnp.maximum(m, jnp.max(s, axis=1, keepdims=True))
            p = jnp.exp2(s - m_new).astype(BF16)
            acc = jnp.exp2(m - m_new) * acc + jnp.dot(p, ones_v, preferred_element_type=F32)
            new.append((m_new, acc))
        return tuple(new)

    init = (jnp.full((tq, 1), NEG_INF, F32), jnp.zeros((tq, LANES), F32))
    n_pairs = (n_full - j_first) // 2
    carry = lax.fori_loop(
        0, n_pairs, lambda i, c: step(j_first + 2 * i + 1, step(j_first + 2 * i, c, False), False),
        (init,) * heads_per_step)
    carry = lax.fori_loop(j_first + 2 * n_pairs, n_full, functools.partial(step, masked=False), carry)
    for d in range(n_diag):
        carry = step(n_full + d, carry, True)
    outs = [acc / acc[:, 0:1] for (_, acc) in carry]
    lane = lax.broadcasted_iota(jnp.int32, (tq, LANES), 1)
    blocks = []
    for pair in range(heads_per_step // 2):
        a, b = outs[2 * pair], outs[2 * pair + 1]
        blocks.append(jnp.where(lane < HEAD_DIM, pltpu.roll(a, HEAD_DIM, axis=1), b))
    o_ref[0] = jnp.concatenate(blocks, axis=1).astype(o_ref.dtype) if len(blocks) > 1 else blocks[0].astype(o_ref.dtype)


def _head_slots(w, n_heads, second=None):
    D = w.shape[0]
    a = w.reshape(D, n_heads, HEAD_DIM)
    b = jnp.zeros_like(a) if second is None else second.reshape(D, n_heads, HEAD_DIM)
    return jnp.concatenate([a, b], axis=-1).reshape(D, n_heads * LANES)


def fox_layer(h, gamma, w_in, f_bias, w_out, B, S):
    aw = ATTN_WIDTH
    wq = _head_slots(w_in[:, :aw] * (HEAD_DIM ** -0.5 * LOG2E), N_HEADS)
    wkv = _head_slots(w_in[:, aw:2 * aw], N_HEADS, w_in[:, 2 * aw:3 * aw])
    w_main = jnp.concatenate([wq, wkv], axis=1).astype(BF16)
    wf = jnp.pad(w_in[:, 3 * aw:], ((0, 0), (0, LANES - N_HEADS))).astype(BF16)
    bf = jnp.pad(f_bias.astype(F32), (0, LANES - N_HEADS))
    qkv = norm_matmul(h, gamma, w_main)
    logf = norm_matmul(h, gamma, wf, bias=bf, act="log_sigmoid", out_dtype=F32)
    qa, ka, c2 = fox_bias_operands(logf.reshape(B, S, LANES))
    qkv = qkv.reshape(B, S, 2 * N_HEADS * LANES)
    o = fox_attention(qkv, qa, ka, c2)
    return matmul_residual(o.reshape(B * S, aw), w_out.astype(BF16), h)


FOX_NEGLIGIBLE_LOG2 = 160.0


def _norm_maxima_kernel(x_ref, ind_ref, o_ref):
    x = x_ref[0].astype(F32)
    ss = jnp.dot((x * x).astype(BF16), ind_ref[...], preferred_element_type=F32)
    o_ref[0, 0] = jnp.broadcast_to(jnp.max(ss, axis=0, keepdims=True), o_ref.shape[2:])


def fox_norm_maxima(qkv, tile):
    B, S, W = qkv.shape
    n = S // tile
    col = np.arange(W)
    slot, lane = col // LANES, col % LANES
    used = (slot < N_HEADS) | (lane < HEAD_DIM)
    ind = jnp.asarray((slot[:, None] == np.arange(LANES)[None, :]) & used[:, None], BF16)
    rows = min(256, tile)
    ss = pl.pallas_call(
        _norm_maxima_kernel,
        grid=(B, S // rows),
        in_specs=[pl.BlockSpec((1, rows, W), lambda b, i: (b, i, 0)),
                  pl.BlockSpec(ind.shape, lambda b, i: (0, 0))],
        out_specs=pl.BlockSpec((1, 1, 8, LANES), lambda b, i: (b, i, 0, 0)),
        out_shape=jax.ShapeDtypeStruct((B, S // rows, 8, LANES), F32),
        compiler_params=_cparams(("parallel", "parallel")),
    )(qkv, ind)
    ss = jnp.max(ss[:, :, 0, :2 * N_HEADS].reshape(B, n, tile // rows, 2 * N_HEADS), axis=2)
    return jnp.sqrt(ss * 1.01)


def fox_first_chunk(qkv, c2, tile, heads_per_step):
    B, S, _ = qkv.shape
    n = S // tile
    norms = fox_norm_maxima(qkv, tile)
    qmax, kmax = norms[..., :N_HEADS], norms[..., N_HEADS:]
    c = c2[..., :N_HEADS].reshape(B, n, tile, N_HEADS)
    cmax, cmin = jnp.max(c, axis=2), jnp.min(c, axis=2)
    upper = qmax[:, :, None] * kmax[:, None, :] + cmax[:, :, None] - cmin[:, None, :]
    own = -(qmax * kmax)
    earlier = jnp.arange(n)[None, :, None, None] > jnp.arange(n)[None, None, :, None]
    skip = (upper < own[:, :, None] - FOX_NEGLIGIBLE_LOG2) & earlier
    first = jnp.sum(jnp.cumprod(skip.astype(jnp.int32), axis=2), axis=2)
    first = jnp.min(first.reshape(B, n, N_HEADS // heads_per_step, heads_per_step), axis=-1)
    return jnp.transpose(first, (0, 2, 1)).reshape(-1).astype(jnp.int32)


def fox_attention(qkv, qa, ka, c2, *, tq=1024, tk=1024, heads_per_step=2):
    B, S, _ = qkv.shape
    tq = tk = min(tq, S)
    hs = heads_per_step
    wq = hs * LANES
    n_qblk = N_HEADS // hs
    first = fox_first_chunk(qkv, c2, tk, hs)
    grid_spec = pltpu.PrefetchScalarGridSpec(
        num_scalar_prefetch=1,
        grid=(B, n_qblk, S // tq),
        in_specs=[pl.BlockSpec((1, tq, wq), lambda b, h, i, first: (b, i, h)),
                  pl.BlockSpec((1, tq, wq), lambda b, h, i, first: (b, i, h)),
                  pl.BlockSpec((1, S, wq), lambda b, h, i, first: (b, 0, n_qblk + h)),
                  pl.BlockSpec((1, S, wq), lambda b, h, i, first: (b, 0, h))],
        out_specs=pl.BlockSpec((1, tq, hs * HEAD_DIM), lambda b, h, i, first: (b, i, h)))
    return pl.pallas_call(
        functools.partial(_fox_attn_kernel, tq=tq, tk=tk, heads_per_step=hs),
        grid_spec=grid_spec,
        out_shape=jax.ShapeDtypeStruct((B, S, ATTN_WIDTH), BF16),
        compiler_params=_cparams(("parallel", "parallel", "arbitrary")),
    )(first, qkv, qa, qkv, ka)


def _rot_half_cols(w):
    D = w.shape[0]
    a = w.reshape(D, -1, HEAD_DIM)
    half = HEAD_DIM // 2
    return jnp.concatenate([-a[..., half:], a[..., :half]], axis=-1).reshape(w.shape)


def _rope_tables(S):
    half = HEAD_DIM // 2
    inv_freq = ROPE_THETA ** (-jnp.arange(half, dtype=F32) / half)
    ang = jnp.arange(S, dtype=F32)[:, None] * inv_freq[None, :]
    c, s = jnp.cos(ang), jnp.sin(ang)
    c2, s2 = jnp.concatenate([c, c], axis=1), jnp.concatenate([s, s], axis=1)
    cos = jnp.stack([jnp.concatenate([c2, jnp.ones_like(c2)], axis=1), jnp.concatenate([c2, c2], axis=1)])
    sin = jnp.stack([jnp.concatenate([s2, jnp.zeros_like(s2)], axis=1), jnp.concatenate([s2, s2], axis=1)])
    return cos, sin


def _compress_kernel(x_ref, pea_ref, peb_ref, wa_ref, wb_ref, w2_ref, o_ref, pa_ref, pb0_ref, *, n_rows):
    u = pl.program_id(1)
    x = x_ref[0].astype(F32)
    pa = jnp.dot((x + pea_ref[...]).astype(BF16), wa_ref[...], preferred_element_type=F32)
    pb = jnp.dot((x + peb_ref[...]).astype(BF16), wb_ref[...], preferred_element_type=F32)

    def emit(slab, hid):
        y = jnp.dot(_gelu_tanh(hid).astype(BF16), w2_ref[...], preferred_element_type=F32)
        o_ref[0, pl.ds(pl.multiple_of(slab * n_rows, n_rows), n_rows), :] = y

    @pl.when(u == 0)
    def _():
        pb0_ref[...] = pb

    @pl.when(u > 0)
    def _():
        emit(u - 1, pa_ref[...] + pb)

    @pl.when(u == 3)
    def _():
        emit(3, pa + pltpu.roll(pb0_ref[...], n_rows - 1, axis=0))

    pa_ref[...] = pa


def nsa_compress(src, pe, w1, w2):
    B, S, W = src.shape
    G = NSA_GROUPS
    n_rows = S // 64
    half = CMP_BLOCK // 2
    cw = half * W
    xv = src.reshape(B, n_rows, 4 * cw)
    pe_flat = jnp.transpose(pe, (1, 0, 2)).reshape(CMP_BLOCK, W).astype(F32)
    pea, peb = pe_flat[:half].reshape(1, cw), pe_flat[half:].reshape(1, cw)
    eye = jnp.eye(G, dtype=F32)
    wfull = jnp.einsum('gldh,gk->lkdgh', w1.astype(F32), eye).reshape(CMP_BLOCK, W, G * CMP_HIDDEN)
    wa = wfull[:half].reshape(cw, G * CMP_HIDDEN).astype(BF16)
    wb = wfull[half:].reshape(cw, G * CMP_HIDDEN).astype(BF16)
    w2bd = jnp.einsum('ghd,gk->ghkd', w2.astype(F32), eye).reshape(G * CMP_HIDDEN, W).astype(BF16)
    const = lambda a: pl.BlockSpec(a.shape, lambda b, u: (0,) * a.ndim)
    return pl.pallas_call(
        functools.partial(_compress_kernel, n_rows=n_rows),
        grid=(B, 4),
        in_specs=[pl.BlockSpec((1, n_rows, cw), lambda b, u: (b, 0, u)),
                  const(pea), const(peb), const(wa), const(wb), const(w2bd)],
        out_specs=pl.BlockSpec((1, 4 * n_rows, W), lambda b, u: (b, 0, 0)),
        out_shape=jax.ShapeDtypeStruct((B, 4 * n_rows, W), F32),
        scratch_shapes=[pltpu.VMEM((n_rows, G * CMP_HIDDEN), F32), pltpu.VMEM((n_rows, G * CMP_HIDDEN), F32)],
        compiler_params=_cparams(("parallel", "arbitrary")),
    )(xv, pea, peb, wa, wb, w2bd)


def _nsa_attn_kernel(q_ref, kvs_ref, kvw_ref, kvc_ref, kvct_ref, gate_ref, bmat_ref, pmat_ref, o_ref,
                     *, tq, ts, tk, seq):
    R = NSA_Q_PER_GROUP
    n_slc = seq // SLC_BLOCK
    n_sel = min(SLC_TOPK, n_slc)
    assert ts & (ts - 1) == 0 and n_slc & (n_slc - 1) == 0 and tk % tq == 0 and tq % ts == 0
    log_slc = n_slc.bit_length() - 1
    qi = pl.program_id(2)
    t0 = qi * tq

    local = [_nsa_local_branches(q_ref, kvw_ref, kvc_ref, kvct_ref, t0 + i * ts, i * ts, ts, n_slc, n_sel, log_slc)
             for i in range(tq // ts)]
    o_c = jnp.concatenate([o[0][r * ts:(r + 1) * ts] for r in range(R) for o in local], axis=0)
    o_w = jnp.concatenate([o[1][r * ts:(r + 1) * ts] for r in range(R) for o in local], axis=0)
    sel_bias = jnp.concatenate([o[2] for o in local], axis=0)
    qs = jnp.concatenate([q_ref[0, :, r * LANES:(r + 1) * LANES] for r in range(R)], axis=0)
    _nsa_selected_and_combine(qs, o_c, o_w, sel_bias, kvs_ref, gate_ref, bmat_ref, pmat_ref, o_ref,
                              t0=t0, tq=tq, tk=tk, seq=seq)


def _nsa_local_branches(q_ref, kvw_ref, kvc_ref, kvct_ref, t0, row0, tq, n_slc, n_sel, log_slc):
    R = NSA_Q_PER_GROUP
    M = R * tq
    qs = jnp.concatenate([q_ref[0, row0:row0 + tq, r * LANES:(r + 1) * LANES] for r in range(R)], axis=0)

    wlen = WINDOW + tq
    start = jnp.maximum(t0 - WINDOW, 0)
    kvw = kvw_ref[0, pl.ds(pl.multiple_of(start, tq), wlen), :]
    s_w = lax.dot_general(qs, kvw, _NT, preferred_element_type=F32)
    qpos = t0 + (lax.broadcasted_iota(jnp.int32, (M, 1), 0) & (tq - 1))
    kpos = start + lax.broadcasted_iota(jnp.int32, (1, wlen), 1)
    s_w = jnp.where(kpos <= qpos, jnp.where(kpos > qpos - WINDOW, s_w, NEG_INF), NEG_INF)
    p_w = jnp.exp2(s_w - jnp.max(s_w, axis=1, keepdims=True)).astype(BF16)
    wlane = lax.broadcasted_iota(jnp.int32, (wlen, LANES), 1)
    acc_w = jnp.dot(p_w, jnp.where(wlane < HEAD_DIM, 1.0, kvw).astype(BF16), preferred_element_type=F32)
    o_w = acc_w / acc_w[:, 0:1]

    n_groups, gs = _nsa_row_groups(n_slc)
    o_c, sel_bias = _nsa_compressed_and_select(qs, kvc_ref, kvct_ref, t0, tq, n_slc, n_sel, gs, n_groups)
    return o_c, o_w, sel_bias


def _nsa_row_groups(n_slc):
    n_groups = max(1, min(4, n_slc // 8))
    assert n_slc % n_groups == 0 and (n_slc // n_groups) % 8 == 0
    return n_groups, n_slc // n_groups


def _nsa_compressed_and_select(qs, kvc_ref, kvct_ref, t0, tq, n_slc, n_sel, gs, n_groups):
    R = NSA_Q_PER_GROUP
    M = R * tq
    n_rows, n_j = n_groups * 4 * gs, n_groups * gs
    log_gs = gs.bit_length() - 1
    sT = lax.dot_general(kvc_ref[0, 0, 0:n_rows, :], qs, _NT, preferred_element_type=F32)
    rowc = lax.broadcasted_iota(jnp.int32, (n_rows, 1), 0)
    blk_j = ((rowc >> (log_gs + 2)) << log_gs) + (rowc & (gs - 1))
    blk_u = (rowc >> log_gs) & 3
    cmp_end = blk_j * SLC_BLOCK + blk_u * CMP_STRIDE + (CMP_BLOCK - 1)
    tcol = t0 + (lax.broadcasted_iota(jnp.int32, (1, M), 1) & (tq - 1))
    sm = jnp.where(cmp_end <= tcol, sT, NEG_INF)
    mx = jnp.max(sm, axis=0, keepdims=True)
    e = jnp.exp2(sm - mx)
    inv = jnp.where(mx > 0.5 * NEG_INF, 1.0 / jnp.sum(e, axis=0, keepdims=True), 0.0)
    pT = e * inv
    o_c = jnp.dot(kvct_ref[0, 0, :, 0:n_rows], pT.astype(BF16), preferred_element_type=F32).T

    psum = pT[:, 0:tq]
    for r in range(1, R):
        psum = psum + pT[:, r * tq:(r + 1) * tq]
    p0, p1, p2, p3 = (jnp.concatenate([psum[(4 * g + u) * gs:(4 * g + u + 1) * gs, :] for g in range(n_groups)],
                                      axis=0) for u in range(4))
    jrow = lax.broadcasted_iota(jnp.int32, (n_j, tq), 0)
    p3_prev = jnp.where(jrow == 0, 0.0, pltpu.roll(p3, 1, axis=0))
    imp = p0 + p1 + p2 + 0.5 * p3 + 0.5 * p3_prev
    cur = (t0 + lax.broadcasted_iota(jnp.int32, (n_j, tq), 1)) >> (SLC_BLOCK.bit_length() - 1)
    forced = (jrow == 0) | (jrow == cur) | (jrow == cur - 1)
    vals = jnp.where(forced, -jnp.inf, jnp.where(jrow <= cur, imp, -jnp.inf))
    sel0 = jnp.where(forced, 1.0, 0.0)
    n_free = n_sel - 3

    def pick(exact_ties):
        v, sel = vals, sel0
        for _ in range(n_free):
            m = jnp.max(v, axis=0, keepdims=True)
            if exact_ties:
                hit = jrow == jnp.min(jnp.where(v == m, jrow, n_slc), axis=0, keepdims=True)
            else:
                hit = v == jnp.where(m == -jnp.inf, jnp.nan, m)
            sel = jnp.where(hit, 1.0, sel)
            v = jnp.where(hit, -jnp.inf, v)
        return sel

    sel_fast = pick(False)
    n_cand = jnp.sum(jnp.where(vals > -jnp.inf, 1.0, 0.0), axis=0, keepdims=True)
    n_picked = jnp.sum(sel_fast - sel0, axis=0, keepdims=True)
    unique = jnp.min(jnp.where(n_picked == jnp.minimum(n_cand, float(n_free)), 1.0, 0.0)) > 0.5
    sel = lax.cond(unique, lambda: sel_fast, lambda: pick(True))
    if n_j < n_slc:
        sel = jnp.concatenate([sel, jnp.zeros((n_slc - n_j, tq), F32)], axis=0)
    sel_bias = ((sel.T - 1.0) * (-NEG_INF)).astype(BF16)
    return o_c, sel_bias


def _nsa_selected_and_combine(qs, o_c, o_w, sel_bias, kvs_ref, gate_ref, bmat_ref, pmat_ref, o_ref,
                              *, t0, tq, tk, seq):
    R = NSA_Q_PER_GROUP
    n_slc = seq // SLC_BLOCK
    blocks_per_chunk = tk // SLC_BLOCK
    n_parts = 2
    hp = R // n_parts
    mp = hp * tq
    qparts = [qs[i * mp:(i + 1) * mp] for i in range(n_parts)]
    qrow = t0 + (lax.broadcasted_iota(jnp.int32, (mp, 1), 0) & (tq - 1))
    kcol = lax.broadcasted_iota(jnp.int32, (1, tk), 1)
    n_chunks = seq // tk
    j_last = t0 // tk

    klane = lax.broadcasted_iota(jnp.int32, (tk, LANES), 1)

    def slc_step(j, carry, masked):
        kv = kvs_ref[0, pl.ds(pl.multiple_of(j * tk, tk), tk), :]
        k_sel = jnp.where(klane < HEAD_DIM, kv, bmat_ref[...]).astype(BF16)
        ones_v = jnp.where(klane < HEAD_DIM, 1.0, kv).astype(BF16)
        off = pl.multiple_of(blocks_per_chunk * (n_chunks - 1 - j), blocks_per_chunk)
        place = pmat_ref[pl.ds(off, n_slc), :].astype(BF16)
        q_bias = jnp.dot(sel_bias, place, preferred_element_type=F32).astype(BF16)
        q_bias = jnp.concatenate([q_bias] * hp, axis=0)
        new = []
        for qp, (m, acc) in zip(qparts, carry):
            s = lax.dot_general(qp + q_bias, k_sel, _NT, preferred_element_type=F32)
            if masked:
                s = jnp.where(j * tk + kcol <= qrow, s, NEG_INF)
            m_new = jnp.maximum(m, jnp.max(s, axis=1, keepdims=True))
            p = jnp.exp2(s - m_new).astype(BF16)
            acc = jnp.exp2(m - m_new) * acc + jnp.dot(p, ones_v, preferred_element_type=F32)
            new.append((m_new, acc))
        return tuple(new)

    init = (jnp.full((mp, 1), NEG_INF, F32), jnp.zeros((mp, LANES), F32))
    n_pairs = j_last // 2
    carry = lax.fori_loop(0, n_pairs, lambda i, c: slc_step(2 * i + 1, slc_step(2 * i, c, False), False),
                          (init,) * n_parts)
    carry = lax.fori_loop(2 * n_pairs, j_last, functools.partial(slc_step, masked=False), carry)
    carry = slc_step(j_last, carry, True)
    o_s = jnp.concatenate([acc / acc[:, 0:1] for (_, acc) in carry], axis=0)

    gates = gate_ref[0]
    lane = lax.broadcasted_iota(jnp.int32, (tq, LANES), 1)
    comb = []
    for r in range(R):
        rs = slice(r * tq, (r + 1) * tq)
        comb.append(gates[:, r:r + 1] * o_c[rs] + gates[:, R + r:R + r + 1] * o_s[rs]
                    + gates[:, 2 * R + r:2 * R + r + 1] * o_w[rs])
    out = [jnp.where(lane < HEAD_DIM, pltpu.roll(comb[2 * i], HEAD_DIM, axis=1), comb[2 * i + 1])
           for i in range(R // 2)]
    o_ref[0] = jnp.concatenate(out, axis=1).astype(o_ref.dtype)


def nsa_attention(qkv, kvc, kvct, gates, *, tq=512, ts=256, tk=1024):
    B, S, _ = qkv.shape
    G, R = NSA_GROUPS, NSA_Q_PER_GROUP
    tk = min(tk, S)
    n_slc = S // SLC_BLOCK
    n_cmp = kvc.shape[2]
    bpc = tk // SLC_BLOCK
    assert bpc <= LANES - HEAD_DIM
    off = bpc * (S // tk - 1)
    lane = np.arange(LANES)[None, :]
    bmat = jnp.asarray(lane - HEAD_DIM == np.arange(tk)[:, None] // SLC_BLOCK, BF16)
    pmat = jnp.asarray((np.arange(n_slc + off)[:, None] - off == lane - HEAD_DIM) & (lane >= HEAD_DIM)
                       & (lane < HEAD_DIM + bpc), F32)
    slc_blk0, win_blk0 = N_HEADS, N_HEADS + G
    return pl.pallas_call(
        functools.partial(_nsa_attn_kernel, tq=tq, ts=ts, tk=tk, seq=S),
        grid=(B, G, S // tq),
        in_specs=[pl.BlockSpec((1, tq, R * LANES), lambda b, g, i: (b, i, g)),
                  pl.BlockSpec((1, S, LANES), lambda b, g, i: (b, 0, slc_blk0 + g)),
                  pl.BlockSpec((1, S, LANES), lambda b, g, i: (b, 0, win_blk0 + g)),
                  pl.BlockSpec((1, 1, n_cmp, LANES), lambda b, g, i: (b, g, 0, 0)),
                  pl.BlockSpec((1, 1, LANES, n_cmp), lambda b, g, i: (b, g, 0, 0)),
                  pl.BlockSpec((1, tq, LANES), lambda b, g, i: (b, i, g)),
                  pl.BlockSpec(bmat.shape, lambda b, g, i: (0, 0)),
                  pl.BlockSpec(pmat.shape, lambda b, g, i: (0, 0))],
        out_specs=pl.BlockSpec((1, tq, R * HEAD_DIM), lambda b, g, i: (b, i, g)),
        out_shape=jax.ShapeDtypeStruct((B, S, ATTN_WIDTH), BF16),
        compiler_params=_cparams(("parallel", "parallel", "arbitrary")),
    )(qkv, qkv, qkv, kvc, kvct, gates, bmat, pmat)


def nsa_layer(h, gamma, w_in, pe_k, w1_k, w2_k, pe_v, w1_v, w2_v, w_out, B, S):
    G, R, hd, aw = NSA_GROUPS, NSA_Q_PER_GROUP, HEAD_DIM, ATTN_WIDTH
    kvd = G * hd
    sec = lambda i: w_in[:, aw + i * kvd: aw + (i + 1) * kvd]
    wq = w_in[:, :aw] * (hd ** -0.5 * LOG2E)
    wa = jnp.concatenate([_head_slots(wq, N_HEADS), _head_slots(sec(2), G, sec(3)),
                          _head_slots(sec(4), G, sec(5))], axis=1).astype(BF16)
    wb = jnp.concatenate([_head_slots(_rot_half_cols(wq), N_HEADS), _head_slots(_rot_half_cols(sec(2)), G),
                          _head_slots(_rot_half_cols(sec(4)), G)], axis=1).astype(BF16)
    cos, sin = _rope_tables(S)
    qkv = norm_matmul(h, gamma, wa, wb=wb, cos=cos, sin=sin)
    kc_src = norm_matmul(h, gamma, sec(0).astype(BF16), wb=_rot_half_cols(sec(0)).astype(BF16),
                         cos=cos, sin=sin, table_of_tile=lambda j: 1)
    vc_src = norm_matmul(h, gamma, sec(1).astype(BF16))
    wg = w_in[:, aw + 6 * kvd:].reshape(-1, 3, G, R)
    wg = jnp.transpose(wg, (0, 2, 1, 3)).reshape(-1, G, 3 * R)
    wg = jnp.pad(wg, ((0, 0), (0, 0), (0, LANES - 3 * R))).reshape(-1, G * LANES).astype(BF16)
    gates = norm_matmul(h, gamma, wg, act="sigmoid", out_dtype=F32)
    kc = nsa_compress(kc_src.reshape(B, S, kvd), pe_k, w1_k, w2_k)
    vc = nsa_compress(vc_src.reshape(B, S, kvd), pe_v, w1_v, w2_v)
    n_cmp = kc.shape[1]
    kvc = jnp.concatenate([kc.reshape(B, n_cmp, G, hd), vc.reshape(B, n_cmp, G, hd)], axis=-1)
    n_groups, gs = _nsa_row_groups(S // SLC_BLOCK)
    kvc = kvc.reshape(B, 4, n_groups, gs, G, 2 * hd)
    kvc = jnp.transpose(kvc, (0, 4, 2, 1, 3, 5)).reshape(B, G, n_cmp, 2 * hd).astype(BF16)
    kvct = jnp.swapaxes(kvc, 2, 3)
    o = nsa_attention(qkv.reshape(B, S, -1), kvc, kvct, gates.reshape(B, S, G * LANES))
    return matmul_residual(o.reshape(B * S, aw), w_out.astype(BF16), h)


def kernel(x, l0_attn_norm, l0_w_in, l0_cmp_pe_k, l0_cmp_w1_k, l0_cmp_w2_k, l0_cmp_pe_v, l0_cmp_w1_v,
           l0_cmp_w2_v, l0_w_out, l0_ffn_norm, l0_peer_wq, l0_peer_keys, l0_peer_u, l0_peer_v,
           l1_attn_norm, l1_w_in, l1_f_bias, l1_w_out, l1_ffn_norm, l1_peer_wq, l1_peer_keys, l1_peer_u,
           l1_peer_v, final_norm):
    B, S, D = x.shape
    h = x.reshape(B * S, D)
    h = nsa_layer(h, l0_attn_norm, l0_w_in, l0_cmp_pe_k, l0_cmp_w1_k, l0_cmp_w2_k, l0_cmp_pe_v, l0_cmp_w1_v,
                  l0_cmp_w2_v, l0_w_out, B, S)
    h = peer_layer(h, l0_ffn_norm, l0_peer_wq, l0_peer_keys, l0_peer_u, l0_peer_v)
    h = fox_layer(h, l1_attn_norm, l1_w_in, l1_f_bias, l1_w_out, B, S)
    h = peer_layer(h, l1_ffn_norm, l1_peer_wq, l1_peer_keys, l1_peer_u, l1_peer_v)
    return rmsnorm(h, final_norm).reshape(B, S, D)
```

```python
import functools

import numpy as np
import jax
import jax.numpy as jnp
from jax import lax
from jax.experimental import pallas as pl
from jax.experimental.pallas import tpu as pltpu

F32 = jnp.float32
BF16 = jnp.bfloat16

D_MODEL = 1024
N_HEADS = 16
HEAD_DIM = 64
ATTN_WIDTH = N_HEADS * HEAD_DIM
NSA_GROUPS = 4
NSA_Q_PER_GROUP = N_HEADS // NSA_GROUPS
CMP_BLOCK = 32
CMP_STRIDE = 16
CMP_HIDDEN = 2 * HEAD_DIM
SLC_BLOCK = 64
SLC_TOPK = 16
WINDOW = 512
FORCE_SCORE = 1.0e4
ROPE_THETA = 10000.0
PEER_HEADS = 8
PEER_N_KEYS = 128
PEER_TOPK = 16
PEER_HALF_DIM = 128
RMS_EPS = 1e-6
NEG_INF = -1e30
LOG2E = 1.4426950408889634

LANES = 128
VMEM_LIMIT_BYTES = 56 * 1024 * 1024

_NT = (((1,), (1,)), ((), ()))


def _cparams(sem, vmem=VMEM_LIMIT_BYTES, flags=None):
    return pltpu.CompilerParams(dimension_semantics=sem, vmem_limit_bytes=vmem, flags=flags)


def _gelu_tanh(x):
    return 0.5 * x * (1.0 + jnp.tanh(0.7978845608028654 * (x + 0.044715 * (x * x * x))))


def _gelu_sigmoid(x):
    c = -2.0 * 0.7978845608028654 * LOG2E
    t = x * (c + (c * 0.044715) * (x * x))
    return x / (1.0 + jnp.exp2(t))


def _rms_rows(x, g):
    ms = jnp.mean(x * x, axis=-1, keepdims=True)
    return x * lax.rsqrt(ms + RMS_EPS) * g


def _norm_mm_kernel(*refs, act, has_bias, rope, emit_xn):
    it = iter(refs)
    x_ref, g_ref = next(it), next(it)
    wa_ref = next(it)
    wb_ref = next(it) if rope else None
    cos_ref = next(it) if rope else None
    sin_ref = next(it) if rope else None
    b_ref = next(it) if has_bias else None
    o_ref = next(it)
    xo_ref = next(it) if emit_xn else None
    xn_ref = next(it)

    @pl.when(pl.program_id(1) == 0)
    def _():
        xn = _rms_rows(x_ref[...], g_ref[...])
        xn_ref[...] = xn.astype(BF16)
        if emit_xn:
            xo_ref[...] = xn.T.astype(BF16)

    xn = xn_ref[...]
    y = jnp.dot(xn, wa_ref[...], preferred_element_type=F32)
    if rope:
        yb = jnp.dot(xn, wb_ref[...], preferred_element_type=F32)
        cos, sin = cos_ref[...], sin_ref[...]
        for s in range(y.shape[1] // LANES):
            sl = slice(s * LANES, (s + 1) * LANES)
            o_ref[:, sl] = (y[:, sl] * cos + yb[:, sl] * sin).astype(o_ref.dtype)
        return
    if has_bias:
        y = y + b_ref[...]
    if act == "sigmoid":
        y = jax.nn.sigmoid(y)
    elif act == "log_sigmoid":
        y = jax.nn.log_sigmoid(y)
    o_ref[...] = y.astype(o_ref.dtype)


def norm_matmul(x, gamma, wa, *, wb=None, cos=None, sin=None, table_of_tile=None, bias=None,
                act=None, out_dtype=BF16, emit_xn=False, tm=1024, tn=1024):
    T, D = x.shape
    N = wa.shape[1]
    tm, tn = min(tm, T), min(tn, N)
    assert T % tm == 0 and N % tn == 0 and tn % LANES == 0
    rope = wb is not None
    in_specs = [pl.BlockSpec((tm, D), lambda i, j: (i, 0)),
                pl.BlockSpec((1, D), lambda i, j: (0, 0)),
                pl.BlockSpec((D, tn), lambda i, j: (0, j))]
    args = [x, gamma.reshape(1, D).astype(F32), wa]
    if rope:
        S = cos.shape[1]
        assert S % tm == 0
        n_pos = S // tm
        tmap = table_of_tile if table_of_tile is not None else (lambda j: 0)
        in_specs += [pl.BlockSpec((D, tn), lambda i, j: (0, j)),
                     pl.BlockSpec((None, tm, LANES), lambda i, j: (tmap(j), i % n_pos, 0)),
                     pl.BlockSpec((None, tm, LANES), lambda i, j: (tmap(j), i % n_pos, 0))]
        args += [wb, cos, sin]
    if bias is not None:
        in_specs.append(pl.BlockSpec((1, tn), lambda i, j: (0, j)))
        args.append(bias.reshape(1, N).astype(F32))
    out_shape = [jax.ShapeDtypeStruct((T, N), out_dtype)]
    out_specs = [pl.BlockSpec((tm, tn), lambda i, j: (i, j))]
    if emit_xn:
        out_shape.append(jax.ShapeDtypeStruct((D, T), BF16))
        out_specs.append(pl.BlockSpec((D, tm), lambda i, j: (0, i)))
    res = pl.pallas_call(
        functools.partial(_norm_mm_kernel, act=act, has_bias=bias is not None, rope=rope, emit_xn=emit_xn),
        grid=(T // tm, N // tn),
        in_specs=in_specs,
        out_specs=out_specs,
        out_shape=out_shape,
        scratch_shapes=[pltpu.VMEM((tm, D), BF16)],
        compiler_params=_cparams(("parallel", "arbitrary")),
    )(*args)
    return res if emit_xn else res[0]


def _mm_res_kernel(a_ref, w_ref, r_ref, o_ref):
    o_ref[...] = r_ref[...] + jnp.dot(a_ref[...], w_ref[...], preferred_element_type=F32)


def matmul_residual(a, w, res, *, tm=1024, tn=1024):
    T, K = a.shape
    N = w.shape[1]
    tm, tn = min(tm, T), min(tn, N)
    assert T % tm == 0 and N % tn == 0
    return pl.pallas_call(
        _mm_res_kernel,
        grid=(T // tm, N // tn),
        in_specs=[pl.BlockSpec((tm, K), lambda i, j: (i, 0)),
                  pl.BlockSpec((K, tn), lambda i, j: (0, j)),
                  pl.BlockSpec((tm, tn), lambda i, j: (i, j))],
        out_specs=pl.BlockSpec((tm, tn), lambda i, j: (i, j)),
        out_shape=jax.ShapeDtypeStruct((T, N), F32),
        compiler_params=_cparams(("parallel", "arbitrary")),
    )(a, w, res)


def _rmsnorm_kernel(x_ref, g_ref, o_ref):
    o_ref[...] = _rms_rows(x_ref[...], g_ref[...])


def rmsnorm(x, gamma, *, tm=1024):
    T, D = x.shape
    tm = min(tm, T)
    return pl.pallas_call(
        _rmsnorm_kernel,
        grid=(T // tm,),
        in_specs=[pl.BlockSpec((tm, D), lambda i: (i, 0)), pl.BlockSpec((1, D), lambda i: (0, 0))],
        out_specs=pl.BlockSpec((tm, D), lambda i: (i, 0)),
        out_shape=jax.ShapeDtypeStruct((T, D), F32),
        compiler_params=_cparams(("parallel",)),
    )(x, gamma.reshape(1, D).astype(F32))


def _peer_cand_tables(tn):
    fidx, vmask = [], []
    for k2 in range(16):
        fidx.append(k2); vmask.append(0.0)
    for k1 in range(1, 8):
        lim = PEER_TOPK // (k1 + 1)
        for k2 in range(8):
            fidx.append(k1 * 16 + k2); vmask.append(0.0 if k2 < lim else -np.inf)
    for k1 in range(8, 16):
        fidx.append(k1 * 16); vmask.append(0.0)
    fidx = np.broadcast_to(np.asarray(fidx, np.int32)[:, None], (80, tn))
    vmask = np.broadcast_to(np.asarray(vmask, np.float32)[:, None], (80, tn))
    return jnp.asarray(fidx), jnp.asarray(vmask)


def _top16_rows(s, exact_ties):
    n, tn = s.shape
    rows = lax.broadcasted_iota(jnp.int32, (n, tn), 0)
    rows16 = lax.broadcasted_iota(jnp.int32, (PEER_TOPK, tn), 0)
    tops = jnp.zeros((PEER_TOPK, tn), F32)
    unit = 2.0 ** 122
    v = s
    for k in range(PEER_TOPK):
        m = jnp.max(v, axis=0, keepdims=True)
        if exact_ties:
            hit = rows == jnp.min(jnp.where(v == m, rows, n), axis=0, keepdims=True)
        else:
            hit = v == m
        v = jnp.where(hit, -(32.0 + k) * unit, v)
        tops = jnp.where(rows16 == k, m, tops)
    was_picked = v <= -32.0 * unit
    rank = jnp.where(was_picked, v * (-1.0 / unit) - 32.0, float(PEER_TOPK))
    n_picked = jnp.sum(jnp.where(was_picked, 1.0, 0.0), axis=0, keepdims=True)
    return tops, rank, n_picked


def _peer_select_head(q_ref, keys_ref, fidx, vmask, exact_ties):
    tops, ranks, es, picked = [], [], [], []
    for p in range(2):
        q = q_ref[:, p * PEER_HALF_DIM:(p + 1) * PEER_HALF_DIM]
        s = lax.dot_general(keys_ref[p], q, _NT, preferred_element_type=F32)
        t, r, n_picked = _top16_rows(s, exact_ties)
        tops.append(t); ranks.append(r)
        es.append(jnp.exp(s - t[0:1, :]))
        picked.append(n_picked)
    ts1, ts2 = tops
    pieces = [ts1[0:1, :] + ts2]
    for k1 in range(1, 8):
        pieces.append(ts1[k1:k1 + 1, :] + ts2[0:8, :])
    pieces.append(ts1[8:16, :] + ts2[0:1, :])
    cand0 = jnp.concatenate(pieces, axis=0) + vmask
    cand = cand0
    for _ in range(PEER_TOPK):
        m = jnp.max(cand, axis=0, keepdims=True)
        if exact_ties:
            hit = fidx == jnp.min(jnp.where(cand == m, fidx, 4096), axis=0, keepdims=True)
        else:
            hit = cand == m
        cand = jnp.where(hit, -jnp.inf, cand)
    taken = jnp.logical_and(cand == -jnp.inf, vmask == 0.0)
    takenf = taken.astype(F32)
    picked.append(jnp.sum(takenf, axis=0, keepdims=True))
    unique = jnp.min(jnp.where((picked[0] == PEER_TOPK) & (picked[1] == PEER_TOPK) & (picked[2] == PEER_TOPK),
                               1.0, 0.0)) > 0.5
    best = ts1[0:1, :] + ts2[0:1, :]
    z = jnp.sum(jnp.where(taken, jnp.exp(cand0 - best), 0.0), axis=0, keepdims=True)
    counts = [jnp.sum(takenf[0:16, :], axis=0, keepdims=True)]
    for k1 in range(1, 8):
        counts.append(jnp.sum(takenf[16 + 8 * (k1 - 1):16 + 8 * k1, :], axis=0, keepdims=True))
    tail = takenf[72:80, :]
    cnt = jnp.zeros_like(ranks[0])
    for k1 in range(PEER_TOPK):
        nk = counts[k1] if k1 < 8 else tail[k1 - 8:k1 - 7, :]
        cnt = jnp.where(ranks[0] == float(k1), nk, cnt)
    return (cnt, ranks[1], es[0], es[1] / z), unique


def _peer_select_kernel(q_ref, keys_ref, fidx_ref, vmask_ref, cnt_ref, rank2_ref, e1_ref, e2_ref):
    fidx = fidx_ref[...]
    vmask = vmask_ref[...]

    def store(vals):
        for ref, val in zip((cnt_ref, rank2_ref, e1_ref, e2_ref), vals):
            ref[...] = val.astype(ref.dtype)

    vals, unique = _peer_select_head(q_ref, keys_ref, fidx, vmask, exact_ties=False)
    store(vals)

    @pl.when(jnp.logical_not(unique))
    def _():
        store(_peer_select_head(q_ref, keys_ref, fidx, vmask, exact_ties=True)[0])


def peer_select(q, keys, *, tn=512):
    T = q.shape[0]
    tn = min(tn, T)
    fidx, vmask = _peer_cand_tables(tn)
    rows = PEER_HEADS * PEER_N_KEYS
    ospec = pl.BlockSpec((PEER_N_KEYS, tn), lambda i, h: (h, i))
    return pl.pallas_call(
        _peer_select_kernel,
        grid=(T // tn, PEER_HEADS),
        in_specs=[pl.BlockSpec((tn, 2 * PEER_HALF_DIM), lambda i, h: (i, h)),
                  pl.BlockSpec((2, PEER_N_KEYS, PEER_HALF_DIM), lambda i, h: (h, 0, 0)),
                  pl.BlockSpec((80, tn), lambda i, h: (0, 0)),
                  pl.BlockSpec((80, tn), lambda i, h: (0, 0))],
        out_specs=[ospec] * 4,
        out_shape=[jax.ShapeDtypeStruct((rows, T), dt) for dt in (F32, BF16, F32, BF16)],
        compiler_params=_cparams(("parallel", "parallel")),
    )(q, keys, fidx, vmask)


def _peer_dense_kernel(xn_ref, u_ref, vt_ref, cnt_ref, rank2_ref, e1_ref, e2_ref, res_ref, o_ref,
                       acc_ref, g0_ref, g1_ref, *, c_per_step):
    j = pl.program_id(1)
    n_tiles = pl.num_programs(1) - 1

    @pl.when(j == 0)
    def _():
        acc_ref[...] = jnp.zeros_like(acc_ref)
        g1_ref[...] = jnp.zeros_like(g1_ref)

    @pl.when((j % 2 == 0) & (j < n_tiles))
    def _():
        _peer_dense_step(xn_ref, u_ref, vt_ref, cnt_ref, rank2_ref, e1_ref, e2_ref, acc_ref,
                         g1_ref, g0_ref, j, c_per_step)

    @pl.when(j % 2 == 1)
    def _():
        _peer_dense_step(xn_ref, u_ref, vt_ref, cnt_ref, rank2_ref, e1_ref, e2_ref, acc_ref,
                         g0_ref, g1_ref, j, c_per_step)

    @pl.when(j == n_tiles)
    def _():
        _peer_dense_step(xn_ref, u_ref, vt_ref, cnt_ref, rank2_ref, e1_ref, e2_ref, acc_ref,
                         g1_ref, None, j, c_per_step)
        o_ref[...] = res_ref[...] + acc_ref[...].T


def _peer_dense_step(xn_ref, u_ref, vt_ref, cnt_ref, rank2_ref, e1_ref, e2_ref, acc_ref, g_ref, g_next_ref,
                     j, c_per_step):
    tn = xn_ref.shape[1]
    bf16_rows = 16
    reps = PEER_N_KEYS // bf16_rows

    def row_tile(ref, row):
        r16 = jnp.broadcast_to(ref[pl.ds(row, 1), :], (bf16_rows, tn)).astype(BF16)
        return jnp.concatenate([r16] * reps, axis=0)

    c0 = jnp.maximum(j - 1, 0) * c_per_step
    up_rows = 2 * PEER_N_KEYS
    blocks = []
    for cc in range(c_per_step):
        if g_next_ref is not None and (cc * PEER_N_KEYS) % up_rows == 0:
            rs = slice(cc * PEER_N_KEYS, cc * PEER_N_KEYS + up_rows)
            hT = jnp.dot(u_ref[rs, :], xn_ref[...], preferred_element_type=F32)
            g_next_ref[rs, :] = _gelu_sigmoid(hT.astype(BF16))
        c = c0 + cc
        w = None
        for h in range(PEER_HEADS):
            row = h * PEER_N_KEYS + c
            n_row = row_tile(cnt_ref, row)
            e1_row = row_tile(e1_ref, row)
            sl = slice(h * PEER_N_KEYS, (h + 1) * PEER_N_KEYS)
            term = jnp.where(rank2_ref[sl, :] < n_row, e2_ref[sl, :], 0.0) * e1_row
            w = term if w is None else w + term
        blocks.append(w * g_ref[cc * PEER_N_KEYS:(cc + 1) * PEER_N_KEYS, :])
    aT = jnp.concatenate(blocks, axis=0)
    acc_ref[...] += jnp.dot(vt_ref[...], aT, preferred_element_type=F32)


def peer_dense(xn, u, vt, cnt, rank2, e1, e2, res, *, tn=512, te=2048):
    D, T = xn.shape
    E = u.shape[0]
    tn = min(tn, T)
    rows = PEER_HEADS * PEER_N_KEYS
    sel_spec = pl.BlockSpec((rows, tn), lambda i, j: (0, i))
    n_tiles = E // te
    assert n_tiles % 2 == 0
    return pl.pallas_call(
        functools.partial(_peer_dense_kernel, c_per_step=te // PEER_N_KEYS),
        grid=(T // tn, n_tiles + 1),
        in_specs=[pl.BlockSpec((D, tn), lambda i, j: (0, i)),
                  pl.BlockSpec((te, D), lambda i, j: (jnp.minimum(j, n_tiles - 1), 0)),
                  pl.BlockSpec((D, te), lambda i, j: (0, jnp.maximum(j - 1, 0))),
                  sel_spec, sel_spec, sel_spec, sel_spec,
                  pl.BlockSpec((tn, D), lambda i, j: (i, 0))],
        out_specs=pl.BlockSpec((tn, D), lambda i, j: (i, 0)),
        out_shape=jax.ShapeDtypeStruct((T, D), F32),
        scratch_shapes=[pltpu.VMEM((D, tn), F32), pltpu.VMEM((te, tn), BF16), pltpu.VMEM((te, tn), BF16)],
        compiler_params=_cparams(("parallel", "arbitrary")),
    )(xn, u, vt, cnt, rank2, e1, e2, res)


def peer_layer(h, gamma, w_q, sub_keys, u, v):
    q, xn = norm_matmul(h, gamma, w_q.astype(BF16), emit_xn=True)
    keys = sub_keys.reshape(2 * PEER_HEADS, PEER_N_KEYS, PEER_HALF_DIM).astype(BF16)
    cnt, rank2, e1, e2 = peer_select(q, keys)
    return peer_dense(xn, u.astype(BF16), v.T.astype(BF16), cnt, rank2, e1, e2, h)


def _cumsum_aug_kernel(lf_ref, tri_ref, place_q_ref, place_k_ref, ones_q_ref, ones_k_ref,
                       qa_ref, ka_ref, c_ref, carry_ref):
    @pl.when(pl.program_id(1) == 0)
    def _():
        carry_ref[...] = jnp.zeros_like(carry_ref)

    lf = lf_ref[0]
    c = jnp.dot(tri_ref[...], lf, preferred_element_type=F32, precision=lax.Precision.HIGHEST) + carry_ref[...]
    carry_ref[...] = c[-1:, :]
    c = c * LOG2E
    c_ref[0] = c
    hi = c.astype(BF16)
    r1 = c - hi.astype(F32)
    mid = r1.astype(BF16)
    lo = (r1 - mid.astype(F32)).astype(BF16)
    nh = N_HEADS
    lane = lax.broadcasted_iota(jnp.int32, c.shape, 1)
    parts = jnp.where(lane < nh, hi.astype(F32),
                      jnp.where(lane < 2 * nh, pltpu.roll(mid.astype(F32), nh, axis=1),
                                pltpu.roll(lo.astype(F32), 2 * nh, axis=1)))
    parts = jnp.where(lane < 3 * nh, parts, 0.0).astype(BF16)
    qa_ref[0] = (jnp.dot(parts, place_q_ref[...], preferred_element_type=F32) + ones_q_ref[...]).astype(BF16)
    ka_ref[0] = (jnp.dot(parts, place_k_ref[...], preferred_element_type=F32) + ones_k_ref[...]).astype(BF16)


def fox_bias_operands(logf, *, tc=256):
    B, S, _ = logf.shape
    nh = N_HEADS
    tri = jnp.asarray(np.tril(np.ones((tc, tc), np.float32)))
    pq = np.zeros((LANES, nh * LANES), np.float32)
    pk = np.zeros((LANES, nh * LANES), np.float32)
    oq = np.zeros((1, nh * LANES), np.float32)
    ok = np.zeros((1, nh * LANES), np.float32)
    for h in range(nh):
        for part in range(3):
            pq[part * nh + h, h * LANES + part] = 1.0
            pk[part * nh + h, h * LANES + 3 + part] = -1.0
            oq[0, h * LANES + 3 + part] = 1.0
            ok[0, h * LANES + part] = 1.0
    const = lambda a: pl.BlockSpec(a.shape, lambda b, i: (0,) * a.ndim)
    pq, pk, oq, ok = jnp.asarray(pq, BF16), jnp.asarray(pk, BF16), jnp.asarray(oq), jnp.asarray(ok)
    out = jax.ShapeDtypeStruct((B, S, nh * LANES), BF16)
    return pl.pallas_call(
        _cumsum_aug_kernel,
        grid=(B, S // tc),
        in_specs=[pl.BlockSpec((1, tc, LANES), lambda b, i: (b, i, 0)),
                  const(tri), const(pq), const(pk), const(oq), const(ok)],
        out_specs=[pl.BlockSpec((1, tc, nh * LANES), lambda b, i: (b, i, 0))] * 2
        + [pl.BlockSpec((1, tc, LANES), lambda b, i: (b, i, 0))],
        out_shape=[out, out, jax.ShapeDtypeStruct((B, S, LANES), F32)],
        scratch_shapes=[pltpu.VMEM((1, LANES), F32)],
        compiler_params=_cparams(("parallel", "arbitrary")),
    )(logf, tri, pq, pk, oq, ok)


def _fox_attn_kernel(first_ref, q_ref, qa_ref, kv_ref, ka_ref, o_ref, *, tq, tk, heads_per_step):
    qi = pl.program_id(2)
    t0 = qi * tq
    n_full = t0 // tk
    j_first = first_ref[(pl.program_id(0) * pl.num_programs(1) + pl.program_id(1)) * pl.num_programs(2) + qi]
    n_diag = tq // tk
    lanes = [slice(hh * LANES, (hh + 1) * LANES) for hh in range(heads_per_step)]
    qs = [jnp.concatenate([q_ref[0, :, lsl], qa_ref[0, :, lsl]], axis=1) for lsl in lanes]

    def step(j, carry, masked):
        rows = pl.ds(pl.multiple_of(j * tk, tk), tk)
        klane = lax.broadcasted_iota(jnp.int32, (tk, LANES), 1)
        new = []
        for lsl, q, (m, acc) in zip(lanes, qs, carry):
            kv = kv_ref[0, rows, lsl]
            kk = jnp.concatenate([kv, ka_ref[0, rows, lsl]], axis=1)
            ones_v = jnp.where(klane < HEAD_DIM, 1.0, kv).astype(BF16)
            s = lax.dot_general(q, kk, _NT, preferred_element_type=F32)
            if masked:
                qpos = t0 + lax.broadcasted_iota(jnp.int32, (tq, 1), 0)
                kpos = j * tk + lax.broadcasted_iota(jnp.int32, (1, tk), 1)
                s = jnp.where(kpos <= qpos, s, NEG_INF)
            m_new = jnp.maximum(m, jnp.max(s, axis=1, keepdims=True))
            p = jnp.exp2(s - m_new).astype(BF16)
            acc = jnp.exp2(m - m_new) * acc + jnp.dot(p, ones_v, preferred_element_type=F32)
            new.append((m_new, acc))
        return tuple(new)

    init = (jnp.full((tq, 1), NEG_INF, F32), jnp.zeros((tq, LANES), F32))
    n_pairs = (n_full - j_first) // 2
    carry = lax.fori_loop(
        0, n_pairs, lambda i, c: step(j_first + 2 * i + 1, step(j_first + 2 * i, c, False), False),
        (init,) * heads_per_step)
    carry = lax.fori_loop(j_first + 2 * n_pairs, n_full, functools.partial(step, masked=False), carry)
    for d in range(n_diag):
        carry = step(n_full + d, carry, True)
    outs = [acc / acc[:, 0:1] for (_, acc) in carry]
    lane = lax.broadcasted_iota(jnp.int32, (tq, LANES), 1)
    blocks = []
    for pair in range(heads_per_step // 2):
        a, b = outs[2 * pair], outs[2 * pair + 1]
        blocks.append(jnp.where(lane < HEAD_DIM, pltpu.roll(a, HEAD_DIM, axis=1), b))
    o_ref[0] = jnp.concatenate(blocks, axis=1).astype(o_ref.dtype) if len(blocks) > 1 else blocks[0].astype(o_ref.dtype)


def _head_slots(w, n_heads, second=None):
    D = w.shape[0]
    a = w.reshape(D, n_heads, HEAD_DIM)
    b = jnp.zeros_like(a) if second is None else second.reshape(D, n_heads, HEAD_DIM)
    return jnp.concatenate([a, b], axis=-1).reshape(D, n_heads * LANES)


def fox_layer(h, gamma, w_in, f_bias, w_out, B, S):
    aw = ATTN_WIDTH
    wq = _head_slots(w_in[:, :aw] * (HEAD_DIM ** -0.5 * LOG2E), N_HEADS)
    wkv = _head_slots(w_in[:, aw:2 * aw], N_HEADS, w_in[:, 2 * aw:3 * aw])
    w_main = jnp.concatenate([wq, wkv], axis=1).astype(BF16)
    wf = jnp.pad(w_in[:, 3 * aw:], ((0, 0), (0, LANES - N_HEADS))).astype(BF16)
    bf = jnp.pad(f_bias.astype(F32), (0, LANES - N_HEADS))
    qkv = norm_matmul(h, gamma, w_main)
    logf = norm_matmul(h, gamma, wf, bias=bf, act="log_sigmoid", out_dtype=F32)
    qa, ka, c2 = fox_bias_operands(logf.reshape(B, S, LANES))
    qkv = qkv.reshape(B, S, 2 * N_HEADS * LANES)
    o = fox_attention(qkv, qa, ka, c2)
    return matmul_residual(o.reshape(B * S, aw), w_out.astype(BF16), h)


FOX_NEGLIGIBLE_LOG2 = 160.0


def _norm_maxima_kernel(x_ref, ind_ref, o_ref):
    x = x_ref[0].astype(F32)
    ss = jnp.dot((x * x).astype(BF16), ind_ref[...], preferred_element_type=F32)
    o_ref[0, 0] = jnp.broadcast_to(jnp.max(ss, axis=0, keepdims=True), o_ref.shape[2:])


def fox_norm_maxima(qkv, tile):
    B, S, W = qkv.shape
    n = S // tile
    col = np.arange(W)
    slot, lane = col // LANES, col % LANES
    used = (slot < N_HEADS) | (lane < HEAD_DIM)
    ind = jnp.asarray((slot[:, None] == np.arange(LANES)[None, :]) & used[:, None], BF16)
    rows = min(256, tile)
    ss = pl.pallas_call(
        _norm_maxima_kernel,
        grid=(B, S // rows),
        in_specs=[pl.BlockSpec((1, rows, W), lambda b, i: (b, i, 0)),
                  pl.BlockSpec(ind.shape, lambda b, i: (0, 0))],
        out_specs=pl.BlockSpec((1, 1, 8, LANES), lambda b, i: (b, i, 0, 0)),
        out_shape=jax.ShapeDtypeStruct((B, S // rows, 8, LANES), F32),
        compiler_params=_cparams(("parallel", "parallel")),
    )(qkv, ind)
    ss = jnp.max(ss[:, :, 0, :2 * N_HEADS].reshape(B, n, tile // rows, 2 * N_HEADS), axis=2)
    return jnp.sqrt(ss * 1.01)


def fox_first_chunk(qkv, c2, tile, heads_per_step):
    B, S, _ = qkv.shape
    n = S // tile
    norms = fox_norm_maxima(qkv, tile)
    qmax, kmax = norms[..., :N_HEADS], norms[..., N_HEADS:]
    c = c2[..., :N_HEADS].reshape(B, n, tile, N_HEADS)
    cmax, cmin = jnp.max(c, axis=2), jnp.min(c, axis=2)
    upper = qmax[:, :, None] * kmax[:, None, :] + cmax[:, :, None] - cmin[:, None, :]
    own = -(qmax * kmax)
    earlier = jnp.arange(n)[None, :, None, None] > jnp.arange(n)[None, None, :, None]
    skip = (upper < own[:, :, None] - FOX_NEGLIGIBLE_LOG2) & earlier
    first = jnp.sum(jnp.cumprod(skip.astype(jnp.int32), axis=2), axis=2)
    first = jnp.min(first.reshape(B, n, N_HEADS // heads_per_step, heads_per_step), axis=-1)
    return jnp.transpose(first, (0, 2, 1)).reshape(-1).astype(jnp.int32)


def fox_attention(qkv, qa, ka, c2, *, tq=1024, tk=1024, heads_per_step=2):
    B, S, _ = qkv.shape
    tq = tk = min(tq, S)
    hs = heads_per_step
    wq = hs * LANES
    n_qblk = N_HEADS // hs
    first = fox_first_chunk(qkv, c2, tk, hs)
    grid_spec = pltpu.PrefetchScalarGridSpec(
        num_scalar_prefetch=1,
        grid=(B, n_qblk, S // tq),
        in_specs=[pl.BlockSpec((1, tq, wq), lambda b, h, i, first: (b, i, h)),
                  pl.BlockSpec((1, tq, wq), lambda b, h, i, first: (b, i, h)),
                  pl.BlockSpec((1, S, wq), lambda b, h, i, first: (b, 0, n_qblk + h)),
                  pl.BlockSpec((1, S, wq), lambda b, h, i, first: (b, 0, h))],
        out_specs=pl.BlockSpec((1, tq, hs * HEAD_DIM), lambda b, h, i, first: (b, i, h)))
    return pl.pallas_call(
        functools.partial(_fox_attn_kernel, tq=tq, tk=tk, heads_per_step=hs),
        grid_spec=grid_spec,
        out_shape=jax.ShapeDtypeStruct((B, S, ATTN_WIDTH), BF16),
        compiler_params=_cparams(("parallel", "parallel", "arbitrary")),
    )(first, qkv, qa, qkv, ka)


def _rot_half_cols(w):
    D = w.shape[0]
    a = w.reshape(D, -1, HEAD_DIM)
    half = HEAD_DIM // 2
    return jnp.concatenate([-a[..., half:], a[..., :half]], axis=-1).reshape(w.shape)


def _rope_tables(S):
    half = HEAD_DIM // 2
    inv_freq = ROPE_THETA ** (-jnp.arange(half, dtype=F32) / half)
    ang = jnp.arange(S, dtype=F32)[:, None] * inv_freq[None, :]
    c, s = jnp.cos(ang), jnp.sin(ang)
    c2, s2 = jnp.concatenate([c, c], axis=1), jnp.concatenate([s, s], axis=1)
    cos = jnp.stack([jnp.concatenate([c2, jnp.ones_like(c2)], axis=1), jnp.concatenate([c2, c2], axis=1)])
    sin = jnp.stack([jnp.concatenate([s2, jnp.zeros_like(s2)], axis=1), jnp.concatenate([s2, s2], axis=1)])
    return cos, sin


def _compress_kernel(x_ref, pea_ref, peb_ref, wa_ref, wb_ref, w2_ref, o_ref, pa_ref, pb0_ref, *, n_rows):
    u = pl.program_id(1)
    x = x_ref[0].astype(F32)
    pa = jnp.dot((x + pea_ref[...]).astype(BF16), wa_ref[...], preferred_element_type=F32)
    pb = jnp.dot((x + peb_ref[...]).astype(BF16), wb_ref[...], preferred_element_type=F32)

    def emit(slab, hid):
        y = jnp.dot(_gelu_tanh(hid).astype(BF16), w2_ref[...], preferred_element_type=F32)
        o_ref[0, pl.ds(pl.multiple_of(slab * n_rows, n_rows), n_rows), :] = y

    @pl.when(u == 0)
    def _():
        pb0_ref[...] = pb

    @pl.when(u > 0)
    def _():
        emit(u - 1, pa_ref[...] + pb)

    @pl.when(u == 3)
    def _():
        emit(3, pa + pltpu.roll(pb0_ref[...], n_rows - 1, axis=0))

    pa_ref[...] = pa


def nsa_compress(src, pe, w1, w2):
    B, S, W = src.shape
    G = NSA_GROUPS
    n_rows = S // 64
    half = CMP_BLOCK // 2
    cw = half * W
    xv = src.reshape(B, n_rows, 4 * cw)
    pe_flat = jnp.transpose(pe, (1, 0, 2)).reshape(CMP_BLOCK, W).astype(F32)
    pea, peb = pe_flat[:half].reshape(1, cw), pe_flat[half:].reshape(1, cw)
    eye = jnp.eye(G, dtype=F32)
    wfull = jnp.einsum('gldh,gk->lkdgh', w1.astype(F32), eye).reshape(CMP_BLOCK, W, G * CMP_HIDDEN)
    wa = wfull[:half].reshape(cw, G * CMP_HIDDEN).astype(BF16)
    wb = wfull[half:].reshape(cw, G * CMP_HIDDEN).astype(BF16)
    w2bd = jnp.einsum('ghd,gk->ghkd', w2.astype(F32), eye).reshape(G * CMP_HIDDEN, W).astype(BF16)
    const = lambda a: pl.BlockSpec(a.shape, lambda b, u: (0,) * a.ndim)
    return pl.pallas_call(
        functools.partial(_compress_kernel, n_rows=n_rows),
        grid=(B, 4),
        in_specs=[pl.BlockSpec((1, n_rows, cw), lambda b, u: (b, 0, u)),
                  const(pea), const(peb), const(wa), const(wb), const(w2bd)],
        out_specs=pl.BlockSpec((1, 4 * n_rows, W), lambda b, u: (b, 0, 0)),
        out_shape=jax.ShapeDtypeStruct((B, 4 * n_rows, W), F32),
        scratch_shapes=[pltpu.VMEM((n_rows, G * CMP_HIDDEN), F32), pltpu.VMEM((n_rows, G * CMP_HIDDEN), F32)],
        compiler_params=_cparams(("parallel", "arbitrary")),
    )(xv, pea, peb, wa, wb, w2bd)


def _nsa_attn_kernel(q_ref, kvs_ref, kvw_ref, kvc_ref, kvct_ref, gate_ref, bmat_ref, pmat_ref, o_ref,
                     *, tq, ts, tk, seq):
    R = NSA_Q_PER_GROUP
    n_slc = seq // SLC_BLOCK
    n_sel = min(SLC_TOPK, n_slc)
    assert ts & (ts - 1) == 0 and n_slc & (n_slc - 1) == 0 and tk % tq == 0 and tq % ts == 0
    log_slc = n_slc.bit_length() - 1
    qi = pl.program_id(2)
    t0 = qi * tq

    local = [_nsa_local_branches(q_ref, kvw_ref, kvc_ref, kvct_ref, t0 + i * ts, i * ts, ts, n_slc, n_sel, log_slc)
             for i in range(tq // ts)]
    o_c = jnp.concatenate([o[0][r * ts:(r + 1) * ts] for r in range(R) for o in local], axis=0)
    o_w = jnp.concatenate([o[1][r * ts:(r + 1) * ts] for r in range(R) for o in local], axis=0)
    sel_bias = jnp.concatenate([o[2] for o in local], axis=0)
    qs = jnp.concatenate([q_ref[0, :, r * LANES:(r + 1) * LANES] for r in range(R)], axis=0)
    _nsa_selected_and_combine(qs, o_c, o_w, sel_bias, kvs_ref, gate_ref, bmat_ref, pmat_ref, o_ref,
                              t0=t0, tq=tq, tk=tk, seq=seq)


def _nsa_local_branches(q_ref, kvw_ref, kvc_ref, kvct_ref, t0, row0, tq, n_slc, n_sel, log_slc):
    R = NSA_Q_PER_GROUP
    M = R * tq
    qs = jnp.concatenate([q_ref[0, row0:row0 + tq, r * LANES:(r + 1) * LANES] for r in range(R)], axis=0)

    wlen = WINDOW + tq
    start = jnp.maximum(t0 - WINDOW, 0)
    kvw = kvw_ref[0, pl.ds(pl.multiple_of(start, tq), wlen), :]
    s_w = lax.dot_general(qs, kvw, _NT, preferred_element_type=F32)
    qpos = t0 + (lax.broadcasted_iota(jnp.int32, (M, 1), 0) & (tq - 1))
    kpos = start + lax.broadcasted_iota(jnp.int32, (1, wlen), 1)
    s_w = jnp.where(kpos <= qpos, jnp.where(kpos > qpos - WINDOW, s_w, NEG_INF), NEG_INF)
    p_w = jnp.exp2(s_w - jnp.max(s_w, axis=1, keepdims=True)).astype(BF16)
    wlane = lax.broadcasted_iota(jnp.int32, (wlen, LANES), 1)
    acc_w = jnp.dot(p_w, jnp.where(wlane < HEAD_DIM, 1.0, kvw).astype(BF16), preferred_element_type=F32)
    o_w = acc_w / acc_w[:, 0:1]

    n_groups, gs = _nsa_row_groups(n_slc)
    o_c, sel_bias = _nsa_compressed_and_select(qs, kvc_ref, kvct_ref, t0, tq, n_slc, n_sel, gs, n_groups)
    return o_c, o_w, sel_bias


def _nsa_row_groups(n_slc):
    n_groups = max(1, min(4, n_slc // 8))
    assert n_slc % n_groups == 0 and (n_slc // n_groups) % 8 == 0
    return n_groups, n_slc // n_groups


def _nsa_compressed_and_select(qs, kvc_ref, kvct_ref, t0, tq, n_slc, n_sel, gs, n_groups):
    R = NSA_Q_PER_GROUP
    M = R * tq
    n_rows, n_j = n_groups * 4 * gs, n_groups * gs
    log_gs = gs.bit_length() - 1
    sT = lax.dot_general(kvc_ref[0, 0, 0:n_rows, :], qs, _NT, preferred_element_type=F32)
    rowc = lax.broadcasted_iota(jnp.int32, (n_rows, 1), 0)
    blk_j = ((rowc >> (log_gs + 2)) << log_gs) + (rowc & (gs - 1))
    blk_u = (rowc >> log_gs) & 3
    cmp_end = blk_j * SLC_BLOCK + blk_u * CMP_STRIDE + (CMP_BLOCK - 1)
    tcol = t0 + (lax.broadcasted_iota(jnp.int32, (1, M), 1) & (tq - 1))
    sm = jnp.where(cmp_end <= tcol, sT, NEG_INF)
    mx = jnp.max(sm, axis=0, keepdims=True)
    e = jnp.exp2(sm - mx)
    inv = jnp.where(mx > 0.5 * NEG_INF, 1.0 / jnp.sum(e, axis=0, keepdims=True), 0.0)
    pT = e * inv
    o_c = jnp.dot(kvct_ref[0, 0, :, 0:n_rows], pT.astype(BF16), preferred_element_type=F32).T

    psum = pT[:, 0:tq]
    for r in range(1, R):
        psum = psum + pT[:, r * tq:(r + 1) * tq]
    p0, p1, p2, p3 = (jnp.concatenate([psum[(4 * g + u) * gs:(4 * g + u + 1) * gs, :] for g in range(n_groups)],
                                      axis=0) for u in range(4))
    jrow = lax.broadcasted_iota(jnp.int32, (n_j, tq), 0)
    p3_prev = jnp.where(jrow == 0, 0.0, pltpu.roll(p3, 1, axis=0))
    imp = p0 + p1 + p2 + 0.5 * p3 + 0.5 * p3_prev
    cur = (t0 + lax.broadcasted_iota(jnp.int32, (n_j, tq), 1)) >> (SLC_BLOCK.bit_length() - 1)
    forced = (jrow == 0) | (jrow == cur) | (jrow == cur - 1)
    vals = jnp.where(forced, -jnp.inf, jnp.where(jrow <= cur, imp, -jnp.inf))
    sel0 = jnp.where(forced, 1.0, 0.0)
    n_free = n_sel - 3

    def pick(exact_ties):
        v, sel = vals, sel0
        for _ in range(n_free):
            m = jnp.max(v, axis=0, keepdims=True)
            if exact_ties:
                hit = jrow == jnp.min(jnp.where(v == m, jrow, n_slc), axis=0, keepdims=True)
            else:
                hit = v == jnp.where(m == -jnp.inf, jnp.nan, m)
            sel = jnp.where(hit, 1.0, sel)
            v = jnp.where(hit, -jnp.inf, v)
        return sel

    sel_fast = pick(False)
    n_cand = jnp.sum(jnp.where(vals > -jnp.inf, 1.0, 0.0), axis=0, keepdims=True)
    n_picked = jnp.sum(sel_fast - sel0, axis=0, keepdims=True)
    unique = jnp.min(jnp.where(n_picked == jnp.minimum(n_cand, float(n_free)), 1.0, 0.0)) > 0.5
    sel = lax.cond(unique, lambda: sel_fast, lambda: pick(True))
    if n_j < n_slc:
        sel = jnp.concatenate([sel, jnp.zeros((n_slc - n_j, tq), F32)], axis=0)
    sel_bias = ((sel.T - 1.0) * (-NEG_INF)).astype(BF16)
    return o_c, sel_bias


def _nsa_selected_and_combine(qs, o_c, o_w, sel_bias, kvs_ref, gate_ref, bmat_ref, pmat_ref, o_ref,
                              *, t0, tq, tk, seq):
    R = NSA_Q_PER_GROUP
    n_slc = seq // SLC_BLOCK
    blocks_per_chunk = tk // SLC_BLOCK
    n_parts = 2
    hp = R // n_parts
    mp = hp * tq
    qparts = [qs[i * mp:(i + 1) * mp] for i in range(n_parts)]
    qrow = t0 + (lax.broadcasted_iota(jnp.int32, (mp, 1), 0) & (tq - 1))
    kcol = lax.broadcasted_iota(jnp.int32, (1, tk), 1)
    n_chunks = seq // tk
    j_last = t0 // tk

    klane = lax.broadcasted_iota(jnp.int32, (tk, LANES), 1)

    def slc_step(j, carry, masked):
        kv = kvs_ref[0, pl.ds(pl.multiple_of(j * tk, tk), tk), :]
        k_sel = jnp.where(klane < HEAD_DIM, kv, bmat_ref[...]).astype(BF16)
        ones_v = jnp.where(klane < HEAD_DIM, 1.0, kv).astype(BF16)
        off = pl.multiple_of(blocks_per_chunk * (n_chunks - 1 - j), blocks_per_chunk)
        place = pmat_ref[pl.ds(off, n_slc), :].astype(BF16)
        q_bias = jnp.dot(sel_bias, place, preferred_element_type=F32).astype(BF16)
        q_bias = jnp.concatenate([q_bias] * hp, axis=0)
        new = []
        for qp, (m, acc) in zip(qparts, carry):
            s = lax.dot_general(qp + q_bias, k_sel, _NT, preferred_element_type=F32)
            if masked:
                s = jnp.where(j * tk + kcol <= qrow, s, NEG_INF)
            m_new = jnp.maximum(m, jnp.max(s, axis=1, keepdims=True))
            p = jnp.exp2(s - m_new).astype(BF16)
            acc = jnp.exp2(m - m_new) * acc + jnp.dot(p, ones_v, preferred_element_type=F32)
            new.append((m_new, acc))
        return tuple(new)

    init = (jnp.full((mp, 1), NEG_INF, F32), jnp.zeros((mp, LANES), F32))
    n_pairs = j_last // 2
    carry = lax.fori_loop(0, n_pairs, lambda i, c: slc_step(2 * i + 1, slc_step(2 * i, c, False), False),
                          (init,) * n_parts)
    carry = lax.fori_loop(2 * n_pairs, j_last, functools.partial(slc_step, masked=False), carry)
    carry = slc_step(j_last, carry, True)
    o_s = jnp.concatenate([acc / acc[:, 0:1] for (_, acc) in carry], axis=0)

    gates = gate_ref[0]
    lane = lax.broadcasted_iota(jnp.int32, (tq, LANES), 1)
    comb = []
    for r in range(R):
        rs = slice(r * tq, (r + 1) * tq)
        comb.append(gates[:, r:r + 1] * o_c[rs] + gates[:, R + r:R + r + 1] * o_s[rs]
                    + gates[:, 2 * R + r:2 * R + r + 1] * o_w[rs])
    out = [jnp.where(lane < HEAD_DIM, pltpu.roll(comb[2 * i], HEAD_DIM, axis=1), comb[2 * i + 1])
           for i in range(R // 2)]
    o_ref[0] = jnp.concatenate(out, axis=1).astype(o_ref.dtype)


def nsa_attention(qkv, kvc, kvct, gates, *, tq=512, ts=256, tk=1024):
    B, S, _ = qkv.shape
    G, R = NSA_GROUPS, NSA_Q_PER_GROUP
    tk = min(tk, S)
    n_slc = S // SLC_BLOCK
    n_cmp = kvc.shape[2]
    bpc = tk // SLC_BLOCK
    assert bpc <= LANES - HEAD_DIM
    off = bpc * (S // tk - 1)
    lane = np.arange(LANES)[None, :]
    bmat = jnp.asarray(lane - HEAD_DIM == np.arange(tk)[:, None] // SLC_BLOCK, BF16)
    pmat = jnp.asarray((np.arange(n_slc + off)[:, None] - off == lane - HEAD_DIM) & (lane >= HEAD_DIM)
                       & (lane < HEAD_DIM + bpc), F32)
    slc_blk0, win_blk0 = N_HEADS, N_HEADS + G
    return pl.pallas_call(
        functools.partial(_nsa_attn_kernel, tq=tq, ts=ts, tk=tk, seq=S),
        grid=(B, G, S // tq),
        in_specs=[pl.BlockSpec((1, tq, R * LANES), lambda b, g, i: (b, i, g)),
                  pl.BlockSpec((1, S, LANES), lambda b, g, i: (b, 0, slc_blk0 + g)),
                  pl.BlockSpec((1, S, LANES), lambda b, g, i: (b, 0, win_blk0 + g)),
                  pl.BlockSpec((1, 1, n_cmp, LANES), lambda b, g, i: (b, g, 0, 0)),
                  pl.BlockSpec((1, 1, LANES, n_cmp), lambda b, g, i: (b, g, 0, 0)),
                  pl.BlockSpec((1, tq, LANES), lambda b, g, i: (b, i, g)),
                  pl.BlockSpec(bmat.shape, lambda b, g, i: (0, 0)),
                  pl.BlockSpec(pmat.shape, lambda b, g, i: (0, 0))],
        out_specs=pl.BlockSpec((1, tq, R * HEAD_DIM), lambda b, g, i: (b, i, g)),
        out_shape=jax.ShapeDtypeStruct((B, S, ATTN_WIDTH), BF16),
        compiler_params=_cparams(("parallel", "parallel", "arbitrary")),
    )(qkv, qkv, qkv, kvc, kvct, gates, bmat, pmat)


def nsa_layer(h, gamma, w_in, pe_k, w1_k, w2_k, pe_v, w1_v, w2_v, w_out, B, S):
    G, R, hd, aw = NSA_GROUPS, NSA_Q_PER_GROUP, HEAD_DIM, ATTN_WIDTH
    kvd = G * hd
    sec = lambda i: w_in[:, aw + i * kvd: aw + (i + 1) * kvd]
    wq = w_in[:, :aw] * (hd ** -0.5 * LOG2E)
    wa = jnp.concatenate([_head_slots(wq, N_HEADS), _head_slots(sec(2), G, sec(3)),
                          _head_slots(sec(4), G, sec(5))], axis=1).astype(BF16)
    wb = jnp.concatenate([_head_slots(_rot_half_cols(wq), N_HEADS), _head_slots(_rot_half_cols(sec(2)), G),
                          _head_slots(_rot_half_cols(sec(4)), G)], axis=1).astype(BF16)
    cos, sin = _rope_tables(S)
    qkv = norm_matmul(h, gamma, wa, wb=wb, cos=cos, sin=sin)
    kc_src = norm_matmul(h, gamma, sec(0).astype(BF16), wb=_rot_half_cols(sec(0)).astype(BF16),
                         cos=cos, sin=sin, table_of_tile=lambda j: 1)
    vc_src = norm_matmul(h, gamma, sec(1).astype(BF16))
    wg = w_in[:, aw + 6 * kvd:].reshape(-1, 3, G, R)
    wg = jnp.transpose(wg, (0, 2, 1, 3)).reshape(-1, G, 3 * R)
    wg = jnp.pad(wg, ((0, 0), (0, 0), (0, LANES - 3 * R))).reshape(-1, G * LANES).astype(BF16)
    gates = norm_matmul(h, gamma, wg, act="sigmoid", out_dtype=F32)
    kc = nsa_compress(kc_src.reshape(B, S, kvd), pe_k, w1_k, w2_k)
    vc = nsa_compress(vc_src.reshape(B, S, kvd), pe_v, w1_v, w2_v)
    n_cmp = kc.shape[1]
    kvc = jnp.concatenate([kc.reshape(B, n_cmp, G, hd), vc.reshape(B, n_cmp, G, hd)], axis=-1)
    n_groups, gs = _nsa_row_groups(S // SLC_BLOCK)
    kvc = kvc.reshape(B, 4, n_groups, gs, G, 2 * hd)
    kvc = jnp.transpose(kvc, (0, 4, 2, 1, 3, 5)).reshape(B, G, n_cmp, 2 * hd).astype(BF16)
    kvct = jnp.swapaxes(kvc, 2, 3)
    o = nsa_attention(qkv.reshape(B, S, -1), kvc, kvct, gates.reshape(B, S, G * LANES))
    return matmul_residual(o.reshape(B * S, aw), w_out.astype(BF16), h)


def kernel(x, l0_attn_norm, l0_w_in, l0_cmp_pe_k, l0_cmp_w1_k, l0_cmp_w2_k, l0_cmp_pe_v, l0_cmp_w1_v,
           l0_cmp_w2_v, l0_w_out, l0_ffn_norm, l0_peer_wq, l0_peer_keys, l0_peer_u, l0_peer_v,
           l1_attn_norm, l1_w_in, l1_f_bias, l1_w_out, l1_ffn_norm, l1_peer_wq, l1_peer_keys, l1_peer_u,
           l1_peer_v, final_norm):
    B, S, D = x.shape
    h = x.reshape(B * S, D)
    h = nsa_layer(h, l0_attn_norm, l0_w_in, l0_cmp_pe_k, l0_cmp_w1_k, l0_cmp_w2_k, l0_cmp_pe_v, l0_cmp_w1_v,
                  l0_cmp_w2_v, l0_w_out, B, S)
    h = peer_layer(h, l0_ffn_norm, l0_peer_wq, l0_peer_keys, l0_peer_u, l0_peer_v)
    h = fox_layer(h, l1_attn_norm, l1_w_in, l1_f_bias, l1_w_out, B, S)
    h = peer_layer(h, l1_ffn_norm, l1_peer_wq, l1_peer_keys, l1_peer_u, l1_peer_v)
    return rmsnorm(h, final_norm).reshape(B, S, D)
```

```python
import functools

import numpy as np
import jax
import jax.numpy as jnp
from jax import lax
from jax.experimental import pallas as pl
from jax.experimental.pallas import tpu as pltpu

F32 = jnp.float32
BF16 = jnp.bfloat16

D_MODEL = 1024
N_HEADS = 16
HEAD_DIM = 64
ATTN_WIDTH = N_HEADS * HEAD_DIM
NSA_GROUPS = 4
NSA_Q_PER_GROUP = N_HEADS // NSA_GROUPS
CMP_BLOCK = 32
CMP_STRIDE = 16
CMP_HIDDEN = 2 * HEAD_DIM
SLC_BLOCK = 64
SLC_TOPK = 16
WINDOW = 512
FORCE_SCORE = 1.0e4
ROPE_THETA = 10000.0
PEER_HEADS = 8
PEER_N_KEYS = 128
PEER_TOPK = 16
PEER_HALF_DIM = 128
RMS_EPS = 1e-6
NEG_INF = -1e30
LOG2E = 1.4426950408889634

LANES = 128
VMEM_LIMIT_BYTES = 56 * 1024 * 1024

_NT = (((1,), (1,)), ((), ()))


def _cparams(sem, vmem=VMEM_LIMIT_BYTES, flags=None):
    return pltpu.CompilerParams(dimension_semantics=sem, vmem_limit_bytes=vmem, flags=flags)


def _gelu_tanh(x):
    return 0.5 * x * (1.0 + jnp.tanh(0.7978845608028654 * (x + 0.044715 * (x * x * x))))


def _gelu_sigmoid(x):
    c = -2.0 * 0.7978845608028654 * LOG2E
    t = x * (c + (c * 0.044715) * (x * x))
    return x / (1.0 + jnp.exp2(t))


def _rms_rows(x, g):
    ms = jnp.mean(x * x, axis=-1, keepdims=True)
    return x * lax.rsqrt(ms + RMS_EPS) * g


def _norm_mm_kernel(*refs, act, has_bias, rope, emit_xn):
    it = iter(refs)
    x_ref, g_ref = next(it), next(it)
    wa_ref = next(it)
    wb_ref = next(it) if rope else None
    cos_ref = next(it) if rope else None
    sin_ref = next(it) if rope else None
    b_ref = next(it) if has_bias else None
    o_ref = next(it)
    xo_ref = next(it) if emit_xn else None
    xn_ref = next(it)

    @pl.when(pl.program_id(1) == 0)
    def _():
        xn = _rms_rows(x_ref[...], g_ref[...])
        xn_ref[...] = xn.astype(BF16)
        if emit_xn:
            xo_ref[...] = xn.T.astype(BF16)

    xn = xn_ref[...]
    y = jnp.dot(xn, wa_ref[...], preferred_element_type=F32)
    if rope:
        yb = jnp.dot(xn, wb_ref[...], preferred_element_type=F32)
        cos, sin = cos_ref[...], sin_ref[...]
        for s in range(y.shape[1] // LANES):
            sl = slice(s * LANES, (s + 1) * LANES)
            o_ref[:, sl] = (y[:, sl] * cos + yb[:, sl] * sin).astype(o_ref.dtype)
        return
    if has_bias:
        y = y + b_ref[...]
    if act == "sigmoid":
        y = jax.nn.sigmoid(y)
    elif act == "log_sigmoid":
        y = jax.nn.log_sigmoid(y)
    o_ref[...] = y.astype(o_ref.dtype)


def norm_matmul(x, gamma, wa, *, wb=None, cos=None, sin=None, table_of_tile=None, bias=None,
                act=None, out_dtype=BF16, emit_xn=False, tm=1024, tn=1024):
    T, D = x.shape
    N = wa.shape[1]
    tm, tn = min(tm, T), min(tn, N)
    assert T % tm == 0 and N % tn == 0 and tn % LANES == 0
    rope = wb is not None
    in_specs = [pl.BlockSpec((tm, D), lambda i, j: (i, 0)),
                pl.BlockSpec((1, D), lambda i, j: (0, 0)),
                pl.BlockSpec((D, tn), lambda i, j: (0, j))]
    args = [x, gamma.reshape(1, D).astype(F32), wa]
    if rope:
        S = cos.shape[1]
        assert S % tm == 0
        n_pos = S // tm
        tmap = table_of_tile if table_of_tile is not None else (lambda j: 0)
        in_specs += [pl.BlockSpec((D, tn), lambda i, j: (0, j)),
                     pl.BlockSpec((None, tm, LANES), lambda i, j: (tmap(j), i % n_pos, 0)),
                     pl.BlockSpec((None, tm, LANES), lambda i, j: (tmap(j), i % n_pos, 0))]
        args += [wb, cos, sin]
    if bias is not None:
        in_specs.append(pl.BlockSpec((1, tn), lambda i, j: (0, j)))
        args.append(bias.reshape(1, N).astype(F32))
    out_shape = [jax.ShapeDtypeStruct((T, N), out_dtype)]
    out_specs = [pl.BlockSpec((tm, tn), lambda i, j: (i, j))]
    if emit_xn:
        out_shape.append(jax.ShapeDtypeStruct((D, T), BF16))
        out_specs.append(pl.BlockSpec((D, tm), lambda i, j: (0, i)))
    res = pl.pallas_call(
        functools.partial(_norm_mm_kernel, act=act, has_bias=bias is not None, rope=rope, emit_xn=emit_xn),
        grid=(T // tm, N // tn),
        in_specs=in_specs,
        out_specs=out_specs,
        out_shape=out_shape,
        scratch_shapes=[pltpu.VMEM((tm, D), BF16)],
        compiler_params=_cparams(("parallel", "arbitrary")),
    )(*args)
    return res if emit_xn else res[0]


def _mm_res_kernel(a_ref, w_ref, r_ref, o_ref):
    o_ref[...] = r_ref[...] + jnp.dot(a_ref[...], w_ref[...], preferred_element_type=F32)


def matmul_residual(a, w, res, *, tm=1024, tn=1024):
    T, K = a.shape
    N = w.shape[1]
    tm, tn = min(tm, T), min(tn, N)
    assert T % tm == 0 and N % tn == 0
    return pl.pallas_call(
        _mm_res_kernel,
        grid=(T // tm, N // tn),
        in_specs=[pl.BlockSpec((tm, K), lambda i, j: (i, 0)),
                  pl.BlockSpec((K, tn), lambda i, j: (0, j)),
                  pl.BlockSpec((tm, tn), lambda i, j: (i, j))],
        out_specs=pl.BlockSpec((tm, tn), lambda i, j: (i, j)),
        out_shape=jax.ShapeDtypeStruct((T, N), F32),
        compiler_params=_cparams(("parallel", "arbitrary")),
    )(a, w, res)


def _rmsnorm_kernel(x_ref, g_ref, o_ref):
    o_ref[...] = _rms_rows(x_ref[...], g_ref[...])


def rmsnorm(x, gamma, *, tm=1024):
    T, D = x.shape
    tm = min(tm, T)
    return pl.pallas_call(
        _rmsnorm_kernel,
        grid=(T // tm,),
        in_specs=[pl.BlockSpec((tm, D), lambda i: (i, 0)), pl.BlockSpec((1, D), lambda i: (0, 0))],
        out_specs=pl.BlockSpec((tm, D), lambda i: (i, 0)),
        out_shape=jax.ShapeDtypeStruct((T, D), F32),
        compiler_params=_cparams(("parallel",)),
    )(x, gamma.reshape(1, D).astype(F32))


def _peer_cand_tables(tn):
    fidx, vmask = [], []
    for k2 in range(16):
        fidx.append(k2); vmask.append(0.0)
    for k1 in range(1, 8):
        lim = PEER_TOPK // (k1 + 1)
        for k2 in range(8):
            fidx.append(k1 * 16 + k2); vmask.append(0.0 if k2 < lim else -np.inf)
    for k1 in range(8, 16):
        fidx.append(k1 * 16); vmask.append(0.0)
    fidx = np.broadcast_to(np.asarray(fidx, np.int32)[:, None], (80, tn))
    vmask = np.broadcast_to(np.asarray(vmask, np.float32)[:, None], (80, tn))
    return jnp.asarray(fidx), jnp.asarray(vmask)


def _top16_rows(s, exact_ties):
    n, tn = s.shape
    rows = lax.broadcasted_iota(jnp.int32, (n, tn), 0)
    rows16 = lax.broadcasted_iota(jnp.int32, (PEER_TOPK, tn), 0)
    tops = jnp.zeros((PEER_TOPK, tn), F32)
    unit = 2.0 ** 122
    v = s
    for k in range(PEER_TOPK):
        m = jnp.max(v, axis=0, keepdims=True)
        if exact_ties:
            hit = rows == jnp.min(jnp.where(v == m, rows, n), axis=0, keepdims=True)
        else:
            hit = v == m
        v = jnp.where(hit, -(32.0 + k) * unit, v)
        tops = jnp.where(rows16 == k, m, tops)
    was_picked = v <= -32.0 * unit
    rank = jnp.where(was_picked, v * (-1.0 / unit) - 32.0, float(PEER_TOPK))
    n_picked = jnp.sum(jnp.where(was_picked, 1.0, 0.0), axis=0, keepdims=True)
    return tops, rank, n_picked


def _peer_select_head(q_ref, keys_ref, fidx, vmask, exact_ties):
    tops, ranks, es, picked = [], [], [], []
    for p in range(2):
        q = q_ref[:, p * PEER_HALF_DIM:(p + 1) * PEER_HALF_DIM]
        s = lax.dot_general(keys_ref[p], q, _NT, preferred_element_type=F32)
        t, r, n_picked = _top16_rows(s, exact_ties)
        tops.append(t); ranks.append(r)
        es.append(jnp.exp(s - t[0:1, :]))
        picked.append(n_picked)
    ts1, ts2 = tops
    pieces = [ts1[0:1, :] + ts2]
    for k1 in range(1, 8):
        pieces.append(ts1[k1:k1 + 1, :] + ts2[0:8, :])
    pieces.append(ts1[8:16, :] + ts2[0:1, :])
    cand0 = jnp.concatenate(pieces, axis=0) + vmask
    cand = cand0
    for _ in range(PEER_TOPK):
        m = jnp.max(cand, axis=0, keepdims=True)
        if exact_ties:
            hit = fidx == jnp.min(jnp.where(cand == m, fidx, 4096), axis=0, keepdims=True)
        else:
            hit = cand == m
        cand = jnp.where(hit, -jnp.inf, cand)
    taken = jnp.logical_and(cand == -jnp.inf, vmask == 0.0)
    takenf = taken.astype(F32)
    picked.append(jnp.sum(takenf, axis=0, keepdims=True))
    unique = jnp.min(jnp.where((picked[0] == PEER_TOPK) & (picked[1] == PEER_TOPK) & (picked[2] == PEER_TOPK),
                               1.0, 0.0)) > 0.5
    best = ts1[0:1, :] + ts2[0:1, :]
    z = jnp.sum(jnp.where(taken, jnp.exp(cand0 - best), 0.0), axis=0, keepdims=True)
    counts = [jnp.sum(takenf[0:16, :], axis=0, keepdims=True)]
    for k1 in range(1, 8):
        counts.append(jnp.sum(takenf[16 + 8 * (k1 - 1):16 + 8 * k1, :], axis=0, keepdims=True))
    tail = takenf[72:80, :]
    cnt = jnp.zeros_like(ranks[0])
    for k1 in range(PEER_TOPK):
        nk = counts[k1] if k1 < 8 else tail[k1 - 8:k1 - 7, :]
        cnt = jnp.where(ranks[0] == float(k1), nk, cnt)
    return (cnt, ranks[1], es[0], es[1] / z), unique


def _peer_select_kernel(q_ref, keys_ref, fidx_ref, vmask_ref, cnt_ref, rank2_ref, e1_ref, e2_ref):
    fidx = fidx_ref[...]
    vmask = vmask_ref[...]

    def store(vals):
        for ref, val in zip((cnt_ref, rank2_ref, e1_ref, e2_ref), vals):
            ref[...] = val.astype(ref.dtype)

    vals, unique = _peer_select_head(q_ref, keys_ref, fidx, vmask, exact_ties=False)
    store(vals)

    @pl.when(jnp.logical_not(unique))
    def _():
        store(_peer_select_head(q_ref, keys_ref, fidx, vmask, exact_ties=True)[0])


def peer_select(q, keys, *, tn=512):
    T = q.shape[0]
    tn = min(tn, T)
    fidx, vmask = _peer_cand_tables(tn)
    rows = PEER_HEADS * PEER_N_KEYS
    ospec = pl.BlockSpec((PEER_N_KEYS, tn), lambda i, h: (h, i))
    return pl.pallas_call(
        _peer_select_kernel,
        grid=(T // tn, PEER_HEADS),
        in_specs=[pl.BlockSpec((tn, 2 * PEER_HALF_DIM), lambda i, h: (i, h)),
                  pl.BlockSpec((2, PEER_N_KEYS, PEER_HALF_DIM), lambda i, h: (h, 0, 0)),
                  pl.BlockSpec((80, tn), lambda i, h: (0, 0)),
                  pl.BlockSpec((80, tn), lambda i, h: (0, 0))],
        out_specs=[ospec] * 4,
        out_shape=[jax.ShapeDtypeStruct((rows, T), dt) for dt in (F32, BF16, F32, BF16)],
        compiler_params=_cparams(("parallel", "parallel")),
    )(q, keys, fidx, vmask)


def _peer_dense_kernel(xn_ref, u_ref, vt_ref, cnt_ref, rank2_ref, e1_ref, e2_ref, res_ref, o_ref,
                       acc_ref, g0_ref, g1_ref, *, c_per_step):
    j = pl.program_id(1)
    n_tiles = pl.num_programs(1) - 1

    @pl.when(j == 0)
    def _():
        acc_ref[...] = jnp.zeros_like(acc_ref)
        g1_ref[...] = jnp.zeros_like(g1_ref)

    @pl.when((j % 2 == 0) & (j < n_tiles))
    def _():
        _peer_dense_step(xn_ref, u_ref, vt_ref, cnt_ref, rank2_ref, e1_ref, e2_ref, acc_ref,
                         g1_ref, g0_ref, j, c_per_step)

    @pl.when(j % 2 == 1)
    def _():
        _peer_dense_step(xn_ref, u_ref, vt_ref, cnt_ref, rank2_ref, e1_ref, e2_ref, acc_ref,
                         g0_ref, g1_ref, j, c_per_step)

    @pl.when(j == n_tiles)
    def _():
        _peer_dense_step(xn_ref, u_ref, vt_ref, cnt_ref, rank2_ref, e1_ref, e2_ref, acc_ref,
                         g1_ref, None, j, c_per_step)
        o_ref[...] = res_ref[...] + acc_ref[...].T


def _peer_dense_step(xn_ref, u_ref, vt_ref, cnt_ref, rank2_ref, e1_ref, e2_ref, acc_ref, g_ref, g_next_ref,
                     j, c_per_step):
    tn = xn_ref.shape[1]
    bf16_rows = 16
    reps = PEER_N_KEYS // bf16_rows

    def row_tile(ref, row):
        r16 = jnp.broadcast_to(ref[pl.ds(row, 1), :], (bf16_rows, tn)).astype(BF16)
        return jnp.concatenate([r16] * reps, axis=0)

    c0 = jnp.maximum(j - 1, 0) * c_per_step
    up_rows = 2 * PEER_N_KEYS
    blocks = []
    for cc in range(c_per_step):
        if g_next_ref is not None and (cc * PEER_N_KEYS) % up_rows == 0:
            rs = slice(cc * PEER_N_KEYS, cc * PEER_N_KEYS + up_rows)
            hT = jnp.dot(u_ref[rs, :], xn_ref[...], preferred_element_type=F32)
            g_next_ref[rs, :] = _gelu_sigmoid(hT.astype(BF16))
        c = c0 + cc
        w = None
        for h in range(PEER_HEADS):
            row = h * PEER_N_KEYS + c
            n_row = row_tile(cnt_ref, row)
            e1_row = row_tile(e1_ref, row)
            sl = slice(h * PEER_N_KEYS, (h + 1) * PEER_N_KEYS)
            term = jnp.where(rank2_ref[sl, :] < n_row, e2_ref[sl, :], 0.0) * e1_row
            w = term if w is None else w + term
        blocks.append(w * g_ref[cc * PEER_N_KEYS:(cc + 1) * PEER_N_KEYS, :])
    aT = jnp.concatenate(blocks, axis=0)
    acc_ref[...] += jnp.dot(vt_ref[...], aT, preferred_element_type=F32)


def peer_dense(xn, u, vt, cnt, rank2, e1, e2, res, *, tn=512, te=2048):
    D, T = xn.shape
    E = u.shape[0]
    tn = min(tn, T)
    rows = PEER_HEADS * PEER_N_KEYS
    sel_spec = pl.BlockSpec((rows, tn), lambda i, j: (0, i))
    n_tiles = E // te
    assert n_tiles % 2 == 0
    return pl.pallas_call(
        functools.partial(_peer_dense_kernel, c_per_step=te // PEER_N_KEYS),
        grid=(T // tn, n_tiles + 1),
        in_specs=[pl.BlockSpec((D, tn), lambda i, j: (0, i)),
                  pl.BlockSpec((te, D), lambda i, j: (jnp.minimum(j, n_tiles - 1), 0)),
                  pl.BlockSpec((D, te), lambda i, j: (0, jnp.maximum(j - 1, 0))),
                  sel_spec, sel_spec, sel_spec, sel_spec,
                  pl.BlockSpec((tn, D), lambda i, j: (i, 0))],
        out_specs=pl.BlockSpec((tn, D), lambda i, j: (i, 0)),
        out_shape=jax.ShapeDtypeStruct((T, D), F32),
        scratch_shapes=[pltpu.VMEM((D, tn), F32), pltpu.VMEM((te, tn), BF16), pltpu.VMEM((te, tn), BF16)],
        compiler_params=_cparams(("parallel", "arbitrary")),
    )(xn, u, vt, cnt, rank2, e1, e2, res)


def peer_layer(h, gamma, w_q, sub_keys, u, v):
    q, xn = norm_matmul(h, gamma, w_q.astype(BF16), emit_xn=True)
    keys = sub_keys.reshape(2 * PEER_HEADS, PEER_N_KEYS, PEER_HALF_DIM).astype(BF16)
    cnt, rank2, e1, e2 = peer_select(q, keys)
    return peer_dense(xn, u.astype(BF16), v.T.astype(BF16), cnt, rank2, e1, e2, h)


def _cumsum_aug_kernel(lf_ref, tri_ref, place_q_ref, place_k_ref, ones_q_ref, ones_k_ref,
                       qa_ref, ka_ref, c_ref, carry_ref):
    @pl.when(pl.program_id(1) == 0)
    def _():
        carry_ref[...] = jnp.zeros_like(carry_ref)

    lf = lf_ref[0]
    c = jnp.dot(tri_ref[...], lf, preferred_element_type=F32, precision=lax.Precision.HIGHEST) + carry_ref[...]
    carry_ref[...] = c[-1:, :]
    c = c * LOG2E
    c_ref[0] = c
    hi = c.astype(BF16)
    r1 = c - hi.astype(F32)
    mid = r1.astype(BF16)
    lo = (r1 - mid.astype(F32)).astype(BF16)
    nh = N_HEADS
    lane = lax.broadcasted_iota(jnp.int32, c.shape, 1)
    parts = jnp.where(lane < nh, hi.astype(F32),
                      jnp.where(lane < 2 * nh, pltpu.roll(mid.astype(F32), nh, axis=1),
                                pltpu.roll(lo.astype(F32), 2 * nh, axis=1)))
    parts = jnp.where(lane < 3 * nh, parts, 0.0).astype(BF16)
    qa_ref[0] = (jnp.dot(parts, place_q_ref[...], preferred_element_type=F32) + ones_q_ref[...]).astype(BF16)
    ka_ref[0] = (jnp.dot(parts, place_k_ref[...], preferred_element_type=F32) + ones_k_ref[...]).astype(BF16)


def fox_bias_operands(logf, *, tc=256):
    B, S, _ = logf.shape
    nh = N_HEADS
    tri = jnp.asarray(np.tril(np.ones((tc, tc), np.float32)))
    pq = np.zeros((LANES, nh * LANES), np.float32)
    pk = np.zeros((LANES, nh * LANES), np.float32)
    oq = np.zeros((1, nh * LANES), np.float32)
    ok = np.zeros((1, nh * LANES), np.float32)
    for h in range(nh):
        for part in range(3):
            pq[part * nh + h, h * LANES + part] = 1.0
            pk[part * nh + h, h * LANES + 3 + part] = -1.0
            oq[0, h * LANES + 3 + part] = 1.0
            ok[0, h * LANES + part] = 1.0
    const = lambda a: pl.BlockSpec(a.shape, lambda b, i: (0,) * a.ndim)
    pq, pk, oq, ok = jnp.asarray(pq, BF16), jnp.asarray(pk, BF16), jnp.asarray(oq), jnp.asarray(ok)
    out = jax.ShapeDtypeStruct((B, S, nh * LANES), BF16)
    return pl.pallas_call(
        _cumsum_aug_kernel,
        grid=(B, S // tc),
        in_specs=[pl.BlockSpec((1, tc, LANES), lambda b, i: (b, i, 0)),
                  const(tri), const(pq), const(pk), const(oq), const(ok)],
        out_specs=[pl.BlockSpec((1, tc, nh * LANES), lambda b, i: (b, i, 0))] * 2
        + [pl.BlockSpec((1, tc, LANES), lambda b, i: (b, i, 0))],
        out_shape=[out, out, jax.ShapeDtypeStruct((B, S, LANES), F32)],
        scratch_shapes=[pltpu.VMEM((1, LANES), F32)],
        compiler_params=_cparams(("parallel", "arbitrary")),
    )(logf, tri, pq, pk, oq, ok)


def _fox_attn_kernel(first_ref, q_ref, qa_ref, kv_ref, ka_ref, o_ref, *, tq, tk, heads_per_step):
    qi = pl.program_id(2)
    t0 = qi * tq
    n_full = t0 // tk
    j_first = first_ref[(pl.program_id(0) * pl.num_programs(1) + pl.program_id(1)) * pl.num_programs(2) + qi]
    n_diag = tq // tk
    lanes = [slice(hh * LANES, (hh + 1) * LANES) for hh in range(heads_per_step)]
    qs = [jnp.concatenate([q_ref[0, :, lsl], qa_ref[0, :, lsl]], axis=1) for lsl in lanes]

    def step(j, carry, masked):
        rows = pl.ds(pl.multiple_of(j * tk, tk), tk)
        klane = lax.broadcasted_iota(jnp.int32, (tk, LANES), 1)
        new = []
        for lsl, q, (m, acc) in zip(lanes, qs, carry):
            kv = kv_ref[0, rows, lsl]
            kk = jnp.concatenate([kv, ka_ref[0, rows, lsl]], axis=1)
            ones_v = jnp.where(klane < HEAD_DIM, 1.0, kv).astype(BF16)
            s = lax.dot_general(q, kk, _NT, preferred_element_type=F32)
            if masked:
                qpos = t0 + lax.broadcasted_iota(jnp.int32, (tq, 1), 0)
                kpos = j * tk + lax.broadcasted_iota(jnp.int32, (1, tk), 1)
                s = jnp.where(kpos <= qpos, s, NEG_INF)
            m_new = jnp.maximum(m, jnp.max(s, axis=1, keepdims=True))
            p = jnp.exp2(s - m_new).astype(BF16)
            acc = jnp.exp2(m - m_new) * acc + jnp.dot(p, ones_v, preferred_element_type=F32)
            new.append((m_new, acc))
        return tuple(new)

    init = (jnp.full((tq, 1), NEG_INF, F32), jnp.zeros((tq, LANES), F32))
    n_pairs = (n_full - j_first) // 2
    carry = lax.fori_loop(
        0, n_pairs, lambda i, c: step(j_first + 2 * i + 1, step(j_first + 2 * i, c, False), False),
        (init,) * heads_per_step)
    carry = lax.fori_loop(j_first + 2 * n_pairs, n_full, functools.partial(step, masked=False), carry)
    for d in range(n_diag):
        carry = step(n_full + d, carry, True)
    outs = [acc / acc[:, 0:1] for (_, acc) in carry]
    lane = lax.broadcasted_iota(jnp.int32, (tq, LANES), 1)
    blocks = []
    for pair in range(heads_per_step // 2):
        a, b = outs[2 * pair], outs[2 * pair + 1]
        blocks.append(jnp.where(lane < HEAD_DIM, pltpu.roll(a, HEAD_DIM, axis=1), b))
    o_ref[0] = jnp.concatenate(blocks, axis=1).astype(o_ref.dtype) if len(blocks) > 1 else blocks[0].astype(o_ref.dtype)


def _head_slots(w, n_heads, second=None):
    D = w.shape[0]
    a = w.reshape(D, n_heads, HEAD_DIM)
    b = jnp.zeros_like(a) if second is None else second.reshape(D, n_heads, HEAD_DIM)
    return jnp.concatenate([a, b], axis=-1).reshape(D, n_heads * LANES)


def fox_layer(h, gamma, w_in, f_bias, w_out, B, S):
    aw = ATTN_WIDTH
    wq = _head_slots(w_in[:, :aw] * (HEAD_DIM ** -0.5 * LOG2E), N_HEADS)
    wkv = _head_slots(w_in[:, aw:2 * aw], N_HEADS, w_in[:, 2 * aw:3 * aw])
    w_main = jnp.concatenate([wq, wkv], axis=1).astype(BF16)
    wf = jnp.pad(w_in[:, 3 * aw:], ((0, 0), (0, LANES - N_HEADS))).astype(BF16)
    bf = jnp.pad(f_bias.astype(F32), (0, LANES - N_HEADS))
    qkv = norm_matmul(h, gamma, w_main)
    logf = norm_matmul(h, gamma, wf, bias=bf, act="log_sigmoid", out_dtype=F32)
    qa, ka, c2 = fox_bias_operands(logf.reshape(B, S, LANES))
    qkv = qkv.reshape(B, S, 2 * N_HEADS * LANES)
    o = fox_attention(qkv, qa, ka, c2)
    return matmul_residual(o.reshape(B * S, aw), w_out.astype(BF16), h)


FOX_NEGLIGIBLE_LOG2 = 160.0


def _norm_maxima_kernel(x_ref, ind_ref, o_ref):
    x = x_ref[0].astype(F32)
    ss = jnp.dot((x * x).astype(BF16), ind_ref[...], preferred_element_type=F32)
    o_ref[0, 0] = jnp.broadcast_to(jnp.max(ss, axis=0, keepdims=True), o_ref.shape[2:])


def fox_norm_maxima(qkv, tile):
    B, S, W = qkv.shape
    n = S // tile
    col = np.arange(W)
    slot, lane = col // LANES, col % LANES
    used = (slot < N_HEADS) | (lane < HEAD_DIM)
    ind = jnp.asarray((slot[:, None] == np.arange(LANES)[None, :]) & used[:, None], BF16)
    rows = min(256, tile)
    ss = pl.pallas_call(
        _norm_maxima_kernel,
        grid=(B, S // rows),
        in_specs=[pl.BlockSpec((1, rows, W), lambda b, i: (b, i, 0)),
                  pl.BlockSpec(ind.shape, lambda b, i: (0, 0))],
        out_specs=pl.BlockSpec((1, 1, 8, LANES), lambda b, i: (b, i, 0, 0)),
        out_shape=jax.ShapeDtypeStruct((B, S // rows, 8, LANES), F32),
        compiler_params=_cparams(("parallel", "parallel")),
    )(qkv, ind)
    ss = jnp.max(ss[:, :, 0, :2 * N_HEADS].reshape(B, n, tile // rows, 2 * N_HEADS), axis=2)
    return jnp.sqrt(ss * 1.01)


def fox_first_chunk(qkv, c2, tile, heads_per_step):
    B, S, _ = qkv.shape
    n = S // tile
    norms = fox_norm_maxima(qkv, tile)
    qmax, kmax = norms[..., :N_HEADS], norms[..., N_HEADS:]
    c = c2[..., :N_HEADS].reshape(B, n, tile, N_HEADS)
    cmax, cmin = jnp.max(c, axis=2), jnp.min(c, axis=2)
    upper = qmax[:, :, None] * kmax[:, None, :] + cmax[:, :, None] - cmin[:, None, :]
    own = -(qmax * kmax)
    earlier = jnp.arange(n)[None, :, None, None] > jnp.arange(n)[None, None, :, None]
    skip = (upper < own[:, :, None] - FOX_NEGLIGIBLE_LOG2) & earlier
    first = jnp.sum(jnp.cumprod(skip.astype(jnp.int32), axis=2), axis=2)
    first = jnp.min(first.reshape(B, n, N_HEADS // heads_per_step, heads_per_step), axis=-1)
    return jnp.transpose(first, (0, 2, 1)).reshape(-1).astype(jnp.int32)


def fox_attention(qkv, qa, ka, c2, *, tq=1024, tk=1024, heads_per_step=2):
    B, S, _ = qkv.shape
    tq = tk = min(tq, S)
    hs = heads_per_step
    wq = hs * LANES
    n_qblk = N_HEADS // hs
    first = fox_first_chunk(qkv, c2, tk, hs)
    grid_spec = pltpu.PrefetchScalarGridSpec(
        num_scalar_prefetch=1,
        grid=(B, n_qblk, S // tq),
        in_specs=[pl.BlockSpec((1, tq, wq), lambda b, h, i, first: (b, i, h)),
                  pl.BlockSpec((1, tq, wq), lambda b, h, i, first: (b, i, h)),
                  pl.BlockSpec((1, S, wq), lambda b, h, i, first: (b, 0, n_qblk + h)),
                  pl.BlockSpec((1, S, wq), lambda b, h, i, first: (b, 0, h))],
        out_specs=pl.BlockSpec((1, tq, hs * HEAD_DIM), lambda b, h, i, first: (b, i, h)))
    return pl.pallas_call(
        functools.partial(_fox_attn_kernel, tq=tq, tk=tk, heads_per_step=hs),
        grid_spec=grid_spec,
        out_shape=jax.ShapeDtypeStruct((B, S, ATTN_WIDTH), BF16),
        compiler_params=_cparams(("parallel", "parallel", "arbitrary")),
    )(first, qkv, qa, qkv, ka)


def _rot_half_cols(w):
    D = w.shape[0]
    a = w.reshape(D, -1, HEAD_DIM)
    half = HEAD_DIM // 2
    return jnp.concatenate([-a[..., half:], a[..., :half]], axis=-1).reshape(w.shape)


def _rope_tables(S):
    half = HEAD_DIM // 2
    inv_freq = ROPE_THETA ** (-jnp.arange(half, dtype=F32) / half)
    ang = jnp.arange(S, dtype=F32)[:, None] * inv_freq[None, :]
    c, s = jnp.cos(ang), jnp.sin(ang)
    c2, s2 = jnp.concatenate([c, c], axis=1), jnp.concatenate([s, s], axis=1)
    cos = jnp.stack([jnp.concatenate([c2, jnp.ones_like(c2)], axis=1), jnp.concatenate([c2, c2], axis=1)])
    sin = jnp.stack([jnp.concatenate([s2, jnp.zeros_like(s2)], axis=1), jnp.concatenate([s2, s2], axis=1)])
    return cos, sin


def _compress_kernel(x_ref, pea_ref, peb_ref, wa_ref, wb_ref, w2_ref, o_ref, pa_ref, pb0_ref, *, n_rows):
    u = pl.program_id(1)
    x = x_ref[0].astype(F32)
    pa = jnp.dot((x + pea_ref[...]).astype(BF16), wa_ref[...], preferred_element_type=F32)
    pb = jnp.dot((x + peb_ref[...]).astype(BF16), wb_ref[...], preferred_element_type=F32)

    def emit(slab, hid):
        y = jnp.dot(_gelu_tanh(hid).astype(BF16), w2_ref[...], preferred_element_type=F32)
        o_ref[0, pl.ds(pl.multiple_of(slab * n_rows, n_rows), n_rows), :] = y

    @pl.when(u == 0)
    def _():
        pb0_ref[...] = pb

    @pl.when(u > 0)
    def _():
        emit(u - 1, pa_ref[...] + pb)

    @pl.when(u == 3)
    def _():
        emit(3, pa + pltpu.roll(pb0_ref[...], n_rows - 1, axis=0))

    pa_ref[...] = pa


def nsa_compress(src, pe, w1, w2):
    B, S, W = src.shape
    G = NSA_GROUPS
    n_rows = S // 64
    half = CMP_BLOCK // 2
    cw = half * W
    xv = src.reshape(B, n_rows, 4 * cw)
    pe_flat = jnp.transpose(pe, (1, 0, 2)).reshape(CMP_BLOCK, W).astype(F32)
    pea, peb = pe_flat[:half].reshape(1, cw), pe_flat[half:].reshape(1, cw)
    eye = jnp.eye(G, dtype=F32)
    wfull = jnp.einsum('gldh,gk->lkdgh', w1.astype(F32), eye).reshape(CMP_BLOCK, W, G * CMP_HIDDEN)
    wa = wfull[:half].reshape(cw, G * CMP_HIDDEN).astype(BF16)
    wb = wfull[half:].reshape(cw, G * CMP_HIDDEN).astype(BF16)
    w2bd = jnp.einsum('ghd,gk->ghkd', w2.astype(F32), eye).reshape(G * CMP_HIDDEN, W).astype(BF16)
    const = lambda a: pl.BlockSpec(a.shape, lambda b, u: (0,) * a.ndim)
    return pl.pallas_call(
        functools.partial(_compress_kernel, n_rows=n_rows),
        grid=(B, 4),
        in_specs=[pl.BlockSpec((1, n_rows, cw), lambda b, u: (b, 0, u)),
                  const(pea), const(peb), const(wa), const(wb), const(w2bd)],
        out_specs=pl.BlockSpec((1, 4 * n_rows, W), lambda b, u: (b, 0, 0)),
        out_shape=jax.ShapeDtypeStruct((B, 4 * n_rows, W), F32),
        scratch_shapes=[pltpu.VMEM((n_rows, G * CMP_HIDDEN), F32), pltpu.VMEM((n_rows, G * CMP_HIDDEN), F32)],
        compiler_params=_cparams(("parallel", "arbitrary")),
    )(xv, pea, peb, wa, wb, w2bd)


def _nsa_attn_kernel(q_ref, kvs_ref, kvw_ref, kvc_ref, kvct_ref, gate_ref, bmat_ref, pmat_ref, o_ref,
                     *, tq, ts, tk, seq):
    R = NSA_Q_PER_GROUP
    n_slc = seq // SLC_BLOCK
    n_sel = min(SLC_TOPK, n_slc)
    assert ts & (ts - 1) == 0 and n_slc & (n_slc - 1) == 0 and tk % tq == 0 and tq % ts == 0
    log_slc = n_slc.bit_length() - 1
    qi = pl.program_id(2)
    t0 = qi * tq

    local = [_nsa_local_branches(q_ref, kvw_ref, kvc_ref, kvct_ref, t0 + i * ts, i * ts, ts, n_slc, n_sel, log_slc)
             for i in range(tq // ts)]
    o_c = jnp.concatenate([o[0][r * ts:(r + 1) * ts] for r in range(R) for o in local], axis=0)
    o_w = jnp.concatenate([o[1][r * ts:(r + 1) * ts] for r in range(R) for o in local], axis=0)
    sel_bias = jnp.concatenate([o[2] for o in local], axis=0)
    qs = jnp.concatenate([q_ref[0, :, r * LANES:(r + 1) * LANES] for r in range(R)], axis=0)
    _nsa_selected_and_combine(qs, o_c, o_w, sel_bias, kvs_ref, gate_ref, bmat_ref, pmat_ref, o_ref,
                              t0=t0, tq=tq, tk=tk, seq=seq)


def _nsa_local_branches(q_ref, kvw_ref, kvc_ref, kvct_ref, t0, row0, tq, n_slc, n_sel, log_slc):
    R = NSA_Q_PER_GROUP
    M = R * tq
    qs = jnp.concatenate([q_ref[0, row0:row0 + tq, r * LANES:(r + 1) * LANES] for r in range(R)], axis=0)

    wlen = WINDOW + tq
    start = jnp.maximum(t0 - WINDOW, 0)
    kvw = kvw_ref[0, pl.ds(pl.multiple_of(start, tq), wlen), :]
    s_w = lax.dot_general(qs, kvw, _NT, preferred_element_type=F32)
    qpos = t0 + (lax.broadcasted_iota(jnp.int32, (M, 1), 0) & (tq - 1))
    kpos = start + lax.broadcasted_iota(jnp.int32, (1, wlen), 1)
    s_w = jnp.where(kpos <= qpos, jnp.where(kpos > qpos - WINDOW, s_w, NEG_INF), NEG_INF)
    p_w = jnp.exp2(s_w - jnp.max(s_w, axis=1, keepdims=True)).astype(BF16)
    wlane = lax.broadcasted_iota(jnp.int32, (wlen, LANES), 1)
    acc_w = jnp.dot(p_w, jnp.where(wlane < HEAD_DIM, 1.0, kvw).astype(BF16), preferred_element_type=F32)
    o_w = acc_w / acc_w[:, 0:1]

    n_groups, gs = _nsa_row_groups(n_slc)
    o_c, sel_bias = _nsa_compressed_and_select(qs, kvc_ref, kvct_ref, t0, tq, n_slc, n_sel, gs, n_groups)
    return o_c, o_w, sel_bias


def _nsa_row_groups(n_slc):
    n_groups = max(1, min(4, n_slc // 8))
    assert n_slc % n_groups == 0 and (n_slc // n_groups) % 8 == 0
    return n_groups, n_slc // n_groups


def _nsa_compressed_and_select(qs, kvc_ref, kvct_ref, t0, tq, n_slc, n_sel, gs, n_groups):
    R = NSA_Q_PER_GROUP
    M = R * tq
    n_rows, n_j = n_groups * 4 * gs, n_groups * gs
    log_gs = gs.bit_length() - 1
    sT = lax.dot_general(kvc_ref[0, 0, 0:n_rows, :], qs, _NT, preferred_element_type=F32)
    rowc = lax.broadcasted_iota(jnp.int32, (n_rows, 1), 0)
    blk_j = ((rowc >> (log_gs + 2)) << log_gs) + (rowc & (gs - 1))
    blk_u = (rowc >> log_gs) & 3
    cmp_end = blk_j * SLC_BLOCK + blk_u * CMP_STRIDE + (CMP_BLOCK - 1)
    tcol = t0 + (lax.broadcasted_iota(jnp.int32, (1, M), 1) & (tq - 1))
    sm = jnp.where(cmp_end <= tcol, sT, NEG_INF)
    mx = jnp.max(sm, axis=0, keepdims=True)
    e = jnp.exp2(sm - mx)
    inv = jnp.where(mx > 0.5 * NEG_INF, 1.0 / jnp.sum(e, axis=0, keepdims=True), 0.0)
    pT = e * inv
    o_c = jnp.dot(kvct_ref[0, 0, :, 0:n_rows], pT.astype(BF16), preferred_element_type=F32).T

    psum = pT[:, 0:tq]
    for r in range(1, R):
        psum = psum + pT[:, r * tq:(r + 1) * tq]
    p0, p1, p2, p3 = (jnp.concatenate([psum[(4 * g + u) * gs:(4 * g + u + 1) * gs, :] for g in range(n_groups)],
                                      axis=0) for u in range(4))
    jrow = lax.broadcasted_iota(jnp.int32, (n_j, tq), 0)
    p3_prev = jnp.where(jrow == 0, 0.0, pltpu.roll(p3, 1, axis=0))
    imp = p0 + p1 + p2 + 0.5 * p3 + 0.5 * p3_prev
    cur = (t0 + lax.broadcasted_iota(jnp.int32, (n_j, tq), 1)) >> (SLC_BLOCK.bit_length() - 1)
    forced = (jrow == 0) | (jrow == cur) | (jrow == cur - 1)
    vals = jnp.where(forced, -jnp.inf, jnp.where(jrow <= cur, imp, -jnp.inf))
    sel0 = jnp.where(forced, 1.0, 0.0)
    n_free = n_sel - 3

    picked = -2.0 ** 100

    def pick(exact_ties):
        v = vals
        for _ in range(n_free):
            m = jnp.max(v, axis=0, keepdims=True)
            if exact_ties:
                hit = jrow == jnp.min(jnp.where(v == m, jrow, n_slc), axis=0, keepdims=True)
            else:
                hit = v == jnp.where(m < 0.0, jnp.nan, m)
            v = jnp.where(hit, picked, v)
        return jnp.where(v == picked, 1.0, sel0)

    sel_fast = pick(False)
    n_cand = jnp.sum(jnp.where(vals > -jnp.inf, 1.0, 0.0), axis=0, keepdims=True)
    n_picked = jnp.sum(sel_fast - sel0, axis=0, keepdims=True)
    unique = jnp.min(jnp.where(n_picked == jnp.minimum(n_cand, float(n_free)), 1.0, 0.0)) > 0.5
    sel = lax.cond(unique, lambda: sel_fast, lambda: pick(True))
    if n_j < n_slc:
        sel = jnp.concatenate([sel, jnp.zeros((n_slc - n_j, tq), F32)], axis=0)
    sel_bias = ((sel.T - 1.0) * (-NEG_INF)).astype(BF16)
    return o_c, sel_bias


def _nsa_selected_and_combine(qs, o_c, o_w, sel_bias, kvs_ref, gate_ref, bmat_ref, pmat_ref, o_ref,
                              *, t0, tq, tk, seq):
    R = NSA_Q_PER_GROUP
    n_slc = seq // SLC_BLOCK
    blocks_per_chunk = tk // SLC_BLOCK
    n_parts = 2
    hp = R // n_parts
    mp = hp * tq
    qparts = [qs[i * mp:(i + 1) * mp] for i in range(n_parts)]
    qrow = t0 + (lax.broadcasted_iota(jnp.int32, (mp, 1), 0) & (tq - 1))
    kcol = lax.broadcasted_iota(jnp.int32, (1, tk), 1)
    n_chunks = seq // tk
    j_last = t0 // tk

    klane = lax.broadcasted_iota(jnp.int32, (tk, LANES), 1)

    def slc_step(j, carry, masked):
        kv = kvs_ref[0, pl.ds(pl.multiple_of(j * tk, tk), tk), :]
        k_sel = jnp.where(klane < HEAD_DIM, kv, bmat_ref[...]).astype(BF16)
        ones_v = jnp.where(klane < HEAD_DIM, 1.0, kv).astype(BF16)
        off = pl.multiple_of(blocks_per_chunk * (n_chunks - 1 - j), blocks_per_chunk)
        place = pmat_ref[pl.ds(off, n_slc), :].astype(BF16)
        q_bias = jnp.dot(sel_bias, place, preferred_element_type=F32).astype(BF16)
        q_bias = jnp.concatenate([q_bias] * hp, axis=0)
        new = []
        for qp, (m, acc) in zip(qparts, carry):
            s = lax.dot_general(qp + q_bias, k_sel, _NT, preferred_element_type=F32)
            if masked:
                s = jnp.where(j * tk + kcol <= qrow, s, NEG_INF)
            m_new = jnp.maximum(m, jnp.max(s, axis=1, keepdims=True))
            p = jnp.exp2(s - m_new).astype(BF16)
            acc = jnp.exp2(m - m_new) * acc + jnp.dot(p, ones_v, preferred_element_type=F32)
            new.append((m_new, acc))
        return tuple(new)

    init = (jnp.full((mp, 1), NEG_INF, F32), jnp.zeros((mp, LANES), F32))
    n_pairs = j_last // 2
    carry = lax.fori_loop(0, n_pairs, lambda i, c: slc_step(2 * i + 1, slc_step(2 * i, c, False), False),
                          (init,) * n_parts)
    carry = lax.fori_loop(2 * n_pairs, j_last, functools.partial(slc_step, masked=False), carry)
    carry = slc_step(j_last, carry, True)
    o_s = jnp.concatenate([acc / acc[:, 0:1] for (_, acc) in carry], axis=0)

    gates = gate_ref[0]
    lane = lax.broadcasted_iota(jnp.int32, (tq, LANES), 1)
    comb = []
    for r in range(R):
        rs = slice(r * tq, (r + 1) * tq)
        comb.append(gates[:, r:r + 1] * o_c[rs] + gates[:, R + r:R + r + 1] * o_s[rs]
                    + gates[:, 2 * R + r:2 * R + r + 1] * o_w[rs])
    out = [jnp.where(lane < HEAD_DIM, pltpu.roll(comb[2 * i], HEAD_DIM, axis=1), comb[2 * i + 1])
           for i in range(R // 2)]
    o_ref[0] = jnp.concatenate(out, axis=1).astype(o_ref.dtype)


def nsa_attention(qkv, kvc, kvct, gates, *, tq=512, ts=256, tk=1024):
    B, S, _ = qkv.shape
    G, R = NSA_GROUPS, NSA_Q_PER_GROUP
    tk = min(tk, S)
    n_slc = S // SLC_BLOCK
    n_cmp = kvc.shape[2]
    bpc = tk // SLC_BLOCK
    assert bpc <= LANES - HEAD_DIM
    off = bpc * (S // tk - 1)
    lane = np.arange(LANES)[None, :]
    bmat = jnp.asarray(lane - HEAD_DIM == np.arange(tk)[:, None] // SLC_BLOCK, BF16)
    pmat = jnp.asarray((np.arange(n_slc + off)[:, None] - off == lane - HEAD_DIM) & (lane >= HEAD_DIM)
                       & (lane < HEAD_DIM + bpc), F32)
    slc_blk0, win_blk0 = N_HEADS, N_HEADS + G
    return pl.pallas_call(
        functools.partial(_nsa_attn_kernel, tq=tq, ts=ts, tk=tk, seq=S),
        grid=(B, G, S // tq),
        in_specs=[pl.BlockSpec((1, tq, R * LANES), lambda b, g, i: (b, i, g)),
                  pl.BlockSpec((1, S, LANES), lambda b, g, i: (b, 0, slc_blk0 + g)),
                  pl.BlockSpec((1, S, LANES), lambda b, g, i: (b, 0, win_blk0 + g)),
                  pl.BlockSpec((1, 1, n_cmp, LANES), lambda b, g, i: (b, g, 0, 0)),
                  pl.BlockSpec((1, 1, LANES, n_cmp), lambda b, g, i: (b, g, 0, 0)),
                  pl.BlockSpec((1, tq, LANES), lambda b, g, i: (b, i, g)),
                  pl.BlockSpec(bmat.shape, lambda b, g, i: (0, 0)),
                  pl.BlockSpec(pmat.shape, lambda b, g, i: (0, 0))],
        out_specs=pl.BlockSpec((1, tq, R * HEAD_DIM), lambda b, g, i: (b, i, g)),
        out_shape=jax.ShapeDtypeStruct((B, S, ATTN_WIDTH), BF16),
        compiler_params=_cparams(("parallel", "parallel", "arbitrary")),
    )(qkv, qkv, qkv, kvc, kvct, gates, bmat, pmat)


def nsa_layer(h, gamma, w_in, pe_k, w1_k, w2_k, pe_v, w1_v, w2_v, w_out, B, S):
    G, R, hd, aw = NSA_GROUPS, NSA_Q_PER_GROUP, HEAD_DIM, ATTN_WIDTH
    kvd = G * hd
    sec = lambda i: w_in[:, aw + i * kvd: aw + (i + 1) * kvd]
    wq = w_in[:, :aw] * (hd ** -0.5 * LOG2E)
    wa = jnp.concatenate([_head_slots(wq, N_HEADS), _head_slots(sec(2), G, sec(3)),
                          _head_slots(sec(4), G, sec(5))], axis=1).astype(BF16)
    wb = jnp.concatenate([_head_slots(_rot_half_cols(wq), N_HEADS), _head_slots(_rot_half_cols(sec(2)), G),
                          _head_slots(_rot_half_cols(sec(4)), G)], axis=1).astype(BF16)
    cos, sin = _rope_tables(S)
    qkv = norm_matmul(h, gamma, wa, wb=wb, cos=cos, sin=sin)
    kc_src = norm_matmul(h, gamma, sec(0).astype(BF16), wb=_rot_half_cols(sec(0)).astype(BF16),
                         cos=cos, sin=sin, table_of_tile=lambda j: 1)
    vc_src = norm_matmul(h, gamma, sec(1).astype(BF16))
    wg = w_in[:, aw + 6 * kvd:].reshape(-1, 3, G, R)
    wg = jnp.transpose(wg, (0, 2, 1, 3)).reshape(-1, G, 3 * R)
    wg = jnp.pad(wg, ((0, 0), (0, 0), (0, LANES - 3 * R))).reshape(-1, G * LANES).astype(BF16)
    gates = norm_matmul(h, gamma, wg, act="sigmoid", out_dtype=F32)
    kc = nsa_compress(kc_src.reshape(B, S, kvd), pe_k, w1_k, w2_k)
    vc = nsa_compress(vc_src.reshape(B, S, kvd), pe_v, w1_v, w2_v)
    n_cmp = kc.shape[1]
    kvc = jnp.concatenate([kc.reshape(B, n_cmp, G, hd), vc.reshape(B, n_cmp, G, hd)], axis=-1)
    n_groups, gs = _nsa_row_groups(S // SLC_BLOCK)
    kvc = kvc.reshape(B, 4, n_groups, gs, G, 2 * hd)
    kvc = jnp.transpose(kvc, (0, 4, 2, 1, 3, 5)).reshape(B, G, n_cmp, 2 * hd).astype(BF16)
    kvct = jnp.swapaxes(kvc, 2, 3)
    o = nsa_attention(qkv.reshape(B, S, -1), kvc, kvct, gates.reshape(B, S, G * LANES))
    return matmul_residual(o.reshape(B * S, aw), w_out.astype(BF16), h)


def kernel(x, l0_attn_norm, l0_w_in, l0_cmp_pe_k, l0_cmp_w1_k, l0_cmp_w2_k, l0_cmp_pe_v, l0_cmp_w1_v,
           l0_cmp_w2_v, l0_w_out, l0_ffn_norm, l0_peer_wq, l0_peer_keys, l0_peer_u, l0_peer_v,
           l1_attn_norm, l1_w_in, l1_f_bias, l1_w_out, l1_ffn_norm, l1_peer_wq, l1_peer_keys, l1_peer_u,
           l1_peer_v, final_norm):
    B, S, D = x.shape
    h = x.reshape(B * S, D)
    h = nsa_layer(h, l0_attn_norm, l0_w_in, l0_cmp_pe_k, l0_cmp_w1_k, l0_cmp_w2_k, l0_cmp_pe_v, l0_cmp_w1_v,
                  l0_cmp_w2_v, l0_w_out, B, S)
    h = peer_layer(h, l0_ffn_norm, l0_peer_wq, l0_peer_keys, l0_peer_u, l0_peer_v)
    h = fox_layer(h, l1_attn_norm, l1_w_in, l1_f_bias, l1_w_out, B, S)
    h = peer_layer(h, l1_ffn_norm, l1_peer_wq, l1_peer_keys, l1_peer_u, l1_peer_v)
    return rmsnorm(h, final_norm).reshape(B, S, D)
```

```python
import functools

import numpy as np
import jax
import jax.numpy as jnp
from jax import lax
from jax.experimental import pallas as pl
from jax.experimental.pallas import tpu as pltpu

F32 = jnp.float32
BF16 = jnp.bfloat16

D_MODEL = 1024
N_HEADS = 16
HEAD_DIM = 64
ATTN_WIDTH = N_HEADS * HEAD_DIM
NSA_GROUPS = 4
NSA_Q_PER_GROUP = N_HEADS // NSA_GROUPS
CMP_BLOCK = 32
CMP_STRIDE = 16
CMP_HIDDEN = 2 * HEAD_DIM
SLC_BLOCK = 64
SLC_TOPK = 16
WINDOW = 512
FORCE_SCORE = 1.0e4
ROPE_THETA = 10000.0
PEER_HEADS = 8
PEER_N_KEYS = 128
PEER_TOPK = 16
PEER_HALF_DIM = 128
RMS_EPS = 1e-6
NEG_INF = -1e30
LOG2E = 1.4426950408889634

LANES = 128
VMEM_LIMIT_BYTES = 56 * 1024 * 1024

_NT = (((1,), (1,)), ((), ()))


def _cparams(sem, vmem=VMEM_LIMIT_BYTES, flags=None):
    return pltpu.CompilerParams(dimension_semantics=sem, vmem_limit_bytes=vmem, flags=flags)


def _gelu_tanh(x):
    return 0.5 * x * (1.0 + jnp.tanh(0.7978845608028654 * (x + 0.044715 * (x * x * x))))


def _gelu_sigmoid(x):
    c = -2.0 * 0.7978845608028654 * LOG2E
    t = x * (c + (c * 0.044715) * (x * x))
    return x / (1.0 + jnp.exp2(t))


def _rms_rows(x, g):
    ms = jnp.mean(x * x, axis=-1, keepdims=True)
    return x * lax.rsqrt(ms + RMS_EPS) * g


def _norm_mm_kernel(*refs, act, has_bias, rope, emit_xn):
    it = iter(refs)
    x_ref, g_ref = next(it), next(it)
    wa_ref = next(it)
    wb_ref = next(it) if rope else None
    cos_ref = next(it) if rope else None
    sin_ref = next(it) if rope else None
    b_ref = next(it) if has_bias else None
    o_ref = next(it)
    xo_ref = next(it) if emit_xn else None
    xn_ref = next(it)

    @pl.when(pl.program_id(1) == 0)
    def _():
        xn = _rms_rows(x_ref[...], g_ref[...])
        xn_ref[...] = xn.astype(BF16)
        if emit_xn:
            xo_ref[...] = xn.T.astype(BF16)

    xn = xn_ref[...]
    y = jnp.dot(xn, wa_ref[...], preferred_element_type=F32)
    if rope:
        yb = jnp.dot(xn, wb_ref[...], preferred_element_type=F32)
        cos, sin = cos_ref[...], sin_ref[...]
        packed_b = 2 * yb.shape[1] == y.shape[1]
        for s in range(y.shape[1] // LANES):
            sl = slice(s * LANES, (s + 1) * LANES)
            if packed_b:
                b = yb[:, (s // 2) * LANES:(s // 2 + 1) * LANES]
                b = pltpu.roll(b, HEAD_DIM, axis=1) if s % 2 else b
            else:
                b = yb[:, sl]
            o_ref[:, sl] = (y[:, sl] * cos + b * sin).astype(o_ref.dtype)
        return
    if has_bias:
        y = y + b_ref[...]
    if act == "sigmoid":
        y = jax.nn.sigmoid(y)
    elif act == "log_sigmoid":
        y = jax.nn.log_sigmoid(y)
    o_ref[...] = y.astype(o_ref.dtype)


def norm_matmul(x, gamma, wa, *, wb=None, cos=None, sin=None, table_of_tile=None, bias=None,
                act=None, out_dtype=BF16, emit_xn=False, tm=1024, tn=1024):
    T, D = x.shape
    N = wa.shape[1]
    tm, tn = min(tm, T), min(tn, N)
    assert T % tm == 0 and N % tn == 0 and tn % LANES == 0
    rope = wb is not None
    in_specs = [pl.BlockSpec((tm, D), lambda i, j: (i, 0)),
                pl.BlockSpec((1, D), lambda i, j: (0, 0)),
                pl.BlockSpec((D, tn), lambda i, j: (0, j))]
    args = [x, gamma.reshape(1, D).astype(F32), wa]
    if rope:
        S = cos.shape[1]
        assert S % tm == 0
        n_pos = S // tm
        tmap = table_of_tile if table_of_tile is not None else (lambda j: 0)
        assert wb.shape[1] in (N, N // 2)
        in_specs += [pl.BlockSpec((D, tn * wb.shape[1] // N), lambda i, j: (0, j)),
                     pl.BlockSpec((None, tm, LANES), lambda i, j: (tmap(j), i % n_pos, 0)),
                     pl.BlockSpec((None, tm, LANES), lambda i, j: (tmap(j), i % n_pos, 0))]
        args += [wb, cos, sin]
    if bias is not None:
        in_specs.append(pl.BlockSpec((1, tn), lambda i, j: (0, j)))
        args.append(bias.reshape(1, N).astype(F32))
    out_shape = [jax.ShapeDtypeStruct((T, N), out_dtype)]
    out_specs = [pl.BlockSpec((tm, tn), lambda i, j: (i, j))]
    if emit_xn:
        out_shape.append(jax.ShapeDtypeStruct((D, T), BF16))
        out_specs.append(pl.BlockSpec((D, tm), lambda i, j: (0, i)))
    res = pl.pallas_call(
        functools.partial(_norm_mm_kernel, act=act, has_bias=bias is not None, rope=rope, emit_xn=emit_xn),
        grid=(T // tm, N // tn),
        in_specs=in_specs,
        out_specs=out_specs,
        out_shape=out_shape,
        scratch_shapes=[pltpu.VMEM((tm, D), BF16)],
        compiler_params=_cparams(("parallel", "arbitrary")),
    )(*args)
    return res if emit_xn else res[0]


def _mm_res_kernel(a_ref, w_ref, r_ref, o_ref):
    o_ref[...] = r_ref[...] + jnp.dot(a_ref[...], w_ref[...], preferred_element_type=F32)


def matmul_residual(a, w, res, *, tm=1024, tn=1024):
    T, K = a.shape
    N = w.shape[1]
    tm, tn = min(tm, T), min(tn, N)
    assert T % tm == 0 and N % tn == 0
    return pl.pallas_call(
        _mm_res_kernel,
        grid=(T // tm, N // tn),
        in_specs=[pl.BlockSpec((tm, K), lambda i, j: (i, 0)),
                  pl.BlockSpec((K, tn), lambda i, j: (0, j)),
                  pl.BlockSpec((tm, tn), lambda i, j: (i, j))],
        out_specs=pl.BlockSpec((tm, tn), lambda i, j: (i, j)),
        out_shape=jax.ShapeDtypeStruct((T, N), F32),
        compiler_params=_cparams(("parallel", "arbitrary")),
    )(a, w, res)


def _rmsnorm_kernel(x_ref, g_ref, o_ref):
    o_ref[...] = _rms_rows(x_ref[...], g_ref[...])


def rmsnorm(x, gamma, *, tm=1024):
    T, D = x.shape
    tm = min(tm, T)
    return pl.pallas_call(
        _rmsnorm_kernel,
        grid=(T // tm,),
        in_specs=[pl.BlockSpec((tm, D), lambda i: (i, 0)), pl.BlockSpec((1, D), lambda i: (0, 0))],
        out_specs=pl.BlockSpec((tm, D), lambda i: (i, 0)),
        out_shape=jax.ShapeDtypeStruct((T, D), F32),
        compiler_params=_cparams(("parallel",)),
    )(x, gamma.reshape(1, D).astype(F32))


def _peer_cand_tables(tn):
    fidx, vmask = [], []
    for k2 in range(16):
        fidx.append(k2); vmask.append(0.0)
    for k1 in range(1, 8):
        lim = PEER_TOPK // (k1 + 1)
        for k2 in range(8):
            fidx.append(k1 * 16 + k2); vmask.append(0.0 if k2 < lim else -np.inf)
    for k1 in range(8, 16):
        fidx.append(k1 * 16); vmask.append(0.0)
    fidx = np.broadcast_to(np.asarray(fidx, np.int32)[:, None], (80, tn))
    vmask = np.broadcast_to(np.asarray(vmask, np.float32)[:, None], (80, tn))
    return jnp.asarray(fidx), jnp.asarray(vmask)


def _top16_rows(s, exact_ties):
    n, tn = s.shape
    rows = lax.broadcasted_iota(jnp.int32, (n, tn), 0)
    rows16 = lax.broadcasted_iota(jnp.int32, (PEER_TOPK, tn), 0)
    tops = jnp.zeros((PEER_TOPK, tn), F32)
    unit = 2.0 ** 122
    v = s
    for k in range(PEER_TOPK):
        m = jnp.max(v, axis=0, keepdims=True)
        if exact_ties:
            hit = rows == jnp.min(jnp.where(v == m, rows, n), axis=0, keepdims=True)
        else:
            hit = v == m
        v = jnp.where(hit, -(32.0 + k) * unit, v)
        tops = jnp.where(rows16 == k, m, tops)
    was_picked = v <= -32.0 * unit
    rank = jnp.where(was_picked, v * (-1.0 / unit) - 32.0, float(PEER_TOPK))
    n_picked = jnp.sum(jnp.where(was_picked, 1.0, 0.0), axis=0, keepdims=True)
    return tops, rank, n_picked


def _peer_select_head(q_ref, keys_ref, fidx, vmask, exact_ties):
    tops, ranks, es, picked = [], [], [], []
    for p in range(2):
        q = q_ref[:, p * PEER_HALF_DIM:(p + 1) * PEER_HALF_DIM]
        s = lax.dot_general(keys_ref[p], q, _NT, preferred_element_type=F32)
        t, r, n_picked = _top16_rows(s, exact_ties)
        tops.append(t); ranks.append(r)
        es.append(jnp.exp(s - t[0:1, :]))
        picked.append(n_picked)
    ts1, ts2 = tops
    pieces = [ts1[0:1, :] + ts2]
    for k1 in range(1, 8):
        pieces.append(ts1[k1:k1 + 1, :] + ts2[0:8, :])
    pieces.append(ts1[8:16, :] + ts2[0:1, :])
    cand0 = jnp.concatenate(pieces, axis=0) + vmask
    cand = cand0
    for _ in range(PEER_TOPK):
        m = jnp.max(cand, axis=0, keepdims=True)
        if exact_ties:
            hit = fidx == jnp.min(jnp.where(cand == m, fidx, 4096), axis=0, keepdims=True)
        else:
            hit = cand == m
        cand = jnp.where(hit, -jnp.inf, cand)
    taken = jnp.logical_and(cand == -jnp.inf, vmask == 0.0)
    takenf = taken.astype(F32)
    picked.append(jnp.sum(takenf, axis=0, keepdims=True))
    unique = jnp.min(jnp.where((picked[0] == PEER_TOPK) & (picked[1] == PEER_TOPK) & (picked[2] == PEER_TOPK),
                               1.0, 0.0)) > 0.5
    best = ts1[0:1, :] + ts2[0:1, :]
    z = jnp.sum(jnp.where(taken, jnp.exp(cand0 - best), 0.0), axis=0, keepdims=True)
    counts = [jnp.sum(takenf[0:16, :], axis=0, keepdims=True)]
    for k1 in range(1, 8):
        counts.append(jnp.sum(takenf[16 + 8 * (k1 - 1):16 + 8 * k1, :], axis=0, keepdims=True))
    tail = takenf[72:80, :]
    cnt = jnp.zeros_like(ranks[0])
    for k1 in range(PEER_TOPK):
        nk = counts[k1] if k1 < 8 else tail[k1 - 8:k1 - 7, :]
        cnt = jnp.where(ranks[0] == float(k1), nk, cnt)
    return (cnt, ranks[1], es[0], es[1] / z), unique


def _peer_select_kernel(q_ref, keys_ref, fidx_ref, vmask_ref, cnt_ref, rank2_ref, e1_ref, e2_ref):
    fidx = fidx_ref[...]
    vmask = vmask_ref[...]

    def store(vals):
        for ref, val in zip((cnt_ref, rank2_ref, e1_ref, e2_ref), vals):
            ref[...] = val.astype(ref.dtype)

    vals, unique = _peer_select_head(q_ref, keys_ref, fidx, vmask, exact_ties=False)
    store(vals)

    @pl.when(jnp.logical_not(unique))
    def _():
        store(_peer_select_head(q_ref, keys_ref, fidx, vmask, exact_ties=True)[0])


def peer_select(q, keys, *, tn=512):
    T = q.shape[0]
    tn = min(tn, T)
    fidx, vmask = _peer_cand_tables(tn)
    rows = PEER_HEADS * PEER_N_KEYS
    ospec = pl.BlockSpec((PEER_N_KEYS, tn), lambda i, h: (h, i))
    return pl.pallas_call(
        _peer_select_kernel,
        grid=(T // tn, PEER_HEADS),
        in_specs=[pl.BlockSpec((tn, 2 * PEER_HALF_DIM), lambda i, h: (i, h)),
                  pl.BlockSpec((2, PEER_N_KEYS, PEER_HALF_DIM), lambda i, h: (h, 0, 0)),
                  pl.BlockSpec((80, tn), lambda i, h: (0, 0)),
                  pl.BlockSpec((80, tn), lambda i, h: (0, 0))],
        out_specs=[ospec] * 4,
        out_shape=[jax.ShapeDtypeStruct((rows, T), dt) for dt in (F32, BF16, F32, BF16)],
        compiler_params=_cparams(("parallel", "parallel")),
    )(q, keys, fidx, vmask)


def _peer_dense_kernel(xn_ref, u_ref, vt_ref, cnt_ref, rank2_ref, e1_ref, e2_ref, res_ref, o_ref,
                       acc_ref, g0_ref, g1_ref, *, c_per_step):
    j = pl.program_id(1)
    n_tiles = pl.num_programs(1) - 1

    @pl.when(j == 0)
    def _():
        acc_ref[...] = jnp.zeros_like(acc_ref)
        g1_ref[...] = jnp.zeros_like(g1_ref)

    @pl.when((j % 2 == 0) & (j < n_tiles))
    def _():
        _peer_dense_step(xn_ref, u_ref, vt_ref, cnt_ref, rank2_ref, e1_ref, e2_ref, acc_ref,
                         g1_ref, g0_ref, j, c_per_step)

    @pl.when(j % 2 == 1)
    def _():
        _peer_dense_step(xn_ref, u_ref, vt_ref, cnt_ref, rank2_ref, e1_ref, e2_ref, acc_ref,
                         g0_ref, g1_ref, j, c_per_step)

    @pl.when(j == n_tiles)
    def _():
        _peer_dense_step(xn_ref, u_ref, vt_ref, cnt_ref, rank2_ref, e1_ref, e2_ref, acc_ref,
                         g1_ref, None, j, c_per_step)
        o_ref[...] = res_ref[...] + acc_ref[...].T


def _peer_dense_step(xn_ref, u_ref, vt_ref, cnt_ref, rank2_ref, e1_ref, e2_ref, acc_ref, g_ref, g_next_ref,
                     j, c_per_step):
    tn = xn_ref.shape[1]
    bf16_rows = 16
    reps = PEER_N_KEYS // bf16_rows

    def row_tile(ref, row):
        r16 = jnp.broadcast_to(ref[pl.ds(row, 1), :], (bf16_rows, tn)).astype(BF16)
        return jnp.concatenate([r16] * reps, axis=0)

    c0 = jnp.maximum(j - 1, 0) * c_per_step
    up_rows = 2 * PEER_N_KEYS
    blocks = []
    for cc in range(c_per_step):
        if g_next_ref is not None and (cc * PEER_N_KEYS) % up_rows == 0:
            rs = slice(cc * PEER_N_KEYS, cc * PEER_N_KEYS + up_rows)
            hT = jnp.dot(u_ref[rs, :], xn_ref[...], preferred_element_type=F32)
            g_next_ref[rs, :] = _gelu_sigmoid(hT.astype(BF16))
        c = c0 + cc
        w = None
        for h in range(PEER_HEADS):
            row = h * PEER_N_KEYS + c
            n_row = row_tile(cnt_ref, row)
            e1_row = row_tile(e1_ref, row)
            sl = slice(h * PEER_N_KEYS, (h + 1) * PEER_N_KEYS)
            term = jnp.where(rank2_ref[sl, :] < n_row, e2_ref[sl, :], 0.0) * e1_row
            w = term if w is None else w + term
        blocks.append(w * g_ref[cc * PEER_N_KEYS:(cc + 1) * PEER_N_KEYS, :])
    aT = jnp.concatenate(blocks, axis=0)
    acc_ref[...] += jnp.dot(vt_ref[...], aT, preferred_element_type=F32)


def peer_dense(xn, u, vt, cnt, rank2, e1, e2, res, *, tn=512, te=2048):
    D, T = xn.shape
    E = u.shape[0]
    tn = min(tn, T)
    rows = PEER_HEADS * PEER_N_KEYS
    sel_spec = pl.BlockSpec((rows, tn), lambda i, j: (0, i))
    n_tiles = E // te
    assert n_tiles % 2 == 0
    return pl.pallas_call(
        functools.partial(_peer_dense_kernel, c_per_step=te // PEER_N_KEYS),
        grid=(T // tn, n_tiles + 1),
        in_specs=[pl.BlockSpec((D, tn), lambda i, j: (0, i)),
                  pl.BlockSpec((te, D), lambda i, j: (jnp.minimum(j, n_tiles - 1), 0)),
                  pl.BlockSpec((D, te), lambda i, j: (0, jnp.maximum(j - 1, 0))),
                  sel_spec, sel_spec, sel_spec, sel_spec,
                  pl.BlockSpec((tn, D), lambda i, j: (i, 0))],
        out_specs=pl.BlockSpec((tn, D), lambda i, j: (i, 0)),
        out_shape=jax.ShapeDtypeStruct((T, D), F32),
        scratch_shapes=[pltpu.VMEM((D, tn), F32), pltpu.VMEM((te, tn), BF16), pltpu.VMEM((te, tn), BF16)],
        compiler_params=_cparams(("parallel", "arbitrary")),
    )(xn, u, vt, cnt, rank2, e1, e2, res)


def peer_layer(h, gamma, w_q, sub_keys, u, v):
    q, xn = norm_matmul(h, gamma, w_q.astype(BF16), emit_xn=True)
    keys = sub_keys.reshape(2 * PEER_HEADS, PEER_N_KEYS, PEER_HALF_DIM).astype(BF16)
    cnt, rank2, e1, e2 = peer_select(q, keys)
    return peer_dense(xn, u.astype(BF16), v.T.astype(BF16), cnt, rank2, e1, e2, h)


def _cumsum_aug_kernel(lf_ref, tri_ref, place_q_ref, place_k_ref, ones_q_ref, ones_k_ref,
                       qa_ref, ka_ref, c_ref, carry_ref):
    @pl.when(pl.program_id(1) == 0)
    def _():
        carry_ref[...] = jnp.zeros_like(carry_ref)

    lf = lf_ref[0]
    c = jnp.dot(tri_ref[...], lf, preferred_element_type=F32, precision=lax.Precision.HIGHEST) + carry_ref[...]
    carry_ref[...] = c[-1:, :]
    c = c * LOG2E
    c_ref[0] = c
    hi = c.astype(BF16)
    r1 = c - hi.astype(F32)
    mid = r1.astype(BF16)
    lo = (r1 - mid.astype(F32)).astype(BF16)
    nh = N_HEADS
    lane = lax.broadcasted_iota(jnp.int32, c.shape, 1)
    parts = jnp.where(lane < nh, hi.astype(F32),
                      jnp.where(lane < 2 * nh, pltpu.roll(mid.astype(F32), nh, axis=1),
                                pltpu.roll(lo.astype(F32), 2 * nh, axis=1)))
    parts = jnp.where(lane < 3 * nh, parts, 0.0).astype(BF16)
    qa_ref[0] = (jnp.dot(parts, place_q_ref[...], preferred_element_type=F32) + ones_q_ref[...]).astype(BF16)
    ka_ref[0] = (jnp.dot(parts, place_k_ref[...], preferred_element_type=F32) + ones_k_ref[...]).astype(BF16)


def fox_bias_operands(logf, *, tc=256):
    B, S, _ = logf.shape
    nh = N_HEADS
    tri = jnp.asarray(np.tril(np.ones((tc, tc), np.float32)))
    pq = np.zeros((LANES, nh * LANES), np.float32)
    pk = np.zeros((LANES, nh * LANES), np.float32)
    oq = np.zeros((1, nh * LANES), np.float32)
    ok = np.zeros((1, nh * LANES), np.float32)
    for h in range(nh):
        for part in range(3):
            pq[part * nh + h, h * LANES + part] = 1.0
            pk[part * nh + h, h * LANES + 3 + part] = -1.0
            oq[0, h * LANES + 3 + part] = 1.0
            ok[0, h * LANES + part] = 1.0
    const = lambda a: pl.BlockSpec(a.shape, lambda b, i: (0,) * a.ndim)
    pq, pk, oq, ok = jnp.asarray(pq, BF16), jnp.asarray(pk, BF16), jnp.asarray(oq), jnp.asarray(ok)
    out = jax.ShapeDtypeStruct((B, S, nh * LANES), BF16)
    return pl.pallas_call(
        _cumsum_aug_kernel,
        grid=(B, S // tc),
        in_specs=[pl.BlockSpec((1, tc, LANES), lambda b, i: (b, i, 0)),
                  const(tri), const(pq), const(pk), const(oq), const(ok)],
        out_specs=[pl.BlockSpec((1, tc, nh * LANES), lambda b, i: (b, i, 0))] * 2
        + [pl.BlockSpec((1, tc, LANES), lambda b, i: (b, i, 0))],
        out_shape=[out, out, jax.ShapeDtypeStruct((B, S, LANES), F32)],
        scratch_shapes=[pltpu.VMEM((1, LANES), F32)],
        compiler_params=_cparams(("parallel", "arbitrary")),
    )(logf, tri, pq, pk, oq, ok)


def _fox_attn_kernel(first_ref, q_ref, qa_ref, kv_ref, ka_ref, o_ref, *, tq, tk, heads_per_step):
    qi = pl.program_id(2)
    t0 = qi * tq
    n_full = t0 // tk
    j_first = first_ref[(pl.program_id(0) * pl.num_programs(1) + pl.program_id(1)) * pl.num_programs(2) + qi]
    n_diag = tq // tk
    lanes = [slice(hh * LANES, (hh + 1) * LANES) for hh in range(heads_per_step)]
    qs = [jnp.concatenate([q_ref[0, :, lsl], qa_ref[0, :, lsl]], axis=1) for lsl in lanes]

    def step(j, carry, masked):
        rows = pl.ds(pl.multiple_of(j * tk, tk), tk)
        klane = lax.broadcasted_iota(jnp.int32, (tk, LANES), 1)
        new = []
        for lsl, q, (m, acc) in zip(lanes, qs, carry):
            kv = kv_ref[0, rows, lsl]
            kk = jnp.concatenate([kv, ka_ref[0, rows, lsl]], axis=1)
            ones_v = jnp.where(klane < HEAD_DIM, 1.0, kv).astype(BF16)
            s = lax.dot_general(q, kk, _NT, preferred_element_type=F32)
            if masked:
                qpos = t0 + lax.broadcasted_iota(jnp.int32, (tq, 1), 0)
                kpos = j * tk + lax.broadcasted_iota(jnp.int32, (1, tk), 1)
                s = jnp.where(kpos <= qpos, s, NEG_INF)
            m_new = jnp.maximum(m, jnp.max(s, axis=1, keepdims=True))
            p = jnp.exp2(s - m_new).astype(BF16)
            acc = jnp.exp2(m - m_new) * acc + jnp.dot(p, ones_v, preferred_element_type=F32)
            new.append((m_new, acc))
        return tuple(new)

    init = (jnp.full((tq, 1), NEG_INF, F32), jnp.zeros((tq, LANES), F32))
    n_pairs = (n_full - j_first) // 2
    carry = lax.fori_loop(
        0, n_pairs, lambda i, c: step(j_first + 2 * i + 1, step(j_first + 2 * i, c, False), False),
        (init,) * heads_per_step)
    carry = lax.fori_loop(j_first + 2 * n_pairs, n_full, functools.partial(step, masked=False), carry)
    for d in range(n_diag):
        carry = step(n_full + d, carry, True)
    outs = [acc / acc[:, 0:1] for (_, acc) in carry]
    lane = lax.broadcasted_iota(jnp.int32, (tq, LANES), 1)
    blocks = []
    for pair in range(heads_per_step // 2):
        a, b = outs[2 * pair], outs[2 * pair + 1]
        blocks.append(jnp.where(lane < HEAD_DIM, pltpu.roll(a, HEAD_DIM, axis=1), b))
    o_ref[0] = jnp.concatenate(blocks, axis=1).astype(o_ref.dtype) if len(blocks) > 1 else blocks[0].astype(o_ref.dtype)


def _head_slots(w, n_heads, second=None):
    D = w.shape[0]
    a = w.reshape(D, n_heads, HEAD_DIM)
    b = jnp.zeros_like(a) if second is None else second.reshape(D, n_heads, HEAD_DIM)
    return jnp.concatenate([a, b], axis=-1).reshape(D, n_heads * LANES)


def fox_layer(h, gamma, w_in, f_bias, w_out, B, S):
    aw = ATTN_WIDTH
    wq = _head_slots(w_in[:, :aw] * (HEAD_DIM ** -0.5 * LOG2E), N_HEADS)
    wkv = _head_slots(w_in[:, aw:2 * aw], N_HEADS, w_in[:, 2 * aw:3 * aw])
    w_main = jnp.concatenate([wq, wkv], axis=1).astype(BF16)
    wf = jnp.pad(w_in[:, 3 * aw:], ((0, 0), (0, LANES - N_HEADS))).astype(BF16)
    bf = jnp.pad(f_bias.astype(F32), (0, LANES - N_HEADS))
    qkv = norm_matmul(h, gamma, w_main)
    logf = norm_matmul(h, gamma, wf, bias=bf, act="log_sigmoid", out_dtype=F32)
    qa, ka, c2 = fox_bias_operands(logf.reshape(B, S, LANES))
    qkv = qkv.reshape(B, S, 2 * N_HEADS * LANES)
    o = fox_attention(qkv, qa, ka, c2)
    return matmul_residual(o.reshape(B * S, aw), w_out.astype(BF16), h)


FOX_NEGLIGIBLE_LOG2 = 160.0


def _norm_maxima_kernel(x_ref, ind_ref, o_ref):
    x = x_ref[0].astype(F32)
    ss = jnp.dot((x * x).astype(BF16), ind_ref[...], preferred_element_type=F32)
    o_ref[0, 0] = jnp.broadcast_to(jnp.max(ss, axis=0, keepdims=True), o_ref.shape[2:])


def fox_norm_maxima(qkv, tile):
    B, S, W = qkv.shape
    n = S // tile
    col = np.arange(W)
    slot, lane = col // LANES, col % LANES
    used = (slot < N_HEADS) | (lane < HEAD_DIM)
    ind = jnp.asarray((slot[:, None] == np.arange(LANES)[None, :]) & used[:, None], BF16)
    rows = min(256, tile)
    ss = pl.pallas_call(
        _norm_maxima_kernel,
        grid=(B, S // rows),
        in_specs=[pl.BlockSpec((1, rows, W), lambda b, i: (b, i, 0)),
                  pl.BlockSpec(ind.shape, lambda b, i: (0, 0))],
        out_specs=pl.BlockSpec((1, 1, 8, LANES), lambda b, i: (b, i, 0, 0)),
        out_shape=jax.ShapeDtypeStruct((B, S // rows, 8, LANES), F32),
        compiler_params=_cparams(("parallel", "parallel")),
    )(qkv, ind)
    ss = jnp.max(ss[:, :, 0, :2 * N_HEADS].reshape(B, n, tile // rows, 2 * N_HEADS), axis=2)
    return jnp.sqrt(ss * 1.01)


def fox_first_chunk(qkv, c2, tile, heads_per_step):
    B, S, _ = qkv.shape
    n = S // tile
    norms = fox_norm_maxima(qkv, tile)
    qmax, kmax = norms[..., :N_HEADS], norms[..., N_HEADS:]
    c = c2[..., :N_HEADS].reshape(B, n, tile, N_HEADS)
    cmax, cmin = jnp.max(c, axis=2), jnp.min(c, axis=2)
    upper = qmax[:, :, None] * kmax[:, None, :] + cmax[:, :, None] - cmin[:, None, :]
    own = -(qmax * kmax)
    earlier = jnp.arange(n)[None, :, None, None] > jnp.arange(n)[None, None, :, None]
    skip = (upper < own[:, :, None] - FOX_NEGLIGIBLE_LOG2) & earlier
    first = jnp.sum(jnp.cumprod(skip.astype(jnp.int32), axis=2), axis=2)
    first = jnp.min(first.reshape(B, n, N_HEADS // heads_per_step, heads_per_step), axis=-1)
    return jnp.transpose(first, (0, 2, 1)).reshape(-1).astype(jnp.int32)


def fox_attention(qkv, qa, ka, c2, *, tq=1024, tk=1024, heads_per_step=2):
    B, S, _ = qkv.shape
    tq = tk = min(tq, S)
    hs = heads_per_step
    wq = hs * LANES
    n_qblk = N_HEADS // hs
    first = fox_first_chunk(qkv, c2, tk, hs)
    grid_spec = pltpu.PrefetchScalarGridSpec(
        num_scalar_prefetch=1,
        grid=(B, n_qblk, S // tq),
        in_specs=[pl.BlockSpec((1, tq, wq), lambda b, h, i, first: (b, i, h)),
                  pl.BlockSpec((1, tq, wq), lambda b, h, i, first: (b, i, h)),
                  pl.BlockSpec((1, S, wq), lambda b, h, i, first: (b, 0, n_qblk + h)),
                  pl.BlockSpec((1, S, wq), lambda b, h, i, first: (b, 0, h))],
        out_specs=pl.BlockSpec((1, tq, hs * HEAD_DIM), lambda b, h, i, first: (b, i, h)))
    return pl.pallas_call(
        functools.partial(_fox_attn_kernel, tq=tq, tk=tk, heads_per_step=hs),
        grid_spec=grid_spec,
        out_shape=jax.ShapeDtypeStruct((B, S, ATTN_WIDTH), BF16),
        compiler_params=_cparams(("parallel", "parallel", "arbitrary")),
    )(first, qkv, qa, qkv, ka)


def _rot_half_cols(w):
    D = w.shape[0]
    a = w.reshape(D, -1, HEAD_DIM)
    half = HEAD_DIM // 2
    return jnp.concatenate([-a[..., half:], a[..., :half]], axis=-1).reshape(w.shape)


def _rope_tables(S):
    half = HEAD_DIM // 2
    inv_freq = ROPE_THETA ** (-jnp.arange(half, dtype=F32) / half)
    ang = jnp.arange(S, dtype=F32)[:, None] * inv_freq[None, :]
    c, s = jnp.cos(ang), jnp.sin(ang)
    c2, s2 = jnp.concatenate([c, c], axis=1), jnp.concatenate([s, s], axis=1)
    cos = jnp.stack([jnp.concatenate([c2, jnp.ones_like(c2)], axis=1), jnp.concatenate([c2, c2], axis=1)])
    sin = jnp.stack([jnp.concatenate([s2, jnp.zeros_like(s2)], axis=1), jnp.concatenate([s2, s2], axis=1)])
    return cos, sin


def _compress_kernel(x_ref, pea_ref, peb_ref, wa_ref, wb_ref, w2_ref, o_ref, pa_ref, pb0_ref, *, n_rows):
    u = pl.program_id(1)
    x = x_ref[0].astype(F32)
    pa = jnp.dot((x + pea_ref[...]).astype(BF16), wa_ref[...], preferred_element_type=F32)
    pb = jnp.dot((x + peb_ref[...]).astype(BF16), wb_ref[...], preferred_element_type=F32)

    def emit(slab, hid):
        y = jnp.dot(_gelu_tanh(hid).astype(BF16), w2_ref[...], preferred_element_type=F32)
        o_ref[0, pl.ds(pl.multiple_of(slab * n_rows, n_rows), n_rows), :] = y

    @pl.when(u == 0)
    def _():
        pb0_ref[...] = pb

    @pl.when(u > 0)
    def _():
        emit(u - 1, pa_ref[...] + pb)

    @pl.when(u == 3)
    def _():
        emit(3, pa + pltpu.roll(pb0_ref[...], n_rows - 1, axis=0))

    pa_ref[...] = pa


def nsa_compress(src, pe, w1, w2):
    B, S, W = src.shape
    G = NSA_GROUPS
    n_rows = S // 64
    half = CMP_BLOCK // 2
    cw = half * W
    xv = src.reshape(B, n_rows, 4 * cw)
    pe_flat = jnp.transpose(pe, (1, 0, 2)).reshape(CMP_BLOCK, W).astype(F32)
    pea, peb = pe_flat[:half].reshape(1, cw), pe_flat[half:].reshape(1, cw)
    eye = jnp.eye(G, dtype=F32)
    wfull = jnp.einsum('gldh,gk->lkdgh', w1.astype(F32), eye).reshape(CMP_BLOCK, W, G * CMP_HIDDEN)
    wa = wfull[:half].reshape(cw, G * CMP_HIDDEN).astype(BF16)
    wb = wfull[half:].reshape(cw, G * CMP_HIDDEN).astype(BF16)
    w2bd = jnp.einsum('ghd,gk->ghkd', w2.astype(F32), eye).reshape(G * CMP_HIDDEN, W).astype(BF16)
    const = lambda a: pl.BlockSpec(a.shape, lambda b, u: (0,) * a.ndim)
    return pl.pallas_call(
        functools.partial(_compress_kernel, n_rows=n_rows),
        grid=(B, 4),
        in_specs=[pl.BlockSpec((1, n_rows, cw), lambda b, u: (b, 0, u)),
                  const(pea), const(peb), const(wa), const(wb), const(w2bd)],
        out_specs=pl.BlockSpec((1, 4 * n_rows, W), lambda b, u: (b, 0, 0)),
        out_shape=jax.ShapeDtypeStruct((B, 4 * n_rows, W), F32),
        scratch_shapes=[pltpu.VMEM((n_rows, G * CMP_HIDDEN), F32), pltpu.VMEM((n_rows, G * CMP_HIDDEN), F32)],
        compiler_params=_cparams(("parallel", "arbitrary")),
    )(xv, pea, peb, wa, wb, w2bd)


def _nsa_attn_kernel(q_ref, kvs_ref, kvw_ref, kvc_ref, kvct_ref, gate_ref, bmat_ref, pmat_ref, o_ref,
                     *, tq, ts, tk, seq):
    R = NSA_Q_PER_GROUP
    n_slc = seq // SLC_BLOCK
    n_sel = min(SLC_TOPK, n_slc)
    assert ts & (ts - 1) == 0 and n_slc & (n_slc - 1) == 0 and tk % tq == 0 and tq % ts == 0
    qi = pl.program_id(2)
    t0 = qi * tq

    local = [_nsa_local_branches(q_ref, kvw_ref, kvc_ref, kvct_ref, t0 + i * ts, i * ts, ts, n_slc, n_sel)
             for i in range(tq // ts)]
    o_c = jnp.concatenate([o[0][r * ts:(r + 1) * ts] for r in range(R) for o in local], axis=0)
    o_w = jnp.concatenate([o[1][r * ts:(r + 1) * ts] for r in range(R) for o in local], axis=0)
    sel_bias = jnp.concatenate([o[2] for o in local], axis=0)
    qs = jnp.concatenate([q_ref[0, :, r * LANES:(r + 1) * LANES] for r in range(R)], axis=0)
    _nsa_selected_and_combine(qs, o_c, o_w, sel_bias, kvs_ref, gate_ref, bmat_ref, pmat_ref, o_ref,
                              t0=t0, tq=tq, tk=tk, seq=seq)


def _nsa_local_branches(q_ref, kvw_ref, kvc_ref, kvct_ref, t0, row0, tq, n_slc, n_sel):
    R = NSA_Q_PER_GROUP
    M = R * tq
    qs = jnp.concatenate([q_ref[0, row0:row0 + tq, r * LANES:(r + 1) * LANES] for r in range(R)], axis=0)

    wlen = WINDOW + tq
    start = jnp.maximum(t0 - WINDOW, 0)
    kvw = kvw_ref[0, pl.ds(pl.multiple_of(start, tq), wlen), :]
    s_w = lax.dot_general(qs, kvw, _NT, preferred_element_type=F32)
    qpos = t0 + (lax.broadcasted_iota(jnp.int32, (M, 1), 0) & (tq - 1))
    kpos = start + lax.broadcasted_iota(jnp.int32, (1, wlen), 1)
    s_w = jnp.where(kpos <= qpos, jnp.where(kpos > qpos - WINDOW, s_w, NEG_INF), NEG_INF)
    p_w = jnp.exp2(s_w - jnp.max(s_w, axis=1, keepdims=True)).astype(BF16)
    wlane = lax.broadcasted_iota(jnp.int32, (wlen, LANES), 1)
    acc_w = jnp.dot(p_w, jnp.where(wlane < HEAD_DIM, 1.0, kvw).astype(BF16), preferred_element_type=F32)
    o_w = acc_w / acc_w[:, 0:1]

    n_groups, gs = _nsa_row_groups(n_slc)
    o_c, sel_bias = _nsa_compressed_and_select(qs, kvc_ref, kvct_ref, t0, tq, n_slc, n_sel, gs, n_groups)
    return o_c, o_w, sel_bias


def _nsa_row_groups(n_slc):
    n_groups = max(1, min(4, n_slc // 8))
    assert n_slc % n_groups == 0 and (n_slc // n_groups) % 8 == 0
    return n_groups, n_slc // n_groups


def _nsa_compressed_and_select(qs, kvc_ref, kvct_ref, t0, tq, n_slc, n_sel, gs, n_groups):
    R = NSA_Q_PER_GROUP
    M = R * tq
    n_rows, n_j = n_groups * 4 * gs, n_groups * gs
    log_gs = gs.bit_length() - 1
    sT = lax.dot_general(kvc_ref[0, 0, 0:n_rows, :], qs, _NT, preferred_element_type=F32)
    rowc = lax.broadcasted_iota(jnp.int32, (n_rows, 1), 0)
    blk_j = ((rowc >> (log_gs + 2)) << log_gs) + (rowc & (gs - 1))
    blk_u = (rowc >> log_gs) & 3
    cmp_end = blk_j * SLC_BLOCK + blk_u * CMP_STRIDE + (CMP_BLOCK - 1)
    tcol = t0 + (lax.broadcasted_iota(jnp.int32, (1, M), 1) & (tq - 1))
    sm = jnp.where(cmp_end <= tcol, sT, NEG_INF)
    mx = jnp.max(sm, axis=0, keepdims=True)
    e = jnp.exp2(sm - mx)
    inv = jnp.where(mx > 0.5 * NEG_INF, 1.0 / jnp.sum(e, axis=0, keepdims=True), 0.0)
    pT = e * inv
    o_c = jnp.dot(kvct_ref[0, 0, :, 0:n_rows], pT.astype(BF16), preferred_element_type=F32).T

    psum = pT[:, 0:tq]
    for r in range(1, R):
        psum = psum + pT[:, r * tq:(r + 1) * tq]
    p0, p1, p2, p3 = (jnp.concatenate([psum[(4 * g + u) * gs:(4 * g + u + 1) * gs, :] for g in range(n_groups)],
                                      axis=0) for u in range(4))
    jrow = lax.broadcasted_iota(jnp.int32, (n_j, tq), 0)
    p3_prev = jnp.where(jrow == 0, 0.0, pltpu.roll(p3, 1, axis=0))
    imp = p0 + p1 + p2 + 0.5 * p3 + 0.5 * p3_prev
    cur = (t0 + lax.broadcasted_iota(jnp.int32, (n_j, tq), 1)) >> (SLC_BLOCK.bit_length() - 1)
    forced = (jrow == 0) | (jrow == cur) | (jrow == cur - 1)
    vals = jnp.where(forced, -jnp.inf, jnp.where(jrow <= cur, imp, -jnp.inf))
    sel0 = jnp.where(forced, 1.0, 0.0)
    n_free = n_sel - 3

    picked = -2.0 ** 100

    def pick(exact_ties):
        v = vals
        for _ in range(n_free):
            m = jnp.max(v, axis=0, keepdims=True)
            if exact_ties:
                hit = jrow == jnp.min(jnp.where(v == m, jrow, n_slc), axis=0, keepdims=True)
            else:
                hit = v == jnp.where(m < 0.0, jnp.nan, m)
            v = jnp.where(hit, picked, v)
        return jnp.where(v == picked, 1.0, sel0)

    sel_fast = pick(False)
    n_cand = jnp.sum(jnp.where(vals > -jnp.inf, 1.0, 0.0), axis=0, keepdims=True)
    n_picked = jnp.sum(sel_fast - sel0, axis=0, keepdims=True)
    unique = jnp.min(jnp.where(n_picked == jnp.minimum(n_cand, float(n_free)), 1.0, 0.0)) > 0.5
    sel = lax.cond(unique, lambda: sel_fast, lambda: pick(True))
    if n_j < n_slc:
        sel = jnp.concatenate([sel, jnp.zeros((n_slc - n_j, tq), F32)], axis=0)
    sel_bias = ((sel.T - 1.0) * (-NEG_INF)).astype(BF16)
    return o_c, sel_bias


def _nsa_selected_and_combine(qs, o_c, o_w, sel_bias, kvs_ref, gate_ref, bmat_ref, pmat_ref, o_ref,
                              *, t0, tq, tk, seq):
    R = NSA_Q_PER_GROUP
    n_slc = seq // SLC_BLOCK
    blocks_per_chunk = tk // SLC_BLOCK
    n_parts = 2
    hp = R // n_parts
    mp = hp * tq
    qparts = [qs[i * mp:(i + 1) * mp] for i in range(n_parts)]
    qrow = t0 + (lax.broadcasted_iota(jnp.int32, (mp, 1), 0) & (tq - 1))
    kcol = lax.broadcasted_iota(jnp.int32, (1, tk), 1)
    n_chunks = seq // tk
    j_last = t0 // tk

    klane = lax.broadcasted_iota(jnp.int32, (tk, LANES), 1)

    def slc_step(j, carry, masked):
        kv = kvs_ref[0, pl.ds(pl.multiple_of(j * tk, tk), tk), :]
        k_sel = jnp.where(klane < HEAD_DIM, kv, bmat_ref[...]).astype(BF16)
        ones_v = jnp.where(klane < HEAD_DIM, 1.0, kv).astype(BF16)
        off = pl.multiple_of(blocks_per_chunk * (n_chunks - 1 - j), blocks_per_chunk)
        place = pmat_ref[pl.ds(off, n_slc), :].astype(BF16)
        q_bias = jnp.dot(sel_bias, place, preferred_element_type=F32).astype(BF16)
        q_bias = jnp.concatenate([q_bias] * hp, axis=0)
        new = []
        for qp, (m, acc) in zip(qparts, carry):
            s = lax.dot_general(qp + q_bias, k_sel, _NT, preferred_element_type=F32)
            if masked:
                s = jnp.where(j * tk + kcol <= qrow, s, NEG_INF)
            m_new = jnp.maximum(m, jnp.max(s, axis=1, keepdims=True))
            p = jnp.exp2(s - m_new).astype(BF16)
            acc = jnp.exp2(m - m_new) * acc + jnp.dot(p, ones_v, preferred_element_type=F32)
            new.append((m_new, acc))
        return tuple(new)

    init = (jnp.full((mp, 1), NEG_INF, F32), jnp.zeros((mp, LANES), F32))
    n_pairs = j_last // 2
    carry = lax.fori_loop(0, n_pairs, lambda i, c: slc_step(2 * i + 1, slc_step(2 * i, c, False), False),
                          (init,) * n_parts)
    carry = lax.fori_loop(2 * n_pairs, j_last, functools.partial(slc_step, masked=False), carry)
    carry = slc_step(j_last, carry, True)
    o_s = jnp.concatenate([acc / acc[:, 0:1] for (_, acc) in carry], axis=0)

    gates = gate_ref[0]
    lane = lax.broadcasted_iota(jnp.int32, (tq, LANES), 1)
    comb = []
    for r in range(R):
        rs = slice(r * tq, (r + 1) * tq)
        comb.append(gates[:, r:r + 1] * o_c[rs] + gates[:, R + r:R + r + 1] * o_s[rs]
                    + gates[:, 2 * R + r:2 * R + r + 1] * o_w[rs])
    out = [jnp.where(lane < HEAD_DIM, pltpu.roll(comb[2 * i], HEAD_DIM, axis=1), comb[2 * i + 1])
           for i in range(R // 2)]
    o_ref[0] = jnp.concatenate(out, axis=1).astype(o_ref.dtype)


def nsa_attention(qkv, kvc, kvct, gates, *, tq=512, ts=256, tk=1024):
    B, S, _ = qkv.shape
    G, R = NSA_GROUPS, NSA_Q_PER_GROUP
    tk = min(tk, S)
    n_slc = S // SLC_BLOCK
    n_cmp = kvc.shape[2]
    bpc = tk // SLC_BLOCK
    assert bpc <= LANES - HEAD_DIM
    off = bpc * (S // tk - 1)
    lane = np.arange(LANES)[None, :]
    bmat = jnp.asarray(lane - HEAD_DIM == np.arange(tk)[:, None] // SLC_BLOCK, BF16)
    pmat = jnp.asarray((np.arange(n_slc + off)[:, None] - off == lane - HEAD_DIM) & (lane >= HEAD_DIM)
                       & (lane < HEAD_DIM + bpc), F32)
    slc_blk0, win_blk0 = N_HEADS, N_HEADS + G
    return pl.pallas_call(
        functools.partial(_nsa_attn_kernel, tq=tq, ts=ts, tk=tk, seq=S),
        grid=(B, G, S // tq),
        in_specs=[pl.BlockSpec((1, tq, R * LANES), lambda b, g, i: (b, i, g)),
                  pl.BlockSpec((1, S, LANES), lambda b, g, i: (b, 0, slc_blk0 + g)),
                  pl.BlockSpec((1, S, LANES), lambda b, g, i: (b, 0, win_blk0 + g)),
                  pl.BlockSpec((1, 1, n_cmp, LANES), lambda b, g, i: (b, g, 0, 0)),
                  pl.BlockSpec((1, 1, LANES, n_cmp), lambda b, g, i: (b, g, 0, 0)),
                  pl.BlockSpec((1, tq, LANES), lambda b, g, i: (b, i, g)),
                  pl.BlockSpec(bmat.shape, lambda b, g, i: (0, 0)),
                  pl.BlockSpec(pmat.shape, lambda b, g, i: (0, 0))],
        out_specs=pl.BlockSpec((1, tq, R * HEAD_DIM), lambda b, g, i: (b, i, g)),
        out_shape=jax.ShapeDtypeStruct((B, S, ATTN_WIDTH), BF16),
        compiler_params=_cparams(("parallel", "parallel", "arbitrary")),
    )(qkv, qkv, qkv, kvc, kvct, gates, bmat, pmat)


def nsa_layer(h, gamma, w_in, pe_k, w1_k, w2_k, pe_v, w1_v, w2_v, w_out, B, S):
    G, R, hd, aw = NSA_GROUPS, NSA_Q_PER_GROUP, HEAD_DIM, ATTN_WIDTH
    kvd = G * hd
    sec = lambda i: w_in[:, aw + i * kvd: aw + (i + 1) * kvd]
    wq = w_in[:, :aw] * (hd ** -0.5 * LOG2E)
    wa = jnp.concatenate([_head_slots(wq, N_HEADS), _head_slots(sec(2), G, sec(3)),
                          _head_slots(sec(4), G, sec(5))], axis=1).astype(BF16)
    wb = jnp.concatenate([_rot_half_cols(wq), _rot_half_cols(sec(2)), _rot_half_cols(sec(4))], axis=1).astype(BF16)
    cos, sin = _rope_tables(S)
    qkv = norm_matmul(h, gamma, wa, wb=wb, cos=cos, sin=sin)
    kc_src = norm_matmul(h, gamma, sec(0).astype(BF16), wb=_rot_half_cols(sec(0)).astype(BF16),
                         cos=cos, sin=sin, table_of_tile=lambda j: 1)
    vc_src = norm_matmul(h, gamma, sec(1).astype(BF16))
    wg = w_in[:, aw + 6 * kvd:].reshape(-1, 3, G, R)
    wg = jnp.transpose(wg, (0, 2, 1, 3)).reshape(-1, G, 3 * R)
    wg = jnp.pad(wg, ((0, 0), (0, 0), (0, LANES - 3 * R))).reshape(-1, G * LANES).astype(BF16)
    gates = norm_matmul(h, gamma, wg, act="sigmoid", out_dtype=F32)
    kc = nsa_compress(kc_src.reshape(B, S, kvd), pe_k, w1_k, w2_k)
    vc = nsa_compress(vc_src.reshape(B, S, kvd), pe_v, w1_v, w2_v)
    n_cmp = kc.shape[1]
    kvc = jnp.concatenate([kc.reshape(B, n_cmp, G, hd), vc.reshape(B, n_cmp, G, hd)], axis=-1)
    n_groups, gs = _nsa_row_groups(S // SLC_BLOCK)
    kvc = kvc.reshape(B, 4, n_groups, gs, G, 2 * hd)
    kvc = jnp.transpose(kvc, (0, 4, 2, 1, 3, 5)).reshape(B, G, n_cmp, 2 * hd).astype(BF16)
    kvct = jnp.swapaxes(kvc, 2, 3)
    o = nsa_attention(qkv.reshape(B, S, -1), kvc, kvct, gates.reshape(B, S, G * LANES))
    return matmul_residual(o.reshape(B * S, aw), w_out.astype(BF16), h)


def kernel(x, l0_attn_norm, l0_w_in, l0_cmp_pe_k, l0_cmp_w1_k, l0_cmp_w2_k, l0_cmp_pe_v, l0_cmp_w1_v,
           l0_cmp_w2_v, l0_w_out, l0_ffn_norm, l0_peer_wq, l0_peer_keys, l0_peer_u, l0_peer_v,
           l1_attn_norm, l1_w_in, l1_f_bias, l1_w_out, l1_ffn_norm, l1_peer_wq, l1_peer_keys, l1_peer_u,
           l1_peer_v, final_norm):
    B, S, D = x.shape
    h = x.reshape(B * S, D)
    h = nsa_layer(h, l0_attn_norm, l0_w_in, l0_cmp_pe_k, l0_cmp_w1_k, l0_cmp_w2_k, l0_cmp_pe_v, l0_cmp_w1_v,
                  l0_cmp_w2_v, l0_w_out, B, S)
    h = peer_layer(h, l0_ffn_norm, l0_peer_wq, l0_peer_keys, l0_peer_u, l0_peer_v)
    h = fox_layer(h, l1_attn_norm, l1_w_in, l1_f_bias, l1_w_out, B, S)
    h = peer_layer(h, l1_ffn_norm, l1_peer_wq, l1_peer_keys, l1_peer_u, l1_peer_v)
    return rmsnorm(h, final_norm).reshape(B, S, D)
```

```python
import functools

import numpy as np
import jax
import jax.numpy as jnp
from jax import lax
from jax.experimental import pallas as pl
from jax.experimental.pallas import tpu as pltpu

F32 = jnp.float32
BF16 = jnp.bfloat16

D_MODEL = 1024
N_HEADS = 16
HEAD_DIM = 64
ATTN_WIDTH = N_HEADS * HEAD_DIM
NSA_GROUPS = 4
NSA_Q_PER_GROUP = N_HEADS // NSA_GROUPS
CMP_BLOCK = 32
CMP_STRIDE = 16
CMP_HIDDEN = 2 * HEAD_DIM
SLC_BLOCK = 64
SLC_TOPK = 16
WINDOW = 512
FORCE_SCORE = 1.0e4
ROPE_THETA = 10000.0
PEER_HEADS = 8
PEER_N_KEYS = 128
PEER_TOPK = 16
PEER_HALF_DIM = 128
RMS_EPS = 1e-6
NEG_INF = -1e30
LOG2E = 1.4426950408889634

LANES = 128
VMEM_LIMIT_BYTES = 56 * 1024 * 1024

_NT = (((1,), (1,)), ((), ()))


def _cparams(sem, vmem=VMEM_LIMIT_BYTES, flags=None):
    return pltpu.CompilerParams(dimension_semantics=sem, vmem_limit_bytes=vmem, flags=flags)


def _gelu_tanh(x):
    return 0.5 * x * (1.0 + jnp.tanh(0.7978845608028654 * (x + 0.044715 * (x * x * x))))


def _gelu_sigmoid(x):
    c = -2.0 * 0.7978845608028654 * LOG2E
    t = x * (c + (c * 0.044715) * (x * x))
    return x / (1.0 + jnp.exp2(t))


def _rms_rows(x, g):
    ms = jnp.mean(x * x, axis=-1, keepdims=True)
    return x * lax.rsqrt(ms + RMS_EPS) * g


def _norm_mm_kernel(*refs, act, has_bias, rope, emit_xn):
    it = iter(refs)
    x_ref, g_ref = next(it), next(it)
    wa_ref = next(it)
    wb_ref = next(it) if rope else None
    cos_ref = next(it) if rope else None
    sin_ref = next(it) if rope else None
    b_ref = next(it) if has_bias else None
    o_ref = next(it)
    xo_ref = next(it) if emit_xn else None
    xn_ref = next(it)

    @pl.when(pl.program_id(1) == 0)
    def _():
        xn = _rms_rows(x_ref[...], g_ref[...])
        xn_ref[...] = xn.astype(BF16)
        if emit_xn:
            xo_ref[...] = xn.T.astype(BF16)

    xn = xn_ref[...]
    y = jnp.dot(xn, wa_ref[...], preferred_element_type=F32)
    if rope:
        yb = jnp.dot(xn, wb_ref[...], preferred_element_type=F32)
        cos, sin = cos_ref[...], sin_ref[...]
        packed_b = 2 * yb.shape[1] == y.shape[1]
        for s in range(y.shape[1] // LANES):
            sl = slice(s * LANES, (s + 1) * LANES)
            if packed_b:
                b = yb[:, (s // 2) * LANES:(s // 2 + 1) * LANES]
                b = pltpu.roll(b, HEAD_DIM, axis=1) if s % 2 else b
            else:
                b = yb[:, sl]
            o_ref[:, sl] = (y[:, sl] * cos + b * sin).astype(o_ref.dtype)
        return
    if has_bias:
        y = y + b_ref[...]
    if act == "sigmoid":
        y = jax.nn.sigmoid(y)
    elif act == "log_sigmoid":
        y = jax.nn.log_sigmoid(y)
    o_ref[...] = y.astype(o_ref.dtype)


def norm_matmul(x, gamma, wa, *, wb=None, cos=None, sin=None, table_of_tile=None, bias=None,
                act=None, out_dtype=BF16, emit_xn=False, tm=1024, tn=1024):
    T, D = x.shape
    N = wa.shape[1]
    tm, tn = min(tm, T), min(tn, N)
    assert T % tm == 0 and N % tn == 0 and tn % LANES == 0
    rope = wb is not None
    in_specs = [pl.BlockSpec((tm, D), lambda i, j: (i, 0)),
                pl.BlockSpec((1, D), lambda i, j: (0, 0)),
                pl.BlockSpec((D, tn), lambda i, j: (0, j))]
    args = [x, gamma.reshape(1, D).astype(F32), wa]
    if rope:
        S = cos.shape[1]
        assert S % tm == 0
        n_pos = S // tm
        tmap = table_of_tile if table_of_tile is not None else (lambda j: 0)
        assert wb.shape[1] in (N, N // 2)
        in_specs += [pl.BlockSpec((D, tn * wb.shape[1] // N), lambda i, j: (0, j)),
                     pl.BlockSpec((None, tm, LANES), lambda i, j: (tmap(j), i % n_pos, 0)),
                     pl.BlockSpec((None, tm, LANES), lambda i, j: (tmap(j), i % n_pos, 0))]
        args += [wb, cos, sin]
    if bias is not None:
        in_specs.append(pl.BlockSpec((1, tn), lambda i, j: (0, j)))
        args.append(bias.reshape(1, N).astype(F32))
    out_shape = [jax.ShapeDtypeStruct((T, N), out_dtype)]
    out_specs = [pl.BlockSpec((tm, tn), lambda i, j: (i, j))]
    if emit_xn:
        out_shape.append(jax.ShapeDtypeStruct((D, T), BF16))
        out_specs.append(pl.BlockSpec((D, tm), lambda i, j: (0, i)))
    res = pl.pallas_call(
        functools.partial(_norm_mm_kernel, act=act, has_bias=bias is not None, rope=rope, emit_xn=emit_xn),
        grid=(T // tm, N // tn),
        in_specs=in_specs,
        out_specs=out_specs,
        out_shape=out_shape,
        scratch_shapes=[pltpu.VMEM((tm, D), BF16)],
        compiler_params=_cparams(("parallel", "arbitrary")),
    )(*args)
    return res if emit_xn else res[0]


def _mm_res_kernel(a_ref, w_ref, r_ref, o_ref):
    o_ref[...] = r_ref[...] + jnp.dot(a_ref[...], w_ref[...], preferred_element_type=F32)


def matmul_residual(a, w, res, *, tm=1024, tn=1024):
    T, K = a.shape
    N = w.shape[1]
    tm, tn = min(tm, T), min(tn, N)
    assert T % tm == 0 and N % tn == 0
    return pl.pallas_call(
        _mm_res_kernel,
        grid=(T // tm, N // tn),
        in_specs=[pl.BlockSpec((tm, K), lambda i, j: (i, 0)),
                  pl.BlockSpec((K, tn), lambda i, j: (0, j)),
                  pl.BlockSpec((tm, tn), lambda i, j: (i, j))],
        out_specs=pl.BlockSpec((tm, tn), lambda i, j: (i, j)),
        out_shape=jax.ShapeDtypeStruct((T, N), F32),
        compiler_params=_cparams(("parallel", "arbitrary")),
    )(a, w, res)


def _rmsnorm_kernel(x_ref, g_ref, o_ref):
    o_ref[...] = _rms_rows(x_ref[...], g_ref[...])


def rmsnorm(x, gamma, *, tm=1024):
    T, D = x.shape
    tm = min(tm, T)
    return pl.pallas_call(
        _rmsnorm_kernel,
        grid=(T // tm,),
        in_specs=[pl.BlockSpec((tm, D), lambda i: (i, 0)), pl.BlockSpec((1, D), lambda i: (0, 0))],
        out_specs=pl.BlockSpec((tm, D), lambda i: (i, 0)),
        out_shape=jax.ShapeDtypeStruct((T, D), F32),
        compiler_params=_cparams(("parallel",)),
    )(x, gamma.reshape(1, D).astype(F32))


def _peer_cand_tables(tn):
    fidx, vmask = [], []
    for k2 in range(16):
        fidx.append(k2); vmask.append(0.0)
    for k1 in range(1, 8):
        lim = PEER_TOPK // (k1 + 1)
        for k2 in range(8):
            fidx.append(k1 * 16 + k2); vmask.append(0.0 if k2 < lim else -np.inf)
    for k1 in range(8, 16):
        fidx.append(k1 * 16); vmask.append(0.0)
    fidx = np.broadcast_to(np.asarray(fidx, np.int32)[:, None], (80, tn))
    vmask = np.broadcast_to(np.asarray(vmask, np.float32)[:, None], (80, tn))
    return jnp.asarray(fidx), jnp.asarray(vmask)


def _top16_rows(s, exact_ties):
    n, tn = s.shape
    rows = lax.broadcasted_iota(jnp.int32, (n, tn), 0)
    rows16 = lax.broadcasted_iota(jnp.int32, (PEER_TOPK, tn), 0)
    tops = jnp.zeros((PEER_TOPK, tn), F32)
    unit = 2.0 ** 122
    v = s
    for k in range(PEER_TOPK):
        m = jnp.max(v, axis=0, keepdims=True)
        if exact_ties:
            hit = rows == jnp.min(jnp.where(v == m, rows, n), axis=0, keepdims=True)
        else:
            hit = v == m
        v = jnp.where(hit, -(32.0 + k) * unit, v)
        tops = jnp.where(rows16 == k, m, tops)
    was_picked = v <= -32.0 * unit
    rank = jnp.where(was_picked, v * (-1.0 / unit) - 32.0, float(PEER_TOPK))
    n_picked = jnp.sum(jnp.where(was_picked, 1.0, 0.0), axis=0, keepdims=True)
    return tops, rank, n_picked


def _peer_select_head(q_ref, keys_ref, fidx, vmask, exact_ties):
    tops, ranks, es, picked = [], [], [], []
    for p in range(2):
        q = q_ref[:, p * PEER_HALF_DIM:(p + 1) * PEER_HALF_DIM]
        s = lax.dot_general(keys_ref[p], q, _NT, preferred_element_type=F32)
        t, r, n_picked = _top16_rows(s, exact_ties)
        tops.append(t); ranks.append(r)
        es.append(jnp.exp(s - t[0:1, :]))
        picked.append(n_picked)
    ts1, ts2 = tops
    pieces = [ts1[0:1, :] + ts2]
    for k1 in range(1, 8):
        pieces.append(ts1[k1:k1 + 1, :] + ts2[0:8, :])
    pieces.append(ts1[8:16, :] + ts2[0:1, :])
    cand0 = jnp.concatenate(pieces, axis=0) + vmask
    cand = cand0
    for _ in range(PEER_TOPK):
        m = jnp.max(cand, axis=0, keepdims=True)
        if exact_ties:
            hit = fidx == jnp.min(jnp.where(cand == m, fidx, 4096), axis=0, keepdims=True)
        else:
            hit = cand == m
        cand = jnp.where(hit, -jnp.inf, cand)
    taken = jnp.logical_and(cand == -jnp.inf, vmask == 0.0)
    takenf = taken.astype(F32)
    picked.append(jnp.sum(takenf, axis=0, keepdims=True))
    unique = jnp.min(jnp.where((picked[0] == PEER_TOPK) & (picked[1] == PEER_TOPK) & (picked[2] == PEER_TOPK),
                               1.0, 0.0)) > 0.5
    best = ts1[0:1, :] + ts2[0:1, :]
    z = jnp.sum(jnp.where(taken, jnp.exp(cand0 - best), 0.0), axis=0, keepdims=True)
    counts = [jnp.sum(takenf[0:16, :], axis=0, keepdims=True)]
    for k1 in range(1, 8):
        counts.append(jnp.sum(takenf[16 + 8 * (k1 - 1):16 + 8 * k1, :], axis=0, keepdims=True))
    tail = takenf[72:80, :]
    cnt = jnp.zeros_like(ranks[0])
    for k1 in range(PEER_TOPK):
        nk = counts[k1] if k1 < 8 else tail[k1 - 8:k1 - 7, :]
        cnt = jnp.where(ranks[0] == float(k1), nk, cnt)
    return (cnt, ranks[1], es[0], es[1] / z), unique


def _peer_select_kernel(q_ref, keys_ref, fidx_ref, vmask_ref, cnt_ref, rank2_ref, e1_ref, e2_ref):
    fidx = fidx_ref[...]
    vmask = vmask_ref[...]

    def store(vals):
        for ref, val in zip((cnt_ref, rank2_ref, e1_ref, e2_ref), vals):
            ref[...] = val.astype(ref.dtype)

    vals, unique = _peer_select_head(q_ref, keys_ref, fidx, vmask, exact_ties=False)
    store(vals)

    @pl.when(jnp.logical_not(unique))
    def _():
        store(_peer_select_head(q_ref, keys_ref, fidx, vmask, exact_ties=True)[0])


def peer_select(q, keys, *, tn=512):
    T = q.shape[0]
    tn = min(tn, T)
    fidx, vmask = _peer_cand_tables(tn)
    rows = PEER_HEADS * PEER_N_KEYS
    ospec = pl.BlockSpec((PEER_N_KEYS, tn), lambda i, h: (h, i))
    return pl.pallas_call(
        _peer_select_kernel,
        grid=(T // tn, PEER_HEADS),
        in_specs=[pl.BlockSpec((tn, 2 * PEER_HALF_DIM), lambda i, h: (i, h)),
                  pl.BlockSpec((2, PEER_N_KEYS, PEER_HALF_DIM), lambda i, h: (h, 0, 0)),
                  pl.BlockSpec((80, tn), lambda i, h: (0, 0)),
                  pl.BlockSpec((80, tn), lambda i, h: (0, 0))],
        out_specs=[ospec] * 4,
        out_shape=[jax.ShapeDtypeStruct((rows, T), dt) for dt in (F32, BF16, F32, BF16)],
        compiler_params=_cparams(("parallel", "parallel")),
    )(q, keys, fidx, vmask)


def _peer_dense_kernel(xn_ref, u_ref, vt_ref, cnt_ref, rank2_ref, e1_ref, e2_ref, res_ref, o_ref,
                       acc_ref, g0_ref, g1_ref, *, c_per_step):
    j = pl.program_id(1)
    n_tiles = pl.num_programs(1) - 1

    @pl.when(j == 0)
    def _():
        acc_ref[...] = jnp.zeros_like(acc_ref)
        g1_ref[...] = jnp.zeros_like(g1_ref)

    @pl.when((j % 2 == 0) & (j < n_tiles))
    def _():
        _peer_dense_step(xn_ref, u_ref, vt_ref, cnt_ref, rank2_ref, e1_ref, e2_ref, acc_ref,
                         g1_ref, g0_ref, j, c_per_step)

    @pl.when(j % 2 == 1)
    def _():
        _peer_dense_step(xn_ref, u_ref, vt_ref, cnt_ref, rank2_ref, e1_ref, e2_ref, acc_ref,
                         g0_ref, g1_ref, j, c_per_step)

    @pl.when(j == n_tiles)
    def _():
        _peer_dense_step(xn_ref, u_ref, vt_ref, cnt_ref, rank2_ref, e1_ref, e2_ref, acc_ref,
                         g1_ref, None, j, c_per_step)
        o_ref[...] = res_ref[...] + acc_ref[...].T


def _peer_dense_step(xn_ref, u_ref, vt_ref, cnt_ref, rank2_ref, e1_ref, e2_ref, acc_ref, g_ref, g_next_ref,
                     j, c_per_step):
    tn = xn_ref.shape[1]
    bf16_rows = 16
    reps = PEER_N_KEYS // bf16_rows

    def row_tile(ref, row):
        r16 = jnp.broadcast_to(ref[pl.ds(row, 1), :], (bf16_rows, tn)).astype(BF16)
        return jnp.concatenate([r16] * reps, axis=0)

    c0 = jnp.maximum(j - 1, 0) * c_per_step
    up_rows = 4 * PEER_N_KEYS
    blocks = []
    for cc in range(c_per_step):
        if g_next_ref is not None and (cc * PEER_N_KEYS) % up_rows == 0:
            rs = slice(cc * PEER_N_KEYS, cc * PEER_N_KEYS + up_rows)
            hT = jnp.dot(u_ref[rs, :], xn_ref[...], preferred_element_type=F32)
            g_next_ref[rs, :] = _gelu_sigmoid(hT.astype(BF16))
        c = c0 + cc
        w = None
        for h in range(PEER_HEADS):
            row = h * PEER_N_KEYS + c
            n_row = row_tile(cnt_ref, row)
            e1_row = row_tile(e1_ref, row)
            sl = slice(h * PEER_N_KEYS, (h + 1) * PEER_N_KEYS)
            term = jnp.where(rank2_ref[sl, :] < n_row, e2_ref[sl, :], 0.0) * e1_row
            w = term if w is None else w + term
        blocks.append(w * g_ref[cc * PEER_N_KEYS:(cc + 1) * PEER_N_KEYS, :])
    aT = jnp.concatenate(blocks, axis=0)
    acc_ref[...] += jnp.dot(vt_ref[...], aT, preferred_element_type=F32)


def peer_dense(xn, u, vt, cnt, rank2, e1, e2, res, *, tn=512, te=2048):
    D, T = xn.shape
    E = u.shape[0]
    tn = min(tn, T)
    rows = PEER_HEADS * PEER_N_KEYS
    sel_spec = pl.BlockSpec((rows, tn), lambda i, j: (0, i))
    n_tiles = E // te
    assert n_tiles % 2 == 0
    return pl.pallas_call(
        functools.partial(_peer_dense_kernel, c_per_step=te // PEER_N_KEYS),
        grid=(T // tn, n_tiles + 1),
        in_specs=[pl.BlockSpec((D, tn), lambda i, j: (0, i)),
                  pl.BlockSpec((te, D), lambda i, j: (jnp.minimum(j, n_tiles - 1), 0)),
                  pl.BlockSpec((D, te), lambda i, j: (0, jnp.maximum(j - 1, 0))),
                  sel_spec, sel_spec, sel_spec, sel_spec,
                  pl.BlockSpec((tn, D), lambda i, j: (i, 0))],
        out_specs=pl.BlockSpec((tn, D), lambda i, j: (i, 0)),
        out_shape=jax.ShapeDtypeStruct((T, D), F32),
        scratch_shapes=[pltpu.VMEM((D, tn), F32), pltpu.VMEM((te, tn), BF16), pltpu.VMEM((te, tn), BF16)],
        compiler_params=_cparams(("parallel", "arbitrary")),
    )(xn, u, vt, cnt, rank2, e1, e2, res)


def peer_layer(h, gamma, w_q, sub_keys, u, v):
    q, xn = norm_matmul(h, gamma, w_q.astype(BF16), emit_xn=True)
    keys = sub_keys.reshape(2 * PEER_HEADS, PEER_N_KEYS, PEER_HALF_DIM).astype(BF16)
    cnt, rank2, e1, e2 = peer_select(q, keys)
    return peer_dense(xn, u.astype(BF16), v.T.astype(BF16), cnt, rank2, e1, e2, h)


def _cumsum_aug_kernel(lf_ref, tri_ref, place_q_ref, place_k_ref, ones_q_ref, ones_k_ref,
                       qa_ref, ka_ref, c_ref, carry_ref):
    @pl.when(pl.program_id(1) == 0)
    def _():
        carry_ref[...] = jnp.zeros_like(carry_ref)

    lf = lf_ref[0]
    c = jnp.dot(tri_ref[...], lf, preferred_element_type=F32, precision=lax.Precision.HIGHEST) + carry_ref[...]
    carry_ref[...] = c[-1:, :]
    c = c * LOG2E
    c_ref[0] = c
    hi = c.astype(BF16)
    r1 = c - hi.astype(F32)
    mid = r1.astype(BF16)
    lo = (r1 - mid.astype(F32)).astype(BF16)
    nh = N_HEADS
    lane = lax.broadcasted_iota(jnp.int32, c.shape, 1)
    parts = jnp.where(lane < nh, hi.astype(F32),
                      jnp.where(lane < 2 * nh, pltpu.roll(mid.astype(F32), nh, axis=1),
                                pltpu.roll(lo.astype(F32), 2 * nh, axis=1)))
    parts = jnp.where(lane < 3 * nh, parts, 0.0).astype(BF16)
    qa_ref[0] = (jnp.dot(parts, place_q_ref[...], preferred_element_type=F32) + ones_q_ref[...]).astype(BF16)
    ka_ref[0] = (jnp.dot(parts, place_k_ref[...], preferred_element_type=F32) + ones_k_ref[...]).astype(BF16)


def fox_bias_operands(logf, *, tc=256):
    B, S, _ = logf.shape
    nh = N_HEADS
    tri = jnp.asarray(np.tril(np.ones((tc, tc), np.float32)))
    pq = np.zeros((LANES, nh * LANES), np.float32)
    pk = np.zeros((LANES, nh * LANES), np.float32)
    oq = np.zeros((1, nh * LANES), np.float32)
    ok = np.zeros((1, nh * LANES), np.float32)
    for h in range(nh):
        for part in range(3):
            pq[part * nh + h, h * LANES + part] = 1.0
            pk[part * nh + h, h * LANES + 3 + part] = -1.0
            oq[0, h * LANES + 3 + part] = 1.0
            ok[0, h * LANES + part] = 1.0
    const = lambda a: pl.BlockSpec(a.shape, lambda b, i: (0,) * a.ndim)
    pq, pk, oq, ok = jnp.asarray(pq, BF16), jnp.asarray(pk, BF16), jnp.asarray(oq), jnp.asarray(ok)
    out = jax.ShapeDtypeStruct((B, S, nh * LANES), BF16)
    return pl.pallas_call(
        _cumsum_aug_kernel,
        grid=(B, S // tc),
        in_specs=[pl.BlockSpec((1, tc, LANES), lambda b, i: (b, i, 0)),
                  const(tri), const(pq), const(pk), const(oq), const(ok)],
        out_specs=[pl.BlockSpec((1, tc, nh * LANES), lambda b, i: (b, i, 0))] * 2
        + [pl.BlockSpec((1, tc, LANES), lambda b, i: (b, i, 0))],
        out_shape=[out, out, jax.ShapeDtypeStruct((B, S, LANES), F32)],
        scratch_shapes=[pltpu.VMEM((1, LANES), F32)],
        compiler_params=_cparams(("parallel", "arbitrary")),
    )(logf, tri, pq, pk, oq, ok)


def _fox_attn_kernel(first_ref, q_ref, qa_ref, kv_ref, ka_ref, o_ref, *, tq, tk, heads_per_step):
    qi = pl.program_id(2)
    t0 = qi * tq
    n_full = t0 // tk
    j_first = first_ref[(pl.program_id(0) * pl.num_programs(1) + pl.program_id(1)) * pl.num_programs(2) + qi]
    n_diag = tq // tk
    lanes = [slice(hh * LANES, (hh + 1) * LANES) for hh in range(heads_per_step)]
    qs = [jnp.concatenate([q_ref[0, :, lsl], qa_ref[0, :, lsl]], axis=1) for lsl in lanes]

    def step(j, carry, masked):
        rows = pl.ds(pl.multiple_of(j * tk, tk), tk)
        klane = lax.broadcasted_iota(jnp.int32, (tk, LANES), 1)
        new = []
        for lsl, q, (m, acc) in zip(lanes, qs, carry):
            kv = kv_ref[0, rows, lsl]
            kk = jnp.concatenate([kv, ka_ref[0, rows, lsl]], axis=1)
            ones_v = jnp.where(klane < HEAD_DIM, 1.0, kv).astype(BF16)
            s = lax.dot_general(q, kk, _NT, preferred_element_type=F32)
            if masked:
                qpos = t0 + lax.broadcasted_iota(jnp.int32, (tq, 1), 0)
                kpos = j * tk + lax.broadcasted_iota(jnp.int32, (1, tk), 1)
                s = jnp.where(kpos <= qpos, s, NEG_INF)
            m_new = jnp.maximum(m, jnp.max(s, axis=1, keepdims=True))
            p = jnp.exp2(s - m_new).astype(BF16)
            acc = jnp.exp2(m - m_new) * acc + jnp.dot(p, ones_v, preferred_element_type=F32)
            new.append((m_new, acc))
        return tuple(new)

    init = (jnp.full((tq, 1), NEG_INF, F32), jnp.zeros((tq, LANES), F32))
    n_pairs = (n_full - j_first) // 2
    carry = lax.fori_loop(
        0, n_pairs, lambda i, c: step(j_first + 2 * i + 1, step(j_first + 2 * i, c, False), False),
        (init,) * heads_per_step)
    carry = lax.fori_loop(j_first + 2 * n_pairs, n_full, functools.partial(step, masked=False), carry)
    for d in range(n_diag):
        carry = step(n_full + d, carry, True)
    outs = [acc / acc[:, 0:1] for (_, acc) in carry]
    lane = lax.broadcasted_iota(jnp.int32, (tq, LANES), 1)
    blocks = []
    for pair in range(heads_per_step // 2):
        a, b = outs[2 * pair], outs[2 * pair + 1]
        blocks.append(jnp.where(lane < HEAD_DIM, pltpu.roll(a, HEAD_DIM, axis=1), b))
    o_ref[0] = jnp.concatenate(blocks, axis=1).astype(o_ref.dtype) if len(blocks) > 1 else blocks[0].astype(o_ref.dtype)


def _head_slots(w, n_heads, second=None):
    D = w.shape[0]
    a = w.reshape(D, n_heads, HEAD_DIM)
    b = jnp.zeros_like(a) if second is None else second.reshape(D, n_heads, HEAD_DIM)
    return jnp.concatenate([a, b], axis=-1).reshape(D, n_heads * LANES)


def fox_layer(h, gamma, w_in, f_bias, w_out, B, S):
    aw = ATTN_WIDTH
    wq = _head_slots(w_in[:, :aw] * (HEAD_DIM ** -0.5 * LOG2E), N_HEADS)
    wkv = _head_slots(w_in[:, aw:2 * aw], N_HEADS, w_in[:, 2 * aw:3 * aw])
    w_main = jnp.concatenate([wq, wkv], axis=1).astype(BF16)
    wf = jnp.pad(w_in[:, 3 * aw:], ((0, 0), (0, LANES - N_HEADS))).astype(BF16)
    bf = jnp.pad(f_bias.astype(F32), (0, LANES - N_HEADS))
    qkv = norm_matmul(h, gamma, w_main)
    logf = norm_matmul(h, gamma, wf, bias=bf, act="log_sigmoid", out_dtype=F32)
    qa, ka, c2 = fox_bias_operands(logf.reshape(B, S, LANES))
    qkv = qkv.reshape(B, S, 2 * N_HEADS * LANES)
    o = fox_attention(qkv, qa, ka, c2)
    return matmul_residual(o.reshape(B * S, aw), w_out.astype(BF16), h)


FOX_NEGLIGIBLE_LOG2 = 160.0


def _norm_maxima_kernel(x_ref, ind_ref, o_ref):
    x = x_ref[0].astype(F32)
    ss = jnp.dot((x * x).astype(BF16), ind_ref[...], preferred_element_type=F32)
    o_ref[0, 0] = jnp.broadcast_to(jnp.max(ss, axis=0, keepdims=True), o_ref.shape[2:])


def fox_norm_maxima(qkv, tile):
    B, S, W = qkv.shape
    n = S // tile
    col = np.arange(W)
    slot, lane = col // LANES, col % LANES
    used = (slot < N_HEADS) | (lane < HEAD_DIM)
    ind = jnp.asarray((slot[:, None] == np.arange(LANES)[None, :]) & used[:, None], BF16)
    rows = min(256, tile)
    ss = pl.pallas_call(
        _norm_maxima_kernel,
        grid=(B, S // rows),
        in_specs=[pl.BlockSpec((1, rows, W), lambda b, i: (b, i, 0)),
                  pl.BlockSpec(ind.shape, lambda b, i: (0, 0))],
        out_specs=pl.BlockSpec((1, 1, 8, LANES), lambda b, i: (b, i, 0, 0)),
        out_shape=jax.ShapeDtypeStruct((B, S // rows, 8, LANES), F32),
        compiler_params=_cparams(("parallel", "parallel")),
    )(qkv, ind)
    ss = jnp.max(ss[:, :, 0, :2 * N_HEADS].reshape(B, n, tile // rows, 2 * N_HEADS), axis=2)
    return jnp.sqrt(ss * 1.01)


def fox_first_chunk(qkv, c2, tile, heads_per_step):
    B, S, _ = qkv.shape
    n = S // tile
    norms = fox_norm_maxima(qkv, tile)
    qmax, kmax = norms[..., :N_HEADS], norms[..., N_HEADS:]
    c = c2[..., :N_HEADS].reshape(B, n, tile, N_HEADS)
    cmax, cmin = jnp.max(c, axis=2), jnp.min(c, axis=2)
    upper = qmax[:, :, None] * kmax[:, None, :] + cmax[:, :, None] - cmin[:, None, :]
    own = -(qmax * kmax)
    earlier = jnp.arange(n)[None, :, None, None] > jnp.arange(n)[None, None, :, None]
    skip = (upper < own[:, :, None] - FOX_NEGLIGIBLE_LOG2) & earlier
    first = jnp.sum(jnp.cumprod(skip.astype(jnp.int32), axis=2), axis=2)
    first = jnp.min(first.reshape(B, n, N_HEADS // heads_per_step, heads_per_step), axis=-1)
    return jnp.transpose(first, (0, 2, 1)).reshape(-1).astype(jnp.int32)


def fox_attention(qkv, qa, ka, c2, *, tq=1024, tk=1024, heads_per_step=2):
    B, S, _ = qkv.shape
    tq = tk = min(tq, S)
    hs = heads_per_step
    wq = hs * LANES
    n_qblk = N_HEADS // hs
    first = fox_first_chunk(qkv, c2, tk, hs)
    grid_spec = pltpu.PrefetchScalarGridSpec(
        num_scalar_prefetch=1,
        grid=(B, n_qblk, S // tq),
        in_specs=[pl.BlockSpec((1, tq, wq), lambda b, h, i, first: (b, i, h)),
                  pl.BlockSpec((1, tq, wq), lambda b, h, i, first: (b, i, h)),
                  pl.BlockSpec((1, S, wq), lambda b, h, i, first: (b, 0, n_qblk + h)),
                  pl.BlockSpec((1, S, wq), lambda b, h, i, first: (b, 0, h))],
        out_specs=pl.BlockSpec((1, tq, hs * HEAD_DIM), lambda b, h, i, first: (b, i, h)))
    return pl.pallas_call(
        functools.partial(_fox_attn_kernel, tq=tq, tk=tk, heads_per_step=hs),
        grid_spec=grid_spec,
        out_shape=jax.ShapeDtypeStruct((B, S, ATTN_WIDTH), BF16),
        compiler_params=_cparams(("parallel", "parallel", "arbitrary")),
    )(first, qkv, qa, qkv, ka)


def _rot_half_cols(w):
    D = w.shape[0]
    a = w.reshape(D, -1, HEAD_DIM)
    half = HEAD_DIM // 2
    return jnp.concatenate([-a[..., half:], a[..., :half]], axis=-1).reshape(w.shape)


def _rope_tables(S):
    half = HEAD_DIM // 2
    inv_freq = ROPE_THETA ** (-jnp.arange(half, dtype=F32) / half)
    ang = jnp.arange(S, dtype=F32)[:, None] * inv_freq[None, :]
    c, s = jnp.cos(ang), jnp.sin(ang)
    c2, s2 = jnp.concatenate([c, c], axis=1), jnp.concatenate([s, s], axis=1)
    cos = jnp.stack([jnp.concatenate([c2, jnp.ones_like(c2)], axis=1), jnp.concatenate([c2, c2], axis=1)])
    sin = jnp.stack([jnp.concatenate([s2, jnp.zeros_like(s2)], axis=1), jnp.concatenate([s2, s2], axis=1)])
    return cos, sin


def _compress_kernel(x_ref, pea_ref, peb_ref, wa_ref, wb_ref, w2_ref, o_ref, pa_ref, pb0_ref, *, n_rows):
    u = pl.program_id(1)
    x = x_ref[0].astype(F32)
    pa = jnp.dot((x + pea_ref[...]).astype(BF16), wa_ref[...], preferred_element_type=F32)
    pb = jnp.dot((x + peb_ref[...]).astype(BF16), wb_ref[...], preferred_element_type=F32)

    def emit(slab, hid):
        y = jnp.dot(_gelu_tanh(hid).astype(BF16), w2_ref[...], preferred_element_type=F32)
        o_ref[0, pl.ds(pl.multiple_of(slab * n_rows, n_rows), n_rows), :] = y

    @pl.when(u == 0)
    def _():
        pb0_ref[...] = pb

    @pl.when(u > 0)
    def _():
        emit(u - 1, pa_ref[...] + pb)

    @pl.when(u == 3)
    def _():
        emit(3, pa + pltpu.roll(pb0_ref[...], n_rows - 1, axis=0))

    pa_ref[...] = pa


def nsa_compress(src, pe, w1, w2):
    B, S, W = src.shape
    G = NSA_GROUPS
    n_rows = S // 64
    half = CMP_BLOCK // 2
    cw = half * W
    xv = src.reshape(B, n_rows, 4 * cw)
    pe_flat = jnp.transpose(pe, (1, 0, 2)).reshape(CMP_BLOCK, W).astype(F32)
    pea, peb = pe_flat[:half].reshape(1, cw), pe_flat[half:].reshape(1, cw)
    eye = jnp.eye(G, dtype=F32)
    wfull = jnp.einsum('gldh,gk->lkdgh', w1.astype(F32), eye).reshape(CMP_BLOCK, W, G * CMP_HIDDEN)
    wa = wfull[:half].reshape(cw, G * CMP_HIDDEN).astype(BF16)
    wb = wfull[half:].reshape(cw, G * CMP_HIDDEN).astype(BF16)
    w2bd = jnp.einsum('ghd,gk->ghkd', w2.astype(F32), eye).reshape(G * CMP_HIDDEN, W).astype(BF16)
    const = lambda a: pl.BlockSpec(a.shape, lambda b, u: (0,) * a.ndim)
    return pl.pallas_call(
        functools.partial(_compress_kernel, n_rows=n_rows),
        grid=(B, 4),
        in_specs=[pl.BlockSpec((1, n_rows, cw), lambda b, u: (b, 0, u)),
                  const(pea), const(peb), const(wa), const(wb), const(w2bd)],
        out_specs=pl.BlockSpec((1, 4 * n_rows, W), lambda b, u: (b, 0, 0)),
        out_shape=jax.ShapeDtypeStruct((B, 4 * n_rows, W), F32),
        scratch_shapes=[pltpu.VMEM((n_rows, G * CMP_HIDDEN), F32), pltpu.VMEM((n_rows, G * CMP_HIDDEN), F32)],
        compiler_params=_cparams(("parallel", "arbitrary")),
    )(xv, pea, peb, wa, wb, w2bd)


def _nsa_attn_kernel(q_ref, kvs_ref, kvw_ref, kvc_ref, kvct_ref, gate_ref, bmat_ref, pmat_ref, o_ref,
                     *, tq, ts, tk, seq):
    R = NSA_Q_PER_GROUP
    n_slc = seq // SLC_BLOCK
    n_sel = min(SLC_TOPK, n_slc)
    assert ts & (ts - 1) == 0 and n_slc & (n_slc - 1) == 0 and tk % tq == 0 and tq % ts == 0
    qi = pl.program_id(2)
    t0 = qi * tq

    local = [_nsa_local_branches(q_ref, kvw_ref, kvc_ref, kvct_ref, t0 + i * ts, i * ts, ts, n_slc, n_sel)
             for i in range(tq // ts)]
    o_c = jnp.concatenate([o[0][r * ts:(r + 1) * ts] for r in range(R) for o in local], axis=0)
    o_w = jnp.concatenate([o[1][r * ts:(r + 1) * ts] for r in range(R) for o in local], axis=0)
    sel_bias = jnp.concatenate([o[2] for o in local], axis=0)
    qs = jnp.concatenate([q_ref[0, :, r * LANES:(r + 1) * LANES] for r in range(R)], axis=0)
    _nsa_selected_and_combine(qs, o_c, o_w, sel_bias, kvs_ref, gate_ref, bmat_ref, pmat_ref, o_ref,
                              t0=t0, tq=tq, tk=tk, seq=seq)


def _nsa_local_branches(q_ref, kvw_ref, kvc_ref, kvct_ref, t0, row0, tq, n_slc, n_sel):
    R = NSA_Q_PER_GROUP
    M = R * tq
    qs = jnp.concatenate([q_ref[0, row0:row0 + tq, r * LANES:(r + 1) * LANES] for r in range(R)], axis=0)

    wlen = WINDOW + tq
    start = jnp.maximum(t0 - WINDOW, 0)
    kvw = kvw_ref[0, pl.ds(pl.multiple_of(start, tq), wlen), :]
    s_w = lax.dot_general(qs, kvw, _NT, preferred_element_type=F32)
    qpos = t0 + (lax.broadcasted_iota(jnp.int32, (M, 1), 0) & (tq - 1))
    kpos = start + lax.broadcasted_iota(jnp.int32, (1, wlen), 1)
    s_w = jnp.where(kpos <= qpos, jnp.where(kpos > qpos - WINDOW, s_w, NEG_INF), NEG_INF)
    p_w = jnp.exp2(s_w - jnp.max(s_w, axis=1, keepdims=True)).astype(BF16)
    wlane = lax.broadcasted_iota(jnp.int32, (wlen, LANES), 1)
    acc_w = jnp.dot(p_w, jnp.where(wlane < HEAD_DIM, 1.0, kvw).astype(BF16), preferred_element_type=F32)
    o_w = acc_w / acc_w[:, 0:1]

    n_groups, gs = _nsa_row_groups(n_slc)
    o_c, sel_bias = _nsa_compressed_and_select(qs, kvc_ref, kvct_ref, t0, tq, n_slc, n_sel, gs, n_groups)
    return o_c, o_w, sel_bias


def _nsa_row_groups(n_slc):
    n_groups = max(1, min(4, n_slc // 8))
    assert n_slc % n_groups == 0 and (n_slc // n_groups) % 8 == 0
    return n_groups, n_slc // n_groups


def _nsa_compressed_and_select(qs, kvc_ref, kvct_ref, t0, tq, n_slc, n_sel, gs, n_groups):
    R = NSA_Q_PER_GROUP
    M = R * tq
    n_rows, n_j = n_groups * 4 * gs, n_groups * gs
    log_gs = gs.bit_length() - 1
    sT = lax.dot_general(kvc_ref[0, 0, 0:n_rows, :], qs, _NT, preferred_element_type=F32)
    rowc = lax.broadcasted_iota(jnp.int32, (n_rows, 1), 0)
    blk_j = ((rowc >> (log_gs + 2)) << log_gs) + (rowc & (gs - 1))
    blk_u = (rowc >> log_gs) & 3
    cmp_end = blk_j * SLC_BLOCK + blk_u * CMP_STRIDE + (CMP_BLOCK - 1)
    tcol = t0 + (lax.broadcasted_iota(jnp.int32, (1, M), 1) & (tq - 1))
    sm = jnp.where(cmp_end <= tcol, sT, NEG_INF)
    mx = jnp.max(sm, axis=0, keepdims=True)
    e = jnp.exp2(sm - mx)
    inv = jnp.where(mx > 0.5 * NEG_INF, 1.0 / jnp.sum(e, axis=0, keepdims=True), 0.0)
    pT = e * inv
    o_c = jnp.dot(kvct_ref[0, 0, :, 0:n_rows], pT.astype(BF16), preferred_element_type=F32).T

    psum = pT[:, 0:tq]
    for r in range(1, R):
        psum = psum + pT[:, r * tq:(r + 1) * tq]
    p0, p1, p2, p3 = (jnp.concatenate([psum[(4 * g + u) * gs:(4 * g + u + 1) * gs, :] for g in range(n_groups)],
                                      axis=0) for u in range(4))
    jrow = lax.broadcasted_iota(jnp.int32, (n_j, tq), 0)
    p3_prev = jnp.where(jrow == 0, 0.0, pltpu.roll(p3, 1, axis=0))
    imp = p0 + p1 + p2 + 0.5 * p3 + 0.5 * p3_prev
    cur = (t0 + lax.broadcasted_iota(jnp.int32, (n_j, tq), 1)) >> (SLC_BLOCK.bit_length() - 1)
    forced = (jrow == 0) | (jrow == cur) | (jrow == cur - 1)
    vals = jnp.where(forced, -jnp.inf, jnp.where(jrow <= cur, imp, -jnp.inf))
    sel0 = jnp.where(forced, 1.0, 0.0)
    n_free = n_sel - 3

    picked = -2.0 ** 100

    def pick(exact_ties):
        v = vals
        for _ in range(n_free):
            m = jnp.max(v, axis=0, keepdims=True)
            if exact_ties:
                hit = jrow == jnp.min(jnp.where(v == m, jrow, n_slc), axis=0, keepdims=True)
            else:
                hit = v == jnp.where(m < 0.0, jnp.nan, m)
            v = jnp.where(hit, picked, v)
        return jnp.where(v == picked, 1.0, sel0)

    sel_fast = pick(False)
    n_cand = jnp.sum(jnp.where(vals > -jnp.inf, 1.0, 0.0), axis=0, keepdims=True)
    n_picked = jnp.sum(sel_fast - sel0, axis=0, keepdims=True)
    unique = jnp.min(jnp.where(n_picked == jnp.minimum(n_cand, float(n_free)), 1.0, 0.0)) > 0.5
    sel = lax.cond(unique, lambda: sel_fast, lambda: pick(True))
    if n_j < n_slc:
        sel = jnp.concatenate([sel, jnp.zeros((n_slc - n_j, tq), F32)], axis=0)
    sel_bias = ((sel.T - 1.0) * (-NEG_INF)).astype(BF16)
    return o_c, sel_bias


def _nsa_selected_and_combine(qs, o_c, o_w, sel_bias, kvs_ref, gate_ref, bmat_ref, pmat_ref, o_ref,
                              *, t0, tq, tk, seq):
    R = NSA_Q_PER_GROUP
    n_slc = seq // SLC_BLOCK
    blocks_per_chunk = tk // SLC_BLOCK
    n_parts = 2
    hp = R // n_parts
    mp = hp * tq
    qparts = [qs[i * mp:(i + 1) * mp] for i in range(n_parts)]
    qrow = t0 + (lax.broadcasted_iota(jnp.int32, (mp, 1), 0) & (tq - 1))
    kcol = lax.broadcasted_iota(jnp.int32, (1, tk), 1)
    n_chunks = seq // tk
    j_last = t0 // tk

    klane = lax.broadcasted_iota(jnp.int32, (tk, LANES), 1)

    def slc_step(j, carry, masked):
        kv = kvs_ref[0, pl.ds(pl.multiple_of(j * tk, tk), tk), :]
        k_sel = jnp.where(klane < HEAD_DIM, kv, bmat_ref[...]).astype(BF16)
        ones_v = jnp.where(klane < HEAD_DIM, 1.0, kv).astype(BF16)
        off = pl.multiple_of(blocks_per_chunk * (n_chunks - 1 - j), blocks_per_chunk)
        place = pmat_ref[pl.ds(off, n_slc), :].astype(BF16)
        q_bias = jnp.dot(sel_bias, place, preferred_element_type=F32).astype(BF16)
        q_bias = jnp.concatenate([q_bias] * hp, axis=0)
        new = []
        for qp, (m, acc) in zip(qparts, carry):
            s = lax.dot_general(qp + q_bias, k_sel, _NT, preferred_element_type=F32)
            if masked:
                s = jnp.where(j * tk + kcol <= qrow, s, NEG_INF)
            m_new = jnp.maximum(m, jnp.max(s, axis=1, keepdims=True))
            p = jnp.exp2(s - m_new).astype(BF16)
            acc = jnp.exp2(m - m_new) * acc + jnp.dot(p, ones_v, preferred_element_type=F32)
            new.append((m_new, acc))
        return tuple(new)

    init = (jnp.full((mp, 1), NEG_INF, F32), jnp.zeros((mp, LANES), F32))
    n_pairs = j_last // 2
    carry = lax.fori_loop(0, n_pairs, lambda i, c: slc_step(2 * i + 1, slc_step(2 * i, c, False), False),
                          (init,) * n_parts)
    carry = lax.fori_loop(2 * n_pairs, j_last, functools.partial(slc_step, masked=False), carry)
    carry = slc_step(j_last, carry, True)
    o_s = jnp.concatenate([acc / acc[:, 0:1] for (_, acc) in carry], axis=0)

    gates = gate_ref[0]
    lane = lax.broadcasted_iota(jnp.int32, (tq, LANES), 1)
    comb = []
    for r in range(R):
        rs = slice(r * tq, (r + 1) * tq)
        comb.append(gates[:, r:r + 1] * o_c[rs] + gates[:, R + r:R + r + 1] * o_s[rs]
                    + gates[:, 2 * R + r:2 * R + r + 1] * o_w[rs])
    out = [jnp.where(lane < HEAD_DIM, pltpu.roll(comb[2 * i], HEAD_DIM, axis=1), comb[2 * i + 1])
           for i in range(R // 2)]
    o_ref[0] = jnp.concatenate(out, axis=1).astype(o_ref.dtype)


def nsa_attention(qkv, kvc, kvct, gates, *, tq=512, ts=256, tk=1024):
    B, S, _ = qkv.shape
    G, R = NSA_GROUPS, NSA_Q_PER_GROUP
    tk = min(tk, S)
    n_slc = S // SLC_BLOCK
    n_cmp = kvc.shape[2]
    bpc = tk // SLC_BLOCK
    assert bpc <= LANES - HEAD_DIM
    off = bpc * (S // tk - 1)
    lane = np.arange(LANES)[None, :]
    bmat = jnp.asarray(lane - HEAD_DIM == np.arange(tk)[:, None] // SLC_BLOCK, BF16)
    pmat = jnp.asarray((np.arange(n_slc + off)[:, None] - off == lane - HEAD_DIM) & (lane >= HEAD_DIM)
                       & (lane < HEAD_DIM + bpc), F32)
    slc_blk0, win_blk0 = N_HEADS, N_HEADS + G
    return pl.pallas_call(
        functools.partial(_nsa_attn_kernel, tq=tq, ts=ts, tk=tk, seq=S),
        grid=(B, G, S // tq),
        in_specs=[pl.BlockSpec((1, tq, R * LANES), lambda b, g, i: (b, i, g)),
                  pl.BlockSpec((1, S, LANES), lambda b, g, i: (b, 0, slc_blk0 + g)),
                  pl.BlockSpec((1, S, LANES), lambda b, g, i: (b, 0, win_blk0 + g)),
                  pl.BlockSpec((1, 1, n_cmp, LANES), lambda b, g, i: (b, g, 0, 0)),
                  pl.BlockSpec((1, 1, LANES, n_cmp), lambda b, g, i: (b, g, 0, 0)),
                  pl.BlockSpec((1, tq, LANES), lambda b, g, i: (b, i, g)),
                  pl.BlockSpec(bmat.shape, lambda b, g, i: (0, 0)),
                  pl.BlockSpec(pmat.shape, lambda b, g, i: (0, 0))],
        out_specs=pl.BlockSpec((1, tq, R * HEAD_DIM), lambda b, g, i: (b, i, g)),
        out_shape=jax.ShapeDtypeStruct((B, S, ATTN_WIDTH), BF16),
        compiler_params=_cparams(("parallel", "parallel", "arbitrary")),
    )(qkv, qkv, qkv, kvc, kvct, gates, bmat, pmat)


def nsa_layer(h, gamma, w_in, pe_k, w1_k, w2_k, pe_v, w1_v, w2_v, w_out, B, S):
    G, R, hd, aw = NSA_GROUPS, NSA_Q_PER_GROUP, HEAD_DIM, ATTN_WIDTH
    kvd = G * hd
    sec = lambda i: w_in[:, aw + i * kvd: aw + (i + 1) * kvd]
    wq = w_in[:, :aw] * (hd ** -0.5 * LOG2E)
    wa = jnp.concatenate([_head_slots(wq, N_HEADS), _head_slots(sec(2), G, sec(3)),
                          _head_slots(sec(4), G, sec(5))], axis=1).astype(BF16)
    wb = jnp.concatenate([_rot_half_cols(wq), _rot_half_cols(sec(2)), _rot_half_cols(sec(4))], axis=1).astype(BF16)
    cos, sin = _rope_tables(S)
    qkv = norm_matmul(h, gamma, wa, wb=wb, cos=cos, sin=sin)
    kc_src = norm_matmul(h, gamma, sec(0).astype(BF16), wb=_rot_half_cols(sec(0)).astype(BF16),
                         cos=cos, sin=sin, table_of_tile=lambda j: 1)
    vc_src = norm_matmul(h, gamma, sec(1).astype(BF16))
    wg = w_in[:, aw + 6 * kvd:].reshape(-1, 3, G, R)
    wg = jnp.transpose(wg, (0, 2, 1, 3)).reshape(-1, G, 3 * R)
    wg = jnp.pad(wg, ((0, 0), (0, 0), (0, LANES - 3 * R))).reshape(-1, G * LANES).astype(BF16)
    gates = norm_matmul(h, gamma, wg, act="sigmoid", out_dtype=F32)
    kc = nsa_compress(kc_src.reshape(B, S, kvd), pe_k, w1_k, w2_k)
    vc = nsa_compress(vc_src.reshape(B, S, kvd), pe_v, w1_v, w2_v)
    n_cmp = kc.shape[1]
    kvc = jnp.concatenate([kc.reshape(B, n_cmp, G, hd), vc.reshape(B, n_cmp, G, hd)], axis=-1)
    n_groups, gs = _nsa_row_groups(S // SLC_BLOCK)
    kvc = kvc.reshape(B, 4, n_groups, gs, G, 2 * hd)
    kvc = jnp.transpose(kvc, (0, 4, 2, 1, 3, 5)).reshape(B, G, n_cmp, 2 * hd).astype(BF16)
    kvct = jnp.swapaxes(kvc, 2, 3)
    o = nsa_attention(qkv.reshape(B, S, -1), kvc, kvct, gates.reshape(B, S, G * LANES))
    return matmul_residual(o.reshape(B * S, aw), w_out.astype(BF16), h)


def kernel(x, l0_attn_norm, l0_w_in, l0_cmp_pe_k, l0_cmp_w1_k, l0_cmp_w2_k, l0_cmp_pe_v, l0_cmp_w1_v,
           l0_cmp_w2_v, l0_w_out, l0_ffn_norm, l0_peer_wq, l0_peer_keys, l0_peer_u, l0_peer_v,
           l1_attn_norm, l1_w_in, l1_f_bias, l1_w_out, l1_ffn_norm, l1_peer_wq, l1_peer_keys, l1_peer_u,
           l1_peer_v, final_norm):
    B, S, D = x.shape
    h = x.reshape(B * S, D)
    h = nsa_layer(h, l0_attn_norm, l0_w_in, l0_cmp_pe_k, l0_cmp_w1_k, l0_cmp_w2_k, l0_cmp_pe_v, l0_cmp_w1_v,
                  l0_cmp_w2_v, l0_w_out, B, S)
    h = peer_layer(h, l0_ffn_norm, l0_peer_wq, l0_peer_keys, l0_peer_u, l0_peer_v)
    h = fox_layer(h, l1_attn_norm, l1_w_in, l1_f_bias, l1_w_out, B, S)
    h = peer_layer(h, l1_ffn_norm, l1_peer_wq, l1_peer_keys, l1_peer_u, l1_peer_v)
    return rmsnorm(h, final_norm).reshape(B, S, D)
```

```python
import functools

import numpy as np
import jax
import jax.numpy as jnp
from jax import lax
from jax.experimental import pallas as pl
from jax.experimental.pallas import tpu as pltpu

F32 = jnp.float32
BF16 = jnp.bfloat16

D_MODEL = 1024
N_HEADS = 16
HEAD_DIM = 64
ATTN_WIDTH = N_HEADS * HEAD_DIM
NSA_GROUPS = 4
NSA_Q_PER_GROUP = N_HEADS // NSA_GROUPS
CMP_BLOCK = 32
CMP_STRIDE = 16
CMP_HIDDEN = 2 * HEAD_DIM
SLC_BLOCK = 64
SLC_TOPK = 16
WINDOW = 512
FORCE_SCORE = 1.0e4
ROPE_THETA = 10000.0
PEER_HEADS = 8
PEER_N_KEYS = 128
PEER_TOPK = 16
PEER_HALF_DIM = 128
RMS_EPS = 1e-6
NEG_INF = -1e30
LOG2E = 1.4426950408889634

LANES = 128
VMEM_LIMIT_BYTES = 56 * 1024 * 1024

_NT = (((1,), (1,)), ((), ()))


def _cparams(sem, vmem=VMEM_LIMIT_BYTES, flags=None):
    return pltpu.CompilerParams(dimension_semantics=sem, vmem_limit_bytes=vmem, flags=flags)


def _gelu_tanh(x):
    return 0.5 * x * (1.0 + jnp.tanh(0.7978845608028654 * (x + 0.044715 * (x * x * x))))


def _gelu_sigmoid(x):
    c = -2.0 * 0.7978845608028654 * LOG2E
    t = x * (c + (c * 0.044715) * (x * x))
    return x / (1.0 + jnp.exp2(t))


def _rms_rows(x, g):
    ms = jnp.mean(x * x, axis=-1, keepdims=True)
    return x * lax.rsqrt(ms + RMS_EPS) * g


def _norm_mm_kernel(*refs, act, has_bias, rope, emit_xn):
    it = iter(refs)
    x_ref, g_ref = next(it), next(it)
    wa_ref = next(it)
    wb_ref = next(it) if rope else None
    cos_ref = next(it) if rope else None
    sin_ref = next(it) if rope else None
    b_ref = next(it) if has_bias else None
    o_ref = next(it)
    xo_ref = next(it) if emit_xn else None
    xn_ref = next(it)

    @pl.when(pl.program_id(1) == 0)
    def _():
        xn = _rms_rows(x_ref[...], g_ref[...])
        xn_ref[...] = xn.astype(BF16)
        if emit_xn:
            xo_ref[...] = xn.T.astype(BF16)

    xn = xn_ref[...]
    y = jnp.dot(xn, wa_ref[...], preferred_element_type=F32)
    if rope:
        yb = jnp.dot(xn, wb_ref[...], preferred_element_type=F32)
        cos, sin = cos_ref[...], sin_ref[...]
        packed_b = 2 * yb.shape[1] == y.shape[1]
        for s in range(y.shape[1] // LANES):
            sl = slice(s * LANES, (s + 1) * LANES)
            if packed_b:
                b = yb[:, (s // 2) * LANES:(s // 2 + 1) * LANES]
                b = pltpu.roll(b, HEAD_DIM, axis=1) if s % 2 else b
            else:
                b = yb[:, sl]
            o_ref[:, sl] = (y[:, sl] * cos + b * sin).astype(o_ref.dtype)
        return
    if has_bias:
        y = y + b_ref[...]
    if act == "sigmoid":
        y = jax.nn.sigmoid(y)
    elif act == "log_sigmoid":
        y = jax.nn.log_sigmoid(y)
    o_ref[...] = y.astype(o_ref.dtype)


def norm_matmul(x, gamma, wa, *, wb=None, cos=None, sin=None, table_of_tile=None, bias=None,
                act=None, out_dtype=BF16, emit_xn=False, tm=1024, tn=1024):
    T, D = x.shape
    N = wa.shape[1]
    tm, tn = min(tm, T), min(tn, N)
    assert T % tm == 0 and N % tn == 0 and tn % LANES == 0
    rope = wb is not None
    in_specs = [pl.BlockSpec((tm, D), lambda i, j: (i, 0)),
                pl.BlockSpec((1, D), lambda i, j: (0, 0)),
                pl.BlockSpec((D, tn), lambda i, j: (0, j))]
    args = [x, gamma.reshape(1, D).astype(F32), wa]
    if rope:
        S = cos.shape[1]
        assert S % tm == 0
        n_pos = S // tm
        tmap = table_of_tile if table_of_tile is not None else (lambda j: 0)
        assert wb.shape[1] in (N, N // 2)
        in_specs += [pl.BlockSpec((D, tn * wb.shape[1] // N), lambda i, j: (0, j)),
                     pl.BlockSpec((None, tm, LANES), lambda i, j: (tmap(j), i % n_pos, 0)),
                     pl.BlockSpec((None, tm, LANES), lambda i, j: (tmap(j), i % n_pos, 0))]
        args += [wb, cos, sin]
    if bias is not None:
        in_specs.append(pl.BlockSpec((1, tn), lambda i, j: (0, j)))
        args.append(bias.reshape(1, N).astype(F32))
    out_shape = [jax.ShapeDtypeStruct((T, N), out_dtype)]
    out_specs = [pl.BlockSpec((tm, tn), lambda i, j: (i, j))]
    if emit_xn:
        out_shape.append(jax.ShapeDtypeStruct((D, T), BF16))
        out_specs.append(pl.BlockSpec((D, tm), lambda i, j: (0, i)))
    res = pl.pallas_call(
        functools.partial(_norm_mm_kernel, act=act, has_bias=bias is not None, rope=rope, emit_xn=emit_xn),
        grid=(T // tm, N // tn),
        in_specs=in_specs,
        out_specs=out_specs,
        out_shape=out_shape,
        scratch_shapes=[pltpu.VMEM((tm, D), BF16)],
        compiler_params=_cparams(("parallel", "arbitrary")),
    )(*args)
    return res if emit_xn else res[0]


def _mm_res_kernel(a_ref, w_ref, r_ref, o_ref):
    o_ref[...] = r_ref[...] + jnp.dot(a_ref[...], w_ref[...], preferred_element_type=F32)


def matmul_residual(a, w, res, *, tm=1024, tn=1024):
    T, K = a.shape
    N = w.shape[1]
    tm, tn = min(tm, T), min(tn, N)
    assert T % tm == 0 and N % tn == 0
    return pl.pallas_call(
        _mm_res_kernel,
        grid=(T // tm, N // tn),
        in_specs=[pl.BlockSpec((tm, K), lambda i, j: (i, 0)),
                  pl.BlockSpec((K, tn), lambda i, j: (0, j)),
                  pl.BlockSpec((tm, tn), lambda i, j: (i, j))],
        out_specs=pl.BlockSpec((tm, tn), lambda i, j: (i, j)),
        out_shape=jax.ShapeDtypeStruct((T, N), F32),
        compiler_params=_cparams(("parallel", "arbitrary")),
    )(a, w, res)


def _rmsnorm_kernel(x_ref, g_ref, o_ref):
    o_ref[...] = _rms_rows(x_ref[...], g_ref[...])


def rmsnorm(x, gamma, *, tm=1024):
    T, D = x.shape
    tm = min(tm, T)
    return pl.pallas_call(
        _rmsnorm_kernel,
        grid=(T // tm,),
        in_specs=[pl.BlockSpec((tm, D), lambda i: (i, 0)), pl.BlockSpec((1, D), lambda i: (0, 0))],
        out_specs=pl.BlockSpec((tm, D), lambda i: (i, 0)),
        out_shape=jax.ShapeDtypeStruct((T, D), F32),
        compiler_params=_cparams(("parallel",)),
    )(x, gamma.reshape(1, D).astype(F32))


def _peer_cand_tables(tn):
    fidx, vmask = [], []
    for k2 in range(16):
        fidx.append(k2); vmask.append(0.0)
    for k1 in range(1, 8):
        lim = PEER_TOPK // (k1 + 1)
        for k2 in range(8):
            fidx.append(k1 * 16 + k2); vmask.append(0.0 if k2 < lim else -np.inf)
    for k1 in range(8, 16):
        fidx.append(k1 * 16); vmask.append(0.0)
    fidx = np.broadcast_to(np.asarray(fidx, np.int32)[:, None], (80, tn))
    vmask = np.broadcast_to(np.asarray(vmask, np.float32)[:, None], (80, tn))
    return jnp.asarray(fidx), jnp.asarray(vmask)


def _top16_rows(s, exact_ties):
    n, tn = s.shape
    rows = lax.broadcasted_iota(jnp.int32, (n, tn), 0)
    rows16 = lax.broadcasted_iota(jnp.int32, (PEER_TOPK, tn), 0)
    tops = jnp.zeros((PEER_TOPK, tn), F32)
    unit = 2.0 ** 122
    v = s
    for k in range(PEER_TOPK):
        m = jnp.max(v, axis=0, keepdims=True)
        if exact_ties:
            hit = rows == jnp.min(jnp.where(v == m, rows, n), axis=0, keepdims=True)
        else:
            hit = v == m
        v = jnp.where(hit, -(32.0 + k) * unit, v)
        tops = jnp.where(rows16 == k, m, tops)
    was_picked = v <= -32.0 * unit
    rank = jnp.where(was_picked, v * (-1.0 / unit) - 32.0, float(PEER_TOPK))
    n_picked = jnp.sum(jnp.where(was_picked, 1.0, 0.0), axis=0, keepdims=True)
    return tops, rank, n_picked


def _peer_select_head(q_ref, keys_ref, fidx, vmask, exact_ties):
    tops, ranks, es, picked = [], [], [], []
    for p in range(2):
        q = q_ref[:, p * PEER_HALF_DIM:(p + 1) * PEER_HALF_DIM]
        s = lax.dot_general(keys_ref[p], q, _NT, preferred_element_type=F32)
        t, r, n_picked = _top16_rows(s, exact_ties)
        tops.append(t); ranks.append(r)
        es.append(jnp.exp(s - t[0:1, :]))
        picked.append(n_picked)
    ts1, ts2 = tops
    pieces = [ts1[0:1, :] + ts2]
    for k1 in range(1, 8):
        pieces.append(ts1[k1:k1 + 1, :] + ts2[0:8, :])
    pieces.append(ts1[8:16, :] + ts2[0:1, :])
    cand0 = jnp.concatenate(pieces, axis=0) + vmask
    cand = cand0
    for _ in range(PEER_TOPK):
        m = jnp.max(cand, axis=0, keepdims=True)
        if exact_ties:
            hit = fidx == jnp.min(jnp.where(cand == m, fidx, 4096), axis=0, keepdims=True)
        else:
            hit = cand == m
        cand = jnp.where(hit, -jnp.inf, cand)
    taken = jnp.logical_and(cand == -jnp.inf, vmask == 0.0)
    takenf = taken.astype(F32)
    picked.append(jnp.sum(takenf, axis=0, keepdims=True))
    unique = jnp.min(jnp.where((picked[0] == PEER_TOPK) & (picked[1] == PEER_TOPK) & (picked[2] == PEER_TOPK),
                               1.0, 0.0)) > 0.5
    best = ts1[0:1, :] + ts2[0:1, :]
    z = jnp.sum(jnp.where(taken, jnp.exp(cand0 - best), 0.0), axis=0, keepdims=True)
    counts = [jnp.sum(takenf[0:16, :], axis=0, keepdims=True)]
    for k1 in range(1, 8):
        counts.append(jnp.sum(takenf[16 + 8 * (k1 - 1):16 + 8 * k1, :], axis=0, keepdims=True))
    tail = takenf[72:80, :]
    cnt = jnp.zeros_like(ranks[0])
    for k1 in range(PEER_TOPK):
        nk = counts[k1] if k1 < 8 else tail[k1 - 8:k1 - 7, :]
        cnt = jnp.where(ranks[0] == float(k1), nk, cnt)
    return (cnt, ranks[1], es[0], es[1] / z), unique


def _peer_select_kernel(q_ref, keys_ref, fidx_ref, vmask_ref, cnt_ref, rank2_ref, e1_ref, e2_ref):
    fidx = fidx_ref[...]
    vmask = vmask_ref[...]

    def store(vals):
        for ref, val in zip((cnt_ref, rank2_ref, e1_ref, e2_ref), vals):
            ref[...] = val.astype(ref.dtype)

    vals, unique = _peer_select_head(q_ref, keys_ref, fidx, vmask, exact_ties=False)
    store(vals)

    @pl.when(jnp.logical_not(unique))
    def _():
        store(_peer_select_head(q_ref, keys_ref, fidx, vmask, exact_ties=True)[0])


def peer_select(q, keys, *, tn=512):
    T = q.shape[0]
    tn = min(tn, T)
    fidx, vmask = _peer_cand_tables(tn)
    rows = PEER_HEADS * PEER_N_KEYS
    ospec = pl.BlockSpec((PEER_N_KEYS, tn), lambda i, h: (h, i))
    return pl.pallas_call(
        _peer_select_kernel,
        grid=(T // tn, PEER_HEADS),
        in_specs=[pl.BlockSpec((tn, 2 * PEER_HALF_DIM), lambda i, h: (i, h)),
                  pl.BlockSpec((2, PEER_N_KEYS, PEER_HALF_DIM), lambda i, h: (h, 0, 0)),
                  pl.BlockSpec((80, tn), lambda i, h: (0, 0)),
                  pl.BlockSpec((80, tn), lambda i, h: (0, 0))],
        out_specs=[ospec] * 4,
        out_shape=[jax.ShapeDtypeStruct((rows, T), dt) for dt in (F32, BF16, F32, BF16)],
        compiler_params=_cparams(("parallel", "parallel")),
    )(q, keys, fidx, vmask)


def _peer_dense_kernel(xn_ref, u_ref, vt_ref, cnt_ref, rank2_ref, e1_ref, e2_ref, res_ref, o_ref,
                       acc_ref, g0_ref, g1_ref, *, c_per_step):
    j = pl.program_id(1)
    n_tiles = pl.num_programs(1) - 1

    @pl.when(j == 0)
    def _():
        acc_ref[...] = jnp.zeros_like(acc_ref)
        g1_ref[...] = jnp.zeros_like(g1_ref)

    @pl.when((j % 2 == 0) & (j < n_tiles))
    def _():
        _peer_dense_step(xn_ref, u_ref, vt_ref, cnt_ref, rank2_ref, e1_ref, e2_ref, acc_ref,
                         g1_ref, g0_ref, j, c_per_step)

    @pl.when(j % 2 == 1)
    def _():
        _peer_dense_step(xn_ref, u_ref, vt_ref, cnt_ref, rank2_ref, e1_ref, e2_ref, acc_ref,
                         g0_ref, g1_ref, j, c_per_step)

    @pl.when(j == n_tiles)
    def _():
        _peer_dense_step(xn_ref, u_ref, vt_ref, cnt_ref, rank2_ref, e1_ref, e2_ref, acc_ref,
                         g1_ref, None, j, c_per_step)
        o_ref[...] = res_ref[...] + acc_ref[...].T


def _peer_dense_step(xn_ref, u_ref, vt_ref, cnt_ref, rank2_ref, e1_ref, e2_ref, acc_ref, g_ref, g_next_ref,
                     j, c_per_step):
    tn = xn_ref.shape[1]
    bf16_rows = 16
    reps = PEER_N_KEYS // bf16_rows

    def row_tile(ref, row):
        r16 = jnp.broadcast_to(ref[pl.ds(row, 1), :], (bf16_rows, tn)).astype(BF16)
        return jnp.concatenate([r16] * reps, axis=0)

    c0 = jnp.maximum(j - 1, 0) * c_per_step
    up_rows = 8 * PEER_N_KEYS
    blocks = []
    for cc in range(c_per_step):
        if g_next_ref is not None and (cc * PEER_N_KEYS) % up_rows == 0:
            rs = slice(cc * PEER_N_KEYS, cc * PEER_N_KEYS + up_rows)
            hT = jnp.dot(u_ref[rs, :], xn_ref[...], preferred_element_type=F32)
            g_next_ref[rs, :] = _gelu_sigmoid(hT.astype(BF16))
        c = c0 + cc
        w = None
        for h in range(PEER_HEADS):
            row = h * PEER_N_KEYS + c
            n_row = row_tile(cnt_ref, row)
            e1_row = row_tile(e1_ref, row)
            sl = slice(h * PEER_N_KEYS, (h + 1) * PEER_N_KEYS)
            term = jnp.where(rank2_ref[sl, :] < n_row, e2_ref[sl, :], 0.0) * e1_row
            w = term if w is None else w + term
        blocks.append(w * g_ref[cc * PEER_N_KEYS:(cc + 1) * PEER_N_KEYS, :])
    aT = jnp.concatenate(blocks, axis=0)
    acc_ref[...] += jnp.dot(vt_ref[...], aT, preferred_element_type=F32)


def peer_dense(xn, u, vt, cnt, rank2, e1, e2, res, *, tn=512, te=2048):
    D, T = xn.shape
    E = u.shape[0]
    tn = min(tn, T)
    rows = PEER_HEADS * PEER_N_KEYS
    sel_spec = pl.BlockSpec((rows, tn), lambda i, j: (0, i))
    n_tiles = E // te
    assert n_tiles % 2 == 0
    return pl.pallas_call(
        functools.partial(_peer_dense_kernel, c_per_step=te // PEER_N_KEYS),
        grid=(T // tn, n_tiles + 1),
        in_specs=[pl.BlockSpec((D, tn), lambda i, j: (0, i)),
                  pl.BlockSpec((te, D), lambda i, j: (jnp.minimum(j, n_tiles - 1), 0)),
                  pl.BlockSpec((D, te), lambda i, j: (0, jnp.maximum(j - 1, 0))),
                  sel_spec, sel_spec, sel_spec, sel_spec,
                  pl.BlockSpec((tn, D), lambda i, j: (i, 0))],
        out_specs=pl.BlockSpec((tn, D), lambda i, j: (i, 0)),
        out_shape=jax.ShapeDtypeStruct((T, D), F32),
        scratch_shapes=[pltpu.VMEM((D, tn), F32), pltpu.VMEM((te, tn), BF16), pltpu.VMEM((te, tn), BF16)],
        compiler_params=_cparams(("parallel", "arbitrary")),
    )(xn, u, vt, cnt, rank2, e1, e2, res)


def peer_layer(h, gamma, w_q, sub_keys, u, v):
    q, xn = norm_matmul(h, gamma, w_q.astype(BF16), emit_xn=True)
    keys = sub_keys.reshape(2 * PEER_HEADS, PEER_N_KEYS, PEER_HALF_DIM).astype(BF16)
    cnt, rank2, e1, e2 = peer_select(q, keys)
    return peer_dense(xn, u.astype(BF16), v.T.astype(BF16), cnt, rank2, e1, e2, h)


def _cumsum_aug_kernel(lf_ref, tri_ref, place_q_ref, place_k_ref, ones_q_ref, ones_k_ref,
                       qa_ref, ka_ref, c_ref, carry_ref):
    @pl.when(pl.program_id(1) == 0)
    def _():
        carry_ref[...] = jnp.zeros_like(carry_ref)

    lf = lf_ref[0]
    c = jnp.dot(tri_ref[...], lf, preferred_element_type=F32, precision=lax.Precision.HIGHEST) + carry_ref[...]
    carry_ref[...] = c[-1:, :]
    c = c * LOG2E
    c_ref[0] = c
    hi = c.astype(BF16)
    r1 = c - hi.astype(F32)
    mid = r1.astype(BF16)
    lo = (r1 - mid.astype(F32)).astype(BF16)
    nh = N_HEADS
    lane = lax.broadcasted_iota(jnp.int32, c.shape, 1)
    parts = jnp.where(lane < nh, hi.astype(F32),
                      jnp.where(lane < 2 * nh, pltpu.roll(mid.astype(F32), nh, axis=1),
                                pltpu.roll(lo.astype(F32), 2 * nh, axis=1)))
    parts = jnp.where(lane < 3 * nh, parts, 0.0).astype(BF16)
    qa_ref[0] = (jnp.dot(parts, place_q_ref[...], preferred_element_type=F32) + ones_q_ref[...]).astype(BF16)
    ka_ref[0] = (jnp.dot(parts, place_k_ref[...], preferred_element_type=F32) + ones_k_ref[...]).astype(BF16)


def fox_bias_operands(logf, *, tc=256):
    B, S, _ = logf.shape
    nh = N_HEADS
    tri = jnp.asarray(np.tril(np.ones((tc, tc), np.float32)))
    pq = np.zeros((LANES, nh * LANES), np.float32)
    pk = np.zeros((LANES, nh * LANES), np.float32)
    oq = np.zeros((1, nh * LANES), np.float32)
    ok = np.zeros((1, nh * LANES), np.float32)
    for h in range(nh):
        for part in range(3):
            pq[part * nh + h, h * LANES + part] = 1.0
            pk[part * nh + h, h * LANES + 3 + part] = -1.0
            oq[0, h * LANES + 3 + part] = 1.0
            ok[0, h * LANES + part] = 1.0
    const = lambda a: pl.BlockSpec(a.shape, lambda b, i: (0,) * a.ndim)
    pq, pk, oq, ok = jnp.asarray(pq, BF16), jnp.asarray(pk, BF16), jnp.asarray(oq), jnp.asarray(ok)
    out = jax.ShapeDtypeStruct((B, S, nh * LANES), BF16)
    return pl.pallas_call(
        _cumsum_aug_kernel,
        grid=(B, S // tc),
        in_specs=[pl.BlockSpec((1, tc, LANES), lambda b, i: (b, i, 0)),
                  const(tri), const(pq), const(pk), const(oq), const(ok)],
        out_specs=[pl.BlockSpec((1, tc, nh * LANES), lambda b, i: (b, i, 0))] * 2
        + [pl.BlockSpec((1, tc, LANES), lambda b, i: (b, i, 0))],
        out_shape=[out, out, jax.ShapeDtypeStruct((B, S, LANES), F32)],
        scratch_shapes=[pltpu.VMEM((1, LANES), F32)],
        compiler_params=_cparams(("parallel", "arbitrary")),
    )(logf, tri, pq, pk, oq, ok)


def _fox_attn_kernel(first_ref, q_ref, qa_ref, kv_ref, ka_ref, o_ref, *, tq, tk, heads_per_step):
    qi = pl.program_id(2)
    t0 = qi * tq
    n_full = t0 // tk
    j_first = first_ref[(pl.program_id(0) * pl.num_programs(1) + pl.program_id(1)) * pl.num_programs(2) + qi]
    n_diag = tq // tk
    lanes = [slice(hh * LANES, (hh + 1) * LANES) for hh in range(heads_per_step)]
    qs = [jnp.concatenate([q_ref[0, :, lsl], qa_ref[0, :, lsl]], axis=1) for lsl in lanes]

    def step(j, carry, masked):
        rows = pl.ds(pl.multiple_of(j * tk, tk), tk)
        klane = lax.broadcasted_iota(jnp.int32, (tk, LANES), 1)
        new = []
        for lsl, q, (m, acc) in zip(lanes, qs, carry):
            kv = kv_ref[0, rows, lsl]
            kk = jnp.concatenate([kv, ka_ref[0, rows, lsl]], axis=1)
            ones_v = jnp.where(klane < HEAD_DIM, 1.0, kv).astype(BF16)
            s = lax.dot_general(q, kk, _NT, preferred_element_type=F32)
            if masked:
                qpos = t0 + lax.broadcasted_iota(jnp.int32, (tq, 1), 0)
                kpos = j * tk + lax.broadcasted_iota(jnp.int32, (1, tk), 1)
                s = jnp.where(kpos <= qpos, s, NEG_INF)
            m_new = jnp.maximum(m, jnp.max(s, axis=1, keepdims=True))
            p = jnp.exp2(s - m_new).astype(BF16)
            acc = jnp.exp2(m - m_new) * acc + jnp.dot(p, ones_v, preferred_element_type=F32)
            new.append((m_new, acc))
        return tuple(new)

    init = (jnp.full((tq, 1), NEG_INF, F32), jnp.zeros((tq, LANES), F32))
    n_pairs = (n_full - j_first) // 2
    carry = lax.fori_loop(
        0, n_pairs, lambda i, c: step(j_first + 2 * i + 1, step(j_first + 2 * i, c, False), False),
        (init,) * heads_per_step)
    carry = lax.fori_loop(j_first + 2 * n_pairs, n_full, functools.partial(step, masked=False), carry)
    for d in range(n_diag):
        carry = step(n_full + d, carry, True)
    outs = [acc / acc[:, 0:1] for (_, acc) in carry]
    lane = lax.broadcasted_iota(jnp.int32, (tq, LANES), 1)
    blocks = []
    for pair in range(heads_per_step // 2):
        a, b = outs[2 * pair], outs[2 * pair + 1]
        blocks.append(jnp.where(lane < HEAD_DIM, pltpu.roll(a, HEAD_DIM, axis=1), b))
    o_ref[0] = jnp.concatenate(blocks, axis=1).astype(o_ref.dtype) if len(blocks) > 1 else blocks[0].astype(o_ref.dtype)


def _head_slots(w, n_heads, second=None):
    D = w.shape[0]
    a = w.reshape(D, n_heads, HEAD_DIM)
    b = jnp.zeros_like(a) if second is None else second.reshape(D, n_heads, HEAD_DIM)
    return jnp.concatenate([a, b], axis=-1).reshape(D, n_heads * LANES)


def fox_layer(h, gamma, w_in, f_bias, w_out, B, S):
    aw = ATTN_WIDTH
    wq = _head_slots(w_in[:, :aw] * (HEAD_DIM ** -0.5 * LOG2E), N_HEADS)
    wkv = _head_slots(w_in[:, aw:2 * aw], N_HEADS, w_in[:, 2 * aw:3 * aw])
    w_main = jnp.concatenate([wq, wkv], axis=1).astype(BF16)
    wf = jnp.pad(w_in[:, 3 * aw:], ((0, 0), (0, LANES - N_HEADS))).astype(BF16)
    bf = jnp.pad(f_bias.astype(F32), (0, LANES - N_HEADS))
    qkv = norm_matmul(h, gamma, w_main)
    logf = norm_matmul(h, gamma, wf, bias=bf, act="log_sigmoid", out_dtype=F32)
    qa, ka, c2 = fox_bias_operands(logf.reshape(B, S, LANES))
    qkv = qkv.reshape(B, S, 2 * N_HEADS * LANES)
    o = fox_attention(qkv, qa, ka, c2)
    return matmul_residual(o.reshape(B * S, aw), w_out.astype(BF16), h)


FOX_NEGLIGIBLE_LOG2 = 160.0


def _norm_maxima_kernel(x_ref, ind_ref, o_ref):
    x = x_ref[0].astype(F32)
    ss = jnp.dot((x * x).astype(BF16), ind_ref[...], preferred_element_type=F32)
    o_ref[0, 0] = jnp.broadcast_to(jnp.max(ss, axis=0, keepdims=True), o_ref.shape[2:])


def fox_norm_maxima(qkv, tile):
    B, S, W = qkv.shape
    n = S // tile
    col = np.arange(W)
    slot, lane = col // LANES, col % LANES
    used = (slot < N_HEADS) | (lane < HEAD_DIM)
    ind = jnp.asarray((slot[:, None] == np.arange(LANES)[None, :]) & used[:, None], BF16)
    rows = min(256, tile)
    ss = pl.pallas_call(
        _norm_maxima_kernel,
        grid=(B, S // rows),
        in_specs=[pl.BlockSpec((1, rows, W), lambda b, i: (b, i, 0)),
                  pl.BlockSpec(ind.shape, lambda b, i: (0, 0))],
        out_specs=pl.BlockSpec((1, 1, 8, LANES), lambda b, i: (b, i, 0, 0)),
        out_shape=jax.ShapeDtypeStruct((B, S // rows, 8, LANES), F32),
        compiler_params=_cparams(("parallel", "parallel")),
    )(qkv, ind)
    ss = jnp.max(ss[:, :, 0, :2 * N_HEADS].reshape(B, n, tile // rows, 2 * N_HEADS), axis=2)
    return jnp.sqrt(ss * 1.01)


def fox_first_chunk(qkv, c2, tile, heads_per_step):
    B, S, _ = qkv.shape
    n = S // tile
    norms = fox_norm_maxima(qkv, tile)
    qmax, kmax = norms[..., :N_HEADS], norms[..., N_HEADS:]
    c = c2[..., :N_HEADS].reshape(B, n, tile, N_HEADS)
    cmax, cmin = jnp.max(c, axis=2), jnp.min(c, axis=2)
    upper = qmax[:, :, None] * kmax[:, None, :] + cmax[:, :, None] - cmin[:, None, :]
    own = -(qmax * kmax)
    earlier = jnp.arange(n)[None, :, None, None] > jnp.arange(n)[None, None, :, None]
    skip = (upper < own[:, :, None] - FOX_NEGLIGIBLE_LOG2) & earlier
    first = jnp.sum(jnp.cumprod(skip.astype(jnp.int32), axis=2), axis=2)
    first = jnp.min(first.reshape(B, n, N_HEADS // heads_per_step, heads_per_step), axis=-1)
    return jnp.transpose(first, (0, 2, 1)).reshape(-1).astype(jnp.int32)


def fox_attention(qkv, qa, ka, c2, *, tq=1024, tk=1024, heads_per_step=2):
    B, S, _ = qkv.shape
    tq = tk = min(tq, S)
    hs = heads_per_step
    wq = hs * LANES
    n_qblk = N_HEADS // hs
    first = fox_first_chunk(qkv, c2, tk, hs)
    grid_spec = pltpu.PrefetchScalarGridSpec(
        num_scalar_prefetch=1,
        grid=(B, n_qblk, S // tq),
        in_specs=[pl.BlockSpec((1, tq, wq), lambda b, h, i, first: (b, i, h)),
                  pl.BlockSpec((1, tq, wq), lambda b, h, i, first: (b, i, h)),
                  pl.BlockSpec((1, S, wq), lambda b, h, i, first: (b, 0, n_qblk + h)),
                  pl.BlockSpec((1, S, wq), lambda b, h, i, first: (b, 0, h))],
        out_specs=pl.BlockSpec((1, tq, hs * HEAD_DIM), lambda b, h, i, first: (b, i, h)))
    return pl.pallas_call(
        functools.partial(_fox_attn_kernel, tq=tq, tk=tk, heads_per_step=hs),
        grid_spec=grid_spec,
        out_shape=jax.ShapeDtypeStruct((B, S, ATTN_WIDTH), BF16),
        compiler_params=_cparams(("parallel", "parallel", "arbitrary")),
    )(first, qkv, qa, qkv, ka)


def _rot_half_cols(w):
    D = w.shape[0]
    a = w.reshape(D, -1, HEAD_DIM)
    half = HEAD_DIM // 2
    return jnp.concatenate([-a[..., half:], a[..., :half]], axis=-1).reshape(w.shape)


def _rope_tables(S):
    half = HEAD_DIM // 2
    inv_freq = ROPE_THETA ** (-jnp.arange(half, dtype=F32) / half)
    ang = jnp.arange(S, dtype=F32)[:, None] * inv_freq[None, :]
    c, s = jnp.cos(ang), jnp.sin(ang)
    c2, s2 = jnp.concatenate([c, c], axis=1), jnp.concatenate([s, s], axis=1)
    cos = jnp.stack([jnp.concatenate([c2, jnp.ones_like(c2)], axis=1), jnp.concatenate([c2, c2], axis=1)])
    sin = jnp.stack([jnp.concatenate([s2, jnp.zeros_like(s2)], axis=1), jnp.concatenate([s2, s2], axis=1)])
    return cos, sin


def _compress_kernel(x_ref, pea_ref, peb_ref, wa_ref, wb_ref, w2_ref, o_ref, pa_ref, pb0_ref, *, n_rows):
    u = pl.program_id(1)
    x = x_ref[0].astype(F32)
    pa = jnp.dot((x + pea_ref[...]).astype(BF16), wa_ref[...], preferred_element_type=F32)
    pb = jnp.dot((x + peb_ref[...]).astype(BF16), wb_ref[...], preferred_element_type=F32)

    def emit(slab, hid):
        y = jnp.dot(_gelu_tanh(hid).astype(BF16), w2_ref[...], preferred_element_type=F32)
        o_ref[0, pl.ds(pl.multiple_of(slab * n_rows, n_rows), n_rows), :] = y

    @pl.when(u == 0)
    def _():
        pb0_ref[...] = pb

    @pl.when(u > 0)
    def _():
        emit(u - 1, pa_ref[...] + pb)

    @pl.when(u == 3)
    def _():
        emit(3, pa + pltpu.roll(pb0_ref[...], n_rows - 1, axis=0))

    pa_ref[...] = pa


def nsa_compress(src, pe, w1, w2):
    B, S, W = src.shape
    G = NSA_GROUPS
    n_rows = S // 64
    half = CMP_BLOCK // 2
    cw = half * W
    xv = src.reshape(B, n_rows, 4 * cw)
    pe_flat = jnp.transpose(pe, (1, 0, 2)).reshape(CMP_BLOCK, W).astype(F32)
    pea, peb = pe_flat[:half].reshape(1, cw), pe_flat[half:].reshape(1, cw)
    eye = jnp.eye(G, dtype=F32)
    wfull = jnp.einsum('gldh,gk->lkdgh', w1.astype(F32), eye).reshape(CMP_BLOCK, W, G * CMP_HIDDEN)
    wa = wfull[:half].reshape(cw, G * CMP_HIDDEN).astype(BF16)
    wb = wfull[half:].reshape(cw, G * CMP_HIDDEN).astype(BF16)
    w2bd = jnp.einsum('ghd,gk->ghkd', w2.astype(F32), eye).reshape(G * CMP_HIDDEN, W).astype(BF16)
    const = lambda a: pl.BlockSpec(a.shape, lambda b, u: (0,) * a.ndim)
    return pl.pallas_call(
        functools.partial(_compress_kernel, n_rows=n_rows),
        grid=(B, 4),
        in_specs=[pl.BlockSpec((1, n_rows, cw), lambda b, u: (b, 0, u)),
                  const(pea), const(peb), const(wa), const(wb), const(w2bd)],
        out_specs=pl.BlockSpec((1, 4 * n_rows, W), lambda b, u: (b, 0, 0)),
        out_shape=jax.ShapeDtypeStruct((B, 4 * n_rows, W), F32),
        scratch_shapes=[pltpu.VMEM((n_rows, G * CMP_HIDDEN), F32), pltpu.VMEM((n_rows, G * CMP_HIDDEN), F32)],
        compiler_params=_cparams(("parallel", "arbitrary")),
    )(xv, pea, peb, wa, wb, w2bd)


def _nsa_attn_kernel(q_ref, kvs_ref, kvw_ref, kvc_ref, kvct_ref, gate_ref, bmat_ref, pmat_ref, o_ref,
                     *, tq, ts, tk, seq):
    R = NSA_Q_PER_GROUP
    n_slc = seq // SLC_BLOCK
    n_sel = min(SLC_TOPK, n_slc)
    assert ts & (ts - 1) == 0 and n_slc & (n_slc - 1) == 0 and tk % tq == 0 and tq % ts == 0
    qi = pl.program_id(2)
    t0 = qi * tq

    local = [_nsa_local_branches(q_ref, kvw_ref, kvc_ref, kvct_ref, t0 + i * ts, i * ts, ts, n_slc, n_sel)
             for i in range(tq // ts)]
    o_c = jnp.concatenate([o[0][r * ts:(r + 1) * ts] for r in range(R) for o in local], axis=0)
    o_w = jnp.concatenate([o[1][r * ts:(r + 1) * ts] for r in range(R) for o in local], axis=0)
    sel_bias = jnp.concatenate([o[2] for o in local], axis=0)
    qs = jnp.concatenate([q_ref[0, :, r * LANES:(r + 1) * LANES] for r in range(R)], axis=0)
    _nsa_selected_and_combine(qs, o_c, o_w, sel_bias, kvs_ref, gate_ref, bmat_ref, pmat_ref, o_ref,
                              t0=t0, tq=tq, tk=tk, seq=seq)


def _nsa_local_branches(q_ref, kvw_ref, kvc_ref, kvct_ref, t0, row0, tq, n_slc, n_sel):
    R = NSA_Q_PER_GROUP
    M = R * tq
    qs = jnp.concatenate([q_ref[0, row0:row0 + tq, r * LANES:(r + 1) * LANES] for r in range(R)], axis=0)

    wlen = WINDOW + tq
    start = jnp.maximum(t0 - WINDOW, 0)
    kvw = kvw_ref[0, pl.ds(pl.multiple_of(start, tq), wlen), :]
    s_w = lax.dot_general(qs, kvw, _NT, preferred_element_type=F32)
    qpos = t0 + (lax.broadcasted_iota(jnp.int32, (M, 1), 0) & (tq - 1))
    kpos = start + lax.broadcasted_iota(jnp.int32, (1, wlen), 1)
    s_w = jnp.where(kpos <= qpos, jnp.where(kpos > qpos - WINDOW, s_w, NEG_INF), NEG_INF)
    p_w = jnp.exp2(s_w - jnp.max(s_w, axis=1, keepdims=True)).astype(BF16)
    wlane = lax.broadcasted_iota(jnp.int32, (wlen, LANES), 1)
    acc_w = jnp.dot(p_w, jnp.where(wlane < HEAD_DIM, 1.0, kvw).astype(BF16), preferred_element_type=F32)
    o_w = acc_w / acc_w[:, 0:1]

    n_groups, gs = _nsa_row_groups(n_slc)
    o_c, sel_bias = _nsa_compressed_and_select(qs, kvc_ref, kvct_ref, t0, tq, n_slc, n_sel, gs, n_groups)
    return o_c, o_w, sel_bias


def _nsa_row_groups(n_slc):
    n_groups = max(1, min(4, n_slc // 8))
    assert n_slc % n_groups == 0 and (n_slc // n_groups) % 8 == 0
    return n_groups, n_slc // n_groups


def _nsa_compressed_and_select(qs, kvc_ref, kvct_ref, t0, tq, n_slc, n_sel, gs, n_groups):
    R = NSA_Q_PER_GROUP
    M = R * tq
    n_rows, n_j = n_groups * 4 * gs, n_groups * gs
    log_gs = gs.bit_length() - 1
    sT = lax.dot_general(kvc_ref[0, 0, 0:n_rows, :], qs, _NT, preferred_element_type=F32)
    rowc = lax.broadcasted_iota(jnp.int32, (n_rows, 1), 0)
    blk_j = ((rowc >> (log_gs + 2)) << log_gs) + (rowc & (gs - 1))
    blk_u = (rowc >> log_gs) & 3
    cmp_end = blk_j * SLC_BLOCK + blk_u * CMP_STRIDE + (CMP_BLOCK - 1)
    tcol = t0 + (lax.broadcasted_iota(jnp.int32, (1, M), 1) & (tq - 1))
    sm = jnp.where(cmp_end <= tcol, sT, NEG_INF)
    mx = jnp.max(sm, axis=0, keepdims=True)
    e = jnp.exp2(sm - mx)
    inv = jnp.where(mx > 0.5 * NEG_INF, 1.0 / jnp.sum(e, axis=0, keepdims=True), 0.0)
    pT = e * inv
    o_c = jnp.dot(kvct_ref[0, 0, :, 0:n_rows], pT.astype(BF16), preferred_element_type=F32).T

    psum = pT[:, 0:tq]
    for r in range(1, R):
        psum = psum + pT[:, r * tq:(r + 1) * tq]
    p0, p1, p2, p3 = (jnp.concatenate([psum[(4 * g + u) * gs:(4 * g + u + 1) * gs, :] for g in range(n_groups)],
                                      axis=0) for u in range(4))
    jrow = lax.broadcasted_iota(jnp.int32, (n_j, tq), 0)
    p3_prev = jnp.where(jrow == 0, 0.0, pltpu.roll(p3, 1, axis=0))
    imp = p0 + p1 + p2 + 0.5 * p3 + 0.5 * p3_prev
    cur = (t0 + lax.broadcasted_iota(jnp.int32, (n_j, tq), 1)) >> (SLC_BLOCK.bit_length() - 1)
    forced = (jrow == 0) | (jrow == cur) | (jrow == cur - 1)
    vals = jnp.where(forced, -jnp.inf, jnp.where(jrow <= cur, imp, -jnp.inf))
    sel0 = jnp.where(forced, 1.0, 0.0)
    n_free = n_sel - 3

    picked = -2.0 ** 100

    def pick(exact_ties):
        v = vals
        for _ in range(n_free):
            m = jnp.max(v, axis=0, keepdims=True)
            if exact_ties:
                hit = jrow == jnp.min(jnp.where(v == m, jrow, n_slc), axis=0, keepdims=True)
            else:
                hit = v == jnp.where(m < 0.0, jnp.nan, m)
            v = jnp.where(hit, picked, v)
        return jnp.where(v == picked, 1.0, sel0)

    sel_fast = pick(False)
    n_cand = jnp.sum(jnp.where(vals > -jnp.inf, 1.0, 0.0), axis=0, keepdims=True)
    n_picked = jnp.sum(sel_fast - sel0, axis=0, keepdims=True)
    unique = jnp.min(jnp.where(n_picked == jnp.minimum(n_cand, float(n_free)), 1.0, 0.0)) > 0.5
    sel = lax.cond(unique, lambda: sel_fast, lambda: pick(True))
    if n_j < n_slc:
        sel = jnp.concatenate([sel, jnp.zeros((n_slc - n_j, tq), F32)], axis=0)
    sel_bias = ((sel.T - 1.0) * (-NEG_INF)).astype(BF16)
    return o_c, sel_bias


def _nsa_selected_and_combine(qs, o_c, o_w, sel_bias, kvs_ref, gate_ref, bmat_ref, pmat_ref, o_ref,
                              *, t0, tq, tk, seq):
    R = NSA_Q_PER_GROUP
    n_slc = seq // SLC_BLOCK
    blocks_per_chunk = tk // SLC_BLOCK
    n_parts = 2
    hp = R // n_parts
    mp = hp * tq
    qparts = [qs[i * mp:(i + 1) * mp] for i in range(n_parts)]
    qrow = t0 + (lax.broadcasted_iota(jnp.int32, (mp, 1), 0) & (tq - 1))
    kcol = lax.broadcasted_iota(jnp.int32, (1, tk), 1)
    n_chunks = seq // tk
    j_last = t0 // tk

    klane = lax.broadcasted_iota(jnp.int32, (tk, LANES), 1)

    def slc_step(j, carry, masked):
        kv = kvs_ref[0, pl.ds(pl.multiple_of(j * tk, tk), tk), :]
        k_sel = jnp.where(klane < HEAD_DIM, kv, bmat_ref[...]).astype(BF16)
        ones_v = jnp.where(klane < HEAD_DIM, 1.0, kv).astype(BF16)
        off = pl.multiple_of(blocks_per_chunk * (n_chunks - 1 - j), blocks_per_chunk)
        place = pmat_ref[pl.ds(off, n_slc), :].astype(BF16)
        q_bias = jnp.dot(sel_bias, place, preferred_element_type=F32).astype(BF16)
        q_bias = jnp.concatenate([q_bias] * hp, axis=0)
        new = []
        for qp, (m, acc) in zip(qparts, carry):
            s = lax.dot_general(qp + q_bias, k_sel, _NT, preferred_element_type=F32)
            if masked:
                s = jnp.where(j * tk + kcol <= qrow, s, NEG_INF)
            m_new = jnp.maximum(m, jnp.max(s, axis=1, keepdims=True))
            p = jnp.exp2(s - m_new).astype(BF16)
            acc = jnp.exp2(m - m_new) * acc + jnp.dot(p, ones_v, preferred_element_type=F32)
            new.append((m_new, acc))
        return tuple(new)

    init = (jnp.full((mp, 1), NEG_INF, F32), jnp.zeros((mp, LANES), F32))
    n_pairs = j_last // 2
    carry = lax.fori_loop(0, n_pairs, lambda i, c: slc_step(2 * i + 1, slc_step(2 * i, c, False), False),
                          (init,) * n_parts)
    carry = lax.fori_loop(2 * n_pairs, j_last, functools.partial(slc_step, masked=False), carry)
    carry = slc_step(j_last, carry, True)
    o_s = jnp.concatenate([acc / acc[:, 0:1] for (_, acc) in carry], axis=0)

    gates = gate_ref[0]
    lane = lax.broadcasted_iota(jnp.int32, (tq, LANES), 1)
    comb = []
    for r in range(R):
        rs = slice(r * tq, (r + 1) * tq)
        comb.append(gates[:, r:r + 1] * o_c[rs] + gates[:, R + r:R + r + 1] * o_s[rs]
                    + gates[:, 2 * R + r:2 * R + r + 1] * o_w[rs])
    out = [jnp.where(lane < HEAD_DIM, pltpu.roll(comb[2 * i], HEAD_DIM, axis=1), comb[2 * i + 1])
           for i in range(R // 2)]
    o_ref[0] = jnp.concatenate(out, axis=1).astype(o_ref.dtype)


def nsa_attention(qkv, kvc, kvct, gates, *, tq=512, ts=256, tk=1024):
    B, S, _ = qkv.shape
    G, R = NSA_GROUPS, NSA_Q_PER_GROUP
    tk = min(tk, S)
    n_slc = S // SLC_BLOCK
    n_cmp = kvc.shape[2]
    bpc = tk // SLC_BLOCK
    assert bpc <= LANES - HEAD_DIM
    off = bpc * (S // tk - 1)
    lane = np.arange(LANES)[None, :]
    bmat = jnp.asarray(lane - HEAD_DIM == np.arange(tk)[:, None] // SLC_BLOCK, BF16)
    pmat = jnp.asarray((np.arange(n_slc + off)[:, None] - off == lane - HEAD_DIM) & (lane >= HEAD_DIM)
                       & (lane < HEAD_DIM + bpc), F32)
    slc_blk0, win_blk0 = N_HEADS, N_HEADS + G
    return pl.pallas_call(
        functools.partial(_nsa_attn_kernel, tq=tq, ts=ts, tk=tk, seq=S),
        grid=(B, G, S // tq),
        in_specs=[pl.BlockSpec((1, tq, R * LANES), lambda b, g, i: (b, i, g)),
                  pl.BlockSpec((1, S, LANES), lambda b, g, i: (b, 0, slc_blk0 + g)),
                  pl.BlockSpec((1, S, LANES), lambda b, g, i: (b, 0, win_blk0 + g)),
                  pl.BlockSpec((1, 1, n_cmp, LANES), lambda b, g, i: (b, g, 0, 0)),
                  pl.BlockSpec((1, 1, LANES, n_cmp), lambda b, g, i: (b, g, 0, 0)),
                  pl.BlockSpec((1, tq, LANES), lambda b, g, i: (b, i, g)),
                  pl.BlockSpec(bmat.shape, lambda b, g, i: (0, 0)),
                  pl.BlockSpec(pmat.shape, lambda b, g, i: (0, 0))],
        out_specs=pl.BlockSpec((1, tq, R * HEAD_DIM), lambda b, g, i: (b, i, g)),
        out_shape=jax.ShapeDtypeStruct((B, S, ATTN_WIDTH), BF16),
        compiler_params=_cparams(("parallel", "parallel", "arbitrary")),
    )(qkv, qkv, qkv, kvc, kvct, gates, bmat, pmat)


def nsa_layer(h, gamma, w_in, pe_k, w1_k, w2_k, pe_v, w1_v, w2_v, w_out, B, S):
    G, R, hd, aw = NSA_GROUPS, NSA_Q_PER_GROUP, HEAD_DIM, ATTN_WIDTH
    kvd = G * hd
    sec = lambda i: w_in[:, aw + i * kvd: aw + (i + 1) * kvd]
    wq = w_in[:, :aw] * (hd ** -0.5 * LOG2E)
    wa = jnp.concatenate([_head_slots(wq, N_HEADS), _head_slots(sec(2), G, sec(3)),
                          _head_slots(sec(4), G, sec(5))], axis=1).astype(BF16)
    wb = jnp.concatenate([_rot_half_cols(wq), _rot_half_cols(sec(2)), _rot_half_cols(sec(4))], axis=1).astype(BF16)
    cos, sin = _rope_tables(S)
    qkv = norm_matmul(h, gamma, wa, wb=wb, cos=cos, sin=sin)
    kc_src = norm_matmul(h, gamma, sec(0).astype(BF16), wb=_rot_half_cols(sec(0)).astype(BF16),
                         cos=cos, sin=sin, table_of_tile=lambda j: 1)
    vc_src = norm_matmul(h, gamma, sec(1).astype(BF16))
    wg = w_in[:, aw + 6 * kvd:].reshape(-1, 3, G, R)
    wg = jnp.transpose(wg, (0, 2, 1, 3)).reshape(-1, G, 3 * R)
    wg = jnp.pad(wg, ((0, 0), (0, 0), (0, LANES - 3 * R))).reshape(-1, G * LANES).astype(BF16)
    gates = norm_matmul(h, gamma, wg, act="sigmoid", out_dtype=F32)
    kc = nsa_compress(kc_src.reshape(B, S, kvd), pe_k, w1_k, w2_k)
    vc = nsa_compress(vc_src.reshape(B, S, kvd), pe_v, w1_v, w2_v)
    n_cmp = kc.shape[1]
    kvc = jnp.concatenate([kc.reshape(B, n_cmp, G, hd), vc.reshape(B, n_cmp, G, hd)], axis=-1)
    n_groups, gs = _nsa_row_groups(S // SLC_BLOCK)
    kvc = kvc.reshape(B, 4, n_groups, gs, G, 2 * hd)
    kvc = jnp.transpose(kvc, (0, 4, 2, 1, 3, 5)).reshape(B, G, n_cmp, 2 * hd).astype(BF16)
    kvct = jnp.swapaxes(kvc, 2, 3)
    o = nsa_attention(qkv.reshape(B, S, -1), kvc, kvct, gates.reshape(B, S, G * LANES))
    return matmul_residual(o.reshape(B * S, aw), w_out.astype(BF16), h)


def kernel(x, l0_attn_norm, l0_w_in, l0_cmp_pe_k, l0_cmp_w1_k, l0_cmp_w2_k, l0_cmp_pe_v, l0_cmp_w1_v,
           l0_cmp_w2_v, l0_w_out, l0_ffn_norm, l0_peer_wq, l0_peer_keys, l0_peer_u, l0_peer_v,
           l1_attn_norm, l1_w_in, l1_f_bias, l1_w_out, l1_ffn_norm, l1_peer_wq, l1_peer_keys, l1_peer_u,
           l1_peer_v, final_norm):
    B, S, D = x.shape
    h = x.reshape(B * S, D)
    h = nsa_layer(h, l0_attn_norm, l0_w_in, l0_cmp_pe_k, l0_cmp_w1_k, l0_cmp_w2_k, l0_cmp_pe_v, l0_cmp_w1_v,
                  l0_cmp_w2_v, l0_w_out, B, S)
    h = peer_layer(h, l0_ffn_norm, l0_peer_wq, l0_peer_keys, l0_peer_u, l0_peer_v)
    h = fox_layer(h, l1_attn_norm, l1_w_in, l1_f_bias, l1_w_out, B, S)
    h = peer_layer(h, l1_ffn_norm, l1_peer_wq, l1_peer_keys, l1_peer_u, l1_peer_v)
    return rmsnorm(h, final_norm).reshape(B, S, D)
```

```python
import functools

import numpy as np
import jax
import jax.numpy as jnp
from jax import lax
from jax.experimental import pallas as pl
from jax.experimental.pallas import tpu as pltpu

F32 = jnp.float32
BF16 = jnp.bfloat16

D_MODEL = 1024
N_HEADS = 16
HEAD_DIM = 64
ATTN_WIDTH = N_HEADS * HEAD_DIM
NSA_GROUPS = 4
NSA_Q_PER_GROUP = N_HEADS // NSA_GROUPS
CMP_BLOCK = 32
CMP_STRIDE = 16
CMP_HIDDEN = 2 * HEAD_DIM
SLC_BLOCK = 64
SLC_TOPK = 16
WINDOW = 512
FORCE_SCORE = 1.0e4
ROPE_THETA = 10000.0
PEER_HEADS = 8
PEER_N_KEYS = 128
PEER_TOPK = 16
PEER_HALF_DIM = 128
RMS_EPS = 1e-6
NEG_INF = -1e30
LOG2E = 1.4426950408889634

LANES = 128
VMEM_LIMIT_BYTES = 56 * 1024 * 1024

_NT = (((1,), (1,)), ((), ()))


def _cparams(sem, vmem=VMEM_LIMIT_BYTES, flags=None):
    return pltpu.CompilerParams(dimension_semantics=sem, vmem_limit_bytes=vmem, flags=flags)


def _gelu_tanh(x):
    return 0.5 * x * (1.0 + jnp.tanh(0.7978845608028654 * (x + 0.044715 * (x * x * x))))


def _gelu_sigmoid(x):
    c = -2.0 * 0.7978845608028654 * LOG2E
    t = x * (c + (c * 0.044715) * (x * x))
    return x / (1.0 + jnp.exp2(t))


def _rms_rows(x, g):
    ms = jnp.mean(x * x, axis=-1, keepdims=True)
    return x * lax.rsqrt(ms + RMS_EPS) * g


def _norm_mm_kernel(*refs, act, has_bias, rope, emit_xn):
    it = iter(refs)
    x_ref, g_ref = next(it), next(it)
    wa_ref = next(it)
    wb_ref = next(it) if rope else None
    cos_ref = next(it) if rope else None
    sin_ref = next(it) if rope else None
    b_ref = next(it) if has_bias else None
    o_ref = next(it)
    xo_ref = next(it) if emit_xn else None
    xn_ref = next(it)

    @pl.when(pl.program_id(1) == 0)
    def _():
        xn = _rms_rows(x_ref[...], g_ref[...])
        xn_ref[...] = xn.astype(BF16)
        if emit_xn:
            xo_ref[...] = xn.T.astype(BF16)

    xn = xn_ref[...]
    y = jnp.dot(xn, wa_ref[...], preferred_element_type=F32)
    if rope:
        yb = jnp.dot(xn, wb_ref[...], preferred_element_type=F32)
        cos, sin = cos_ref[...], sin_ref[...]
        packed_b = 2 * yb.shape[1] == y.shape[1]
        for s in range(y.shape[1] // LANES):
            sl = slice(s * LANES, (s + 1) * LANES)
            if packed_b:
                b = yb[:, (s // 2) * LANES:(s // 2 + 1) * LANES]
                b = pltpu.roll(b, HEAD_DIM, axis=1) if s % 2 else b
            else:
                b = yb[:, sl]
            o_ref[:, sl] = (y[:, sl] * cos + b * sin).astype(o_ref.dtype)
        return
    if has_bias:
        y = y + b_ref[...]
    if act == "sigmoid":
        y = jax.nn.sigmoid(y)
    elif act == "log_sigmoid":
        y = jax.nn.log_sigmoid(y)
    o_ref[...] = y.astype(o_ref.dtype)


def norm_matmul(x, gamma, wa, *, wb=None, cos=None, sin=None, table_of_tile=None, bias=None,
                act=None, out_dtype=BF16, emit_xn=False, tm=1024, tn=1024):
    T, D = x.shape
    N = wa.shape[1]
    tm, tn = min(tm, T), min(tn, N)
    assert T % tm == 0 and N % tn == 0 and tn % LANES == 0
    rope = wb is not None
    in_specs = [pl.BlockSpec((tm, D), lambda i, j: (i, 0)),
                pl.BlockSpec((1, D), lambda i, j: (0, 0)),
                pl.BlockSpec((D, tn), lambda i, j: (0, j))]
    args = [x, gamma.reshape(1, D).astype(F32), wa]
    if rope:
        S = cos.shape[1]
        assert S % tm == 0
        n_pos = S // tm
        tmap = table_of_tile if table_of_tile is not None else (lambda j: 0)
        assert wb.shape[1] in (N, N // 2)
        in_specs += [pl.BlockSpec((D, tn * wb.shape[1] // N), lambda i, j: (0, j)),
                     pl.BlockSpec((None, tm, LANES), lambda i, j: (tmap(j), i % n_pos, 0)),
                     pl.BlockSpec((None, tm, LANES), lambda i, j: (tmap(j), i % n_pos, 0))]
        args += [wb, cos, sin]
    if bias is not None:
        in_specs.append(pl.BlockSpec((1, tn), lambda i, j: (0, j)))
        args.append(bias.reshape(1, N).astype(F32))
    out_shape = [jax.ShapeDtypeStruct((T, N), out_dtype)]
    out_specs = [pl.BlockSpec((tm, tn), lambda i, j: (i, j))]
    if emit_xn:
        out_shape.append(jax.ShapeDtypeStruct((D, T), BF16))
        out_specs.append(pl.BlockSpec((D, tm), lambda i, j: (0, i)))
    res = pl.pallas_call(
        functools.partial(_norm_mm_kernel, act=act, has_bias=bias is not None, rope=rope, emit_xn=emit_xn),
        grid=(T // tm, N // tn),
        in_specs=in_specs,
        out_specs=out_specs,
        out_shape=out_shape,
        scratch_shapes=[pltpu.VMEM((tm, D), BF16)],
        compiler_params=_cparams(("parallel", "arbitrary")),
    )(*args)
    return res if emit_xn else res[0]


def _mm_res_kernel(a_ref, w_ref, r_ref, o_ref):
    o_ref[...] = r_ref[...] + jnp.dot(a_ref[...], w_ref[...], preferred_element_type=F32)


def matmul_residual(a, w, res, *, tm=1024, tn=1024):
    T, K = a.shape
    N = w.shape[1]
    tm, tn = min(tm, T), min(tn, N)
    assert T % tm == 0 and N % tn == 0
    return pl.pallas_call(
        _mm_res_kernel,
        grid=(T // tm, N // tn),
        in_specs=[pl.BlockSpec((tm, K), lambda i, j: (i, 0)),
                  pl.BlockSpec((K, tn), lambda i, j: (0, j)),
                  pl.BlockSpec((tm, tn), lambda i, j: (i, j))],
        out_specs=pl.BlockSpec((tm, tn), lambda i, j: (i, j)),
        out_shape=jax.ShapeDtypeStruct((T, N), F32),
        compiler_params=_cparams(("parallel", "arbitrary")),
    )(a, w, res)


def _rmsnorm_kernel(x_ref, g_ref, o_ref):
    o_ref[...] = _rms_rows(x_ref[...], g_ref[...])


def rmsnorm(x, gamma, *, tm=1024):
    T, D = x.shape
    tm = min(tm, T)
    return pl.pallas_call(
        _rmsnorm_kernel,
        grid=(T // tm,),
        in_specs=[pl.BlockSpec((tm, D), lambda i: (i, 0)), pl.BlockSpec((1, D), lambda i: (0, 0))],
        out_specs=pl.BlockSpec((tm, D), lambda i: (i, 0)),
        out_shape=jax.ShapeDtypeStruct((T, D), F32),
        compiler_params=_cparams(("parallel",)),
    )(x, gamma.reshape(1, D).astype(F32))


def _peer_cand_tables(tn):
    fidx, vmask = [], []
    for k2 in range(16):
        fidx.append(k2); vmask.append(0.0)
    for k1 in range(1, 8):
        lim = PEER_TOPK // (k1 + 1)
        for k2 in range(8):
            fidx.append(k1 * 16 + k2); vmask.append(0.0 if k2 < lim else -np.inf)
    for k1 in range(8, 16):
        fidx.append(k1 * 16); vmask.append(0.0)
    fidx = np.broadcast_to(np.asarray(fidx, np.int32)[:, None], (80, tn))
    vmask = np.broadcast_to(np.asarray(vmask, np.float32)[:, None], (80, tn))
    return jnp.asarray(fidx), jnp.asarray(vmask)


def _top16_rows(s, exact_ties):
    n, tn = s.shape
    rows = lax.broadcasted_iota(jnp.int32, (n, tn), 0)
    rows16 = lax.broadcasted_iota(jnp.int32, (PEER_TOPK, tn), 0)
    tops = jnp.zeros((PEER_TOPK, tn), F32)
    unit = 2.0 ** 122
    v = s
    for k in range(PEER_TOPK):
        m = jnp.max(v, axis=0, keepdims=True)
        if exact_ties:
            hit = rows == jnp.min(jnp.where(v == m, rows, n), axis=0, keepdims=True)
        else:
            hit = v == m
        v = jnp.where(hit, -(32.0 + k) * unit, v)
        tops = jnp.where(rows16 == k, m, tops)
    was_picked = v <= -32.0 * unit
    rank = jnp.where(was_picked, v * (-1.0 / unit) - 32.0, float(PEER_TOPK))
    n_picked = jnp.sum(jnp.where(was_picked, 1.0, 0.0), axis=0, keepdims=True)
    return tops, rank, n_picked


def _peer_select_head(q_ref, keys_ref, fidx, vmask, exact_ties):
    tops, ranks, es, picked = [], [], [], []
    for p in range(2):
        q = q_ref[:, p * PEER_HALF_DIM:(p + 1) * PEER_HALF_DIM]
        s = lax.dot_general(keys_ref[p], q, _NT, preferred_element_type=F32)
        t, r, n_picked = _top16_rows(s, exact_ties)
        tops.append(t); ranks.append(r)
        es.append(jnp.exp(s - t[0:1, :]))
        picked.append(n_picked)
    ts1, ts2 = tops
    pieces = [ts1[0:1, :] + ts2]
    for k1 in range(1, 8):
        pieces.append(ts1[k1:k1 + 1, :] + ts2[0:8, :])
    pieces.append(ts1[8:16, :] + ts2[0:1, :])
    cand0 = jnp.concatenate(pieces, axis=0) + vmask
    cand = cand0
    for _ in range(PEER_TOPK):
        m = jnp.max(cand, axis=0, keepdims=True)
        if exact_ties:
            hit = fidx == jnp.min(jnp.where(cand == m, fidx, 4096), axis=0, keepdims=True)
        else:
            hit = cand == m
        cand = jnp.where(hit, -jnp.inf, cand)
    taken = jnp.logical_and(cand == -jnp.inf, vmask == 0.0)
    takenf = taken.astype(F32)
    picked.append(jnp.sum(takenf, axis=0, keepdims=True))
    unique = jnp.min(jnp.where((picked[0] == PEER_TOPK) & (picked[1] == PEER_TOPK) & (picked[2] == PEER_TOPK),
                               1.0, 0.0)) > 0.5
    best = ts1[0:1, :] + ts2[0:1, :]
    z = jnp.sum(jnp.where(taken, jnp.exp(cand0 - best), 0.0), axis=0, keepdims=True)
    counts = [jnp.sum(takenf[0:16, :], axis=0, keepdims=True)]
    for k1 in range(1, 8):
        counts.append(jnp.sum(takenf[16 + 8 * (k1 - 1):16 + 8 * k1, :], axis=0, keepdims=True))
    tail = takenf[72:80, :]
    cnt = jnp.zeros_like(ranks[0])
    for k1 in range(PEER_TOPK):
        nk = counts[k1] if k1 < 8 else tail[k1 - 8:k1 - 7, :]
        cnt = jnp.where(ranks[0] == float(k1), nk, cnt)
    return (cnt, ranks[1], es[0], es[1] / z), unique


def _peer_select_kernel(q_ref, keys_ref, fidx_ref, vmask_ref, cnt_ref, rank2_ref, e1_ref, e2_ref):
    fidx = fidx_ref[...]
    vmask = vmask_ref[...]

    def store(vals):
        for ref, val in zip((cnt_ref, rank2_ref, e1_ref, e2_ref), vals):
            ref[...] = val.astype(ref.dtype)

    vals, unique = _peer_select_head(q_ref, keys_ref, fidx, vmask, exact_ties=False)
    store(vals)

    @pl.when(jnp.logical_not(unique))
    def _():
        store(_peer_select_head(q_ref, keys_ref, fidx, vmask, exact_ties=True)[0])


def peer_select(q, keys, *, tn=512):
    T = q.shape[0]
    tn = min(tn, T)
    fidx, vmask = _peer_cand_tables(tn)
    rows = PEER_HEADS * PEER_N_KEYS
    ospec = pl.BlockSpec((PEER_N_KEYS, tn), lambda i, h: (h, i))
    return pl.pallas_call(
        _peer_select_kernel,
        grid=(T // tn, PEER_HEADS),
        in_specs=[pl.BlockSpec((tn, 2 * PEER_HALF_DIM), lambda i, h: (i, h)),
                  pl.BlockSpec((2, PEER_N_KEYS, PEER_HALF_DIM), lambda i, h: (h, 0, 0)),
                  pl.BlockSpec((80, tn), lambda i, h: (0, 0)),
                  pl.BlockSpec((80, tn), lambda i, h: (0, 0))],
        out_specs=[ospec] * 4,
        out_shape=[jax.ShapeDtypeStruct((rows, T), dt) for dt in (F32, BF16, F32, BF16)],
        compiler_params=_cparams(("parallel", "parallel")),
    )(q, keys, fidx, vmask)


def _peer_dense_kernel(xn_ref, u_ref, vt_ref, cnt_ref, rank2_ref, e1_ref, e2_ref, res_ref, o_ref,
                       acc_ref, g0_ref, g1_ref, *, c_per_step):
    j = pl.program_id(1)
    n_tiles = pl.num_programs(1) - 1

    @pl.when(j == 0)
    def _():
        acc_ref[...] = jnp.zeros_like(acc_ref)
        g1_ref[...] = jnp.zeros_like(g1_ref)

    @pl.when((j % 2 == 0) & (j < n_tiles))
    def _():
        _peer_dense_step(xn_ref, u_ref, vt_ref, cnt_ref, rank2_ref, e1_ref, e2_ref, acc_ref,
                         g1_ref, g0_ref, j, c_per_step)

    @pl.when(j % 2 == 1)
    def _():
        _peer_dense_step(xn_ref, u_ref, vt_ref, cnt_ref, rank2_ref, e1_ref, e2_ref, acc_ref,
                         g0_ref, g1_ref, j, c_per_step)

    @pl.when(j == n_tiles)
    def _():
        _peer_dense_step(xn_ref, u_ref, vt_ref, cnt_ref, rank2_ref, e1_ref, e2_ref, acc_ref,
                         g1_ref, None, j, c_per_step)
        o_ref[...] = res_ref[...] + acc_ref[...].T


def _peer_dense_step(xn_ref, u_ref, vt_ref, cnt_ref, rank2_ref, e1_ref, e2_ref, acc_ref, g_ref, g_next_ref,
                     j, c_per_step):
    tn = xn_ref.shape[1]
    bf16_rows = 16
    reps = PEER_N_KEYS // bf16_rows

    def row_tile(ref, row):
        r16 = jnp.broadcast_to(ref[pl.ds(row, 1), :], (bf16_rows, tn)).astype(BF16)
        return jnp.concatenate([r16] * reps, axis=0)

    c0 = jnp.maximum(j - 1, 0) * c_per_step
    up_rows = 4 * PEER_N_KEYS
    blocks = []
    for cc in range(c_per_step):
        if g_next_ref is not None and (cc * PEER_N_KEYS) % up_rows == 0:
            rs = slice(cc * PEER_N_KEYS, cc * PEER_N_KEYS + up_rows)
            hT = jnp.dot(u_ref[rs, :], xn_ref[...], preferred_element_type=F32)
            g_next_ref[rs, :] = _gelu_sigmoid(hT.astype(BF16))
        c = c0 + cc
        w = None
        for h in range(PEER_HEADS):
            row = h * PEER_N_KEYS + c
            n_row = row_tile(cnt_ref, row)
            e1_row = row_tile(e1_ref, row)
            sl = slice(h * PEER_N_KEYS, (h + 1) * PEER_N_KEYS)
            term = jnp.where(rank2_ref[sl, :] < n_row, e2_ref[sl, :], 0.0) * e1_row
            w = term if w is None else w + term
        blocks.append(w * g_ref[cc * PEER_N_KEYS:(cc + 1) * PEER_N_KEYS, :])
    aT = jnp.concatenate(blocks, axis=0)
    acc_ref[...] += jnp.dot(vt_ref[...], aT, preferred_element_type=F32)


def peer_dense(xn, u, vt, cnt, rank2, e1, e2, res, *, tn=512, te=2048):
    D, T = xn.shape
    E = u.shape[0]
    tn = min(tn, T)
    rows = PEER_HEADS * PEER_N_KEYS
    sel_spec = pl.BlockSpec((rows, tn), lambda i, j: (0, i))
    n_tiles = E // te
    assert n_tiles % 2 == 0
    return pl.pallas_call(
        functools.partial(_peer_dense_kernel, c_per_step=te // PEER_N_KEYS),
        grid=(T // tn, n_tiles + 1),
        in_specs=[pl.BlockSpec((D, tn), lambda i, j: (0, i)),
                  pl.BlockSpec((te, D), lambda i, j: (jnp.minimum(j, n_tiles - 1), 0)),
                  pl.BlockSpec((D, te), lambda i, j: (0, jnp.maximum(j - 1, 0))),
                  sel_spec, sel_spec, sel_spec, sel_spec,
                  pl.BlockSpec((tn, D), lambda i, j: (i, 0))],
        out_specs=pl.BlockSpec((tn, D), lambda i, j: (i, 0)),
        out_shape=jax.ShapeDtypeStruct((T, D), F32),
        scratch_shapes=[pltpu.VMEM((D, tn), F32), pltpu.VMEM((te, tn), BF16), pltpu.VMEM((te, tn), BF16)],
        compiler_params=_cparams(("parallel", "arbitrary")),
    )(xn, u, vt, cnt, rank2, e1, e2, res)


def peer_layer(h, gamma, w_q, sub_keys, u, v):
    q, xn = norm_matmul(h, gamma, w_q.astype(BF16), emit_xn=True)
    keys = sub_keys.reshape(2 * PEER_HEADS, PEER_N_KEYS, PEER_HALF_DIM).astype(BF16)
    cnt, rank2, e1, e2 = peer_select(q, keys)
    return peer_dense(xn, u.astype(BF16), v.T.astype(BF16), cnt, rank2, e1, e2, h)


def _cumsum_aug_kernel(lf_ref, tri_ref, place_q_ref, place_k_ref, ones_q_ref, ones_k_ref,
                       qa_ref, ka_ref, c_ref, carry_ref):
    @pl.when(pl.program_id(1) == 0)
    def _():
        carry_ref[...] = jnp.zeros_like(carry_ref)

    lf = lf_ref[0]
    c = jnp.dot(tri_ref[...], lf, preferred_element_type=F32, precision=lax.Precision.HIGHEST) + carry_ref[...]
    carry_ref[...] = c[-1:, :]
    c = c * LOG2E
    c_ref[0] = c
    hi = c.astype(BF16)
    r1 = c - hi.astype(F32)
    mid = r1.astype(BF16)
    lo = (r1 - mid.astype(F32)).astype(BF16)
    nh = N_HEADS
    lane = lax.broadcasted_iota(jnp.int32, c.shape, 1)
    parts = jnp.where(lane < nh, hi.astype(F32),
                      jnp.where(lane < 2 * nh, pltpu.roll(mid.astype(F32), nh, axis=1),
                                pltpu.roll(lo.astype(F32), 2 * nh, axis=1)))
    parts = jnp.where(lane < 3 * nh, parts, 0.0).astype(BF16)
    qa_ref[0] = (jnp.dot(parts, place_q_ref[...], preferred_element_type=F32) + ones_q_ref[...]).astype(BF16)
    ka_ref[0] = (jnp.dot(parts, place_k_ref[...], preferred_element_type=F32) + ones_k_ref[...]).astype(BF16)


def fox_bias_operands(logf, *, tc=256):
    B, S, _ = logf.shape
    nh = N_HEADS
    tri = jnp.asarray(np.tril(np.ones((tc, tc), np.float32)))
    pq = np.zeros((LANES, nh * LANES), np.float32)
    pk = np.zeros((LANES, nh * LANES), np.float32)
    oq = np.zeros((1, nh * LANES), np.float32)
    ok = np.zeros((1, nh * LANES), np.float32)
    for h in range(nh):
        for part in range(3):
            pq[part * nh + h, h * LANES + part] = 1.0
            pk[part * nh + h, h * LANES + 3 + part] = -1.0
            oq[0, h * LANES + 3 + part] = 1.0
            ok[0, h * LANES + part] = 1.0
    const = lambda a: pl.BlockSpec(a.shape, lambda b, i: (0,) * a.ndim)
    pq, pk, oq, ok = jnp.asarray(pq, BF16), jnp.asarray(pk, BF16), jnp.asarray(oq), jnp.asarray(ok)
    out = jax.ShapeDtypeStruct((B, S, nh * LANES), BF16)
    return pl.pallas_call(
        _cumsum_aug_kernel,
        grid=(B, S // tc),
        in_specs=[pl.BlockSpec((1, tc, LANES), lambda b, i: (b, i, 0)),
                  const(tri), const(pq), const(pk), const(oq), const(ok)],
        out_specs=[pl.BlockSpec((1, tc, nh * LANES), lambda b, i: (b, i, 0))] * 2
        + [pl.BlockSpec((1, tc, LANES), lambda b, i: (b, i, 0))],
        out_shape=[out, out, jax.ShapeDtypeStruct((B, S, LANES), F32)],
        scratch_shapes=[pltpu.VMEM((1, LANES), F32)],
        compiler_params=_cparams(("parallel", "arbitrary")),
    )(logf, tri, pq, pk, oq, ok)


def _fox_attn_kernel(first_ref, q_ref, qa_ref, kv_ref, ka_ref, o_ref, *, tq, tk, heads_per_step):
    qi = pl.program_id(2)
    t0 = qi * tq
    n_full = t0 // tk
    j_first = first_ref[(pl.program_id(0) * pl.num_programs(1) + pl.program_id(1)) * pl.num_programs(2) + qi]
    n_diag = tq // tk
    lanes = [slice(hh * LANES, (hh + 1) * LANES) for hh in range(heads_per_step)]
    qs = [jnp.concatenate([q_ref[0, :, lsl], qa_ref[0, :, lsl]], axis=1) for lsl in lanes]

    def step(j, carry, masked):
        rows = pl.ds(pl.multiple_of(j * tk, tk), tk)
        klane = lax.broadcasted_iota(jnp.int32, (tk, LANES), 1)
        new = []
        for lsl, q, (m, acc) in zip(lanes, qs, carry):
            kv = kv_ref[0, rows, lsl]
            kk = jnp.concatenate([kv, ka_ref[0, rows, lsl]], axis=1)
            ones_v = jnp.where(klane < HEAD_DIM, 1.0, kv).astype(BF16)
            s = lax.dot_general(q, kk, _NT, preferred_element_type=F32)
            if masked:
                qpos = t0 + lax.broadcasted_iota(jnp.int32, (tq, 1), 0)
                kpos = j * tk + lax.broadcasted_iota(jnp.int32, (1, tk), 1)
                s = jnp.where(kpos <= qpos, s, NEG_INF)
            m_new = jnp.maximum(m, jnp.max(s, axis=1, keepdims=True))
            p = jnp.exp2(s - m_new).astype(BF16)
            acc = jnp.exp2(m - m_new) * acc + jnp.dot(p, ones_v, preferred_element_type=F32)
            new.append((m_new, acc))
        return tuple(new)

    init = (jnp.full((tq, 1), NEG_INF, F32), jnp.zeros((tq, LANES), F32))
    n_pairs = (n_full - j_first) // 2
    carry = lax.fori_loop(
        0, n_pairs, lambda i, c: step(j_first + 2 * i + 1, step(j_first + 2 * i, c, False), False),
        (init,) * heads_per_step)
    carry = lax.fori_loop(j_first + 2 * n_pairs, n_full, functools.partial(step, masked=False), carry)
    for d in range(n_diag):
        carry = step(n_full + d, carry, True)
    outs = [acc / acc[:, 0:1] for (_, acc) in carry]
    lane = lax.broadcasted_iota(jnp.int32, (tq, LANES), 1)
    blocks = []
    for pair in range(heads_per_step // 2):
        a, b = outs[2 * pair], outs[2 * pair + 1]
        blocks.append(jnp.where(lane < HEAD_DIM, pltpu.roll(a, HEAD_DIM, axis=1), b))
    o_ref[0] = jnp.concatenate(blocks, axis=1).astype(o_ref.dtype) if len(blocks) > 1 else blocks[0].astype(o_ref.dtype)


def _head_slots(w, n_heads, second=None):
    D = w.shape[0]
    a = w.reshape(D, n_heads, HEAD_DIM)
    b = jnp.zeros_like(a) if second is None else second.reshape(D, n_heads, HEAD_DIM)
    return jnp.concatenate([a, b], axis=-1).reshape(D, n_heads * LANES)


def fox_layer(h, gamma, w_in, f_bias, w_out, B, S):
    aw = ATTN_WIDTH
    wq = _head_slots(w_in[:, :aw] * (HEAD_DIM ** -0.5 * LOG2E), N_HEADS)
    wkv = _head_slots(w_in[:, aw:2 * aw], N_HEADS, w_in[:, 2 * aw:3 * aw])
    w_main = jnp.concatenate([wq, wkv], axis=1).astype(BF16)
    wf = jnp.pad(w_in[:, 3 * aw:], ((0, 0), (0, LANES - N_HEADS))).astype(BF16)
    bf = jnp.pad(f_bias.astype(F32), (0, LANES - N_HEADS))
    qkv = norm_matmul(h, gamma, w_main)
    logf = norm_matmul(h, gamma, wf, bias=bf, act="log_sigmoid", out_dtype=F32)
    qa, ka, c2 = fox_bias_operands(logf.reshape(B, S, LANES))
    qkv = qkv.reshape(B, S, 2 * N_HEADS * LANES)
    o = fox_attention(qkv, qa, ka, c2)
    return matmul_residual(o.reshape(B * S, aw), w_out.astype(BF16), h)


FOX_NEGLIGIBLE_LOG2 = 160.0


def _norm_maxima_kernel(x_ref, ind_ref, o_ref):
    x = x_ref[0].astype(F32)
    ss = jnp.dot((x * x).astype(BF16), ind_ref[...], preferred_element_type=F32)
    o_ref[0, 0] = jnp.broadcast_to(jnp.max(ss, axis=0, keepdims=True), o_ref.shape[2:])


def fox_norm_maxima(qkv, tile):
    B, S, W = qkv.shape
    n = S // tile
    col = np.arange(W)
    slot, lane = col // LANES, col % LANES
    used = (slot < N_HEADS) | (lane < HEAD_DIM)
    ind = jnp.asarray((slot[:, None] == np.arange(LANES)[None, :]) & used[:, None], BF16)
    rows = min(256, tile)
    ss = pl.pallas_call(
        _norm_maxima_kernel,
        grid=(B, S // rows),
        in_specs=[pl.BlockSpec((1, rows, W), lambda b, i: (b, i, 0)),
                  pl.BlockSpec(ind.shape, lambda b, i: (0, 0))],
        out_specs=pl.BlockSpec((1, 1, 8, LANES), lambda b, i: (b, i, 0, 0)),
        out_shape=jax.ShapeDtypeStruct((B, S // rows, 8, LANES), F32),
        compiler_params=_cparams(("parallel", "parallel")),
    )(qkv, ind)
    ss = jnp.max(ss[:, :, 0, :2 * N_HEADS].reshape(B, n, tile // rows, 2 * N_HEADS), axis=2)
    return jnp.sqrt(ss * 1.01)


def fox_first_chunk(qkv, c2, tile, heads_per_step):
    B, S, _ = qkv.shape
    n = S // tile
    norms = fox_norm_maxima(qkv, tile)
    qmax, kmax = norms[..., :N_HEADS], norms[..., N_HEADS:]
    c = c2[..., :N_HEADS].reshape(B, n, tile, N_HEADS)
    cmax, cmin = jnp.max(c, axis=2), jnp.min(c, axis=2)
    upper = qmax[:, :, None] * kmax[:, None, :] + cmax[:, :, None] - cmin[:, None, :]
    own = -(qmax * kmax)
    earlier = jnp.arange(n)[None, :, None, None] > jnp.arange(n)[None, None, :, None]
    skip = (upper < own[:, :, None] - FOX_NEGLIGIBLE_LOG2) & earlier
    first = jnp.sum(jnp.cumprod(skip.astype(jnp.int32), axis=2), axis=2)
    first = jnp.min(first.reshape(B, n, N_HEADS // heads_per_step, heads_per_step), axis=-1)
    return jnp.transpose(first, (0, 2, 1)).reshape(-1).astype(jnp.int32)


def fox_attention(qkv, qa, ka, c2, *, tq=1024, tk=1024, heads_per_step=2):
    B, S, _ = qkv.shape
    tq = tk = min(tq, S)
    hs = heads_per_step
    wq = hs * LANES
    n_qblk = N_HEADS // hs
    first = fox_first_chunk(qkv, c2, tk, hs)
    grid_spec = pltpu.PrefetchScalarGridSpec(
        num_scalar_prefetch=1,
        grid=(B, n_qblk, S // tq),
        in_specs=[pl.BlockSpec((1, tq, wq), lambda b, h, i, first: (b, i, h)),
                  pl.BlockSpec((1, tq, wq), lambda b, h, i, first: (b, i, h)),
                  pl.BlockSpec((1, S, wq), lambda b, h, i, first: (b, 0, n_qblk + h)),
                  pl.BlockSpec((1, S, wq), lambda b, h, i, first: (b, 0, h))],
        out_specs=pl.BlockSpec((1, tq, hs * HEAD_DIM), lambda b, h, i, first: (b, i, h)))
    return pl.pallas_call(
        functools.partial(_fox_attn_kernel, tq=tq, tk=tk, heads_per_step=hs),
        grid_spec=grid_spec,
        out_shape=jax.ShapeDtypeStruct((B, S, ATTN_WIDTH), BF16),
        compiler_params=_cparams(("parallel", "parallel", "arbitrary")),
    )(first, qkv, qa, qkv, ka)


def _rot_half_cols(w):
    D = w.shape[0]
    a = w.reshape(D, -1, HEAD_DIM)
    half = HEAD_DIM // 2
    return jnp.concatenate([-a[..., half:], a[..., :half]], axis=-1).reshape(w.shape)


def _rope_tables(S):
    half = HEAD_DIM // 2
    inv_freq = ROPE_THETA ** (-jnp.arange(half, dtype=F32) / half)
    ang = jnp.arange(S, dtype=F32)[:, None] * inv_freq[None, :]
    c, s = jnp.cos(ang), jnp.sin(ang)
    c2, s2 = jnp.concatenate([c, c], axis=1), jnp.concatenate([s, s], axis=1)
    cos = jnp.stack([jnp.concatenate([c2, jnp.ones_like(c2)], axis=1), jnp.concatenate([c2, c2], axis=1)])
    sin = jnp.stack([jnp.concatenate([s2, jnp.zeros_like(s2)], axis=1), jnp.concatenate([s2, s2], axis=1)])
    return cos, sin


def _compress_kernel(x_ref, pea_ref, peb_ref, wa_ref, wb_ref, w2_ref, o_ref, pa_ref, pb0_ref, *, n_rows):
    u = pl.program_id(1)
    x = x_ref[0].astype(F32)
    pa = jnp.dot((x + pea_ref[...]).astype(BF16), wa_ref[...], preferred_element_type=F32)
    pb = jnp.dot((x + peb_ref[...]).astype(BF16), wb_ref[...], preferred_element_type=F32)

    def emit(slab, hid):
        y = jnp.dot(_gelu_tanh(hid).astype(BF16), w2_ref[...], preferred_element_type=F32)
        o_ref[0, pl.ds(pl.multiple_of(slab * n_rows, n_rows), n_rows), :] = y

    @pl.when(u == 0)
    def _():
        pb0_ref[...] = pb

    @pl.when(u > 0)
    def _():
        emit(u - 1, pa_ref[...] + pb)

    @pl.when(u == 3)
    def _():
        emit(3, pa + pltpu.roll(pb0_ref[...], n_rows - 1, axis=0))

    pa_ref[...] = pa


def nsa_compress(src, pe, w1, w2):
    B, S, W = src.shape
    G = NSA_GROUPS
    n_rows = S // 64
    half = CMP_BLOCK // 2
    cw = half * W
    xv = src.reshape(B, n_rows, 4 * cw)
    pe_flat = jnp.transpose(pe, (1, 0, 2)).reshape(CMP_BLOCK, W).astype(F32)
    pea, peb = pe_flat[:half].reshape(1, cw), pe_flat[half:].reshape(1, cw)
    eye = jnp.eye(G, dtype=F32)
    wfull = jnp.einsum('gldh,gk->lkdgh', w1.astype(F32), eye).reshape(CMP_BLOCK, W, G * CMP_HIDDEN)
    wa = wfull[:half].reshape(cw, G * CMP_HIDDEN).astype(BF16)
    wb = wfull[half:].reshape(cw, G * CMP_HIDDEN).astype(BF16)
    w2bd = jnp.einsum('ghd,gk->ghkd', w2.astype(F32), eye).reshape(G * CMP_HIDDEN, W).astype(BF16)
    const = lambda a: pl.BlockSpec(a.shape, lambda b, u: (0,) * a.ndim)
    return pl.pallas_call(
        functools.partial(_compress_kernel, n_rows=n_rows),
        grid=(B, 4),
        in_specs=[pl.BlockSpec((1, n_rows, cw), lambda b, u: (b, 0, u)),
                  const(pea), const(peb), const(wa), const(wb), const(w2bd)],
        out_specs=pl.BlockSpec((1, 4 * n_rows, W), lambda b, u: (b, 0, 0)),
        out_shape=jax.ShapeDtypeStruct((B, 4 * n_rows, W), F32),
        scratch_shapes=[pltpu.VMEM((n_rows, G * CMP_HIDDEN), F32), pltpu.VMEM((n_rows, G * CMP_HIDDEN), F32)],
        compiler_params=_cparams(("parallel", "arbitrary")),
    )(xv, pea, peb, wa, wb, w2bd)


def _nsa_attn_kernel(q_ref, kvs_ref, kvw_ref, kvc_ref, kvct_ref, gate_ref, bmat_ref, pmat_ref, o_ref,
                     *, tq, ts, tk, seq):
    R = NSA_Q_PER_GROUP
    n_slc = seq // SLC_BLOCK
    n_sel = min(SLC_TOPK, n_slc)
    assert ts & (ts - 1) == 0 and n_slc & (n_slc - 1) == 0 and tk % tq == 0 and tq % ts == 0
    qi = pl.program_id(2)
    t0 = qi * tq

    local = [_nsa_local_branches(q_ref, kvw_ref, kvc_ref, kvct_ref, t0 + i * ts, i * ts, ts, n_slc, n_sel)
             for i in range(tq // ts)]
    o_c = jnp.concatenate([o[0][r * ts:(r + 1) * ts] for r in range(R) for o in local], axis=0)
    o_w = jnp.concatenate([o[1][r * ts:(r + 1) * ts] for r in range(R) for o in local], axis=0)
    sel_bias = jnp.concatenate([o[2] for o in local], axis=0)
    qs = jnp.concatenate([q_ref[0, :, r * LANES:(r + 1) * LANES] for r in range(R)], axis=0)
    _nsa_selected_and_combine(qs, o_c, o_w, sel_bias, kvs_ref, gate_ref, bmat_ref, pmat_ref, o_ref,
                              t0=t0, tq=tq, tk=tk, seq=seq)


def _nsa_local_branches(q_ref, kvw_ref, kvc_ref, kvct_ref, t0, row0, tq, n_slc, n_sel):
    R = NSA_Q_PER_GROUP
    M = R * tq
    qs = jnp.concatenate([q_ref[0, row0:row0 + tq, r * LANES:(r + 1) * LANES] for r in range(R)], axis=0)

    wlen = WINDOW + tq
    start = jnp.maximum(t0 - WINDOW, 0)
    kvw = kvw_ref[0, pl.ds(pl.multiple_of(start, tq), wlen), :]
    s_w = lax.dot_general(qs, kvw, _NT, preferred_element_type=F32)
    qpos = t0 + (lax.broadcasted_iota(jnp.int32, (M, 1), 0) & (tq - 1))
    kpos = start + lax.broadcasted_iota(jnp.int32, (1, wlen), 1)
    s_w = jnp.where(kpos <= qpos, jnp.where(kpos > qpos - WINDOW, s_w, NEG_INF), NEG_INF)
    p_w = jnp.exp2(s_w - jnp.max(s_w, axis=1, keepdims=True)).astype(BF16)
    wlane = lax.broadcasted_iota(jnp.int32, (wlen, LANES), 1)
    acc_w = jnp.dot(p_w, jnp.where(wlane < HEAD_DIM, 1.0, kvw).astype(BF16), preferred_element_type=F32)
    o_w = acc_w / acc_w[:, 0:1]

    n_groups, gs = _nsa_row_groups(n_slc)
    o_c, sel_bias = _nsa_compressed_and_select(qs, kvc_ref, kvct_ref, t0, tq, n_slc, n_sel, gs, n_groups)
    return o_c, o_w, sel_bias


def _nsa_row_groups(n_slc):
    n_groups = max(1, min(4, n_slc // 8))
    assert n_slc % n_groups == 0 and (n_slc // n_groups) % 8 == 0
    return n_groups, n_slc // n_groups


def _nsa_compressed_and_select(qs, kvc_ref, kvct_ref, t0, tq, n_slc, n_sel, gs, n_groups):
    R = NSA_Q_PER_GROUP
    M = R * tq
    n_rows, n_j = n_groups * 4 * gs, n_groups * gs
    log_gs = gs.bit_length() - 1
    sT = lax.dot_general(kvc_ref[0, 0, 0:n_rows, :], qs, _NT, preferred_element_type=F32)
    rowc = lax.broadcasted_iota(jnp.int32, (n_rows, 1), 0)
    blk_j = ((rowc >> (log_gs + 2)) << log_gs) + (rowc & (gs - 1))
    blk_u = (rowc >> log_gs) & 3
    cmp_end = blk_j * SLC_BLOCK + blk_u * CMP_STRIDE + (CMP_BLOCK - 1)
    tcol = t0 + (lax.broadcasted_iota(jnp.int32, (1, M), 1) & (tq - 1))
    sm = jnp.where(cmp_end <= tcol, sT, NEG_INF)
    mx = jnp.max(sm, axis=0, keepdims=True)
    e = jnp.exp2(sm - mx)
    inv = jnp.where(mx > 0.5 * NEG_INF, 1.0 / jnp.sum(e, axis=0, keepdims=True), 0.0)
    pT = e * inv
    o_c = jnp.dot(kvct_ref[0, 0, :, 0:n_rows], pT.astype(BF16), preferred_element_type=F32).T

    psum = pT[:, 0:tq]
    for r in range(1, R):
        psum = psum + pT[:, r * tq:(r + 1) * tq]
    p0, p1, p2, p3 = (jnp.concatenate([psum[(4 * g + u) * gs:(4 * g + u + 1) * gs, :] for g in range(n_groups)],
                                      axis=0) for u in range(4))
    jrow = lax.broadcasted_iota(jnp.int32, (n_j, tq), 0)
    p3_prev = jnp.where(jrow == 0, 0.0, pltpu.roll(p3, 1, axis=0))
    imp = p0 + p1 + p2 + 0.5 * p3 + 0.5 * p3_prev
    cur = (t0 + lax.broadcasted_iota(jnp.int32, (n_j, tq), 1)) >> (SLC_BLOCK.bit_length() - 1)
    forced = (jrow == 0) | (jrow == cur) | (jrow == cur - 1)
    vals = jnp.where(forced, -jnp.inf, jnp.where(jrow <= cur, imp, -jnp.inf))
    sel0 = jnp.where(forced, 1.0, 0.0)
    n_free = n_sel - 3

    picked = -2.0 ** 100

    def pick(exact_ties):
        v = vals
        for _ in range(n_free):
            m = jnp.max(v, axis=0, keepdims=True)
            if exact_ties:
                hit = jrow == jnp.min(jnp.where(v == m, jrow, n_slc), axis=0, keepdims=True)
            else:
                hit = v == jnp.where(m < 0.0, jnp.nan, m)
            v = jnp.where(hit, picked, v)
        return jnp.where(v == picked, 1.0, sel0)

    sel_fast = pick(False)
    n_cand = jnp.sum(jnp.where(vals > -jnp.inf, 1.0, 0.0), axis=0, keepdims=True)
    n_picked = jnp.sum(sel_fast - sel0, axis=0, keepdims=True)
    unique = jnp.min(jnp.where(n_picked == jnp.minimum(n_cand, float(n_free)), 1.0, 0.0)) > 0.5
    sel = lax.cond(unique, lambda: sel_fast, lambda: pick(True))
    if n_j < n_slc:
        sel = jnp.concatenate([sel, jnp.zeros((n_slc - n_j, tq), F32)], axis=0)
    sel_bias = ((sel.T - 1.0) * (-NEG_INF)).astype(BF16)
    return o_c, sel_bias


def _nsa_selected_and_combine(qs, o_c, o_w, sel_bias, kvs_ref, gate_ref, bmat_ref, pmat_ref, o_ref,
                              *, t0, tq, tk, seq):
    R = NSA_Q_PER_GROUP
    n_slc = seq // SLC_BLOCK
    blocks_per_chunk = tk // SLC_BLOCK
    n_parts = 2
    hp = R // n_parts
    mp = hp * tq
    qparts = [qs[i * mp:(i + 1) * mp] for i in range(n_parts)]
    qrow = t0 + (lax.broadcasted_iota(jnp.int32, (mp, 1), 0) & (tq - 1))
    n_chunks = seq // tk
    j_last = t0 // tk

    def slc_step(j, carry, masked, half=None):
        width = tk if half is None else tk // 2
        if half is None or isinstance(half, int):
            first = (half or 0) * width
        else:
            first = pl.multiple_of(half * width, width)
        kcol = first + lax.broadcasted_iota(jnp.int32, (1, width), 1)
        klane = lax.broadcasted_iota(jnp.int32, (width, LANES), 1)
        kv = kvs_ref[0, pl.ds(pl.multiple_of(j * tk + first, width), width), :]
        k_sel = jnp.where(klane < HEAD_DIM, kv, bmat_ref[pl.ds(first, width), :]).astype(BF16)
        ones_v = jnp.where(klane < HEAD_DIM, 1.0, kv).astype(BF16)
        off = pl.multiple_of(blocks_per_chunk * (n_chunks - 1 - j), blocks_per_chunk)
        place = pmat_ref[pl.ds(off, n_slc), :].astype(BF16)
        q_bias = jnp.dot(sel_bias, place, preferred_element_type=F32).astype(BF16)
        q_bias = jnp.concatenate([q_bias] * hp, axis=0)
        new = []
        for qp, (m, acc) in zip(qparts, carry):
            s = lax.dot_general(qp + q_bias, k_sel, _NT, preferred_element_type=F32)
            if masked:
                s = jnp.where(j * tk + kcol <= qrow, s, NEG_INF)
            m_new = jnp.maximum(m, jnp.max(s, axis=1, keepdims=True))
            p = jnp.exp2(s - m_new).astype(BF16)
            acc = jnp.exp2(m - m_new) * acc + jnp.dot(p, ones_v, preferred_element_type=F32)
            new.append((m_new, acc))
        return tuple(new)

    init = (jnp.full((mp, 1), NEG_INF, F32), jnp.zeros((mp, LANES), F32))
    n_pairs = j_last // 2
    carry = lax.fori_loop(0, n_pairs, lambda i, c: slc_step(2 * i + 1, slc_step(2 * i, c, False), False),
                          (init,) * n_parts)
    carry = lax.fori_loop(2 * n_pairs, j_last, functools.partial(slc_step, masked=False), carry)
    if 2 * tq == tk:
        own_half = (t0 - j_last * tk) // tq
        carry = lax.fori_loop(0, own_half, lambda _, c: slc_step(j_last, c, False, half=0), carry)
        carry = slc_step(j_last, carry, True, half=own_half)
    else:
        carry = slc_step(j_last, carry, True)
    o_s = jnp.concatenate([acc / acc[:, 0:1] for (_, acc) in carry], axis=0)

    gates = gate_ref[0]
    lane = lax.broadcasted_iota(jnp.int32, (tq, LANES), 1)
    comb = []
    for r in range(R):
        rs = slice(r * tq, (r + 1) * tq)
        comb.append(gates[:, r:r + 1] * o_c[rs] + gates[:, R + r:R + r + 1] * o_s[rs]
                    + gates[:, 2 * R + r:2 * R + r + 1] * o_w[rs])
    out = [jnp.where(lane < HEAD_DIM, pltpu.roll(comb[2 * i], HEAD_DIM, axis=1), comb[2 * i + 1])
           for i in range(R // 2)]
    o_ref[0] = jnp.concatenate(out, axis=1).astype(o_ref.dtype)


def nsa_attention(qkv, kvc, kvct, gates, *, tq=512, ts=256, tk=1024):
    B, S, _ = qkv.shape
    G, R = NSA_GROUPS, NSA_Q_PER_GROUP
    tk = min(tk, S)
    n_slc = S // SLC_BLOCK
    n_cmp = kvc.shape[2]
    bpc = tk // SLC_BLOCK
    assert bpc <= LANES - HEAD_DIM
    off = bpc * (S // tk - 1)
    lane = np.arange(LANES)[None, :]
    bmat = jnp.asarray(lane - HEAD_DIM == np.arange(tk)[:, None] // SLC_BLOCK, BF16)
    pmat = jnp.asarray((np.arange(n_slc + off)[:, None] - off == lane - HEAD_DIM) & (lane >= HEAD_DIM)
                       & (lane < HEAD_DIM + bpc), F32)
    slc_blk0, win_blk0 = N_HEADS, N_HEADS + G
    return pl.pallas_call(
        functools.partial(_nsa_attn_kernel, tq=tq, ts=ts, tk=tk, seq=S),
        grid=(B, G, S // tq),
        in_specs=[pl.BlockSpec((1, tq, R * LANES), lambda b, g, i: (b, i, g)),
                  pl.BlockSpec((1, S, LANES), lambda b, g, i: (b, 0, slc_blk0 + g)),
                  pl.BlockSpec((1, S, LANES), lambda b, g, i: (b, 0, win_blk0 + g)),
                  pl.BlockSpec((1, 1, n_cmp, LANES), lambda b, g, i: (b, g, 0, 0)),
                  pl.BlockSpec((1, 1, LANES, n_cmp), lambda b, g, i: (b, g, 0, 0)),
                  pl.BlockSpec((1, tq, LANES), lambda b, g, i: (b, i, g)),
                  pl.BlockSpec(bmat.shape, lambda b, g, i: (0, 0)),
                  pl.BlockSpec(pmat.shape, lambda b, g, i: (0, 0))],
        out_specs=pl.BlockSpec((1, tq, R * HEAD_DIM), lambda b, g, i: (b, i, g)),
        out_shape=jax.ShapeDtypeStruct((B, S, ATTN_WIDTH), BF16),
        compiler_params=_cparams(("parallel", "parallel", "arbitrary")),
    )(qkv, qkv, qkv, kvc, kvct, gates, bmat, pmat)


def nsa_layer(h, gamma, w_in, pe_k, w1_k, w2_k, pe_v, w1_v, w2_v, w_out, B, S):
    G, R, hd, aw = NSA_GROUPS, NSA_Q_PER_GROUP, HEAD_DIM, ATTN_WIDTH
    kvd = G * hd
    sec = lambda i: w_in[:, aw + i * kvd: aw + (i + 1) * kvd]
    wq = w_in[:, :aw] * (hd ** -0.5 * LOG2E)
    wa = jnp.concatenate([_head_slots(wq, N_HEADS), _head_slots(sec(2), G, sec(3)),
                          _head_slots(sec(4), G, sec(5))], axis=1).astype(BF16)
    wb = jnp.concatenate([_rot_half_cols(wq), _rot_half_cols(sec(2)), _rot_half_cols(sec(4))], axis=1).astype(BF16)
    cos, sin = _rope_tables(S)
    qkv = norm_matmul(h, gamma, wa, wb=wb, cos=cos, sin=sin)
    kc_src = norm_matmul(h, gamma, sec(0).astype(BF16), wb=_rot_half_cols(sec(0)).astype(BF16),
                         cos=cos, sin=sin, table_of_tile=lambda j: 1)
    vc_src = norm_matmul(h, gamma, sec(1).astype(BF16))
    wg = w_in[:, aw + 6 * kvd:].reshape(-1, 3, G, R)
    wg = jnp.transpose(wg, (0, 2, 1, 3)).reshape(-1, G, 3 * R)
    wg = jnp.pad(wg, ((0, 0), (0, 0), (0, LANES - 3 * R))).reshape(-1, G * LANES).astype(BF16)
    gates = norm_matmul(h, gamma, wg, act="sigmoid", out_dtype=F32)
    kc = nsa_compress(kc_src.reshape(B, S, kvd), pe_k, w1_k, w2_k)
    vc = nsa_compress(vc_src.reshape(B, S, kvd), pe_v, w1_v, w2_v)
    n_cmp = kc.shape[1]
    kvc = jnp.concatenate([kc.reshape(B, n_cmp, G, hd), vc.reshape(B, n_cmp, G, hd)], axis=-1)
    n_groups, gs = _nsa_row_groups(S // SLC_BLOCK)
    kvc = kvc.reshape(B, 4, n_groups, gs, G, 2 * hd)
    kvc = jnp.transpose(kvc, (0, 4, 2, 1, 3, 5)).reshape(B, G, n_cmp, 2 * hd).astype(BF16)
    kvct = jnp.swapaxes(kvc, 2, 3)
    o = nsa_attention(qkv.reshape(B, S, -1), kvc, kvct, gates.reshape(B, S, G * LANES))
    return matmul_residual(o.reshape(B * S, aw), w_out.astype(BF16), h)


def kernel(x, l0_attn_norm, l0_w_in, l0_cmp_pe_k, l0_cmp_w1_k, l0_cmp_w2_k, l0_cmp_pe_v, l0_cmp_w1_v,
           l0_cmp_w2_v, l0_w_out, l0_ffn_norm, l0_peer_wq, l0_peer_keys, l0_peer_u, l0_peer_v,
           l1_attn_norm, l1_w_in, l1_f_bias, l1_w_out, l1_ffn_norm, l1_peer_wq, l1_peer_keys, l1_peer_u,
           l1_peer_v, final_norm):
    B, S, D = x.shape
    h = x.reshape(B * S, D)
    h = nsa_layer(h, l0_attn_norm, l0_w_in, l0_cmp_pe_k, l0_cmp_w1_k, l0_cmp_w2_k, l0_cmp_pe_v, l0_cmp_w1_v,
                  l0_cmp_w2_v, l0_w_out, B, S)
    h = peer_layer(h, l0_ffn_norm, l0_peer_wq, l0_peer_keys, l0_peer_u, l0_peer_v)
    h = fox_layer(h, l1_attn_norm, l1_w_in, l1_f_bias, l1_w_out, B, S)
    h = peer_layer(h, l1_ffn_norm, l1_peer_wq, l1_peer_keys, l1_peer_u, l1_peer_v)
    return rmsnorm(h, final_norm).reshape(B, S, D)
```

```python
import functools

import numpy as np
import jax
import jax.numpy as jnp
from jax import lax
from jax.experimental import pallas as pl
from jax.experimental.pallas import tpu as pltpu

F32 = jnp.float32
BF16 = jnp.bfloat16

D_MODEL = 1024
N_HEADS = 16
HEAD_DIM = 64
ATTN_WIDTH = N_HEADS * HEAD_DIM
NSA_GROUPS = 4
NSA_Q_PER_GROUP = N_HEADS // NSA_GROUPS
CMP_BLOCK = 32
CMP_STRIDE = 16
CMP_HIDDEN = 2 * HEAD_DIM
SLC_BLOCK = 64
SLC_TOPK = 16
WINDOW = 512
FORCE_SCORE = 1.0e4
ROPE_THETA = 10000.0
PEER_HEADS = 8
PEER_N_KEYS = 128
PEER_TOPK = 16
PEER_HALF_DIM = 128
RMS_EPS = 1e-6
NEG_INF = -1e30
LOG2E = 1.4426950408889634

LANES = 128
VMEM_LIMIT_BYTES = 56 * 1024 * 1024

_NT = (((1,), (1,)), ((), ()))


def _cparams(sem, vmem=VMEM_LIMIT_BYTES, flags=None):
    return pltpu.CompilerParams(dimension_semantics=sem, vmem_limit_bytes=vmem, flags=flags)


def _gelu_tanh(x):
    return 0.5 * x * (1.0 + jnp.tanh(0.7978845608028654 * (x + 0.044715 * (x * x * x))))


def _gelu_sigmoid(x):
    c = -2.0 * 0.7978845608028654 * LOG2E
    t = x * (c + (c * 0.044715) * (x * x))
    return x / (1.0 + jnp.exp2(t))


def _rms_rows(x, g):
    ms = jnp.mean(x * x, axis=-1, keepdims=True)
    return x * lax.rsqrt(ms + RMS_EPS) * g


def _norm_mm_kernel(*refs, act, has_bias, rope, emit_xn):
    it = iter(refs)
    x_ref, g_ref = next(it), next(it)
    wa_ref = next(it)
    wb_ref = next(it) if rope else None
    cos_ref = next(it) if rope else None
    sin_ref = next(it) if rope else None
    b_ref = next(it) if has_bias else None
    o_ref = next(it)
    xo_ref = next(it) if emit_xn else None
    xn_ref = next(it)

    @pl.when(pl.program_id(1) == 0)
    def _():
        xn = _rms_rows(x_ref[...], g_ref[...])
        xn_ref[...] = xn.astype(BF16)
        if emit_xn:
            xo_ref[...] = xn.T.astype(BF16)

    xn = xn_ref[...]
    y = jnp.dot(xn, wa_ref[...], preferred_element_type=F32)
    if rope:
        yb = jnp.dot(xn, wb_ref[...], preferred_element_type=F32)
        cos, sin = cos_ref[...], sin_ref[...]
        packed_b = 2 * yb.shape[1] == y.shape[1]
        for s in range(y.shape[1] // LANES):
            sl = slice(s * LANES, (s + 1) * LANES)
            if packed_b:
                b = yb[:, (s // 2) * LANES:(s // 2 + 1) * LANES]
                b = pltpu.roll(b, HEAD_DIM, axis=1) if s % 2 else b
            else:
                b = yb[:, sl]
            o_ref[:, sl] = (y[:, sl] * cos + b * sin).astype(o_ref.dtype)
        return
    if has_bias:
        y = y + b_ref[...]
    if act == "sigmoid":
        y = jax.nn.sigmoid(y)
    elif act == "log_sigmoid":
        y = jax.nn.log_sigmoid(y)
    o_ref[...] = y.astype(o_ref.dtype)


def norm_matmul(x, gamma, wa, *, wb=None, cos=None, sin=None, table_of_tile=None, bias=None,
                act=None, out_dtype=BF16, emit_xn=False, tm=1024, tn=1024):
    T, D = x.shape
    N = wa.shape[1]
    tm, tn = min(tm, T), min(tn, N)
    assert T % tm == 0 and N % tn == 0 and tn % LANES == 0
    rope = wb is not None
    in_specs = [pl.BlockSpec((tm, D), lambda i, j: (i, 0)),
                pl.BlockSpec((1, D), lambda i, j: (0, 0)),
                pl.BlockSpec((D, tn), lambda i, j: (0, j))]
    args = [x, gamma.reshape(1, D).astype(F32), wa]
    if rope:
        S = cos.shape[1]
        assert S % tm == 0
        n_pos = S // tm
        tmap = table_of_tile if table_of_tile is not None else (lambda j: 0)
        assert wb.shape[1] in (N, N // 2)
        in_specs += [pl.BlockSpec((D, tn * wb.shape[1] // N), lambda i, j: (0, j)),
                     pl.BlockSpec((None, tm, LANES), lambda i, j: (tmap(j), i % n_pos, 0)),
                     pl.BlockSpec((None, tm, LANES), lambda i, j: (tmap(j), i % n_pos, 0))]
        args += [wb, cos, sin]
    if bias is not None:
        in_specs.append(pl.BlockSpec((1, tn), lambda i, j: (0, j)))
        args.append(bias.reshape(1, N).astype(F32))
    out_shape = [jax.ShapeDtypeStruct((T, N), out_dtype)]
    out_specs = [pl.BlockSpec((tm, tn), lambda i, j: (i, j))]
    if emit_xn:
        out_shape.append(jax.ShapeDtypeStruct((D, T), BF16))
        out_specs.append(pl.BlockSpec((D, tm), lambda i, j: (0, i)))
    res = pl.pallas_call(
        functools.partial(_norm_mm_kernel, act=act, has_bias=bias is not None, rope=rope, emit_xn=emit_xn),
        grid=(T // tm, N // tn),
        in_specs=in_specs,
        out_specs=out_specs,
        out_shape=out_shape,
        scratch_shapes=[pltpu.VMEM((tm, D), BF16)],
        compiler_params=_cparams(("parallel", "arbitrary")),
    )(*args)
    return res if emit_xn else res[0]


def _mm_res_kernel(a_ref, w_ref, r_ref, o_ref):
    o_ref[...] = r_ref[...] + jnp.dot(a_ref[...], w_ref[...], preferred_element_type=F32)


def matmul_residual(a, w, res, *, tm=1024, tn=1024):
    T, K = a.shape
    N = w.shape[1]
    tm, tn = min(tm, T), min(tn, N)
    assert T % tm == 0 and N % tn == 0
    return pl.pallas_call(
        _mm_res_kernel,
        grid=(T // tm, N // tn),
        in_specs=[pl.BlockSpec((tm, K), lambda i, j: (i, 0)),
                  pl.BlockSpec((K, tn), lambda i, j: (0, j)),
                  pl.BlockSpec((tm, tn), lambda i, j: (i, j))],
        out_specs=pl.BlockSpec((tm, tn), lambda i, j: (i, j)),
        out_shape=jax.ShapeDtypeStruct((T, N), F32),
        compiler_params=_cparams(("parallel", "arbitrary")),
    )(a, w, res)


def _rmsnorm_kernel(x_ref, g_ref, o_ref):
    o_ref[...] = _rms_rows(x_ref[...], g_ref[...])


def rmsnorm(x, gamma, *, tm=1024):
    T, D = x.shape
    tm = min(tm, T)
    return pl.pallas_call(
        _rmsnorm_kernel,
        grid=(T // tm,),
        in_specs=[pl.BlockSpec((tm, D), lambda i: (i, 0)), pl.BlockSpec((1, D), lambda i: (0, 0))],
        out_specs=pl.BlockSpec((tm, D), lambda i: (i, 0)),
        out_shape=jax.ShapeDtypeStruct((T, D), F32),
        compiler_params=_cparams(("parallel",)),
    )(x, gamma.reshape(1, D).astype(F32))


def _peer_cand_tables(tn):
    fidx, vmask = [], []
    for k2 in range(16):
        fidx.append(k2); vmask.append(0.0)
    for k1 in range(1, 8):
        lim = PEER_TOPK // (k1 + 1)
        for k2 in range(8):
            fidx.append(k1 * 16 + k2); vmask.append(0.0 if k2 < lim else -np.inf)
    for k1 in range(8, 16):
        fidx.append(k1 * 16); vmask.append(0.0)
    fidx = np.broadcast_to(np.asarray(fidx, np.int32)[:, None], (80, tn))
    vmask = np.broadcast_to(np.asarray(vmask, np.float32)[:, None], (80, tn))
    return jnp.asarray(fidx), jnp.asarray(vmask)


def _top16_rows(s, exact_ties):
    n, tn = s.shape
    rows = lax.broadcasted_iota(jnp.int32, (n, tn), 0)
    rows16 = lax.broadcasted_iota(jnp.int32, (PEER_TOPK, tn), 0)
    tops = jnp.zeros((PEER_TOPK, tn), F32)
    unit = 2.0 ** 122
    v = s
    for k in range(PEER_TOPK):
        m = jnp.max(v, axis=0, keepdims=True)
        if exact_ties:
            hit = rows == jnp.min(jnp.where(v == m, rows, n), axis=0, keepdims=True)
        else:
            hit = v == m
        v = jnp.where(hit, -(32.0 + k) * unit, v)
        tops = jnp.where(rows16 == k, m, tops)
    was_picked = v <= -32.0 * unit
    rank = jnp.where(was_picked, v * (-1.0 / unit) - 32.0, float(PEER_TOPK))
    n_picked = jnp.sum(jnp.where(was_picked, 1.0, 0.0), axis=0, keepdims=True)
    return tops, rank, n_picked


def _peer_select_head(q_ref, keys_ref, fidx, vmask, exact_ties):
    tops, ranks, es, picked = [], [], [], []
    for p in range(2):
        q = q_ref[:, p * PEER_HALF_DIM:(p + 1) * PEER_HALF_DIM]
        s = lax.dot_general(keys_ref[p], q, _NT, preferred_element_type=F32)
        t, r, n_picked = _top16_rows(s, exact_ties)
        tops.append(t); ranks.append(r)
        es.append(jnp.exp(s - t[0:1, :]))
        picked.append(n_picked)
    ts1, ts2 = tops
    pieces = [ts1[0:1, :] + ts2]
    for k1 in range(1, 8):
        pieces.append(ts1[k1:k1 + 1, :] + ts2[0:8, :])
    pieces.append(ts1[8:16, :] + ts2[0:1, :])
    cand0 = jnp.concatenate(pieces, axis=0) + vmask
    cand = cand0
    for _ in range(PEER_TOPK):
        m = jnp.max(cand, axis=0, keepdims=True)
        if exact_ties:
            hit = fidx == jnp.min(jnp.where(cand == m, fidx, 4096), axis=0, keepdims=True)
        else:
            hit = cand == m
        cand = jnp.where(hit, -jnp.inf, cand)
    taken = jnp.logical_and(cand == -jnp.inf, vmask == 0.0)
    takenf = taken.astype(F32)
    picked.append(jnp.sum(takenf, axis=0, keepdims=True))
    unique = jnp.min(jnp.where((picked[0] == PEER_TOPK) & (picked[1] == PEER_TOPK) & (picked[2] == PEER_TOPK),
                               1.0, 0.0)) > 0.5
    best = ts1[0:1, :] + ts2[0:1, :]
    z = jnp.sum(jnp.where(taken, jnp.exp(cand0 - best), 0.0), axis=0, keepdims=True)
    counts = [jnp.sum(takenf[0:16, :], axis=0, keepdims=True)]
    for k1 in range(1, 8):
        counts.append(jnp.sum(takenf[16 + 8 * (k1 - 1):16 + 8 * k1, :], axis=0, keepdims=True))
    tail = takenf[72:80, :]
    cnt = jnp.zeros_like(ranks[0])
    for k1 in range(PEER_TOPK):
        nk = counts[k1] if k1 < 8 else tail[k1 - 8:k1 - 7, :]
        cnt = jnp.where(ranks[0] == float(k1), nk, cnt)
    return (cnt, ranks[1], es[0], es[1] / z), unique


def _peer_select_kernel(q_ref, keys_ref, fidx_ref, vmask_ref, cnt_ref, rank2_ref, e1_ref, e2_ref):
    fidx = fidx_ref[...]
    vmask = vmask_ref[...]

    def store(vals):
        for ref, val in zip((cnt_ref, rank2_ref, e1_ref, e2_ref), vals):
            ref[...] = val.astype(ref.dtype)

    vals, unique = _peer_select_head(q_ref, keys_ref, fidx, vmask, exact_ties=False)
    store(vals)

    @pl.when(jnp.logical_not(unique))
    def _():
        store(_peer_select_head(q_ref, keys_ref, fidx, vmask, exact_ties=True)[0])


def peer_select(q, keys, *, tn=512):
    T = q.shape[0]
    tn = min(tn, T)
    fidx, vmask = _peer_cand_tables(tn)
    rows = PEER_HEADS * PEER_N_KEYS
    ospec = pl.BlockSpec((PEER_N_KEYS, tn), lambda i, h: (h, i))
    return pl.pallas_call(
        _peer_select_kernel,
        grid=(T // tn, PEER_HEADS),
        in_specs=[pl.BlockSpec((tn, 2 * PEER_HALF_DIM), lambda i, h: (i, h)),
                  pl.BlockSpec((2, PEER_N_KEYS, PEER_HALF_DIM), lambda i, h: (h, 0, 0)),
                  pl.BlockSpec((80, tn), lambda i, h: (0, 0)),
                  pl.BlockSpec((80, tn), lambda i, h: (0, 0))],
        out_specs=[ospec] * 4,
        out_shape=[jax.ShapeDtypeStruct((rows, T), dt) for dt in (F32, BF16, F32, BF16)],
        compiler_params=_cparams(("parallel", "parallel")),
    )(q, keys, fidx, vmask)


def _peer_dense_kernel(xn_ref, u_ref, vt_ref, cnt_ref, rank2_ref, e1_ref, e2_ref, res_ref, o_ref,
                       acc_ref, g0_ref, g1_ref, *, c_per_step):
    j = pl.program_id(1)
    n_tiles = pl.num_programs(1) - 1

    @pl.when(j == 0)
    def _():
        acc_ref[...] = jnp.zeros_like(acc_ref)
        g1_ref[...] = jnp.zeros_like(g1_ref)

    @pl.when((j % 2 == 0) & (j < n_tiles))
    def _():
        _peer_dense_step(xn_ref, u_ref, vt_ref, cnt_ref, rank2_ref, e1_ref, e2_ref, acc_ref,
                         g1_ref, g0_ref, j, c_per_step)

    @pl.when(j % 2 == 1)
    def _():
        _peer_dense_step(xn_ref, u_ref, vt_ref, cnt_ref, rank2_ref, e1_ref, e2_ref, acc_ref,
                         g0_ref, g1_ref, j, c_per_step)

    @pl.when(j == n_tiles)
    def _():
        _peer_dense_step(xn_ref, u_ref, vt_ref, cnt_ref, rank2_ref, e1_ref, e2_ref, acc_ref,
                         g1_ref, None, j, c_per_step)
        o_ref[...] = res_ref[...] + acc_ref[...].T


def _peer_dense_step(xn_ref, u_ref, vt_ref, cnt_ref, rank2_ref, e1_ref, e2_ref, acc_ref, g_ref, g_next_ref,
                     j, c_per_step):
    tn = xn_ref.shape[1]
    bf16_rows = 16
    reps = PEER_N_KEYS // bf16_rows

    def row_tile(ref, row):
        r16 = jnp.broadcast_to(ref[pl.ds(row, 1), :], (bf16_rows, tn)).astype(BF16)
        return jnp.concatenate([r16] * reps, axis=0)

    c0 = jnp.maximum(j - 1, 0) * c_per_step
    up_rows = 4 * PEER_N_KEYS
    blocks = []
    for cc in range(c_per_step):
        if g_next_ref is not None and (cc * PEER_N_KEYS) % up_rows == 0:
            rs = slice(cc * PEER_N_KEYS, cc * PEER_N_KEYS + up_rows)
            hT = jnp.dot(u_ref[rs, :], xn_ref[...], preferred_element_type=F32)
            g_next_ref[rs, :] = _gelu_sigmoid(hT.astype(BF16))
        c = c0 + cc
        w = None
        for h in range(PEER_HEADS):
            row = h * PEER_N_KEYS + c
            n_row = row_tile(cnt_ref, row)
            e1_row = row_tile(e1_ref, row)
            sl = slice(h * PEER_N_KEYS, (h + 1) * PEER_N_KEYS)
            term = jnp.where(rank2_ref[sl, :] < n_row, e2_ref[sl, :], 0.0) * e1_row
            w = term if w is None else w + term
        blocks.append(w * g_ref[cc * PEER_N_KEYS:(cc + 1) * PEER_N_KEYS, :])
    aT = jnp.concatenate(blocks, axis=0)
    acc_ref[...] += jnp.dot(vt_ref[...], aT, preferred_element_type=F32)


def peer_dense(xn, u, vt, cnt, rank2, e1, e2, res, *, tn=512, te=2048):
    D, T = xn.shape
    E = u.shape[0]
    tn = min(tn, T)
    rows = PEER_HEADS * PEER_N_KEYS
    sel_spec = pl.BlockSpec((rows, tn), lambda i, j: (0, i))
    n_tiles = E // te
    assert n_tiles % 2 == 0
    return pl.pallas_call(
        functools.partial(_peer_dense_kernel, c_per_step=te // PEER_N_KEYS),
        grid=(T // tn, n_tiles + 1),
        in_specs=[pl.BlockSpec((D, tn), lambda i, j: (0, i)),
                  pl.BlockSpec((te, D), lambda i, j: (jnp.minimum(j, n_tiles - 1), 0)),
                  pl.BlockSpec((D, te), lambda i, j: (0, jnp.maximum(j - 1, 0))),
                  sel_spec, sel_spec, sel_spec, sel_spec,
                  pl.BlockSpec((tn, D), lambda i, j: (i, 0))],
        out_specs=pl.BlockSpec((tn, D), lambda i, j: (i, 0)),
        out_shape=jax.ShapeDtypeStruct((T, D), F32),
        scratch_shapes=[pltpu.VMEM((D, tn), F32), pltpu.VMEM((te, tn), BF16), pltpu.VMEM((te, tn), BF16)],
        compiler_params=_cparams(("parallel", "arbitrary")),
    )(xn, u, vt, cnt, rank2, e1, e2, res)


def peer_layer(h, gamma, w_q, sub_keys, u, v):
    q, xn = norm_matmul(h, gamma, w_q.astype(BF16), emit_xn=True)
    keys = sub_keys.reshape(2 * PEER_HEADS, PEER_N_KEYS, PEER_HALF_DIM).astype(BF16)
    cnt, rank2, e1, e2 = peer_select(q, keys)
    return peer_dense(xn, u.astype(BF16), v.T.astype(BF16), cnt, rank2, e1, e2, h)


def _cumsum_aug_kernel(lf_ref, tri_ref, place_q_ref, place_k_ref, ones_q_ref, ones_k_ref,
                       qa_ref, ka_ref, c_ref, carry_ref):
    @pl.when(pl.program_id(1) == 0)
    def _():
        carry_ref[...] = jnp.zeros_like(carry_ref)

    lf = lf_ref[0]
    c = jnp.dot(tri_ref[...], lf, preferred_element_type=F32, precision=lax.Precision.HIGHEST) + carry_ref[...]
    carry_ref[...] = c[-1:, :]
    c = c * LOG2E
    c_ref[0] = c
    hi = c.astype(BF16)
    r1 = c - hi.astype(F32)
    mid = r1.astype(BF16)
    lo = (r1 - mid.astype(F32)).astype(BF16)
    nh = N_HEADS
    lane = lax.broadcasted_iota(jnp.int32, c.shape, 1)
    parts = jnp.where(lane < nh, hi.astype(F32),
                      jnp.where(lane < 2 * nh, pltpu.roll(mid.astype(F32), nh, axis=1),
                                pltpu.roll(lo.astype(F32), 2 * nh, axis=1)))
    parts = jnp.where(lane < 3 * nh, parts, 0.0).astype(BF16)
    qa_ref[0] = (jnp.dot(parts, place_q_ref[...], preferred_element_type=F32) + ones_q_ref[...]).astype(BF16)
    ka_ref[0] = (jnp.dot(parts, place_k_ref[...], preferred_element_type=F32) + ones_k_ref[...]).astype(BF16)


def fox_bias_operands(logf, *, tc=256):
    B, S, _ = logf.shape
    nh = N_HEADS
    tri = jnp.asarray(np.tril(np.ones((tc, tc), np.float32)))
    pq = np.zeros((LANES, nh * LANES), np.float32)
    pk = np.zeros((LANES, nh * LANES), np.float32)
    oq = np.zeros((1, nh * LANES), np.float32)
    ok = np.zeros((1, nh * LANES), np.float32)
    for h in range(nh):
        for part in range(3):
            pq[part * nh + h, h * LANES + part] = 1.0
            pk[part * nh + h, h * LANES + 3 + part] = -1.0
            oq[0, h * LANES + 3 + part] = 1.0
            ok[0, h * LANES + part] = 1.0
    const = lambda a: pl.BlockSpec(a.shape, lambda b, i: (0,) * a.ndim)
    pq, pk, oq, ok = jnp.asarray(pq, BF16), jnp.asarray(pk, BF16), jnp.asarray(oq), jnp.asarray(ok)
    out = jax.ShapeDtypeStruct((B, S, nh * LANES), BF16)
    return pl.pallas_call(
        _cumsum_aug_kernel,
        grid=(B, S // tc),
        in_specs=[pl.BlockSpec((1, tc, LANES), lambda b, i: (b, i, 0)),
                  const(tri), const(pq), const(pk), const(oq), const(ok)],
        out_specs=[pl.BlockSpec((1, tc, nh * LANES), lambda b, i: (b, i, 0))] * 2
        + [pl.BlockSpec((1, tc, LANES), lambda b, i: (b, i, 0))],
        out_shape=[out, out, jax.ShapeDtypeStruct((B, S, LANES), F32)],
        scratch_shapes=[pltpu.VMEM((1, LANES), F32)],
        compiler_params=_cparams(("parallel", "arbitrary")),
    )(logf, tri, pq, pk, oq, ok)


def _fox_attn_kernel(first_ref, q_ref, qa_ref, kv_ref, ka_ref, o_ref, *, tq, tk, heads_per_step):
    qi = pl.program_id(2)
    t0 = qi * tq
    n_full = t0 // tk
    j_first = first_ref[(pl.program_id(0) * pl.num_programs(1) + pl.program_id(1)) * pl.num_programs(2) + qi]
    n_diag = tq // tk
    lanes = [slice(hh * LANES, (hh + 1) * LANES) for hh in range(heads_per_step)]
    qs = [jnp.concatenate([q_ref[0, :, lsl], qa_ref[0, :, lsl]], axis=1) for lsl in lanes]

    def step(j, carry, masked):
        rows = pl.ds(pl.multiple_of(j * tk, tk), tk)
        klane = lax.broadcasted_iota(jnp.int32, (tk, LANES), 1)
        new = []
        for lsl, q, (m, acc) in zip(lanes, qs, carry):
            kv = kv_ref[0, rows, lsl]
            kk = jnp.concatenate([kv, ka_ref[0, rows, lsl]], axis=1)
            ones_v = jnp.where(klane < HEAD_DIM, 1.0, kv).astype(BF16)
            s = lax.dot_general(q, kk, _NT, preferred_element_type=F32)
            if masked:
                qpos = t0 + lax.broadcasted_iota(jnp.int32, (tq, 1), 0)
                kpos = j * tk + lax.broadcasted_iota(jnp.int32, (1, tk), 1)
                s = jnp.where(kpos <= qpos, s, NEG_INF)
            m_new = jnp.maximum(m, jnp.max(s, axis=1, keepdims=True))
            p = jnp.exp2(s - m_new).astype(BF16)
            acc = jnp.exp2(m - m_new) * acc + jnp.dot(p, ones_v, preferred_element_type=F32)
            new.append((m_new, acc))
        return tuple(new)

    init = (jnp.full((tq, 1), NEG_INF, F32), jnp.zeros((tq, LANES), F32))
    n_pairs = (n_full - j_first) // 2
    carry = lax.fori_loop(
        0, n_pairs, lambda i, c: step(j_first + 2 * i + 1, step(j_first + 2 * i, c, False), False),
        (init,) * heads_per_step)
    carry = lax.fori_loop(j_first + 2 * n_pairs, n_full, functools.partial(step, masked=False), carry)
    for d in range(n_diag):
        carry = step(n_full + d, carry, True)
    outs = [acc / acc[:, 0:1] for (_, acc) in carry]
    lane = lax.broadcasted_iota(jnp.int32, (tq, LANES), 1)
    blocks = []
    for pair in range(heads_per_step // 2):
        a, b = outs[2 * pair], outs[2 * pair + 1]
        blocks.append(jnp.where(lane < HEAD_DIM, pltpu.roll(a, HEAD_DIM, axis=1), b))
    o_ref[0] = jnp.concatenate(blocks, axis=1).astype(o_ref.dtype) if len(blocks) > 1 else blocks[0].astype(o_ref.dtype)


def _head_slots(w, n_heads, second=None):
    D = w.shape[0]
    a = w.reshape(D, n_heads, HEAD_DIM)
    b = jnp.zeros_like(a) if second is None else second.reshape(D, n_heads, HEAD_DIM)
    return jnp.concatenate([a, b], axis=-1).reshape(D, n_heads * LANES)


def fox_layer(h, gamma, w_in, f_bias, w_out, B, S):
    aw = ATTN_WIDTH
    wq = _head_slots(w_in[:, :aw] * (HEAD_DIM ** -0.5 * LOG2E), N_HEADS)
    wkv = _head_slots(w_in[:, aw:2 * aw], N_HEADS, w_in[:, 2 * aw:3 * aw])
    w_main = jnp.concatenate([wq, wkv], axis=1).astype(BF16)
    wf = jnp.pad(w_in[:, 3 * aw:], ((0, 0), (0, LANES - N_HEADS))).astype(BF16)
    bf = jnp.pad(f_bias.astype(F32), (0, LANES - N_HEADS))
    qkv = norm_matmul(h, gamma, w_main)
    logf = norm_matmul(h, gamma, wf, bias=bf, act="log_sigmoid", out_dtype=F32)
    qa, ka, c2 = fox_bias_operands(logf.reshape(B, S, LANES))
    qkv = qkv.reshape(B, S, 2 * N_HEADS * LANES)
    o = fox_attention(qkv, qa, ka, c2)
    return matmul_residual(o.reshape(B * S, aw), w_out.astype(BF16), h)


FOX_NEGLIGIBLE_LOG2 = 160.0


def _norm_maxima_kernel(x_ref, ind_ref, o_ref):
    x = x_ref[0].astype(F32)
    ss = jnp.dot((x * x).astype(BF16), ind_ref[...], preferred_element_type=F32)
    o_ref[0, 0] = jnp.broadcast_to(jnp.max(ss, axis=0, keepdims=True), o_ref.shape[2:])


def fox_norm_maxima(qkv, tile):
    B, S, W = qkv.shape
    n = S // tile
    col = np.arange(W)
    slot, lane = col // LANES, col % LANES
    used = (slot < N_HEADS) | (lane < HEAD_DIM)
    ind = jnp.asarray((slot[:, None] == np.arange(LANES)[None, :]) & used[:, None], BF16)
    rows = min(256, tile)
    ss = pl.pallas_call(
        _norm_maxima_kernel,
        grid=(B, S // rows),
        in_specs=[pl.BlockSpec((1, rows, W), lambda b, i: (b, i, 0)),
                  pl.BlockSpec(ind.shape, lambda b, i: (0, 0))],
        out_specs=pl.BlockSpec((1, 1, 8, LANES), lambda b, i: (b, i, 0, 0)),
        out_shape=jax.ShapeDtypeStruct((B, S // rows, 8, LANES), F32),
        compiler_params=_cparams(("parallel", "parallel")),
    )(qkv, ind)
    ss = jnp.max(ss[:, :, 0, :2 * N_HEADS].reshape(B, n, tile // rows, 2 * N_HEADS), axis=2)
    return jnp.sqrt(ss * 1.01)


def fox_first_chunk(qkv, c2, tile, heads_per_step):
    B, S, _ = qkv.shape
    n = S // tile
    norms = fox_norm_maxima(qkv, tile)
    qmax, kmax = norms[..., :N_HEADS], norms[..., N_HEADS:]
    c = c2[..., :N_HEADS].reshape(B, n, tile, N_HEADS)
    cmax, cmin = jnp.max(c, axis=2), jnp.min(c, axis=2)
    upper = qmax[:, :, None] * kmax[:, None, :] + cmax[:, :, None] - cmin[:, None, :]
    own = -(qmax * kmax)
    earlier = jnp.arange(n)[None, :, None, None] > jnp.arange(n)[None, None, :, None]
    skip = (upper < own[:, :, None] - FOX_NEGLIGIBLE_LOG2) & earlier
    first = jnp.sum(jnp.cumprod(skip.astype(jnp.int32), axis=2), axis=2)
    first = jnp.min(first.reshape(B, n, N_HEADS // heads_per_step, heads_per_step), axis=-1)
    return jnp.transpose(first, (0, 2, 1)).reshape(-1).astype(jnp.int32)


def fox_attention(qkv, qa, ka, c2, *, tq=1024, tk=1024, heads_per_step=2):
    B, S, _ = qkv.shape
    tq = tk = min(tq, S)
    hs = heads_per_step
    wq = hs * LANES
    n_qblk = N_HEADS // hs
    first = fox_first_chunk(qkv, c2, tk, hs)
    grid_spec = pltpu.PrefetchScalarGridSpec(
        num_scalar_prefetch=1,
        grid=(B, n_qblk, S // tq),
        in_specs=[pl.BlockSpec((1, tq, wq), lambda b, h, i, first: (b, i, h)),
                  pl.BlockSpec((1, tq, wq), lambda b, h, i, first: (b, i, h)),
                  pl.BlockSpec((1, S, wq), lambda b, h, i, first: (b, 0, n_qblk + h)),
                  pl.BlockSpec((1, S, wq), lambda b, h, i, first: (b, 0, h))],
        out_specs=pl.BlockSpec((1, tq, hs * HEAD_DIM), lambda b, h, i, first: (b, i, h)))
    return pl.pallas_call(
        functools.partial(_fox_attn_kernel, tq=tq, tk=tk, heads_per_step=hs),
        grid_spec=grid_spec,
        out_shape=jax.ShapeDtypeStruct((B, S, ATTN_WIDTH), BF16),
        compiler_params=_cparams(("parallel", "parallel", "arbitrary")),
    )(first, qkv, qa, qkv, ka)


def _rot_half_cols(w):
    D = w.shape[0]
    a = w.reshape(D, -1, HEAD_DIM)
    half = HEAD_DIM // 2
    return jnp.concatenate([-a[..., half:], a[..., :half]], axis=-1).reshape(w.shape)


def _rope_tables(S):
    half = HEAD_DIM // 2
    inv_freq = ROPE_THETA ** (-jnp.arange(half, dtype=F32) / half)
    ang = jnp.arange(S, dtype=F32)[:, None] * inv_freq[None, :]
    c, s = jnp.cos(ang), jnp.sin(ang)
    c2, s2 = jnp.concatenate([c, c], axis=1), jnp.concatenate([s, s], axis=1)
    cos = jnp.stack([jnp.concatenate([c2, jnp.ones_like(c2)], axis=1), jnp.concatenate([c2, c2], axis=1)])
    sin = jnp.stack([jnp.concatenate([s2, jnp.zeros_like(s2)], axis=1), jnp.concatenate([s2, s2], axis=1)])
    return cos, sin


def _compress_kernel(x_ref, pea_ref, peb_ref, wa_ref, wb_ref, w2_ref, o_ref, pa_ref, pb0_ref, *, n_rows):
    u = pl.program_id(1)
    x = x_ref[0].astype(F32)
    pa = jnp.dot((x + pea_ref[...]).astype(BF16), wa_ref[...], preferred_element_type=F32)
    pb = jnp.dot((x + peb_ref[...]).astype(BF16), wb_ref[...], preferred_element_type=F32)

    def emit(slab, hid):
        y = jnp.dot(_gelu_tanh(hid).astype(BF16), w2_ref[...], preferred_element_type=F32)
        o_ref[0, pl.ds(pl.multiple_of(slab * n_rows, n_rows), n_rows), :] = y

    @pl.when(u == 0)
    def _():
        pb0_ref[...] = pb

    @pl.when(u > 0)
    def _():
        emit(u - 1, pa_ref[...] + pb)

    @pl.when(u == 3)
    def _():
        emit(3, pa + pltpu.roll(pb0_ref[...], n_rows - 1, axis=0))

    pa_ref[...] = pa


def nsa_compress(src, pe, w1, w2):
    B, S, W = src.shape
    G = NSA_GROUPS
    n_rows = S // 64
    half = CMP_BLOCK // 2
    cw = half * W
    xv = src.reshape(B, n_rows, 4 * cw)
    pe_flat = jnp.transpose(pe, (1, 0, 2)).reshape(CMP_BLOCK, W).astype(F32)
    pea, peb = pe_flat[:half].reshape(1, cw), pe_flat[half:].reshape(1, cw)
    eye = jnp.eye(G, dtype=F32)
    wfull = jnp.einsum('gldh,gk->lkdgh', w1.astype(F32), eye).reshape(CMP_BLOCK, W, G * CMP_HIDDEN)
    wa = wfull[:half].reshape(cw, G * CMP_HIDDEN).astype(BF16)
    wb = wfull[half:].reshape(cw, G * CMP_HIDDEN).astype(BF16)
    w2bd = jnp.einsum('ghd,gk->ghkd', w2.astype(F32), eye).reshape(G * CMP_HIDDEN, W).astype(BF16)
    const = lambda a: pl.BlockSpec(a.shape, lambda b, u: (0,) * a.ndim)
    return pl.pallas_call(
        functools.partial(_compress_kernel, n_rows=n_rows),
        grid=(B, 4),
        in_specs=[pl.BlockSpec((1, n_rows, cw), lambda b, u: (b, 0, u)),
                  const(pea), const(peb), const(wa), const(wb), const(w2bd)],
        out_specs=pl.BlockSpec((1, 4 * n_rows, W), lambda b, u: (b, 0, 0)),
        out_shape=jax.ShapeDtypeStruct((B, 4 * n_rows, W), F32),
        scratch_shapes=[pltpu.VMEM((n_rows, G * CMP_HIDDEN), F32), pltpu.VMEM((n_rows, G * CMP_HIDDEN), F32)],
        compiler_params=_cparams(("parallel", "arbitrary")),
    )(xv, pea, peb, wa, wb, w2bd)


def _nsa_attn_kernel(q_ref, kvs_ref, kvw_ref, kvc_ref, kvct_ref, gate_ref, bmat_ref, pmat_ref, o_ref,
                     *, tq, ts, tk, seq):
    R = NSA_Q_PER_GROUP
    n_slc = seq // SLC_BLOCK
    n_sel = min(SLC_TOPK, n_slc)
    assert ts & (ts - 1) == 0 and n_slc & (n_slc - 1) == 0 and tk % tq == 0 and tq % ts == 0
    qi = pl.program_id(2)
    t0 = qi * tq

    local = [_nsa_local_branches(q_ref, kvw_ref, kvc_ref, kvct_ref, t0 + i * ts, i * ts, ts, n_slc, n_sel)
             for i in range(tq // ts)]
    o_c = jnp.concatenate([o[0][r * ts:(r + 1) * ts] for r in range(R) for o in local], axis=0)
    o_w = jnp.concatenate([o[1][r * ts:(r + 1) * ts] for r in range(R) for o in local], axis=0)
    sel_bias = jnp.concatenate([o[2] for o in local], axis=0)
    qs = jnp.concatenate([q_ref[0, :, r * LANES:(r + 1) * LANES] for r in range(R)], axis=0)
    _nsa_selected_and_combine(qs, o_c, o_w, sel_bias, kvs_ref, gate_ref, bmat_ref, pmat_ref, o_ref,
                              t0=t0, tq=tq, tk=tk, seq=seq)


def _nsa_local_branches(q_ref, kvw_ref, kvc_ref, kvct_ref, t0, row0, tq, n_slc, n_sel):
    R = NSA_Q_PER_GROUP
    M = R * tq
    qs = jnp.concatenate([q_ref[0, row0:row0 + tq, r * LANES:(r + 1) * LANES] for r in range(R)], axis=0)

    wlen = WINDOW + tq
    start = jnp.maximum(t0 - WINDOW, 0)
    kvw = kvw_ref[0, pl.ds(pl.multiple_of(start, tq), wlen), :]
    s_w = lax.dot_general(qs, kvw, _NT, preferred_element_type=F32)
    qpos = t0 + (lax.broadcasted_iota(jnp.int32, (M, 1), 0) & (tq - 1))
    kpos = start + lax.broadcasted_iota(jnp.int32, (1, wlen), 1)
    s_w = jnp.where(kpos <= qpos, jnp.where(kpos > qpos - WINDOW, s_w, NEG_INF), NEG_INF)
    p_w = jnp.exp2(s_w - jnp.max(s_w, axis=1, keepdims=True)).astype(BF16)
    wlane = lax.broadcasted_iota(jnp.int32, (wlen, LANES), 1)
    acc_w = jnp.dot(p_w, jnp.where(wlane < HEAD_DIM, 1.0, kvw).astype(BF16), preferred_element_type=F32)
    o_w = acc_w / acc_w[:, 0:1]

    n_groups, gs = _nsa_row_groups(n_slc)
    o_c, sel_bias = _nsa_compressed_and_select(qs, kvc_ref, kvct_ref, t0, tq, n_slc, n_sel, gs, n_groups)
    return o_c, o_w, sel_bias


def _nsa_row_groups(n_slc):
    n_groups = max(1, min(4, n_slc // 8))
    assert n_slc % n_groups == 0 and (n_slc // n_groups) % 8 == 0
    return n_groups, n_slc // n_groups


def _nsa_compressed_and_select(qs, kvc_ref, kvct_ref, t0, tq, n_slc, n_sel, gs, n_groups):
    R = NSA_Q_PER_GROUP
    M = R * tq
    n_rows, n_j = n_groups * 4 * gs, n_groups * gs
    log_gs = gs.bit_length() - 1
    sT = lax.dot_general(kvc_ref[0, 0, 0:n_rows, :], qs, _NT, preferred_element_type=F32)
    rowc = lax.broadcasted_iota(jnp.int32, (n_rows, 1), 0)
    blk_j = ((rowc >> (log_gs + 2)) << log_gs) + (rowc & (gs - 1))
    blk_u = (rowc >> log_gs) & 3
    cmp_end = blk_j * SLC_BLOCK + blk_u * CMP_STRIDE + (CMP_BLOCK - 1)
    tcol = t0 + (lax.broadcasted_iota(jnp.int32, (1, M), 1) & (tq - 1))
    sm = jnp.where(cmp_end <= tcol, sT, NEG_INF)
    mx = jnp.max(sm, axis=0, keepdims=True)
    e = jnp.exp2(sm - mx)
    inv = jnp.where(mx > 0.5 * NEG_INF, 1.0 / jnp.sum(e, axis=0, keepdims=True), 0.0)
    pT = e * inv
    o_c = jnp.dot(kvct_ref[0, 0, :, 0:n_rows], pT.astype(BF16), preferred_element_type=F32).T

    psum = pT[:, 0:tq]
    for r in range(1, R):
        psum = psum + pT[:, r * tq:(r + 1) * tq]
    p0, p1, p2, p3 = (jnp.concatenate([psum[(4 * g + u) * gs:(4 * g + u + 1) * gs, :] for g in range(n_groups)],
                                      axis=0) for u in range(4))
    jrow = lax.broadcasted_iota(jnp.int32, (n_j, tq), 0)
    p3_prev = jnp.where(jrow == 0, 0.0, pltpu.roll(p3, 1, axis=0))
    imp = p0 + p1 + p2 + 0.5 * p3 + 0.5 * p3_prev
    cur = (t0 + lax.broadcasted_iota(jnp.int32, (n_j, tq), 1)) >> (SLC_BLOCK.bit_length() - 1)
    forced = (jrow == 0) | (jrow == cur) | (jrow == cur - 1)
    vals = jnp.where(forced, -jnp.inf, jnp.where(jrow <= cur, imp, -jnp.inf))
    sel0 = jnp.where(forced, 1.0, 0.0)
    n_free = n_sel - 3

    picked = -2.0 ** 100

    def pick(exact_ties):
        v = vals
        for _ in range(n_free):
            m = jnp.max(v, axis=0, keepdims=True)
            if exact_ties:
                hit = jrow == jnp.min(jnp.where(v == m, jrow, n_slc), axis=0, keepdims=True)
            else:
                hit = v == jnp.where(m < 0.0, jnp.nan, m)
            v = jnp.where(hit, picked, v)
        return jnp.where(v == picked, 1.0, sel0)

    sel_fast = pick(False)
    n_cand = jnp.sum(jnp.where(vals > -jnp.inf, 1.0, 0.0), axis=0, keepdims=True)
    n_picked = jnp.sum(sel_fast - sel0, axis=0, keepdims=True)
    unique = jnp.min(jnp.where(n_picked == jnp.minimum(n_cand, float(n_free)), 1.0, 0.0)) > 0.5
    sel = lax.cond(unique, lambda: sel_fast, lambda: pick(True))
    if n_j < n_slc:
        sel = jnp.concatenate([sel, jnp.zeros((n_slc - n_j, tq), F32)], axis=0)
    sel_bias = ((sel.T - 1.0) * (-NEG_INF)).astype(BF16)
    return o_c, sel_bias


def _nsa_selected_and_combine(qs, o_c, o_w, sel_bias, kvs_ref, gate_ref, bmat_ref, pmat_ref, o_ref,
                              *, t0, tq, tk, seq):
    R = NSA_Q_PER_GROUP
    n_slc = seq // SLC_BLOCK
    blocks_per_chunk = tk // SLC_BLOCK
    n_parts = 2
    hp = R // n_parts
    mp = hp * tq
    qparts = [qs[i * mp:(i + 1) * mp] for i in range(n_parts)]
    qrow = t0 + (lax.broadcasted_iota(jnp.int32, (mp, 1), 0) & (tq - 1))
    kcol = lax.broadcasted_iota(jnp.int32, (1, tk), 1)
    n_chunks = seq // tk
    j_last = t0 // tk

    klane = lax.broadcasted_iota(jnp.int32, (tk, LANES), 1)

    def slc_step(j, carry, masked):
        kv = kvs_ref[0, pl.ds(pl.multiple_of(j * tk, tk), tk), :]
        k_sel = jnp.where(klane < HEAD_DIM, kv, bmat_ref[...]).astype(BF16)
        ones_v = jnp.where(klane < HEAD_DIM, 1.0, kv).astype(BF16)
        off = pl.multiple_of(blocks_per_chunk * (n_chunks - 1 - j), blocks_per_chunk)
        place = pmat_ref[pl.ds(off, n_slc), :].astype(BF16)
        q_bias = jnp.dot(sel_bias, place, preferred_element_type=F32).astype(BF16)
        q_bias = jnp.concatenate([q_bias] * hp, axis=0)
        new = []
        for qp, (m, acc) in zip(qparts, carry):
            s = lax.dot_general(qp + q_bias, k_sel, _NT, preferred_element_type=F32)
            if masked:
                s = jnp.where(j * tk + kcol <= qrow, s, NEG_INF)
            m_new = jnp.maximum(m, jnp.max(s, axis=1, keepdims=True))
            p = jnp.exp2(s - m_new).astype(BF16)
            acc = jnp.exp2(m - m_new) * acc + jnp.dot(p, ones_v, preferred_element_type=F32)
            new.append((m_new, acc))
        return tuple(new)

    init = (jnp.full((mp, 1), NEG_INF, F32), jnp.zeros((mp, LANES), F32))
    carry = lax.fori_loop(0, j_last, functools.partial(slc_step, masked=False), (init,) * n_parts)
    carry = slc_step(j_last, carry, True)
    o_s = jnp.concatenate([acc / acc[:, 0:1] for (_, acc) in carry], axis=0)

    gates = gate_ref[0]
    lane = lax.broadcasted_iota(jnp.int32, (tq, LANES), 1)
    comb = []
    for r in range(R):
        rs = slice(r * tq, (r + 1) * tq)
        comb.append(gates[:, r:r + 1] * o_c[rs] + gates[:, R + r:R + r + 1] * o_s[rs]
                    + gates[:, 2 * R + r:2 * R + r + 1] * o_w[rs])
    out = [jnp.where(lane < HEAD_DIM, pltpu.roll(comb[2 * i], HEAD_DIM, axis=1), comb[2 * i + 1])
           for i in range(R // 2)]
    o_ref[0] = jnp.concatenate(out, axis=1).astype(o_ref.dtype)


def nsa_attention(qkv, kvc, kvct, gates, *, tq=512, ts=256, tk=1024):
    B, S, _ = qkv.shape
    G, R = NSA_GROUPS, NSA_Q_PER_GROUP
    tk = min(tk, S)
    n_slc = S // SLC_BLOCK
    n_cmp = kvc.shape[2]
    bpc = tk // SLC_BLOCK
    assert bpc <= LANES - HEAD_DIM
    off = bpc * (S // tk - 1)
    lane = np.arange(LANES)[None, :]
    bmat = jnp.asarray(lane - HEAD_DIM == np.arange(tk)[:, None] // SLC_BLOCK, BF16)
    pmat = jnp.asarray((np.arange(n_slc + off)[:, None] - off == lane - HEAD_DIM) & (lane >= HEAD_DIM)
                       & (lane < HEAD_DIM + bpc), F32)
    slc_blk0, win_blk0 = N_HEADS, N_HEADS + G
    return pl.pallas_call(
        functools.partial(_nsa_attn_kernel, tq=tq, ts=ts, tk=tk, seq=S),
        grid=(B, G, S // tq),
        in_specs=[pl.BlockSpec((1, tq, R * LANES), lambda b, g, i: (b, i, g)),
                  pl.BlockSpec((1, S, LANES), lambda b, g, i: (b, 0, slc_blk0 + g)),
                  pl.BlockSpec((1, S, LANES), lambda b, g, i: (b, 0, win_blk0 + g)),
                  pl.BlockSpec((1, 1, n_cmp, LANES), lambda b, g, i: (b, g, 0, 0)),
                  pl.BlockSpec((1, 1, LANES, n_cmp), lambda b, g, i: (b, g, 0, 0)),
                  pl.BlockSpec((1, tq, LANES), lambda b, g, i: (b, i, g)),
                  pl.BlockSpec(bmat.shape, lambda b, g, i: (0, 0)),
                  pl.BlockSpec(pmat.shape, lambda b, g, i: (0, 0))],
        out_specs=pl.BlockSpec((1, tq, R * HEAD_DIM), lambda b, g, i: (b, i, g)),
        out_shape=jax.ShapeDtypeStruct((B, S, ATTN_WIDTH), BF16),
        compiler_params=_cparams(("parallel", "parallel", "arbitrary")),
    )(qkv, qkv, qkv, kvc, kvct, gates, bmat, pmat)


def nsa_layer(h, gamma, w_in, pe_k, w1_k, w2_k, pe_v, w1_v, w2_v, w_out, B, S):
    G, R, hd, aw = NSA_GROUPS, NSA_Q_PER_GROUP, HEAD_DIM, ATTN_WIDTH
    kvd = G * hd
    sec = lambda i: w_in[:, aw + i * kvd: aw + (i + 1) * kvd]
    wq = w_in[:, :aw] * (hd ** -0.5 * LOG2E)
    wa = jnp.concatenate([_head_slots(wq, N_HEADS), _head_slots(sec(2), G, sec(3)),
                          _head_slots(sec(4), G, sec(5))], axis=1).astype(BF16)
    wb = jnp.concatenate([_rot_half_cols(wq), _rot_half_cols(sec(2)), _rot_half_cols(sec(4))], axis=1).astype(BF16)
    cos, sin = _rope_tables(S)
    qkv = norm_matmul(h, gamma, wa, wb=wb, cos=cos, sin=sin)
    kc_src = norm_matmul(h, gamma, sec(0).astype(BF16), wb=_rot_half_cols(sec(0)).astype(BF16),
                         cos=cos, sin=sin, table_of_tile=lambda j: 1)
    vc_src = norm_matmul(h, gamma, sec(1).astype(BF16))
    wg = w_in[:, aw + 6 * kvd:].reshape(-1, 3, G, R)
    wg = jnp.transpose(wg, (0, 2, 1, 3)).reshape(-1, G, 3 * R)
    wg = jnp.pad(wg, ((0, 0), (0, 0), (0, LANES - 3 * R))).reshape(-1, G * LANES).astype(BF16)
    gates = norm_matmul(h, gamma, wg, act="sigmoid", out_dtype=F32)
    kc = nsa_compress(kc_src.reshape(B, S, kvd), pe_k, w1_k, w2_k)
    vc = nsa_compress(vc_src.reshape(B, S, kvd), pe_v, w1_v, w2_v)
    n_cmp = kc.shape[1]
    kvc = jnp.concatenate([kc.reshape(B, n_cmp, G, hd), vc.reshape(B, n_cmp, G, hd)], axis=-1)
    n_groups, gs = _nsa_row_groups(S // SLC_BLOCK)
    kvc = kvc.reshape(B, 4, n_groups, gs, G, 2 * hd)
    kvc = jnp.transpose(kvc, (0, 4, 2, 1, 3, 5)).reshape(B, G, n_cmp, 2 * hd).astype(BF16)
    kvct = jnp.swapaxes(kvc, 2, 3)
    o = nsa_attention(qkv.reshape(B, S, -1), kvc, kvct, gates.reshape(B, S, G * LANES))
    return matmul_residual(o.reshape(B * S, aw), w_out.astype(BF16), h)


def kernel(x, l0_attn_norm, l0_w_in, l0_cmp_pe_k, l0_cmp_w1_k, l0_cmp_w2_k, l0_cmp_pe_v, l0_cmp_w1_v,
           l0_cmp_w2_v, l0_w_out, l0_ffn_norm, l0_peer_wq, l0_peer_keys, l0_peer_u, l0_peer_v,
           l1_attn_norm, l1_w_in, l1_f_bias, l1_w_out, l1_ffn_norm, l1_peer_wq, l1_peer_keys, l1_peer_u,
           l1_peer_v, final_norm):
    B, S, D = x.shape
    h = x.reshape(B * S, D)
    h = nsa_layer(h, l0_attn_norm, l0_w_in, l0_cmp_pe_k, l0_cmp_w1_k, l0_cmp_w2_k, l0_cmp_pe_v, l0_cmp_w1_v,
                  l0_cmp_w2_v, l0_w_out, B, S)
    h = peer_layer(h, l0_ffn_norm, l0_peer_wq, l0_peer_keys, l0_peer_u, l0_peer_v)
    h = fox_layer(h, l1_attn_norm, l1_w_in, l1_f_bias, l1_w_out, B, S)
    h = peer_layer(h, l1_ffn_norm, l1_peer_wq, l1_peer_keys, l1_peer_u, l1_peer_v)
    return rmsnorm(h, final_norm).reshape(B, S, D)
```

```python
import functools

import numpy as np
import jax
import jax.numpy as jnp
from jax import lax
from jax.experimental import pallas as pl
from jax.experimental.pallas import tpu as pltpu

F32 = jnp.float32
BF16 = jnp.bfloat16

D_MODEL = 1024
N_HEADS = 16
HEAD_DIM = 64
ATTN_WIDTH = N_HEADS * HEAD_DIM
NSA_GROUPS = 4
NSA_Q_PER_GROUP = N_HEADS // NSA_GROUPS
CMP_BLOCK = 32
CMP_STRIDE = 16
CMP_HIDDEN = 2 * HEAD_DIM
SLC_BLOCK = 64
SLC_TOPK = 16
WINDOW = 512
FORCE_SCORE = 1.0e4
ROPE_THETA = 10000.0
PEER_HEADS = 8
PEER_N_KEYS = 128
PEER_TOPK = 16
PEER_HALF_DIM = 128
RMS_EPS = 1e-6
NEG_INF = -1e30
LOG2E = 1.4426950408889634

LANES = 128
VMEM_LIMIT_BYTES = 56 * 1024 * 1024

_NT = (((1,), (1,)), ((), ()))


def _cparams(sem, vmem=VMEM_LIMIT_BYTES, flags=None):
    return pltpu.CompilerParams(dimension_semantics=sem, vmem_limit_bytes=vmem, flags=flags)


def _gelu_tanh(x):
    return 0.5 * x * (1.0 + jnp.tanh(0.7978845608028654 * (x + 0.044715 * (x * x * x))))


def _gelu_sigmoid(x):
    c = -2.0 * 0.7978845608028654 * LOG2E
    t = x * (c + (c * 0.044715) * (x * x))
    return x / (1.0 + jnp.exp2(t))


def _rms_rows(x, g):
    ms = jnp.mean(x * x, axis=-1, keepdims=True)
    return x * lax.rsqrt(ms + RMS_EPS) * g


def _norm_mm_kernel(*refs, act, has_bias, rope, emit_xn):
    it = iter(refs)
    x_ref, g_ref = next(it), next(it)
    wa_ref = next(it)
    wb_ref = next(it) if rope else None
    cos_ref = next(it) if rope else None
    sin_ref = next(it) if rope else None
    b_ref = next(it) if has_bias else None
    o_ref = next(it)
    xo_ref = next(it) if emit_xn else None
    xn_ref = next(it)

    @pl.when(pl.program_id(1) == 0)
    def _():
        xn = _rms_rows(x_ref[...], g_ref[...])
        xn_ref[...] = xn.astype(BF16)
        if emit_xn:
            xo_ref[...] = xn.T.astype(BF16)

    xn = xn_ref[...]
    y = jnp.dot(xn, wa_ref[...], preferred_element_type=F32)
    if rope:
        yb = jnp.dot(xn, wb_ref[...], preferred_element_type=F32)
        cos, sin = cos_ref[...], sin_ref[...]
        packed_b = 2 * yb.shape[1] == y.shape[1]
        for s in range(y.shape[1] // LANES):
            sl = slice(s * LANES, (s + 1) * LANES)
            if packed_b:
                b = yb[:, (s // 2) * LANES:(s // 2 + 1) * LANES]
                b = pltpu.roll(b, HEAD_DIM, axis=1) if s % 2 else b
            else:
                b = yb[:, sl]
            o_ref[:, sl] = (y[:, sl] * cos + b * sin).astype(o_ref.dtype)
        return
    if has_bias:
        y = y + b_ref[...]
    if act == "sigmoid":
        y = jax.nn.sigmoid(y)
    elif act == "log_sigmoid":
        y = jax.nn.log_sigmoid(y)
    o_ref[...] = y.astype(o_ref.dtype)


def norm_matmul(x, gamma, wa, *, wb=None, cos=None, sin=None, table_of_tile=None, bias=None,
                act=None, out_dtype=BF16, emit_xn=False, tm=1024, tn=1024):
    T, D = x.shape
    N = wa.shape[1]
    tm, tn = min(tm, T), min(tn, N)
    assert T % tm == 0 and N % tn == 0 and tn % LANES == 0
    rope = wb is not None
    in_specs = [pl.BlockSpec((tm, D), lambda i, j: (i, 0)),
                pl.BlockSpec((1, D), lambda i, j: (0, 0)),
                pl.BlockSpec((D, tn), lambda i, j: (0, j))]
    args = [x, gamma.reshape(1, D).astype(F32), wa]
    if rope:
        S = cos.shape[1]
        assert S % tm == 0
        n_pos = S // tm
        tmap = table_of_tile if table_of_tile is not None else (lambda j: 0)
        assert wb.shape[1] in (N, N // 2)
        in_specs += [pl.BlockSpec((D, tn * wb.shape[1] // N), lambda i, j: (0, j)),
                     pl.BlockSpec((None, tm, LANES), lambda i, j: (tmap(j), i % n_pos, 0)),
                     pl.BlockSpec((None, tm, LANES), lambda i, j: (tmap(j), i % n_pos, 0))]
        args += [wb, cos, sin]
    if bias is not None:
        in_specs.append(pl.BlockSpec((1, tn), lambda i, j: (0, j)))
        args.append(bias.reshape(1, N).astype(F32))
    out_shape = [jax.ShapeDtypeStruct((T, N), out_dtype)]
    out_specs = [pl.BlockSpec((tm, tn), lambda i, j: (i, j))]
    if emit_xn:
        out_shape.append(jax.ShapeDtypeStruct((D, T), BF16))
        out_specs.append(pl.BlockSpec((D, tm), lambda i, j: (0, i)))
    res = pl.pallas_call(
        functools.partial(_norm_mm_kernel, act=act, has_bias=bias is not None, rope=rope, emit_xn=emit_xn),
        grid=(T // tm, N // tn),
        in_specs=in_specs,
        out_specs=out_specs,
        out_shape=out_shape,
        scratch_shapes=[pltpu.VMEM((tm, D), BF16)],
        compiler_params=_cparams(("parallel", "arbitrary")),
    )(*args)
    return res if emit_xn else res[0]


def _mm_res_kernel(a_ref, w_ref, r_ref, o_ref):
    o_ref[...] = r_ref[...] + jnp.dot(a_ref[...], w_ref[...], preferred_element_type=F32)


def matmul_residual(a, w, res, *, tm=1024, tn=1024):
    T, K = a.shape
    N = w.shape[1]
    tm, tn = min(tm, T), min(tn, N)
    assert T % tm == 0 and N % tn == 0
    return pl.pallas_call(
        _mm_res_kernel,
        grid=(T // tm, N // tn),
        in_specs=[pl.BlockSpec((tm, K), lambda i, j: (i, 0)),
                  pl.BlockSpec((K, tn), lambda i, j: (0, j)),
                  pl.BlockSpec((tm, tn), lambda i, j: (i, j))],
        out_specs=pl.BlockSpec((tm, tn), lambda i, j: (i, j)),
        out_shape=jax.ShapeDtypeStruct((T, N), F32),
        compiler_params=_cparams(("parallel", "arbitrary")),
    )(a, w, res)


def _peer_cand_tables(tn):
    fidx, vmask = [], []
    for k2 in range(16):
        fidx.append(k2); vmask.append(0.0)
    for k1 in range(1, 8):
        lim = PEER_TOPK // (k1 + 1)
        for k2 in range(8):
            fidx.append(k1 * 16 + k2); vmask.append(0.0 if k2 < lim else -np.inf)
    for k1 in range(8, 16):
        fidx.append(k1 * 16); vmask.append(0.0)
    fidx = np.broadcast_to(np.asarray(fidx, np.int32)[:, None], (80, tn))
    vmask = np.broadcast_to(np.asarray(vmask, np.float32)[:, None], (80, tn))
    return jnp.asarray(fidx), jnp.asarray(vmask)


def _top16_rows(s, exact_ties):
    n, tn = s.shape
    rows = lax.broadcasted_iota(jnp.int32, (n, tn), 0)
    rows16 = lax.broadcasted_iota(jnp.int32, (PEER_TOPK, tn), 0)
    tops = jnp.zeros((PEER_TOPK, tn), F32)
    unit = 2.0 ** 122
    v = s
    for k in range(PEER_TOPK):
        m = jnp.max(v, axis=0, keepdims=True)
        if exact_ties:
            hit = rows == jnp.min(jnp.where(v == m, rows, n), axis=0, keepdims=True)
        else:
            hit = v == m
        v = jnp.where(hit, -(32.0 + k) * unit, v)
        tops = jnp.where(rows16 == k, m, tops)
    was_picked = v <= -32.0 * unit
    rank = jnp.where(was_picked, v * (-1.0 / unit) - 32.0, float(PEER_TOPK))
    n_picked = jnp.sum(jnp.where(was_picked, 1.0, 0.0), axis=0, keepdims=True)
    return tops, rank, n_picked


def _peer_select_head(q_ref, keys_ref, fidx, vmask, exact_ties):
    tops, ranks, es, picked = [], [], [], []
    for p in range(2):
        q = q_ref[:, p * PEER_HALF_DIM:(p + 1) * PEER_HALF_DIM]
        s = lax.dot_general(keys_ref[p], q, _NT, preferred_element_type=F32)
        t, r, n_picked = _top16_rows(s, exact_ties)
        tops.append(t); ranks.append(r)
        es.append(jnp.exp(s - t[0:1, :]))
        picked.append(n_picked)
    ts1, ts2 = tops
    pieces = [ts1[0:1, :] + ts2]
    for k1 in range(1, 8):
        pieces.append(ts1[k1:k1 + 1, :] + ts2[0:8, :])
    pieces.append(ts1[8:16, :] + ts2[0:1, :])
    cand0 = jnp.concatenate(pieces, axis=0) + vmask
    cand = cand0
    for _ in range(PEER_TOPK):
        m = jnp.max(cand, axis=0, keepdims=True)
        if exact_ties:
            hit = fidx == jnp.min(jnp.where(cand == m, fidx, 4096), axis=0, keepdims=True)
        else:
            hit = cand == m
        cand = jnp.where(hit, -jnp.inf, cand)
    taken = jnp.logical_and(cand == -jnp.inf, vmask == 0.0)
    takenf = taken.astype(F32)
    picked.append(jnp.sum(takenf, axis=0, keepdims=True))
    unique = jnp.min(jnp.where((picked[0] == PEER_TOPK) & (picked[1] == PEER_TOPK) & (picked[2] == PEER_TOPK),
                               1.0, 0.0)) > 0.5
    best = ts1[0:1, :] + ts2[0:1, :]
    z = jnp.sum(jnp.where(taken, jnp.exp(cand0 - best), 0.0), axis=0, keepdims=True)
    counts = [jnp.sum(takenf[0:16, :], axis=0, keepdims=True)]
    for k1 in range(1, 8):
        counts.append(jnp.sum(takenf[16 + 8 * (k1 - 1):16 + 8 * k1, :], axis=0, keepdims=True))
    tail = takenf[72:80, :]
    cnt = jnp.zeros_like(ranks[0])
    for k1 in range(PEER_TOPK):
        nk = counts[k1] if k1 < 8 else tail[k1 - 8:k1 - 7, :]
        cnt = jnp.where(ranks[0] == float(k1), nk, cnt)
    return (cnt, ranks[1], es[0], es[1] / z), unique


def _peer_select_kernel(q_ref, keys_ref, fidx_ref, vmask_ref, cnt_ref, rank2_ref, e1_ref, e2_ref):
    fidx = fidx_ref[...]
    vmask = vmask_ref[...]

    def store(vals):
        for ref, val in zip((cnt_ref, rank2_ref, e1_ref, e2_ref), vals):
            ref[...] = val.astype(ref.dtype)

    vals, unique = _peer_select_head(q_ref, keys_ref, fidx, vmask, exact_ties=False)
    store(vals)

    @pl.when(jnp.logical_not(unique))
    def _():
        store(_peer_select_head(q_ref, keys_ref, fidx, vmask, exact_ties=True)[0])


def peer_select(q, keys, *, tn=512):
    T = q.shape[0]
    tn = min(tn, T)
    fidx, vmask = _peer_cand_tables(tn)
    rows = PEER_HEADS * PEER_N_KEYS
    ospec = pl.BlockSpec((PEER_N_KEYS, tn), lambda i, h: (h, i))
    return pl.pallas_call(
        _peer_select_kernel,
        grid=(T // tn, PEER_HEADS),
        in_specs=[pl.BlockSpec((tn, 2 * PEER_HALF_DIM), lambda i, h: (i, h)),
                  pl.BlockSpec((2, PEER_N_KEYS, PEER_HALF_DIM), lambda i, h: (h, 0, 0)),
                  pl.BlockSpec((80, tn), lambda i, h: (0, 0)),
                  pl.BlockSpec((80, tn), lambda i, h: (0, 0))],
        out_specs=[ospec] * 4,
        out_shape=[jax.ShapeDtypeStruct((rows, T), dt) for dt in (F32, BF16, F32, BF16)],
        compiler_params=_cparams(("parallel", "parallel")),
    )(q, keys, fidx, vmask)


def _peer_dense_kernel(xn_ref, u_ref, vt_ref, cnt_ref, rank2_ref, e1_ref, e2_ref, res_ref, og_ref, o_ref,
                       acc_ref, g0_ref, g1_ref, *, c_per_step, out_norm):
    j = pl.program_id(1)
    n_tiles = pl.num_programs(1) - 1

    @pl.when(j == 0)
    def _():
        acc_ref[...] = jnp.zeros_like(acc_ref)
        g1_ref[...] = jnp.zeros_like(g1_ref)

    @pl.when((j % 2 == 0) & (j < n_tiles))
    def _():
        _peer_dense_step(xn_ref, u_ref, vt_ref, cnt_ref, rank2_ref, e1_ref, e2_ref, acc_ref,
                         g1_ref, g0_ref, j, c_per_step)

    @pl.when(j % 2 == 1)
    def _():
        _peer_dense_step(xn_ref, u_ref, vt_ref, cnt_ref, rank2_ref, e1_ref, e2_ref, acc_ref,
                         g0_ref, g1_ref, j, c_per_step)

    @pl.when(j == n_tiles)
    def _():
        _peer_dense_step(xn_ref, u_ref, vt_ref, cnt_ref, rank2_ref, e1_ref, e2_ref, acc_ref,
                         g1_ref, None, j, c_per_step)
        out = res_ref[...] + acc_ref[...].T
        o_ref[...] = _rms_rows(out, og_ref[...]) if out_norm else out


def _peer_dense_step(xn_ref, u_ref, vt_ref, cnt_ref, rank2_ref, e1_ref, e2_ref, acc_ref, g_ref, g_next_ref,
                     j, c_per_step):
    tn = xn_ref.shape[1]
    bf16_rows = 16
    reps = PEER_N_KEYS // bf16_rows

    def row_tile(ref, row):
        r16 = jnp.broadcast_to(ref[pl.ds(row, 1), :], (bf16_rows, tn)).astype(BF16)
        return jnp.concatenate([r16] * reps, axis=0)

    c0 = jnp.maximum(j - 1, 0) * c_per_step
    up_rows = 4 * PEER_N_KEYS
    blocks = []
    for cc in range(c_per_step):
        if g_next_ref is not None and (cc * PEER_N_KEYS) % up_rows == 0:
            rs = slice(cc * PEER_N_KEYS, cc * PEER_N_KEYS + up_rows)
            hT = jnp.dot(u_ref[rs, :], xn_ref[...], preferred_element_type=F32)
            g_next_ref[rs, :] = _gelu_sigmoid(hT.astype(BF16))
        c = c0 + cc
        w = None
        for h in range(PEER_HEADS):
            row = h * PEER_N_KEYS + c
            n_row = row_tile(cnt_ref, row)
            e1_row = row_tile(e1_ref, row)
            sl = slice(h * PEER_N_KEYS, (h + 1) * PEER_N_KEYS)
            term = jnp.where(rank2_ref[sl, :] < n_row, e2_ref[sl, :], 0.0) * e1_row
            w = term if w is None else w + term
        blocks.append(w * g_ref[cc * PEER_N_KEYS:(cc + 1) * PEER_N_KEYS, :])
    aT = jnp.concatenate(blocks, axis=0)
    acc_ref[...] += jnp.dot(vt_ref[...], aT, preferred_element_type=F32)


def peer_dense(xn, u, vt, cnt, rank2, e1, e2, res, out_gamma=None, *, tn=512, te=2048):
    D, T = xn.shape
    out_norm = out_gamma is not None
    og = (out_gamma if out_norm else jnp.zeros((D,), F32)).reshape(1, D).astype(F32)
    E = u.shape[0]
    tn = min(tn, T)
    rows = PEER_HEADS * PEER_N_KEYS
    sel_spec = pl.BlockSpec((rows, tn), lambda i, j: (0, i))
    n_tiles = E // te
    assert n_tiles % 2 == 0
    return pl.pallas_call(
        functools.partial(_peer_dense_kernel, c_per_step=te // PEER_N_KEYS, out_norm=out_norm),
        grid=(T // tn, n_tiles + 1),
        in_specs=[pl.BlockSpec((D, tn), lambda i, j: (0, i)),
                  pl.BlockSpec((te, D), lambda i, j: (jnp.minimum(j, n_tiles - 1), 0)),
                  pl.BlockSpec((D, te), lambda i, j: (0, jnp.maximum(j - 1, 0))),
                  sel_spec, sel_spec, sel_spec, sel_spec,
                  pl.BlockSpec((tn, D), lambda i, j: (i, 0)),
                  pl.BlockSpec((1, D), lambda i, j: (0, 0))],
        out_specs=pl.BlockSpec((tn, D), lambda i, j: (i, 0)),
        out_shape=jax.ShapeDtypeStruct((T, D), F32),
        scratch_shapes=[pltpu.VMEM((D, tn), F32), pltpu.VMEM((te, tn), BF16), pltpu.VMEM((te, tn), BF16)],
        compiler_params=_cparams(("parallel", "arbitrary")),
    )(xn, u, vt, cnt, rank2, e1, e2, res, og)


def peer_layer(h, gamma, w_q, sub_keys, u, v, out_gamma=None):
    q, xn = norm_matmul(h, gamma, w_q.astype(BF16), emit_xn=True)
    keys = sub_keys.reshape(2 * PEER_HEADS, PEER_N_KEYS, PEER_HALF_DIM).astype(BF16)
    cnt, rank2, e1, e2 = peer_select(q, keys)
    return peer_dense(xn, u.astype(BF16), v.T.astype(BF16), cnt, rank2, e1, e2, h, out_gamma)


def _cumsum_aug_kernel(lf_ref, tri_ref, place_q_ref, place_k_ref, ones_q_ref, ones_k_ref,
                       qa_ref, ka_ref, c_ref, carry_ref):
    @pl.when(pl.program_id(1) == 0)
    def _():
        carry_ref[...] = jnp.zeros_like(carry_ref)

    lf = lf_ref[0]
    c = jnp.dot(tri_ref[...], lf, preferred_element_type=F32, precision=lax.Precision.HIGHEST) + carry_ref[...]
    carry_ref[...] = c[-1:, :]
    c = c * LOG2E
    c_ref[0] = c
    hi = c.astype(BF16)
    r1 = c - hi.astype(F32)
    mid = r1.astype(BF16)
    lo = (r1 - mid.astype(F32)).astype(BF16)
    nh = N_HEADS
    lane = lax.broadcasted_iota(jnp.int32, c.shape, 1)
    parts = jnp.where(lane < nh, hi.astype(F32),
                      jnp.where(lane < 2 * nh, pltpu.roll(mid.astype(F32), nh, axis=1),
                                pltpu.roll(lo.astype(F32), 2 * nh, axis=1)))
    parts = jnp.where(lane < 3 * nh, parts, 0.0).astype(BF16)
    qa_ref[0] = (jnp.dot(parts, place_q_ref[...], preferred_element_type=F32) + ones_q_ref[...]).astype(BF16)
    ka_ref[0] = (jnp.dot(parts, place_k_ref[...], preferred_element_type=F32) + ones_k_ref[...]).astype(BF16)


def fox_bias_operands(logf, *, tc=256):
    B, S, _ = logf.shape
    nh = N_HEADS
    tri = jnp.asarray(np.tril(np.ones((tc, tc), np.float32)))
    pq = np.zeros((LANES, nh * LANES), np.float32)
    pk = np.zeros((LANES, nh * LANES), np.float32)
    oq = np.zeros((1, nh * LANES), np.float32)
    ok = np.zeros((1, nh * LANES), np.float32)
    for h in range(nh):
        for part in range(3):
            pq[part * nh + h, h * LANES + part] = 1.0
            pk[part * nh + h, h * LANES + 3 + part] = -1.0
            oq[0, h * LANES + 3 + part] = 1.0
            ok[0, h * LANES + part] = 1.0
    const = lambda a: pl.BlockSpec(a.shape, lambda b, i: (0,) * a.ndim)
    pq, pk, oq, ok = jnp.asarray(pq, BF16), jnp.asarray(pk, BF16), jnp.asarray(oq), jnp.asarray(ok)
    out = jax.ShapeDtypeStruct((B, S, nh * LANES), BF16)
    return pl.pallas_call(
        _cumsum_aug_kernel,
        grid=(B, S // tc),
        in_specs=[pl.BlockSpec((1, tc, LANES), lambda b, i: (b, i, 0)),
                  const(tri), const(pq), const(pk), const(oq), const(ok)],
        out_specs=[pl.BlockSpec((1, tc, nh * LANES), lambda b, i: (b, i, 0))] * 2
        + [pl.BlockSpec((1, tc, LANES), lambda b, i: (b, i, 0))],
        out_shape=[out, out, jax.ShapeDtypeStruct((B, S, LANES), F32)],
        scratch_shapes=[pltpu.VMEM((1, LANES), F32)],
        compiler_params=_cparams(("parallel", "arbitrary")),
    )(logf, tri, pq, pk, oq, ok)


def _fox_attn_kernel(first_ref, q_ref, qa_ref, kv_ref, ka_ref, o_ref, *, tq, tk, heads_per_step):
    qi = pl.program_id(2)
    t0 = qi * tq
    n_full = t0 // tk
    j_first = first_ref[(pl.program_id(0) * pl.num_programs(1) + pl.program_id(1)) * pl.num_programs(2) + qi]
    n_diag = tq // tk
    lanes = [slice(hh * LANES, (hh + 1) * LANES) for hh in range(heads_per_step)]
    qs = [jnp.concatenate([q_ref[0, :, lsl], qa_ref[0, :, lsl]], axis=1) for lsl in lanes]

    def step(j, carry, masked):
        rows = pl.ds(pl.multiple_of(j * tk, tk), tk)
        klane = lax.broadcasted_iota(jnp.int32, (tk, LANES), 1)
        new = []
        for lsl, q, (m, acc) in zip(lanes, qs, carry):
            kv = kv_ref[0, rows, lsl]
            kk = jnp.concatenate([kv, ka_ref[0, rows, lsl]], axis=1)
            ones_v = jnp.where(klane < HEAD_DIM, 1.0, kv).astype(BF16)
            s = lax.dot_general(q, kk, _NT, preferred_element_type=F32)
            if masked:
                qpos = t0 + lax.broadcasted_iota(jnp.int32, (tq, 1), 0)
                kpos = j * tk + lax.broadcasted_iota(jnp.int32, (1, tk), 1)
                s = jnp.where(kpos <= qpos, s, NEG_INF)
            m_new = jnp.maximum(m, jnp.max(s, axis=1, keepdims=True))
            p = jnp.exp2(s - m_new).astype(BF16)
            acc = jnp.exp2(m - m_new) * acc + jnp.dot(p, ones_v, preferred_element_type=F32)
            new.append((m_new, acc))
        return tuple(new)

    init = (jnp.full((tq, 1), NEG_INF, F32), jnp.zeros((tq, LANES), F32))
    n_pairs = (n_full - j_first) // 2
    carry = lax.fori_loop(
        0, n_pairs, lambda i, c: step(j_first + 2 * i + 1, step(j_first + 2 * i, c, False), False),
        (init,) * heads_per_step)
    carry = lax.fori_loop(j_first + 2 * n_pairs, n_full, functools.partial(step, masked=False), carry)
    for d in range(n_diag):
        carry = step(n_full + d, carry, True)
    outs = [acc / acc[:, 0:1] for (_, acc) in carry]
    lane = lax.broadcasted_iota(jnp.int32, (tq, LANES), 1)
    blocks = []
    for pair in range(heads_per_step // 2):
        a, b = outs[2 * pair], outs[2 * pair + 1]
        blocks.append(jnp.where(lane < HEAD_DIM, pltpu.roll(a, HEAD_DIM, axis=1), b))
    o_ref[0] = jnp.concatenate(blocks, axis=1).astype(o_ref.dtype) if len(blocks) > 1 else blocks[0].astype(o_ref.dtype)


def _head_slots(w, n_heads, second=None):
    D = w.shape[0]
    a = w.reshape(D, n_heads, HEAD_DIM)
    b = jnp.zeros_like(a) if second is None else second.reshape(D, n_heads, HEAD_DIM)
    return jnp.concatenate([a, b], axis=-1).reshape(D, n_heads * LANES)


def fox_layer(h, gamma, w_in, f_bias, w_out, B, S):
    aw = ATTN_WIDTH
    wq = _head_slots(w_in[:, :aw] * (HEAD_DIM ** -0.5 * LOG2E), N_HEADS)
    wkv = _head_slots(w_in[:, aw:2 * aw], N_HEADS, w_in[:, 2 * aw:3 * aw])
    w_main = jnp.concatenate([wq, wkv], axis=1).astype(BF16)
    wf = jnp.pad(w_in[:, 3 * aw:], ((0, 0), (0, LANES - N_HEADS))).astype(BF16)
    bf = jnp.pad(f_bias.astype(F32), (0, LANES - N_HEADS))
    qkv = norm_matmul(h, gamma, w_main)
    logf = norm_matmul(h, gamma, wf, bias=bf, act="log_sigmoid", out_dtype=F32)
    qa, ka, c2 = fox_bias_operands(logf.reshape(B, S, LANES))
    qkv = qkv.reshape(B, S, 2 * N_HEADS * LANES)
    o = fox_attention(qkv, qa, ka, c2)
    return matmul_residual(o.reshape(B * S, aw), w_out.astype(BF16), h)


FOX_NEGLIGIBLE_LOG2 = 160.0


def _norm_maxima_kernel(x_ref, ind_ref, o_ref):
    x = x_ref[0].astype(F32)
    ss = jnp.dot((x * x).astype(BF16), ind_ref[...], preferred_element_type=F32)
    o_ref[0, 0] = jnp.broadcast_to(jnp.max(ss, axis=0, keepdims=True), o_ref.shape[2:])


def fox_norm_maxima(qkv, tile):
    B, S, W = qkv.shape
    n = S // tile
    col = np.arange(W)
    slot, lane = col // LANES, col % LANES
    used = (slot < N_HEADS) | (lane < HEAD_DIM)
    ind = jnp.asarray((slot[:, None] == np.arange(LANES)[None, :]) & used[:, None], BF16)
    rows = min(256, tile)
    ss = pl.pallas_call(
        _norm_maxima_kernel,
        grid=(B, S // rows),
        in_specs=[pl.BlockSpec((1, rows, W), lambda b, i: (b, i, 0)),
                  pl.BlockSpec(ind.shape, lambda b, i: (0, 0))],
        out_specs=pl.BlockSpec((1, 1, 8, LANES), lambda b, i: (b, i, 0, 0)),
        out_shape=jax.ShapeDtypeStruct((B, S // rows, 8, LANES), F32),
        compiler_params=_cparams(("parallel", "parallel")),
    )(qkv, ind)
    ss = jnp.max(ss[:, :, 0, :2 * N_HEADS].reshape(B, n, tile // rows, 2 * N_HEADS), axis=2)
    return jnp.sqrt(ss * 1.01)


def fox_first_chunk(qkv, c2, tile, heads_per_step):
    B, S, _ = qkv.shape
    n = S // tile
    norms = fox_norm_maxima(qkv, tile)
    qmax, kmax = norms[..., :N_HEADS], norms[..., N_HEADS:]
    c = c2[..., :N_HEADS].reshape(B, n, tile, N_HEADS)
    cmax, cmin = jnp.max(c, axis=2), jnp.min(c, axis=2)
    upper = qmax[:, :, None] * kmax[:, None, :] + cmax[:, :, None] - cmin[:, None, :]
    own = -(qmax * kmax)
    earlier = jnp.arange(n)[None, :, None, None] > jnp.arange(n)[None, None, :, None]
    skip = (upper < own[:, :, None] - FOX_NEGLIGIBLE_LOG2) & earlier
    first = jnp.sum(jnp.cumprod(skip.astype(jnp.int32), axis=2), axis=2)
    first = jnp.min(first.reshape(B, n, N_HEADS // heads_per_step, heads_per_step), axis=-1)
    return jnp.transpose(first, (0, 2, 1)).reshape(-1).astype(jnp.int32)


def fox_attention(qkv, qa, ka, c2, *, tq=1024, tk=1024, heads_per_step=2):
    B, S, _ = qkv.shape
    tq = tk = min(tq, S)
    hs = heads_per_step
    wq = hs * LANES
    n_qblk = N_HEADS // hs
    first = fox_first_chunk(qkv, c2, tk, hs)
    grid_spec = pltpu.PrefetchScalarGridSpec(
        num_scalar_prefetch=1,
        grid=(B, n_qblk, S // tq),
        in_specs=[pl.BlockSpec((1, tq, wq), lambda b, h, i, first: (b, i, h)),
                  pl.BlockSpec((1, tq, wq), lambda b, h, i, first: (b, i, h)),
                  pl.BlockSpec((1, S, wq), lambda b, h, i, first: (b, 0, n_qblk + h)),
                  pl.BlockSpec((1, S, wq), lambda b, h, i, first: (b, 0, h))],
        out_specs=pl.BlockSpec((1, tq, hs * HEAD_DIM), lambda b, h, i, first: (b, i, h)))
    return pl.pallas_call(
        functools.partial(_fox_attn_kernel, tq=tq, tk=tk, heads_per_step=hs),
        grid_spec=grid_spec,
        out_shape=jax.ShapeDtypeStruct((B, S, ATTN_WIDTH), BF16),
        compiler_params=_cparams(("parallel", "parallel", "arbitrary")),
    )(first, qkv, qa, qkv, ka)


def _rot_half_cols(w):
    D = w.shape[0]
    a = w.reshape(D, -1, HEAD_DIM)
    half = HEAD_DIM // 2
    return jnp.concatenate([-a[..., half:], a[..., :half]], axis=-1).reshape(w.shape)


def _rope_tables(S):
    half = HEAD_DIM // 2
    inv_freq = ROPE_THETA ** (-jnp.arange(half, dtype=F32) / half)
    ang = jnp.arange(S, dtype=F32)[:, None] * inv_freq[None, :]
    c, s = jnp.cos(ang), jnp.sin(ang)
    c2, s2 = jnp.concatenate([c, c], axis=1), jnp.concatenate([s, s], axis=1)
    cos = jnp.stack([jnp.concatenate([c2, jnp.ones_like(c2)], axis=1), jnp.concatenate([c2, c2], axis=1)])
    sin = jnp.stack([jnp.concatenate([s2, jnp.zeros_like(s2)], axis=1), jnp.concatenate([s2, s2], axis=1)])
    return cos, sin


def _compress_kernel(x_ref, pea_ref, peb_ref, wa_ref, wb_ref, w2_ref, o_ref, pa_ref, pb0_ref, *, n_rows):
    u = pl.program_id(1)
    x = x_ref[0].astype(F32)
    pa = jnp.dot((x + pea_ref[...]).astype(BF16), wa_ref[...], preferred_element_type=F32)
    pb = jnp.dot((x + peb_ref[...]).astype(BF16), wb_ref[...], preferred_element_type=F32)

    def emit(slab, hid):
        y = jnp.dot(_gelu_tanh(hid).astype(BF16), w2_ref[...], preferred_element_type=F32)
        o_ref[0, pl.ds(pl.multiple_of(slab * n_rows, n_rows), n_rows), :] = y

    @pl.when(u == 0)
    def _():
        pb0_ref[...] = pb

    @pl.when(u > 0)
    def _():
        emit(u - 1, pa_ref[...] + pb)

    @pl.when(u == 3)
    def _():
        emit(3, pa + pltpu.roll(pb0_ref[...], n_rows - 1, axis=0))

    pa_ref[...] = pa


def nsa_compress(src, pe, w1, w2):
    B, S, W = src.shape
    G = NSA_GROUPS
    n_rows = S // 64
    half = CMP_BLOCK // 2
    cw = half * W
    xv = src.reshape(B, n_rows, 4 * cw)
    pe_flat = jnp.transpose(pe, (1, 0, 2)).reshape(CMP_BLOCK, W).astype(F32)
    pea, peb = pe_flat[:half].reshape(1, cw), pe_flat[half:].reshape(1, cw)
    eye = jnp.eye(G, dtype=F32)
    wfull = jnp.einsum('gldh,gk->lkdgh', w1.astype(F32), eye).reshape(CMP_BLOCK, W, G * CMP_HIDDEN)
    wa = wfull[:half].reshape(cw, G * CMP_HIDDEN).astype(BF16)
    wb = wfull[half:].reshape(cw, G * CMP_HIDDEN).astype(BF16)
    w2bd = jnp.einsum('ghd,gk->ghkd', w2.astype(F32), eye).reshape(G * CMP_HIDDEN, W).astype(BF16)
    const = lambda a: pl.BlockSpec(a.shape, lambda b, u: (0,) * a.ndim)
    return pl.pallas_call(
        functools.partial(_compress_kernel, n_rows=n_rows),
        grid=(B, 4),
        in_specs=[pl.BlockSpec((1, n_rows, cw), lambda b, u: (b, 0, u)),
                  const(pea), const(peb), const(wa), const(wb), const(w2bd)],
        out_specs=pl.BlockSpec((1, 4 * n_rows, W), lambda b, u: (b, 0, 0)),
        out_shape=jax.ShapeDtypeStruct((B, 4 * n_rows, W), F32),
        scratch_shapes=[pltpu.VMEM((n_rows, G * CMP_HIDDEN), F32), pltpu.VMEM((n_rows, G * CMP_HIDDEN), F32)],
        compiler_params=_cparams(("parallel", "arbitrary")),
    )(xv, pea, peb, wa, wb, w2bd)


def _nsa_attn_kernel(q_ref, kvs_ref, kvw_ref, kvc_ref, kvct_ref, gate_ref, bmat_ref, pmat_ref, o_ref,
                     *, tq, ts, tk, seq):
    R = NSA_Q_PER_GROUP
    n_slc = seq // SLC_BLOCK
    n_sel = min(SLC_TOPK, n_slc)
    assert ts & (ts - 1) == 0 and n_slc & (n_slc - 1) == 0 and tk % tq == 0 and tq % ts == 0
    qi = pl.program_id(2)
    t0 = qi * tq

    local = [_nsa_local_branches(q_ref, kvw_ref, kvc_ref, kvct_ref, t0 + i * ts, i * ts, ts, n_slc, n_sel)
             for i in range(tq // ts)]
    o_c = jnp.concatenate([o[0][r * ts:(r + 1) * ts] for r in range(R) for o in local], axis=0)
    o_w = jnp.concatenate([o[1][r * ts:(r + 1) * ts] for r in range(R) for o in local], axis=0)
    sel_bias = jnp.concatenate([o[2] for o in local], axis=0)
    qs = jnp.concatenate([q_ref[0, :, r * LANES:(r + 1) * LANES] for r in range(R)], axis=0)
    _nsa_selected_and_combine(qs, o_c, o_w, sel_bias, kvs_ref, gate_ref, bmat_ref, pmat_ref, o_ref,
                              t0=t0, tq=tq, tk=tk, seq=seq)


def _nsa_local_branches(q_ref, kvw_ref, kvc_ref, kvct_ref, t0, row0, tq, n_slc, n_sel):
    R = NSA_Q_PER_GROUP
    M = R * tq
    qs = jnp.concatenate([q_ref[0, row0:row0 + tq, r * LANES:(r + 1) * LANES] for r in range(R)], axis=0)

    wlen = WINDOW + tq
    start = jnp.maximum(t0 - WINDOW, 0)
    kvw = kvw_ref[0, pl.ds(pl.multiple_of(start, tq), wlen), :]
    s_w = lax.dot_general(qs, kvw, _NT, preferred_element_type=F32)
    qpos = t0 + (lax.broadcasted_iota(jnp.int32, (M, 1), 0) & (tq - 1))
    kpos = start + lax.broadcasted_iota(jnp.int32, (1, wlen), 1)
    s_w = jnp.where(kpos <= qpos, jnp.where(kpos > qpos - WINDOW, s_w, NEG_INF), NEG_INF)
    p_w = jnp.exp2(s_w - jnp.max(s_w, axis=1, keepdims=True)).astype(BF16)
    wlane = lax.broadcasted_iota(jnp.int32, (wlen, LANES), 1)
    acc_w = jnp.dot(p_w, jnp.where(wlane < HEAD_DIM, 1.0, kvw).astype(BF16), preferred_element_type=F32)
    o_w = acc_w / acc_w[:, 0:1]

    n_groups, gs = _nsa_row_groups(n_slc)
    o_c, sel_bias = _nsa_compressed_and_select(qs, kvc_ref, kvct_ref, t0, tq, n_slc, n_sel, gs, n_groups)
    return o_c, o_w, sel_bias


def _nsa_row_groups(n_slc):
    n_groups = max(1, min(4, n_slc // 8))
    assert n_slc % n_groups == 0 and (n_slc // n_groups) % 8 == 0
    return n_groups, n_slc // n_groups


def _nsa_compressed_and_select(qs, kvc_ref, kvct_ref, t0, tq, n_slc, n_sel, gs, n_groups):
    R = NSA_Q_PER_GROUP
    M = R * tq
    n_rows, n_j = n_groups * 4 * gs, n_groups * gs
    log_gs = gs.bit_length() - 1
    sT = lax.dot_general(kvc_ref[0, 0, 0:n_rows, :], qs, _NT, preferred_element_type=F32)
    rowc = lax.broadcasted_iota(jnp.int32, (n_rows, 1), 0)
    blk_j = ((rowc >> (log_gs + 2)) << log_gs) + (rowc & (gs - 1))
    blk_u = (rowc >> log_gs) & 3
    cmp_end = blk_j * SLC_BLOCK + blk_u * CMP_STRIDE + (CMP_BLOCK - 1)
    tcol = t0 + (lax.broadcasted_iota(jnp.int32, (1, M), 1) & (tq - 1))
    sm = jnp.where(cmp_end <= tcol, sT, NEG_INF)
    mx = jnp.max(sm, axis=0, keepdims=True)
    e = jnp.exp2(sm - mx)
    inv = jnp.where(mx > 0.5 * NEG_INF, 1.0 / jnp.sum(e, axis=0, keepdims=True), 0.0)
    pT = e * inv
    o_c = jnp.dot(kvct_ref[0, 0, :, 0:n_rows], pT.astype(BF16), preferred_element_type=F32).T

    psum = pT[:, 0:tq]
    for r in range(1, R):
        psum = psum + pT[:, r * tq:(r + 1) * tq]
    p0, p1, p2, p3 = (jnp.concatenate([psum[(4 * g + u) * gs:(4 * g + u + 1) * gs, :] for g in range(n_groups)],
                                      axis=0) for u in range(4))
    jrow = lax.broadcasted_iota(jnp.int32, (n_j, tq), 0)
    p3_prev = jnp.where(jrow == 0, 0.0, pltpu.roll(p3, 1, axis=0))
    imp = p0 + p1 + p2 + 0.5 * p3 + 0.5 * p3_prev
    cur = (t0 + lax.broadcasted_iota(jnp.int32, (n_j, tq), 1)) >> (SLC_BLOCK.bit_length() - 1)
    forced = (jrow == 0) | (jrow == cur) | (jrow == cur - 1)
    vals = jnp.where(forced, -jnp.inf, jnp.where(jrow <= cur, imp, -jnp.inf))
    sel0 = jnp.where(forced, 1.0, 0.0)
    n_free = n_sel - 3

    picked = -2.0 ** 100

    def pick(exact_ties):
        v = vals
        for _ in range(n_free):
            m = jnp.max(v, axis=0, keepdims=True)
            if exact_ties:
                hit = jrow == jnp.min(jnp.where(v == m, jrow, n_slc), axis=0, keepdims=True)
            else:
                hit = v == jnp.where(m < 0.0, jnp.nan, m)
            v = jnp.where(hit, picked, v)
        return jnp.where(v == picked, 1.0, sel0)

    sel_fast = pick(False)
    n_cand = jnp.sum(jnp.where(vals > -jnp.inf, 1.0, 0.0), axis=0, keepdims=True)
    n_picked = jnp.sum(sel_fast - sel0, axis=0, keepdims=True)
    unique = jnp.min(jnp.where(n_picked == jnp.minimum(n_cand, float(n_free)), 1.0, 0.0)) > 0.5
    sel = lax.cond(unique, lambda: sel_fast, lambda: pick(True))
    if n_j < n_slc:
        sel = jnp.concatenate([sel, jnp.zeros((n_slc - n_j, tq), F32)], axis=0)
    sel_bias = ((sel.T - 1.0) * (-NEG_INF)).astype(BF16)
    return o_c, sel_bias


def _nsa_selected_and_combine(qs, o_c, o_w, sel_bias, kvs_ref, gate_ref, bmat_ref, pmat_ref, o_ref,
                              *, t0, tq, tk, seq):
    R = NSA_Q_PER_GROUP
    n_slc = seq // SLC_BLOCK
    blocks_per_chunk = tk // SLC_BLOCK
    n_parts = 2
    hp = R // n_parts
    mp = hp * tq
    qparts = [qs[i * mp:(i + 1) * mp] for i in range(n_parts)]
    qrow = t0 + (lax.broadcasted_iota(jnp.int32, (mp, 1), 0) & (tq - 1))
    kcol = lax.broadcasted_iota(jnp.int32, (1, tk), 1)
    n_chunks = seq // tk
    j_last = t0 // tk

    klane = lax.broadcasted_iota(jnp.int32, (tk, LANES), 1)

    def slc_step(j, carry, masked):
        kv = kvs_ref[0, pl.ds(pl.multiple_of(j * tk, tk), tk), :]
        k_sel = jnp.where(klane < HEAD_DIM, kv, bmat_ref[...]).astype(BF16)
        ones_v = jnp.where(klane < HEAD_DIM, 1.0, kv).astype(BF16)
        off = pl.multiple_of(blocks_per_chunk * (n_chunks - 1 - j), blocks_per_chunk)
        place = pmat_ref[pl.ds(off, n_slc), :].astype(BF16)
        q_bias = jnp.dot(sel_bias, place, preferred_element_type=F32).astype(BF16)
        q_bias = jnp.concatenate([q_bias] * hp, axis=0)
        new = []
        for qp, (m, acc) in zip(qparts, carry):
            s = lax.dot_general(qp + q_bias, k_sel, _NT, preferred_element_type=F32)
            if masked:
                s = jnp.where(j * tk + kcol <= qrow, s, NEG_INF)
            m_new = jnp.maximum(m, jnp.max(s, axis=1, keepdims=True))
            p = jnp.exp2(s - m_new).astype(BF16)
            acc = jnp.exp2(m - m_new) * acc + jnp.dot(p, ones_v, preferred_element_type=F32)
            new.append((m_new, acc))
        return tuple(new)

    init = (jnp.full((mp, 1), NEG_INF, F32), jnp.zeros((mp, LANES), F32))
    n_pairs = j_last // 2
    carry = lax.fori_loop(0, n_pairs, lambda i, c: slc_step(2 * i + 1, slc_step(2 * i, c, False), False),
                          (init,) * n_parts)
    carry = lax.fori_loop(2 * n_pairs, j_last, functools.partial(slc_step, masked=False), carry)
    carry = slc_step(j_last, carry, True)
    o_s = jnp.concatenate([acc / acc[:, 0:1] for (_, acc) in carry], axis=0)

    gates = gate_ref[0]
    lane = lax.broadcasted_iota(jnp.int32, (tq, LANES), 1)
    comb = []
    for r in range(R):
        rs = slice(r * tq, (r + 1) * tq)
        comb.append(gates[:, r:r + 1] * o_c[rs] + gates[:, R + r:R + r + 1] * o_s[rs]
                    + gates[:, 2 * R + r:2 * R + r + 1] * o_w[rs])
    out = [jnp.where(lane < HEAD_DIM, pltpu.roll(comb[2 * i], HEAD_DIM, axis=1), comb[2 * i + 1])
           for i in range(R // 2)]
    o_ref[0] = jnp.concatenate(out, axis=1).astype(o_ref.dtype)


def nsa_attention(qkv, kvc, kvct, gates, *, tq=512, ts=256, tk=1024):
    B, S, _ = qkv.shape
    G, R = NSA_GROUPS, NSA_Q_PER_GROUP
    tk = min(tk, S)
    n_slc = S // SLC_BLOCK
    n_cmp = kvc.shape[2]
    bpc = tk // SLC_BLOCK
    assert bpc <= LANES - HEAD_DIM
    off = bpc * (S // tk - 1)
    lane = np.arange(LANES)[None, :]
    bmat = jnp.asarray(lane - HEAD_DIM == np.arange(tk)[:, None] // SLC_BLOCK, BF16)
    pmat = jnp.asarray((np.arange(n_slc + off)[:, None] - off == lane - HEAD_DIM) & (lane >= HEAD_DIM)
                       & (lane < HEAD_DIM + bpc), F32)
    slc_blk0, win_blk0 = N_HEADS, N_HEADS + G
    return pl.pallas_call(
        functools.partial(_nsa_attn_kernel, tq=tq, ts=ts, tk=tk, seq=S),
        grid=(B, G, S // tq),
        in_specs=[pl.BlockSpec((1, tq, R * LANES), lambda b, g, i: (b, i, g)),
                  pl.BlockSpec((1, S, LANES), lambda b, g, i: (b, 0, slc_blk0 + g)),
                  pl.BlockSpec((1, S, LANES), lambda b, g, i: (b, 0, win_blk0 + g)),
                  pl.BlockSpec((1, 1, n_cmp, LANES), lambda b, g, i: (b, g, 0, 0)),
                  pl.BlockSpec((1, 1, LANES, n_cmp), lambda b, g, i: (b, g, 0, 0)),
                  pl.BlockSpec((1, tq, LANES), lambda b, g, i: (b, i, g)),
                  pl.BlockSpec(bmat.shape, lambda b, g, i: (0, 0)),
                  pl.BlockSpec(pmat.shape, lambda b, g, i: (0, 0))],
        out_specs=pl.BlockSpec((1, tq, R * HEAD_DIM), lambda b, g, i: (b, i, g)),
        out_shape=jax.ShapeDtypeStruct((B, S, ATTN_WIDTH), BF16),
        compiler_params=_cparams(("parallel", "parallel", "arbitrary")),
    )(qkv, qkv, qkv, kvc, kvct, gates, bmat, pmat)


def nsa_layer(h, gamma, w_in, pe_k, w1_k, w2_k, pe_v, w1_v, w2_v, w_out, B, S):
    G, R, hd, aw = NSA_GROUPS, NSA_Q_PER_GROUP, HEAD_DIM, ATTN_WIDTH
    kvd = G * hd
    sec = lambda i: w_in[:, aw + i * kvd: aw + (i + 1) * kvd]
    wq = w_in[:, :aw] * (hd ** -0.5 * LOG2E)
    wa = jnp.concatenate([_head_slots(wq, N_HEADS), _head_slots(sec(2), G, sec(3)),
                          _head_slots(sec(4), G, sec(5))], axis=1).astype(BF16)
    wb = jnp.concatenate([_rot_half_cols(wq), _rot_half_cols(sec(2)), _rot_half_cols(sec(4))], axis=1).astype(BF16)
    cos, sin = _rope_tables(S)
    qkv = norm_matmul(h, gamma, wa, wb=wb, cos=cos, sin=sin)
    kc_src = norm_matmul(h, gamma, sec(0).astype(BF16), wb=_rot_half_cols(sec(0)).astype(BF16),
                         cos=cos, sin=sin, table_of_tile=lambda j: 1)
    vc_src = norm_matmul(h, gamma, sec(1).astype(BF16))
    wg = w_in[:, aw + 6 * kvd:].reshape(-1, 3, G, R)
    wg = jnp.transpose(wg, (0, 2, 1, 3)).reshape(-1, G, 3 * R)
    wg = jnp.pad(wg, ((0, 0), (0, 0), (0, LANES - 3 * R))).reshape(-1, G * LANES).astype(BF16)
    gates = norm_matmul(h, gamma, wg, act="sigmoid", out_dtype=F32)
    kc = nsa_compress(kc_src.reshape(B, S, kvd), pe_k, w1_k, w2_k)
    vc = nsa_compress(vc_src.reshape(B, S, kvd), pe_v, w1_v, w2_v)
    n_cmp = kc.shape[1]
    kvc = jnp.concatenate([kc.reshape(B, n_cmp, G, hd), vc.reshape(B, n_cmp, G, hd)], axis=-1)
    n_groups, gs = _nsa_row_groups(S // SLC_BLOCK)
    kvc = kvc.reshape(B, 4, n_groups, gs, G, 2 * hd)
    kvc = jnp.transpose(kvc, (0, 4, 2, 1, 3, 5)).reshape(B, G, n_cmp, 2 * hd).astype(BF16)
    kvct = jnp.swapaxes(kvc, 2, 3)
    o = nsa_attention(qkv.reshape(B, S, -1), kvc, kvct, gates.reshape(B, S, G * LANES))
    return matmul_residual(o.reshape(B * S, aw), w_out.astype(BF16), h)


def kernel(x, l0_attn_norm, l0_w_in, l0_cmp_pe_k, l0_cmp_w1_k, l0_cmp_w2_k, l0_cmp_pe_v, l0_cmp_w1_v,
           l0_cmp_w2_v, l0_w_out, l0_ffn_norm, l0_peer_wq, l0_peer_keys, l0_peer_u, l0_peer_v,
           l1_attn_norm, l1_w_in, l1_f_bias, l1_w_out, l1_ffn_norm, l1_peer_wq, l1_peer_keys, l1_peer_u,
           l1_peer_v, final_norm):
    B, S, D = x.shape
    h = x.reshape(B * S, D)
    h = nsa_layer(h, l0_attn_norm, l0_w_in, l0_cmp_pe_k, l0_cmp_w1_k, l0_cmp_w2_k, l0_cmp_pe_v, l0_cmp_w1_v,
                  l0_cmp_w2_v, l0_w_out, B, S)
    h = peer_layer(h, l0_ffn_norm, l0_peer_wq, l0_peer_keys, l0_peer_u, l0_peer_v)
    h = fox_layer(h, l1_attn_norm, l1_w_in, l1_f_bias, l1_w_out, B, S)
    h = peer_layer(h, l1_ffn_norm, l1_peer_wq, l1_peer_keys, l1_peer_u, l1_peer_v, out_gamma=final_norm)
    return h.reshape(B, S, D)
```

```python
import functools

import numpy as np
import jax
import jax.numpy as jnp
from jax import lax
from jax.experimental import pallas as pl
from jax.experimental.pallas import tpu as pltpu

F32 = jnp.float32
BF16 = jnp.bfloat16

D_MODEL = 1024
N_HEADS = 16
HEAD_DIM = 64
ATTN_WIDTH = N_HEADS * HEAD_DIM
NSA_GROUPS = 4
NSA_Q_PER_GROUP = N_HEADS // NSA_GROUPS
CMP_BLOCK = 32
CMP_STRIDE = 16
CMP_HIDDEN = 2 * HEAD_DIM
SLC_BLOCK = 64
SLC_TOPK = 16
WINDOW = 512
FORCE_SCORE = 1.0e4
ROPE_THETA = 10000.0
PEER_HEADS = 8
PEER_N_KEYS = 128
PEER_TOPK = 16
PEER_HALF_DIM = 128
RMS_EPS = 1e-6
NEG_INF = -1e30
LOG2E = 1.4426950408889634

LANES = 128
VMEM_LIMIT_BYTES = 56 * 1024 * 1024

_NT = (((1,), (1,)), ((), ()))


def _cparams(sem, vmem=VMEM_LIMIT_BYTES, flags=None):
    return pltpu.CompilerParams(dimension_semantics=sem, vmem_limit_bytes=vmem, flags=flags)


def _gelu_tanh(x):
    return 0.5 * x * (1.0 + jnp.tanh(0.7978845608028654 * (x + 0.044715 * (x * x * x))))


def _gelu_sigmoid(x):
    c = -2.0 * 0.7978845608028654 * LOG2E
    t = x * (c + (c * 0.044715) * (x * x))
    return x / (1.0 + jnp.exp2(t))


def _rms_rows(x, g):
    ms = jnp.mean(x * x, axis=-1, keepdims=True)
    return x * lax.rsqrt(ms + RMS_EPS) * g


def _norm_mm_kernel(*refs, act, has_bias, rope, emit_xn):
    it = iter(refs)
    x_ref, g_ref = next(it), next(it)
    wa_ref = next(it)
    wb_ref = next(it) if rope else None
    cos_ref = next(it) if rope else None
    sin_ref = next(it) if rope else None
    b_ref = next(it) if has_bias else None
    o_ref = next(it)
    xo_ref = next(it) if emit_xn else None
    xn_ref = next(it)

    @pl.when(pl.program_id(1) == 0)
    def _():
        xn = _rms_rows(x_ref[...], g_ref[...])
        xn_ref[...] = xn.astype(BF16)
        if emit_xn:
            xo_ref[...] = xn.T.astype(BF16)

    xn = xn_ref[...]
    y = jnp.dot(xn, wa_ref[...], preferred_element_type=F32)
    if rope:
        yb = jnp.dot(xn, wb_ref[...], preferred_element_type=F32)
        cos, sin = cos_ref[...], sin_ref[...]
        packed_b = 2 * yb.shape[1] == y.shape[1]
        for s in range(y.shape[1] // LANES):
            sl = slice(s * LANES, (s + 1) * LANES)
            if packed_b:
                b = yb[:, (s // 2) * LANES:(s // 2 + 1) * LANES]
                b = pltpu.roll(b, HEAD_DIM, axis=1) if s % 2 else b
            else:
                b = yb[:, sl]
            o_ref[:, sl] = (y[:, sl] * cos + b * sin).astype(o_ref.dtype)
        return
    if has_bias:
        y = y + b_ref[...]
    if act == "sigmoid":
        y = jax.nn.sigmoid(y)
    elif act == "log_sigmoid":
        y = jax.nn.log_sigmoid(y)
    o_ref[...] = y.astype(o_ref.dtype)


def norm_matmul(x, gamma, wa, *, wb=None, cos=None, sin=None, table_of_tile=None, bias=None,
                act=None, out_dtype=BF16, emit_xn=False, tm=1024, tn=1024):
    T, D = x.shape
    N = wa.shape[1]
    tm, tn = min(tm, T), min(tn, N)
    assert T % tm == 0 and N % tn == 0 and tn % LANES == 0
    rope = wb is not None
    in_specs = [pl.BlockSpec((tm, D), lambda i, j: (i, 0)),
                pl.BlockSpec((1, D), lambda i, j: (0, 0)),
                pl.BlockSpec((D, tn), lambda i, j: (0, j))]
    args = [x, gamma.reshape(1, D).astype(F32), wa]
    if rope:
        S = cos.shape[1]
        assert S % tm == 0
        n_pos = S // tm
        tmap = table_of_tile if table_of_tile is not None else (lambda j: 0)
        assert wb.shape[1] in (N, N // 2)
        in_specs += [pl.BlockSpec((D, tn * wb.shape[1] // N), lambda i, j: (0, j)),
                     pl.BlockSpec((None, tm, LANES), lambda i, j: (tmap(j), i % n_pos, 0)),
                     pl.BlockSpec((None, tm, LANES), lambda i, j: (tmap(j), i % n_pos, 0))]
        args += [wb, cos, sin]
    if bias is not None:
        in_specs.append(pl.BlockSpec((1, tn), lambda i, j: (0, j)))
        args.append(bias.reshape(1, N).astype(F32))
    out_shape = [jax.ShapeDtypeStruct((T, N), out_dtype)]
    out_specs = [pl.BlockSpec((tm, tn), lambda i, j: (i, j))]
    if emit_xn:
        out_shape.append(jax.ShapeDtypeStruct((D, T), BF16))
        out_specs.append(pl.BlockSpec((D, tm), lambda i, j: (0, i)))
    res = pl.pallas_call(
        functools.partial(_norm_mm_kernel, act=act, has_bias=bias is not None, rope=rope, emit_xn=emit_xn),
        grid=(T // tm, N // tn),
        in_specs=in_specs,
        out_specs=out_specs,
        out_shape=out_shape,
        scratch_shapes=[pltpu.VMEM((tm, D), BF16)],
        compiler_params=_cparams(("parallel", "arbitrary")),
    )(*args)
    return res if emit_xn else res[0]


def _mm_res_kernel(a_ref, w_ref, r_ref, o_ref):
    o_ref[...] = r_ref[...] + jnp.dot(a_ref[...], w_ref[...], preferred_element_type=F32)


def matmul_residual(a, w, res, *, tm=1024, tn=1024):
    T, K = a.shape
    N = w.shape[1]
    tm, tn = min(tm, T), min(tn, N)
    assert T % tm == 0 and N % tn == 0
    return pl.pallas_call(
        _mm_res_kernel,
        grid=(T // tm, N // tn),
        in_specs=[pl.BlockSpec((tm, K), lambda i, j: (i, 0)),
                  pl.BlockSpec((K, tn), lambda i, j: (0, j)),
                  pl.BlockSpec((tm, tn), lambda i, j: (i, j))],
        out_specs=pl.BlockSpec((tm, tn), lambda i, j: (i, j)),
        out_shape=jax.ShapeDtypeStruct((T, N), F32),
        compiler_params=_cparams(("parallel", "arbitrary")),
    )(a, w, res)


def _peer_cand_tables(tn):
    fidx, vmask = [], []
    for k2 in range(16):
        fidx.append(k2); vmask.append(0.0)
    for k1 in range(1, 8):
        lim = PEER_TOPK // (k1 + 1)
        for k2 in range(8):
            fidx.append(k1 * 16 + k2); vmask.append(0.0 if k2 < lim else -np.inf)
    for k1 in range(8, 16):
        fidx.append(k1 * 16); vmask.append(0.0)
    fidx = np.broadcast_to(np.asarray(fidx, np.int32)[:, None], (80, tn))
    vmask = np.broadcast_to(np.asarray(vmask, np.float32)[:, None], (80, tn))
    return jnp.asarray(fidx), jnp.asarray(vmask)


def _top16_rows(s, exact_ties):
    n, tn = s.shape
    rows = lax.broadcasted_iota(jnp.int32, (n, tn), 0)
    rows16 = lax.broadcasted_iota(jnp.int32, (PEER_TOPK, tn), 0)
    tops = jnp.zeros((PEER_TOPK, tn), F32)
    unit = 2.0 ** 122
    v = s
    for k in range(PEER_TOPK):
        m = jnp.max(v, axis=0, keepdims=True)
        if exact_ties:
            hit = rows == jnp.min(jnp.where(v == m, rows, n), axis=0, keepdims=True)
        else:
            hit = v == m
        v = jnp.where(hit, -(32.0 + k) * unit, v)
        tops = jnp.where(rows16 == k, m, tops)
    was_picked = v <= -32.0 * unit
    rank = jnp.where(was_picked, v * (-1.0 / unit) - 32.0, float(PEER_TOPK))
    n_picked = jnp.sum(jnp.where(was_picked, 1.0, 0.0), axis=0, keepdims=True)
    return tops, rank, n_picked


def _peer_select_head(q_ref, keys_ref, fidx, vmask, exact_ties):
    tops, ranks, es, picked = [], [], [], []
    for p in range(2):
        q = q_ref[:, p * PEER_HALF_DIM:(p + 1) * PEER_HALF_DIM]
        s = lax.dot_general(keys_ref[p], q, _NT, preferred_element_type=F32)
        t, r, n_picked = _top16_rows(s, exact_ties)
        tops.append(t); ranks.append(r)
        es.append(jnp.exp(s - t[0:1, :]))
        picked.append(n_picked)
    ts1, ts2 = tops
    pieces = [ts1[0:1, :] + ts2]
    for k1 in range(1, 8):
        pieces.append(ts1[k1:k1 + 1, :] + ts2[0:8, :])
    pieces.append(ts1[8:16, :] + ts2[0:1, :])
    cand0 = jnp.concatenate(pieces, axis=0) + vmask
    cand = cand0
    for _ in range(PEER_TOPK):
        m = jnp.max(cand, axis=0, keepdims=True)
        if exact_ties:
            hit = fidx == jnp.min(jnp.where(cand == m, fidx, 4096), axis=0, keepdims=True)
        else:
            hit = cand == m
        cand = jnp.where(hit, -jnp.inf, cand)
    taken = jnp.logical_and(cand == -jnp.inf, vmask == 0.0)
    takenf = taken.astype(F32)
    picked.append(jnp.sum(takenf, axis=0, keepdims=True))
    unique = jnp.min(jnp.where((picked[0] == PEER_TOPK) & (picked[1] == PEER_TOPK) & (picked[2] == PEER_TOPK),
                               1.0, 0.0)) > 0.5
    best = ts1[0:1, :] + ts2[0:1, :]
    z = jnp.sum(jnp.where(taken, jnp.exp(cand0 - best), 0.0), axis=0, keepdims=True)
    counts = [jnp.sum(takenf[0:16, :], axis=0, keepdims=True)]
    for k1 in range(1, 8):
        counts.append(jnp.sum(takenf[16 + 8 * (k1 - 1):16 + 8 * k1, :], axis=0, keepdims=True))
    tail = takenf[72:80, :]
    cnt = jnp.zeros_like(ranks[0])
    for k1 in range(PEER_TOPK):
        nk = counts[k1] if k1 < 8 else tail[k1 - 8:k1 - 7, :]
        cnt = jnp.where(ranks[0] == float(k1), nk, cnt)
    return (cnt, ranks[1], es[0], es[1] / z), unique


def _peer_select_kernel(q_ref, keys_ref, fidx_ref, vmask_ref, cnt_ref, rank2_ref, e1_ref, e2_ref):
    fidx = fidx_ref[...]
    vmask = vmask_ref[...]

    def store(vals):
        for ref, val in zip((cnt_ref, rank2_ref, e1_ref, e2_ref), vals):
            ref[...] = val.astype(ref.dtype)

    vals, unique = _peer_select_head(q_ref, keys_ref, fidx, vmask, exact_ties=False)
    store(vals)

    @pl.when(jnp.logical_not(unique))
    def _():
        store(_peer_select_head(q_ref, keys_ref, fidx, vmask, exact_ties=True)[0])


def peer_select(q, keys, *, tn=512):
    T = q.shape[0]
    tn = min(tn, T)
    fidx, vmask = _peer_cand_tables(tn)
    rows = PEER_HEADS * PEER_N_KEYS
    ospec = pl.BlockSpec((PEER_N_KEYS, tn), lambda h, i: (h, i))
    return pl.pallas_call(
        _peer_select_kernel,
        grid=(PEER_HEADS, T // tn),
        in_specs=[pl.BlockSpec((tn, 2 * PEER_HALF_DIM), lambda h, i: (i, h)),
                  pl.BlockSpec((2, PEER_N_KEYS, PEER_HALF_DIM), lambda h, i: (h, 0, 0)),
                  pl.BlockSpec((80, tn), lambda h, i: (0, 0)),
                  pl.BlockSpec((80, tn), lambda h, i: (0, 0))],
        out_specs=[ospec] * 4,
        out_shape=[jax.ShapeDtypeStruct((rows, T), dt) for dt in (F32, BF16, F32, BF16)],
        compiler_params=_cparams(("parallel", "parallel")),
    )(q, keys, fidx, vmask)


def _peer_dense_kernel(xn_ref, u_ref, vt_ref, cnt_ref, rank2_ref, e1_ref, e2_ref, res_ref, og_ref, o_ref,
                       acc_ref, g0_ref, g1_ref, *, c_per_step, out_norm):
    j = pl.program_id(1)
    n_tiles = pl.num_programs(1) - 1

    @pl.when(j == 0)
    def _():
        acc_ref[...] = jnp.zeros_like(acc_ref)
        g1_ref[...] = jnp.zeros_like(g1_ref)

    @pl.when((j % 2 == 0) & (j < n_tiles))
    def _():
        _peer_dense_step(xn_ref, u_ref, vt_ref, cnt_ref, rank2_ref, e1_ref, e2_ref, acc_ref,
                         g1_ref, g0_ref, j, c_per_step)

    @pl.when(j % 2 == 1)
    def _():
        _peer_dense_step(xn_ref, u_ref, vt_ref, cnt_ref, rank2_ref, e1_ref, e2_ref, acc_ref,
                         g0_ref, g1_ref, j, c_per_step)

    @pl.when(j == n_tiles)
    def _():
        _peer_dense_step(xn_ref, u_ref, vt_ref, cnt_ref, rank2_ref, e1_ref, e2_ref, acc_ref,
                         g1_ref, None, j, c_per_step)
        out = res_ref[...] + acc_ref[...].T
        o_ref[...] = _rms_rows(out, og_ref[...]) if out_norm else out


def _peer_dense_step(xn_ref, u_ref, vt_ref, cnt_ref, rank2_ref, e1_ref, e2_ref, acc_ref, g_ref, g_next_ref,
                     j, c_per_step):
    tn = xn_ref.shape[1]
    bf16_rows = 16
    reps = PEER_N_KEYS // bf16_rows

    def row_tile(ref, row):
        r16 = jnp.broadcast_to(ref[pl.ds(row, 1), :], (bf16_rows, tn)).astype(BF16)
        return jnp.concatenate([r16] * reps, axis=0)

    c0 = jnp.maximum(j - 1, 0) * c_per_step
    up_rows = 4 * PEER_N_KEYS
    blocks = []
    for cc in range(c_per_step):
        if g_next_ref is not None and (cc * PEER_N_KEYS) % up_rows == 0:
            rs = slice(cc * PEER_N_KEYS, cc * PEER_N_KEYS + up_rows)
            hT = jnp.dot(u_ref[rs, :], xn_ref[...], preferred_element_type=F32)
            g_next_ref[rs, :] = _gelu_sigmoid(hT.astype(BF16))
        c = c0 + cc
        w = None
        for h in range(PEER_HEADS):
            row = h * PEER_N_KEYS + c
            n_row = row_tile(cnt_ref, row)
            e1_row = row_tile(e1_ref, row)
            sl = slice(h * PEER_N_KEYS, (h + 1) * PEER_N_KEYS)
            term = jnp.where(rank2_ref[sl, :] < n_row, e2_ref[sl, :], 0.0) * e1_row
            w = term if w is None else w + term
        blocks.append(w * g_ref[cc * PEER_N_KEYS:(cc + 1) * PEER_N_KEYS, :])
    aT = jnp.concatenate(blocks, axis=0)
    acc_ref[...] += jnp.dot(vt_ref[...], aT, preferred_element_type=F32)


def peer_dense(xn, u, vt, cnt, rank2, e1, e2, res, out_gamma=None, *, tn=512, te=2048):
    D, T = xn.shape
    out_norm = out_gamma is not None
    og = (out_gamma if out_norm else jnp.zeros((D,), F32)).reshape(1, D).astype(F32)
    E = u.shape[0]
    tn = min(tn, T)
    rows = PEER_HEADS * PEER_N_KEYS
    sel_spec = pl.BlockSpec((rows, tn), lambda i, j: (0, i))
    n_tiles = E // te
    assert n_tiles % 2 == 0
    return pl.pallas_call(
        functools.partial(_peer_dense_kernel, c_per_step=te // PEER_N_KEYS, out_norm=out_norm),
        grid=(T // tn, n_tiles + 1),
        in_specs=[pl.BlockSpec((D, tn), lambda i, j: (0, i)),
                  pl.BlockSpec((te, D), lambda i, j: (jnp.minimum(j, n_tiles - 1), 0)),
                  pl.BlockSpec((D, te), lambda i, j: (0, jnp.maximum(j - 1, 0))),
                  sel_spec, sel_spec, sel_spec, sel_spec,
                  pl.BlockSpec((tn, D), lambda i, j: (i, 0)),
                  pl.BlockSpec((1, D), lambda i, j: (0, 0))],
        out_specs=pl.BlockSpec((tn, D), lambda i, j: (i, 0)),
        out_shape=jax.ShapeDtypeStruct((T, D), F32),
        scratch_shapes=[pltpu.VMEM((D, tn), F32), pltpu.VMEM((te, tn), BF16), pltpu.VMEM((te, tn), BF16)],
        compiler_params=_cparams(("parallel", "arbitrary")),
    )(xn, u, vt, cnt, rank2, e1, e2, res, og)


def peer_layer(h, gamma, w_q, sub_keys, u, v, out_gamma=None):
    q, xn = norm_matmul(h, gamma, w_q.astype(BF16), emit_xn=True)
    keys = sub_keys.reshape(2 * PEER_HEADS, PEER_N_KEYS, PEER_HALF_DIM).astype(BF16)
    cnt, rank2, e1, e2 = peer_select(q, keys)
    return peer_dense(xn, u.astype(BF16), v.T.astype(BF16), cnt, rank2, e1, e2, h, out_gamma)


def _cumsum_aug_kernel(lf_ref, tri_ref, place_q_ref, place_k_ref, ones_q_ref, ones_k_ref,
                       qa_ref, ka_ref, c_ref, carry_ref):
    @pl.when(pl.program_id(1) == 0)
    def _():
        carry_ref[...] = jnp.zeros_like(carry_ref)

    lf = lf_ref[0]
    c = jnp.dot(tri_ref[...], lf, preferred_element_type=F32, precision=lax.Precision.HIGHEST) + carry_ref[...]
    carry_ref[...] = c[-1:, :]
    c = c * LOG2E
    c_ref[0] = c
    hi = c.astype(BF16)
    r1 = c - hi.astype(F32)
    mid = r1.astype(BF16)
    lo = (r1 - mid.astype(F32)).astype(BF16)
    nh = N_HEADS
    lane = lax.broadcasted_iota(jnp.int32, c.shape, 1)
    parts = jnp.where(lane < nh, hi.astype(F32),
                      jnp.where(lane < 2 * nh, pltpu.roll(mid.astype(F32), nh, axis=1),
                                pltpu.roll(lo.astype(F32), 2 * nh, axis=1)))
    parts = jnp.where(lane < 3 * nh, parts, 0.0).astype(BF16)
    qa_ref[0] = (jnp.dot(parts, place_q_ref[...], preferred_element_type=F32) + ones_q_ref[...]).astype(BF16)
    ka_ref[0] = (jnp.dot(parts, place_k_ref[...], preferred_element_type=F32) + ones_k_ref[...]).astype(BF16)


def fox_bias_operands(logf, *, tc=256):
    B, S, _ = logf.shape
    nh = N_HEADS
    tri = jnp.asarray(np.tril(np.ones((tc, tc), np.float32)))
    pq = np.zeros((LANES, nh * LANES), np.float32)
    pk = np.zeros((LANES, nh * LANES), np.float32)
    oq = np.zeros((1, nh * LANES), np.float32)
    ok = np.zeros((1, nh * LANES), np.float32)
    for h in range(nh):
        for part in range(3):
            pq[part * nh + h, h * LANES + part] = 1.0
            pk[part * nh + h, h * LANES + 3 + part] = -1.0
            oq[0, h * LANES + 3 + part] = 1.0
            ok[0, h * LANES + part] = 1.0
    const = lambda a: pl.BlockSpec(a.shape, lambda b, i: (0,) * a.ndim)
    pq, pk, oq, ok = jnp.asarray(pq, BF16), jnp.asarray(pk, BF16), jnp.asarray(oq), jnp.asarray(ok)
    out = jax.ShapeDtypeStruct((B, S, nh * LANES), BF16)
    return pl.pallas_call(
        _cumsum_aug_kernel,
        grid=(B, S // tc),
        in_specs=[pl.BlockSpec((1, tc, LANES), lambda b, i: (b, i, 0)),
                  const(tri), const(pq), const(pk), const(oq), const(ok)],
        out_specs=[pl.BlockSpec((1, tc, nh * LANES), lambda b, i: (b, i, 0))] * 2
        + [pl.BlockSpec((1, tc, LANES), lambda b, i: (b, i, 0))],
        out_shape=[out, out, jax.ShapeDtypeStruct((B, S, LANES), F32)],
        scratch_shapes=[pltpu.VMEM((1, LANES), F32)],
        compiler_params=_cparams(("parallel", "arbitrary")),
    )(logf, tri, pq, pk, oq, ok)


def _fox_attn_kernel(first_ref, q_ref, qa_ref, kv_ref, ka_ref, o_ref, *, tq, tk, heads_per_step):
    qi = pl.program_id(2)
    t0 = qi * tq
    n_full = t0 // tk
    j_first = first_ref[(pl.program_id(0) * pl.num_programs(1) + pl.program_id(1)) * pl.num_programs(2) + qi]
    n_diag = tq // tk
    lanes = [slice(hh * LANES, (hh + 1) * LANES) for hh in range(heads_per_step)]
    qs = [jnp.concatenate([q_ref[0, :, lsl], qa_ref[0, :, lsl]], axis=1) for lsl in lanes]

    def step(j, carry, masked):
        rows = pl.ds(pl.multiple_of(j * tk, tk), tk)
        klane = lax.broadcasted_iota(jnp.int32, (tk, LANES), 1)
        new = []
        for lsl, q, (m, acc) in zip(lanes, qs, carry):
            kv = kv_ref[0, rows, lsl]
            kk = jnp.concatenate([kv, ka_ref[0, rows, lsl]], axis=1)
            ones_v = jnp.where(klane < HEAD_DIM, 1.0, kv).astype(BF16)
            s = lax.dot_general(q, kk, _NT, preferred_element_type=F32)
            if masked:
                qpos = t0 + lax.broadcasted_iota(jnp.int32, (tq, 1), 0)
                kpos = j * tk + lax.broadcasted_iota(jnp.int32, (1, tk), 1)
                s = jnp.where(kpos <= qpos, s, NEG_INF)
            m_new = jnp.maximum(m, jnp.max(s, axis=1, keepdims=True))
            p = jnp.exp2(s - m_new).astype(BF16)
            acc = jnp.exp2(m - m_new) * acc + jnp.dot(p, ones_v, preferred_element_type=F32)
            new.append((m_new, acc))
        return tuple(new)

    init = (jnp.full((tq, 1), NEG_INF, F32), jnp.zeros((tq, LANES), F32))
    n_pairs = (n_full - j_first) // 2
    carry = lax.fori_loop(
        0, n_pairs, lambda i, c: step(j_first + 2 * i + 1, step(j_first + 2 * i, c, False), False),
        (init,) * heads_per_step)
    carry = lax.fori_loop(j_first + 2 * n_pairs, n_full, functools.partial(step, masked=False), carry)
    for d in range(n_diag):
        carry = step(n_full + d, carry, True)
    outs = [acc / acc[:, 0:1] for (_, acc) in carry]
    lane = lax.broadcasted_iota(jnp.int32, (tq, LANES), 1)
    blocks = []
    for pair in range(heads_per_step // 2):
        a, b = outs[2 * pair], outs[2 * pair + 1]
        blocks.append(jnp.where(lane < HEAD_DIM, pltpu.roll(a, HEAD_DIM, axis=1), b))
    o_ref[0] = jnp.concatenate(blocks, axis=1).astype(o_ref.dtype) if len(blocks) > 1 else blocks[0].astype(o_ref.dtype)


def _head_slots(w, n_heads, second=None):
    D = w.shape[0]
    a = w.reshape(D, n_heads, HEAD_DIM)
    b = jnp.zeros_like(a) if second is None else second.reshape(D, n_heads, HEAD_DIM)
    return jnp.concatenate([a, b], axis=-1).reshape(D, n_heads * LANES)


def fox_layer(h, gamma, w_in, f_bias, w_out, B, S):
    aw = ATTN_WIDTH
    wq = _head_slots(w_in[:, :aw] * (HEAD_DIM ** -0.5 * LOG2E), N_HEADS)
    wkv = _head_slots(w_in[:, aw:2 * aw], N_HEADS, w_in[:, 2 * aw:3 * aw])
    w_main = jnp.concatenate([wq, wkv], axis=1).astype(BF16)
    wf = jnp.pad(w_in[:, 3 * aw:], ((0, 0), (0, LANES - N_HEADS))).astype(BF16)
    bf = jnp.pad(f_bias.astype(F32), (0, LANES - N_HEADS))
    qkv = norm_matmul(h, gamma, w_main)
    logf = norm_matmul(h, gamma, wf, bias=bf, act="log_sigmoid", out_dtype=F32)
    qa, ka, c2 = fox_bias_operands(logf.reshape(B, S, LANES))
    qkv = qkv.reshape(B, S, 2 * N_HEADS * LANES)
    o = fox_attention(qkv, qa, ka, c2)
    return matmul_residual(o.reshape(B * S, aw), w_out.astype(BF16), h)


FOX_NEGLIGIBLE_LOG2 = 160.0


def _norm_maxima_kernel(x_ref, ind_ref, o_ref):
    x = x_ref[0].astype(F32)
    ss = jnp.dot((x * x).astype(BF16), ind_ref[...], preferred_element_type=F32)
    o_ref[0, 0] = jnp.broadcast_to(jnp.max(ss, axis=0, keepdims=True), o_ref.shape[2:])


def fox_norm_maxima(qkv, tile):
    B, S, W = qkv.shape
    n = S // tile
    col = np.arange(W)
    slot, lane = col // LANES, col % LANES
    used = (slot < N_HEADS) | (lane < HEAD_DIM)
    ind = jnp.asarray((slot[:, None] == np.arange(LANES)[None, :]) & used[:, None], BF16)
    rows = min(256, tile)
    ss = pl.pallas_call(
        _norm_maxima_kernel,
        grid=(B, S // rows),
        in_specs=[pl.BlockSpec((1, rows, W), lambda b, i: (b, i, 0)),
                  pl.BlockSpec(ind.shape, lambda b, i: (0, 0))],
        out_specs=pl.BlockSpec((1, 1, 8, LANES), lambda b, i: (b, i, 0, 0)),
        out_shape=jax.ShapeDtypeStruct((B, S // rows, 8, LANES), F32),
        compiler_params=_cparams(("parallel", "parallel")),
    )(qkv, ind)
    ss = jnp.max(ss[:, :, 0, :2 * N_HEADS].reshape(B, n, tile // rows, 2 * N_HEADS), axis=2)
    return jnp.sqrt(ss * 1.01)


def fox_first_chunk(qkv, c2, tile, heads_per_step):
    B, S, _ = qkv.shape
    n = S // tile
    norms = fox_norm_maxima(qkv, tile)
    qmax, kmax = norms[..., :N_HEADS], norms[..., N_HEADS:]
    c = c2[..., :N_HEADS].reshape(B, n, tile, N_HEADS)
    cmax, cmin = jnp.max(c, axis=2), jnp.min(c, axis=2)
    upper = qmax[:, :, None] * kmax[:, None, :] + cmax[:, :, None] - cmin[:, None, :]
    own = -(qmax * kmax)
    earlier = jnp.arange(n)[None, :, None, None] > jnp.arange(n)[None, None, :, None]
    skip = (upper < own[:, :, None] - FOX_NEGLIGIBLE_LOG2) & earlier
    first = jnp.sum(jnp.cumprod(skip.astype(jnp.int32), axis=2), axis=2)
    first = jnp.min(first.reshape(B, n, N_HEADS // heads_per_step, heads_per_step), axis=-1)
    return jnp.transpose(first, (0, 2, 1)).reshape(-1).astype(jnp.int32)


def fox_attention(qkv, qa, ka, c2, *, tq=1024, tk=1024, heads_per_step=2):
    B, S, _ = qkv.shape
    tq = tk = min(tq, S)
    hs = heads_per_step
    wq = hs * LANES
    n_qblk = N_HEADS // hs
    first = fox_first_chunk(qkv, c2, tk, hs)
    grid_spec = pltpu.PrefetchScalarGridSpec(
        num_scalar_prefetch=1,
        grid=(B, n_qblk, S // tq),
        in_specs=[pl.BlockSpec((1, tq, wq), lambda b, h, i, first: (b, i, h)),
                  pl.BlockSpec((1, tq, wq), lambda b, h, i, first: (b, i, h)),
                  pl.BlockSpec((1, S, wq), lambda b, h, i, first: (b, 0, n_qblk + h)),
                  pl.BlockSpec((1, S, wq), lambda b, h, i, first: (b, 0, h))],
        out_specs=pl.BlockSpec((1, tq, hs * HEAD_DIM), lambda b, h, i, first: (b, i, h)))
    return pl.pallas_call(
        functools.partial(_fox_attn_kernel, tq=tq, tk=tk, heads_per_step=hs),
        grid_spec=grid_spec,
        out_shape=jax.ShapeDtypeStruct((B, S, ATTN_WIDTH), BF16),
        compiler_params=_cparams(("parallel", "parallel", "arbitrary")),
    )(first, qkv, qa, qkv, ka)


def _rot_half_cols(w):
    D = w.shape[0]
    a = w.reshape(D, -1, HEAD_DIM)
    half = HEAD_DIM // 2
    return jnp.concatenate([-a[..., half:], a[..., :half]], axis=-1).reshape(w.shape)


def _rope_tables(S):
    half = HEAD_DIM // 2
    inv_freq = ROPE_THETA ** (-jnp.arange(half, dtype=F32) / half)
    ang = jnp.arange(S, dtype=F32)[:, None] * inv_freq[None, :]
    c, s = jnp.cos(ang), jnp.sin(ang)
    c2, s2 = jnp.concatenate([c, c], axis=1), jnp.concatenate([s, s], axis=1)
    cos = jnp.stack([jnp.concatenate([c2, jnp.ones_like(c2)], axis=1), jnp.concatenate([c2, c2], axis=1)])
    sin = jnp.stack([jnp.concatenate([s2, jnp.zeros_like(s2)], axis=1), jnp.concatenate([s2, s2], axis=1)])
    return cos, sin


def _compress_kernel(x_ref, pea_ref, peb_ref, wa_ref, wb_ref, w2_ref, o_ref, pa_ref, pb0_ref, *, n_rows):
    u = pl.program_id(1)
    x = x_ref[0].astype(F32)
    pa = jnp.dot((x + pea_ref[...]).astype(BF16), wa_ref[...], preferred_element_type=F32)
    pb = jnp.dot((x + peb_ref[...]).astype(BF16), wb_ref[...], preferred_element_type=F32)

    def emit(slab, hid):
        y = jnp.dot(_gelu_tanh(hid).astype(BF16), w2_ref[...], preferred_element_type=F32)
        o_ref[0, pl.ds(pl.multiple_of(slab * n_rows, n_rows), n_rows), :] = y

    @pl.when(u == 0)
    def _():
        pb0_ref[...] = pb

    @pl.when(u > 0)
    def _():
        emit(u - 1, pa_ref[...] + pb)

    @pl.when(u == 3)
    def _():
        emit(3, pa + pltpu.roll(pb0_ref[...], n_rows - 1, axis=0))

    pa_ref[...] = pa


def nsa_compress(src, pe, w1, w2):
    B, S, W = src.shape
    G = NSA_GROUPS
    n_rows = S // 64
    half = CMP_BLOCK // 2
    cw = half * W
    xv = src.reshape(B, n_rows, 4 * cw)
    pe_flat = jnp.transpose(pe, (1, 0, 2)).reshape(CMP_BLOCK, W).astype(F32)
    pea, peb = pe_flat[:half].reshape(1, cw), pe_flat[half:].reshape(1, cw)
    eye = jnp.eye(G, dtype=F32)
    wfull = jnp.einsum('gldh,gk->lkdgh', w1.astype(F32), eye).reshape(CMP_BLOCK, W, G * CMP_HIDDEN)
    wa = wfull[:half].reshape(cw, G * CMP_HIDDEN).astype(BF16)
    wb = wfull[half:].reshape(cw, G * CMP_HIDDEN).astype(BF16)
    w2bd = jnp.einsum('ghd,gk->ghkd', w2.astype(F32), eye).reshape(G * CMP_HIDDEN, W).astype(BF16)
    const = lambda a: pl.BlockSpec(a.shape, lambda b, u: (0,) * a.ndim)
    return pl.pallas_call(
        functools.partial(_compress_kernel, n_rows=n_rows),
        grid=(B, 4),
        in_specs=[pl.BlockSpec((1, n_rows, cw), lambda b, u: (b, 0, u)),
                  const(pea), const(peb), const(wa), const(wb), const(w2bd)],
        out_specs=pl.BlockSpec((1, 4 * n_rows, W), lambda b, u: (b, 0, 0)),
        out_shape=jax.ShapeDtypeStruct((B, 4 * n_rows, W), F32),
        scratch_shapes=[pltpu.VMEM((n_rows, G * CMP_HIDDEN), F32), pltpu.VMEM((n_rows, G * CMP_HIDDEN), F32)],
        compiler_params=_cparams(("parallel", "arbitrary")),
    )(xv, pea, peb, wa, wb, w2bd)


def _nsa_attn_kernel(q_ref, kvs_ref, kvw_ref, kvc_ref, kvct_ref, gate_ref, bmat_ref, pmat_ref, o_ref,
                     *, tq, ts, tk, seq):
    R = NSA_Q_PER_GROUP
    n_slc = seq // SLC_BLOCK
    n_sel = min(SLC_TOPK, n_slc)
    assert ts & (ts - 1) == 0 and n_slc & (n_slc - 1) == 0 and tk % tq == 0 and tq % ts == 0
    qi = pl.program_id(2)
    t0 = qi * tq

    local = [_nsa_local_branches(q_ref, kvw_ref, kvc_ref, kvct_ref, t0 + i * ts, i * ts, ts, n_slc, n_sel)
             for i in range(tq // ts)]
    o_c = jnp.concatenate([o[0][r * ts:(r + 1) * ts] for r in range(R) for o in local], axis=0)
    o_w = jnp.concatenate([o[1][r * ts:(r + 1) * ts] for r in range(R) for o in local], axis=0)
    sel_bias = jnp.concatenate([o[2] for o in local], axis=0)
    qs = jnp.concatenate([q_ref[0, :, r * LANES:(r + 1) * LANES] for r in range(R)], axis=0)
    _nsa_selected_and_combine(qs, o_c, o_w, sel_bias, kvs_ref, gate_ref, bmat_ref, pmat_ref, o_ref,
                              t0=t0, tq=tq, tk=tk, seq=seq)


def _nsa_local_branches(q_ref, kvw_ref, kvc_ref, kvct_ref, t0, row0, tq, n_slc, n_sel):
    R = NSA_Q_PER_GROUP
    M = R * tq
    qs = jnp.concatenate([q_ref[0, row0:row0 + tq, r * LANES:(r + 1) * LANES] for r in range(R)], axis=0)

    wlen = WINDOW + tq
    start = jnp.maximum(t0 - WINDOW, 0)
    kvw = kvw_ref[0, pl.ds(pl.multiple_of(start, tq), wlen), :]
    s_w = lax.dot_general(qs, kvw, _NT, preferred_element_type=F32)
    qpos = t0 + (lax.broadcasted_iota(jnp.int32, (M, 1), 0) & (tq - 1))
    kpos = start + lax.broadcasted_iota(jnp.int32, (1, wlen), 1)
    s_w = jnp.where(kpos <= qpos, jnp.where(kpos > qpos - WINDOW, s_w, NEG_INF), NEG_INF)
    p_w = jnp.exp2(s_w - jnp.max(s_w, axis=1, keepdims=True)).astype(BF16)
    wlane = lax.broadcasted_iota(jnp.int32, (wlen, LANES), 1)
    acc_w = jnp.dot(p_w, jnp.where(wlane < HEAD_DIM, 1.0, kvw).astype(BF16), preferred_element_type=F32)
    o_w = acc_w / acc_w[:, 0:1]

    n_groups, gs = _nsa_row_groups(n_slc)
    o_c, sel_bias = _nsa_compressed_and_select(qs, kvc_ref, kvct_ref, t0, tq, n_slc, n_sel, gs, n_groups)
    return o_c, o_w, sel_bias


def _nsa_row_groups(n_slc):
    n_groups = max(1, min(4, n_slc // 8))
    assert n_slc % n_groups == 0 and (n_slc // n_groups) % 8 == 0
    return n_groups, n_slc // n_groups


def _nsa_compressed_and_select(qs, kvc_ref, kvct_ref, t0, tq, n_slc, n_sel, gs, n_groups):
    R = NSA_Q_PER_GROUP
    M = R * tq
    n_rows, n_j = n_groups * 4 * gs, n_groups * gs
    log_gs = gs.bit_length() - 1
    sT = lax.dot_general(kvc_ref[0, 0, 0:n_rows, :], qs, _NT, preferred_element_type=F32)
    rowc = lax.broadcasted_iota(jnp.int32, (n_rows, 1), 0)
    blk_j = ((rowc >> (log_gs + 2)) << log_gs) + (rowc & (gs - 1))
    blk_u = (rowc >> log_gs) & 3
    cmp_end = blk_j * SLC_BLOCK + blk_u * CMP_STRIDE + (CMP_BLOCK - 1)
    tcol = t0 + (lax.broadcasted_iota(jnp.int32, (1, M), 1) & (tq - 1))
    sm = jnp.where(cmp_end <= tcol, sT, NEG_INF)
    mx = jnp.max(sm, axis=0, keepdims=True)
    e = jnp.exp2(sm - mx)
    inv = jnp.where(mx > 0.5 * NEG_INF, 1.0 / jnp.sum(e, axis=0, keepdims=True), 0.0)
    pT = e * inv
    o_c = jnp.dot(kvct_ref[0, 0, :, 0:n_rows], pT.astype(BF16), preferred_element_type=F32).T

    psum = pT[:, 0:tq]
    for r in range(1, R):
        psum = psum + pT[:, r * tq:(r + 1) * tq]
    p0, p1, p2, p3 = (jnp.concatenate([psum[(4 * g + u) * gs:(4 * g + u + 1) * gs, :] for g in range(n_groups)],
                                      axis=0) for u in range(4))
    jrow = lax.broadcasted_iota(jnp.int32, (n_j, tq), 0)
    p3_prev = jnp.where(jrow == 0, 0.0, pltpu.roll(p3, 1, axis=0))
    imp = p0 + p1 + p2 + 0.5 * p3 + 0.5 * p3_prev
    cur = (t0 + lax.broadcasted_iota(jnp.int32, (n_j, tq), 1)) >> (SLC_BLOCK.bit_length() - 1)
    forced = (jrow == 0) | (jrow == cur) | (jrow == cur - 1)
    vals = jnp.where(forced, -jnp.inf, jnp.where(jrow <= cur, imp, -jnp.inf))
    sel0 = jnp.where(forced, 1.0, 0.0)
    n_free = n_sel - 3

    picked = -2.0 ** 100

    def pick(exact_ties):
        v = vals
        for _ in range(n_free):
            m = jnp.max(v, axis=0, keepdims=True)
            if exact_ties:
                hit = jrow == jnp.min(jnp.where(v == m, jrow, n_slc), axis=0, keepdims=True)
            else:
                hit = v == jnp.where(m < 0.0, jnp.nan, m)
            v = jnp.where(hit, picked, v)
        return jnp.where(v == picked, 1.0, sel0)

    sel_fast = pick(False)
    n_cand = jnp.sum(jnp.where(vals > -jnp.inf, 1.0, 0.0), axis=0, keepdims=True)
    n_picked = jnp.sum(sel_fast - sel0, axis=0, keepdims=True)
    unique = jnp.min(jnp.where(n_picked == jnp.minimum(n_cand, float(n_free)), 1.0, 0.0)) > 0.5
    sel = lax.cond(unique, lambda: sel_fast, lambda: pick(True))
    if n_j < n_slc:
        sel = jnp.concatenate([sel, jnp.zeros((n_slc - n_j, tq), F32)], axis=0)
    sel_bias = ((sel.T - 1.0) * (-NEG_INF)).astype(BF16)
    return o_c, sel_bias


def _nsa_selected_and_combine(qs, o_c, o_w, sel_bias, kvs_ref, gate_ref, bmat_ref, pmat_ref, o_ref,
                              *, t0, tq, tk, seq):
    R = NSA_Q_PER_GROUP
    n_slc = seq // SLC_BLOCK
    blocks_per_chunk = tk // SLC_BLOCK
    n_parts = 2
    hp = R // n_parts
    mp = hp * tq
    qparts = [qs[i * mp:(i + 1) * mp] for i in range(n_parts)]
    qrow = t0 + (lax.broadcasted_iota(jnp.int32, (mp, 1), 0) & (tq - 1))
    kcol = lax.broadcasted_iota(jnp.int32, (1, tk), 1)
    n_chunks = seq // tk
    j_last = t0 // tk

    klane = lax.broadcasted_iota(jnp.int32, (tk, LANES), 1)

    def slc_step(j, carry, masked):
        kv = kvs_ref[0, pl.ds(pl.multiple_of(j * tk, tk), tk), :]
        k_sel = jnp.where(klane < HEAD_DIM, kv, bmat_ref[...]).astype(BF16)
        ones_v = jnp.where(klane < HEAD_DIM, 1.0, kv).astype(BF16)
        off = pl.multiple_of(blocks_per_chunk * (n_chunks - 1 - j), blocks_per_chunk)
        place = pmat_ref[pl.ds(off, n_slc), :].astype(BF16)
        q_bias = jnp.dot(sel_bias, place, preferred_element_type=F32).astype(BF16)
        q_bias = jnp.concatenate([q_bias] * hp, axis=0)
        new = []
        for qp, (m, acc) in zip(qparts, carry):
            s = lax.dot_general(qp + q_bias, k_sel, _NT, preferred_element_type=F32)
            if masked:
                s = jnp.where(j * tk + kcol <= qrow, s, NEG_INF)
            m_new = jnp.maximum(m, jnp.max(s, axis=1, keepdims=True))
            p = jnp.exp2(s - m_new).astype(BF16)
            acc = jnp.exp2(m - m_new) * acc + jnp.dot(p, ones_v, preferred_element_type=F32)
            new.append((m_new, acc))
        return tuple(new)

    init = (jnp.full((mp, 1), NEG_INF, F32), jnp.zeros((mp, LANES), F32))
    n_pairs = j_last // 2
    carry = lax.fori_loop(0, n_pairs, lambda i, c: slc_step(2 * i + 1, slc_step(2 * i, c, False), False),
                          (init,) * n_parts)
    carry = lax.fori_loop(2 * n_pairs, j_last, functools.partial(slc_step, masked=False), carry)
    carry = slc_step(j_last, carry, True)
    o_s = jnp.concatenate([acc / acc[:, 0:1] for (_, acc) in carry], axis=0)

    gates = gate_ref[0]
    lane = lax.broadcasted_iota(jnp.int32, (tq, LANES), 1)
    comb = []
    for r in range(R):
        rs = slice(r * tq, (r + 1) * tq)
        comb.append(gates[:, r:r + 1] * o_c[rs] + gates[:, R + r:R + r + 1] * o_s[rs]
                    + gates[:, 2 * R + r:2 * R + r + 1] * o_w[rs])
    out = [jnp.where(lane < HEAD_DIM, pltpu.roll(comb[2 * i], HEAD_DIM, axis=1), comb[2 * i + 1])
           for i in range(R // 2)]
    o_ref[0] = jnp.concatenate(out, axis=1).astype(o_ref.dtype)


def nsa_attention(qkv, kvc, kvct, gates, *, tq=512, ts=256, tk=1024):
    B, S, _ = qkv.shape
    G, R = NSA_GROUPS, NSA_Q_PER_GROUP
    tk = min(tk, S)
    n_slc = S // SLC_BLOCK
    n_cmp = kvc.shape[2]
    bpc = tk // SLC_BLOCK
    assert bpc <= LANES - HEAD_DIM
    off = bpc * (S // tk - 1)
    lane = np.arange(LANES)[None, :]
    bmat = jnp.asarray(lane - HEAD_DIM == np.arange(tk)[:, None] // SLC_BLOCK, BF16)
    pmat = jnp.asarray((np.arange(n_slc + off)[:, None] - off == lane - HEAD_DIM) & (lane >= HEAD_DIM)
                       & (lane < HEAD_DIM + bpc), F32)
    slc_blk0, win_blk0 = N_HEADS, N_HEADS + G
    return pl.pallas_call(
        functools.partial(_nsa_attn_kernel, tq=tq, ts=ts, tk=tk, seq=S),
        grid=(B, G, S // tq),
        in_specs=[pl.BlockSpec((1, tq, R * LANES), lambda b, g, i: (b, i, g)),
                  pl.BlockSpec((1, S, LANES), lambda b, g, i: (b, 0, slc_blk0 + g)),
                  pl.BlockSpec((1, S, LANES), lambda b, g, i: (b, 0, win_blk0 + g)),
                  pl.BlockSpec((1, 1, n_cmp, LANES), lambda b, g, i: (b, g, 0, 0)),
                  pl.BlockSpec((1, 1, LANES, n_cmp), lambda b, g, i: (b, g, 0, 0)),
                  pl.BlockSpec((1, tq, LANES), lambda b, g, i: (b, i, g)),
                  pl.BlockSpec(bmat.shape, lambda b, g, i: (0, 0)),
                  pl.BlockSpec(pmat.shape, lambda b, g, i: (0, 0))],
        out_specs=pl.BlockSpec((1, tq, R * HEAD_DIM), lambda b, g, i: (b, i, g)),
        out_shape=jax.ShapeDtypeStruct((B, S, ATTN_WIDTH), BF16),
        compiler_params=_cparams(("parallel", "parallel", "arbitrary")),
    )(qkv, qkv, qkv, kvc, kvct, gates, bmat, pmat)


def nsa_layer(h, gamma, w_in, pe_k, w1_k, w2_k, pe_v, w1_v, w2_v, w_out, B, S):
    G, R, hd, aw = NSA_GROUPS, NSA_Q_PER_GROUP, HEAD_DIM, ATTN_WIDTH
    kvd = G * hd
    sec = lambda i: w_in[:, aw + i * kvd: aw + (i + 1) * kvd]
    wq = w_in[:, :aw] * (hd ** -0.5 * LOG2E)
    wa = jnp.concatenate([_head_slots(wq, N_HEADS), _head_slots(sec(2), G, sec(3)),
                          _head_slots(sec(4), G, sec(5))], axis=1).astype(BF16)
    wb = jnp.concatenate([_rot_half_cols(wq), _rot_half_cols(sec(2)), _rot_half_cols(sec(4))], axis=1).astype(BF16)
    cos, sin = _rope_tables(S)
    qkv = norm_matmul(h, gamma, wa, wb=wb, cos=cos, sin=sin)
    kc_src = norm_matmul(h, gamma, sec(0).astype(BF16), wb=_rot_half_cols(sec(0)).astype(BF16),
                         cos=cos, sin=sin, table_of_tile=lambda j: 1)
    vc_src = norm_matmul(h, gamma, sec(1).astype(BF16))
    wg = w_in[:, aw + 6 * kvd:].reshape(-1, 3, G, R)
    wg = jnp.transpose(wg, (0, 2, 1, 3)).reshape(-1, G, 3 * R)
    wg = jnp.pad(wg, ((0, 0), (0, 0), (0, LANES - 3 * R))).reshape(-1, G * LANES).astype(BF16)
    gates = norm_matmul(h, gamma, wg, act="sigmoid", out_dtype=F32)
    kc = nsa_compress(kc_src.reshape(B, S, kvd), pe_k, w1_k, w2_k)
    vc = nsa_compress(vc_src.reshape(B, S, kvd), pe_v, w1_v, w2_v)
    n_cmp = kc.shape[1]
    kvc = jnp.concatenate([kc.reshape(B, n_cmp, G, hd), vc.reshape(B, n_cmp, G, hd)], axis=-1)
    n_groups, gs = _nsa_row_groups(S // SLC_BLOCK)
    kvc = kvc.reshape(B, 4, n_groups, gs, G, 2 * hd)
    kvc = jnp.transpose(kvc, (0, 4, 2, 1, 3, 5)).reshape(B, G, n_cmp, 2 * hd).astype(BF16)
    kvct = jnp.swapaxes(kvc, 2, 3)
    o = nsa_attention(qkv.reshape(B, S, -1), kvc, kvct, gates.reshape(B, S, G * LANES))
    return matmul_residual(o.reshape(B * S, aw), w_out.astype(BF16), h)


def kernel(x, l0_attn_norm, l0_w_in, l0_cmp_pe_k, l0_cmp_w1_k, l0_cmp_w2_k, l0_cmp_pe_v, l0_cmp_w1_v,
           l0_cmp_w2_v, l0_w_out, l0_ffn_norm, l0_peer_wq, l0_peer_keys, l0_peer_u, l0_peer_v,
           l1_attn_norm, l1_w_in, l1_f_bias, l1_w_out, l1_ffn_norm, l1_peer_wq, l1_peer_keys, l1_peer_u,
           l1_peer_v, final_norm):
    B, S, D = x.shape
    h = x.reshape(B * S, D)
    h = nsa_layer(h, l0_attn_norm, l0_w_in, l0_cmp_pe_k, l0_cmp_w1_k, l0_cmp_w2_k, l0_cmp_pe_v, l0_cmp_w1_v,
                  l0_cmp_w2_v, l0_w_out, B, S)
    h = peer_layer(h, l0_ffn_norm, l0_peer_wq, l0_peer_keys, l0_peer_u, l0_peer_v)
    h = fox_layer(h, l1_attn_norm, l1_w_in, l1_f_bias, l1_w_out, B, S)
    h = peer_layer(h, l1_ffn_norm, l1_peer_wq, l1_peer_keys, l1_peer_u, l1_peer_v, out_gamma=final_norm)
    return h.reshape(B, S, D)
```
